```python
import jax, jax.numpy as jnp
from jax import lax
import numpy as np

D_MODEL = 1024
BATCH = 8
SEQ = 4096
DEPTH = 2

CHUNK = 64
Q_BLOCK = 128
D_MIX = D_MODEL
N_HEADS = 8
QK_NOPE = 64
QK_ROPE = 32
QK_DIM = QK_NOPE + QK_ROPE
V_DIM = 64
Q_LORA = 384
KV_LORA = 256
ATTN_W = N_HEADS * V_DIM
CONV_W = D_MIX - ATTN_W
CONV_K = 31
D_FF = 2816
ROPE_THETA = 10000.0
EPS = 1e-6
IN_COLS = Q_LORA + KV_LORA + QK_ROPE + 2 * CONV_W

kernel_name = "hymba_mla_conformer_conv_macaron"


def rms_norm(x, g):
    xf = x.astype(jnp.float32)
    y = xf * lax.rsqrt(jnp.mean(xf * xf, axis=-1, keepdims=True) + EPS)
    return (y * g.astype(jnp.float32)).astype(x.dtype)


def layer_norm(x, g, b):
    xf = x.astype(jnp.float32)
    mu = jnp.mean(xf, axis=-1, keepdims=True)
    xc = xf - mu
    y = xc * lax.rsqrt(jnp.mean(xc * xc, axis=-1, keepdims=True) + EPS)
    return (y * g.astype(jnp.float32) + b.astype(jnp.float32)).astype(x.dtype)


def swiglu(x, w_gate, w_up, w_down):
    return (jax.nn.silu(x @ w_gate) * (x @ w_up)) @ w_down


def rope_tables(seq_len):
    pos = jnp.arange(seq_len, dtype=jnp.float32)
    inv_freq = 1.0 / (ROPE_THETA ** (jnp.arange(0, QK_ROPE, 2, dtype=jnp.float32) / QK_ROPE))
    ang = pos[:, None] * inv_freq[None, :]
    return jnp.cos(ang), jnp.sin(ang)


def apply_rope(x, cos, sin):
    xf = x.astype(jnp.float32)
    half = QK_ROPE // 2
    x1, x2 = xf[..., :half], xf[..., half:]
    c, s = cos[None, :, None, :], sin[None, :, None, :]
    out = jnp.concatenate([x1 * c - x2 * s, x1 * s + x2 * c], axis=-1)
    return out.astype(x.dtype)


def chunk_causal_attention(q, k, v):
    b, h, s, _ = q.shape
    scale = QK_DIM ** -0.5
    outs = []
    for i in range(s // Q_BLOCK):
        q0 = i * Q_BLOCK
        k_end = q0 + Q_BLOCK
        qs = q[:, :, q0:k_end]
        ks = k[:, :, :k_end]
        vs = v[:, :, :k_end]
        scores = jnp.einsum('bhqd,bhkd->bhqk', qs, ks).astype(jnp.float32) * scale
        q_chunk = (q0 + jnp.arange(Q_BLOCK)) // CHUNK
        k_chunk = jnp.arange(k_end) // CHUNK
        allowed = k_chunk[None, :] <= q_chunk[:, None]
        scores = jnp.where(allowed[None, None], scores, -1e30)
        probs = jax.nn.softmax(scores, axis=-1).astype(vs.dtype)
        outs.append(jnp.einsum('bhqk,bhkd->bqhd', probs, vs))
    out = jnp.concatenate(outs, axis=1)
    return out.reshape(b, s, h * V_DIM)


def mla_group(c_q_raw, c_kv_raw, k_pe_raw, q_lat_g, w_uq, kv_lat_g, w_ukv, q_g, k_g, cos, sin):
    b, s, _ = c_q_raw.shape
    q = (rms_norm(c_q_raw, q_lat_g) @ w_uq).reshape(b, s, N_HEADS, QK_DIM)
    kv = (rms_norm(c_kv_raw, kv_lat_g) @ w_ukv).reshape(b, s, N_HEADS, QK_NOPE + V_DIM)
    k_nope, v = kv[..., :QK_NOPE], kv[..., QK_NOPE:]
    k_pe = jnp.broadcast_to(k_pe_raw[:, :, None, :], (b, s, N_HEADS, QK_ROPE))
    k = jnp.concatenate([k_nope, k_pe], axis=-1)
    q = rms_norm(q, q_g)
    k = rms_norm(k, k_g)
    q = jnp.concatenate([q[..., :QK_NOPE], apply_rope(q[..., QK_NOPE:], cos, sin)], axis=-1)
    k = jnp.concatenate([k[..., :QK_NOPE], apply_rope(k[..., QK_NOPE:], cos, sin)], axis=-1)
    q = jnp.transpose(q, (0, 2, 1, 3))
    k = jnp.transpose(k, (0, 2, 1, 3))
    v = jnp.transpose(v, (0, 2, 1, 3))
    return chunk_causal_attention(q, k, v)


def conv_group(u_raw, conv_w, conv_b, ln_g, ln_b):
    a, g = u_raw[..., :CONV_W], u_raw[..., CONV_W:]
    u = a * jax.nn.sigmoid(g)
    y = lax.conv_general_dilated(
        u, conv_w[:, None, :].astype(u.dtype),
        window_strides=(1,), padding=[(CONV_K - 1, 0)],
        dimension_numbers=('NWC', 'WIO', 'NWC'),
        feature_group_count=CONV_W)
    y = y + conv_b
    return jax.nn.silu(layer_norm(y, ln_g, ln_b))


def _fwd_setup_inputs(seed: int = 0) -> dict:
    key = jax.random.key(seed)
    ks = jax.random.split(key, 24)
    L, D, F = DEPTH, D_MODEL, D_FF

    def w(k, shape, fan_in):
        return jax.random.normal(k, shape, jnp.float32) * (fan_in ** -0.5)

    def gain(k, shape):
        return 1.0 + 0.02 * jax.random.normal(k, shape, jnp.float32)

    def bias(k, shape):
        return 0.02 * jax.random.normal(k, shape, jnp.float32)

    return {
        "x": jax.random.normal(ks[0], (BATCH, SEQ, D), jnp.float32),
        "ffn1_norm": gain(ks[1], (L, D)),
        "ffn1_w_gate": w(ks[2], (L, D, F), D),
        "ffn1_w_up": w(ks[3], (L, D, F), D),
        "ffn1_w_down": w(ks[4], (L, F, D), F),
        "mix_norm": gain(ks[5], (L, D)),
        "w_in": w(ks[6], (L, D, IN_COLS), D),
        "q_latent_norm": gain(ks[7], (L, Q_LORA)),
        "w_uq": w(ks[8], (L, Q_LORA, N_HEADS * QK_DIM), Q_LORA),
        "kv_latent_norm": gain(ks[9], (L, KV_LORA)),
        "w_ukv": w(ks[10], (L, KV_LORA, N_HEADS * (QK_NOPE + V_DIM)), KV_LORA),
        "q_norm": gain(ks[11], (L, QK_DIM)),
        "k_norm": gain(ks[12], (L, QK_DIM)),
        "conv_w": w(ks[13], (L, CONV_K, CONV_W), CONV_K),
        "conv_b": bias(ks[14], (L, CONV_W)),
        "conv_ln_g": gain(ks[15], (L, CONV_W)),
        "conv_ln_b": bias(ks[16], (L, CONV_W)),
        "w_out": w(ks[17], (L, D_MIX, D), D_MIX),
        "ffn2_norm": gain(ks[18], (L, D)),
        "ffn2_w_gate": w(ks[19], (L, D, F), D),
        "ffn2_w_up": w(ks[20], (L, D, F), D),
        "ffn2_w_down": w(ks[21], (L, F, D), F),
        "post_norm": gain(ks[22], (L, D)),
    }


def _fwd_reference(x, ffn1_norm, ffn1_w_gate, ffn1_w_up, ffn1_w_down, mix_norm, w_in,
              q_latent_norm, w_uq, kv_latent_norm, w_ukv, q_norm, k_norm,
              conv_w, conv_b, conv_ln_g, conv_ln_b, w_out,
              ffn2_norm, ffn2_w_gate, ffn2_w_up, ffn2_w_down, post_norm):
    cos, sin = rope_tables(x.shape[1])
    o_kv = Q_LORA
    o_pe = Q_LORA + KV_LORA
    o_cv = Q_LORA + KV_LORA + QK_ROPE
    for l in range(DEPTH):
        x = x + 0.5 * swiglu(rms_norm(x, ffn1_norm[l]), ffn1_w_gate[l], ffn1_w_up[l], ffn1_w_down[l])
        h = rms_norm(x, mix_norm[l])
        p = h @ w_in[l]
        attn = mla_group(p[..., :o_kv], p[..., o_kv:o_pe], p[..., o_pe:o_cv],
                         q_latent_norm[l], w_uq[l], kv_latent_norm[l], w_ukv[l],
                         q_norm[l], k_norm[l], cos, sin)
        conv = conv_group(p[..., o_cv:], conv_w[l], conv_b[l], conv_ln_g[l], conv_ln_b[l])
        x = x + jnp.concatenate([attn, conv], axis=-1) @ w_out[l]
        x = x + 0.5 * swiglu(rms_norm(x, ffn2_norm[l]), ffn2_w_gate[l], ffn2_w_up[l], ffn2_w_down[l])
        x = rms_norm(x, post_norm[l])
    return x


import jax as _jax
import jax.numpy as _jnp

TWIN_FORMAT = 'train_step'
FWD_PARAMS = ['x', 'ffn1_norm', 'ffn1_w_gate', 'ffn1_w_up', 'ffn1_w_down', 'mix_norm', 'w_in', 'q_latent_norm', 'w_uq', 'kv_latent_norm', 'w_ukv', 'q_norm', 'k_norm', 'conv_w', 'conv_b', 'conv_ln_g', 'conv_ln_b', 'w_out', 'ffn2_norm', 'ffn2_w_gate', 'ffn2_w_up', 'ffn2_w_down', 'post_norm']
TWIN_WEIGHTS = ['ffn1_norm', 'ffn1_w_gate', 'ffn1_w_up', 'ffn1_w_down', 'mix_norm', 'w_in', 'q_latent_norm', 'w_uq', 'kv_latent_norm', 'w_ukv', 'q_norm', 'k_norm', 'conv_w', 'conv_b', 'conv_ln_g', 'conv_ln_b', 'w_out', 'ffn2_norm', 'ffn2_w_gate', 'ffn2_w_up', 'ffn2_w_down', 'post_norm']
TWIN_DIFF_INPUT = 'x'
TWIN_INPUTS = ['x', 'ffn1_norm', 'ffn1_w_gate', 'ffn1_w_up', 'ffn1_w_down', 'mix_norm', 'w_in', 'q_latent_norm', 'w_uq', 'kv_latent_norm', 'w_ukv', 'q_norm', 'k_norm', 'conv_w', 'conv_b', 'conv_ln_g', 'conv_ln_b', 'w_out', 'ffn2_norm', 'ffn2_w_gate', 'ffn2_w_up', 'ffn2_w_down', 'post_norm', 'loss_target', 'm_ffn1_norm', 'm_ffn1_w_gate', 'm_ffn1_w_up', 'm_ffn1_w_down', 'm_mix_norm', 'm_w_in', 'm_q_latent_norm', 'm_w_uq', 'm_kv_latent_norm', 'm_w_ukv', 'm_q_norm', 'm_k_norm', 'm_conv_w', 'm_conv_b', 'm_conv_ln_g', 'm_conv_ln_b', 'm_w_out', 'm_ffn2_norm', 'm_ffn2_w_gate', 'm_ffn2_w_up', 'm_ffn2_w_down', 'm_post_norm', 'v_ffn1_norm', 'v_ffn1_w_gate', 'v_ffn1_w_up', 'v_ffn1_w_down', 'v_mix_norm', 'v_w_in', 'v_q_latent_norm', 'v_w_uq', 'v_kv_latent_norm', 'v_w_ukv', 'v_q_norm', 'v_k_norm', 'v_conv_w', 'v_conv_b', 'v_conv_ln_g', 'v_conv_ln_b', 'v_w_out', 'v_ffn2_norm', 'v_ffn2_w_gate', 'v_ffn2_w_up', 'v_ffn2_w_down', 'v_post_norm']
TWIN_OUTPUTS = ['loss', 'grad_x', 'grad_ffn1_norm', 'grad_ffn1_w_gate', 'grad_ffn1_w_up', 'grad_ffn1_w_down', 'grad_mix_norm', 'grad_w_in', 'grad_q_latent_norm', 'grad_w_uq', 'grad_kv_latent_norm', 'grad_w_ukv', 'grad_q_norm', 'grad_k_norm', 'grad_conv_w', 'grad_conv_b', 'grad_conv_ln_g', 'grad_conv_ln_b', 'grad_w_out', 'grad_ffn2_norm', 'grad_ffn2_w_gate', 'grad_ffn2_w_up', 'grad_ffn2_w_down', 'grad_post_norm', 'delta_ffn1_norm', 'delta_ffn1_w_gate', 'delta_ffn1_w_up', 'delta_ffn1_w_down', 'delta_mix_norm', 'delta_w_in', 'delta_q_latent_norm', 'delta_w_uq', 'delta_kv_latent_norm', 'delta_w_ukv', 'delta_q_norm', 'delta_k_norm', 'delta_conv_w', 'delta_conv_b', 'delta_conv_ln_g', 'delta_conv_ln_b', 'delta_w_out', 'delta_ffn2_norm', 'delta_ffn2_w_gate', 'delta_ffn2_w_up', 'delta_ffn2_w_down', 'delta_post_norm', 'new_m_ffn1_norm', 'new_m_ffn1_w_gate', 'new_m_ffn1_w_up', 'new_m_ffn1_w_down', 'new_m_mix_norm', 'new_m_w_in', 'new_m_q_latent_norm', 'new_m_w_uq', 'new_m_kv_latent_norm', 'new_m_w_ukv', 'new_m_q_norm', 'new_m_k_norm', 'new_m_conv_w', 'new_m_conv_b', 'new_m_conv_ln_g', 'new_m_conv_ln_b', 'new_m_w_out', 'new_m_ffn2_norm', 'new_m_ffn2_w_gate', 'new_m_ffn2_w_up', 'new_m_ffn2_w_down', 'new_m_post_norm', 'new_v_ffn1_norm', 'new_v_ffn1_w_gate', 'new_v_ffn1_w_up', 'new_v_ffn1_w_down', 'new_v_mix_norm', 'new_v_w_in', 'new_v_q_latent_norm', 'new_v_w_uq', 'new_v_kv_latent_norm', 'new_v_w_ukv', 'new_v_q_norm', 'new_v_k_norm', 'new_v_conv_w', 'new_v_conv_b', 'new_v_conv_ln_g', 'new_v_conv_ln_b', 'new_v_w_out', 'new_v_ffn2_norm', 'new_v_ffn2_w_gate', 'new_v_ffn2_w_up', 'new_v_ffn2_w_down', 'new_v_post_norm']
TWIN_LEAF_KINDS = {'loss': 'loss', 'grad_x': 'grad_x', 'grad_ffn1_norm': 'grad_w', 'grad_ffn1_w_gate': 'grad_w', 'grad_ffn1_w_up': 'grad_w', 'grad_ffn1_w_down': 'grad_w', 'grad_mix_norm': 'grad_w', 'grad_w_in': 'grad_w', 'grad_q_latent_norm': 'grad_w', 'grad_w_uq': 'grad_w', 'grad_kv_latent_norm': 'grad_w', 'grad_w_ukv': 'grad_w', 'grad_q_norm': 'grad_w', 'grad_k_norm': 'grad_w', 'grad_conv_w': 'grad_w', 'grad_conv_b': 'grad_w', 'grad_conv_ln_g': 'grad_w', 'grad_conv_ln_b': 'grad_w', 'grad_w_out': 'grad_w', 'grad_ffn2_norm': 'grad_w', 'grad_ffn2_w_gate': 'grad_w', 'grad_ffn2_w_up': 'grad_w', 'grad_ffn2_w_down': 'grad_w', 'grad_post_norm': 'grad_w', 'delta_ffn1_norm': 'delta_w', 'delta_ffn1_w_gate': 'delta_w', 'delta_ffn1_w_up': 'delta_w', 'delta_ffn1_w_down': 'delta_w', 'delta_mix_norm': 'delta_w', 'delta_w_in': 'delta_w', 'delta_q_latent_norm': 'delta_w', 'delta_w_uq': 'delta_w', 'delta_kv_latent_norm': 'delta_w', 'delta_w_ukv': 'delta_w', 'delta_q_norm': 'delta_w', 'delta_k_norm': 'delta_w', 'delta_conv_w': 'delta_w', 'delta_conv_b': 'delta_w', 'delta_conv_ln_g': 'delta_w', 'delta_conv_ln_b': 'delta_w', 'delta_w_out': 'delta_w', 'delta_ffn2_norm': 'delta_w', 'delta_ffn2_w_gate': 'delta_w', 'delta_ffn2_w_up': 'delta_w', 'delta_ffn2_w_down': 'delta_w', 'delta_post_norm': 'delta_w', 'new_m_ffn1_norm': 'new_m', 'new_m_ffn1_w_gate': 'new_m', 'new_m_ffn1_w_up': 'new_m', 'new_m_ffn1_w_down': 'new_m', 'new_m_mix_norm': 'new_m', 'new_m_w_in': 'new_m', 'new_m_q_latent_norm': 'new_m', 'new_m_w_uq': 'new_m', 'new_m_kv_latent_norm': 'new_m', 'new_m_w_ukv': 'new_m', 'new_m_q_norm': 'new_m', 'new_m_k_norm': 'new_m', 'new_m_conv_w': 'new_m', 'new_m_conv_b': 'new_m', 'new_m_conv_ln_g': 'new_m', 'new_m_conv_ln_b': 'new_m', 'new_m_w_out': 'new_m', 'new_m_ffn2_norm': 'new_m', 'new_m_ffn2_w_gate': 'new_m', 'new_m_ffn2_w_up': 'new_m', 'new_m_ffn2_w_down': 'new_m', 'new_m_post_norm': 'new_m', 'new_v_ffn1_norm': 'new_v', 'new_v_ffn1_w_gate': 'new_v', 'new_v_ffn1_w_up': 'new_v', 'new_v_ffn1_w_down': 'new_v', 'new_v_mix_norm': 'new_v', 'new_v_w_in': 'new_v', 'new_v_q_latent_norm': 'new_v', 'new_v_w_uq': 'new_v', 'new_v_kv_latent_norm': 'new_v', 'new_v_w_ukv': 'new_v', 'new_v_q_norm': 'new_v', 'new_v_k_norm': 'new_v', 'new_v_conv_w': 'new_v', 'new_v_conv_b': 'new_v', 'new_v_conv_ln_g': 'new_v', 'new_v_conv_ln_b': 'new_v', 'new_v_w_out': 'new_v', 'new_v_ffn2_norm': 'new_v', 'new_v_ffn2_w_gate': 'new_v', 'new_v_ffn2_w_up': 'new_v', 'new_v_ffn2_w_down': 'new_v', 'new_v_post_norm': 'new_v'}


def _forward(args):
    return _fwd_reference(*[args[k] for k in FWD_PARAMS])


def _output_shape():
    out = _jax.eval_shape(lambda: _forward(_fwd_setup_inputs(0)))
    return out.shape, out.dtype

N_MICROBATCH = 1
ADAM_LR = 0.001
ADAM_B1 = 0.9
ADAM_B2 = 0.999
ADAM_EPS = 1e-08
ADAM_WD = 0.01
ADAM_STEP = 10
PER_EXAMPLE_BATCH_AXIS = {'x': 0, 'loss_target': 0}
SHARED_INPUTS = []
_WEIGHT_DTYPES = {'ffn1_norm': _jnp.float32, 'ffn1_w_gate': _jnp.float32, 'ffn1_w_up': _jnp.float32, 'ffn1_w_down': _jnp.float32, 'mix_norm': _jnp.float32, 'w_in': _jnp.float32, 'q_latent_norm': _jnp.float32, 'w_uq': _jnp.float32, 'kv_latent_norm': _jnp.float32, 'w_ukv': _jnp.float32, 'q_norm': _jnp.float32, 'k_norm': _jnp.float32, 'conv_w': _jnp.float32, 'conv_b': _jnp.float32, 'conv_ln_g': _jnp.float32, 'conv_ln_b': _jnp.float32, 'w_out': _jnp.float32, 'ffn2_norm': _jnp.float32, 'ffn2_w_gate': _jnp.float32, 'ffn2_w_up': _jnp.float32, 'ffn2_w_down': _jnp.float32, 'post_norm': _jnp.float32}
MOMENT_SCALE = {'ffn1_norm': 7.940259e-02, 'ffn1_w_gate': 3.500540e-02, 'ffn1_w_up': 3.383302e-02, 'ffn1_w_down': 5.617686e-02, 'mix_norm': 8.664651e-02, 'w_in': 6.710509e-02, 'q_latent_norm': 2.802076e-02, 'w_uq': 2.010418e-02, 'kv_latent_norm': 5.134521e-02, 'w_ukv': 2.658762e-02, 'q_norm': 5.566362e-02, 'k_norm': 5.565160e-02, 'conv_w': 1.054160e-01, 'conv_b': 2.356467e-01, 'conv_ln_g': 1.376704e-01, 'conv_ln_b': 1.427419e-01, 'w_out': 7.673272e-02, 'ffn2_norm': 6.950500e-02, 'ffn2_w_gate': 2.953467e-02, 'ffn2_w_up': 2.871721e-02, 'ffn2_w_down': 4.754339e-02, 'post_norm': 2.257942e+01}


def _to_microbatches(a, axis):
    t = _jnp.moveaxis(a, axis, 0)
    t = t.reshape((N_MICROBATCH, t.shape[0] // N_MICROBATCH) + t.shape[1:])
    return _jnp.moveaxis(t, 1, axis + 1)


def setup_inputs(seed: int = 0) -> dict:
    inp = _fwd_setup_inputs(seed)
    key = _jax.random.fold_in(_jax.random.key(seed), 7919)
    shape, _ = _output_shape()
    out = dict(inp)
    out["loss_target"] = _jax.random.normal(_jax.random.fold_in(key, 0), shape, _jnp.float32)
    for i, name in enumerate(TWIN_WEIGHTS):
        w = inp[name].astype(_jnp.float32)
        if MOMENT_SCALE is None:
            s = _jnp.sqrt(_jnp.mean(_jnp.square(w)) + 1e-30)
        else:
            s = MOMENT_SCALE[name]
        km, kv = _jax.random.split(_jax.random.fold_in(key, i + 1))
        out[name] = w
        out["m_" + name] = s * _jax.random.normal(km, w.shape, _jnp.float32)
        out["v_" + name] = (s * s) * _jax.random.uniform(kv, w.shape, _jnp.float32, 0.5, 1.5)
    if N_MICROBATCH > 1:
        for name, axis in PER_EXAMPLE_BATCH_AXIS.items():
            out[name] = _to_microbatches(out[name], axis)
    return {'x': out['x'], 'ffn1_norm': out['ffn1_norm'], 'ffn1_w_gate': out['ffn1_w_gate'], 'ffn1_w_up': out['ffn1_w_up'], 'ffn1_w_down': out['ffn1_w_down'], 'mix_norm': out['mix_norm'], 'w_in': out['w_in'], 'q_latent_norm': out['q_latent_norm'], 'w_uq': out['w_uq'], 'kv_latent_norm': out['kv_latent_norm'], 'w_ukv': out['w_ukv'], 'q_norm': out['q_norm'], 'k_norm': out['k_norm'], 'conv_w': out['conv_w'], 'conv_b': out['conv_b'], 'conv_ln_g': out['conv_ln_g'], 'conv_ln_b': out['conv_ln_b'], 'w_out': out['w_out'], 'ffn2_norm': out['ffn2_norm'], 'ffn2_w_gate': out['ffn2_w_gate'], 'ffn2_w_up': out['ffn2_w_up'], 'ffn2_w_down': out['ffn2_w_down'], 'post_norm': out['post_norm'], 'loss_target': out['loss_target'], 'm_ffn1_norm': out['m_ffn1_norm'], 'm_ffn1_w_gate': out['m_ffn1_w_gate'], 'm_ffn1_w_up': out['m_ffn1_w_up'], 'm_ffn1_w_down': out['m_ffn1_w_down'], 'm_mix_norm': out['m_mix_norm'], 'm_w_in': out['m_w_in'], 'm_q_latent_norm': out['m_q_latent_norm'], 'm_w_uq': out['m_w_uq'], 'm_kv_latent_norm': out['m_kv_latent_norm'], 'm_w_ukv': out['m_w_ukv'], 'm_q_norm': out['m_q_norm'], 'm_k_norm': out['m_k_norm'], 'm_conv_w': out['m_conv_w'], 'm_conv_b': out['m_conv_b'], 'm_conv_ln_g': out['m_conv_ln_g'], 'm_conv_ln_b': out['m_conv_ln_b'], 'm_w_out': out['m_w_out'], 'm_ffn2_norm': out['m_ffn2_norm'], 'm_ffn2_w_gate': out['m_ffn2_w_gate'], 'm_ffn2_w_up': out['m_ffn2_w_up'], 'm_ffn2_w_down': out['m_ffn2_w_down'], 'm_post_norm': out['m_post_norm'], 'v_ffn1_norm': out['v_ffn1_norm'], 'v_ffn1_w_gate': out['v_ffn1_w_gate'], 'v_ffn1_w_up': out['v_ffn1_w_up'], 'v_ffn1_w_down': out['v_ffn1_w_down'], 'v_mix_norm': out['v_mix_norm'], 'v_w_in': out['v_w_in'], 'v_q_latent_norm': out['v_q_latent_norm'], 'v_w_uq': out['v_w_uq'], 'v_kv_latent_norm': out['v_kv_latent_norm'], 'v_w_ukv': out['v_w_ukv'], 'v_q_norm': out['v_q_norm'], 'v_k_norm': out['v_k_norm'], 'v_conv_w': out['v_conv_w'], 'v_conv_b': out['v_conv_b'], 'v_conv_ln_g': out['v_conv_ln_g'], 'v_conv_ln_b': out['v_conv_ln_b'], 'v_w_out': out['v_w_out'], 'v_ffn2_norm': out['v_ffn2_norm'], 'v_ffn2_w_gate': out['v_ffn2_w_gate'], 'v_ffn2_w_up': out['v_ffn2_w_up'], 'v_ffn2_w_down': out['v_ffn2_w_down'], 'v_post_norm': out['v_post_norm']}


def _loss(weights, diff, rest, loss_target):
    with _jax.named_scope("forward"):
        args = {**rest, TWIN_DIFF_INPUT: diff, **{k: w.astype(_WEIGHT_DTYPES[k]) for k, w in weights.items()}}
        y = _forward(args)
    with _jax.named_scope("loss_head"):
        err = _jnp.square(y.astype(_jnp.float32) - loss_target)
        return 0.5 * _jnp.sum(_jnp.mean(err, axis=-1)) if err.ndim else 0.5 * err


def _adamw(w, g, m, v):
    m = ADAM_B1 * m + (1.0 - ADAM_B1) * g
    v = ADAM_B2 * v + (1.0 - ADAM_B2) * _jnp.square(g)
    m_hat = m / (1.0 - ADAM_B1 ** ADAM_STEP)
    v_hat = v / (1.0 - ADAM_B2 ** ADAM_STEP)
    delta = -ADAM_LR * (m_hat / (_jnp.sqrt(v_hat) + ADAM_EPS) + ADAM_WD * w)
    return delta, m, v


def reference(x, ffn1_norm, ffn1_w_gate, ffn1_w_up, ffn1_w_down, mix_norm, w_in, q_latent_norm, w_uq, kv_latent_norm, w_ukv, q_norm, k_norm, conv_w, conv_b, conv_ln_g, conv_ln_b, w_out, ffn2_norm, ffn2_w_gate, ffn2_w_up, ffn2_w_down, post_norm, loss_target, m_ffn1_norm, m_ffn1_w_gate, m_ffn1_w_up, m_ffn1_w_down, m_mix_norm, m_w_in, m_q_latent_norm, m_w_uq, m_kv_latent_norm, m_w_ukv, m_q_norm, m_k_norm, m_conv_w, m_conv_b, m_conv_ln_g, m_conv_ln_b, m_w_out, m_ffn2_norm, m_ffn2_w_gate, m_ffn2_w_up, m_ffn2_w_down, m_post_norm, v_ffn1_norm, v_ffn1_w_gate, v_ffn1_w_up, v_ffn1_w_down, v_mix_norm, v_w_in, v_q_latent_norm, v_w_uq, v_kv_latent_norm, v_w_ukv, v_q_norm, v_k_norm, v_conv_w, v_conv_b, v_conv_ln_g, v_conv_ln_b, v_w_out, v_ffn2_norm, v_ffn2_w_gate, v_ffn2_w_up, v_ffn2_w_down, v_post_norm):
    given = dict(x=x, ffn1_norm=ffn1_norm, ffn1_w_gate=ffn1_w_gate, ffn1_w_up=ffn1_w_up, ffn1_w_down=ffn1_w_down, mix_norm=mix_norm, w_in=w_in, q_latent_norm=q_latent_norm, w_uq=w_uq, kv_latent_norm=kv_latent_norm, w_ukv=w_ukv, q_norm=q_norm, k_norm=k_norm, conv_w=conv_w, conv_b=conv_b, conv_ln_g=conv_ln_g, conv_ln_b=conv_ln_b, w_out=w_out, ffn2_norm=ffn2_norm, ffn2_w_gate=ffn2_w_gate, ffn2_w_up=ffn2_w_up, ffn2_w_down=ffn2_w_down, post_norm=post_norm, loss_target=loss_target, m_ffn1_norm=m_ffn1_norm, m_ffn1_w_gate=m_ffn1_w_gate, m_ffn1_w_up=m_ffn1_w_up, m_ffn1_w_down=m_ffn1_w_down, m_mix_norm=m_mix_norm, m_w_in=m_w_in, m_q_latent_norm=m_q_latent_norm, m_w_uq=m_w_uq, m_kv_latent_norm=m_kv_latent_norm, m_w_ukv=m_w_ukv, m_q_norm=m_q_norm, m_k_norm=m_k_norm, m_conv_w=m_conv_w, m_conv_b=m_conv_b, m_conv_ln_g=m_conv_ln_g, m_conv_ln_b=m_conv_ln_b, m_w_out=m_w_out, m_ffn2_norm=m_ffn2_norm, m_ffn2_w_gate=m_ffn2_w_gate, m_ffn2_w_up=m_ffn2_w_up, m_ffn2_w_down=m_ffn2_w_down, m_post_norm=m_post_norm, v_ffn1_norm=v_ffn1_norm, v_ffn1_w_gate=v_ffn1_w_gate, v_ffn1_w_up=v_ffn1_w_up, v_ffn1_w_down=v_ffn1_w_down, v_mix_norm=v_mix_norm, v_w_in=v_w_in, v_q_latent_norm=v_q_latent_norm, v_w_uq=v_w_uq, v_kv_latent_norm=v_kv_latent_norm, v_w_ukv=v_w_ukv, v_q_norm=v_q_norm, v_k_norm=v_k_norm, v_conv_w=v_conv_w, v_conv_b=v_conv_b, v_conv_ln_g=v_conv_ln_g, v_conv_ln_b=v_conv_ln_b, v_w_out=v_w_out, v_ffn2_norm=v_ffn2_norm, v_ffn2_w_gate=v_ffn2_w_gate, v_ffn2_w_up=v_ffn2_w_up, v_ffn2_w_down=v_ffn2_w_down, v_post_norm=v_post_norm)
    weights = {n: given[n] for n in TWIN_WEIGHTS}
    shared = {n: given[n] for n in SHARED_INPUTS}
    per_example = {n: given[n] for n in ['x']}
    grad_fn = _jax.value_and_grad(_loss, argnums=(0, 1))

    def one_microbatch(ex, loss_target):
        ex = dict(ex)
        diff = ex.pop(TWIN_DIFF_INPUT)
        return grad_fn(weights, diff, {**shared, **ex}, loss_target)

    if N_MICROBATCH == 1:
        loss, (grad_w, grad_x) = one_microbatch(per_example, given["loss_target"])
    else:
        def body(carry, xs):
            loss_sum, grad_sum = carry
            l_k, (gw_k, gx_k) = one_microbatch(xs[0], xs[1])
            with _jax.named_scope("update"):
                return (loss_sum + l_k, _jax.tree.map(_jnp.add, grad_sum, gw_k)), gx_k

        init = (_jnp.zeros((), _jnp.float32), _jax.tree.map(_jnp.zeros_like, weights))
        (loss, grad_w), grad_x = _jax.lax.scan(body, init, (per_example, given["loss_target"]))
    with _jax.named_scope("update"):
        delta_w, new_m, new_v = {}, {}, {}
        for n in TWIN_WEIGHTS:
            delta_w[n], new_m[n], new_v[n] = _adamw(weights[n], grad_w[n], given["m_" + n], given["v_" + n])
    return (loss, grad_x, *[grad_w[n] for n in TWIN_WEIGHTS], *[delta_w[n] for n in TWIN_WEIGHTS],
            *[new_m[n] for n in TWIN_WEIGHTS], *[new_v[n] for n in TWIN_WEIGHTS])
```

```python
import functools

import jax
import jax.numpy as jnp
from jax import lax
from jax.experimental import pallas as pl
from jax.experimental.pallas import tpu as pltpu

F32, BF16 = jnp.float32, jnp.bfloat16

N_DEV = 8
N_HEADS = 8
QK_NOPE, QK_ROPE, V_DIM = 64, 32, 64
QK_DIM = QK_NOPE + QK_ROPE
HEAD_PAD = 128
Q_LORA, KV_LORA = 384, 256
LAT = Q_LORA + KV_LORA
CONV_W, CONV_K = 512, 31
CONV_HALO = 32
CHUNK = 64
ROPE_THETA = 10000.0
EPS = 1e-6
ATTN_SCALE = QK_DIM ** -0.5
P_KPE = LAT
P_A = LAT + HEAD_PAD
P_G = P_A + CONV_W
P_COLS = P_G + CONV_W

ADAM_LR, ADAM_B1, ADAM_B2, ADAM_EPS, ADAM_WD, ADAM_STEP = 0.001, 0.9, 0.999, 1e-08, 0.01, 10

V7X_VMEM_BYTES = 64 << 20
VMEM_LIMIT = V7X_VMEM_BYTES - (8 << 20)
LANE = 128
ROW_TILE = 512
ATTN_BLOCK = 256
CONV_TILE = 256

MESH_ID = pl.DeviceIdType.MESH
ANY = pl.BlockSpec(memory_space=pl.ANY)
VMEM_SPEC = pl.BlockSpec(memory_space=pltpu.VMEM)

WEIGHTS = ['ffn1_norm', 'ffn1_w_gate', 'ffn1_w_up', 'ffn1_w_down', 'mix_norm', 'w_in', 'q_latent_norm', 'w_uq',
           'kv_latent_norm', 'w_ukv', 'q_norm', 'k_norm', 'conv_w', 'conv_b', 'conv_ln_g', 'conv_ln_b', 'w_out',
           'ffn2_norm', 'ffn2_w_gate', 'ffn2_w_up', 'ffn2_w_down', 'post_norm']
COL_SHARDED = ['ffn1_w_gate', 'ffn1_w_up', 'w_in', 'w_uq', 'w_ukv', 'ffn2_w_gate', 'ffn2_w_up']
ROW_SHARDED = ['ffn1_w_down', 'w_out', 'ffn2_w_down']
BIG = ['ffn1_w_gate', 'ffn1_w_up', 'ffn1_w_down', 'w_in', 'w_uq', 'w_ukv', 'w_out', 'ffn2_w_gate', 'ffn2_w_up',
       'ffn2_w_down']
VECTORS = ['ffn1_norm', 'mix_norm', 'q_latent_norm', 'kv_latent_norm', 'q_norm', 'k_norm', 'conv_b', 'conv_ln_g',
           'conv_ln_b', 'ffn2_norm', 'post_norm']


def _params(*sem):
    return pltpu.CompilerParams(dimension_semantics=sem if sem else None, vmem_limit_bytes=VMEM_LIMIT)


def _tile(n, cap):
    if n <= cap:
        return n
    best = 0
    for d in range(LANE, cap + 1, LANE):
        if n % d == 0:
            best = d
    assert best, (n, cap)
    return best


def _row_tile(n, cap=ROW_TILE):
    if n <= cap:
        return n
    best = 0
    for d in range(8, cap + 1, 8):
        if n % d == 0:
            best = d
    assert best, (n, cap)
    return best


def _mm(a, b, *, name, ta=False, tb=False, res=None, scale=1.0, out_dtype=F32):
    (kdim, m) = a.shape if ta else a.shape[::-1]
    (n, kb) = b.shape if tb else b.shape[::-1]
    assert kdim == kb, (a.shape, b.shape, ta, tb)
    tm, tn, tk = _tile(m, 512), _tile(n, 1408), _tile(kdim, 1408)
    nk = kdim // tk
    dims = (((0 if ta else 1,), (1 if tb else 0,)), ((), ()))

    def body(*refs):
        a_ref, b_ref = refs[0], refs[1]
        r_ref = refs[2] if res is not None else None
        o_ref = refs[3] if res is not None else refs[2]
        acc_ref = refs[-1] if nk > 1 else None
        part = lax.dot_general(a_ref[...].astype(BF16), b_ref[...].astype(BF16), dims, preferred_element_type=F32)

        def finish(acc):
            if scale != 1.0:
                acc = acc * scale
            if r_ref is not None:
                acc = r_ref[...] + acc
            o_ref[...] = acc.astype(o_ref.dtype)

        if nk == 1:
            finish(part)
        else:
            k = pl.program_id(2)

            @pl.when(k == 0)
            def _():
                acc_ref[...] = part

            @pl.when(k > 0)
            def _():
                acc_ref[...] += part

            @pl.when(k == nk - 1)
            def _():
                finish(acc_ref[...])

    a_spec = pl.BlockSpec((tk, tm), lambda i, j, k: (k, i)) if ta else pl.BlockSpec((tm, tk), lambda i, j, k: (i, k))
    b_spec = pl.BlockSpec((tn, tk), lambda i, j, k: (j, k)) if tb else pl.BlockSpec((tk, tn), lambda i, j, k: (k, j))
    o_spec = pl.BlockSpec((tm, tn), lambda i, j, k: (i, j))
    in_specs = [a_spec, b_spec] + ([o_spec] if res is not None else [])
    args = (a, b) + ((res,) if res is not None else ())
    return pl.pallas_call(
        body, name=name, grid=(m // tm, n // tn, nk), in_specs=in_specs, out_specs=o_spec,
        out_shape=jax.ShapeDtypeStruct((m, n), out_dtype),
        scratch_shapes=[pltpu.VMEM((tm, tn), F32)] if nk > 1 else [],
        compiler_params=_params("parallel", "parallel", "arbitrary"),
    )(*args)


def _rms_fwd(x, g, out_dtype, name):
    t, d = x.shape
    tm = _row_tile(t)

    def body(x_ref, g_ref, o_ref):
        xv = x_ref[...]
        r = lax.rsqrt(jnp.mean(xv * xv, axis=-1, keepdims=True) + EPS)
        o_ref[...] = (xv * r * g_ref[...]).astype(o_ref.dtype)

    return pl.pallas_call(
        body, name=name, grid=(t // tm,),
        in_specs=[pl.BlockSpec((tm, d), lambda i: (i, 0)), pl.BlockSpec((1, d), lambda i: (0, 0))],
        out_specs=pl.BlockSpec((tm, d), lambda i: (i, 0)),
        out_shape=jax.ShapeDtypeStruct((t, d), out_dtype), compiler_params=_params("parallel"),
    )(x, g.reshape(1, d))


def _rms_bwd(x, g, dh, res, name):
    t, d = x.shape
    tm = _row_tile(t)

    def body(*refs):
        x_ref, g_ref, dh_ref = refs[:3]
        r_ref = refs[3] if res is not None else None
        dx_ref, dg_ref = refs[-2:]
        xv, dhv = x_ref[...], dh_ref[...]
        r = lax.rsqrt(jnp.mean(xv * xv, axis=-1, keepdims=True) + EPS)
        y = xv * r
        dy = dhv * g_ref[...]
        dx = r * (dy - y * jnp.mean(dy * y, axis=-1, keepdims=True))
        if r_ref is not None:
            dx = r_ref[...] + dx
        dx_ref[...] = dx

        @pl.when(pl.program_id(0) == 0)
        def _():
            dg_ref[...] = jnp.zeros_like(dg_ref)

        dg_ref[...] += jnp.sum(dhv * y, axis=0, keepdims=True)

    row = pl.BlockSpec((tm, d), lambda i: (i, 0))
    vec = pl.BlockSpec((1, d), lambda i: (0, 0))
    args = (x, g.reshape(1, d), dh) + ((res,) if res is not None else ())
    dx, dg = pl.pallas_call(
        body, name=name, grid=(t // tm,), in_specs=[row, vec, row] + ([row] if res is not None else []),
        out_specs=(row, vec), out_shape=(jax.ShapeDtypeStruct((t, d), F32), jax.ShapeDtypeStruct((1, d), F32)),
        compiler_params=_params("arbitrary"),
    )(*args)
    return dx, dg.reshape(d)


def _swiglu_fwd(ab, name):
    t, f2 = ab.shape
    f = f2 // 2
    tm, tf = _row_tile(t), _tile(f, 1408)
    nf = f // tf

    def body(a_ref, b_ref, z_ref):
        av = a_ref[...]
        z_ref[...] = (av * jax.nn.sigmoid(av) * b_ref[...]).astype(z_ref.dtype)

    return pl.pallas_call(
        body, name=name, grid=(t // tm, nf),
        in_specs=[pl.BlockSpec((tm, tf), lambda i, j: (i, j)), pl.BlockSpec((tm, tf), lambda i, j: (i, j + nf))],
        out_specs=pl.BlockSpec((tm, tf), lambda i, j: (i, j)),
        out_shape=jax.ShapeDtypeStruct((t, f), BF16), compiler_params=_params("parallel", "parallel"),
    )(ab, ab)


def _swiglu_bwd(ab, dz, name):
    t, f2 = ab.shape
    f = f2 // 2
    tm, tf = _row_tile(t), _tile(f, 1408)
    nf = f // tf

    def body(a_ref, b_ref, dz_ref, o_ref):
        j = pl.program_id(1)
        av, dzv = a_ref[...], dz_ref[...]
        s = jax.nn.sigmoid(av)

        @pl.when(j < nf)
        def _():
            o_ref[...] = (dzv * b_ref[...] * (s * (1.0 + av * (1.0 - s)))).astype(o_ref.dtype)

        @pl.when(j >= nf)
        def _():
            o_ref[...] = (dzv * (av * s)).astype(o_ref.dtype)

    return pl.pallas_call(
        body, name=name, grid=(t // tm, 2 * nf),
        in_specs=[pl.BlockSpec((tm, tf), lambda i, j: (i, j % nf)),
                  pl.BlockSpec((tm, tf), lambda i, j: (i, j % nf + nf)),
                  pl.BlockSpec((tm, tf), lambda i, j: (i, j % nf))],
        out_specs=pl.BlockSpec((tm, tf), lambda i, j: (i, j)),
        out_shape=jax.ShapeDtypeStruct((t, f2), BF16), compiler_params=_params("parallel", "parallel"),
    )(ab, ab, dz)


def _lat_norm_fwd(p, g_q, g_kv, name):
    t = p.shape[0]
    tm = _row_tile(t)

    def body(p_ref, gq_ref, gkv_ref, q_ref, kv_ref):
        for lo, hi, g_ref, o_ref in ((0, Q_LORA, gq_ref, q_ref), (Q_LORA, LAT, gkv_ref, kv_ref)):
            xv = p_ref[:, lo:hi]
            r = lax.rsqrt(jnp.mean(xv * xv, axis=-1, keepdims=True) + EPS)
            o_ref[...] = (xv * r * g_ref[...]).astype(o_ref.dtype)

    return pl.pallas_call(
        body, name=name, grid=(t // tm,),
        in_specs=[pl.BlockSpec((tm, P_COLS), lambda i: (i, 0)), pl.BlockSpec((1, Q_LORA), lambda i: (0, 0)),
                  pl.BlockSpec((1, KV_LORA), lambda i: (0, 0))],
        out_specs=(pl.BlockSpec((tm, Q_LORA), lambda i: (i, 0)), pl.BlockSpec((tm, KV_LORA), lambda i: (i, 0))),
        out_shape=(jax.ShapeDtypeStruct((t, Q_LORA), BF16), jax.ShapeDtypeStruct((t, KV_LORA), BF16)),
        compiler_params=_params("parallel"),
    )(p, g_q.reshape(1, Q_LORA), g_kv.reshape(1, KV_LORA))


def _lat_norm_bwd(p, g_q, g_kv, dq, dkv, name):
    t = p.shape[0]
    tm = _row_tile(t)

    def body(p_ref, gq_ref, gkv_ref, dq_ref, dkv_ref, dp_ref, dgq_ref, dgkv_ref):
        first = pl.program_id(0) == 0
        for lo, hi, g_ref, d_ref, dg_ref in ((0, Q_LORA, gq_ref, dq_ref, dgq_ref),
                                             (Q_LORA, LAT, gkv_ref, dkv_ref, dgkv_ref)):
            xv, dhv = p_ref[:, lo:hi], d_ref[...]
            r = lax.rsqrt(jnp.mean(xv * xv, axis=-1, keepdims=True) + EPS)
            y = xv * r
            dy = dhv * g_ref[...]
            dp_ref[:, lo:hi] = r * (dy - y * jnp.mean(dy * y, axis=-1, keepdims=True))

            @pl.when(first)
            def _():
                dg_ref[...] = jnp.zeros_like(dg_ref)

            dg_ref[...] += jnp.sum(dhv * y, axis=0, keepdims=True)

    vq = pl.BlockSpec((1, Q_LORA), lambda i: (0, 0))
    vkv = pl.BlockSpec((1, KV_LORA), lambda i: (0, 0))
    dp, dgq, dgkv = pl.pallas_call(
        body, name=name, grid=(t // tm,),
        in_specs=[pl.BlockSpec((tm, P_COLS), lambda i: (i, 0)), vq, vkv,
                  pl.BlockSpec((tm, Q_LORA), lambda i: (i, 0)), pl.BlockSpec((tm, KV_LORA), lambda i: (i, 0))],
        out_specs=(pl.BlockSpec((tm, LAT), lambda i: (i, 0)), vq, vkv),
        out_shape=(jax.ShapeDtypeStruct((t, LAT), F32), jax.ShapeDtypeStruct((1, Q_LORA), F32),
                   jax.ShapeDtypeStruct((1, KV_LORA), F32)),
        compiler_params=_params("arbitrary"),
    )(p, g_q.reshape(1, Q_LORA), g_kv.reshape(1, KV_LORA), dq, dkv)
    return dp, dgq.reshape(Q_LORA), dgkv.reshape(KV_LORA)


def _rope_tables(t):
    half = QK_ROPE // 2
    pos = jnp.arange(t, dtype=F32)
    inv_freq = 1.0 / (ROPE_THETA ** (jnp.arange(0, QK_ROPE, 2, dtype=F32) / QK_ROPE))
    ang = pos[:, None] * inv_freq[None, :]
    cos, sin = jnp.cos(ang), jnp.sin(ang)
    z = lambda n: jnp.zeros((t, n), F32)
    c_tab = jnp.concatenate([jnp.ones((t, QK_NOPE), F32), cos, cos, z(HEAD_PAD - QK_DIM)], axis=1)
    sa_tab = jnp.concatenate([z(QK_NOPE), -sin, z(half), z(HEAD_PAD - QK_DIM)], axis=1)
    sb_tab = jnp.concatenate([z(QK_NOPE), z(half), sin, z(HEAD_PAD - QK_DIM)], axis=1)
    return c_tab, sa_tab, sb_tab


def _rope(x, c, sa, sb):
    half = QK_ROPE // 2
    return x * c + pltpu.roll(x, HEAD_PAD - half, 1) * sa + pltpu.roll(x, half, 1) * sb


def _rope_t(d, c, sa, sb):
    half = QK_ROPE // 2
    return d * c + pltpu.roll(d * sa, half, 1) + pltpu.roll(d * sb, HEAD_PAD - half, 1)


def _head_rms(x):
    r = lax.rsqrt(jnp.sum(x * x, axis=-1, keepdims=True) * (1.0 / QK_DIM) + EPS)
    return x * r, r


def _qk_prep_fwd(q_raw, k_raw, p, gq, gk, tabs, name):
    t = q_raw.shape[0]
    tm = _row_tile(t)

    def body(q_ref, k_ref, p_ref, gq_ref, gk_ref, c_ref, sa_ref, sb_ref, qo_ref, ko_ref):
        c, sa, sb = c_ref[...], sa_ref[...], sb_ref[...]
        qn, _ = _head_rms(q_ref[...])
        qo_ref[...] = _rope(qn * gq_ref[...], c, sa, sb).astype(qo_ref.dtype)
        kn, _ = _head_rms(k_ref[...] + p_ref[...])
        ko_ref[...] = _rope(kn * gk_ref[...], c, sa, sb).astype(ko_ref.dtype)

    head = pl.BlockSpec((tm, HEAD_PAD), lambda i, h: (i, h))
    tab = pl.BlockSpec((tm, HEAD_PAD), lambda i, h: (i, 0))
    vec = pl.BlockSpec((1, HEAD_PAD), lambda i, h: (0, 0))
    kpe = pl.BlockSpec((tm, HEAD_PAD), lambda i, h: (i, P_KPE // HEAD_PAD))
    return pl.pallas_call(
        body, name=name, grid=(t // tm, N_HEADS), in_specs=[head, head, kpe, vec, vec, tab, tab, tab],
        out_specs=(head, head),
        out_shape=(jax.ShapeDtypeStruct(q_raw.shape, BF16), jax.ShapeDtypeStruct(k_raw.shape, BF16)),
        compiler_params=_params("parallel", "parallel"),
    )(q_raw, k_raw, p, gq.reshape(1, HEAD_PAD), gk.reshape(1, HEAD_PAD), *tabs)


def _qk_prep_bwd(q_raw, k_raw, p, dq, dk, gq, gk, tabs, name):
    t = q_raw.shape[0]
    tm = _row_tile(t)

    def body(q_ref, k_ref, p_ref, dq_ref, dk_ref, gq_ref, gk_ref, c_ref, sa_ref, sb_ref,
             dqr_ref, dkr_ref, dkpe_ref, dgq_ref, dgk_ref):
        i, h = pl.program_id(0), pl.program_id(1)
        c, sa, sb = c_ref[...], sa_ref[...], sb_ref[...]

        def one(x, d, g_ref, dg_ref):
            n, r = _head_rms(x)
            dng = _rope_t(d, c, sa, sb)

            @pl.when(jnp.logical_and(i == 0, h == 0))
            def _():
                dg_ref[...] = jnp.zeros_like(dg_ref)

            dg_ref[...] += jnp.sum(dng * n, axis=0, keepdims=True)
            dn = dng * g_ref[...]
            return r * (dn - n * (jnp.sum(dn * n, axis=-1, keepdims=True) * (1.0 / QK_DIM)))

        dqr_ref[...] = one(q_ref[...], dq_ref[...], gq_ref, dgq_ref)
        dkr = one(k_ref[...] + p_ref[...], dk_ref[...], gk_ref, dgk_ref)
        dkr_ref[...] = dkr

        @pl.when(h == 0)
        def _():
            dkpe_ref[...] = dkr

        @pl.when(h > 0)
        def _():
            dkpe_ref[...] += dkr

    head = pl.BlockSpec((tm, HEAD_PAD), lambda i, h: (i, h))
    tab = pl.BlockSpec((tm, HEAD_PAD), lambda i, h: (i, 0))
    vec = pl.BlockSpec((1, HEAD_PAD), lambda i, h: (0, 0))
    kpe = pl.BlockSpec((tm, HEAD_PAD), lambda i, h: (i, P_KPE // HEAD_PAD))
    dqr, dkr, dkpe, dgq, dgk = pl.pallas_call(
        body, name=name, grid=(t // tm, N_HEADS), in_specs=[head, head, kpe, head, head, vec, vec, tab, tab, tab],
        out_specs=(head, head, tab, vec, vec),
        out_shape=(jax.ShapeDtypeStruct(q_raw.shape, F32), jax.ShapeDtypeStruct(k_raw.shape, F32),
                   jax.ShapeDtypeStruct((t, HEAD_PAD), F32), jax.ShapeDtypeStruct((1, HEAD_PAD), F32),
                   jax.ShapeDtypeStruct((1, HEAD_PAD), F32)),
        compiler_params=_params("arbitrary", "arbitrary"),
    )(q_raw, k_raw, p, dq, dk, gq.reshape(1, HEAD_PAD), gk.reshape(1, HEAD_PAD), *tabs)
    return dqr, dkr, dkpe, dgq.reshape(HEAD_PAD), dgk.reshape(HEAD_PAD)


def _dot_nt(a, b):
    return lax.dot_general(a, b, (((1,), (1,)), ((), ())), preferred_element_type=F32)


def _dot_tn(a, b):
    return lax.dot_general(a, b, (((0,), (0,)), ((), ())), preferred_element_type=F32)


def _diag_mask():
    rows = lax.broadcasted_iota(jnp.int32, (ATTN_BLOCK, ATTN_BLOCK), 0) // CHUNK
    cols = lax.broadcasted_iota(jnp.int32, (ATTN_BLOCK, ATTN_BLOCK), 1) // CHUNK
    return cols <= rows


def _attn_fwd(q, k, v, name):
    t = q.shape[0]
    bq = ATTN_BLOCK
    nq = t // bq

    def body(q_ref, k_ref, v_ref, o_ref, lse_ref):
        i = pl.program_id(1)
        qv = q_ref[...]

        def block(j, carry, masked):
            m, l, acc = carry
            rows = pl.ds(pl.multiple_of(j * bq, bq), bq)
            s = _dot_nt(qv, k_ref[rows, :]) * ATTN_SCALE
            if masked:
                s = jnp.where(_diag_mask(), s, -1e30)
            m_new = jnp.maximum(m, jnp.max(s, axis=-1, keepdims=True))
            alpha = jnp.exp(m - m_new)
            pe = jnp.exp(s - m_new)
            l = alpha * l + jnp.sum(pe, axis=-1, keepdims=True)
            acc = alpha * acc + jnp.dot(pe.astype(BF16), v_ref[rows, :], preferred_element_type=F32)
            return m_new, l, acc

        init = (jnp.full((bq, 1), -1e30, F32), jnp.zeros((bq, 1), F32), jnp.zeros((bq, HEAD_PAD), F32))
        carry = lax.fori_loop(0, i, lambda j, cr: block(j, cr, False), init)
        m, l, acc = block(i, carry, True)
        o_ref[...] = acc / l
        lse_ref[...] = jnp.broadcast_to(m + jnp.log(l), (bq, HEAD_PAD))

    blk = pl.BlockSpec((bq, HEAD_PAD), lambda h, i: (i, h))
    full = pl.BlockSpec((t, HEAD_PAD), lambda h, i: (0, h))
    return pl.pallas_call(
        body, name=name, grid=(N_HEADS, nq), in_specs=[blk, full, full], out_specs=(blk, blk),
        out_shape=(jax.ShapeDtypeStruct(q.shape, F32), jax.ShapeDtypeStruct(q.shape, F32)),
        compiler_params=_params("parallel", "parallel"),
    )(q, k, v)


def _attn_bwd(q, k, v, o, lse, do, name):
    t = q.shape[0]
    bq = ATTN_BLOCK
    nq = t // bq

    def body(q_ref, k_ref, v_ref, o_ref, lse_ref, do_ref, dq_ref, dk_ref, dv_ref, delta_ref):
        def rows_of(i):
            return pl.ds(pl.multiple_of(i * bq, bq), bq)

        def prep(i, _):
            r = rows_of(i)
            delta_ref[r, :] = jnp.broadcast_to(jnp.sum(do_ref[r, :] * o_ref[r, :], axis=-1, keepdims=True),
                                               (bq, HEAD_PAD))
            dq_ref[r, :] = jnp.zeros((bq, HEAD_PAD), F32)
            return 0

        lax.fori_loop(0, nq, prep, 0)

        def key_block(j, _):
            rj = rows_of(j)
            kb, vb = k_ref[rj, :], v_ref[rj, :]

            def query_block(i, carry, masked):
                dk, dv = carry
                ri = rows_of(i)
                qb, dob = q_ref[ri, :], do_ref[ri, :].astype(BF16)
                s = _dot_nt(qb, kb) * ATTN_SCALE
                if masked:
                    s = jnp.where(_diag_mask(), s, -1e30)
                pe = jnp.exp(s - lse_ref[ri, :][:, :1])
                dp = _dot_nt(dob, vb)
                ds = (pe * (dp - delta_ref[ri, :][:, :1]) * ATTN_SCALE).astype(BF16)
                dq_ref[ri, :] += jnp.dot(ds, kb, preferred_element_type=F32)
                return dk + _dot_tn(ds, qb), dv + _dot_tn(pe.astype(BF16), dob)

            zero = jnp.zeros((bq, HEAD_PAD), F32)
            carry = query_block(j, (zero, zero), True)
            dk, dv = lax.fori_loop(j + 1, nq, lambda i, cr: query_block(i, cr, False), carry)
            dk_ref[rj, :] = dk
            dv_ref[rj, :] = dv
            return 0

        lax.fori_loop(0, nq, key_block, 0)

    full = pl.BlockSpec((t, HEAD_PAD), lambda h: (0, h))
    shp = jax.ShapeDtypeStruct(q.shape, F32)
    return pl.pallas_call(
        body, name=name, grid=(N_HEADS,), in_specs=[full] * 6, out_specs=(full, full, full),
        out_shape=(shp, shp, shp), scratch_shapes=[pltpu.VMEM((t, HEAD_PAD), F32)],
        compiler_params=_params("parallel"),
    )(q, k, v, o, lse, do)


def _glu_ext(pc_ref, pp_ref, u_ref, tm, first):
    u_ref[CONV_HALO:CONV_HALO + tm, :] = pc_ref[:, P_A:P_G] * jax.nn.sigmoid(pc_ref[:, P_G:P_COLS])
    up = pp_ref[tm - CONV_HALO:tm, P_A:P_G] * jax.nn.sigmoid(pp_ref[tm - CONV_HALO:tm, P_G:P_COLS])
    u_ref[0:CONV_HALO, :] = jnp.where(first, 0.0, up)


def _conv_fwd(p, w, b, ln_g, ln_b, name):
    t = p.shape[0]
    tm = _row_tile(t, CONV_TILE)
    off = CONV_HALO - (CONV_K - 1)

    def body(pc_ref, pp_ref, w_ref, b_ref, g_ref, bb_ref, y_ref, o_ref, u_ref):
        _glu_ext(pc_ref, pp_ref, u_ref, tm, pl.program_id(0) == 0)
        acc = jnp.zeros((tm, CONV_W), F32)
        for kk in range(CONV_K):
            acc = acc + w_ref[kk:kk + 1, :] * u_ref[off + kk:off + kk + tm, :]
        y = acc + b_ref[...]
        y_ref[...] = y
        xc = y - jnp.mean(y, axis=-1, keepdims=True)
        lo = xc * lax.rsqrt(jnp.mean(xc * xc, axis=-1, keepdims=True) + EPS) * g_ref[...] + bb_ref[...]
        o_ref[...] = (lo * jax.nn.sigmoid(lo)).astype(o_ref.dtype)

    prow = pl.BlockSpec((tm, P_COLS), lambda i: (i, 0))
    pprev = pl.BlockSpec((tm, P_COLS), lambda i: (jnp.maximum(i - 1, 0), 0))
    vec = pl.BlockSpec((1, CONV_W), lambda i: (0, 0))
    row = pl.BlockSpec((tm, CONV_W), lambda i: (i, 0))
    return pl.pallas_call(
        body, name=name, grid=(t // tm,),
        in_specs=[prow, pprev, pl.BlockSpec((CONV_HALO, CONV_W), lambda i: (0, 0)), vec, vec, vec],
        out_specs=(row, row),
        out_shape=(jax.ShapeDtypeStruct((t, CONV_W), F32), jax.ShapeDtypeStruct((t, CONV_W), BF16)),
        scratch_shapes=[pltpu.VMEM((tm + CONV_HALO, CONV_W), F32)], compiler_params=_params("parallel"),
    )(p, p, w, b.reshape(1, CONV_W), ln_g.reshape(1, CONV_W), ln_b.reshape(1, CONV_W))


def _conv_bwd_ln(y, dout, ln_g, ln_b, name):
    t = y.shape[0]
    tm = _row_tile(t)

    def body(y_ref, d_ref, g_ref, bb_ref, dy_ref, dg_ref, db_ref, dcb_ref):
        yv = y_ref[...]
        xc = yv - jnp.mean(yv, axis=-1, keepdims=True)
        r = lax.rsqrt(jnp.mean(xc * xc, axis=-1, keepdims=True) + EPS)
        n = xc * r
        lo = n * g_ref[...] + bb_ref[...]
        s = jax.nn.sigmoid(lo)
        dlo = d_ref[...] * (s * (1.0 + lo * (1.0 - s)))
        dn = dlo * g_ref[...]
        dy = r * (dn - jnp.mean(dn, axis=-1, keepdims=True) - n * jnp.mean(dn * n, axis=-1, keepdims=True))
        dy_ref[...] = dy

        @pl.when(pl.program_id(0) == 0)
        def _():
            dg_ref[...] = jnp.zeros_like(dg_ref)
            db_ref[...] = jnp.zeros_like(db_ref)
            dcb_ref[...] = jnp.zeros_like(dcb_ref)

        dg_ref[...] += jnp.sum(dlo * n, axis=0, keepdims=True)
        db_ref[...] += jnp.sum(dlo, axis=0, keepdims=True)
        dcb_ref[...] += jnp.sum(dy, axis=0, keepdims=True)

    row = pl.BlockSpec((tm, CONV_W), lambda i: (i, 0))
    vec = pl.BlockSpec((1, CONV_W), lambda i: (0, 0))
    vshape = jax.ShapeDtypeStruct((1, CONV_W), F32)
    dy, dg, db, dcb = pl.pallas_call(
        body, name=name, grid=(t // tm,), in_specs=[row, row, vec, vec], out_specs=(row, vec, vec, vec),
        out_shape=(jax.ShapeDtypeStruct((t, CONV_W), F32), vshape, vshape, vshape),
        compiler_params=_params("arbitrary"),
    )(y, dout, ln_g.reshape(1, CONV_W), ln_b.reshape(1, CONV_W))
    return dy, dg.reshape(CONV_W), db.reshape(CONV_W), dcb.reshape(CONV_W)


def _conv_bwd_taps(p, dy, w, name):
    t = p.shape[0]
    tm = _row_tile(t, CONV_TILE)
    nt = t // tm
    off = CONV_HALO - (CONV_K - 1)

    def body(pc_ref, pp_ref, dyc_ref, dyn_ref, w_ref, dag_ref, dw_ref, u_ref, dye_ref):
        i = pl.program_id(0)
        _glu_ext(pc_ref, pp_ref, u_ref, tm, i == 0)
        dyc = dyc_ref[...]
        dye_ref[0:tm, :] = dyc
        dye_ref[tm:tm + CONV_HALO, :] = jnp.where(i == nt - 1, 0.0, dyn_ref[0:CONV_HALO, :])

        @pl.when(i == 0)
        def _():
            dw_ref[...] = jnp.zeros_like(dw_ref)

        du = jnp.zeros((tm, CONV_W), F32)
        for kk in range(CONV_K):
            dw_ref[kk:kk + 1, :] += jnp.sum(dyc * u_ref[off + kk:off + kk + tm, :], axis=0, keepdims=True)
            back = CONV_K - 1 - kk
            du = du + w_ref[kk:kk + 1, :] * dye_ref[back:back + tm, :]
        av, gv = pc_ref[:, P_A:P_G], pc_ref[:, P_G:P_COLS]
        s = jax.nn.sigmoid(gv)
        dag_ref[:, 0:CONV_W] = du * s
        dag_ref[:, CONV_W:2 * CONV_W] = du * av * (s * (1.0 - s))

    prow = pl.BlockSpec((tm, P_COLS), lambda i: (i, 0))
    pprev = pl.BlockSpec((tm, P_COLS), lambda i: (jnp.maximum(i - 1, 0), 0))
    row = pl.BlockSpec((tm, CONV_W), lambda i: (i, 0))
    nxt = pl.BlockSpec((tm, CONV_W), lambda i: (jnp.minimum(i + 1, nt - 1), 0))
    wspec = pl.BlockSpec((CONV_HALO, CONV_W), lambda i: (0, 0))
    return pl.pallas_call(
        body, name=name, grid=(nt,), in_specs=[prow, pprev, row, nxt, wspec],
        out_specs=(pl.BlockSpec((tm, 2 * CONV_W), lambda i: (i, 0)), wspec),
        out_shape=(jax.ShapeDtypeStruct((t, 2 * CONV_W), F32), jax.ShapeDtypeStruct((CONV_HALO, CONV_W), F32)),
        scratch_shapes=[pltpu.VMEM((tm + CONV_HALO, CONV_W), F32), pltpu.VMEM((tm + CONV_HALO, CONV_W), F32)],
        compiler_params=_params("arbitrary"),
    )(p, p, dy, dy, w)


def _loss_head(y, target, name):
    t, d = y.shape
    tm = _row_tile(t)

    def body(y_ref, t_ref, l_ref, dy_ref):
        err = y_ref[...] - t_ref[...]
        dy_ref[...] = err * (1.0 / d)

        @pl.when(pl.program_id(0) == 0)
        def _():
            l_ref[...] = jnp.zeros_like(l_ref)

        row = jnp.sum(err * err, axis=-1, keepdims=True) * (0.5 / d)
        l_ref[...] += jnp.broadcast_to(jnp.sum(row, axis=0, keepdims=True), (1, LANE))

    row = pl.BlockSpec((tm, d), lambda i: (i, 0))
    return pl.pallas_call(
        body, name=name, grid=(t // tm,), in_specs=[row, row],
        out_specs=(pl.BlockSpec((1, LANE), lambda i: (0, 0)), row),
        out_shape=(jax.ShapeDtypeStruct((1, LANE), F32), jax.ShapeDtypeStruct((t, d), F32)),
        compiler_params=_params("arbitrary"),
    )(y, target)


def _adamw(w, g, m, v, name):
    r, c = w.shape
    tr = _row_tile(r, 256)
    c1, c2 = 1.0 - ADAM_B1 ** ADAM_STEP, 1.0 - ADAM_B2 ** ADAM_STEP

    def body(w_ref, g_ref, m_ref, v_ref, d_ref, mo_ref, vo_ref):
        gv = g_ref[...]
        mn = ADAM_B1 * m_ref[...] + (1.0 - ADAM_B1) * gv
        vn = ADAM_B2 * v_ref[...] + (1.0 - ADAM_B2) * (gv * gv)
        mo_ref[...] = mn
        vo_ref[...] = vn
        d_ref[...] = -ADAM_LR * ((mn / c1) / (jnp.sqrt(vn / c2) + ADAM_EPS) + ADAM_WD * w_ref[...])

    blk = pl.BlockSpec((tr, c), lambda i: (i, 0))
    shp = jax.ShapeDtypeStruct((r, c), F32)
    return pl.pallas_call(
        body, name=name, grid=(r // tr,), in_specs=[blk] * 4, out_specs=(blk, blk, blk), out_shape=(shp, shp, shp),
        compiler_params=_params("parallel"),
    )(w, g, m, v)


def _sum_parts(parts, name):
    r, c = parts[0].shape
    tr = _row_tile(r, 256)

    def body(*refs):
        acc = refs[0][...]
        for ref in refs[1:-1]:
            acc = acc + ref[...]
        refs[-1][...] = acc

    blk = pl.BlockSpec((tr, c), lambda i: (i, 0))
    return pl.pallas_call(
        body, name=name, grid=(r // tr,), in_specs=[blk] * len(parts), out_specs=blk,
        out_shape=jax.ShapeDtypeStruct((r, c), F32), compiler_params=_params("parallel"),
    )(*parts)


def _place():
    return lax.axis_index("x"), lax.axis_index("y"), lax.axis_index("c")


def _all_gather(x, name, in_vmem):
    r, c = x.shape

    def body(x_ref, out_ref, send_sems, recv_sems, local_sem):
        px, py, pc = _place()
        me, sibling = (px, py, pc), (px, py, 1 - pc)
        chips = [(1 - px, py), (px, 1 - py), (1 - px, 1 - py)]

        def rows(bx, by, bc):
            return out_ref.at[pl.ds((4 * bx + 2 * by + bc) * r, r), :]

        def copy(k, block, to, src=None):
            return pltpu.make_async_remote_copy(
                src_ref=rows(*block) if src is None else src, dst_ref=rows(*block),
                send_sem=send_sems.at[k], recv_sem=recv_sems.at[k], device_id=to, device_id_type=MESH_ID)

        mine = pltpu.make_async_copy(x_ref, rows(*me), local_sem)
        mine.start()
        first = [copy(0, me, sibling, src=x_ref)]
        first += [copy(1 + j, me, (*chip, pc), src=x_ref) for j, chip in enumerate(chips)]
        for cp in first:
            cp.start()
        passed = [copy(4 + j, (*chip, pc), sibling) for j, chip in enumerate(chips)]
        for j, chip in enumerate(chips):
            copy(1 + j, (*chip, pc), me).wait_recv()
            passed[j].start()
        copy(0, sibling, me).wait_recv()
        for j, chip in enumerate(chips):
            copy(4 + j, (*chip, 1 - pc), me).wait_recv()
        for cp in first + passed:
            cp.wait_send()
        mine.wait()

    spec = VMEM_SPEC if in_vmem else ANY
    return pl.pallas_call(
        body, name=name, in_specs=[spec], out_specs=spec, out_shape=jax.ShapeDtypeStruct((N_DEV * r, c), x.dtype),
        scratch_shapes=[pltpu.SemaphoreType.DMA((7,)), pltpu.SemaphoreType.DMA((7,)), pltpu.SemaphoreType.DMA],
        compiler_params=pltpu.CompilerParams(vmem_limit_bytes=VMEM_LIMIT),
    )(x)


def _swap_sibling_halves(g, name):
    _, r, c = g.shape

    def body(g_ref, out_ref, send_sems, recv_sems):
        px, py, pc = _place()
        copies = [pltpu.make_async_remote_copy(
            src_ref=g_ref.at[2 * j + 1 - pc], dst_ref=out_ref.at[j], send_sem=send_sems.at[j],
            recv_sem=recv_sems.at[j], device_id=(px, py, 1 - pc), device_id_type=MESH_ID) for j in range(4)]
        for cp in copies:
            cp.start()
        for cp in copies:
            cp.wait_recv()
        for cp in copies:
            cp.wait_send()

    return pl.pallas_call(
        body, name=name, in_specs=[ANY], out_specs=ANY, out_shape=jax.ShapeDtypeStruct((4, r, c), g.dtype),
        scratch_shapes=[pltpu.SemaphoreType.DMA((4,)), pltpu.SemaphoreType.DMA((4,))],
        compiler_params=pltpu.CompilerParams(vmem_limit_bytes=VMEM_LIMIT),
    )(g)


def _add_sibling(g, got, name):
    _, r, c = g.shape
    tr = _row_tile(r, 256)

    def body(c_ref, g_ref, got_ref, o_ref):
        o_ref[...] = g_ref[...] + got_ref[...]

    grid_spec = pltpu.PrefetchScalarGridSpec(
        num_scalar_prefetch=1, grid=(4, r // tr),
        in_specs=[pl.BlockSpec((None, tr, c), lambda j, i, c_ref: (2 * j + c_ref[0], i, 0)),
                  pl.BlockSpec((None, tr, c), lambda j, i, c_ref: (j, i, 0))],
        out_specs=pl.BlockSpec((None, tr, c), lambda j, i, c_ref: (j, i, 0)))
    core = lax.axis_index("c").astype(jnp.int32).reshape(1)
    return pl.pallas_call(
        body, name=name, grid_spec=grid_spec, out_shape=jax.ShapeDtypeStruct((4, r, c), F32),
        compiler_params=_params("parallel", "parallel"),
    )(core, g, got)


def _swap_chip_parts(a, name):
    _, r, c = a.shape

    def body(a_ref, out_ref, send_sems, recv_sems):
        px, py, pc = _place()
        chips = [(1 - px, py), (px, 1 - py), (1 - px, 1 - py)]
        copies = [pltpu.make_async_remote_copy(
            src_ref=a_ref.at[2 * cx + cy], dst_ref=out_ref.at[k], send_sem=send_sems.at[k], recv_sem=recv_sems.at[k],
            device_id=(cx, cy, pc), device_id_type=MESH_ID) for k, (cx, cy) in enumerate(chips)]
        for cp in copies:
            cp.start()
        for cp in copies:
            cp.wait_recv()
        for cp in copies:
            cp.wait_send()

    return pl.pallas_call(
        body, name=name, in_specs=[ANY], out_specs=ANY, out_shape=jax.ShapeDtypeStruct((3, r, c), a.dtype),
        scratch_shapes=[pltpu.SemaphoreType.DMA((3,)), pltpu.SemaphoreType.DMA((3,))],
        compiler_params=pltpu.CompilerParams(vmem_limit_bytes=VMEM_LIMIT),
    )(a)


def _sum_chip_parts(a, got, name):
    _, r, c = a.shape
    tr = _row_tile(r, 256)

    def body(j_ref, a_ref, g0_ref, g1_ref, g2_ref, o_ref):
        o_ref[...] = ((a_ref[...] + g0_ref[...]) + g1_ref[...]) + g2_ref[...]

    def part(k):
        return pl.BlockSpec((None, tr, c), lambda i, j_ref: (k, i, 0))

    grid_spec = pltpu.PrefetchScalarGridSpec(
        num_scalar_prefetch=1, grid=(r // tr,),
        in_specs=[pl.BlockSpec((None, tr, c), lambda i, j_ref: (j_ref[0], i, 0)), part(0), part(1), part(2)],
        out_specs=pl.BlockSpec((tr, c), lambda i, j_ref: (i, 0)))
    chip = (2 * lax.axis_index("x") + lax.axis_index("y")).astype(jnp.int32).reshape(1)
    return pl.pallas_call(
        body, name=name, grid_spec=grid_spec, out_shape=jax.ShapeDtypeStruct((r, c), F32),
        compiler_params=_params("parallel"),
    )(chip, a, got, got, got)


def _flat_rows(a, width):
    return a.reshape(-1, width)


def _full_from_blocks(blocks, name):
    if name in COL_SHARDED:
        _, l, k, nb = blocks.shape
        return jnp.transpose(blocks, (1, 2, 0, 3)).reshape(l, k, N_DEV * nb)
    _, l, rb, n = blocks.shape
    return jnp.transpose(blocks, (1, 0, 2, 3)).reshape(l, N_DEV * rb, n)


def _blocks_from_full(full, name):
    if name in COL_SHARDED:
        l, k, n = full.shape
        return jnp.transpose(full.reshape(l, k, N_DEV, n // N_DEV), (2, 0, 1, 3))
    l, rows, n = full.shape
    return jnp.transpose(full.reshape(l, N_DEV, rows // N_DEV, n), (1, 0, 2, 3))


def _pad_heads(w, width):
    k = w.shape[0]
    return jnp.pad(w.reshape(k, N_HEADS, width), ((0, 0), (0, 0), (0, HEAD_PAD - width))).reshape(k, N_HEADS * HEAD_PAD)


def _unpad_heads(w, width):
    k = w.shape[0]
    return w.reshape(k, N_HEADS, HEAD_PAD)[:, :, :width].reshape(k, N_HEADS * width)


def _layer_operands(full, vec, conv_w_full, l):
    w_in = full['w_in'][l]
    kpe = jnp.pad(w_in[:, LAT:LAT + QK_ROPE], ((0, 0), (QK_NOPE, HEAD_PAD - QK_DIM)))
    w_ukv = full['w_ukv'][l].reshape(KV_LORA, N_HEADS, QK_NOPE + V_DIM)
    w_out = full['w_out'][l]
    d_model = w_out.shape[1]
    wo_attn = jnp.pad(w_out[:N_HEADS * V_DIM].reshape(N_HEADS, V_DIM, d_model),
                      ((0, 0), (0, HEAD_PAD - V_DIM), (0, 0))).reshape(N_HEADS * HEAD_PAD, d_model)
    ops = {
        'wgu1': jnp.concatenate([full['ffn1_w_gate'][l], full['ffn1_w_up'][l]], axis=1),
        'wd1': full['ffn1_w_down'][l],
        'wgu2': jnp.concatenate([full['ffn2_w_gate'][l], full['ffn2_w_up'][l]], axis=1),
        'wd2': full['ffn2_w_down'][l],
        'w_in': jnp.concatenate([w_in[:, :LAT], kpe, w_in[:, LAT + QK_ROPE:]], axis=1),
        'w_q': _pad_heads(full['w_uq'][l], QK_DIM),
        'w_k': _pad_heads(w_ukv[:, :, :QK_NOPE].reshape(KV_LORA, N_HEADS * QK_NOPE), QK_NOPE),
        'w_v': _pad_heads(w_ukv[:, :, QK_NOPE:].reshape(KV_LORA, N_HEADS * V_DIM), V_DIM),
        'wo_attn': wo_attn,
        'wo_conv': w_out[N_HEADS * V_DIM:],
        'conv_w': jnp.pad(conv_w_full[l], ((0, CONV_HALO - CONV_K), (0, 0))),
        'gq': jnp.pad(vec['q_norm'][l], (0, HEAD_PAD - QK_DIM)),
        'gk': jnp.pad(vec['k_norm'][l], (0, HEAD_PAD - QK_DIM)),
    }
    for n in ('ffn1_norm', 'mix_norm', 'q_latent_norm', 'kv_latent_norm', 'conv_b', 'conv_ln_g', 'conv_ln_b',
              'ffn2_norm', 'post_norm'):
        ops[n] = vec[n][l]
    return ops


def _ffn_fwd(x, g, wgu, wd, tag):
    h = _rms_fwd(x, g, BF16, f"rms_fwd_{tag}")
    ab = _mm(h, wgu, name=f"ffn_up_{tag}")
    z = _swiglu_fwd(ab, f"swiglu_fwd_{tag}")
    y = _mm(z, wd, res=x, scale=0.5, name=f"ffn_down_{tag}")
    return y, (x, h, ab, z)


def _ffn_bwd(dy, saved, g, wgu, wd, tag):
    x, h, ab, z = saved
    dz = _mm(dy, wd, tb=True, scale=0.5, name=f"ffn_dz_{tag}")
    d_wd = _mm(z, dy, ta=True, scale=0.5, name=f"ffn_dwd_{tag}")
    dab = _swiglu_bwd(ab, dz, f"swiglu_bwd_{tag}")
    d_wgu = _mm(h, dab, ta=True, name=f"ffn_dwgu_{tag}")
    dh = _mm(dab, wgu, tb=True, name=f"ffn_dh_{tag}")
    dx, dg = _rms_bwd(x, g, dh, dy, f"rms_bwd_{tag}")
    return dx, dg, d_wgu, d_wd


def _mixer_fwd(x, ops, tabs):
    h = _rms_fwd(x, ops['mix_norm'], BF16, "rms_fwd_mix")
    p = _mm(h, ops['w_in'], name="mix_in")
    qln, kvln = _lat_norm_fwd(p, ops['q_latent_norm'], ops['kv_latent_norm'], "lat_norm_fwd")
    q_raw = _mm(qln, ops['w_q'], name="mix_q")
    k_raw = _mm(kvln, ops['w_k'], name="mix_k")
    v = _mm(kvln, ops['w_v'], out_dtype=BF16, name="mix_v")
    q, k = _qk_prep_fwd(q_raw, k_raw, p, ops['gq'], ops['gk'], tabs, "qk_prep_fwd")
    o, lse = _attn_fwd(q, k, v, "attn_fwd")
    y_conv, cv = _conv_fwd(p, ops['conv_w'], ops['conv_b'], ops['conv_ln_g'], ops['conv_ln_b'], "conv_fwd")
    x_attn = _mm(o, ops['wo_attn'], res=x, name="mix_out_attn")
    x_out = _mm(cv, ops['wo_conv'], res=x_attn, name="mix_out_conv")
    return x_out, (x, h, p, qln, kvln, q_raw, k_raw, v, q, k, o, lse, y_conv, cv)


def _mixer_bwd(dx_out, saved, ops, tabs):
    x, h, p, qln, kvln, q_raw, k_raw, v, q, k, o, lse, y_conv, cv = saved
    g = {}
    do = _mm(dx_out, ops['wo_attn'], tb=True, name="mix_do")
    dcv = _mm(dx_out, ops['wo_conv'], tb=True, name="mix_dcv")
    g['wo_attn'] = _mm(o, dx_out, ta=True, name="mix_dwo_attn")
    g['wo_conv'] = _mm(cv, dx_out, ta=True, name="mix_dwo_conv")
    dq, dk, dv = _attn_bwd(q, k, v, o, lse, do, "attn_bwd")
    dq_raw, dk_raw, dkpe, g['gq'], g['gk'] = _qk_prep_bwd(q_raw, k_raw, p, dq, dk, ops['gq'], ops['gk'], tabs,
                                                          "qk_prep_bwd")
    g['w_q'] = _mm(qln, dq_raw, ta=True, name="mix_dwq")
    g['w_k'] = _mm(kvln, dk_raw, ta=True, name="mix_dwk")
    g['w_v'] = _mm(kvln, dv, ta=True, name="mix_dwv")
    dqln = _mm(dq_raw, ops['w_q'], tb=True, name="mix_dqln")
    dkvln = _mm(dk_raw, ops['w_k'], tb=True, name="mix_dkvln_k")
    dkvln = _mm(dv, ops['w_v'], tb=True, res=dkvln, name="mix_dkvln_v")
    dp_lat, g['q_latent_norm'], g['kv_latent_norm'] = _lat_norm_bwd(
        p, ops['q_latent_norm'], ops['kv_latent_norm'], dqln, dkvln, "lat_norm_bwd")
    dy_conv, g['conv_ln_g'], g['conv_ln_b'], g['conv_b'] = _conv_bwd_ln(
        y_conv, dcv, ops['conv_ln_g'], ops['conv_ln_b'], "conv_bwd_ln")
    dag, g['conv_w'] = _conv_bwd_taps(p, dy_conv, ops['conv_w'], "conv_bwd_taps")
    dp = jnp.concatenate([dp_lat, dkpe, dag], axis=1)
    g['w_in'] = _mm(h, dp, ta=True, name="mix_dw_in")
    dh = _mm(dp, ops['w_in'], tb=True, name="mix_dh")
    dx, g['mix_norm'] = _rms_bwd(x, ops['mix_norm'], dh, dx_out, "rms_bwd_mix")
    return dx, g


def _layer_grads_to_params(g):
    f = g['wgu1'].shape[1] // 2
    d_w_in = g['w_in']
    d_wk = _unpad_heads(g['w_k'], QK_NOPE).reshape(KV_LORA, N_HEADS, QK_NOPE)
    d_wv = _unpad_heads(g['w_v'], V_DIM).reshape(KV_LORA, N_HEADS, V_DIM)
    d_model = g['wo_attn'].shape[1]
    d_wo_attn = g['wo_attn'].reshape(N_HEADS, HEAD_PAD, d_model)[:, :V_DIM].reshape(N_HEADS * V_DIM, d_model)
    return {
        'ffn1_norm': g['ffn1_norm'], 'ffn1_w_gate': g['wgu1'][:, :f], 'ffn1_w_up': g['wgu1'][:, f:],
        'ffn1_w_down': g['wd1'], 'mix_norm': g['mix_norm'],
        'w_in': jnp.concatenate([d_w_in[:, :LAT], d_w_in[:, LAT + QK_NOPE:LAT + QK_DIM], d_w_in[:, P_A:]], axis=1),
        'q_latent_norm': g['q_latent_norm'], 'w_uq': _unpad_heads(g['w_q'], QK_DIM),
        'kv_latent_norm': g['kv_latent_norm'],
        'w_ukv': jnp.concatenate([d_wk, d_wv], axis=2).reshape(KV_LORA, N_HEADS * (QK_NOPE + V_DIM)),
        'q_norm': g['gq'][:QK_DIM], 'k_norm': g['gk'][:QK_DIM], 'conv_w': g['conv_w'][:CONV_K],
        'conv_b': g['conv_b'], 'conv_ln_g': g['conv_ln_g'], 'conv_ln_b': g['conv_ln_b'],
        'w_out': jnp.concatenate([d_wo_attn, g['wo_conv']], axis=0),
        'ffn2_norm': g['ffn2_norm'], 'ffn2_w_gate': g['wgu2'][:, :f], 'ffn2_w_up': g['wgu2'][:, f:],
        'ffn2_w_down': g['wd2'], 'post_norm': g['post_norm'],
    }


def kernel(x, ffn1_norm, ffn1_w_gate, ffn1_w_up, ffn1_w_down, mix_norm, w_in, q_latent_norm, w_uq, kv_latent_norm, w_ukv, q_norm, k_norm, conv_w, conv_b, conv_ln_g, conv_ln_b, w_out, ffn2_norm, ffn2_w_gate, ffn2_w_up, ffn2_w_down, post_norm, loss_target, m_ffn1_norm, m_ffn1_w_gate, m_ffn1_w_up, m_ffn1_w_down, m_mix_norm, m_w_in, m_q_latent_norm, m_w_uq, m_kv_latent_norm, m_w_ukv, m_q_norm, m_k_norm, m_conv_w, m_conv_b, m_conv_ln_g, m_conv_ln_b, m_w_out, m_ffn2_norm, m_ffn2_w_gate, m_ffn2_w_up, m_ffn2_w_down, m_post_norm, v_ffn1_norm, v_ffn1_w_gate, v_ffn1_w_up, v_ffn1_w_down, v_mix_norm, v_w_in, v_q_latent_norm, v_w_uq, v_kv_latent_norm, v_w_ukv, v_q_norm, v_k_norm, v_conv_w, v_conv_b, v_conv_ln_g, v_conv_ln_b, v_w_out, v_ffn2_norm, v_ffn2_w_gate, v_ffn2_w_up, v_ffn2_w_down, v_post_norm):
    args = locals()
    w = {n: args[n] for n in WEIGHTS}
    mom = {n: args["m_" + n] for n in WEIGHTS}
    var = {n: args["v_" + n] for n in WEIGHTS}
    depth = ffn1_norm.shape[0]
    x0 = x.reshape(x.shape[-2:])
    target = loss_target.reshape(loss_target.shape[-2:])
    t, d_model = x0.shape
    my_block = 4 * lax.axis_index("x") + 2 * lax.axis_index("y") + lax.axis_index("c")

    rows_of = {n: w[n].size // d_model for n in BIG}
    flat = jnp.concatenate([_flat_rows(w[n].astype(BF16), d_model) for n in BIG], axis=0)
    n_rows = flat.shape[0]
    gathered = _all_gather(flat, "gather_matrices", in_vmem=False).reshape(N_DEV, n_rows, d_model)
    full, start = {}, 0
    for n in BIG:
        blocks = gathered[:, start:start + rows_of[n]].reshape((N_DEV,) + w[n].shape)
        full[n] = _full_from_blocks(blocks, n)
        start += rows_of[n]
    cw = conv_w.reshape(-1)
    cw_rows = -(-cw.size // (8 * LANE)) * 8
    cw_flat = jnp.pad(cw, (0, cw_rows * LANE - cw.size)).reshape(cw_rows, LANE)
    cw_all = _all_gather(cw_flat, "gather_conv_w", in_vmem=True).reshape(N_DEV, cw_rows * LANE)[:, :cw.size]
    conv_w_full = jnp.transpose(cw_all.reshape((N_DEV,) + conv_w.shape), (1, 2, 0, 3)).reshape(depth, CONV_K, CONV_W)
    vec = {n: w[n] for n in VECTORS}
    ops = [_layer_operands(full, vec, conv_w_full, l) for l in range(depth)]
    tabs = _rope_tables(t)

    saved, xl = [], x0
    for l in range(depth):
        o = ops[l]
        x1, s1 = _ffn_fwd(xl, o['ffn1_norm'], o['wgu1'], o['wd1'], "a")
        x2, sm = _mixer_fwd(x1, o, tabs)
        x3, s2 = _ffn_fwd(x2, o['ffn2_norm'], o['wgu2'], o['wd2'], "a")
        xl = _rms_fwd(x3, o['post_norm'], F32, "rms_fwd_post")
        saved.append((s1, sm, s2, x3))
    loss_part, dx = _loss_head(xl, target, "loss_head")
    loss = lax.psum(loss_part[0, 0], ("x", "y", "c"))

    grads = [None] * depth
    for l in reversed(range(depth)):
        o = ops[l]
        s1, sm, s2, x3 = saved[l]
        g = {}
        dx, g['post_norm'] = _rms_bwd(x3, o['post_norm'], dx, None, "rms_bwd_post")
        dx, g['ffn2_norm'], g['wgu2'], g['wd2'] = _ffn_bwd(dx, s2, o['ffn2_norm'], o['wgu2'], o['wd2'], "a")
        dx, gm = _mixer_bwd(dx, sm, o, tabs)
        g.update(gm)
        dx, g['ffn1_norm'], g['wgu1'], g['wd1'] = _ffn_bwd(dx, s1, o['ffn1_norm'], o['wgu1'], o['wd1'], "a")
        grads[l] = _layer_grads_to_params(g)
    grad_x = dx.reshape(x.shape)
    part = {n: jnp.stack([grads[l][n] for l in range(depth)]) for n in WEIGHTS}

    send = jnp.concatenate([_blocks_from_full(part[n], n).reshape(N_DEV, rows_of[n], d_model) for n in BIG], axis=1)
    from_sibling = _swap_sibling_halves(send, "grads_to_sibling")
    chip_parts = _add_sibling(send, from_sibling, "grads_add_sibling")
    from_chips = _swap_chip_parts(chip_parts, "grads_to_chips")
    mine = _sum_chip_parts(chip_parts, from_chips, "grads_sum_chips")
    grad, start = {}, 0
    for n in BIG:
        grad[n] = mine[start:start + rows_of[n]].reshape(w[n].shape)
        start += rows_of[n]

    small = jnp.concatenate([part[n].reshape(-1) for n in VECTORS] + [part['conv_w'].reshape(-1)])
    s_rows = -(-small.size // (8 * LANE)) * 8
    small = jnp.pad(small, (0, s_rows * LANE - small.size)).reshape(s_rows, LANE)
    small_all = _all_gather(small, "gather_small_grads", in_vmem=True)
    small_sum = _sum_parts([small_all[k * s_rows:(k + 1) * s_rows] for k in range(N_DEV)], "sum_small_grads")
    small_sum = small_sum.reshape(-1)
    start = 0
    for n in VECTORS:
        grad[n] = small_sum[start:start + w[n].size].reshape(w[n].shape)
        start += w[n].size
    cw_grad = small_sum[start:start + depth * CONV_K * CONV_W].reshape(depth, CONV_K, CONV_W)
    nb = conv_w.shape[-1]
    grad['conv_w'] = lax.dynamic_slice_in_dim(cw_grad, my_block * nb, nb, axis=2)

    delta, new_m, new_v = {}, {}, {}
    for n in BIG + ['conv_w']:
        shp = w[n].shape
        two_d = lambda a: a.reshape(-1, shp[-1])
        dl, mn, vn = _adamw(two_d(w[n]), two_d(grad[n]), two_d(mom[n]), two_d(var[n]), "adamw_" + n)
        delta[n], new_m[n], new_v[n] = dl.reshape(shp), mn.reshape(shp), vn.reshape(shp)
    vcat = lambda src: jnp.concatenate([src[n].reshape(-1) for n in VECTORS]).reshape(-1, LANE)
    dl, mn, vn = _adamw(vcat(w), vcat(grad), vcat(mom), vcat(var), "adamw_vectors")
    start = 0
    for n in VECTORS:
        sl = lambda a: a.reshape(-1)[start:start + w[n].size].reshape(w[n].shape)
        delta[n], new_m[n], new_v[n] = sl(dl), sl(mn), sl(vn)
        start += w[n].size

    return (loss, grad_x, *[grad[n] for n in WEIGHTS], *[delta[n] for n in WEIGHTS],
            *[new_m[n] for n in WEIGHTS], *[new_v[n] for n in WEIGHTS])
```

```python
import functools

import jax
import jax.numpy as jnp
from jax import lax
from jax.experimental import pallas as pl
from jax.experimental.pallas import tpu as pltpu

F32, BF16 = jnp.float32, jnp.bfloat16

N_DEV = 8
N_HEADS = 8
QK_NOPE, QK_ROPE, V_DIM = 64, 32, 64
QK_DIM = QK_NOPE + QK_ROPE
HEAD_PAD = 128
Q_LORA, KV_LORA = 384, 256
LAT = Q_LORA + KV_LORA
CONV_W, CONV_K = 512, 31
CONV_HALO = 32
CHUNK = 64
ROPE_THETA = 10000.0
EPS = 1e-6
ATTN_SCALE = QK_DIM ** -0.5
P_KPE = LAT
P_A = LAT + HEAD_PAD
P_G = P_A + CONV_W
P_COLS = P_G + CONV_W

ADAM_LR, ADAM_B1, ADAM_B2, ADAM_EPS, ADAM_WD, ADAM_STEP = 0.001, 0.9, 0.999, 1e-08, 0.01, 10

V7X_VMEM_BYTES = 64 << 20
VMEM_LIMIT = V7X_VMEM_BYTES - (8 << 20)
MM_VMEM_BUDGET = 36 << 20
LANE = 128
ROW_TILE = 512
ATTN_BLOCK = 512
CONV_TILE = 256

MESH_ID = pl.DeviceIdType.MESH
ANY = pl.BlockSpec(memory_space=pl.ANY)
VMEM_SPEC = pl.BlockSpec(memory_space=pltpu.VMEM)

WEIGHTS = ['ffn1_norm', 'ffn1_w_gate', 'ffn1_w_up', 'ffn1_w_down', 'mix_norm', 'w_in', 'q_latent_norm', 'w_uq',
           'kv_latent_norm', 'w_ukv', 'q_norm', 'k_norm', 'conv_w', 'conv_b', 'conv_ln_g', 'conv_ln_b', 'w_out',
           'ffn2_norm', 'ffn2_w_gate', 'ffn2_w_up', 'ffn2_w_down', 'post_norm']
COL_SHARDED = ['ffn1_w_gate', 'ffn1_w_up', 'w_in', 'w_uq', 'w_ukv', 'ffn2_w_gate', 'ffn2_w_up']
ROW_SHARDED = ['ffn1_w_down', 'w_out', 'ffn2_w_down']
REST = ['w_in', 'w_uq', 'w_ukv', 'w_out']
BIG = ['ffn1_w_gate', 'ffn1_w_up', 'ffn1_w_down', 'w_in', 'w_uq', 'w_ukv', 'w_out', 'ffn2_w_gate', 'ffn2_w_up',
       'ffn2_w_down']
VECTORS = ['ffn1_norm', 'mix_norm', 'q_latent_norm', 'kv_latent_norm', 'q_norm', 'k_norm', 'conv_b', 'conv_ln_g',
           'conv_ln_b', 'ffn2_norm', 'post_norm']


def _params(*sem):
    return pltpu.CompilerParams(dimension_semantics=sem if sem else None, vmem_limit_bytes=VMEM_LIMIT)


def _tile(n, cap):
    if n <= cap:
        return n
    best = 0
    for d in range(LANE, cap + 1, LANE):
        if n % d == 0:
            best = d
    assert best, (n, cap)
    return best


def _row_tile(n, cap=ROW_TILE, mult=8):
    if n <= cap:
        return n
    best = 0
    for d in range(mult, cap + 1, mult):
        if n % d == 0:
            best = d
    assert best, (n, cap)
    return best


def _mm(a, b, *, name, ta=False, tb=False, res=None, scale=1.0, out_dtype=F32, tm=None, tn=None, blocks=None):
    (kdim, m) = a.shape if ta else a.shape[::-1]
    (n, kb) = b.shape if tb else b.shape[::-1]
    assert kdim == kb, (a.shape, b.shape, ta, tb)
    tm, tn = tm or _tile(m, 512), tn or _tile(n, 1024)
    if blocks is not None:
        tm, tn = (tm, n // blocks[1]) if blocks[0] == 'col' else (m // blocks[1], tn)
    size = lambda arr: jnp.dtype(arr.dtype).itemsize
    out_bytes = tm * tn * ((6 if blocks is not None else jnp.dtype(out_dtype).itemsize) + (4 if res is not None else 0))

    def vmem_need(tk):
        return 2 * (tm * tk * size(a) + tk * tn * size(b) + out_bytes) + (tm * tn * 4 if tk < kdim else 0)

    tk = kdim
    for cand in [d for d in range(kdim - LANE, 0, -LANE) if kdim % d == 0]:
        if vmem_need(tk) <= MM_VMEM_BUDGET:
            break
        tk = cand
    nk = kdim // tk
    n_in = 3 if res is not None else 2
    n_out = 2 if blocks is not None else 1
    dims = (((0 if ta else 1,), (1 if tb else 0,)), ((), ()))

    def body(*refs):
        a_ref, b_ref = refs[0], refs[1]
        r_ref = refs[2] if res is not None else None
        o_refs = refs[n_in:n_in + n_out]
        acc_ref = refs[-1] if nk > 1 else None
        part = lax.dot_general(a_ref[...].astype(BF16), b_ref[...].astype(BF16), dims, preferred_element_type=F32)

        def finish(acc):
            if scale != 1.0:
                acc = acc * scale
            if r_ref is not None:
                acc = r_ref[...] + acc
            for o_ref in o_refs:
                o_ref[...] = acc.astype(o_ref.dtype)

        if nk == 1:
            finish(part)
        else:
            k = pl.program_id(2)

            @pl.when(k == 0)
            def _():
                acc_ref[...] = part

            @pl.when(k > 0)
            def _():
                acc_ref[...] += part

            @pl.when(k == nk - 1)
            def _():
                finish(acc_ref[...])

    a_spec = pl.BlockSpec((tk, tm), lambda i, j, k: (k, i)) if ta else pl.BlockSpec((tm, tk), lambda i, j, k: (i, k))
    b_spec = pl.BlockSpec((tn, tk), lambda i, j, k: (j, k)) if tb else pl.BlockSpec((tk, tn), lambda i, j, k: (k, j))
    plain = pl.BlockSpec((tm, tn), lambda i, j, k: (i, j))
    if blocks is None:
        out_specs, out_shape = plain, jax.ShapeDtypeStruct((m, n), out_dtype)
    else:
        if blocks[0] == 'col':
            o_spec, shp = pl.BlockSpec((None, tm, tn), lambda i, j, k: (j, i, 0)), (blocks[1], m, tn)
        else:
            o_spec, shp = pl.BlockSpec((None, tm, tn), lambda i, j, k: (i, 0, j)), (blocks[1], tm, n)
        out_specs, out_shape = (o_spec, o_spec), (jax.ShapeDtypeStruct(shp, F32), jax.ShapeDtypeStruct(shp, BF16))
    in_specs = [a_spec, b_spec] + ([plain] if res is not None else [])
    args = (a, b) + ((res,) if res is not None else ())
    return pl.pallas_call(
        body, name=name, grid=(m // tm, n // tn, nk), in_specs=in_specs, out_specs=out_specs, out_shape=out_shape,
        scratch_shapes=[pltpu.VMEM((tm, tn), F32)] if nk > 1 else [],
        compiler_params=_params("parallel", "parallel", "arbitrary"),
    )(*args)


def _rms_fwd(x, g, out_dtype, name):
    t, d = x.shape
    tm = _row_tile(t)

    def body(x_ref, g_ref, o_ref):
        xv = x_ref[...]
        r = lax.rsqrt(jnp.mean(xv * xv, axis=-1, keepdims=True) + EPS)
        o_ref[...] = (xv * r * g_ref[...]).astype(o_ref.dtype)

    return pl.pallas_call(
        body, name=name, grid=(t // tm,),
        in_specs=[pl.BlockSpec((tm, d), lambda i: (i, 0)), pl.BlockSpec((1, d), lambda i: (0, 0))],
        out_specs=pl.BlockSpec((tm, d), lambda i: (i, 0)),
        out_shape=jax.ShapeDtypeStruct((t, d), out_dtype), compiler_params=_params("parallel"),
    )(x, g.reshape(1, d))


def _rms_bwd(x, g, dh, res, name):
    t, d = x.shape
    tm = _row_tile(t)

    def body(*refs):
        x_ref, g_ref, dh_ref = refs[:3]
        r_ref = refs[3] if res is not None else None
        dx_ref, dxb_ref, dg_ref = refs[-3:]
        xv, dhv = x_ref[...], dh_ref[...]
        r = lax.rsqrt(jnp.mean(xv * xv, axis=-1, keepdims=True) + EPS)
        y = xv * r
        dy = dhv * g_ref[...]
        dx = r * (dy - y * jnp.mean(dy * y, axis=-1, keepdims=True))
        if r_ref is not None:
            dx = r_ref[...] + dx
        dx_ref[...] = dx
        dxb_ref[...] = dx.astype(BF16)

        @pl.when(pl.program_id(0) == 0)
        def _():
            dg_ref[...] = jnp.zeros_like(dg_ref)

        dg_ref[...] += jnp.sum(dhv * y, axis=0, keepdims=True)

    row = pl.BlockSpec((tm, d), lambda i: (i, 0))
    vec = pl.BlockSpec((1, d), lambda i: (0, 0))
    args = (x, g.reshape(1, d), dh) + ((res,) if res is not None else ())
    dx, dxb, dg = pl.pallas_call(
        body, name=name, grid=(t // tm,), in_specs=[row, vec, row] + ([row] if res is not None else []),
        out_specs=(row, row, vec),
        out_shape=(jax.ShapeDtypeStruct((t, d), F32), jax.ShapeDtypeStruct((t, d), BF16),
                   jax.ShapeDtypeStruct((1, d), F32)),
        compiler_params=_params("arbitrary"),
    )(*args)
    return dx, dxb, dg.reshape(d)


FFN_PAIR = 2


def _ffn_up(h, wgu, fp, name):
    t, d = h.shape
    tm, tn = _tile(t, 512), FFN_PAIR * 2 * fp
    nj = wgu.shape[1] // tn

    def body(h_ref, w_ref, ab_ref, z_ref):
        ab = jnp.dot(h_ref[...], w_ref[...], preferred_element_type=F32)
        ab_ref[...] = ab
        for e in range(FFN_PAIR):
            av, bv = ab[:, 2 * fp * e:2 * fp * e + fp], ab[:, 2 * fp * e + fp:2 * fp * (e + 1)]
            z_ref[:, fp * e:fp * (e + 1)] = (av * jax.nn.sigmoid(av) * bv).astype(z_ref.dtype)

    return pl.pallas_call(
        body, name=name, grid=(nj, t // tm),
        in_specs=[pl.BlockSpec((tm, d), lambda j, i: (i, 0)), pl.BlockSpec((d, tn), lambda j, i: (0, j))],
        out_specs=(pl.BlockSpec((tm, tn), lambda j, i: (i, j)), pl.BlockSpec((tm, tn // 2), lambda j, i: (i, j))),
        out_shape=(jax.ShapeDtypeStruct((t, wgu.shape[1]), F32), jax.ShapeDtypeStruct((t, wgu.shape[1] // 2), BF16)),
        compiler_params=_params("parallel", "parallel"),
    )(h, wgu)


def _ffn_dab(dyb, wd, ab, fp, name):
    t, d = dyb.shape
    tm, tn = _tile(t, 512), FFN_PAIR * 2 * fp
    nj = ab.shape[1] // tn

    def body(dy_ref, wd_ref, ab_ref, dab_ref):
        dz = _dot_nt(dy_ref[...], wd_ref[...]) * 0.5
        for e in range(FFN_PAIR):
            av, bv = ab_ref[:, 2 * fp * e:2 * fp * e + fp], ab_ref[:, 2 * fp * e + fp:2 * fp * (e + 1)]
            dze = dz[:, fp * e:fp * (e + 1)]
            s = jax.nn.sigmoid(av)
            dab_ref[:, 2 * fp * e:2 * fp * e + fp] = (dze * bv * (s * (1.0 + av * (1.0 - s)))).astype(dab_ref.dtype)
            dab_ref[:, 2 * fp * e + fp:2 * fp * (e + 1)] = (dze * (av * s)).astype(dab_ref.dtype)

    return pl.pallas_call(
        body, name=name, grid=(nj, t // tm),
        in_specs=[pl.BlockSpec((tm, d), lambda j, i: (i, 0)), pl.BlockSpec((tn // 2, d), lambda j, i: (j, 0)),
                  pl.BlockSpec((tm, tn), lambda j, i: (i, j))],
        out_specs=pl.BlockSpec((tm, tn), lambda j, i: (i, j)),
        out_shape=jax.ShapeDtypeStruct(ab.shape, BF16), compiler_params=_params("parallel", "parallel"),
    )(dyb, wd, ab)


def _lat_norm_fwd(p, g_q, g_kv, name):
    t = p.shape[0]
    tm = _row_tile(t)

    def body(p_ref, gq_ref, gkv_ref, q_ref, kv_ref):
        for lo, hi, g_ref, o_ref in ((0, Q_LORA, gq_ref, q_ref), (Q_LORA, LAT, gkv_ref, kv_ref)):
            xv = p_ref[:, lo:hi]
            r = lax.rsqrt(jnp.mean(xv * xv, axis=-1, keepdims=True) + EPS)
            o_ref[...] = (xv * r * g_ref[...]).astype(o_ref.dtype)

    return pl.pallas_call(
        body, name=name, grid=(t // tm,),
        in_specs=[pl.BlockSpec((tm, P_COLS), lambda i: (i, 0)), pl.BlockSpec((1, Q_LORA), lambda i: (0, 0)),
                  pl.BlockSpec((1, KV_LORA), lambda i: (0, 0))],
        out_specs=(pl.BlockSpec((tm, Q_LORA), lambda i: (i, 0)), pl.BlockSpec((tm, KV_LORA), lambda i: (i, 0))),
        out_shape=(jax.ShapeDtypeStruct((t, Q_LORA), BF16), jax.ShapeDtypeStruct((t, KV_LORA), BF16)),
        compiler_params=_params("parallel"),
    )(p, g_q.reshape(1, Q_LORA), g_kv.reshape(1, KV_LORA))


def _lat_norm_bwd(p, g_q, g_kv, dq, dkv, name):
    t = p.shape[0]
    tm = _row_tile(t)

    def body(p_ref, gq_ref, gkv_ref, dq_ref, dkv_ref, dp_ref, dgq_ref, dgkv_ref):
        first = pl.program_id(0) == 0
        for lo, hi, g_ref, d_ref, dg_ref in ((0, Q_LORA, gq_ref, dq_ref, dgq_ref),
                                             (Q_LORA, LAT, gkv_ref, dkv_ref, dgkv_ref)):
            xv, dhv = p_ref[:, lo:hi], d_ref[...]
            r = lax.rsqrt(jnp.mean(xv * xv, axis=-1, keepdims=True) + EPS)
            y = xv * r
            dy = dhv * g_ref[...]
            dp_ref[:, lo:hi] = r * (dy - y * jnp.mean(dy * y, axis=-1, keepdims=True))

            @pl.when(first)
            def _():
                dg_ref[...] = jnp.zeros_like(dg_ref)

            dg_ref[...] += jnp.sum(dhv * y, axis=0, keepdims=True)

    vq = pl.BlockSpec((1, Q_LORA), lambda i: (0, 0))
    vkv = pl.BlockSpec((1, KV_LORA), lambda i: (0, 0))
    dp, dgq, dgkv = pl.pallas_call(
        body, name=name, grid=(t // tm,),
        in_specs=[pl.BlockSpec((tm, P_COLS), lambda i: (i, 0)), vq, vkv,
                  pl.BlockSpec((tm, Q_LORA), lambda i: (i, 0)), pl.BlockSpec((tm, KV_LORA), lambda i: (i, 0))],
        out_specs=(pl.BlockSpec((tm, LAT), lambda i: (i, 0)), vq, vkv),
        out_shape=(jax.ShapeDtypeStruct((t, LAT), F32), jax.ShapeDtypeStruct((1, Q_LORA), F32),
                   jax.ShapeDtypeStruct((1, KV_LORA), F32)),
        compiler_params=_params("arbitrary"),
    )(p, g_q.reshape(1, Q_LORA), g_kv.reshape(1, KV_LORA), dq, dkv)
    return dp, dgq.reshape(Q_LORA), dgkv.reshape(KV_LORA)


def _rope_tables(t):
    half = QK_ROPE // 2
    pos = jnp.arange(t, dtype=F32)
    inv_freq = 1.0 / (ROPE_THETA ** (jnp.arange(0, QK_ROPE, 2, dtype=F32) / QK_ROPE))
    ang = pos[:, None] * inv_freq[None, :]
    cos, sin = jnp.cos(ang), jnp.sin(ang)
    z = lambda n: jnp.zeros((t, n), F32)
    c_tab = jnp.concatenate([jnp.ones((t, QK_NOPE), F32), cos, cos, z(HEAD_PAD - QK_DIM)], axis=1)
    sa_tab = jnp.concatenate([z(QK_NOPE), -sin, z(half), z(HEAD_PAD - QK_DIM)], axis=1)
    sb_tab = jnp.concatenate([z(QK_NOPE), z(half), sin, z(HEAD_PAD - QK_DIM)], axis=1)
    return c_tab, sa_tab, sb_tab


def _rope(x, c, sa, sb):
    half = QK_ROPE // 2
    return x * c + pltpu.roll(x, HEAD_PAD - half, 1) * sa + pltpu.roll(x, half, 1) * sb


def _rope_t(d, c, sa, sb):
    half = QK_ROPE // 2
    return d * c + pltpu.roll(d * sa, half, 1) + pltpu.roll(d * sb, HEAD_PAD - half, 1)


def _head_rms(x):
    r = lax.rsqrt(jnp.sum(x * x, axis=-1, keepdims=True) * (1.0 / QK_DIM) + EPS)
    return x * r, r


def _qk_prep_fwd(q_raw, k_raw, p, gq, gk, tabs, name):
    t = q_raw.shape[0]
    tm = _row_tile(t)

    def body(q_ref, k_ref, p_ref, gq_ref, gk_ref, c_ref, sa_ref, sb_ref, qo_ref, ko_ref):
        c, sa, sb = c_ref[...], sa_ref[...], sb_ref[...]
        qn, _ = _head_rms(q_ref[...])
        qo_ref[...] = _rope(qn * gq_ref[...], c, sa, sb).astype(qo_ref.dtype)
        kn, _ = _head_rms(k_ref[...] + p_ref[...])
        ko_ref[...] = _rope(kn * gk_ref[...], c, sa, sb).astype(ko_ref.dtype)

    head = pl.BlockSpec((tm, HEAD_PAD), lambda i, h: (i, h))
    tab = pl.BlockSpec((tm, HEAD_PAD), lambda i, h: (i, 0))
    vec = pl.BlockSpec((1, HEAD_PAD), lambda i, h: (0, 0))
    kpe = pl.BlockSpec((tm, HEAD_PAD), lambda i, h: (i, P_KPE // HEAD_PAD))
    return pl.pallas_call(
        body, name=name, grid=(t // tm, N_HEADS), in_specs=[head, head, kpe, vec, vec, tab, tab, tab],
        out_specs=(head, head),
        out_shape=(jax.ShapeDtypeStruct(q_raw.shape, BF16), jax.ShapeDtypeStruct(k_raw.shape, BF16)),
        compiler_params=_params("parallel", "parallel"),
    )(q_raw, k_raw, p, gq.reshape(1, HEAD_PAD), gk.reshape(1, HEAD_PAD), *tabs)


def _qk_prep_bwd(q_raw, k_raw, p, dq, dk, gq, gk, tabs, name):
    t = q_raw.shape[0]
    tm = _row_tile(t)

    def body(q_ref, k_ref, p_ref, dq_ref, dk_ref, gq_ref, gk_ref, c_ref, sa_ref, sb_ref,
             dqr_ref, dkr_ref, dkpe_ref, dgq_ref, dgk_ref):
        i, h = pl.program_id(0), pl.program_id(1)
        c, sa, sb = c_ref[...], sa_ref[...], sb_ref[...]

        def one(x, d, g_ref, dg_ref):
            n, r = _head_rms(x)
            dng = _rope_t(d, c, sa, sb)

            @pl.when(jnp.logical_and(i == 0, h == 0))
            def _():
                dg_ref[...] = jnp.zeros_like(dg_ref)

            dg_ref[...] += jnp.sum(dng * n, axis=0, keepdims=True)
            dn = dng * g_ref[...]
            return r * (dn - n * (jnp.sum(dn * n, axis=-1, keepdims=True) * (1.0 / QK_DIM)))

        dqr_ref[...] = one(q_ref[...], dq_ref[...], gq_ref, dgq_ref)
        dkr = one(k_ref[...] + p_ref[...], dk_ref[...], gk_ref, dgk_ref)
        dkr_ref[...] = dkr

        @pl.when(h == 0)
        def _():
            dkpe_ref[...] = dkr

        @pl.when(h > 0)
        def _():
            dkpe_ref[...] += dkr

    head = pl.BlockSpec((tm, HEAD_PAD), lambda i, h: (i, h))
    tab = pl.BlockSpec((tm, HEAD_PAD), lambda i, h: (i, 0))
    vec = pl.BlockSpec((1, HEAD_PAD), lambda i, h: (0, 0))
    kpe = pl.BlockSpec((tm, HEAD_PAD), lambda i, h: (i, P_KPE // HEAD_PAD))
    dqr, dkr, dkpe, dgq, dgk = pl.pallas_call(
        body, name=name, grid=(t // tm, N_HEADS), in_specs=[head, head, kpe, head, head, vec, vec, tab, tab, tab],
        out_specs=(head, head, tab, vec, vec),
        out_shape=(jax.ShapeDtypeStruct(q_raw.shape, F32), jax.ShapeDtypeStruct(k_raw.shape, F32),
                   jax.ShapeDtypeStruct((t, HEAD_PAD), F32), jax.ShapeDtypeStruct((1, HEAD_PAD), F32),
                   jax.ShapeDtypeStruct((1, HEAD_PAD), F32)),
        compiler_params=_params("arbitrary", "arbitrary"),
    )(q_raw, k_raw, p, dq, dk, gq.reshape(1, HEAD_PAD), gk.reshape(1, HEAD_PAD), *tabs)
    return dqr, dkr, dkpe, dgq.reshape(HEAD_PAD), dgk.reshape(HEAD_PAD)


def _dot_nt(a, b):
    return lax.dot_general(a, b, (((1,), (1,)), ((), ())), preferred_element_type=F32)


def _dot_tn(a, b):
    return lax.dot_general(a, b, (((0,), (0,)), ((), ())), preferred_element_type=F32)


def _diag_mask():
    rows = lax.broadcasted_iota(jnp.int32, (ATTN_BLOCK, ATTN_BLOCK), 0) // CHUNK
    cols = lax.broadcasted_iota(jnp.int32, (ATTN_BLOCK, ATTN_BLOCK), 1) // CHUNK
    return cols <= rows


def _attn_fwd(q, k, v, name):
    t = q.shape[0]
    bq = ATTN_BLOCK
    nq = t // bq

    def body(q_ref, k_ref, v_ref, o_ref, lse_ref):
        i = pl.program_id(1)
        qv = q_ref[...]

        def block(j, carry, masked):
            m, l, acc = carry
            rows = pl.ds(pl.multiple_of(j * bq, bq), bq)
            s = _dot_nt(qv, k_ref[rows, :]) * ATTN_SCALE
            if masked:
                s = jnp.where(_diag_mask(), s, -1e30)
            m_new = jnp.maximum(m, jnp.max(s, axis=-1, keepdims=True))
            alpha = jnp.exp(m - m_new)
            pe = jnp.exp(s - m_new)
            l = alpha * l + jnp.sum(pe, axis=-1, keepdims=True)
            acc = alpha * acc + jnp.dot(pe.astype(BF16), v_ref[rows, :], preferred_element_type=F32)
            return m_new, l, acc

        init = (jnp.full((bq, 1), -1e30, F32), jnp.zeros((bq, 1), F32), jnp.zeros((bq, HEAD_PAD), F32))
        carry = lax.fori_loop(0, i, lambda j, cr: block(j, cr, False), init)
        m, l, acc = block(i, carry, True)
        o_ref[...] = acc / l
        lse_ref[...] = jnp.broadcast_to(m + jnp.log(l), (bq, HEAD_PAD))

    blk = pl.BlockSpec((bq, HEAD_PAD), lambda h, i: (i, h))
    full = pl.BlockSpec((t, HEAD_PAD), lambda h, i: (0, h))
    return pl.pallas_call(
        body, name=name, grid=(N_HEADS, nq), in_specs=[blk, full, full], out_specs=(blk, blk),
        out_shape=(jax.ShapeDtypeStruct(q.shape, F32), jax.ShapeDtypeStruct(q.shape, F32)),
        compiler_params=_params("parallel", "parallel"),
    )(q, k, v)


def _attn_bwd(q, k, v, o, lse, do, name):
    t = q.shape[0]
    bq = ATTN_BLOCK
    nq = t // bq

    def body(q_ref, k_ref, v_ref, o_ref, lse_ref, do_ref, dq_ref, dk_ref, dv_ref, delta_ref):
        def rows_of(i):
            return pl.ds(pl.multiple_of(i * bq, bq), bq)

        def prep(i, _):
            r = rows_of(i)
            delta_ref[r, :] = jnp.broadcast_to(jnp.sum(do_ref[r, :] * o_ref[r, :], axis=-1, keepdims=True),
                                               (bq, HEAD_PAD))
            dq_ref[r, :] = jnp.zeros((bq, HEAD_PAD), F32)
            return 0

        lax.fori_loop(0, nq, prep, 0)

        def key_block(j, _):
            rj = rows_of(j)
            kb, vb = k_ref[rj, :], v_ref[rj, :]

            def query_block(i, carry, masked):
                dk, dv = carry
                ri = rows_of(i)
                qb, dob = q_ref[ri, :], do_ref[ri, :].astype(BF16)
                s = _dot_nt(qb, kb) * ATTN_SCALE
                if masked:
                    s = jnp.where(_diag_mask(), s, -1e30)
                pe = jnp.exp(s - lse_ref[ri, :][:, :1])
                dp = _dot_nt(dob, vb)
                ds = (pe * (dp - delta_ref[ri, :][:, :1]) * ATTN_SCALE).astype(BF16)
                dq_ref[ri, :] += jnp.dot(ds, kb, preferred_element_type=F32)
                return dk + _dot_tn(ds, qb), dv + _dot_tn(pe.astype(BF16), dob)

            zero = jnp.zeros((bq, HEAD_PAD), F32)
            carry = query_block(j, (zero, zero), True)
            dk, dv = lax.fori_loop(j + 1, nq, lambda i, cr: query_block(i, cr, False), carry)
            dk_ref[rj, :] = dk
            dv_ref[rj, :] = dv
            return 0

        lax.fori_loop(0, nq, key_block, 0)

    full = pl.BlockSpec((t, HEAD_PAD), lambda h: (0, h))
    shp = jax.ShapeDtypeStruct(q.shape, F32)
    return pl.pallas_call(
        body, name=name, grid=(N_HEADS,), in_specs=[full] * 6, out_specs=(full, full, full),
        out_shape=(shp, shp, shp), scratch_shapes=[pltpu.VMEM((t, HEAD_PAD), F32)],
        compiler_params=_params("parallel"),
    )(q, k, v, o, lse, do)


def _glu_ext(pc_ref, pp_ref, u_ref, tm, first):
    u_ref[CONV_HALO:CONV_HALO + tm, :] = pc_ref[:, P_A:P_G] * jax.nn.sigmoid(pc_ref[:, P_G:P_COLS])
    up = pp_ref[tm - CONV_HALO:tm, P_A:P_G] * jax.nn.sigmoid(pp_ref[tm - CONV_HALO:tm, P_G:P_COLS])
    u_ref[0:CONV_HALO, :] = jnp.where(first, 0.0, up)


def _conv_fwd(p, w, b, ln_g, ln_b, name):
    t = p.shape[0]
    tm = _row_tile(t, CONV_TILE)
    off = CONV_HALO - (CONV_K - 1)

    def body(pc_ref, pp_ref, w_ref, b_ref, g_ref, bb_ref, y_ref, o_ref, u_ref):
        _glu_ext(pc_ref, pp_ref, u_ref, tm, pl.program_id(0) == 0)
        acc = jnp.zeros((tm, CONV_W), F32)
        for kk in range(CONV_K):
            acc = acc + w_ref[kk:kk + 1, :] * u_ref[off + kk:off + kk + tm, :]
        y = acc + b_ref[...]
        y_ref[...] = y
        xc = y - jnp.mean(y, axis=-1, keepdims=True)
        lo = xc * lax.rsqrt(jnp.mean(xc * xc, axis=-1, keepdims=True) + EPS) * g_ref[...] + bb_ref[...]
        o_ref[...] = (lo * jax.nn.sigmoid(lo)).astype(o_ref.dtype)

    prow = pl.BlockSpec((tm, P_COLS), lambda i: (i, 0))
    pprev = pl.BlockSpec((tm, P_COLS), lambda i: (jnp.maximum(i - 1, 0), 0))
    vec = pl.BlockSpec((1, CONV_W), lambda i: (0, 0))
    row = pl.BlockSpec((tm, CONV_W), lambda i: (i, 0))
    return pl.pallas_call(
        body, name=name, grid=(t // tm,),
        in_specs=[prow, pprev, pl.BlockSpec((CONV_HALO, CONV_W), lambda i: (0, 0)), vec, vec, vec],
        out_specs=(row, row),
        out_shape=(jax.ShapeDtypeStruct((t, CONV_W), F32), jax.ShapeDtypeStruct((t, CONV_W), BF16)),
        scratch_shapes=[pltpu.VMEM((tm + CONV_HALO, CONV_W), F32)], compiler_params=_params("parallel"),
    )(p, p, w, b.reshape(1, CONV_W), ln_g.reshape(1, CONV_W), ln_b.reshape(1, CONV_W))


def _conv_bwd_ln(y, dout, ln_g, ln_b, name):
    t = y.shape[0]
    tm = _row_tile(t)

    def body(y_ref, d_ref, g_ref, bb_ref, dy_ref, dg_ref, db_ref, dcb_ref):
        yv = y_ref[...]
        xc = yv - jnp.mean(yv, axis=-1, keepdims=True)
        r = lax.rsqrt(jnp.mean(xc * xc, axis=-1, keepdims=True) + EPS)
        n = xc * r
        lo = n * g_ref[...] + bb_ref[...]
        s = jax.nn.sigmoid(lo)
        dlo = d_ref[...] * (s * (1.0 + lo * (1.0 - s)))
        dn = dlo * g_ref[...]
        dy = r * (dn - jnp.mean(dn, axis=-1, keepdims=True) - n * jnp.mean(dn * n, axis=-1, keepdims=True))
        dy_ref[...] = dy

        @pl.when(pl.program_id(0) == 0)
        def _():
            dg_ref[...] = jnp.zeros_like(dg_ref)
            db_ref[...] = jnp.zeros_like(db_ref)
            dcb_ref[...] = jnp.zeros_like(dcb_ref)

        dg_ref[...] += jnp.sum(dlo * n, axis=0, keepdims=True)
        db_ref[...] += jnp.sum(dlo, axis=0, keepdims=True)
        dcb_ref[...] += jnp.sum(dy, axis=0, keepdims=True)

    row = pl.BlockSpec((tm, CONV_W), lambda i: (i, 0))
    vec = pl.BlockSpec((1, CONV_W), lambda i: (0, 0))
    vshape = jax.ShapeDtypeStruct((1, CONV_W), F32)
    dy, dg, db, dcb = pl.pallas_call(
        body, name=name, grid=(t // tm,), in_specs=[row, row, vec, vec], out_specs=(row, vec, vec, vec),
        out_shape=(jax.ShapeDtypeStruct((t, CONV_W), F32), vshape, vshape, vshape),
        compiler_params=_params("arbitrary"),
    )(y, dout, ln_g.reshape(1, CONV_W), ln_b.reshape(1, CONV_W))
    return dy, dg.reshape(CONV_W), db.reshape(CONV_W), dcb.reshape(CONV_W)


def _conv_bwd_taps(p, dy, w, name):
    t = p.shape[0]
    tm = _row_tile(t, CONV_TILE)
    nt = t // tm
    off = CONV_HALO - (CONV_K - 1)

    def body(pc_ref, pp_ref, dyc_ref, dyn_ref, w_ref, dag_ref, dw_ref, u_ref, dye_ref):
        i = pl.program_id(0)
        _glu_ext(pc_ref, pp_ref, u_ref, tm, i == 0)
        dyc = dyc_ref[...]
        dye_ref[0:tm, :] = dyc
        dye_ref[tm:tm + CONV_HALO, :] = jnp.where(i == nt - 1, 0.0, dyn_ref[0:CONV_HALO, :])

        @pl.when(i == 0)
        def _():
            dw_ref[...] = jnp.zeros_like(dw_ref)

        du = jnp.zeros((tm, CONV_W), F32)
        for kk in range(CONV_K):
            dw_ref[kk:kk + 1, :] += jnp.sum(dyc * u_ref[off + kk:off + kk + tm, :], axis=0, keepdims=True)
            back = CONV_K - 1 - kk
            du = du + w_ref[kk:kk + 1, :] * dye_ref[back:back + tm, :]
        av, gv = pc_ref[:, P_A:P_G], pc_ref[:, P_G:P_COLS]
        s = jax.nn.sigmoid(gv)
        dag_ref[:, 0:CONV_W] = du * s
        dag_ref[:, CONV_W:2 * CONV_W] = du * av * (s * (1.0 - s))

    prow = pl.BlockSpec((tm, P_COLS), lambda i: (i, 0))
    pprev = pl.BlockSpec((tm, P_COLS), lambda i: (jnp.maximum(i - 1, 0), 0))
    row = pl.BlockSpec((tm, CONV_W), lambda i: (i, 0))
    nxt = pl.BlockSpec((tm, CONV_W), lambda i: (jnp.minimum(i + 1, nt - 1), 0))
    wspec = pl.BlockSpec((CONV_HALO, CONV_W), lambda i: (0, 0))
    return pl.pallas_call(
        body, name=name, grid=(nt,), in_specs=[prow, pprev, row, nxt, wspec],
        out_specs=(pl.BlockSpec((tm, 2 * CONV_W), lambda i: (i, 0)), wspec),
        out_shape=(jax.ShapeDtypeStruct((t, 2 * CONV_W), F32), jax.ShapeDtypeStruct((CONV_HALO, CONV_W), F32)),
        scratch_shapes=[pltpu.VMEM((tm + CONV_HALO, CONV_W), F32), pltpu.VMEM((tm + CONV_HALO, CONV_W), F32)],
        compiler_params=_params("arbitrary"),
    )(p, p, dy, dy, w)


def _loss_head(y, target, name):
    t, d = y.shape
    tm = _row_tile(t)

    def body(y_ref, t_ref, l_ref, dy_ref):
        err = y_ref[...] - t_ref[...]
        dy_ref[...] = err * (1.0 / d)

        @pl.when(pl.program_id(0) == 0)
        def _():
            l_ref[...] = jnp.zeros_like(l_ref)

        row = jnp.sum(err * err, axis=-1, keepdims=True) * (0.5 / d)
        l_ref[...] += jnp.broadcast_to(jnp.sum(row, axis=0, keepdims=True), (1, LANE))

    row = pl.BlockSpec((tm, d), lambda i: (i, 0))
    return pl.pallas_call(
        body, name=name, grid=(t // tm,), in_specs=[row, row],
        out_specs=(pl.BlockSpec((1, LANE), lambda i: (0, 0)), row),
        out_shape=(jax.ShapeDtypeStruct((1, LANE), F32), jax.ShapeDtypeStruct((t, d), F32)),
        compiler_params=_params("arbitrary"),
    )(y, target)


def _adamw(w, g, m, v, name):
    r, c = w.shape
    tr = _row_tile(r, 256)
    c1, c2 = 1.0 - ADAM_B1 ** ADAM_STEP, 1.0 - ADAM_B2 ** ADAM_STEP

    def body(w_ref, g_ref, m_ref, v_ref, d_ref, mo_ref, vo_ref):
        gv = g_ref[...]
        mn = ADAM_B1 * m_ref[...] + (1.0 - ADAM_B1) * gv
        vn = ADAM_B2 * v_ref[...] + (1.0 - ADAM_B2) * (gv * gv)
        mo_ref[...] = mn
        vo_ref[...] = vn
        d_ref[...] = -ADAM_LR * ((mn / c1) / (jnp.sqrt(vn / c2) + ADAM_EPS) + ADAM_WD * w_ref[...])

    blk = pl.BlockSpec((tr, c), lambda i: (i, 0))
    shp = jax.ShapeDtypeStruct((r, c), F32)
    return pl.pallas_call(
        body, name=name, grid=(r // tr,), in_specs=[blk] * 4, out_specs=(blk, blk, blk), out_shape=(shp, shp, shp),
        compiler_params=_params("parallel"),
    )(w, g, m, v)


def _sum_parts(parts, name):
    r, c = parts[0].shape
    tr = _row_tile(r, 256)

    def body(*refs):
        acc = refs[0][...]
        for ref in refs[1:-1]:
            acc = acc + ref[...]
        refs[-1][...] = acc

    blk = pl.BlockSpec((tr, c), lambda i: (i, 0))
    return pl.pallas_call(
        body, name=name, grid=(r // tr,), in_specs=[blk] * len(parts), out_specs=blk,
        out_shape=jax.ShapeDtypeStruct((r, c), F32), compiler_params=_params("parallel"),
    )(*parts)


def _place():
    return lax.axis_index("x"), lax.axis_index("y"), lax.axis_index("c")


def _window(ref, block, size, axis):
    start = pl.multiple_of(block * size, LANE if size % LANE == 0 else 8)
    return ref.at[(slice(None),) * axis + (pl.ds(start, size),)]


def _all_gather(pieces, name, in_vmem=False):
    n_p = len(pieces)

    def body(*refs):
        x_refs, out_refs = refs[:n_p], refs[n_p:2 * n_p]
        send_sems, recv_sems, local_sems = refs[2 * n_p:]
        px, py, pc = _place()
        me, sibling = (px, py, pc), (px, py, 1 - pc)
        chips = [(1 - px, py), (px, 1 - py), (1 - px, 1 - py)]

        def win(p, block):
            bx, by, bc = block
            x, axis = pieces[p]
            return _window(out_refs[p], 4 * bx + 2 * by + bc, x.shape[axis], axis)

        def copy(k, p, block, to, local=False):
            return pltpu.make_async_remote_copy(
                src_ref=x_refs[p] if local else win(p, block), dst_ref=win(p, block),
                send_sem=send_sems.at[k, p], recv_sem=recv_sems.at[k, p], device_id=to, device_id_type=MESH_ID)

        every = range(n_p)
        mine = [pltpu.make_async_copy(x_refs[p], win(p, me), local_sems.at[p]) for p in every]
        first = [copy(0, p, me, sibling, local=True) for p in every]
        first += [copy(1 + j, p, me, (*chip, pc), local=True) for j, chip in enumerate(chips) for p in every]
        for cp in mine + first:
            cp.start()
        passed = []
        for j, chip in enumerate(chips):
            for p in every:
                copy(1 + j, p, (*chip, pc), me).wait_recv()
                passed.append(copy(4 + j, p, (*chip, pc), sibling))
                passed[-1].start()
        for p in every:
            copy(0, p, sibling, me).wait_recv()
        for j, chip in enumerate(chips):
            for p in every:
                copy(4 + j, p, (*chip, 1 - pc), me).wait_recv()
        for cp in first + passed:
            cp.wait_send()
        for cp in mine:
            cp.wait()

    def gathered(x, axis):
        return jax.ShapeDtypeStruct(x.shape[:axis] + (N_DEV * x.shape[axis],) + x.shape[axis + 1:], x.dtype)

    spec = VMEM_SPEC if in_vmem else ANY
    return pl.pallas_call(
        body, name=name, in_specs=[spec] * n_p, out_specs=[spec] * n_p, out_shape=[gathered(*pc_) for pc_ in pieces],
        scratch_shapes=[pltpu.SemaphoreType.DMA((7, n_p)), pltpu.SemaphoreType.DMA((7, n_p)),
                        pltpu.SemaphoreType.DMA((n_p,))],
        compiler_params=pltpu.CompilerParams(vmem_limit_bytes=VMEM_LIMIT),
    )(*[x for x, _ in pieces])


def _other_chips():
    px, py, _ = _place()
    return [(1 - px, py), (px, 1 - py), (1 - px, 1 - py)]


def _exchange(srcs, slots, src_block, target, name):
    n_p = len(srcs)

    def body(*refs):
        src_refs, out_refs, send_sems, recv_sems = refs[:n_p], refs[n_p:2 * n_p], refs[-2], refs[-1]
        copies = [pltpu.make_async_remote_copy(
            src_ref=src_refs[p].at[src_block(s)], dst_ref=out_refs[p].at[s], send_sem=send_sems.at[s, p],
            recv_sem=recv_sems.at[s, p], device_id=target(s), device_id_type=MESH_ID)
            for s in range(slots) for p in range(n_p)]
        for cp in copies:
            cp.start()
        for cp in copies:
            cp.wait_recv()
        for cp in copies:
            cp.wait_send()

    return pl.pallas_call(
        body, name=name, in_specs=[ANY] * n_p, out_specs=[ANY] * n_p,
        out_shape=[jax.ShapeDtypeStruct((slots,) + a.shape[1:], a.dtype) for a in srcs],
        scratch_shapes=[pltpu.SemaphoreType.DMA((slots, n_p)), pltpu.SemaphoreType.DMA((slots, n_p))],
        compiler_params=pltpu.CompilerParams(vmem_limit_bytes=VMEM_LIMIT),
    )(*srcs)


def _blocks_to_sibling(sends, name):
    def src_block(j):
        return 2 * j + 1 - lax.axis_index("c")

    def target(j):
        px, py, pc = _place()
        return (px, py, 1 - pc)

    return _exchange(sends, 4, src_block, target, name)


def _parts_to_chips(parts, name):
    def target(k):
        cx, cy = _other_chips()[k]
        return (cx, cy, lax.axis_index("c"))

    return _exchange(parts, 3, lambda k: k, target, name)


def _pair_sums_for_chips(own, got, name):
    _, r, c = own.shape
    tr = _row_tile(r, 256, 16)

    def body(idx_ref, own_ref, got_ref, o_ref):
        o_ref[...] = (own_ref[...] + got_ref[...].astype(F32)).astype(o_ref.dtype)

    grid_spec = pltpu.PrefetchScalarGridSpec(
        num_scalar_prefetch=1, grid=(3, r // tr),
        in_specs=[pl.BlockSpec((None, tr, c), lambda k, i, idx: (idx[k], i, 0)),
                  pl.BlockSpec((None, tr, c), lambda k, i, idx: (idx[3 + k], i, 0))],
        out_specs=pl.BlockSpec((None, tr, c), lambda k, i, idx: (k, i, 0)))
    chips = [2 * cx + cy for cx, cy in _other_chips()]
    idx = jnp.stack([2 * j + lax.axis_index("c") for j in chips] + chips).astype(jnp.int32)
    return pl.pallas_call(
        body, name=name, grid_spec=grid_spec, out_shape=jax.ShapeDtypeStruct((3, r, c), BF16),
        compiler_params=_params("parallel", "parallel"),
    )(idx, own, got)


def _sum_for_me(own, got_sibling, got_chips, name):
    _, r, c = own.shape
    tr = _row_tile(r, 256, 16)

    def body(idx_ref, own_ref, sib_ref, g0_ref, g1_ref, g2_ref, o_ref):
        acc = own_ref[...] + sib_ref[...].astype(F32)
        for ref in (g0_ref, g1_ref, g2_ref):
            acc = acc + ref[...].astype(F32)
        o_ref[...] = acc

    def part(k):
        return pl.BlockSpec((None, tr, c), lambda i, idx: (k, i, 0))

    grid_spec = pltpu.PrefetchScalarGridSpec(
        num_scalar_prefetch=1, grid=(r // tr,),
        in_specs=[pl.BlockSpec((None, tr, c), lambda i, idx: (idx[0], i, 0)),
                  pl.BlockSpec((None, tr, c), lambda i, idx: (idx[1], i, 0)), part(0), part(1), part(2)],
        out_specs=pl.BlockSpec((tr, c), lambda i, idx: (i, 0)))
    px, py, pc = _place()
    idx = jnp.stack([4 * px + 2 * py + pc, 2 * px + py]).astype(jnp.int32)
    return pl.pallas_call(
        body, name=name, grid_spec=grid_spec, out_shape=jax.ShapeDtypeStruct((r, c), F32),
        compiler_params=_params("parallel"),
    )(idx, own, got_sibling, got_chips, got_chips, got_chips)


def _flat_rows(a, width):
    return a.reshape(-1, width)


def _full_from_blocks(blocks, name):
    if name in COL_SHARDED:
        _, l, k, nb = blocks.shape
        return jnp.transpose(blocks, (1, 2, 0, 3)).reshape(l, k, N_DEV * nb)
    _, l, rb, n = blocks.shape
    return jnp.transpose(blocks, (1, 0, 2, 3)).reshape(l, N_DEV * rb, n)


def _blocks_from_full(full, name):
    if name in COL_SHARDED:
        l, k, n = full.shape
        return jnp.transpose(full.reshape(l, k, N_DEV, n // N_DEV), (2, 0, 1, 3))
    l, rows, n = full.shape
    return jnp.transpose(full.reshape(l, N_DEV, rows // N_DEV, n), (1, 0, 2, 3))


def _pad_heads(w, width):
    k = w.shape[0]
    return jnp.pad(w.reshape(k, N_HEADS, width), ((0, 0), (0, 0), (0, HEAD_PAD - width))).reshape(k, N_HEADS * HEAD_PAD)


def _unpad_heads(w, width):
    k = w.shape[0]
    return w.reshape(k, N_HEADS, HEAD_PAD)[:, :, :width].reshape(k, N_HEADS * width)


def _layer_operands(full, vec, conv_w_full, l):
    w_in = full['w_in'][l]
    kpe = jnp.pad(w_in[:, LAT:LAT + QK_ROPE], ((0, 0), (QK_NOPE, HEAD_PAD - QK_DIM)))
    w_ukv = full['w_ukv'][l].reshape(KV_LORA, N_HEADS, QK_NOPE + V_DIM)
    w_out = full['w_out'][l]
    d_model = w_out.shape[1]
    wo_attn = jnp.pad(w_out[:N_HEADS * V_DIM].reshape(N_HEADS, V_DIM, d_model),
                      ((0, 0), (0, HEAD_PAD - V_DIM), (0, 0))).reshape(N_HEADS * HEAD_PAD, d_model)
    ops = {
        'w_in': jnp.concatenate([w_in[:, :LAT], kpe, w_in[:, LAT + QK_ROPE:]], axis=1),
        'w_q': _pad_heads(full['w_uq'][l], QK_DIM),
        'w_k': _pad_heads(w_ukv[:, :, :QK_NOPE].reshape(KV_LORA, N_HEADS * QK_NOPE), QK_NOPE),
        'w_v': _pad_heads(w_ukv[:, :, QK_NOPE:].reshape(KV_LORA, N_HEADS * V_DIM), V_DIM),
        'wo_attn': wo_attn,
        'wo_conv': w_out[N_HEADS * V_DIM:],
        'conv_w': jnp.pad(conv_w_full[l], ((0, CONV_HALO - CONV_K), (0, 0))),
        'gq': jnp.pad(vec['q_norm'][l], (0, HEAD_PAD - QK_DIM)),
        'gk': jnp.pad(vec['k_norm'][l], (0, HEAD_PAD - QK_DIM)),
    }
    for n in ('ffn1_norm', 'mix_norm', 'q_latent_norm', 'kv_latent_norm', 'conv_b', 'conv_ln_g', 'conv_ln_b',
              'ffn2_norm', 'post_norm'):
        ops[n] = vec[n][l]
    return ops


def _ffn_fwd(x, g, wgu, wd, fp):
    h = _rms_fwd(x, g, BF16, "rms_fwd_ffn")
    ab, z = _ffn_up(h, wgu, fp, "ffn_up")
    y = _mm(z, wd, res=x, scale=0.5, name="ffn_down")
    return y, (x, h, ab, z)


def _ffn_bwd(dy, dyb, saved, g, wgu, wd, fp):
    x, h, ab, z = saved
    d_wd = _mm(z, dyb, ta=True, scale=0.5, blocks=('row', N_DEV), name="ffn_dwd")
    dab = _ffn_dab(dyb, wd, ab, fp, "ffn_dab")
    d_wgu = _mm(h, dab, ta=True, blocks=('col', N_DEV), name="ffn_dwgu")
    dh = _mm(dab, wgu, tb=True, name="ffn_dh")
    dx, dxb, dg = _rms_bwd(x, g, dh, dy, "rms_bwd_ffn")
    return dx, dxb, dg, d_wgu, d_wd


def _mixer_fwd(x, ops, tabs):
    h = _rms_fwd(x, ops['mix_norm'], BF16, "rms_fwd_mix")
    p = _mm(h, ops['w_in'], name="mix_in")
    qln, kvln = _lat_norm_fwd(p, ops['q_latent_norm'], ops['kv_latent_norm'], "lat_norm_fwd")
    q_raw = _mm(qln, ops['w_q'], name="mix_q")
    k_raw = _mm(kvln, ops['w_k'], name="mix_k")
    v = _mm(kvln, ops['w_v'], out_dtype=BF16, name="mix_v")
    q, k = _qk_prep_fwd(q_raw, k_raw, p, ops['gq'], ops['gk'], tabs, "qk_prep_fwd")
    o, lse = _attn_fwd(q, k, v, "attn_fwd")
    y_conv, cv = _conv_fwd(p, ops['conv_w'], ops['conv_b'], ops['conv_ln_g'], ops['conv_ln_b'], "conv_fwd")
    x_attn = _mm(o, ops['wo_attn'], res=x, name="mix_out_attn")
    x_out = _mm(cv, ops['wo_conv'], res=x_attn, name="mix_out_conv")
    return x_out, (x, h, p, qln, kvln, q_raw, k_raw, v, q, k, o, lse, y_conv, cv)


def _mixer_bwd(dx_out, dxb_out, saved, ops, tabs):
    x, h, p, qln, kvln, q_raw, k_raw, v, q, k, o, lse, y_conv, cv = saved
    g = {}
    do = _mm(dxb_out, ops['wo_attn'], tb=True, name="mix_do")
    dcv = _mm(dxb_out, ops['wo_conv'], tb=True, name="mix_dcv")
    g['wo_attn'] = _mm(o, dxb_out, ta=True, name="mix_dwo_attn")
    g['wo_conv'] = _mm(cv, dxb_out, ta=True, name="mix_dwo_conv")
    dq, dk, dv = _attn_bwd(q, k, v, o, lse, do, "attn_bwd")
    dq_raw, dk_raw, dkpe, g['gq'], g['gk'] = _qk_prep_bwd(q_raw, k_raw, p, dq, dk, ops['gq'], ops['gk'], tabs,
                                                          "qk_prep_bwd")
    g['w_q'] = _mm(qln, dq_raw, ta=True, name="mix_dwq")
    g['w_k'] = _mm(kvln, dk_raw, ta=True, name="mix_dwk")
    g['w_v'] = _mm(kvln, dv, ta=True, name="mix_dwv")
    dqln = _mm(dq_raw, ops['w_q'], tb=True, name="mix_dqln")
    dkvln = _mm(dk_raw, ops['w_k'], tb=True, name="mix_dkvln_k")
    dkvln = _mm(dv, ops['w_v'], tb=True, res=dkvln, name="mix_dkvln_v")
    dp_lat, g['q_latent_norm'], g['kv_latent_norm'] = _lat_norm_bwd(
        p, ops['q_latent_norm'], ops['kv_latent_norm'], dqln, dkvln, "lat_norm_bwd")
    dy_conv, g['conv_ln_g'], g['conv_ln_b'], g['conv_b'] = _conv_bwd_ln(
        y_conv, dcv, ops['conv_ln_g'], ops['conv_ln_b'], "conv_bwd_ln")
    dag, g['conv_w'] = _conv_bwd_taps(p, dy_conv, ops['conv_w'], "conv_bwd_taps")
    dp = jnp.concatenate([dp_lat, dkpe, dag], axis=1)
    g['w_in'] = _mm(h, dp, ta=True, name="mix_dw_in")
    dh = _mm(dp, ops['w_in'], tb=True, name="mix_dh")
    dx, dxb, g['mix_norm'] = _rms_bwd(x, ops['mix_norm'], dh, dx_out, "rms_bwd_mix")
    return dx, dxb, g


def _layer_grads_to_params(g):
    d_w_in = g['w_in']
    d_wk = _unpad_heads(g['w_k'], QK_NOPE).reshape(KV_LORA, N_HEADS, QK_NOPE)
    d_wv = _unpad_heads(g['w_v'], V_DIM).reshape(KV_LORA, N_HEADS, V_DIM)
    d_model = g['wo_attn'].shape[1]
    d_wo_attn = g['wo_attn'].reshape(N_HEADS, HEAD_PAD, d_model)[:, :V_DIM].reshape(N_HEADS * V_DIM, d_model)
    return {
        'ffn1_norm': g['ffn1_norm'], 'mix_norm': g['mix_norm'],
        'w_in': jnp.concatenate([d_w_in[:, :LAT], d_w_in[:, LAT + QK_NOPE:LAT + QK_DIM], d_w_in[:, P_A:]], axis=1),
        'q_latent_norm': g['q_latent_norm'], 'w_uq': _unpad_heads(g['w_q'], QK_DIM),
        'kv_latent_norm': g['kv_latent_norm'],
        'w_ukv': jnp.concatenate([d_wk, d_wv], axis=2).reshape(KV_LORA, N_HEADS * (QK_NOPE + V_DIM)),
        'q_norm': g['gq'][:QK_DIM], 'k_norm': g['gk'][:QK_DIM], 'conv_w': g['conv_w'][:CONV_K],
        'conv_b': g['conv_b'], 'conv_ln_g': g['conv_ln_g'], 'conv_ln_b': g['conv_ln_b'],
        'w_out': jnp.concatenate([d_wo_attn, g['wo_conv']], axis=0),
        'ffn2_norm': g['ffn2_norm'], 'post_norm': g['post_norm'],
    }


def kernel(x, ffn1_norm, ffn1_w_gate, ffn1_w_up, ffn1_w_down, mix_norm, w_in, q_latent_norm, w_uq, kv_latent_norm, w_ukv, q_norm, k_norm, conv_w, conv_b, conv_ln_g, conv_ln_b, w_out, ffn2_norm, ffn2_w_gate, ffn2_w_up, ffn2_w_down, post_norm, loss_target, m_ffn1_norm, m_ffn1_w_gate, m_ffn1_w_up, m_ffn1_w_down, m_mix_norm, m_w_in, m_q_latent_norm, m_w_uq, m_kv_latent_norm, m_w_ukv, m_q_norm, m_k_norm, m_conv_w, m_conv_b, m_conv_ln_g, m_conv_ln_b, m_w_out, m_ffn2_norm, m_ffn2_w_gate, m_ffn2_w_up, m_ffn2_w_down, m_post_norm, v_ffn1_norm, v_ffn1_w_gate, v_ffn1_w_up, v_ffn1_w_down, v_mix_norm, v_w_in, v_q_latent_norm, v_w_uq, v_kv_latent_norm, v_w_ukv, v_q_norm, v_k_norm, v_conv_w, v_conv_b, v_conv_ln_g, v_conv_ln_b, v_w_out, v_ffn2_norm, v_ffn2_w_gate, v_ffn2_w_up, v_ffn2_w_down, v_post_norm):
    args = locals()
    w = {n: args[n] for n in WEIGHTS}
    mom = {n: args["m_" + n] for n in WEIGHTS}
    var = {n: args["v_" + n] for n in WEIGHTS}
    depth = ffn1_norm.shape[0]
    x0 = x.reshape(x.shape[-2:])
    target = loss_target.reshape(loss_target.shape[-2:])
    t, d_model = x0.shape
    my_block = 4 * lax.axis_index("x") + 2 * lax.axis_index("y") + lax.axis_index("c")

    fb = ffn1_w_gate.shape[-1]
    fp = -(-fb // LANE) * LANE
    ffns = [(l, f) for l in range(depth) for f in (1, 2)]
    pad_cols = lambda a: jnp.pad(a, ((0, 0), (0, fp - fb)))
    gu_local = [jnp.concatenate([pad_cols(w[f'ffn{f}_w_gate'][l]), pad_cols(w[f'ffn{f}_w_up'][l])], axis=1).astype(BF16)
                for l, f in ffns]
    dn_local = [jnp.pad(w[f'ffn{f}_w_down'][l], ((0, fp - fb), (0, 0))).astype(BF16) for l, f in ffns]
    rows_of = {n: w[n].size // d_model for n in REST}
    rest_local = jnp.concatenate([_flat_rows(w[n].astype(BF16), d_model) for n in REST], axis=0)
    n_rest = rest_local.shape[0]
    got = _all_gather([(a, 1) for a in gu_local] + [(a, 0) for a in dn_local] + [(rest_local, 0)], "gather_matrices")
    wgu, wd = dict(zip(ffns, got[:len(ffns)])), dict(zip(ffns, got[len(ffns):2 * len(ffns)]))
    gathered = got[-1].reshape(N_DEV, n_rest, d_model)
    full, start = {}, 0
    for n in REST:
        blocks = gathered[:, start:start + rows_of[n]].reshape((N_DEV,) + w[n].shape)
        full[n] = _full_from_blocks(blocks, n)
        start += rows_of[n]
    cw = conv_w.reshape(-1)
    cw_rows = -(-cw.size // (8 * LANE)) * 8
    cw_flat = jnp.pad(cw, (0, cw_rows * LANE - cw.size)).reshape(cw_rows, LANE)
    cw_all = _all_gather([(cw_flat, 0)], "gather_conv_w", in_vmem=True)[0].reshape(N_DEV, cw_rows * LANE)[:, :cw.size]
    conv_w_full = jnp.transpose(cw_all.reshape((N_DEV,) + conv_w.shape), (1, 2, 0, 3)).reshape(depth, CONV_K, CONV_W)
    vec = {n: w[n] for n in VECTORS}
    ops = [_layer_operands(full, vec, conv_w_full, l) for l in range(depth)]
    tabs = _rope_tables(t)

    saved, xl = [], x0
    for l in range(depth):
        o = ops[l]
        x1, s1 = _ffn_fwd(xl, o['ffn1_norm'], wgu[l, 1], wd[l, 1], fp)
        x2, sm = _mixer_fwd(x1, o, tabs)
        x3, s2 = _ffn_fwd(x2, o['ffn2_norm'], wgu[l, 2], wd[l, 2], fp)
        xl = _rms_fwd(x3, o['post_norm'], F32, "rms_fwd_post")
        saved.append((s1, sm, s2, x3))
    loss_part, dx = _loss_head(xl, target, "loss_head")
    loss = lax.psum(loss_part[0, 0], ("x", "y", "c"))

    grads, d_gu, d_dn = [None] * depth, {}, {}
    for l in reversed(range(depth)):
        o = ops[l]
        s1, sm, s2, x3 = saved[l]
        g = {}
        dx, dxb, g['post_norm'] = _rms_bwd(x3, o['post_norm'], dx, None, "rms_bwd_post")
        dx, dxb, g['ffn2_norm'], d_gu[l, 2], d_dn[l, 2] = _ffn_bwd(dx, dxb, s2, o['ffn2_norm'], wgu[l, 2], wd[l, 2], fp)
        dx, dxb, gm = _mixer_bwd(dx, dxb, sm, o, tabs)
        g.update(gm)
        dx, dxb, g['ffn1_norm'], d_gu[l, 1], d_dn[l, 1] = _ffn_bwd(dx, dxb, s1, o['ffn1_norm'], wgu[l, 1], wd[l, 1], fp)
        grads[l] = _layer_grads_to_params(g)
    grad_x = dx.reshape(x.shape)
    part = {n: jnp.stack([grads[l][n] for l in range(depth)]) for n in grads[0]}

    rest_own = jnp.concatenate([_blocks_from_full(part[n], n).reshape(N_DEV, rows_of[n], d_model) for n in REST],
                               axis=1)
    own = [d_gu[q][0] for q in ffns] + [d_dn[q][0] for q in ffns] + [rest_own]
    sends = [d_gu[q][1] for q in ffns] + [d_dn[q][1] for q in ffns] + [rest_own.astype(BF16)]
    from_sibling = _blocks_to_sibling(sends, "grads_to_sibling")
    pair_sums = [_pair_sums_for_chips(a, b, "grads_pair_sums") for a, b in zip(own, from_sibling)]
    from_chips = _parts_to_chips(pair_sums, "grads_to_chips")
    mine = [_sum_for_me(a, b, c, "grads_sum") for a, b, c in zip(own, from_sibling, from_chips)]
    mine_gu, mine_dn = dict(zip(ffns, mine[:len(ffns)])), dict(zip(ffns, mine[len(ffns):2 * len(ffns)]))
    grad = {}
    for f in (1, 2):
        grad[f'ffn{f}_w_gate'] = jnp.stack([mine_gu[l, f][:, :fb] for l in range(depth)])
        grad[f'ffn{f}_w_up'] = jnp.stack([mine_gu[l, f][:, fp:fp + fb] for l in range(depth)])
        grad[f'ffn{f}_w_down'] = jnp.stack([mine_dn[l, f][:fb] for l in range(depth)])
    start = 0
    for n in REST:
        grad[n] = mine[-1][start:start + rows_of[n]].reshape(w[n].shape)
        start += rows_of[n]

    small = jnp.concatenate([part[n].reshape(-1) for n in VECTORS] + [part['conv_w'].reshape(-1)])
    s_rows = -(-small.size // (8 * LANE)) * 8
    small = jnp.pad(small, (0, s_rows * LANE - small.size)).reshape(s_rows, LANE)
    small_all = _all_gather([(small, 0)], "gather_small_grads", in_vmem=True)[0]
    small_sum = _sum_parts([small_all[k * s_rows:(k + 1) * s_rows] for k in range(N_DEV)], "sum_small_grads")
    small_sum = small_sum.reshape(-1)
    start = 0
    for n in VECTORS:
        grad[n] = small_sum[start:start + w[n].size].reshape(w[n].shape)
        start += w[n].size
    cw_grad = small_sum[start:start + depth * CONV_K * CONV_W].reshape(depth, CONV_K, CONV_W)
    nb = conv_w.shape[-1]
    grad['conv_w'] = lax.dynamic_slice_in_dim(cw_grad, my_block * nb, nb, axis=2)

    delta, new_m, new_v = {}, {}, {}
    for n in BIG + ['conv_w']:
        shp = w[n].shape
        two_d = lambda a: a.reshape(-1, shp[-1])
        dl, mn, vn = _adamw(two_d(w[n]), two_d(grad[n]), two_d(mom[n]), two_d(var[n]), "adamw_" + n)
        delta[n], new_m[n], new_v[n] = dl.reshape(shp), mn.reshape(shp), vn.reshape(shp)
    vcat = lambda src: jnp.concatenate([src[n].reshape(-1) for n in VECTORS]).reshape(-1, LANE)
    dl, mn, vn = _adamw(vcat(w), vcat(grad), vcat(mom), vcat(var), "adamw_vectors")
    start = 0
    for n in VECTORS:
        sl = lambda a: a.reshape(-1)[start:start + w[n].size].reshape(w[n].shape)
        delta[n], new_m[n], new_v[n] = sl(dl), sl(mn), sl(vn)
        start += w[n].size

    return (loss, grad_x, *[grad[n] for n in WEIGHTS], *[delta[n] for n in WEIGHTS],
            *[new_m[n] for n in WEIGHTS], *[new_v[n] for n in WEIGHTS])
```

```python
import functools

import jax
import jax.numpy as jnp
from jax import lax
from jax.experimental import pallas as pl
from jax.experimental.pallas import tpu as pltpu

F32, BF16 = jnp.float32, jnp.bfloat16

N_DEV = 8
N_HEADS = 8
QK_NOPE, QK_ROPE, V_DIM = 64, 32, 64
QK_DIM = QK_NOPE + QK_ROPE
HEAD_PAD = 128
Q_LORA, KV_LORA = 384, 256
LAT = Q_LORA + KV_LORA
CONV_W, CONV_K = 512, 31
CONV_HALO = 32
CHUNK = 64
ROPE_THETA = 10000.0
EPS = 1e-6
ATTN_SCALE = QK_DIM ** -0.5
P_KPE = LAT
P_A = LAT + HEAD_PAD
P_G = P_A + CONV_W
P_COLS = P_G + CONV_W

ADAM_LR, ADAM_B1, ADAM_B2, ADAM_EPS, ADAM_WD, ADAM_STEP = 0.001, 0.9, 0.999, 1e-08, 0.01, 10

V7X_VMEM_BYTES = 64 << 20
VMEM_LIMIT = V7X_VMEM_BYTES - (8 << 20)
MM_VMEM_BUDGET = 36 << 20
LANE = 128
ROW_TILE = 512
ATTN_BLOCK = 512
CONV_TILE = 256

MESH_ID = pl.DeviceIdType.MESH
ANY = pl.BlockSpec(memory_space=pl.ANY)
VMEM_SPEC = pl.BlockSpec(memory_space=pltpu.VMEM)
HBM_SPEC = pl.BlockSpec(memory_space=pltpu.HBM)
SEM_SPEC = pl.BlockSpec(memory_space=pltpu.SEMAPHORE)
DATAFLOW = pltpu.SideEffectType.DATAFLOW_SIDE_EFFECTING

WEIGHTS = ['ffn1_norm', 'ffn1_w_gate', 'ffn1_w_up', 'ffn1_w_down', 'mix_norm', 'w_in', 'q_latent_norm', 'w_uq',
           'kv_latent_norm', 'w_ukv', 'q_norm', 'k_norm', 'conv_w', 'conv_b', 'conv_ln_g', 'conv_ln_b', 'w_out',
           'ffn2_norm', 'ffn2_w_gate', 'ffn2_w_up', 'ffn2_w_down', 'post_norm']
COL_SHARDED = ['ffn1_w_gate', 'ffn1_w_up', 'w_in', 'w_uq', 'w_ukv', 'ffn2_w_gate', 'ffn2_w_up']
ROW_SHARDED = ['ffn1_w_down', 'w_out', 'ffn2_w_down']
REST = ['w_in', 'w_uq', 'w_ukv', 'w_out']
BIG = ['ffn1_w_gate', 'ffn1_w_up', 'ffn1_w_down', 'w_in', 'w_uq', 'w_ukv', 'w_out', 'ffn2_w_gate', 'ffn2_w_up',
       'ffn2_w_down']
VECTORS = ['ffn1_norm', 'mix_norm', 'q_latent_norm', 'kv_latent_norm', 'q_norm', 'k_norm', 'conv_b', 'conv_ln_g',
           'conv_ln_b', 'ffn2_norm', 'post_norm']


def _params(*sem):
    return pltpu.CompilerParams(dimension_semantics=sem if sem else None, vmem_limit_bytes=VMEM_LIMIT)


def _tile(n, cap):
    if n <= cap:
        return n
    best = 0
    for d in range(LANE, cap + 1, LANE):
        if n % d == 0:
            best = d
    assert best, (n, cap)
    return best


def _row_tile(n, cap=ROW_TILE, mult=8):
    if n <= cap:
        return n
    best = 0
    for d in range(mult, cap + 1, mult):
        if n % d == 0:
            best = d
    assert best, (n, cap)
    return best


def _mm(a, b, *, name, ta=False, tb=False, res=None, scale=1.0, out_dtype=F32, tm=None, tn=None, blocks=None):
    (kdim, m) = a.shape if ta else a.shape[::-1]
    (n, kb) = b.shape if tb else b.shape[::-1]
    assert kdim == kb, (a.shape, b.shape, ta, tb)
    tm, tn = tm or _tile(m, 512), tn or _tile(n, 1024)
    if blocks is not None:
        tm, tn = (tm, n // blocks[1]) if blocks[0] == 'col' else (m // blocks[1], tn)
    size = lambda arr: jnp.dtype(arr.dtype).itemsize
    out_bytes = tm * tn * ((6 if blocks is not None else jnp.dtype(out_dtype).itemsize) + (4 if res is not None else 0))

    def vmem_need(tk):
        return 2 * (tm * tk * size(a) + tk * tn * size(b) + out_bytes) + (tm * tn * 4 if tk < kdim else 0)

    tk = kdim
    for cand in [d for d in range(kdim - LANE, 0, -LANE) if kdim % d == 0]:
        if vmem_need(tk) <= MM_VMEM_BUDGET:
            break
        tk = cand
    nk = kdim // tk
    n_in = 3 if res is not None else 2
    n_out = 2 if blocks is not None else 1
    dims = (((0 if ta else 1,), (1 if tb else 0,)), ((), ()))

    def body(*refs):
        a_ref, b_ref = refs[0], refs[1]
        r_ref = refs[2] if res is not None else None
        o_refs = refs[n_in:n_in + n_out]
        acc_ref = refs[-1] if nk > 1 else None
        part = lax.dot_general(a_ref[...].astype(BF16), b_ref[...].astype(BF16), dims, preferred_element_type=F32)

        def finish(acc):
            if scale != 1.0:
                acc = acc * scale
            if r_ref is not None:
                acc = r_ref[...] + acc
            for o_ref in o_refs:
                o_ref[...] = acc.astype(o_ref.dtype)

        if nk == 1:
            finish(part)
        else:
            k = pl.program_id(2)

            @pl.when(k == 0)
            def _():
                acc_ref[...] = part

            @pl.when(k > 0)
            def _():
                acc_ref[...] += part

            @pl.when(k == nk - 1)
            def _():
                finish(acc_ref[...])

    a_spec = pl.BlockSpec((tk, tm), lambda i, j, k: (k, i)) if ta else pl.BlockSpec((tm, tk), lambda i, j, k: (i, k))
    b_spec = pl.BlockSpec((tn, tk), lambda i, j, k: (j, k)) if tb else pl.BlockSpec((tk, tn), lambda i, j, k: (k, j))
    plain = pl.BlockSpec((tm, tn), lambda i, j, k: (i, j))
    if blocks is None:
        out_specs, out_shape = plain, jax.ShapeDtypeStruct((m, n), out_dtype)
    else:
        if blocks[0] == 'col':
            o_spec, shp = pl.BlockSpec((None, tm, tn), lambda i, j, k: (j, i, 0)), (blocks[1], m, tn)
        else:
            o_spec, shp = pl.BlockSpec((None, tm, tn), lambda i, j, k: (i, 0, j)), (blocks[1], tm, n)
        out_specs, out_shape = (o_spec, o_spec), (jax.ShapeDtypeStruct(shp, F32), jax.ShapeDtypeStruct(shp, BF16))
    in_specs = [a_spec, b_spec] + ([plain] if res is not None else [])
    args = (a, b) + ((res,) if res is not None else ())
    return pl.pallas_call(
        body, name=name, grid=(m // tm, n // tn, nk), in_specs=in_specs, out_specs=out_specs, out_shape=out_shape,
        scratch_shapes=[pltpu.VMEM((tm, tn), F32)] if nk > 1 else [],
        compiler_params=_params("parallel", "parallel", "arbitrary"),
    )(*args)


def _rms_fwd(x, g, out_dtype, name):
    t, d = x.shape
    tm = _row_tile(t)

    def body(x_ref, g_ref, o_ref):
        xv = x_ref[...]
        r = lax.rsqrt(jnp.mean(xv * xv, axis=-1, keepdims=True) + EPS)
        o_ref[...] = (xv * r * g_ref[...]).astype(o_ref.dtype)

    return pl.pallas_call(
        body, name=name, grid=(t // tm,),
        in_specs=[pl.BlockSpec((tm, d), lambda i: (i, 0)), pl.BlockSpec((1, d), lambda i: (0, 0))],
        out_specs=pl.BlockSpec((tm, d), lambda i: (i, 0)),
        out_shape=jax.ShapeDtypeStruct((t, d), out_dtype), compiler_params=_params("parallel"),
    )(x, g.reshape(1, d))


def _rms_bwd(x, g, dh, res, name):
    t, d = x.shape
    tm = _row_tile(t)

    def body(*refs):
        x_ref, g_ref, dh_ref = refs[:3]
        r_ref = refs[3] if res is not None else None
        dx_ref, dxb_ref, dg_ref = refs[-3:]
        xv, dhv = x_ref[...], dh_ref[...]
        r = lax.rsqrt(jnp.mean(xv * xv, axis=-1, keepdims=True) + EPS)
        y = xv * r
        dy = dhv * g_ref[...]
        dx = r * (dy - y * jnp.mean(dy * y, axis=-1, keepdims=True))
        if r_ref is not None:
            dx = r_ref[...] + dx
        dx_ref[...] = dx
        dxb_ref[...] = dx.astype(BF16)

        @pl.when(pl.program_id(0) == 0)
        def _():
            dg_ref[...] = jnp.zeros_like(dg_ref)

        dg_ref[...] += jnp.sum(dhv * y, axis=0, keepdims=True)

    row = pl.BlockSpec((tm, d), lambda i: (i, 0))
    vec = pl.BlockSpec((1, d), lambda i: (0, 0))
    args = (x, g.reshape(1, d), dh) + ((res,) if res is not None else ())
    dx, dxb, dg = pl.pallas_call(
        body, name=name, grid=(t // tm,), in_specs=[row, vec, row] + ([row] if res is not None else []),
        out_specs=(row, row, vec),
        out_shape=(jax.ShapeDtypeStruct((t, d), F32), jax.ShapeDtypeStruct((t, d), BF16),
                   jax.ShapeDtypeStruct((1, d), F32)),
        compiler_params=_params("arbitrary"),
    )(*args)
    return dx, dxb, dg.reshape(d)


FFN_PAIR = 2


def _ffn_up(h, wgu, fp, name):
    t, d = h.shape
    tm, tn = _tile(t, 512), FFN_PAIR * 2 * fp
    nj = wgu.shape[1] // tn

    def body(h_ref, w_ref, ab_ref, z_ref):
        ab = jnp.dot(h_ref[...], w_ref[...], preferred_element_type=F32)
        ab_ref[...] = ab.astype(ab_ref.dtype)
        for e in range(FFN_PAIR):
            av, bv = ab[:, 2 * fp * e:2 * fp * e + fp], ab[:, 2 * fp * e + fp:2 * fp * (e + 1)]
            z_ref[:, fp * e:fp * (e + 1)] = (av * jax.nn.sigmoid(av) * bv).astype(z_ref.dtype)

    return pl.pallas_call(
        body, name=name, grid=(nj, t // tm),
        in_specs=[pl.BlockSpec((tm, d), lambda j, i: (i, 0)), pl.BlockSpec((d, tn), lambda j, i: (0, j))],
        out_specs=(pl.BlockSpec((tm, tn), lambda j, i: (i, j)), pl.BlockSpec((tm, tn // 2), lambda j, i: (i, j))),
        out_shape=(jax.ShapeDtypeStruct((t, wgu.shape[1]), BF16), jax.ShapeDtypeStruct((t, wgu.shape[1] // 2), BF16)),
        compiler_params=_params("parallel", "parallel"),
    )(h, wgu)


def _ffn_dab(dyb, wd, ab, fp, name):
    t, d = dyb.shape
    tm, tn = _tile(t, 512), FFN_PAIR * 2 * fp
    nj = ab.shape[1] // tn

    def body(dy_ref, wd_ref, ab_ref, dab_ref):
        dz = _dot_nt(dy_ref[...], wd_ref[...]) * 0.5
        for e in range(FFN_PAIR):
            av = ab_ref[:, 2 * fp * e:2 * fp * e + fp].astype(F32)
            bv = ab_ref[:, 2 * fp * e + fp:2 * fp * (e + 1)].astype(F32)
            dze = dz[:, fp * e:fp * (e + 1)]
            s = jax.nn.sigmoid(av)
            dab_ref[:, 2 * fp * e:2 * fp * e + fp] = (dze * bv * (s * (1.0 + av * (1.0 - s)))).astype(dab_ref.dtype)
            dab_ref[:, 2 * fp * e + fp:2 * fp * (e + 1)] = (dze * (av * s)).astype(dab_ref.dtype)

    return pl.pallas_call(
        body, name=name, grid=(nj, t // tm),
        in_specs=[pl.BlockSpec((tm, d), lambda j, i: (i, 0)), pl.BlockSpec((tn // 2, d), lambda j, i: (j, 0)),
                  pl.BlockSpec((tm, tn), lambda j, i: (i, j))],
        out_specs=pl.BlockSpec((tm, tn), lambda j, i: (i, j)),
        out_shape=jax.ShapeDtypeStruct(ab.shape, BF16), compiler_params=_params("parallel", "parallel"),
    )(dyb, wd, ab)


def _lat_norm_fwd(p, g_q, g_kv, name):
    t = p.shape[0]
    tm = _row_tile(t)

    def body(p_ref, gq_ref, gkv_ref, q_ref, kv_ref):
        for lo, hi, g_ref, o_ref in ((0, Q_LORA, gq_ref, q_ref), (Q_LORA, LAT, gkv_ref, kv_ref)):
            xv = p_ref[:, lo:hi]
            r = lax.rsqrt(jnp.mean(xv * xv, axis=-1, keepdims=True) + EPS)
            o_ref[...] = (xv * r * g_ref[...]).astype(o_ref.dtype)

    return pl.pallas_call(
        body, name=name, grid=(t // tm,),
        in_specs=[pl.BlockSpec((tm, P_COLS), lambda i: (i, 0)), pl.BlockSpec((1, Q_LORA), lambda i: (0, 0)),
                  pl.BlockSpec((1, KV_LORA), lambda i: (0, 0))],
        out_specs=(pl.BlockSpec((tm, Q_LORA), lambda i: (i, 0)), pl.BlockSpec((tm, KV_LORA), lambda i: (i, 0))),
        out_shape=(jax.ShapeDtypeStruct((t, Q_LORA), BF16), jax.ShapeDtypeStruct((t, KV_LORA), BF16)),
        compiler_params=_params("parallel"),
    )(p, g_q.reshape(1, Q_LORA), g_kv.reshape(1, KV_LORA))


def _lat_norm_bwd(p, g_q, g_kv, dq, dkv, name):
    t = p.shape[0]
    tm = _row_tile(t)

    def body(p_ref, gq_ref, gkv_ref, dq_ref, dkv_ref, dp_ref, dgq_ref, dgkv_ref):
        first = pl.program_id(0) == 0
        for lo, hi, g_ref, d_ref, dg_ref in ((0, Q_LORA, gq_ref, dq_ref, dgq_ref),
                                             (Q_LORA, LAT, gkv_ref, dkv_ref, dgkv_ref)):
            xv, dhv = p_ref[:, lo:hi], d_ref[...]
            r = lax.rsqrt(jnp.mean(xv * xv, axis=-1, keepdims=True) + EPS)
            y = xv * r
            dy = dhv * g_ref[...]
            dp_ref[:, lo:hi] = r * (dy - y * jnp.mean(dy * y, axis=-1, keepdims=True))

            @pl.when(first)
            def _():
                dg_ref[...] = jnp.zeros_like(dg_ref)

            dg_ref[...] += jnp.sum(dhv * y, axis=0, keepdims=True)

    vq = pl.BlockSpec((1, Q_LORA), lambda i: (0, 0))
    vkv = pl.BlockSpec((1, KV_LORA), lambda i: (0, 0))
    dp, dgq, dgkv = pl.pallas_call(
        body, name=name, grid=(t // tm,),
        in_specs=[pl.BlockSpec((tm, P_COLS), lambda i: (i, 0)), vq, vkv,
                  pl.BlockSpec((tm, Q_LORA), lambda i: (i, 0)), pl.BlockSpec((tm, KV_LORA), lambda i: (i, 0))],
        out_specs=(pl.BlockSpec((tm, LAT), lambda i: (i, 0)), vq, vkv),
        out_shape=(jax.ShapeDtypeStruct((t, LAT), F32), jax.ShapeDtypeStruct((1, Q_LORA), F32),
                   jax.ShapeDtypeStruct((1, KV_LORA), F32)),
        compiler_params=_params("arbitrary"),
    )(p, g_q.reshape(1, Q_LORA), g_kv.reshape(1, KV_LORA), dq, dkv)
    return dp, dgq.reshape(Q_LORA), dgkv.reshape(KV_LORA)


def _rope_tables(t):
    half = QK_ROPE // 2
    pos = jnp.arange(t, dtype=F32)
    inv_freq = 1.0 / (ROPE_THETA ** (jnp.arange(0, QK_ROPE, 2, dtype=F32) / QK_ROPE))
    ang = pos[:, None] * inv_freq[None, :]
    cos, sin = jnp.cos(ang), jnp.sin(ang)
    z = lambda n: jnp.zeros((t, n), F32)
    c_tab = jnp.concatenate([jnp.ones((t, QK_NOPE), F32), cos, cos, z(HEAD_PAD - QK_DIM)], axis=1)
    sa_tab = jnp.concatenate([z(QK_NOPE), -sin, z(half), z(HEAD_PAD - QK_DIM)], axis=1)
    sb_tab = jnp.concatenate([z(QK_NOPE), z(half), sin, z(HEAD_PAD - QK_DIM)], axis=1)
    return c_tab, sa_tab, sb_tab


def _rope(x, c, sa, sb):
    half = QK_ROPE // 2
    return x * c + pltpu.roll(x, HEAD_PAD - half, 1) * sa + pltpu.roll(x, half, 1) * sb


def _rope_t(d, c, sa, sb):
    half = QK_ROPE // 2
    return d * c + pltpu.roll(d * sa, half, 1) + pltpu.roll(d * sb, HEAD_PAD - half, 1)


def _head_rms(x):
    r = lax.rsqrt(jnp.sum(x * x, axis=-1, keepdims=True) * (1.0 / QK_DIM) + EPS)
    return x * r, r


def _qk_prep_fwd(q_raw, k_raw, p, gq, gk, tabs, name):
    t = q_raw.shape[0]
    tm = _row_tile(t)

    def body(q_ref, k_ref, p_ref, gq_ref, gk_ref, c_ref, sa_ref, sb_ref, qo_ref, ko_ref):
        c, sa, sb = c_ref[...], sa_ref[...], sb_ref[...]
        qn, _ = _head_rms(q_ref[...])
        qo_ref[...] = _rope(qn * gq_ref[...], c, sa, sb).astype(qo_ref.dtype)
        kn, _ = _head_rms(k_ref[...] + p_ref[...])
        ko_ref[...] = _rope(kn * gk_ref[...], c, sa, sb).astype(ko_ref.dtype)

    head = pl.BlockSpec((tm, HEAD_PAD), lambda i, h: (i, h))
    tab = pl.BlockSpec((tm, HEAD_PAD), lambda i, h: (i, 0))
    vec = pl.BlockSpec((1, HEAD_PAD), lambda i, h: (0, 0))
    kpe = pl.BlockSpec((tm, HEAD_PAD), lambda i, h: (i, P_KPE // HEAD_PAD))
    return pl.pallas_call(
        body, name=name, grid=(t // tm, N_HEADS), in_specs=[head, head, kpe, vec, vec, tab, tab, tab],
        out_specs=(head, head),
        out_shape=(jax.ShapeDtypeStruct(q_raw.shape, BF16), jax.ShapeDtypeStruct(k_raw.shape, BF16)),
        compiler_params=_params("parallel", "parallel"),
    )(q_raw, k_raw, p, gq.reshape(1, HEAD_PAD), gk.reshape(1, HEAD_PAD), *tabs)


def _qk_prep_bwd(q_raw, k_raw, p, dq, dk, gq, gk, tabs, name):
    t = q_raw.shape[0]
    tm = _row_tile(t)

    def body(q_ref, k_ref, p_ref, dq_ref, dk_ref, gq_ref, gk_ref, c_ref, sa_ref, sb_ref,
             dqr_ref, dkr_ref, dkpe_ref, dgq_ref, dgk_ref):
        i, h = pl.program_id(0), pl.program_id(1)
        c, sa, sb = c_ref[...], sa_ref[...], sb_ref[...]

        def one(x, d, g_ref, dg_ref):
            n, r = _head_rms(x)
            dng = _rope_t(d, c, sa, sb)

            @pl.when(jnp.logical_and(i == 0, h == 0))
            def _():
                dg_ref[...] = jnp.zeros_like(dg_ref)

            dg_ref[...] += jnp.sum(dng * n, axis=0, keepdims=True)
            dn = dng * g_ref[...]
            return r * (dn - n * (jnp.sum(dn * n, axis=-1, keepdims=True) * (1.0 / QK_DIM)))

        dqr_ref[...] = one(q_ref[...], dq_ref[...], gq_ref, dgq_ref)
        dkr = one(k_ref[...] + p_ref[...], dk_ref[...], gk_ref, dgk_ref)
        dkr_ref[...] = dkr

        @pl.when(h == 0)
        def _():
            dkpe_ref[...] = dkr

        @pl.when(h > 0)
        def _():
            dkpe_ref[...] += dkr

    head = pl.BlockSpec((tm, HEAD_PAD), lambda i, h: (i, h))
    tab = pl.BlockSpec((tm, HEAD_PAD), lambda i, h: (i, 0))
    vec = pl.BlockSpec((1, HEAD_PAD), lambda i, h: (0, 0))
    kpe = pl.BlockSpec((tm, HEAD_PAD), lambda i, h: (i, P_KPE // HEAD_PAD))
    dqr, dkr, dkpe, dgq, dgk = pl.pallas_call(
        body, name=name, grid=(t // tm, N_HEADS), in_specs=[head, head, kpe, head, head, vec, vec, tab, tab, tab],
        out_specs=(head, head, tab, vec, vec),
        out_shape=(jax.ShapeDtypeStruct(q_raw.shape, F32), jax.ShapeDtypeStruct(k_raw.shape, F32),
                   jax.ShapeDtypeStruct((t, HEAD_PAD), F32), jax.ShapeDtypeStruct((1, HEAD_PAD), F32),
                   jax.ShapeDtypeStruct((1, HEAD_PAD), F32)),
        compiler_params=_params("arbitrary", "arbitrary"),
    )(q_raw, k_raw, p, dq, dk, gq.reshape(1, HEAD_PAD), gk.reshape(1, HEAD_PAD), *tabs)
    return dqr, dkr, dkpe, dgq.reshape(HEAD_PAD), dgk.reshape(HEAD_PAD)


def _dot_nt(a, b):
    return lax.dot_general(a, b, (((1,), (1,)), ((), ())), preferred_element_type=F32)


def _dot_tn(a, b):
    return lax.dot_general(a, b, (((0,), (0,)), ((), ())), preferred_element_type=F32)


def _diag_mask():
    rows = lax.broadcasted_iota(jnp.int32, (ATTN_BLOCK, ATTN_BLOCK), 0) // CHUNK
    cols = lax.broadcasted_iota(jnp.int32, (ATTN_BLOCK, ATTN_BLOCK), 1) // CHUNK
    return cols <= rows


def _attn_fwd(q, k, v, name):
    t = q.shape[0]
    bq = ATTN_BLOCK
    nq = t // bq

    def body(q_ref, k_ref, v_ref, o_ref, lse_ref):
        i = pl.program_id(1)
        qv = q_ref[...]

        def block(j, carry, masked):
            m, l, acc = carry
            rows = pl.ds(pl.multiple_of(j * bq, bq), bq)
            s = _dot_nt(qv, k_ref[rows, :]) * ATTN_SCALE
            if masked:
                s = jnp.where(_diag_mask(), s, -1e30)
            m_new = jnp.maximum(m, jnp.max(s, axis=-1, keepdims=True))
            alpha = jnp.exp(m - m_new)
            pe = jnp.exp(s - m_new)
            l = alpha * l + jnp.sum(pe, axis=-1, keepdims=True)
            acc = alpha * acc + jnp.dot(pe.astype(BF16), v_ref[rows, :], preferred_element_type=F32)
            return m_new, l, acc

        init = (jnp.full((bq, 1), -1e30, F32), jnp.zeros((bq, 1), F32), jnp.zeros((bq, HEAD_PAD), F32))
        carry = lax.fori_loop(0, i, lambda j, cr: block(j, cr, False), init)
        m, l, acc = block(i, carry, True)
        o_ref[...] = acc / l
        lse_ref[...] = jnp.broadcast_to(m + jnp.log(l), (bq, HEAD_PAD))

    blk = pl.BlockSpec((bq, HEAD_PAD), lambda h, i: (i, h))
    full = pl.BlockSpec((t, HEAD_PAD), lambda h, i: (0, h))
    return pl.pallas_call(
        body, name=name, grid=(N_HEADS, nq), in_specs=[blk, full, full], out_specs=(blk, blk),
        out_shape=(jax.ShapeDtypeStruct(q.shape, F32), jax.ShapeDtypeStruct(q.shape, F32)),
        compiler_params=_params("parallel", "parallel"),
    )(q, k, v)


def _attn_bwd(q, k, v, o, lse, do, name):
    t = q.shape[0]
    bq = ATTN_BLOCK
    nq = t // bq

    def body(q_ref, k_ref, v_ref, o_ref, lse_ref, do_ref, dq_ref, dk_ref, dv_ref, delta_ref):
        def rows_of(i):
            return pl.ds(pl.multiple_of(i * bq, bq), bq)

        def prep(i, _):
            r = rows_of(i)
            delta_ref[r, :] = jnp.broadcast_to(jnp.sum(do_ref[r, :] * o_ref[r, :], axis=-1, keepdims=True),
                                               (bq, HEAD_PAD))
            dq_ref[r, :] = jnp.zeros((bq, HEAD_PAD), F32)
            return 0

        lax.fori_loop(0, nq, prep, 0)

        def key_block(j, _):
            rj = rows_of(j)
            kb, vb = k_ref[rj, :], v_ref[rj, :]

            def query_block(i, carry, masked):
                dk, dv = carry
                ri = rows_of(i)
                qb, dob = q_ref[ri, :], do_ref[ri, :].astype(BF16)
                s = _dot_nt(qb, kb) * ATTN_SCALE
                if masked:
                    s = jnp.where(_diag_mask(), s, -1e30)
                pe = jnp.exp(s - lse_ref[ri, :][:, :1])
                dp = _dot_nt(dob, vb)
                ds = (pe * (dp - delta_ref[ri, :][:, :1]) * ATTN_SCALE).astype(BF16)
                dq_ref[ri, :] += jnp.dot(ds, kb, preferred_element_type=F32)
                return dk + _dot_tn(ds, qb), dv + _dot_tn(pe.astype(BF16), dob)

            zero = jnp.zeros((bq, HEAD_PAD), F32)
            carry = query_block(j, (zero, zero), True)
            dk, dv = lax.fori_loop(j + 1, nq, lambda i, cr: query_block(i, cr, False), carry)
            dk_ref[rj, :] = dk
            dv_ref[rj, :] = dv
            return 0

        lax.fori_loop(0, nq, key_block, 0)

    full = pl.BlockSpec((t, HEAD_PAD), lambda h: (0, h))
    shp = jax.ShapeDtypeStruct(q.shape, F32)
    return pl.pallas_call(
        body, name=name, grid=(N_HEADS,), in_specs=[full] * 6, out_specs=(full, full, full),
        out_shape=(shp, shp, shp), scratch_shapes=[pltpu.VMEM((t, HEAD_PAD), F32)],
        compiler_params=_params("parallel"),
    )(q, k, v, o, lse, do)


def _glu_ext(pc_ref, pp_ref, u_ref, tm, first):
    u_ref[CONV_HALO:CONV_HALO + tm, :] = pc_ref[:, P_A:P_G] * jax.nn.sigmoid(pc_ref[:, P_G:P_COLS])
    up = pp_ref[tm - CONV_HALO:tm, P_A:P_G] * jax.nn.sigmoid(pp_ref[tm - CONV_HALO:tm, P_G:P_COLS])
    u_ref[0:CONV_HALO, :] = jnp.where(first, 0.0, up)


def _conv_fwd(p, w, b, ln_g, ln_b, name):
    t = p.shape[0]
    tm = _row_tile(t, CONV_TILE)
    off = CONV_HALO - (CONV_K - 1)

    def body(pc_ref, pp_ref, w_ref, b_ref, g_ref, bb_ref, y_ref, o_ref, u_ref):
        _glu_ext(pc_ref, pp_ref, u_ref, tm, pl.program_id(0) == 0)
        acc = jnp.zeros((tm, CONV_W), F32)
        for kk in range(CONV_K):
            acc = acc + w_ref[kk:kk + 1, :] * u_ref[off + kk:off + kk + tm, :]
        y = acc + b_ref[...]
        y_ref[...] = y
        xc = y - jnp.mean(y, axis=-1, keepdims=True)
        lo = xc * lax.rsqrt(jnp.mean(xc * xc, axis=-1, keepdims=True) + EPS) * g_ref[...] + bb_ref[...]
        o_ref[...] = (lo * jax.nn.sigmoid(lo)).astype(o_ref.dtype)

    prow = pl.BlockSpec((tm, P_COLS), lambda i: (i, 0))
    pprev = pl.BlockSpec((tm, P_COLS), lambda i: (jnp.maximum(i - 1, 0), 0))
    vec = pl.BlockSpec((1, CONV_W), lambda i: (0, 0))
    row = pl.BlockSpec((tm, CONV_W), lambda i: (i, 0))
    return pl.pallas_call(
        body, name=name, grid=(t // tm,),
        in_specs=[prow, pprev, pl.BlockSpec((CONV_HALO, CONV_W), lambda i: (0, 0)), vec, vec, vec],
        out_specs=(row, row),
        out_shape=(jax.ShapeDtypeStruct((t, CONV_W), F32), jax.ShapeDtypeStruct((t, CONV_W), BF16)),
        scratch_shapes=[pltpu.VMEM((tm + CONV_HALO, CONV_W), F32)], compiler_params=_params("parallel"),
    )(p, p, w, b.reshape(1, CONV_W), ln_g.reshape(1, CONV_W), ln_b.reshape(1, CONV_W))


def _conv_bwd_ln(y, dout, ln_g, ln_b, name):
    t = y.shape[0]
    tm = _row_tile(t)

    def body(y_ref, d_ref, g_ref, bb_ref, dy_ref, dg_ref, db_ref, dcb_ref):
        yv = y_ref[...]
        xc = yv - jnp.mean(yv, axis=-1, keepdims=True)
        r = lax.rsqrt(jnp.mean(xc * xc, axis=-1, keepdims=True) + EPS)
        n = xc * r
        lo = n * g_ref[...] + bb_ref[...]
        s = jax.nn.sigmoid(lo)
        dlo = d_ref[...] * (s * (1.0 + lo * (1.0 - s)))
        dn = dlo * g_ref[...]
        dy = r * (dn - jnp.mean(dn, axis=-1, keepdims=True) - n * jnp.mean(dn * n, axis=-1, keepdims=True))
        dy_ref[...] = dy

        @pl.when(pl.program_id(0) == 0)
        def _():
            dg_ref[...] = jnp.zeros_like(dg_ref)
            db_ref[...] = jnp.zeros_like(db_ref)
            dcb_ref[...] = jnp.zeros_like(dcb_ref)

        dg_ref[...] += jnp.sum(dlo * n, axis=0, keepdims=True)
        db_ref[...] += jnp.sum(dlo, axis=0, keepdims=True)
        dcb_ref[...] += jnp.sum(dy, axis=0, keepdims=True)

    row = pl.BlockSpec((tm, CONV_W), lambda i: (i, 0))
    vec = pl.BlockSpec((1, CONV_W), lambda i: (0, 0))
    vshape = jax.ShapeDtypeStruct((1, CONV_W), F32)
    dy, dg, db, dcb = pl.pallas_call(
        body, name=name, grid=(t // tm,), in_specs=[row, row, vec, vec], out_specs=(row, vec, vec, vec),
        out_shape=(jax.ShapeDtypeStruct((t, CONV_W), F32), vshape, vshape, vshape),
        compiler_params=_params("arbitrary"),
    )(y, dout, ln_g.reshape(1, CONV_W), ln_b.reshape(1, CONV_W))
    return dy, dg.reshape(CONV_W), db.reshape(CONV_W), dcb.reshape(CONV_W)


def _conv_bwd_taps(p, dy, w, name):
    t = p.shape[0]
    tm = _row_tile(t, CONV_TILE)
    nt = t // tm
    off = CONV_HALO - (CONV_K - 1)

    def body(pc_ref, pp_ref, dyc_ref, dyn_ref, w_ref, dag_ref, dw_ref, u_ref, dye_ref):
        i = pl.program_id(0)
        _glu_ext(pc_ref, pp_ref, u_ref, tm, i == 0)
        dyc = dyc_ref[...]
        dye_ref[0:tm, :] = dyc
        dye_ref[tm:tm + CONV_HALO, :] = jnp.where(i == nt - 1, 0.0, dyn_ref[0:CONV_HALO, :])

        @pl.when(i == 0)
        def _():
            dw_ref[...] = jnp.zeros_like(dw_ref)

        du = jnp.zeros((tm, CONV_W), F32)
        for kk in range(CONV_K):
            dw_ref[kk:kk + 1, :] += jnp.sum(dyc * u_ref[off + kk:off + kk + tm, :], axis=0, keepdims=True)
            back = CONV_K - 1 - kk
            du = du + w_ref[kk:kk + 1, :] * dye_ref[back:back + tm, :]
        av, gv = pc_ref[:, P_A:P_G], pc_ref[:, P_G:P_COLS]
        s = jax.nn.sigmoid(gv)
        dag_ref[:, 0:CONV_W] = du * s
        dag_ref[:, CONV_W:2 * CONV_W] = du * av * (s * (1.0 - s))

    prow = pl.BlockSpec((tm, P_COLS), lambda i: (i, 0))
    pprev = pl.BlockSpec((tm, P_COLS), lambda i: (jnp.maximum(i - 1, 0), 0))
    row = pl.BlockSpec((tm, CONV_W), lambda i: (i, 0))
    nxt = pl.BlockSpec((tm, CONV_W), lambda i: (jnp.minimum(i + 1, nt - 1), 0))
    wspec = pl.BlockSpec((CONV_HALO, CONV_W), lambda i: (0, 0))
    return pl.pallas_call(
        body, name=name, grid=(nt,), in_specs=[prow, pprev, row, nxt, wspec],
        out_specs=(pl.BlockSpec((tm, 2 * CONV_W), lambda i: (i, 0)), wspec),
        out_shape=(jax.ShapeDtypeStruct((t, 2 * CONV_W), F32), jax.ShapeDtypeStruct((CONV_HALO, CONV_W), F32)),
        scratch_shapes=[pltpu.VMEM((tm + CONV_HALO, CONV_W), F32), pltpu.VMEM((tm + CONV_HALO, CONV_W), F32)],
        compiler_params=_params("arbitrary"),
    )(p, p, dy, dy, w)


def _loss_head(y, target, name):
    t, d = y.shape
    tm = _row_tile(t)

    def body(y_ref, t_ref, l_ref, dy_ref):
        err = y_ref[...] - t_ref[...]
        dy_ref[...] = err * (1.0 / d)

        @pl.when(pl.program_id(0) == 0)
        def _():
            l_ref[...] = jnp.zeros_like(l_ref)

        row = jnp.sum(err * err, axis=-1, keepdims=True) * (0.5 / d)
        l_ref[...] += jnp.broadcast_to(jnp.sum(row, axis=0, keepdims=True), (1, LANE))

    row = pl.BlockSpec((tm, d), lambda i: (i, 0))
    return pl.pallas_call(
        body, name=name, grid=(t // tm,), in_specs=[row, row],
        out_specs=(pl.BlockSpec((1, LANE), lambda i: (0, 0)), row),
        out_shape=(jax.ShapeDtypeStruct((1, LANE), F32), jax.ShapeDtypeStruct((t, d), F32)),
        compiler_params=_params("arbitrary"),
    )(y, target)


def _adamw(w, g, m, v, name):
    r, c = w.shape
    tr = _row_tile(r, 256)
    c1, c2 = 1.0 - ADAM_B1 ** ADAM_STEP, 1.0 - ADAM_B2 ** ADAM_STEP

    def body(w_ref, g_ref, m_ref, v_ref, d_ref, mo_ref, vo_ref):
        gv = g_ref[...]
        mn = ADAM_B1 * m_ref[...] + (1.0 - ADAM_B1) * gv
        vn = ADAM_B2 * v_ref[...] + (1.0 - ADAM_B2) * (gv * gv)
        mo_ref[...] = mn
        vo_ref[...] = vn
        d_ref[...] = -ADAM_LR * ((mn / c1) / (jnp.sqrt(vn / c2) + ADAM_EPS) + ADAM_WD * w_ref[...])

    blk = pl.BlockSpec((tr, c), lambda i: (i, 0))
    shp = jax.ShapeDtypeStruct((r, c), F32)
    return pl.pallas_call(
        body, name=name, grid=(r // tr,), in_specs=[blk] * 4, out_specs=(blk, blk, blk), out_shape=(shp, shp, shp),
        compiler_params=_params("parallel"),
    )(w, g, m, v)


def _sum_parts(parts, name):
    r, c = parts[0].shape
    tr = _row_tile(r, 256)

    def body(*refs):
        acc = refs[0][...]
        for ref in refs[1:-1]:
            acc = acc + ref[...]
        refs[-1][...] = acc

    blk = pl.BlockSpec((tr, c), lambda i: (i, 0))
    return pl.pallas_call(
        body, name=name, grid=(r // tr,), in_specs=[blk] * len(parts), out_specs=blk,
        out_shape=jax.ShapeDtypeStruct((r, c), F32), compiler_params=_params("parallel"),
    )(*parts)


def _place():
    return lax.axis_index("x"), lax.axis_index("y"), lax.axis_index("c")


def _window(ref, block, size, axis):
    start = pl.multiple_of(block * size, LANE if size % LANE == 0 else 8)
    return ref.at[(slice(None),) * axis + (pl.ds(start, size),)]


def _all_gather(pieces, name, in_vmem=False):
    n_p = len(pieces)

    def body(*refs):
        x_refs, out_refs = refs[:n_p], refs[n_p:2 * n_p]
        send_sems, recv_sems, local_sems = refs[2 * n_p:]
        px, py, pc = _place()
        me, sibling = (px, py, pc), (px, py, 1 - pc)
        chips = [(1 - px, py), (px, 1 - py), (1 - px, 1 - py)]

        def win(p, block):
            bx, by, bc = block
            x, axis = pieces[p]
            return _window(out_refs[p], 4 * bx + 2 * by + bc, x.shape[axis], axis)

        def copy(k, p, block, to, local=False):
            return pltpu.make_async_remote_copy(
                src_ref=x_refs[p] if local else win(p, block), dst_ref=win(p, block),
                send_sem=send_sems.at[k, p], recv_sem=recv_sems.at[k, p], device_id=to, device_id_type=MESH_ID)

        every = range(n_p)
        mine = [pltpu.make_async_copy(x_refs[p], win(p, me), local_sems.at[p]) for p in every]
        first = [copy(0, p, me, sibling, local=True) for p in every]
        first += [copy(1 + j, p, me, (*chip, pc), local=True) for j, chip in enumerate(chips) for p in every]
        for cp in mine + first:
            cp.start()
        passed = []
        for j, chip in enumerate(chips):
            for p in every:
                copy(1 + j, p, (*chip, pc), me).wait_recv()
                passed.append(copy(4 + j, p, (*chip, pc), sibling))
                passed[-1].start()
        for p in every:
            copy(0, p, sibling, me).wait_recv()
        for j, chip in enumerate(chips):
            for p in every:
                copy(4 + j, p, (*chip, 1 - pc), me).wait_recv()
        for cp in first + passed:
            cp.wait_send()
        for cp in mine:
            cp.wait()

    def gathered(x, axis):
        return jax.ShapeDtypeStruct(x.shape[:axis] + (N_DEV * x.shape[axis],) + x.shape[axis + 1:], x.dtype)

    spec = VMEM_SPEC if in_vmem else ANY
    return pl.pallas_call(
        body, name=name, in_specs=[spec] * n_p, out_specs=[spec] * n_p, out_shape=[gathered(*pc_) for pc_ in pieces],
        scratch_shapes=[pltpu.SemaphoreType.DMA((7, n_p)), pltpu.SemaphoreType.DMA((7, n_p)),
                        pltpu.SemaphoreType.DMA((n_p,))],
        compiler_params=pltpu.CompilerParams(vmem_limit_bytes=VMEM_LIMIT),
    )(*[x for x, _ in pieces])


def _start_copies(bufs, n_copies, plan, name):
    nb = len(bufs)

    def body(*refs):
        send_sems, recv_sems, token = refs[nb], refs[nb + 1], refs[-1]
        for i, (src, dst, dev) in enumerate(plan(refs[:nb])):
            pltpu.make_async_remote_copy(src_ref=src, dst_ref=dst, send_sem=send_sems.at[i], recv_sem=recv_sems.at[i],
                                         device_id=dev, device_id_type=MESH_ID).start()
        token[...] = jnp.zeros_like(token)

    out = pl.pallas_call(
        body, name=name, in_specs=[HBM_SPEC] * nb,
        out_shape=(pltpu.SemaphoreType.DMA((n_copies,)), pltpu.SemaphoreType.DMA((n_copies,)),
                   *[pltpu.HBM(b.shape, b.dtype) for b in bufs], jax.ShapeDtypeStruct((8, LANE), F32)),
        out_specs=(SEM_SPEC, SEM_SPEC, *[HBM_SPEC] * nb, VMEM_SPEC),
        input_output_aliases={i: 2 + i for i in range(nb)},
        compiler_params=pltpu.CompilerParams(has_side_effects=DATAFLOW),
    )(*[pltpu.with_memory_space_constraint(b, pltpu.HBM) for b in bufs])
    return out[0], out[1], list(out[2:2 + nb]), out[-1]


def _wait_copies(started, after, n_copies, plan, name):
    send_sems, recv_sems, bufs, _ = started
    nb = len(bufs)

    def body(*refs):
        send_ref, recv_ref = refs[nb], refs[nb + 1]
        copies = [pltpu.make_async_remote_copy(src_ref=src, dst_ref=dst, send_sem=send_ref.at[i], recv_sem=recv_ref.at[i],
                                               device_id=dev, device_id_type=MESH_ID)
                  for i, (src, dst, dev) in enumerate(plan(refs[:nb]))]
        for cp in copies:
            cp.wait_send()
        for cp in copies:
            cp.wait_recv()

    out = pl.pallas_call(
        body, name=name, in_specs=[HBM_SPEC] * nb + [SEM_SPEC, SEM_SPEC, ANY],
        out_shape=tuple(pltpu.HBM(b.shape, b.dtype) for b in bufs), out_specs=tuple([HBM_SPEC] * nb),
        input_output_aliases={i: i for i in range(nb)},
        compiler_params=pltpu.CompilerParams(has_side_effects=DATAFLOW),
    )(*bufs, send_sems, recv_sems, after)
    return list(out)


def _after(x, token):
    return x + token[0, 0].astype(x.dtype)


def _other_chips():
    px, py, _ = _place()
    return [(1 - px, py), (px, 1 - py), (1 - px, 1 - py)]


def _exchange(srcs, slots, src_block, target, name):
    n_p = len(srcs)

    def body(*refs):
        src_refs, out_refs, send_sems, recv_sems = refs[:n_p], refs[n_p:2 * n_p], refs[-2], refs[-1]
        copies = [pltpu.make_async_remote_copy(
            src_ref=src_refs[p].at[src_block(s)], dst_ref=out_refs[p].at[s], send_sem=send_sems.at[s, p],
            recv_sem=recv_sems.at[s, p], device_id=target(s), device_id_type=MESH_ID)
            for s in range(slots) for p in range(n_p)]
        for cp in copies:
            cp.start()
        for cp in copies:
            cp.wait_recv()
        for cp in copies:
            cp.wait_send()

    return pl.pallas_call(
        body, name=name, in_specs=[ANY] * n_p, out_specs=[ANY] * n_p,
        out_shape=[jax.ShapeDtypeStruct((slots,) + a.shape[1:], a.dtype) for a in srcs],
        scratch_shapes=[pltpu.SemaphoreType.DMA((slots, n_p)), pltpu.SemaphoreType.DMA((slots, n_p))],
        compiler_params=pltpu.CompilerParams(vmem_limit_bytes=VMEM_LIMIT),
    )(*srcs)


def _blocks_to_sibling(sends, name):
    def src_block(j):
        return 2 * j + 1 - lax.axis_index("c")

    def target(j):
        px, py, pc = _place()
        return (px, py, 1 - pc)

    return _exchange(sends, 4, src_block, target, name)


def _parts_to_chips(parts, name):
    def target(k):
        cx, cy = _other_chips()[k]
        return (cx, cy, lax.axis_index("c"))

    return _exchange(parts, 3, lambda k: k, target, name)


def _pair_sums_for_chips(own, got, name):
    _, r, c = own.shape
    tr = _row_tile(r, 256, 16)

    def body(idx_ref, own_ref, got_ref, o_ref):
        o_ref[...] = (own_ref[...] + got_ref[...].astype(F32)).astype(o_ref.dtype)

    grid_spec = pltpu.PrefetchScalarGridSpec(
        num_scalar_prefetch=1, grid=(3, r // tr),
        in_specs=[pl.BlockSpec((None, tr, c), lambda k, i, idx: (idx[k], i, 0)),
                  pl.BlockSpec((None, tr, c), lambda k, i, idx: (idx[3 + k], i, 0))],
        out_specs=pl.BlockSpec((None, tr, c), lambda k, i, idx: (k, i, 0)))
    chips = [2 * cx + cy for cx, cy in _other_chips()]
    idx = jnp.stack([2 * j + lax.axis_index("c") for j in chips] + chips).astype(jnp.int32)
    return pl.pallas_call(
        body, name=name, grid_spec=grid_spec, out_shape=jax.ShapeDtypeStruct((3, r, c), BF16),
        compiler_params=_params("parallel", "parallel"),
    )(idx, own, got)


def _sum_for_me(own, got_sibling, got_chips, name):
    _, r, c = own.shape
    tr = _row_tile(r, 256, 16)

    def body(idx_ref, own_ref, sib_ref, g0_ref, g1_ref, g2_ref, o_ref):
        acc = own_ref[...] + sib_ref[...].astype(F32)
        for ref in (g0_ref, g1_ref, g2_ref):
            acc = acc + ref[...].astype(F32)
        o_ref[...] = acc

    def part(k):
        return pl.BlockSpec((None, tr, c), lambda i, idx: (k, i, 0))

    grid_spec = pltpu.PrefetchScalarGridSpec(
        num_scalar_prefetch=1, grid=(r // tr,),
        in_specs=[pl.BlockSpec((None, tr, c), lambda i, idx: (idx[0], i, 0)),
                  pl.BlockSpec((None, tr, c), lambda i, idx: (idx[1], i, 0)), part(0), part(1), part(2)],
        out_specs=pl.BlockSpec((tr, c), lambda i, idx: (i, 0)))
    px, py, pc = _place()
    idx = jnp.stack([4 * px + 2 * py + pc, 2 * px + py]).astype(jnp.int32)
    return pl.pallas_call(
        body, name=name, grid_spec=grid_spec, out_shape=jax.ShapeDtypeStruct((r, c), F32),
        compiler_params=_params("parallel"),
    )(idx, own, got_sibling, got_chips, got_chips, got_chips)


def _chip_plan(n_p):
    def plan(refs):
        pc = lax.axis_index("c")
        return [(refs[p].at[k], refs[n_p + p].at[k], (cx, cy, pc))
                for p in range(n_p) for k, (cx, cy) in enumerate(_other_chips())]
    return plan


def _reduce_start(own, sends, tag):
    from_sibling = _blocks_to_sibling(sends, "grads_to_sibling_" + tag)
    pair_sums = [_pair_sums_for_chips(a, b, "grads_pair_sums") for a, b in zip(own, from_sibling)]
    lands = [lax.empty(a.shape, a.dtype) for a in pair_sums]
    started = _start_copies(pair_sums + lands, 3 * len(own), _chip_plan(len(own)), "grads_to_chips_start_" + tag)
    return from_sibling, started


def _reduce_finish(own, from_sibling, started, after, tag):
    n_p = len(own)
    bufs = _wait_copies(started, after, 3 * n_p, _chip_plan(n_p), "grads_to_chips_wait_" + tag)
    return [_sum_for_me(a, b, c, "grads_sum") for a, b, c in zip(own, from_sibling, bufs[n_p:])]


def _gather_plans(pieces):
    n_p = len(pieces)
    dims = [(x.shape[axis], axis) for x, axis in pieces]

    def first(refs):
        px, py, pc = _place()
        targets = [(px, py, 1 - pc)] + [(cx, cy, pc) for cx, cy in _other_chips()]
        return [(refs[p], _window(refs[n_p + p], 4 * px + 2 * py + pc, *dims[p]), to)
                for p in range(n_p) for to in targets]

    def second(refs):
        px, py, pc = _place()
        out = []
        for p in range(n_p):
            for cx, cy in _other_chips():
                win = _window(refs[p], 4 * cx + 2 * cy + pc, *dims[p])
                out.append((win, win, (px, py, 1 - pc)))
        return out

    return first, second


def _flat_rows(a, width):
    return a.reshape(-1, width)


def _full_from_blocks(blocks, name):
    if name in COL_SHARDED:
        _, l, k, nb = blocks.shape
        return jnp.transpose(blocks, (1, 2, 0, 3)).reshape(l, k, N_DEV * nb)
    _, l, rb, n = blocks.shape
    return jnp.transpose(blocks, (1, 0, 2, 3)).reshape(l, N_DEV * rb, n)


def _blocks_from_full(full, name):
    if name in COL_SHARDED:
        l, k, n = full.shape
        return jnp.transpose(full.reshape(l, k, N_DEV, n // N_DEV), (2, 0, 1, 3))
    l, rows, n = full.shape
    return jnp.transpose(full.reshape(l, N_DEV, rows // N_DEV, n), (1, 0, 2, 3))


def _pad_heads(w, width):
    k = w.shape[0]
    return jnp.pad(w.reshape(k, N_HEADS, width), ((0, 0), (0, 0), (0, HEAD_PAD - width))).reshape(k, N_HEADS * HEAD_PAD)


def _unpad_heads(w, width):
    k = w.shape[0]
    return w.reshape(k, N_HEADS, HEAD_PAD)[:, :, :width].reshape(k, N_HEADS * width)


def _layer_operands(full, vec, conv_w_full, l):
    w_in = full['w_in'][l]
    kpe = jnp.pad(w_in[:, LAT:LAT + QK_ROPE], ((0, 0), (QK_NOPE, HEAD_PAD - QK_DIM)))
    w_ukv = full['w_ukv'][l].reshape(KV_LORA, N_HEADS, QK_NOPE + V_DIM)
    w_out = full['w_out'][l]
    d_model = w_out.shape[1]
    wo_attn = jnp.pad(w_out[:N_HEADS * V_DIM].reshape(N_HEADS, V_DIM, d_model),
                      ((0, 0), (0, HEAD_PAD - V_DIM), (0, 0))).reshape(N_HEADS * HEAD_PAD, d_model)
    ops = {
        'w_in': jnp.concatenate([w_in[:, :LAT], kpe, w_in[:, LAT + QK_ROPE:]], axis=1),
        'w_q': _pad_heads(full['w_uq'][l], QK_DIM),
        'w_k': _pad_heads(w_ukv[:, :, :QK_NOPE].reshape(KV_LORA, N_HEADS * QK_NOPE), QK_NOPE),
        'w_v': _pad_heads(w_ukv[:, :, QK_NOPE:].reshape(KV_LORA, N_HEADS * V_DIM), V_DIM),
        'wo_attn': wo_attn,
        'wo_conv': w_out[N_HEADS * V_DIM:],
        'conv_w': jnp.pad(conv_w_full[l], ((0, CONV_HALO - CONV_K), (0, 0))),
        'gq': jnp.pad(vec['q_norm'][l], (0, HEAD_PAD - QK_DIM)),
        'gk': jnp.pad(vec['k_norm'][l], (0, HEAD_PAD - QK_DIM)),
    }
    for n in ('ffn1_norm', 'mix_norm', 'q_latent_norm', 'kv_latent_norm', 'conv_b', 'conv_ln_g', 'conv_ln_b',
              'ffn2_norm', 'post_norm'):
        ops[n] = vec[n][l]
    return ops


def _ffn_fwd(x, g, wgu, wd, fp):
    h = _rms_fwd(x, g, BF16, "rms_fwd_ffn")
    ab, z = _ffn_up(h, wgu, fp, "ffn_up")
    y = _mm(z, wd, res=x, scale=0.5, name="ffn_down")
    return y, (x, h, ab, z)


def _ffn_bwd(dy, dyb, saved, g, wgu, wd, fp):
    x, h, ab, z = saved
    d_wd = _mm(z, dyb, ta=True, scale=0.5, blocks=('row', N_DEV), name="ffn_dwd")
    dab = _ffn_dab(dyb, wd, ab, fp, "ffn_dab")
    d_wgu = _mm(h, dab, ta=True, blocks=('col', N_DEV), name="ffn_dwgu")
    dh = _mm(dab, wgu, tb=True, name="ffn_dh")
    dx, dxb, dg = _rms_bwd(x, g, dh, dy, "rms_bwd_ffn")
    return dx, dxb, dg, d_wgu, d_wd


def _mixer_fwd(x, ops, tabs, after_attention=None):
    h = _rms_fwd(x, ops['mix_norm'], BF16, "rms_fwd_mix")
    p = _mm(h, ops['w_in'], name="mix_in")
    qln, kvln = _lat_norm_fwd(p, ops['q_latent_norm'], ops['kv_latent_norm'], "lat_norm_fwd")
    q_raw = _mm(qln, ops['w_q'], name="mix_q")
    k_raw = _mm(kvln, ops['w_k'], name="mix_k")
    v = _mm(kvln, ops['w_v'], out_dtype=BF16, name="mix_v")
    q, k = _qk_prep_fwd(q_raw, k_raw, p, ops['gq'], ops['gk'], tabs, "qk_prep_fwd")
    o, lse = _attn_fwd(q, k, v, "attn_fwd")
    token = after_attention(o) if after_attention is not None else None
    conv_b = ops['conv_b'] if token is None else _after(ops['conv_b'], token)
    y_conv, cv = _conv_fwd(p, ops['conv_w'], conv_b, ops['conv_ln_g'], ops['conv_ln_b'], "conv_fwd")
    x_attn = _mm(o, ops['wo_attn'], res=x, name="mix_out_attn")
    x_out = _mm(cv, ops['wo_conv'], res=x_attn, name="mix_out_conv")
    return x_out, (x, h, p, qln, kvln, q_raw, k_raw, v, q, k, o, lse, y_conv, cv)


def _mixer_bwd(dx_out, dxb_out, saved, ops, tabs, token=None):
    x, h, p, qln, kvln, q_raw, k_raw, v, q, k, o, lse, y_conv, cv = saved
    g = {}
    do = _mm(dxb_out, ops['wo_attn'] if token is None else _after(ops['wo_attn'], token), tb=True, name="mix_do")
    dcv = _mm(dxb_out, ops['wo_conv'], tb=True, name="mix_dcv")
    g['wo_attn'] = _mm(o, dxb_out, ta=True, name="mix_dwo_attn")
    g['wo_conv'] = _mm(cv, dxb_out, ta=True, name="mix_dwo_conv")
    dq, dk, dv = _attn_bwd(q, k, v, o, lse, do, "attn_bwd")
    dq_raw, dk_raw, dkpe, g['gq'], g['gk'] = _qk_prep_bwd(q_raw, k_raw, p, dq, dk, ops['gq'], ops['gk'], tabs,
                                                          "qk_prep_bwd")
    g['w_q'] = _mm(qln, dq_raw, ta=True, name="mix_dwq")
    g['w_k'] = _mm(kvln, dk_raw, ta=True, name="mix_dwk")
    g['w_v'] = _mm(kvln, dv, ta=True, name="mix_dwv")
    dqln = _mm(dq_raw, ops['w_q'], tb=True, name="mix_dqln")
    dkvln = _mm(dk_raw, ops['w_k'], tb=True, name="mix_dkvln_k")
    dkvln = _mm(dv, ops['w_v'], tb=True, res=dkvln, name="mix_dkvln_v")
    dp_lat, g['q_latent_norm'], g['kv_latent_norm'] = _lat_norm_bwd(
        p, ops['q_latent_norm'], ops['kv_latent_norm'], dqln, dkvln, "lat_norm_bwd")
    dy_conv, g['conv_ln_g'], g['conv_ln_b'], g['conv_b'] = _conv_bwd_ln(
        y_conv, dcv, ops['conv_ln_g'], ops['conv_ln_b'], "conv_bwd_ln")
    dag, g['conv_w'] = _conv_bwd_taps(p, dy_conv, ops['conv_w'], "conv_bwd_taps")
    dp = jnp.concatenate([dp_lat, dkpe, dag], axis=1)
    g['w_in'] = _mm(h, dp, ta=True, name="mix_dw_in")
    dh = _mm(dp, ops['w_in'], tb=True, name="mix_dh")
    dx, dxb, g['mix_norm'] = _rms_bwd(x, ops['mix_norm'], dh, dx_out, "rms_bwd_mix")
    return dx, dxb, g


def _layer_grads_to_params(g):
    d_w_in = g['w_in']
    d_wk = _unpad_heads(g['w_k'], QK_NOPE).reshape(KV_LORA, N_HEADS, QK_NOPE)
    d_wv = _unpad_heads(g['w_v'], V_DIM).reshape(KV_LORA, N_HEADS, V_DIM)
    d_model = g['wo_attn'].shape[1]
    d_wo_attn = g['wo_attn'].reshape(N_HEADS, HEAD_PAD, d_model)[:, :V_DIM].reshape(N_HEADS * V_DIM, d_model)
    return {
        'ffn1_norm': g['ffn1_norm'], 'mix_norm': g['mix_norm'],
        'w_in': jnp.concatenate([d_w_in[:, :LAT], d_w_in[:, LAT + QK_NOPE:LAT + QK_DIM], d_w_in[:, P_A:]], axis=1),
        'q_latent_norm': g['q_latent_norm'], 'w_uq': _unpad_heads(g['w_q'], QK_DIM),
        'kv_latent_norm': g['kv_latent_norm'],
        'w_ukv': jnp.concatenate([d_wk, d_wv], axis=2).reshape(KV_LORA, N_HEADS * (QK_NOPE + V_DIM)),
        'q_norm': g['gq'][:QK_DIM], 'k_norm': g['gk'][:QK_DIM], 'conv_w': g['conv_w'][:CONV_K],
        'conv_b': g['conv_b'], 'conv_ln_g': g['conv_ln_g'], 'conv_ln_b': g['conv_ln_b'],
        'w_out': jnp.concatenate([d_wo_attn, g['wo_conv']], axis=0),
        'ffn2_norm': g['ffn2_norm'], 'post_norm': g['post_norm'],
    }


def kernel(x, ffn1_norm, ffn1_w_gate, ffn1_w_up, ffn1_w_down, mix_norm, w_in, q_latent_norm, w_uq, kv_latent_norm, w_ukv, q_norm, k_norm, conv_w, conv_b, conv_ln_g, conv_ln_b, w_out, ffn2_norm, ffn2_w_gate, ffn2_w_up, ffn2_w_down, post_norm, loss_target, m_ffn1_norm, m_ffn1_w_gate, m_ffn1_w_up, m_ffn1_w_down, m_mix_norm, m_w_in, m_q_latent_norm, m_w_uq, m_kv_latent_norm, m_w_ukv, m_q_norm, m_k_norm, m_conv_w, m_conv_b, m_conv_ln_g, m_conv_ln_b, m_w_out, m_ffn2_norm, m_ffn2_w_gate, m_ffn2_w_up, m_ffn2_w_down, m_post_norm, v_ffn1_norm, v_ffn1_w_gate, v_ffn1_w_up, v_ffn1_w_down, v_mix_norm, v_w_in, v_q_latent_norm, v_w_uq, v_kv_latent_norm, v_w_ukv, v_q_norm, v_k_norm, v_conv_w, v_conv_b, v_conv_ln_g, v_conv_ln_b, v_w_out, v_ffn2_norm, v_ffn2_w_gate, v_ffn2_w_up, v_ffn2_w_down, v_post_norm):
    args = locals()
    w = {n: args[n] for n in WEIGHTS}
    mom = {n: args["m_" + n] for n in WEIGHTS}
    var = {n: args["v_" + n] for n in WEIGHTS}
    depth = ffn1_norm.shape[0]
    x0 = x.reshape(x.shape[-2:])
    target = loss_target.reshape(loss_target.shape[-2:])
    t, d_model = x0.shape
    my_block = 4 * lax.axis_index("x") + 2 * lax.axis_index("y") + lax.axis_index("c")

    fb = ffn1_w_gate.shape[-1]
    fp = -(-fb // LANE) * LANE
    ffns = [(l, f) for l in range(depth) for f in (1, 2)]
    pad_cols = lambda a: jnp.pad(a, ((0, 0), (0, fp - fb)))
    gu_local = {(l, f): jnp.concatenate([pad_cols(w[f'ffn{f}_w_gate'][l]), pad_cols(w[f'ffn{f}_w_up'][l])],
                                        axis=1).astype(BF16) for l, f in ffns}
    dn_local = {(l, f): jnp.pad(w[f'ffn{f}_w_down'][l], ((0, fp - fb), (0, 0))).astype(BF16) for l, f in ffns}
    rows_of = {n: w[n].size // d_model for n in REST}
    rest_local = jnp.concatenate([_flat_rows(w[n].astype(BF16), d_model) for n in REST], axis=0)
    n_rest = rest_local.shape[0]
    first_ffn, later = ffns[0], ffns[1:]
    got = _all_gather([(gu_local[first_ffn], 1), (dn_local[first_ffn], 0), (rest_local, 0)], "gather_first")
    wgu, wd = {first_ffn: got[0]}, {first_ffn: got[1]}
    gathered = got[2].reshape(N_DEV, n_rest, d_model)
    later_pieces = [(gu_local[q], 1) for q in later] + [(dn_local[q], 0) for q in later]
    n_later = len(later_pieces)
    gather_plan, forward_plan = _gather_plans(later_pieces)

    def landing(a, axis):
        shape = a.shape[:axis] + (N_DEV * a.shape[axis],) + a.shape[axis + 1:]
        return lax.dynamic_update_slice_in_dim(lax.empty(shape, a.dtype), a, my_block * a.shape[axis], axis)

    gather_later = _start_copies([a for a, _ in later_pieces] + [landing(a, ax) for a, ax in later_pieces],
                                 4 * n_later, gather_plan, "gather_later_start")
    full, start = {}, 0
    for n in REST:
        blocks = gathered[:, start:start + rows_of[n]].reshape((N_DEV,) + w[n].shape)
        full[n] = _full_from_blocks(blocks, n)
        start += rows_of[n]
    cw = conv_w.reshape(-1)
    cw_rows = -(-cw.size // (8 * LANE)) * 8
    cw_flat = jnp.pad(cw, (0, cw_rows * LANE - cw.size)).reshape(cw_rows, LANE)
    cw_all = _all_gather([(cw_flat, 0)], "gather_conv_w", in_vmem=True)[0].reshape(N_DEV, cw_rows * LANE)[:, :cw.size]
    conv_w_full = jnp.transpose(cw_all.reshape((N_DEV,) + conv_w.shape), (1, 2, 0, 3)).reshape(depth, CONV_K, CONV_W)
    vec = {n: w[n] for n in VECTORS}
    ops = [_layer_operands(full, vec, conv_w_full, l) for l in range(depth)]
    tabs = _rope_tables(t)

    saved, xl = [], x0
    forward_later = []

    def pass_on_later(o_attn):
        lands = _wait_copies(gather_later, o_attn, 4 * n_later, gather_plan, "gather_later_wait")[n_later:]
        forward_later.append(_start_copies(lands, 3 * n_later, forward_plan, "gather_later_forward_start"))
        return forward_later[0][3]

    for l in range(depth):
        o = ops[l]
        if l == 0:
            x1, s1 = _ffn_fwd(xl, _after(o['ffn1_norm'], gather_later[3]), wgu[l, 1], wd[l, 1], fp)
            x2, sm = _mixer_fwd(x1, o, tabs, after_attention=pass_on_later)
            lands = _wait_copies(forward_later[0], x2, 3 * n_later, forward_plan, "gather_later_forward_wait")
            wgu.update(zip(later, lands[:len(later)]))
            wd.update(zip(later, lands[len(later):]))
        else:
            x1, s1 = _ffn_fwd(xl, o['ffn1_norm'], wgu[l, 1], wd[l, 1], fp)
            x2, sm = _mixer_fwd(x1, o, tabs)
        x3, s2 = _ffn_fwd(x2, o['ffn2_norm'], wgu[l, 2], wd[l, 2], fp)
        xl = _rms_fwd(x3, o['post_norm'], F32, "rms_fwd_post")
        saved.append((s1, sm, s2, x3))
    loss_part, dx = _loss_head(xl, target, "loss_head")
    loss = lax.psum(loss_part[0, 0], ("x", "y", "c"))

    grads, d_gu, d_dn, mine_gu, mine_dn = [None] * depth, {}, {}, {}, {}

    def start_group(q):
        tag = f"{q[0]}{q[1]}"
        own_q = [d_gu[q][0], d_dn[q][0]]
        from_sibling, started = _reduce_start(own_q, [d_gu[q][1], d_dn[q][1]], tag)
        return (q, tag, own_q, from_sibling, started), started[3]

    def finish_group(group, after):
        q, tag, own_q, from_sibling, started = group
        mine_gu[q], mine_dn[q] = _reduce_finish(own_q, from_sibling, started, after, tag)

    in_flight, token = None, None
    for l in reversed(range(depth)):
        o = ops[l]
        s1, sm, s2, x3 = saved[l]
        g = {}
        post_gain = o['post_norm'] if token is None else _after(o['post_norm'], token)
        dx, dxb, g['post_norm'] = _rms_bwd(x3, post_gain, dx, None, "rms_bwd_post")
        dx, dxb, g['ffn2_norm'], d_gu[l, 2], d_dn[l, 2] = _ffn_bwd(dx, dxb, s2, o['ffn2_norm'], wgu[l, 2], wd[l, 2], fp)
        if in_flight is not None:
            finish_group(in_flight, dx)
        in_flight, token = start_group((l, 2))
        dx, dxb, gm = _mixer_bwd(dx, dxb, sm, o, tabs, token)
        g.update(gm)
        finish_group(in_flight, dx)
        in_flight, token = None, None
        dx, dxb, g['ffn1_norm'], d_gu[l, 1], d_dn[l, 1] = _ffn_bwd(dx, dxb, s1, o['ffn1_norm'], wgu[l, 1], wd[l, 1], fp)
        if l > 0:
            in_flight, token = start_group((l, 1))
        grads[l] = _layer_grads_to_params(g)
    grad_x = dx.reshape(x.shape)
    part = {n: jnp.stack([grads[l][n] for l in range(depth)]) for n in grads[0]}

    rest_own = jnp.concatenate([_blocks_from_full(part[n], n).reshape(N_DEV, rows_of[n], d_model) for n in REST],
                               axis=1)
    own = [d_gu[first_ffn][0], d_dn[first_ffn][0], rest_own]
    sends = [d_gu[first_ffn][1], d_dn[first_ffn][1], rest_own.astype(BF16)]
    from_sibling = _blocks_to_sibling(sends, "grads_to_sibling")
    pair_sums = [_pair_sums_for_chips(a, b, "grads_pair_sums") for a, b in zip(own, from_sibling)]
    from_chips = _parts_to_chips(pair_sums, "grads_to_chips")
    mine = [_sum_for_me(a, b, c, "grads_sum") for a, b, c in zip(own, from_sibling, from_chips)]
    mine_gu[first_ffn], mine_dn[first_ffn] = mine[0], mine[1]
    grad = {}
    for f in (1, 2):
        grad[f'ffn{f}_w_gate'] = jnp.stack([mine_gu[l, f][:, :fb] for l in range(depth)])
        grad[f'ffn{f}_w_up'] = jnp.stack([mine_gu[l, f][:, fp:fp + fb] for l in range(depth)])
        grad[f'ffn{f}_w_down'] = jnp.stack([mine_dn[l, f][:fb] for l in range(depth)])
    start = 0
    for n in REST:
        grad[n] = mine[-1][start:start + rows_of[n]].reshape(w[n].shape)
        start += rows_of[n]

    small = jnp.concatenate([part[n].reshape(-1) for n in VECTORS] + [part['conv_w'].reshape(-1)])
    s_rows = -(-small.size // (8 * LANE)) * 8
    small = jnp.pad(small, (0, s_rows * LANE - small.size)).reshape(s_rows, LANE)
    small_all = _all_gather([(small, 0)], "gather_small_grads", in_vmem=True)[0]
    small_sum = _sum_parts([small_all[k * s_rows:(k + 1) * s_rows] for k in range(N_DEV)], "sum_small_grads")
    small_sum = small_sum.reshape(-1)
    start = 0
    for n in VECTORS:
        grad[n] = small_sum[start:start + w[n].size].reshape(w[n].shape)
        start += w[n].size
    cw_grad = small_sum[start:start + depth * CONV_K * CONV_W].reshape(depth, CONV_K, CONV_W)
    nb = conv_w.shape[-1]
    grad['conv_w'] = lax.dynamic_slice_in_dim(cw_grad, my_block * nb, nb, axis=2)

    delta, new_m, new_v = {}, {}, {}
    for n in BIG + ['conv_w']:
        shp = w[n].shape
        two_d = lambda a: a.reshape(-1, shp[-1])
        dl, mn, vn = _adamw(two_d(w[n]), two_d(grad[n]), two_d(mom[n]), two_d(var[n]), "adamw_" + n)
        delta[n], new_m[n], new_v[n] = dl.reshape(shp), mn.reshape(shp), vn.reshape(shp)
    vcat = lambda src: jnp.concatenate([src[n].reshape(-1) for n in VECTORS]).reshape(-1, LANE)
    dl, mn, vn = _adamw(vcat(w), vcat(grad), vcat(mom), vcat(var), "adamw_vectors")
    start = 0
    for n in VECTORS:
        sl = lambda a: a.reshape(-1)[start:start + w[n].size].reshape(w[n].shape)
        delta[n], new_m[n], new_v[n] = sl(dl), sl(mn), sl(vn)
        start += w[n].size

    return (loss, grad_x, *[grad[n] for n in WEIGHTS], *[delta[n] for n in WEIGHTS],
            *[new_m[n] for n in WEIGHTS], *[new_v[n] for n in WEIGHTS])
```

```python
import functools

import jax
import jax.numpy as jnp
from jax import lax
from jax.experimental import pallas as pl
from jax.experimental.pallas import tpu as pltpu

F32, BF16 = jnp.float32, jnp.bfloat16

N_DEV = 8
N_HEADS = 8
QK_NOPE, QK_ROPE, V_DIM = 64, 32, 64
QK_DIM = QK_NOPE + QK_ROPE
HEAD_PAD = 128
Q_LORA, KV_LORA = 384, 256
LAT = Q_LORA + KV_LORA
CONV_W, CONV_K = 512, 31
CONV_HALO = 32
CHUNK = 64
ROPE_THETA = 10000.0
EPS = 1e-6
ATTN_SCALE = QK_DIM ** -0.5
P_KPE = LAT
P_A = LAT + HEAD_PAD
P_G = P_A + CONV_W
P_COLS = P_G + CONV_W

ADAM_LR, ADAM_B1, ADAM_B2, ADAM_EPS, ADAM_WD, ADAM_STEP = 0.001, 0.9, 0.999, 1e-08, 0.01, 10

V7X_VMEM_BYTES = 64 << 20
VMEM_LIMIT = V7X_VMEM_BYTES - (8 << 20)
MM_VMEM_BUDGET = 36 << 20
LANE = 128
ROW_TILE = 512
ATTN_BLOCK = 512
CONV_TILE = 256

MESH_ID = pl.DeviceIdType.MESH
ANY = pl.BlockSpec(memory_space=pl.ANY)
VMEM_SPEC = pl.BlockSpec(memory_space=pltpu.VMEM)
HBM_SPEC = pl.BlockSpec(memory_space=pltpu.HBM)
SEM_SPEC = pl.BlockSpec(memory_space=pltpu.SEMAPHORE)
DATAFLOW = pltpu.SideEffectType.DATAFLOW_SIDE_EFFECTING

WEIGHTS = ['ffn1_norm', 'ffn1_w_gate', 'ffn1_w_up', 'ffn1_w_down', 'mix_norm', 'w_in', 'q_latent_norm', 'w_uq',
           'kv_latent_norm', 'w_ukv', 'q_norm', 'k_norm', 'conv_w', 'conv_b', 'conv_ln_g', 'conv_ln_b', 'w_out',
           'ffn2_norm', 'ffn2_w_gate', 'ffn2_w_up', 'ffn2_w_down', 'post_norm']
COL_SHARDED = ['ffn1_w_gate', 'ffn1_w_up', 'w_in', 'w_uq', 'w_ukv', 'ffn2_w_gate', 'ffn2_w_up']
ROW_SHARDED = ['ffn1_w_down', 'w_out', 'ffn2_w_down']
REST = ['w_in', 'w_uq', 'w_ukv', 'w_out']
BIG = ['ffn1_w_gate', 'ffn1_w_up', 'ffn1_w_down', 'w_in', 'w_uq', 'w_ukv', 'w_out', 'ffn2_w_gate', 'ffn2_w_up',
       'ffn2_w_down']
VECTORS = ['ffn1_norm', 'mix_norm', 'q_latent_norm', 'kv_latent_norm', 'q_norm', 'k_norm', 'conv_b', 'conv_ln_g',
           'conv_ln_b', 'ffn2_norm', 'post_norm']


def _params(*sem):
    return pltpu.CompilerParams(dimension_semantics=sem if sem else None, vmem_limit_bytes=VMEM_LIMIT)


def _tile(n, cap):
    if n <= cap:
        return n
    best = 0
    for d in range(LANE, cap + 1, LANE):
        if n % d == 0:
            best = d
    assert best, (n, cap)
    return best


def _row_tile(n, cap=ROW_TILE, mult=8):
    if n <= cap:
        return n
    best = 0
    for d in range(mult, cap + 1, mult):
        if n % d == 0:
            best = d
    assert best, (n, cap)
    return best


def _mm(a, b, *, name, ta=False, tb=False, res=None, scale=1.0, out_dtype=F32, tm=None, tn=None, blocks=None):
    (kdim, m) = a.shape if ta else a.shape[::-1]
    (n, kb) = b.shape if tb else b.shape[::-1]
    assert kdim == kb, (a.shape, b.shape, ta, tb)
    tm, tn = tm or _tile(m, 512), tn or _tile(n, 1024)
    if blocks is not None:
        tm, tn = (tm, n // blocks[1]) if blocks[0] == 'col' else (m // blocks[1], tn)
    size = lambda arr: jnp.dtype(arr.dtype).itemsize
    out_bytes = tm * tn * ((6 if blocks is not None else jnp.dtype(out_dtype).itemsize) + (4 if res is not None else 0))

    def vmem_need(tk):
        return 2 * (tm * tk * size(a) + tk * tn * size(b) + out_bytes) + (tm * tn * 4 if tk < kdim else 0)

    tk = kdim
    for cand in [d for d in range(kdim - LANE, 0, -LANE) if kdim % d == 0]:
        if vmem_need(tk) <= MM_VMEM_BUDGET:
            break
        tk = cand
    nk = kdim // tk
    n_in = 3 if res is not None else 2
    n_out = 2 if blocks is not None else 1
    dims = (((0 if ta else 1,), (1 if tb else 0,)), ((), ()))

    def body(*refs):
        a_ref, b_ref = refs[0], refs[1]
        r_ref = refs[2] if res is not None else None
        o_refs = refs[n_in:n_in + n_out]
        acc_ref = refs[-1] if nk > 1 else None
        part = lax.dot_general(a_ref[...].astype(BF16), b_ref[...].astype(BF16), dims, preferred_element_type=F32)

        def finish(acc):
            if scale != 1.0:
                acc = acc * scale
            if r_ref is not None:
                acc = r_ref[...] + acc
            for o_ref in o_refs:
                o_ref[...] = acc.astype(o_ref.dtype)

        if nk == 1:
            finish(part)
        else:
            k = pl.program_id(2)

            @pl.when(k == 0)
            def _():
                acc_ref[...] = part

            @pl.when(k > 0)
            def _():
                acc_ref[...] += part

            @pl.when(k == nk - 1)
            def _():
                finish(acc_ref[...])

    a_spec = pl.BlockSpec((tk, tm), lambda i, j, k: (k, i)) if ta else pl.BlockSpec((tm, tk), lambda i, j, k: (i, k))
    b_spec = pl.BlockSpec((tn, tk), lambda i, j, k: (j, k)) if tb else pl.BlockSpec((tk, tn), lambda i, j, k: (k, j))
    plain = pl.BlockSpec((tm, tn), lambda i, j, k: (i, j))
    if blocks is None:
        out_specs, out_shape = plain, jax.ShapeDtypeStruct((m, n), out_dtype)
    else:
        if blocks[0] == 'col':
            o_spec, shp = pl.BlockSpec((None, tm, tn), lambda i, j, k: (j, i, 0)), (blocks[1], m, tn)
        else:
            o_spec, shp = pl.BlockSpec((None, tm, tn), lambda i, j, k: (i, 0, j)), (blocks[1], tm, n)
        out_specs, out_shape = (o_spec, o_spec), (jax.ShapeDtypeStruct(shp, F32), jax.ShapeDtypeStruct(shp, BF16))
    in_specs = [a_spec, b_spec] + ([plain] if res is not None else [])
    args = (a, b) + ((res,) if res is not None else ())
    return pl.pallas_call(
        body, name=name, grid=(m // tm, n // tn, nk), in_specs=in_specs, out_specs=out_specs, out_shape=out_shape,
        scratch_shapes=[pltpu.VMEM((tm, tn), F32)] if nk > 1 else [],
        compiler_params=_params("parallel", "parallel", "arbitrary"),
    )(*args)


def _rms_fwd(x, g, out_dtype, name):
    t, d = x.shape
    tm = _row_tile(t)

    def body(x_ref, g_ref, o_ref):
        xv = x_ref[...]
        r = lax.rsqrt(jnp.mean(xv * xv, axis=-1, keepdims=True) + EPS)
        o_ref[...] = (xv * r * g_ref[...]).astype(o_ref.dtype)

    return pl.pallas_call(
        body, name=name, grid=(t // tm,),
        in_specs=[pl.BlockSpec((tm, d), lambda i: (i, 0)), pl.BlockSpec((1, d), lambda i: (0, 0))],
        out_specs=pl.BlockSpec((tm, d), lambda i: (i, 0)),
        out_shape=jax.ShapeDtypeStruct((t, d), out_dtype), compiler_params=_params("parallel"),
    )(x, g.reshape(1, d))


def _rms_bwd(x, g, dh, res, name):
    t, d = x.shape
    tm = _row_tile(t)

    def body(*refs):
        x_ref, g_ref, dh_ref = refs[:3]
        r_ref = refs[3] if res is not None else None
        dx_ref, dxb_ref, dg_ref = refs[-3:]
        xv, dhv = x_ref[...], dh_ref[...]
        r = lax.rsqrt(jnp.mean(xv * xv, axis=-1, keepdims=True) + EPS)
        y = xv * r
        dy = dhv * g_ref[...]
        dx = r * (dy - y * jnp.mean(dy * y, axis=-1, keepdims=True))
        if r_ref is not None:
            dx = r_ref[...] + dx
        dx_ref[...] = dx
        dxb_ref[...] = dx.astype(BF16)

        @pl.when(pl.program_id(0) == 0)
        def _():
            dg_ref[...] = jnp.zeros_like(dg_ref)

        dg_ref[...] += jnp.sum(dhv * y, axis=0, keepdims=True)

    row = pl.BlockSpec((tm, d), lambda i: (i, 0))
    vec = pl.BlockSpec((1, d), lambda i: (0, 0))
    args = (x, g.reshape(1, d), dh) + ((res,) if res is not None else ())
    dx, dxb, dg = pl.pallas_call(
        body, name=name, grid=(t // tm,), in_specs=[row, vec, row] + ([row] if res is not None else []),
        out_specs=(row, row, vec),
        out_shape=(jax.ShapeDtypeStruct((t, d), F32), jax.ShapeDtypeStruct((t, d), BF16),
                   jax.ShapeDtypeStruct((1, d), F32)),
        compiler_params=_params("arbitrary"),
    )(*args)
    return dx, dxb, dg.reshape(d)


FFN_PAIR = 2


def _ffn_up(h, wgu, fp, name):
    t, d = h.shape
    tm, tn = _tile(t, 512), FFN_PAIR * 2 * fp
    nj = wgu.shape[1] // tn

    def body(h_ref, w_ref, ab_ref, z_ref):
        ab = jnp.dot(h_ref[...], w_ref[...], preferred_element_type=F32)
        ab_ref[...] = ab.astype(ab_ref.dtype)
        for e in range(FFN_PAIR):
            av, bv = ab[:, 2 * fp * e:2 * fp * e + fp], ab[:, 2 * fp * e + fp:2 * fp * (e + 1)]
            z_ref[:, fp * e:fp * (e + 1)] = (av * jax.nn.sigmoid(av) * bv).astype(z_ref.dtype)

    return pl.pallas_call(
        body, name=name, grid=(nj, t // tm),
        in_specs=[pl.BlockSpec((tm, d), lambda j, i: (i, 0)), pl.BlockSpec((d, tn), lambda j, i: (0, j))],
        out_specs=(pl.BlockSpec((tm, tn), lambda j, i: (i, j)), pl.BlockSpec((tm, tn // 2), lambda j, i: (i, j))),
        out_shape=(jax.ShapeDtypeStruct((t, wgu.shape[1]), BF16), jax.ShapeDtypeStruct((t, wgu.shape[1] // 2), BF16)),
        compiler_params=_params("parallel", "parallel"),
    )(h, wgu)


def _ffn_dab(dyb, wd, ab, fp, name):
    t, d = dyb.shape
    tm, tn = _tile(t, 512), FFN_PAIR * 2 * fp
    nj = ab.shape[1] // tn

    def body(dy_ref, wd_ref, ab_ref, dab_ref):
        dz = _dot_nt(dy_ref[...], wd_ref[...]) * 0.5
        for e in range(FFN_PAIR):
            av = ab_ref[:, 2 * fp * e:2 * fp * e + fp].astype(F32)
            bv = ab_ref[:, 2 * fp * e + fp:2 * fp * (e + 1)].astype(F32)
            dze = dz[:, fp * e:fp * (e + 1)]
            s = jax.nn.sigmoid(av)
            dab_ref[:, 2 * fp * e:2 * fp * e + fp] = (dze * bv * (s * (1.0 + av * (1.0 - s)))).astype(dab_ref.dtype)
            dab_ref[:, 2 * fp * e + fp:2 * fp * (e + 1)] = (dze * (av * s)).astype(dab_ref.dtype)

    return pl.pallas_call(
        body, name=name, grid=(nj, t // tm),
        in_specs=[pl.BlockSpec((tm, d), lambda j, i: (i, 0)), pl.BlockSpec((tn // 2, d), lambda j, i: (j, 0)),
                  pl.BlockSpec((tm, tn), lambda j, i: (i, j))],
        out_specs=pl.BlockSpec((tm, tn), lambda j, i: (i, j)),
        out_shape=jax.ShapeDtypeStruct(ab.shape, BF16), compiler_params=_params("parallel", "parallel"),
    )(dyb, wd, ab)


def _lat_norm_fwd(p, g_q, g_kv, name):
    t = p.shape[0]
    tm = _row_tile(t)

    def body(p_ref, gq_ref, gkv_ref, q_ref, kv_ref):
        for lo, hi, g_ref, o_ref in ((0, Q_LORA, gq_ref, q_ref), (Q_LORA, LAT, gkv_ref, kv_ref)):
            xv = p_ref[:, lo:hi]
            r = lax.rsqrt(jnp.mean(xv * xv, axis=-1, keepdims=True) + EPS)
            o_ref[...] = (xv * r * g_ref[...]).astype(o_ref.dtype)

    return pl.pallas_call(
        body, name=name, grid=(t // tm,),
        in_specs=[pl.BlockSpec((tm, P_COLS), lambda i: (i, 0)), pl.BlockSpec((1, Q_LORA), lambda i: (0, 0)),
                  pl.BlockSpec((1, KV_LORA), lambda i: (0, 0))],
        out_specs=(pl.BlockSpec((tm, Q_LORA), lambda i: (i, 0)), pl.BlockSpec((tm, KV_LORA), lambda i: (i, 0))),
        out_shape=(jax.ShapeDtypeStruct((t, Q_LORA), BF16), jax.ShapeDtypeStruct((t, KV_LORA), BF16)),
        compiler_params=_params("parallel"),
    )(p, g_q.reshape(1, Q_LORA), g_kv.reshape(1, KV_LORA))


def _lat_norm_bwd(p, g_q, g_kv, dq, dkv, name):
    t = p.shape[0]
    tm = _row_tile(t)

    def body(p_ref, gq_ref, gkv_ref, dq_ref, dkv_ref, dp_ref, dgq_ref, dgkv_ref):
        first = pl.program_id(0) == 0
        for lo, hi, g_ref, d_ref, dg_ref in ((0, Q_LORA, gq_ref, dq_ref, dgq_ref),
                                             (Q_LORA, LAT, gkv_ref, dkv_ref, dgkv_ref)):
            xv, dhv = p_ref[:, lo:hi], d_ref[...]
            r = lax.rsqrt(jnp.mean(xv * xv, axis=-1, keepdims=True) + EPS)
            y = xv * r
            dy = dhv * g_ref[...]
            dp_ref[:, lo:hi] = r * (dy - y * jnp.mean(dy * y, axis=-1, keepdims=True))

            @pl.when(first)
            def _():
                dg_ref[...] = jnp.zeros_like(dg_ref)

            dg_ref[...] += jnp.sum(dhv * y, axis=0, keepdims=True)

    vq = pl.BlockSpec((1, Q_LORA), lambda i: (0, 0))
    vkv = pl.BlockSpec((1, KV_LORA), lambda i: (0, 0))
    dp, dgq, dgkv = pl.pallas_call(
        body, name=name, grid=(t // tm,),
        in_specs=[pl.BlockSpec((tm, P_COLS), lambda i: (i, 0)), vq, vkv,
                  pl.BlockSpec((tm, Q_LORA), lambda i: (i, 0)), pl.BlockSpec((tm, KV_LORA), lambda i: (i, 0))],
        out_specs=(pl.BlockSpec((tm, LAT), lambda i: (i, 0)), vq, vkv),
        out_shape=(jax.ShapeDtypeStruct((t, LAT), F32), jax.ShapeDtypeStruct((1, Q_LORA), F32),
                   jax.ShapeDtypeStruct((1, KV_LORA), F32)),
        compiler_params=_params("arbitrary"),
    )(p, g_q.reshape(1, Q_LORA), g_kv.reshape(1, KV_LORA), dq, dkv)
    return dp, dgq.reshape(Q_LORA), dgkv.reshape(KV_LORA)


def _rope_tables(t):
    half = QK_ROPE // 2
    pos = jnp.arange(t, dtype=F32)
    inv_freq = 1.0 / (ROPE_THETA ** (jnp.arange(0, QK_ROPE, 2, dtype=F32) / QK_ROPE))
    ang = pos[:, None] * inv_freq[None, :]
    cos, sin = jnp.cos(ang), jnp.sin(ang)
    z = lambda n: jnp.zeros((t, n), F32)
    c_tab = jnp.concatenate([jnp.ones((t, QK_NOPE), F32), cos, cos, z(HEAD_PAD - QK_DIM)], axis=1)
    sa_tab = jnp.concatenate([z(QK_NOPE), -sin, z(half), z(HEAD_PAD - QK_DIM)], axis=1)
    sb_tab = jnp.concatenate([z(QK_NOPE), z(half), sin, z(HEAD_PAD - QK_DIM)], axis=1)
    return c_tab, sa_tab, sb_tab


def _rope(x, c, sa, sb):
    half = QK_ROPE // 2
    return x * c + pltpu.roll(x, HEAD_PAD - half, 1) * sa + pltpu.roll(x, half, 1) * sb


def _rope_t(d, c, sa, sb):
    half = QK_ROPE // 2
    return d * c + pltpu.roll(d * sa, half, 1) + pltpu.roll(d * sb, HEAD_PAD - half, 1)


def _head_rms(x):
    r = lax.rsqrt(jnp.sum(x * x, axis=-1, keepdims=True) * (1.0 / QK_DIM) + EPS)
    return x * r, r


def _qk_prep_fwd(q_raw, k_raw, p, gq, gk, tabs, name):
    t = q_raw.shape[0]
    tm = _row_tile(t)

    def body(q_ref, k_ref, p_ref, gq_ref, gk_ref, c_ref, sa_ref, sb_ref, qo_ref, ko_ref):
        c, sa, sb = c_ref[...], sa_ref[...], sb_ref[...]
        qn, _ = _head_rms(q_ref[...])
        qo_ref[...] = _rope(qn * gq_ref[...], c, sa, sb).astype(qo_ref.dtype)
        kn, _ = _head_rms(k_ref[...] + p_ref[...])
        ko_ref[...] = _rope(kn * gk_ref[...], c, sa, sb).astype(ko_ref.dtype)

    head = pl.BlockSpec((tm, HEAD_PAD), lambda i, h: (i, h))
    tab = pl.BlockSpec((tm, HEAD_PAD), lambda i, h: (i, 0))
    vec = pl.BlockSpec((1, HEAD_PAD), lambda i, h: (0, 0))
    kpe = pl.BlockSpec((tm, HEAD_PAD), lambda i, h: (i, P_KPE // HEAD_PAD))
    return pl.pallas_call(
        body, name=name, grid=(t // tm, N_HEADS), in_specs=[head, head, kpe, vec, vec, tab, tab, tab],
        out_specs=(head, head),
        out_shape=(jax.ShapeDtypeStruct(q_raw.shape, BF16), jax.ShapeDtypeStruct(k_raw.shape, BF16)),
        compiler_params=_params("parallel", "parallel"),
    )(q_raw, k_raw, p, gq.reshape(1, HEAD_PAD), gk.reshape(1, HEAD_PAD), *tabs)


def _qk_prep_bwd(q_raw, k_raw, p, dq, dk, gq, gk, tabs, name):
    t = q_raw.shape[0]
    tm = _row_tile(t)

    def body(q_ref, k_ref, p_ref, dq_ref, dk_ref, gq_ref, gk_ref, c_ref, sa_ref, sb_ref,
             dqr_ref, dkr_ref, dkpe_ref, dgq_ref, dgk_ref):
        i, h = pl.program_id(0), pl.program_id(1)
        c, sa, sb = c_ref[...], sa_ref[...], sb_ref[...]

        def one(x, d, g_ref, dg_ref):
            n, r = _head_rms(x)
            dng = _rope_t(d, c, sa, sb)

            @pl.when(jnp.logical_and(i == 0, h == 0))
            def _():
                dg_ref[...] = jnp.zeros_like(dg_ref)

            dg_ref[...] += jnp.sum(dng * n, axis=0, keepdims=True)
            dn = dng * g_ref[...]
            return r * (dn - n * (jnp.sum(dn * n, axis=-1, keepdims=True) * (1.0 / QK_DIM)))

        dqr_ref[...] = one(q_ref[...], dq_ref[...], gq_ref, dgq_ref)
        dkr = one(k_ref[...] + p_ref[...], dk_ref[...], gk_ref, dgk_ref)
        dkr_ref[...] = dkr

        @pl.when(h == 0)
        def _():
            dkpe_ref[...] = dkr

        @pl.when(h > 0)
        def _():
            dkpe_ref[...] += dkr

    head = pl.BlockSpec((tm, HEAD_PAD), lambda i, h: (i, h))
    tab = pl.BlockSpec((tm, HEAD_PAD), lambda i, h: (i, 0))
    vec = pl.BlockSpec((1, HEAD_PAD), lambda i, h: (0, 0))
    kpe = pl.BlockSpec((tm, HEAD_PAD), lambda i, h: (i, P_KPE // HEAD_PAD))
    dqr, dkr, dkpe, dgq, dgk = pl.pallas_call(
        body, name=name, grid=(t // tm, N_HEADS), in_specs=[head, head, kpe, head, head, vec, vec, tab, tab, tab],
        out_specs=(head, head, tab, vec, vec),
        out_shape=(jax.ShapeDtypeStruct(q_raw.shape, F32), jax.ShapeDtypeStruct(k_raw.shape, F32),
                   jax.ShapeDtypeStruct((t, HEAD_PAD), F32), jax.ShapeDtypeStruct((1, HEAD_PAD), F32),
                   jax.ShapeDtypeStruct((1, HEAD_PAD), F32)),
        compiler_params=_params("arbitrary", "arbitrary"),
    )(q_raw, k_raw, p, dq, dk, gq.reshape(1, HEAD_PAD), gk.reshape(1, HEAD_PAD), *tabs)
    return dqr, dkr, dkpe, dgq.reshape(HEAD_PAD), dgk.reshape(HEAD_PAD)


def _dot_nt(a, b):
    return lax.dot_general(a, b, (((1,), (1,)), ((), ())), preferred_element_type=F32)


def _dot_tn(a, b):
    return lax.dot_general(a, b, (((0,), (0,)), ((), ())), preferred_element_type=F32)


def _diag_mask():
    rows = lax.broadcasted_iota(jnp.int32, (ATTN_BLOCK, ATTN_BLOCK), 0) // CHUNK
    cols = lax.broadcasted_iota(jnp.int32, (ATTN_BLOCK, ATTN_BLOCK), 1) // CHUNK
    return cols <= rows


def _attn_fwd(q, k, v, name):
    t = q.shape[0]
    bq = ATTN_BLOCK
    nq = t // bq

    def body(q_ref, k_ref, v_ref, o_ref, lse_ref):
        i = pl.program_id(1)
        qv = q_ref[...]

        def block(j, carry, masked):
            m, l, acc = carry
            rows = pl.ds(pl.multiple_of(j * bq, bq), bq)
            s = _dot_nt(qv, k_ref[rows, :]) * ATTN_SCALE
            if masked:
                s = jnp.where(_diag_mask(), s, -1e30)
            m_new = jnp.maximum(m, jnp.max(s, axis=-1, keepdims=True))
            alpha = jnp.exp(m - m_new)
            pe = jnp.exp(s - m_new)
            l = alpha * l + jnp.sum(pe, axis=-1, keepdims=True)
            acc = alpha * acc + jnp.dot(pe.astype(BF16), v_ref[rows, :], preferred_element_type=F32)
            return m_new, l, acc

        init = (jnp.full((bq, 1), -1e30, F32), jnp.zeros((bq, 1), F32), jnp.zeros((bq, HEAD_PAD), F32))
        carry = lax.fori_loop(0, i, lambda j, cr: block(j, cr, False), init)
        m, l, acc = block(i, carry, True)
        o_ref[...] = acc / l
        lse_ref[...] = jnp.broadcast_to(m + jnp.log(l), (bq, HEAD_PAD))

    blk = pl.BlockSpec((bq, HEAD_PAD), lambda h, i: (i, h))
    full = pl.BlockSpec((t, HEAD_PAD), lambda h, i: (0, h))
    return pl.pallas_call(
        body, name=name, grid=(N_HEADS, nq), in_specs=[blk, full, full], out_specs=(blk, blk),
        out_shape=(jax.ShapeDtypeStruct(q.shape, F32), jax.ShapeDtypeStruct(q.shape, F32)),
        compiler_params=_params("parallel", "parallel"),
    )(q, k, v)


def _attn_bwd(q, k, v, o, lse, do, name):
    t = q.shape[0]
    bq = ATTN_BLOCK
    nq = t // bq

    def body(q_ref, k_ref, v_ref, o_ref, lse_ref, do_ref, dq_ref, dk_ref, dv_ref, delta_ref):
        def rows_of(i):
            return pl.ds(pl.multiple_of(i * bq, bq), bq)

        def prep(i, _):
            r = rows_of(i)
            delta_ref[r, :] = jnp.broadcast_to(jnp.sum(do_ref[r, :] * o_ref[r, :], axis=-1, keepdims=True),
                                               (bq, HEAD_PAD))
            dq_ref[r, :] = jnp.zeros((bq, HEAD_PAD), F32)
            return 0

        lax.fori_loop(0, nq, prep, 0)

        def key_block(j, _):
            rj = rows_of(j)
            kb, vb = k_ref[rj, :], v_ref[rj, :]

            def query_block(i, carry, masked):
                dk, dv = carry
                ri = rows_of(i)
                qb, dob = q_ref[ri, :], do_ref[ri, :].astype(BF16)
                s = _dot_nt(qb, kb) * ATTN_SCALE
                if masked:
                    s = jnp.where(_diag_mask(), s, -1e30)
                pe = jnp.exp(s - lse_ref[ri, :][:, :1])
                dp = _dot_nt(dob, vb)
                ds = (pe * (dp - delta_ref[ri, :][:, :1]) * ATTN_SCALE).astype(BF16)
                dq_ref[ri, :] += jnp.dot(ds, kb, preferred_element_type=F32)
                return dk + _dot_tn(ds, qb), dv + _dot_tn(pe.astype(BF16), dob)

            zero = jnp.zeros((bq, HEAD_PAD), F32)
            carry = query_block(j, (zero, zero), True)
            dk, dv = lax.fori_loop(j + 1, nq, lambda i, cr: query_block(i, cr, False), carry)
            dk_ref[rj, :] = dk
            dv_ref[rj, :] = dv
            return 0

        lax.fori_loop(0, nq, key_block, 0)

    full = pl.BlockSpec((t, HEAD_PAD), lambda h: (0, h))
    shp = jax.ShapeDtypeStruct(q.shape, F32)
    return pl.pallas_call(
        body, name=name, grid=(N_HEADS,), in_specs=[full] * 6, out_specs=(full, full, full),
        out_shape=(shp, shp, shp), scratch_shapes=[pltpu.VMEM((t, HEAD_PAD), F32)],
        compiler_params=_params("parallel"),
    )(q, k, v, o, lse, do)


def _glu_ext(pc_ref, pp_ref, u_ref, tm, first):
    u_ref[CONV_HALO:CONV_HALO + tm, :] = pc_ref[:, P_A:P_G] * jax.nn.sigmoid(pc_ref[:, P_G:P_COLS])
    up = pp_ref[tm - CONV_HALO:tm, P_A:P_G] * jax.nn.sigmoid(pp_ref[tm - CONV_HALO:tm, P_G:P_COLS])
    u_ref[0:CONV_HALO, :] = jnp.where(first, 0.0, up)


def _conv_fwd(p, w, b, ln_g, ln_b, name):
    t = p.shape[0]
    tm = _row_tile(t, CONV_TILE)
    off = CONV_HALO - (CONV_K - 1)

    def body(pc_ref, pp_ref, w_ref, b_ref, g_ref, bb_ref, y_ref, o_ref, u_ref):
        _glu_ext(pc_ref, pp_ref, u_ref, tm, pl.program_id(0) == 0)
        acc = jnp.zeros((tm, CONV_W), F32)
        for kk in range(CONV_K):
            acc = acc + w_ref[kk:kk + 1, :] * u_ref[off + kk:off + kk + tm, :]
        y = acc + b_ref[...]
        y_ref[...] = y
        xc = y - jnp.mean(y, axis=-1, keepdims=True)
        lo = xc * lax.rsqrt(jnp.mean(xc * xc, axis=-1, keepdims=True) + EPS) * g_ref[...] + bb_ref[...]
        o_ref[...] = (lo * jax.nn.sigmoid(lo)).astype(o_ref.dtype)

    prow = pl.BlockSpec((tm, P_COLS), lambda i: (i, 0))
    pprev = pl.BlockSpec((tm, P_COLS), lambda i: (jnp.maximum(i - 1, 0), 0))
    vec = pl.BlockSpec((1, CONV_W), lambda i: (0, 0))
    row = pl.BlockSpec((tm, CONV_W), lambda i: (i, 0))
    return pl.pallas_call(
        body, name=name, grid=(t // tm,),
        in_specs=[prow, pprev, pl.BlockSpec((CONV_HALO, CONV_W), lambda i: (0, 0)), vec, vec, vec],
        out_specs=(row, row),
        out_shape=(jax.ShapeDtypeStruct((t, CONV_W), F32), jax.ShapeDtypeStruct((t, CONV_W), BF16)),
        scratch_shapes=[pltpu.VMEM((tm + CONV_HALO, CONV_W), F32)], compiler_params=_params("parallel"),
    )(p, p, w, b.reshape(1, CONV_W), ln_g.reshape(1, CONV_W), ln_b.reshape(1, CONV_W))


def _conv_bwd_ln(y, dout, ln_g, ln_b, name):
    t = y.shape[0]
    tm = _row_tile(t)

    def body(y_ref, d_ref, g_ref, bb_ref, dy_ref, dg_ref, db_ref, dcb_ref):
        yv = y_ref[...]
        xc = yv - jnp.mean(yv, axis=-1, keepdims=True)
        r = lax.rsqrt(jnp.mean(xc * xc, axis=-1, keepdims=True) + EPS)
        n = xc * r
        lo = n * g_ref[...] + bb_ref[...]
        s = jax.nn.sigmoid(lo)
        dlo = d_ref[...] * (s * (1.0 + lo * (1.0 - s)))
        dn = dlo * g_ref[...]
        dy = r * (dn - jnp.mean(dn, axis=-1, keepdims=True) - n * jnp.mean(dn * n, axis=-1, keepdims=True))
        dy_ref[...] = dy

        @pl.when(pl.program_id(0) == 0)
        def _():
            dg_ref[...] = jnp.zeros_like(dg_ref)
            db_ref[...] = jnp.zeros_like(db_ref)
            dcb_ref[...] = jnp.zeros_like(dcb_ref)

        dg_ref[...] += jnp.sum(dlo * n, axis=0, keepdims=True)
        db_ref[...] += jnp.sum(dlo, axis=0, keepdims=True)
        dcb_ref[...] += jnp.sum(dy, axis=0, keepdims=True)

    row = pl.BlockSpec((tm, CONV_W), lambda i: (i, 0))
    vec = pl.BlockSpec((1, CONV_W), lambda i: (0, 0))
    vshape = jax.ShapeDtypeStruct((1, CONV_W), F32)
    dy, dg, db, dcb = pl.pallas_call(
        body, name=name, grid=(t // tm,), in_specs=[row, row, vec, vec], out_specs=(row, vec, vec, vec),
        out_shape=(jax.ShapeDtypeStruct((t, CONV_W), F32), vshape, vshape, vshape),
        compiler_params=_params("arbitrary"),
    )(y, dout, ln_g.reshape(1, CONV_W), ln_b.reshape(1, CONV_W))
    return dy, dg.reshape(CONV_W), db.reshape(CONV_W), dcb.reshape(CONV_W)


def _conv_bwd_taps(p, dy, w, name):
    t = p.shape[0]
    tm = _row_tile(t, CONV_TILE)
    nt = t // tm
    off = CONV_HALO - (CONV_K - 1)

    def body(pc_ref, pp_ref, dyc_ref, dyn_ref, w_ref, dag_ref, dw_ref, u_ref, dye_ref):
        i = pl.program_id(0)
        _glu_ext(pc_ref, pp_ref, u_ref, tm, i == 0)
        dyc = dyc_ref[...]
        dye_ref[0:tm, :] = dyc
        dye_ref[tm:tm + CONV_HALO, :] = jnp.where(i == nt - 1, 0.0, dyn_ref[0:CONV_HALO, :])

        @pl.when(i == 0)
        def _():
            dw_ref[...] = jnp.zeros_like(dw_ref)

        du = jnp.zeros((tm, CONV_W), F32)
        for kk in range(CONV_K):
            dw_ref[kk:kk + 1, :] += jnp.sum(dyc * u_ref[off + kk:off + kk + tm, :], axis=0, keepdims=True)
            back = CONV_K - 1 - kk
            du = du + w_ref[kk:kk + 1, :] * dye_ref[back:back + tm, :]
        av, gv = pc_ref[:, P_A:P_G], pc_ref[:, P_G:P_COLS]
        s = jax.nn.sigmoid(gv)
        dag_ref[:, 0:CONV_W] = du * s
        dag_ref[:, CONV_W:2 * CONV_W] = du * av * (s * (1.0 - s))

    prow = pl.BlockSpec((tm, P_COLS), lambda i: (i, 0))
    pprev = pl.BlockSpec((tm, P_COLS), lambda i: (jnp.maximum(i - 1, 0), 0))
    row = pl.BlockSpec((tm, CONV_W), lambda i: (i, 0))
    nxt = pl.BlockSpec((tm, CONV_W), lambda i: (jnp.minimum(i + 1, nt - 1), 0))
    wspec = pl.BlockSpec((CONV_HALO, CONV_W), lambda i: (0, 0))
    return pl.pallas_call(
        body, name=name, grid=(nt,), in_specs=[prow, pprev, row, nxt, wspec],
        out_specs=(pl.BlockSpec((tm, 2 * CONV_W), lambda i: (i, 0)), wspec),
        out_shape=(jax.ShapeDtypeStruct((t, 2 * CONV_W), F32), jax.ShapeDtypeStruct((CONV_HALO, CONV_W), F32)),
        scratch_shapes=[pltpu.VMEM((tm + CONV_HALO, CONV_W), F32), pltpu.VMEM((tm + CONV_HALO, CONV_W), F32)],
        compiler_params=_params("arbitrary"),
    )(p, p, dy, dy, w)


def _loss_head(y, target, name):
    t, d = y.shape
    tm = _row_tile(t)

    def body(y_ref, t_ref, l_ref, dy_ref):
        err = y_ref[...] - t_ref[...]
        dy_ref[...] = err * (1.0 / d)

        @pl.when(pl.program_id(0) == 0)
        def _():
            l_ref[...] = jnp.zeros_like(l_ref)

        row = jnp.sum(err * err, axis=-1, keepdims=True) * (0.5 / d)
        l_ref[...] += jnp.broadcast_to(jnp.sum(row, axis=0, keepdims=True), (1, LANE))

    row = pl.BlockSpec((tm, d), lambda i: (i, 0))
    return pl.pallas_call(
        body, name=name, grid=(t // tm,), in_specs=[row, row],
        out_specs=(pl.BlockSpec((1, LANE), lambda i: (0, 0)), row),
        out_shape=(jax.ShapeDtypeStruct((1, LANE), F32), jax.ShapeDtypeStruct((t, d), F32)),
        compiler_params=_params("arbitrary"),
    )(y, target)


def _adamw(w, g, m, v, name):
    r, c = w.shape
    tr = _row_tile(r, 256)
    c1, c2 = 1.0 - ADAM_B1 ** ADAM_STEP, 1.0 - ADAM_B2 ** ADAM_STEP

    def body(w_ref, g_ref, m_ref, v_ref, d_ref, mo_ref, vo_ref):
        gv = g_ref[...]
        mn = ADAM_B1 * m_ref[...] + (1.0 - ADAM_B1) * gv
        vn = ADAM_B2 * v_ref[...] + (1.0 - ADAM_B2) * (gv * gv)
        mo_ref[...] = mn
        vo_ref[...] = vn
        d_ref[...] = -ADAM_LR * ((mn / c1) / (jnp.sqrt(vn / c2) + ADAM_EPS) + ADAM_WD * w_ref[...])

    blk = pl.BlockSpec((tr, c), lambda i: (i, 0))
    shp = jax.ShapeDtypeStruct((r, c), F32)
    return pl.pallas_call(
        body, name=name, grid=(r // tr,), in_specs=[blk] * 4, out_specs=(blk, blk, blk), out_shape=(shp, shp, shp),
        compiler_params=_params("parallel"),
    )(w, g, m, v)


def _sum_parts(parts, name):
    r, c = parts[0].shape
    tr = _row_tile(r, 256)

    def body(*refs):
        acc = refs[0][...]
        for ref in refs[1:-1]:
            acc = acc + ref[...]
        refs[-1][...] = acc

    blk = pl.BlockSpec((tr, c), lambda i: (i, 0))
    return pl.pallas_call(
        body, name=name, grid=(r // tr,), in_specs=[blk] * len(parts), out_specs=blk,
        out_shape=jax.ShapeDtypeStruct((r, c), F32), compiler_params=_params("parallel"),
    )(*parts)


def _place():
    return lax.axis_index("x"), lax.axis_index("y"), lax.axis_index("c")


def _window(ref, block, size, axis):
    start = pl.multiple_of(block * size, LANE if size % LANE == 0 else 8)
    return ref.at[(slice(None),) * axis + (pl.ds(start, size),)]


def _all_gather(pieces, name, in_vmem=False):
    n_p = len(pieces)

    def body(*refs):
        x_refs, out_refs = refs[:n_p], refs[n_p:2 * n_p]
        send_sems, recv_sems, local_sems = refs[2 * n_p:]
        px, py, pc = _place()
        me, sibling = (px, py, pc), (px, py, 1 - pc)
        chips = [(1 - px, py), (px, 1 - py), (1 - px, 1 - py)]

        def win(p, block):
            bx, by, bc = block
            x, axis = pieces[p]
            return _window(out_refs[p], 4 * bx + 2 * by + bc, x.shape[axis], axis)

        def copy(k, p, block, to, local=False):
            return pltpu.make_async_remote_copy(
                src_ref=x_refs[p] if local else win(p, block), dst_ref=win(p, block),
                send_sem=send_sems.at[k, p], recv_sem=recv_sems.at[k, p], device_id=to, device_id_type=MESH_ID)

        every = range(n_p)
        mine = [pltpu.make_async_copy(x_refs[p], win(p, me), local_sems.at[p]) for p in every]
        first = [copy(0, p, me, sibling, local=True) for p in every]
        first += [copy(1 + j, p, me, (*chip, pc), local=True) for j, chip in enumerate(chips) for p in every]
        for cp in mine + first:
            cp.start()
        passed = []
        for j, chip in enumerate(chips):
            for p in every:
                copy(1 + j, p, (*chip, pc), me).wait_recv()
                passed.append(copy(4 + j, p, (*chip, pc), sibling))
                passed[-1].start()
        for p in every:
            copy(0, p, sibling, me).wait_recv()
        for j, chip in enumerate(chips):
            for p in every:
                copy(4 + j, p, (*chip, 1 - pc), me).wait_recv()
        for cp in first + passed:
            cp.wait_send()
        for cp in mine:
            cp.wait()

    def gathered(x, axis):
        return jax.ShapeDtypeStruct(x.shape[:axis] + (N_DEV * x.shape[axis],) + x.shape[axis + 1:], x.dtype)

    spec = VMEM_SPEC if in_vmem else ANY
    return pl.pallas_call(
        body, name=name, in_specs=[spec] * n_p, out_specs=[spec] * n_p, out_shape=[gathered(*pc_) for pc_ in pieces],
        scratch_shapes=[pltpu.SemaphoreType.DMA((7, n_p)), pltpu.SemaphoreType.DMA((7, n_p)),
                        pltpu.SemaphoreType.DMA((n_p,))],
        compiler_params=pltpu.CompilerParams(vmem_limit_bytes=VMEM_LIMIT),
    )(*[x for x, _ in pieces])


def _start_copies(bufs, n_copies, plan, name):
    nb = len(bufs)

    def body(*refs):
        send_sems, recv_sems, token = refs[nb], refs[nb + 1], refs[-1]
        for i, (src, dst, dev) in enumerate(plan(refs[:nb])):
            pltpu.make_async_remote_copy(src_ref=src, dst_ref=dst, send_sem=send_sems.at[i], recv_sem=recv_sems.at[i],
                                         device_id=dev, device_id_type=MESH_ID).start()
        token[...] = jnp.zeros_like(token)

    out = pl.pallas_call(
        body, name=name, in_specs=[HBM_SPEC] * nb,
        out_shape=(pltpu.SemaphoreType.DMA((n_copies,)), pltpu.SemaphoreType.DMA((n_copies,)),
                   *[pltpu.HBM(b.shape, b.dtype) for b in bufs], jax.ShapeDtypeStruct((8, LANE), F32)),
        out_specs=(SEM_SPEC, SEM_SPEC, *[HBM_SPEC] * nb, VMEM_SPEC),
        input_output_aliases={i: 2 + i for i in range(nb)},
        compiler_params=pltpu.CompilerParams(has_side_effects=DATAFLOW),
    )(*[pltpu.with_memory_space_constraint(b, pltpu.HBM) for b in bufs])
    return out[0], out[1], list(out[2:2 + nb]), out[-1]


def _wait_copies(started, after, n_copies, plan, name):
    send_sems, recv_sems, bufs, _ = started
    nb = len(bufs)

    def body(*refs):
        send_ref, recv_ref = refs[nb], refs[nb + 1]
        copies = [pltpu.make_async_remote_copy(src_ref=src, dst_ref=dst, send_sem=send_ref.at[i], recv_sem=recv_ref.at[i],
                                               device_id=dev, device_id_type=MESH_ID)
                  for i, (src, dst, dev) in enumerate(plan(refs[:nb]))]
        for cp in copies:
            cp.wait_send()
        for cp in copies:
            cp.wait_recv()

    out = pl.pallas_call(
        body, name=name, in_specs=[HBM_SPEC] * nb + [SEM_SPEC, SEM_SPEC, ANY],
        out_shape=tuple(pltpu.HBM(b.shape, b.dtype) for b in bufs), out_specs=tuple([HBM_SPEC] * nb),
        input_output_aliases={i: i for i in range(nb)},
        compiler_params=pltpu.CompilerParams(has_side_effects=DATAFLOW),
    )(*bufs, send_sems, recv_sems, after)
    return list(out)


def _after(x, token):
    return lax.optimization_barrier((x, token))[0]


def _other_chips():
    px, py, _ = _place()
    return [(1 - px, py), (px, 1 - py), (1 - px, 1 - py)]


def _exchange(srcs, slots, src_block, target, name):
    n_p = len(srcs)

    def body(*refs):
        src_refs, out_refs, send_sems, recv_sems = refs[:n_p], refs[n_p:2 * n_p], refs[-2], refs[-1]
        copies = [pltpu.make_async_remote_copy(
            src_ref=src_refs[p].at[src_block(s)], dst_ref=out_refs[p].at[s], send_sem=send_sems.at[s, p],
            recv_sem=recv_sems.at[s, p], device_id=target(s), device_id_type=MESH_ID)
            for s in range(slots) for p in range(n_p)]
        for cp in copies:
            cp.start()
        for cp in copies:
            cp.wait_recv()
        for cp in copies:
            cp.wait_send()

    return pl.pallas_call(
        body, name=name, in_specs=[ANY] * n_p, out_specs=[ANY] * n_p,
        out_shape=[jax.ShapeDtypeStruct((slots,) + a.shape[1:], a.dtype) for a in srcs],
        scratch_shapes=[pltpu.SemaphoreType.DMA((slots, n_p)), pltpu.SemaphoreType.DMA((slots, n_p))],
        compiler_params=pltpu.CompilerParams(vmem_limit_bytes=VMEM_LIMIT),
    )(*srcs)


def _blocks_to_sibling(sends, name):
    def src_block(j):
        return 2 * j + 1 - lax.axis_index("c")

    def target(j):
        px, py, pc = _place()
        return (px, py, 1 - pc)

    return _exchange(sends, 4, src_block, target, name)


def _pair_sums_for_chips(own, got, name):
    _, r, c = own.shape
    tr = _row_tile(r, 256, 16)

    def body(idx_ref, own_ref, got_ref, o_ref):
        o_ref[...] = (own_ref[...] + got_ref[...].astype(F32)).astype(o_ref.dtype)

    grid_spec = pltpu.PrefetchScalarGridSpec(
        num_scalar_prefetch=1, grid=(3, r // tr),
        in_specs=[pl.BlockSpec((None, tr, c), lambda k, i, idx: (idx[k], i, 0)),
                  pl.BlockSpec((None, tr, c), lambda k, i, idx: (idx[3 + k], i, 0))],
        out_specs=pl.BlockSpec((None, tr, c), lambda k, i, idx: (k, i, 0)))
    chips = [2 * cx + cy for cx, cy in _other_chips()]
    idx = jnp.stack([2 * j + lax.axis_index("c") for j in chips] + chips).astype(jnp.int32)
    return pl.pallas_call(
        body, name=name, grid_spec=grid_spec, out_shape=jax.ShapeDtypeStruct((3, r, c), BF16),
        compiler_params=_params("parallel", "parallel"),
    )(idx, own, got)


def _sum_for_me(own, got_sibling, got_chips, name):
    _, r, c = own.shape
    tr = _row_tile(r, 256, 16)

    def body(idx_ref, own_ref, sib_ref, g0_ref, g1_ref, g2_ref, o_ref):
        acc = own_ref[...] + sib_ref[...].astype(F32)
        for ref in (g0_ref, g1_ref, g2_ref):
            acc = acc + ref[...].astype(F32)
        o_ref[...] = acc

    def part(k):
        return pl.BlockSpec((None, tr, c), lambda i, idx: (k, i, 0))

    grid_spec = pltpu.PrefetchScalarGridSpec(
        num_scalar_prefetch=1, grid=(r // tr,),
        in_specs=[pl.BlockSpec((None, tr, c), lambda i, idx: (idx[0], i, 0)),
                  pl.BlockSpec((None, tr, c), lambda i, idx: (idx[1], i, 0)), part(0), part(1), part(2)],
        out_specs=pl.BlockSpec((tr, c), lambda i, idx: (i, 0)))
    px, py, pc = _place()
    idx = jnp.stack([4 * px + 2 * py + pc, 2 * px + py]).astype(jnp.int32)
    return pl.pallas_call(
        body, name=name, grid_spec=grid_spec, out_shape=jax.ShapeDtypeStruct((r, c), F32),
        compiler_params=_params("parallel"),
    )(idx, own, got_sibling, got_chips, got_chips, got_chips)


def _chip_plan(n_p):
    def plan(refs):
        pc = lax.axis_index("c")
        return [(refs[p].at[k], refs[n_p + p].at[k], (cx, cy, pc))
                for p in range(n_p) for k, (cx, cy) in enumerate(_other_chips())]
    return plan


def _reduce_start(own, sends, tag):
    from_sibling = _blocks_to_sibling(sends, "grads_to_sibling_" + tag)
    pair_sums = [_pair_sums_for_chips(a, b, "grads_pair_sums") for a, b in zip(own, from_sibling)]
    lands = [lax.empty(a.shape, a.dtype) for a in pair_sums]
    started = _start_copies(pair_sums + lands, 3 * len(own), _chip_plan(len(own)), "grads_to_chips_start_" + tag)
    return from_sibling, started


def _reduce_finish(own, from_sibling, started, after, tag):
    n_p = len(own)
    bufs = _wait_copies(started, after, 3 * n_p, _chip_plan(n_p), "grads_to_chips_wait_" + tag)
    return [_sum_for_me(a, b, c, "grads_sum") for a, b, c in zip(own, from_sibling, bufs[n_p:])]


def _gather_plans(pieces):
    n_p = len(pieces)
    dims = [(x.shape[axis], axis) for x, axis in pieces]

    def first(refs):
        px, py, pc = _place()
        targets = [(px, py, 1 - pc)] + [(cx, cy, pc) for cx, cy in _other_chips()]
        return [(refs[p], _window(refs[n_p + p], 4 * px + 2 * py + pc, *dims[p]), to)
                for p in range(n_p) for to in targets]

    def second(refs):
        px, py, pc = _place()
        out = []
        for p in range(n_p):
            for cx, cy in _other_chips():
                win = _window(refs[p], 4 * cx + 2 * cy + pc, *dims[p])
                out.append((win, win, (px, py, 1 - pc)))
        return out

    return first, second


def _flat_rows(a, width):
    return a.reshape(-1, width)


def _full_from_blocks(blocks, name):
    if name in COL_SHARDED:
        _, l, k, nb = blocks.shape
        return jnp.transpose(blocks, (1, 2, 0, 3)).reshape(l, k, N_DEV * nb)
    _, l, rb, n = blocks.shape
    return jnp.transpose(blocks, (1, 0, 2, 3)).reshape(l, N_DEV * rb, n)


def _blocks_from_full(full, name):
    if name in COL_SHARDED:
        l, k, n = full.shape
        return jnp.transpose(full.reshape(l, k, N_DEV, n // N_DEV), (2, 0, 1, 3))
    l, rows, n = full.shape
    return jnp.transpose(full.reshape(l, N_DEV, rows // N_DEV, n), (1, 0, 2, 3))


def _pad_heads(w, width):
    k = w.shape[0]
    return jnp.pad(w.reshape(k, N_HEADS, width), ((0, 0), (0, 0), (0, HEAD_PAD - width))).reshape(k, N_HEADS * HEAD_PAD)


def _unpad_heads(w, width):
    k = w.shape[0]
    return w.reshape(k, N_HEADS, HEAD_PAD)[:, :, :width].reshape(k, N_HEADS * width)


def _layer_operands(full, vec, conv_w_full, l):
    w_in = full['w_in'][l]
    kpe = jnp.pad(w_in[:, LAT:LAT + QK_ROPE], ((0, 0), (QK_NOPE, HEAD_PAD - QK_DIM)))
    w_ukv = full['w_ukv'][l].reshape(KV_LORA, N_HEADS, QK_NOPE + V_DIM)
    w_out = full['w_out'][l]
    d_model = w_out.shape[1]
    wo_attn = jnp.pad(w_out[:N_HEADS * V_DIM].reshape(N_HEADS, V_DIM, d_model),
                      ((0, 0), (0, HEAD_PAD - V_DIM), (0, 0))).reshape(N_HEADS * HEAD_PAD, d_model)
    ops = {
        'w_in': jnp.concatenate([w_in[:, :LAT], kpe, w_in[:, LAT + QK_ROPE:]], axis=1),
        'w_q': _pad_heads(full['w_uq'][l], QK_DIM),
        'w_k': _pad_heads(w_ukv[:, :, :QK_NOPE].reshape(KV_LORA, N_HEADS * QK_NOPE), QK_NOPE),
        'w_v': _pad_heads(w_ukv[:, :, QK_NOPE:].reshape(KV_LORA, N_HEADS * V_DIM), V_DIM),
        'wo_attn': wo_attn,
        'wo_conv': w_out[N_HEADS * V_DIM:],
        'conv_w': jnp.pad(conv_w_full[l], ((0, CONV_HALO - CONV_K), (0, 0))),
        'gq': jnp.pad(vec['q_norm'][l], (0, HEAD_PAD - QK_DIM)),
        'gk': jnp.pad(vec['k_norm'][l], (0, HEAD_PAD - QK_DIM)),
    }
    for n in ('ffn1_norm', 'mix_norm', 'q_latent_norm', 'kv_latent_norm', 'conv_b', 'conv_ln_g', 'conv_ln_b',
              'ffn2_norm', 'post_norm'):
        ops[n] = vec[n][l]
    return ops


def _ffn_fwd(x, g, wgu, wd, fp):
    h = _rms_fwd(x, g, BF16, "rms_fwd_ffn")
    ab, z = _ffn_up(h, wgu, fp, "ffn_up")
    y = _mm(z, wd, res=x, scale=0.5, name="ffn_down")
    return y, (x, h, ab, z)


def _ffn_bwd(dy, dyb, saved, g, wgu, wd, fp, after_dw=None):
    x, h, ab, z = saved
    d_wd = _mm(z, dyb, ta=True, scale=0.5, blocks=('row', N_DEV), name="ffn_dwd")
    dab = _ffn_dab(dyb, wd, ab, fp, "ffn_dab")
    d_wgu = _mm(h, dab, ta=True, blocks=('col', N_DEV), name="ffn_dwgu")
    if after_dw is not None:
        dab = _after(dab, after_dw(d_wgu, d_wd))
    dh = _mm(dab, wgu, tb=True, name="ffn_dh")
    dx, dxb, dg = _rms_bwd(x, g, dh, dy, "rms_bwd_ffn")
    return dx, dxb, dg, d_wgu, d_wd


def _mixer_fwd(x, ops, tabs, after_attention=None):
    h = _rms_fwd(x, ops['mix_norm'], BF16, "rms_fwd_mix")
    p = _mm(h, ops['w_in'], name="mix_in")
    qln, kvln = _lat_norm_fwd(p, ops['q_latent_norm'], ops['kv_latent_norm'], "lat_norm_fwd")
    q_raw = _mm(qln, ops['w_q'], name="mix_q")
    k_raw = _mm(kvln, ops['w_k'], name="mix_k")
    v = _mm(kvln, ops['w_v'], out_dtype=BF16, name="mix_v")
    q, k = _qk_prep_fwd(q_raw, k_raw, p, ops['gq'], ops['gk'], tabs, "qk_prep_fwd")
    o, lse = _attn_fwd(q, k, v, "attn_fwd")
    token = after_attention(o) if after_attention is not None else None
    conv_b = ops['conv_b'] if token is None else _after(ops['conv_b'], token)
    y_conv, cv = _conv_fwd(p, ops['conv_w'], conv_b, ops['conv_ln_g'], ops['conv_ln_b'], "conv_fwd")
    x_attn = _mm(o, ops['wo_attn'], res=x, name="mix_out_attn")
    x_out = _mm(cv, ops['wo_conv'], res=x_attn, name="mix_out_conv")
    return x_out, (x, h, p, qln, kvln, q_raw, k_raw, v, q, k, o, lse, y_conv, cv)


def _mixer_bwd(dx_out, dxb_out, saved, ops, tabs, token=None):
    x, h, p, qln, kvln, q_raw, k_raw, v, q, k, o, lse, y_conv, cv = saved
    g = {}
    do = _mm(dxb_out, ops['wo_attn'] if token is None else _after(ops['wo_attn'], token), tb=True, name="mix_do")
    dcv = _mm(dxb_out, ops['wo_conv'], tb=True, name="mix_dcv")
    g['wo_attn'] = _mm(o, dxb_out, ta=True, name="mix_dwo_attn")
    g['wo_conv'] = _mm(cv, dxb_out, ta=True, name="mix_dwo_conv")
    dq, dk, dv = _attn_bwd(q, k, v, o, lse, do, "attn_bwd")
    dq_raw, dk_raw, dkpe, g['gq'], g['gk'] = _qk_prep_bwd(q_raw, k_raw, p, dq, dk, ops['gq'], ops['gk'], tabs,
                                                          "qk_prep_bwd")
    g['w_q'] = _mm(qln, dq_raw, ta=True, name="mix_dwq")
    g['w_k'] = _mm(kvln, dk_raw, ta=True, name="mix_dwk")
    g['w_v'] = _mm(kvln, dv, ta=True, name="mix_dwv")
    dqln = _mm(dq_raw, ops['w_q'], tb=True, name="mix_dqln")
    dkvln = _mm(dk_raw, ops['w_k'], tb=True, name="mix_dkvln_k")
    dkvln = _mm(dv, ops['w_v'], tb=True, res=dkvln, name="mix_dkvln_v")
    dp_lat, g['q_latent_norm'], g['kv_latent_norm'] = _lat_norm_bwd(
        p, ops['q_latent_norm'], ops['kv_latent_norm'], dqln, dkvln, "lat_norm_bwd")
    dy_conv, g['conv_ln_g'], g['conv_ln_b'], g['conv_b'] = _conv_bwd_ln(
        y_conv, dcv, ops['conv_ln_g'], ops['conv_ln_b'], "conv_bwd_ln")
    dag, g['conv_w'] = _conv_bwd_taps(p, dy_conv, ops['conv_w'], "conv_bwd_taps")
    dp = jnp.concatenate([dp_lat, dkpe, dag], axis=1)
    g['w_in'] = _mm(h, dp, ta=True, name="mix_dw_in")
    dh = _mm(dp, ops['w_in'], tb=True, name="mix_dh")
    dx, dxb, g['mix_norm'] = _rms_bwd(x, ops['mix_norm'], dh, dx_out, "rms_bwd_mix")
    return dx, dxb, g


def _mixer_grads_to_params(g):
    d_w_in = g['w_in']
    d_wk = _unpad_heads(g['w_k'], QK_NOPE).reshape(KV_LORA, N_HEADS, QK_NOPE)
    d_wv = _unpad_heads(g['w_v'], V_DIM).reshape(KV_LORA, N_HEADS, V_DIM)
    d_model = g['wo_attn'].shape[1]
    d_wo_attn = g['wo_attn'].reshape(N_HEADS, HEAD_PAD, d_model)[:, :V_DIM].reshape(N_HEADS * V_DIM, d_model)
    return {
        'mix_norm': g['mix_norm'],
        'w_in': jnp.concatenate([d_w_in[:, :LAT], d_w_in[:, LAT + QK_NOPE:LAT + QK_DIM], d_w_in[:, P_A:]], axis=1),
        'q_latent_norm': g['q_latent_norm'], 'w_uq': _unpad_heads(g['w_q'], QK_DIM),
        'kv_latent_norm': g['kv_latent_norm'],
        'w_ukv': jnp.concatenate([d_wk, d_wv], axis=2).reshape(KV_LORA, N_HEADS * (QK_NOPE + V_DIM)),
        'q_norm': g['gq'][:QK_DIM], 'k_norm': g['gk'][:QK_DIM], 'conv_w': g['conv_w'][:CONV_K],
        'conv_b': g['conv_b'], 'conv_ln_g': g['conv_ln_g'], 'conv_ln_b': g['conv_ln_b'],
        'w_out': jnp.concatenate([d_wo_attn, g['wo_conv']], axis=0),
    }


def kernel(x, ffn1_norm, ffn1_w_gate, ffn1_w_up, ffn1_w_down, mix_norm, w_in, q_latent_norm, w_uq, kv_latent_norm, w_ukv, q_norm, k_norm, conv_w, conv_b, conv_ln_g, conv_ln_b, w_out, ffn2_norm, ffn2_w_gate, ffn2_w_up, ffn2_w_down, post_norm, loss_target, m_ffn1_norm, m_ffn1_w_gate, m_ffn1_w_up, m_ffn1_w_down, m_mix_norm, m_w_in, m_q_latent_norm, m_w_uq, m_kv_latent_norm, m_w_ukv, m_q_norm, m_k_norm, m_conv_w, m_conv_b, m_conv_ln_g, m_conv_ln_b, m_w_out, m_ffn2_norm, m_ffn2_w_gate, m_ffn2_w_up, m_ffn2_w_down, m_post_norm, v_ffn1_norm, v_ffn1_w_gate, v_ffn1_w_up, v_ffn1_w_down, v_mix_norm, v_w_in, v_q_latent_norm, v_w_uq, v_kv_latent_norm, v_w_ukv, v_q_norm, v_k_norm, v_conv_w, v_conv_b, v_conv_ln_g, v_conv_ln_b, v_w_out, v_ffn2_norm, v_ffn2_w_gate, v_ffn2_w_up, v_ffn2_w_down, v_post_norm):
    args = locals()
    w = {n: args[n] for n in WEIGHTS}
    mom = {n: args["m_" + n] for n in WEIGHTS}
    var = {n: args["v_" + n] for n in WEIGHTS}
    depth = ffn1_norm.shape[0]
    x0 = x.reshape(x.shape[-2:])
    target = loss_target.reshape(loss_target.shape[-2:])
    t, d_model = x0.shape
    my_block = 4 * lax.axis_index("x") + 2 * lax.axis_index("y") + lax.axis_index("c")

    fb = ffn1_w_gate.shape[-1]
    fp = -(-fb // LANE) * LANE
    ffns = [(l, f) for l in range(depth) for f in (1, 2)]
    pad_cols = lambda a: jnp.pad(a, ((0, 0), (0, fp - fb)))
    gu_local = {(l, f): jnp.concatenate([pad_cols(w[f'ffn{f}_w_gate'][l]), pad_cols(w[f'ffn{f}_w_up'][l])],
                                        axis=1).astype(BF16) for l, f in ffns}
    dn_local = {(l, f): jnp.pad(w[f'ffn{f}_w_down'][l], ((0, fp - fb), (0, 0))).astype(BF16) for l, f in ffns}
    rows_of = {n: w[n].size // d_model for n in REST}
    rest_local = jnp.concatenate([_flat_rows(w[n].astype(BF16), d_model) for n in REST], axis=0)
    n_rest = rest_local.shape[0]
    first_ffn, later = ffns[0], ffns[1:]
    cw = conv_w.reshape(-1)
    cw_rows = -(-cw.size // (8 * LANE)) * 8
    cw_flat = jnp.pad(cw, (0, cw_rows * LANE - cw.size)).reshape(cw_rows, LANE)
    got = _all_gather([(gu_local[first_ffn], 1), (dn_local[first_ffn], 0), (rest_local, 0), (cw_flat, 0)],
                      "gather_first")
    wgu, wd = {first_ffn: got[0]}, {first_ffn: got[1]}
    gathered = got[2].reshape(N_DEV, n_rest, d_model)
    cw_all = got[3].reshape(N_DEV, cw_rows * LANE)[:, :cw.size]
    later_pieces = [(_after(gu_local[q], got[2]), 1) for q in later] + [(_after(dn_local[q], got[2]), 0) for q in later]
    n_later = len(later_pieces)
    gather_plan, forward_plan = _gather_plans(later_pieces)

    def landing(a, axis):
        shape = a.shape[:axis] + (N_DEV * a.shape[axis],) + a.shape[axis + 1:]
        return lax.dynamic_update_slice_in_dim(lax.empty(shape, a.dtype), a, my_block * a.shape[axis], axis)

    gather_later = _start_copies([a for a, _ in later_pieces] + [landing(a, ax) for a, ax in later_pieces],
                                 4 * n_later, gather_plan, "gather_later_start")
    full, start = {}, 0
    for n in REST:
        blocks = gathered[:, start:start + rows_of[n]].reshape((N_DEV,) + w[n].shape)
        full[n] = _full_from_blocks(blocks, n)
        start += rows_of[n]
    conv_w_full =jnp.transpose(cw_all.reshape((N_DEV,) + conv_w.shape), (1, 2, 0, 3)).reshape(depth, CONV_K, CONV_W)
    vec = {n: w[n] for n in VECTORS}
    ops = [_layer_operands(full, vec, conv_w_full, l) for l in range(depth)]
    tabs = _rope_tables(t)

    saved, xl = [], x0
    forward_later = []

    def pass_on_later(o_attn):
        lands = _wait_copies(gather_later, o_attn, 4 * n_later, gather_plan, "gather_later_wait")[n_later:]
        forward_later.append(_start_copies(lands, 3 * n_later, forward_plan, "gather_later_forward_start"))
        return forward_later[0][3]

    for l in range(depth):
        o = ops[l]
        if l == 0:
            x1, s1 = _ffn_fwd(xl, _after(o['ffn1_norm'], gather_later[3]), wgu[l, 1], wd[l, 1], fp)
            x2, sm = _mixer_fwd(x1, o, tabs, after_attention=pass_on_later)
            lands = _wait_copies(forward_later[0], x2, 3 * n_later, forward_plan, "gather_later_forward_wait")
            wgu.update(zip(later, lands[:len(later)]))
            wd.update(zip(later, lands[len(later):]))
        else:
            x1, s1 = _ffn_fwd(xl, o['ffn1_norm'], wgu[l, 1], wd[l, 1], fp)
            x2, sm = _mixer_fwd(x1, o, tabs)
        x3, s2 = _ffn_fwd(x2, o['ffn2_norm'], wgu[l, 2], wd[l, 2], fp)
        xl = _rms_fwd(x3, o['post_norm'], F32, "rms_fwd_post")
        saved.append((s1, sm, s2, x3))
    loss_part, dx = _loss_head(xl, target, "loss_head")
    loss = lax.psum(loss_part[0, 0], ("x", "y", "c"))

    grads, mine_gu, mine_dn, in_flight = [None] * depth, {}, {}, {}

    def exchange(tag):
        def after_dw(d_wgu, d_wd):
            own = [d_wgu[0], d_wd[0]]
            from_sibling, started = _reduce_start(own, [d_wgu[1], d_wd[1]], tag)
            in_flight[tag] = (own, from_sibling, started)
            return started[3]
        return after_dw

    def finish(tag, after):
        own, from_sibling, started = in_flight.pop(tag)
        return _reduce_finish(own, from_sibling, started, after, tag)

    for l in reversed(range(depth)):
        o = ops[l]
        s1, sm, s2, x3 = saved[l]
        dx, dxb, d_post = _rms_bwd(x3, o['post_norm'], dx, None, "rms_bwd_post")
        dx, dxb, d_ffn2, _, _ = _ffn_bwd(dx, dxb, s2, o['ffn2_norm'], wgu[l, 2], wd[l, 2], fp, exchange(f"{l}2"))
        if l + 1 < depth:
            mine_gu[l + 1, 1], mine_dn[l + 1, 1] = finish(f"{l + 1}1", dx)
        dx, dxb, gm = _mixer_bwd(dx, dxb, sm, o, tabs)
        mine_gu[l, 2], mine_dn[l, 2] = finish(f"{l}2", dx)
        grads[l] = _mixer_grads_to_params(gm)
        if l == 0:
            rest_own = jnp.concatenate(
                [_blocks_from_full(jnp.stack([grads[k][n] for k in range(depth)]), n).reshape(N_DEV, rows_of[n], d_model)
                 for n in REST], axis=1)
            sibling_rest, started_rest = _reduce_start([rest_own], [rest_own.astype(BF16)], "rest")
            dxb = _after(dxb, started_rest[3])
        dx, dxb, d_ffn1, _, _ = _ffn_bwd(dx, dxb, s1, o['ffn1_norm'], wgu[l, 1], wd[l, 1], fp, exchange(f"{l}1"))
        grads[l].update(post_norm=d_post, ffn2_norm=d_ffn2, ffn1_norm=d_ffn1)
    grad_x = dx.reshape(x.shape)
    part = {n: jnp.stack([grads[l][n] for l in range(depth)]) for n in grads[0]}

    small = jnp.concatenate([part[n].reshape(-1) for n in VECTORS] + [part['conv_w'].reshape(-1)])
    s_rows = -(-small.size // (8 * LANE)) * 8
    small = jnp.pad(small, (0, s_rows * LANE - small.size)).reshape(s_rows, LANE)
    small_all = _all_gather([(small, 0)], "gather_small_grads", in_vmem=True)[0]
    small_sum = _sum_parts([small_all[k * s_rows:(k + 1) * s_rows] for k in range(N_DEV)], "sum_small_grads")

    mine_gu[first_ffn], mine_dn[first_ffn] = finish(f"{first_ffn[0]}{first_ffn[1]}", small_sum)
    mine_rest = _reduce_finish([rest_own], sibling_rest, started_rest, small_sum, "rest")[0]
    grad = {}
    for f in (1, 2):
        grad[f'ffn{f}_w_gate'] = jnp.stack([mine_gu[l, f][:, :fb] for l in range(depth)])
        grad[f'ffn{f}_w_up'] = jnp.stack([mine_gu[l, f][:, fp:fp + fb] for l in range(depth)])
        grad[f'ffn{f}_w_down'] = jnp.stack([mine_dn[l, f][:fb] for l in range(depth)])
    start = 0
    for n in REST:
        grad[n] = mine_rest[start:start + rows_of[n]].reshape(w[n].shape)
        start += rows_of[n]
    small_sum = small_sum.reshape(-1)
    start = 0
    for n in VECTORS:
        grad[n] = small_sum[start:start + w[n].size].reshape(w[n].shape)
        start += w[n].size
    cw_grad = small_sum[start:start + depth * CONV_K * CONV_W].reshape(depth, CONV_K, CONV_W)
    nb = conv_w.shape[-1]
    grad['conv_w'] = lax.dynamic_slice_in_dim(cw_grad, my_block * nb, nb, axis=2)

    delta, new_m, new_v = {}, {}, {}
    for n in BIG + ['conv_w']:
        shp = w[n].shape
        two_d = lambda a: a.reshape(-1, shp[-1])
        dl, mn, vn = _adamw(two_d(w[n]), two_d(grad[n]), two_d(mom[n]), two_d(var[n]), "adamw_" + n)
        delta[n], new_m[n], new_v[n] = dl.reshape(shp), mn.reshape(shp), vn.reshape(shp)
    vcat = lambda src: jnp.concatenate([src[n].reshape(-1) for n in VECTORS]).reshape(-1, LANE)
    dl, mn, vn = _adamw(vcat(w), vcat(grad), vcat(mom), vcat(var), "adamw_vectors")
    start = 0
    for n in VECTORS:
        sl = lambda a: a.reshape(-1)[start:start + w[n].size].reshape(w[n].shape)
        delta[n], new_m[n], new_v[n] = sl(dl), sl(mn), sl(vn)
        start += w[n].size

    return (loss, grad_x, *[grad[n] for n in WEIGHTS], *[delta[n] for n in WEIGHTS],
            *[new_m[n] for n in WEIGHTS], *[new_v[n] for n in WEIGHTS])
```

```python
import functools

import jax
import jax.numpy as jnp
from jax import lax
from jax.experimental import pallas as pl
from jax.experimental.pallas import tpu as pltpu

F32, BF16 = jnp.float32, jnp.bfloat16

N_DEV = 8
N_HEADS = 8
QK_NOPE, QK_ROPE, V_DIM = 64, 32, 64
QK_DIM = QK_NOPE + QK_ROPE
HEAD_PAD = 128
Q_LORA, KV_LORA = 384, 256
LAT = Q_LORA + KV_LORA
CONV_W, CONV_K = 512, 31
CONV_HALO = 32
CHUNK = 64
ROPE_THETA = 10000.0
EPS = 1e-6
ATTN_SCALE = QK_DIM ** -0.5
P_KPE = LAT
P_A = LAT + HEAD_PAD
P_G = P_A + CONV_W
P_COLS = P_G + CONV_W

ADAM_LR, ADAM_B1, ADAM_B2, ADAM_EPS, ADAM_WD, ADAM_STEP = 0.001, 0.9, 0.999, 1e-08, 0.01, 10

V7X_VMEM_BYTES = 64 << 20
VMEM_LIMIT = V7X_VMEM_BYTES - (8 << 20)
MM_VMEM_BUDGET = 36 << 20
LANE = 128
ROW_TILE = 512
ATTN_BLOCK = 512
CONV_TILE = 256

MESH_ID = pl.DeviceIdType.MESH
ANY = pl.BlockSpec(memory_space=pl.ANY)
VMEM_SPEC = pl.BlockSpec(memory_space=pltpu.VMEM)
HBM_SPEC = pl.BlockSpec(memory_space=pltpu.HBM)
SEM_SPEC = pl.BlockSpec(memory_space=pltpu.SEMAPHORE)
DATAFLOW = pltpu.SideEffectType.DATAFLOW_SIDE_EFFECTING

WEIGHTS = ['ffn1_norm', 'ffn1_w_gate', 'ffn1_w_up', 'ffn1_w_down', 'mix_norm', 'w_in', 'q_latent_norm', 'w_uq',
           'kv_latent_norm', 'w_ukv', 'q_norm', 'k_norm', 'conv_w', 'conv_b', 'conv_ln_g', 'conv_ln_b', 'w_out',
           'ffn2_norm', 'ffn2_w_gate', 'ffn2_w_up', 'ffn2_w_down', 'post_norm']
COL_SHARDED = ['ffn1_w_gate', 'ffn1_w_up', 'w_in', 'w_uq', 'w_ukv', 'ffn2_w_gate', 'ffn2_w_up']
ROW_SHARDED = ['ffn1_w_down', 'w_out', 'ffn2_w_down']
REST = ['w_in', 'w_uq', 'w_ukv', 'w_out']
BIG = ['ffn1_w_gate', 'ffn1_w_up', 'ffn1_w_down', 'w_in', 'w_uq', 'w_ukv', 'w_out', 'ffn2_w_gate', 'ffn2_w_up',
       'ffn2_w_down']
VECTORS = ['ffn1_norm', 'mix_norm', 'q_latent_norm', 'kv_latent_norm', 'q_norm', 'k_norm', 'conv_b', 'conv_ln_g',
           'conv_ln_b', 'ffn2_norm', 'post_norm']


def _params(*sem):
    return pltpu.CompilerParams(dimension_semantics=sem if sem else None, vmem_limit_bytes=VMEM_LIMIT)


def _tile(n, cap):
    if n <= cap:
        return n
    best = 0
    for d in range(LANE, cap + 1, LANE):
        if n % d == 0:
            best = d
    assert best, (n, cap)
    return best


def _row_tile(n, cap=ROW_TILE, mult=8):
    if n <= cap:
        return n
    best = 0
    for d in range(mult, cap + 1, mult):
        if n % d == 0:
            best = d
    assert best, (n, cap)
    return best


def _mm(a, b, *, name, ta=False, tb=False, res=None, scale=1.0, out_dtype=F32, tm=None, tn=None, blocks=None,
        after=None):
    (kdim, m) = a.shape if ta else a.shape[::-1]
    (n, kb) = b.shape if tb else b.shape[::-1]
    assert kdim == kb, (a.shape, b.shape, ta, tb)
    tm, tn = tm or _tile(m, 512), tn or _tile(n, 1024)
    if blocks is not None:
        tm, tn = (tm, n // blocks[1]) if blocks[0] == 'col' else (m // blocks[1], tn)
    size = lambda arr: jnp.dtype(arr.dtype).itemsize
    out_bytes = tm * tn * ((6 if blocks is not None else jnp.dtype(out_dtype).itemsize) + (4 if res is not None else 0))

    def vmem_need(tk):
        return 2 * (tm * tk * size(a) + tk * tn * size(b) + out_bytes) + (tm * tn * 4 if tk < kdim else 0)

    tk = kdim
    for cand in [d for d in range(kdim - LANE, 0, -LANE) if kdim % d == 0]:
        if vmem_need(tk) <= MM_VMEM_BUDGET:
            break
        tk = cand
    nk = kdim // tk
    n_in = 2 + (res is not None) + (after is not None)
    n_out = 2 if blocks is not None else 1
    dims = (((0 if ta else 1,), (1 if tb else 0,)), ((), ()))

    def body(*refs):
        a_ref, b_ref = refs[0], refs[1]
        r_ref = refs[2] if res is not None else None
        o_refs = refs[n_in:n_in + n_out]
        acc_ref = refs[-1] if nk > 1 else None
        part = lax.dot_general(a_ref[...].astype(BF16), b_ref[...].astype(BF16), dims, preferred_element_type=F32)

        def finish(acc):
            if scale != 1.0:
                acc = acc * scale
            if r_ref is not None:
                acc = r_ref[...] + acc
            for o_ref in o_refs:
                o_ref[...] = acc.astype(o_ref.dtype)

        if nk == 1:
            finish(part)
        else:
            k = pl.program_id(2)

            @pl.when(k == 0)
            def _():
                acc_ref[...] = part

            @pl.when(k > 0)
            def _():
                acc_ref[...] += part

            @pl.when(k == nk - 1)
            def _():
                finish(acc_ref[...])

    a_spec = pl.BlockSpec((tk, tm), lambda i, j, k: (k, i)) if ta else pl.BlockSpec((tm, tk), lambda i, j, k: (i, k))
    b_spec = pl.BlockSpec((tn, tk), lambda i, j, k: (j, k)) if tb else pl.BlockSpec((tk, tn), lambda i, j, k: (k, j))
    plain = pl.BlockSpec((tm, tn), lambda i, j, k: (i, j))
    if blocks is None:
        out_specs, out_shape = plain, jax.ShapeDtypeStruct((m, n), out_dtype)
    else:
        if blocks[0] == 'col':
            o_spec, shp = pl.BlockSpec((None, tm, tn), lambda i, j, k: (j, i, 0)), (blocks[1], m, tn)
        else:
            o_spec, shp = pl.BlockSpec((None, tm, tn), lambda i, j, k: (i, 0, j)), (blocks[1], tm, n)
        out_specs, out_shape = (o_spec, o_spec), (jax.ShapeDtypeStruct(shp, F32), jax.ShapeDtypeStruct(shp, BF16))
    in_specs = [a_spec, b_spec] + ([plain] if res is not None else [])
    args = (a, b) + ((res,) if res is not None else ())
    if after is not None:
        in_specs.append(pl.BlockSpec(after.shape, lambda i, j, k: (0, 0)))
        args += (after,)
    return pl.pallas_call(
        body, name=name, grid=(m // tm, n // tn, nk), in_specs=in_specs, out_specs=out_specs, out_shape=out_shape,
        scratch_shapes=[pltpu.VMEM((tm, tn), F32)] if nk > 1 else [],
        compiler_params=_params("parallel", "parallel", "arbitrary"),
    )(*args)


def _rms_fwd(x, g, out_dtype, name):
    t, d = x.shape
    tm = _row_tile(t)

    def body(x_ref, g_ref, o_ref):
        xv = x_ref[...]
        r = lax.rsqrt(jnp.mean(xv * xv, axis=-1, keepdims=True) + EPS)
        o_ref[...] = (xv * r * g_ref[...]).astype(o_ref.dtype)

    return pl.pallas_call(
        body, name=name, grid=(t // tm,),
        in_specs=[pl.BlockSpec((tm, d), lambda i: (i, 0)), pl.BlockSpec((1, d), lambda i: (0, 0))],
        out_specs=pl.BlockSpec((tm, d), lambda i: (i, 0)),
        out_shape=jax.ShapeDtypeStruct((t, d), out_dtype), compiler_params=_params("parallel"),
    )(x, g.reshape(1, d))


def _rms_bwd(x, g, dh, res, name):
    t, d = x.shape
    tm = _row_tile(t)

    def body(*refs):
        x_ref, g_ref, dh_ref = refs[:3]
        r_ref = refs[3] if res is not None else None
        dx_ref, dxb_ref, dg_ref = refs[-3:]
        xv, dhv = x_ref[...], dh_ref[...]
        r = lax.rsqrt(jnp.mean(xv * xv, axis=-1, keepdims=True) + EPS)
        y = xv * r
        dy = dhv * g_ref[...]
        dx = r * (dy - y * jnp.mean(dy * y, axis=-1, keepdims=True))
        if r_ref is not None:
            dx = r_ref[...] + dx
        dx_ref[...] = dx
        dxb_ref[...] = dx.astype(BF16)

        @pl.when(pl.program_id(0) == 0)
        def _():
            dg_ref[...] = jnp.zeros_like(dg_ref)

        dg_ref[...] += jnp.sum(dhv * y, axis=0, keepdims=True)

    row = pl.BlockSpec((tm, d), lambda i: (i, 0))
    vec = pl.BlockSpec((1, d), lambda i: (0, 0))
    args = (x, g.reshape(1, d), dh) + ((res,) if res is not None else ())
    dx, dxb, dg = pl.pallas_call(
        body, name=name, grid=(t // tm,), in_specs=[row, vec, row] + ([row] if res is not None else []),
        out_specs=(row, row, vec),
        out_shape=(jax.ShapeDtypeStruct((t, d), F32), jax.ShapeDtypeStruct((t, d), BF16),
                   jax.ShapeDtypeStruct((1, d), F32)),
        compiler_params=_params("arbitrary"),
    )(*args)
    return dx, dxb, dg.reshape(d)


FFN_PAIR = 2


def _ffn_up(h, wgu, fp, name):
    t, d = h.shape
    tm, tn = _tile(t, 512), FFN_PAIR * 2 * fp
    nj = wgu.shape[1] // tn

    def body(h_ref, w_ref, ab_ref, z_ref):
        ab = jnp.dot(h_ref[...], w_ref[...], preferred_element_type=F32)
        ab_ref[...] = ab.astype(ab_ref.dtype)
        for e in range(FFN_PAIR):
            av, bv = ab[:, 2 * fp * e:2 * fp * e + fp], ab[:, 2 * fp * e + fp:2 * fp * (e + 1)]
            z_ref[:, fp * e:fp * (e + 1)] = (av * jax.nn.sigmoid(av) * bv).astype(z_ref.dtype)

    return pl.pallas_call(
        body, name=name, grid=(nj, t // tm),
        in_specs=[pl.BlockSpec((tm, d), lambda j, i: (i, 0)), pl.BlockSpec((d, tn), lambda j, i: (0, j))],
        out_specs=(pl.BlockSpec((tm, tn), lambda j, i: (i, j)), pl.BlockSpec((tm, tn // 2), lambda j, i: (i, j))),
        out_shape=(jax.ShapeDtypeStruct((t, wgu.shape[1]), BF16), jax.ShapeDtypeStruct((t, wgu.shape[1] // 2), BF16)),
        compiler_params=_params("parallel", "parallel"),
    )(h, wgu)


def _ffn_dab(dyb, wd, ab, fp, name):
    t, d = dyb.shape
    tm, tn = _tile(t, 512), FFN_PAIR * 2 * fp
    nj = ab.shape[1] // tn

    def body(dy_ref, wd_ref, ab_ref, dab_ref):
        dz = _dot_nt(dy_ref[...], wd_ref[...]) * 0.5
        for e in range(FFN_PAIR):
            av = ab_ref[:, 2 * fp * e:2 * fp * e + fp].astype(F32)
            bv = ab_ref[:, 2 * fp * e + fp:2 * fp * (e + 1)].astype(F32)
            dze = dz[:, fp * e:fp * (e + 1)]
            s = jax.nn.sigmoid(av)
            dab_ref[:, 2 * fp * e:2 * fp * e + fp] = (dze * bv * (s * (1.0 + av * (1.0 - s)))).astype(dab_ref.dtype)
            dab_ref[:, 2 * fp * e + fp:2 * fp * (e + 1)] = (dze * (av * s)).astype(dab_ref.dtype)

    return pl.pallas_call(
        body, name=name, grid=(nj, t // tm),
        in_specs=[pl.BlockSpec((tm, d), lambda j, i: (i, 0)), pl.BlockSpec((tn // 2, d), lambda j, i: (j, 0)),
                  pl.BlockSpec((tm, tn), lambda j, i: (i, j))],
        out_specs=pl.BlockSpec((tm, tn), lambda j, i: (i, j)),
        out_shape=jax.ShapeDtypeStruct(ab.shape, BF16), compiler_params=_params("parallel", "parallel"),
    )(dyb, wd, ab)


def _lat_norm_fwd(p, g_q, g_kv, name):
    t = p.shape[0]
    tm = _row_tile(t)

    def body(p_ref, gq_ref, gkv_ref, q_ref, kv_ref):
        for lo, hi, g_ref, o_ref in ((0, Q_LORA, gq_ref, q_ref), (Q_LORA, LAT, gkv_ref, kv_ref)):
            xv = p_ref[:, lo:hi]
            r = lax.rsqrt(jnp.mean(xv * xv, axis=-1, keepdims=True) + EPS)
            o_ref[...] = (xv * r * g_ref[...]).astype(o_ref.dtype)

    return pl.pallas_call(
        body, name=name, grid=(t // tm,),
        in_specs=[pl.BlockSpec((tm, P_COLS), lambda i: (i, 0)), pl.BlockSpec((1, Q_LORA), lambda i: (0, 0)),
                  pl.BlockSpec((1, KV_LORA), lambda i: (0, 0))],
        out_specs=(pl.BlockSpec((tm, Q_LORA), lambda i: (i, 0)), pl.BlockSpec((tm, KV_LORA), lambda i: (i, 0))),
        out_shape=(jax.ShapeDtypeStruct((t, Q_LORA), BF16), jax.ShapeDtypeStruct((t, KV_LORA), BF16)),
        compiler_params=_params("parallel"),
    )(p, g_q.reshape(1, Q_LORA), g_kv.reshape(1, KV_LORA))


def _lat_norm_bwd(p, g_q, g_kv, dq, dkv, name):
    t = p.shape[0]
    tm = _row_tile(t)

    def body(p_ref, gq_ref, gkv_ref, dq_ref, dkv_ref, dp_ref, dgq_ref, dgkv_ref):
        first = pl.program_id(0) == 0
        for lo, hi, g_ref, d_ref, dg_ref in ((0, Q_LORA, gq_ref, dq_ref, dgq_ref),
                                             (Q_LORA, LAT, gkv_ref, dkv_ref, dgkv_ref)):
            xv, dhv = p_ref[:, lo:hi], d_ref[...]
            r = lax.rsqrt(jnp.mean(xv * xv, axis=-1, keepdims=True) + EPS)
            y = xv * r
            dy = dhv * g_ref[...]
            dp_ref[:, lo:hi] = r * (dy - y * jnp.mean(dy * y, axis=-1, keepdims=True))

            @pl.when(first)
            def _():
                dg_ref[...] = jnp.zeros_like(dg_ref)

            dg_ref[...] += jnp.sum(dhv * y, axis=0, keepdims=True)

    vq = pl.BlockSpec((1, Q_LORA), lambda i: (0, 0))
    vkv = pl.BlockSpec((1, KV_LORA), lambda i: (0, 0))
    dp, dgq, dgkv = pl.pallas_call(
        body, name=name, grid=(t // tm,),
        in_specs=[pl.BlockSpec((tm, P_COLS), lambda i: (i, 0)), vq, vkv,
                  pl.BlockSpec((tm, Q_LORA), lambda i: (i, 0)), pl.BlockSpec((tm, KV_LORA), lambda i: (i, 0))],
        out_specs=(pl.BlockSpec((tm, LAT), lambda i: (i, 0)), vq, vkv),
        out_shape=(jax.ShapeDtypeStruct((t, LAT), F32), jax.ShapeDtypeStruct((1, Q_LORA), F32),
                   jax.ShapeDtypeStruct((1, KV_LORA), F32)),
        compiler_params=_params("arbitrary"),
    )(p, g_q.reshape(1, Q_LORA), g_kv.reshape(1, KV_LORA), dq, dkv)
    return dp, dgq.reshape(Q_LORA), dgkv.reshape(KV_LORA)


def _rope_tables(t):
    half = QK_ROPE // 2
    pos = jnp.arange(t, dtype=F32)
    inv_freq = 1.0 / (ROPE_THETA ** (jnp.arange(0, QK_ROPE, 2, dtype=F32) / QK_ROPE))
    ang = pos[:, None] * inv_freq[None, :]
    cos, sin = jnp.cos(ang), jnp.sin(ang)
    z = lambda n: jnp.zeros((t, n), F32)
    c_tab = jnp.concatenate([jnp.ones((t, QK_NOPE), F32), cos, cos, z(HEAD_PAD - QK_DIM)], axis=1)
    sa_tab = jnp.concatenate([z(QK_NOPE), -sin, z(half), z(HEAD_PAD - QK_DIM)], axis=1)
    sb_tab = jnp.concatenate([z(QK_NOPE), z(half), sin, z(HEAD_PAD - QK_DIM)], axis=1)
    return c_tab, sa_tab, sb_tab


def _rope(x, c, sa, sb):
    half = QK_ROPE // 2
    return x * c + pltpu.roll(x, HEAD_PAD - half, 1) * sa + pltpu.roll(x, half, 1) * sb


def _rope_t(d, c, sa, sb):
    half = QK_ROPE // 2
    return d * c + pltpu.roll(d * sa, half, 1) + pltpu.roll(d * sb, HEAD_PAD - half, 1)


def _head_rms(x):
    r = lax.rsqrt(jnp.sum(x * x, axis=-1, keepdims=True) * (1.0 / QK_DIM) + EPS)
    return x * r, r


def _qk_prep_fwd(q_raw, k_raw, p, gq, gk, tabs, name):
    t = q_raw.shape[0]
    tm = _row_tile(t)

    def body(q_ref, k_ref, p_ref, gq_ref, gk_ref, c_ref, sa_ref, sb_ref, qo_ref, ko_ref):
        c, sa, sb = c_ref[...], sa_ref[...], sb_ref[...]
        qn, _ = _head_rms(q_ref[...])
        qo_ref[...] = _rope(qn * gq_ref[...], c, sa, sb).astype(qo_ref.dtype)
        kn, _ = _head_rms(k_ref[...] + p_ref[...])
        ko_ref[...] = _rope(kn * gk_ref[...], c, sa, sb).astype(ko_ref.dtype)

    head = pl.BlockSpec((tm, HEAD_PAD), lambda i, h: (i, h))
    tab = pl.BlockSpec((tm, HEAD_PAD), lambda i, h: (i, 0))
    vec = pl.BlockSpec((1, HEAD_PAD), lambda i, h: (0, 0))
    kpe = pl.BlockSpec((tm, HEAD_PAD), lambda i, h: (i, P_KPE // HEAD_PAD))
    return pl.pallas_call(
        body, name=name, grid=(t // tm, N_HEADS), in_specs=[head, head, kpe, vec, vec, tab, tab, tab],
        out_specs=(head, head),
        out_shape=(jax.ShapeDtypeStruct(q_raw.shape, BF16), jax.ShapeDtypeStruct(k_raw.shape, BF16)),
        compiler_params=_params("parallel", "parallel"),
    )(q_raw, k_raw, p, gq.reshape(1, HEAD_PAD), gk.reshape(1, HEAD_PAD), *tabs)


def _qk_prep_bwd(q_raw, k_raw, p, dq, dk, gq, gk, tabs, name):
    t = q_raw.shape[0]
    tm = _row_tile(t)

    def body(q_ref, k_ref, p_ref, dq_ref, dk_ref, gq_ref, gk_ref, c_ref, sa_ref, sb_ref,
             dqr_ref, dkr_ref, dkpe_ref, dgq_ref, dgk_ref):
        i, h = pl.program_id(0), pl.program_id(1)
        c, sa, sb = c_ref[...], sa_ref[...], sb_ref[...]

        def one(x, d, g_ref, dg_ref):
            n, r = _head_rms(x)
            dng = _rope_t(d, c, sa, sb)

            @pl.when(jnp.logical_and(i == 0, h == 0))
            def _():
                dg_ref[...] = jnp.zeros_like(dg_ref)

            dg_ref[...] += jnp.sum(dng * n, axis=0, keepdims=True)
            dn = dng * g_ref[...]
            return r * (dn - n * (jnp.sum(dn * n, axis=-1, keepdims=True) * (1.0 / QK_DIM)))

        dqr_ref[...] = one(q_ref[...], dq_ref[...], gq_ref, dgq_ref)
        dkr = one(k_ref[...] + p_ref[...], dk_ref[...], gk_ref, dgk_ref)
        dkr_ref[...] = dkr

        @pl.when(h == 0)
        def _():
            dkpe_ref[...] = dkr

        @pl.when(h > 0)
        def _():
            dkpe_ref[...] += dkr

    head = pl.BlockSpec((tm, HEAD_PAD), lambda i, h: (i, h))
    tab = pl.BlockSpec((tm, HEAD_PAD), lambda i, h: (i, 0))
    vec = pl.BlockSpec((1, HEAD_PAD), lambda i, h: (0, 0))
    kpe = pl.BlockSpec((tm, HEAD_PAD), lambda i, h: (i, P_KPE // HEAD_PAD))
    dqr, dkr, dkpe, dgq, dgk = pl.pallas_call(
        body, name=name, grid=(t // tm, N_HEADS), in_specs=[head, head, kpe, head, head, vec, vec, tab, tab, tab],
        out_specs=(head, head, tab, vec, vec),
        out_shape=(jax.ShapeDtypeStruct(q_raw.shape, F32), jax.ShapeDtypeStruct(k_raw.shape, F32),
                   jax.ShapeDtypeStruct((t, HEAD_PAD), F32), jax.ShapeDtypeStruct((1, HEAD_PAD), F32),
                   jax.ShapeDtypeStruct((1, HEAD_PAD), F32)),
        compiler_params=_params("arbitrary", "arbitrary"),
    )(q_raw, k_raw, p, dq, dk, gq.reshape(1, HEAD_PAD), gk.reshape(1, HEAD_PAD), *tabs)
    return dqr, dkr, dkpe, dgq.reshape(HEAD_PAD), dgk.reshape(HEAD_PAD)


def _dot_nt(a, b):
    return lax.dot_general(a, b, (((1,), (1,)), ((), ())), preferred_element_type=F32)


def _dot_tn(a, b):
    return lax.dot_general(a, b, (((0,), (0,)), ((), ())), preferred_element_type=F32)


def _diag_mask():
    rows = lax.broadcasted_iota(jnp.int32, (ATTN_BLOCK, ATTN_BLOCK), 0) // CHUNK
    cols = lax.broadcasted_iota(jnp.int32, (ATTN_BLOCK, ATTN_BLOCK), 1) // CHUNK
    return cols <= rows


def _attn_fwd(q, k, v, name):
    t = q.shape[0]
    bq = ATTN_BLOCK
    nq = t // bq

    def body(q_ref, k_ref, v_ref, o_ref, lse_ref):
        i = pl.program_id(1)
        qv = q_ref[...]

        def block(j, carry, masked):
            m, l, acc = carry
            rows = pl.ds(pl.multiple_of(j * bq, bq), bq)
            s = _dot_nt(qv, k_ref[rows, :]) * ATTN_SCALE
            if masked:
                s = jnp.where(_diag_mask(), s, -1e30)
            m_new = jnp.maximum(m, jnp.max(s, axis=-1, keepdims=True))
            alpha = jnp.exp(m - m_new)
            pe = jnp.exp(s - m_new)
            l = alpha * l + jnp.sum(pe, axis=-1, keepdims=True)
            acc = alpha * acc + jnp.dot(pe.astype(BF16), v_ref[rows, :], preferred_element_type=F32)
            return m_new, l, acc

        init = (jnp.full((bq, 1), -1e30, F32), jnp.zeros((bq, 1), F32), jnp.zeros((bq, HEAD_PAD), F32))
        carry = lax.fori_loop(0, i, lambda j, cr: block(j, cr, False), init)
        m, l, acc = block(i, carry, True)
        o_ref[...] = acc / l
        lse_ref[...] = jnp.broadcast_to(m + jnp.log(l), (bq, HEAD_PAD))

    blk = pl.BlockSpec((bq, HEAD_PAD), lambda h, i: (i, h))
    full = pl.BlockSpec((t, HEAD_PAD), lambda h, i: (0, h))
    return pl.pallas_call(
        body, name=name, grid=(N_HEADS, nq), in_specs=[blk, full, full], out_specs=(blk, blk),
        out_shape=(jax.ShapeDtypeStruct(q.shape, F32), jax.ShapeDtypeStruct(q.shape, F32)),
        compiler_params=_params("parallel", "parallel"),
    )(q, k, v)


def _attn_bwd(q, k, v, o, lse, do, name):
    t = q.shape[0]
    bq = ATTN_BLOCK
    nq = t // bq

    def body(q_ref, k_ref, v_ref, o_ref, lse_ref, do_ref, dq_ref, dk_ref, dv_ref, delta_ref):
        def rows_of(i):
            return pl.ds(pl.multiple_of(i * bq, bq), bq)

        def prep(i, _):
            r = rows_of(i)
            delta_ref[r, :] = jnp.broadcast_to(jnp.sum(do_ref[r, :] * o_ref[r, :], axis=-1, keepdims=True),
                                               (bq, HEAD_PAD))
            dq_ref[r, :] = jnp.zeros((bq, HEAD_PAD), F32)
            return 0

        lax.fori_loop(0, nq, prep, 0)

        def key_block(j, _):
            rj = rows_of(j)
            kb, vb = k_ref[rj, :], v_ref[rj, :]

            def query_block(i, carry, masked):
                dk, dv = carry
                ri = rows_of(i)
                qb, dob = q_ref[ri, :], do_ref[ri, :].astype(BF16)
                s = _dot_nt(qb, kb) * ATTN_SCALE
                if masked:
                    s = jnp.where(_diag_mask(), s, -1e30)
                pe = jnp.exp(s - lse_ref[ri, :][:, :1])
                dp = _dot_nt(dob, vb)
                ds = (pe * (dp - delta_ref[ri, :][:, :1]) * ATTN_SCALE).astype(BF16)
                dq_ref[ri, :] += jnp.dot(ds, kb, preferred_element_type=F32)
                return dk + _dot_tn(ds, qb), dv + _dot_tn(pe.astype(BF16), dob)

            zero = jnp.zeros((bq, HEAD_PAD), F32)
            carry = query_block(j, (zero, zero), True)
            dk, dv = lax.fori_loop(j + 1, nq, lambda i, cr: query_block(i, cr, False), carry)
            dk_ref[rj, :] = dk
            dv_ref[rj, :] = dv
            return 0

        lax.fori_loop(0, nq, key_block, 0)

    full = pl.BlockSpec((t, HEAD_PAD), lambda h: (0, h))
    shp = jax.ShapeDtypeStruct(q.shape, F32)
    return pl.pallas_call(
        body, name=name, grid=(N_HEADS,), in_specs=[full] * 6, out_specs=(full, full, full),
        out_shape=(shp, shp, shp), scratch_shapes=[pltpu.VMEM((t, HEAD_PAD), F32)],
        compiler_params=_params("parallel"),
    )(q, k, v, o, lse, do)


def _glu_ext(pc_ref, pp_ref, u_ref, tm, first):
    u_ref[CONV_HALO:CONV_HALO + tm, :] = pc_ref[:, P_A:P_G] * jax.nn.sigmoid(pc_ref[:, P_G:P_COLS])
    up = pp_ref[tm - CONV_HALO:tm, P_A:P_G] * jax.nn.sigmoid(pp_ref[tm - CONV_HALO:tm, P_G:P_COLS])
    u_ref[0:CONV_HALO, :] = jnp.where(first, 0.0, up)


def _conv_fwd(p, w, b, ln_g, ln_b, name):
    t = p.shape[0]
    tm = _row_tile(t, CONV_TILE)
    off = CONV_HALO - (CONV_K - 1)

    def body(pc_ref, pp_ref, w_ref, b_ref, g_ref, bb_ref, y_ref, o_ref, u_ref):
        _glu_ext(pc_ref, pp_ref, u_ref, tm, pl.program_id(0) == 0)
        acc = jnp.zeros((tm, CONV_W), F32)
        for kk in range(CONV_K):
            acc = acc + w_ref[kk:kk + 1, :] * u_ref[off + kk:off + kk + tm, :]
        y = acc + b_ref[...]
        y_ref[...] = y
        xc = y - jnp.mean(y, axis=-1, keepdims=True)
        lo = xc * lax.rsqrt(jnp.mean(xc * xc, axis=-1, keepdims=True) + EPS) * g_ref[...] + bb_ref[...]
        o_ref[...] = (lo * jax.nn.sigmoid(lo)).astype(o_ref.dtype)

    prow = pl.BlockSpec((tm, P_COLS), lambda i: (i, 0))
    pprev = pl.BlockSpec((tm, P_COLS), lambda i: (jnp.maximum(i - 1, 0), 0))
    vec = pl.BlockSpec((1, CONV_W), lambda i: (0, 0))
    row = pl.BlockSpec((tm, CONV_W), lambda i: (i, 0))
    return pl.pallas_call(
        body, name=name, grid=(t // tm,),
        in_specs=[prow, pprev, pl.BlockSpec((CONV_HALO, CONV_W), lambda i: (0, 0)), vec, vec, vec],
        out_specs=(row, row),
        out_shape=(jax.ShapeDtypeStruct((t, CONV_W), F32), jax.ShapeDtypeStruct((t, CONV_W), BF16)),
        scratch_shapes=[pltpu.VMEM((tm + CONV_HALO, CONV_W), F32)], compiler_params=_params("parallel"),
    )(p, p, w, b.reshape(1, CONV_W), ln_g.reshape(1, CONV_W), ln_b.reshape(1, CONV_W))


def _conv_bwd_ln(y, dout, ln_g, ln_b, name):
    t = y.shape[0]
    tm = _row_tile(t)

    def body(y_ref, d_ref, g_ref, bb_ref, dy_ref, dg_ref, db_ref, dcb_ref):
        yv = y_ref[...]
        xc = yv - jnp.mean(yv, axis=-1, keepdims=True)
        r = lax.rsqrt(jnp.mean(xc * xc, axis=-1, keepdims=True) + EPS)
        n = xc * r
        lo = n * g_ref[...] + bb_ref[...]
        s = jax.nn.sigmoid(lo)
        dlo = d_ref[...] * (s * (1.0 + lo * (1.0 - s)))
        dn = dlo * g_ref[...]
        dy = r * (dn - jnp.mean(dn, axis=-1, keepdims=True) - n * jnp.mean(dn * n, axis=-1, keepdims=True))
        dy_ref[...] = dy

        @pl.when(pl.program_id(0) == 0)
        def _():
            dg_ref[...] = jnp.zeros_like(dg_ref)
            db_ref[...] = jnp.zeros_like(db_ref)
            dcb_ref[...] = jnp.zeros_like(dcb_ref)

        dg_ref[...] += jnp.sum(dlo * n, axis=0, keepdims=True)
        db_ref[...] += jnp.sum(dlo, axis=0, keepdims=True)
        dcb_ref[...] += jnp.sum(dy, axis=0, keepdims=True)

    row = pl.BlockSpec((tm, CONV_W), lambda i: (i, 0))
    vec = pl.BlockSpec((1, CONV_W), lambda i: (0, 0))
    vshape = jax.ShapeDtypeStruct((1, CONV_W), F32)
    dy, dg, db, dcb = pl.pallas_call(
        body, name=name, grid=(t // tm,), in_specs=[row, row, vec, vec], out_specs=(row, vec, vec, vec),
        out_shape=(jax.ShapeDtypeStruct((t, CONV_W), F32), vshape, vshape, vshape),
        compiler_params=_params("arbitrary"),
    )(y, dout, ln_g.reshape(1, CONV_W), ln_b.reshape(1, CONV_W))
    return dy, dg.reshape(CONV_W), db.reshape(CONV_W), dcb.reshape(CONV_W)


def _conv_bwd_taps(p, dy, w, name):
    t = p.shape[0]
    tm = _row_tile(t, CONV_TILE)
    nt = t // tm
    off = CONV_HALO - (CONV_K - 1)

    def body(pc_ref, pp_ref, dyc_ref, dyn_ref, w_ref, dag_ref, dw_ref, u_ref, dye_ref):
        i = pl.program_id(0)
        _glu_ext(pc_ref, pp_ref, u_ref, tm, i == 0)
        dyc = dyc_ref[...]
        dye_ref[0:tm, :] = dyc
        dye_ref[tm:tm + CONV_HALO, :] = jnp.where(i == nt - 1, 0.0, dyn_ref[0:CONV_HALO, :])

        @pl.when(i == 0)
        def _():
            dw_ref[...] = jnp.zeros_like(dw_ref)

        du = jnp.zeros((tm, CONV_W), F32)
        for kk in range(CONV_K):
            dw_ref[kk:kk + 1, :] += jnp.sum(dyc * u_ref[off + kk:off + kk + tm, :], axis=0, keepdims=True)
            back = CONV_K - 1 - kk
            du = du + w_ref[kk:kk + 1, :] * dye_ref[back:back + tm, :]
        av, gv = pc_ref[:, P_A:P_G], pc_ref[:, P_G:P_COLS]
        s = jax.nn.sigmoid(gv)
        dag_ref[:, 0:CONV_W] = du * s
        dag_ref[:, CONV_W:2 * CONV_W] = du * av * (s * (1.0 - s))

    prow = pl.BlockSpec((tm, P_COLS), lambda i: (i, 0))
    pprev = pl.BlockSpec((tm, P_COLS), lambda i: (jnp.maximum(i - 1, 0), 0))
    row = pl.BlockSpec((tm, CONV_W), lambda i: (i, 0))
    nxt = pl.BlockSpec((tm, CONV_W), lambda i: (jnp.minimum(i + 1, nt - 1), 0))
    wspec = pl.BlockSpec((CONV_HALO, CONV_W), lambda i: (0, 0))
    return pl.pallas_call(
        body, name=name, grid=(nt,), in_specs=[prow, pprev, row, nxt, wspec],
        out_specs=(pl.BlockSpec((tm, 2 * CONV_W), lambda i: (i, 0)), wspec),
        out_shape=(jax.ShapeDtypeStruct((t, 2 * CONV_W), F32), jax.ShapeDtypeStruct((CONV_HALO, CONV_W), F32)),
        scratch_shapes=[pltpu.VMEM((tm + CONV_HALO, CONV_W), F32), pltpu.VMEM((tm + CONV_HALO, CONV_W), F32)],
        compiler_params=_params("arbitrary"),
    )(p, p, dy, dy, w)


def _loss_head(y, target, name):
    t, d = y.shape
    tm = _row_tile(t)

    def body(y_ref, t_ref, l_ref, dy_ref):
        err = y_ref[...] - t_ref[...]
        dy_ref[...] = err * (1.0 / d)

        @pl.when(pl.program_id(0) == 0)
        def _():
            l_ref[...] = jnp.zeros_like(l_ref)

        row = jnp.sum(err * err, axis=-1, keepdims=True) * (0.5 / d)
        l_ref[...] += jnp.broadcast_to(jnp.sum(row, axis=0, keepdims=True), (1, LANE))

    row = pl.BlockSpec((tm, d), lambda i: (i, 0))
    return pl.pallas_call(
        body, name=name, grid=(t // tm,), in_specs=[row, row],
        out_specs=(pl.BlockSpec((1, LANE), lambda i: (0, 0)), row),
        out_shape=(jax.ShapeDtypeStruct((1, LANE), F32), jax.ShapeDtypeStruct((t, d), F32)),
        compiler_params=_params("arbitrary"),
    )(y, target)


def _adamw(w, g, m, v, name):
    r, c = w.shape
    tr = _row_tile(r, 256)
    c1, c2 = 1.0 - ADAM_B1 ** ADAM_STEP, 1.0 - ADAM_B2 ** ADAM_STEP

    def body(w_ref, g_ref, m_ref, v_ref, d_ref, mo_ref, vo_ref):
        gv = g_ref[...]
        mn = ADAM_B1 * m_ref[...] + (1.0 - ADAM_B1) * gv
        vn = ADAM_B2 * v_ref[...] + (1.0 - ADAM_B2) * (gv * gv)
        mo_ref[...] = mn
        vo_ref[...] = vn
        d_ref[...] = -ADAM_LR * ((mn / c1) / (jnp.sqrt(vn / c2) + ADAM_EPS) + ADAM_WD * w_ref[...])

    blk = pl.BlockSpec((tr, c), lambda i: (i, 0))
    shp = jax.ShapeDtypeStruct((r, c), F32)
    return pl.pallas_call(
        body, name=name, grid=(r // tr,), in_specs=[blk] * 4, out_specs=(blk, blk, blk), out_shape=(shp, shp, shp),
        compiler_params=_params("parallel"),
    )(w, g, m, v)


def _sum_parts(parts, name):
    r, c = parts[0].shape
    tr = _row_tile(r, 256)

    def body(*refs):
        acc = refs[0][...]
        for ref in refs[1:-1]:
            acc = acc + ref[...]
        refs[-1][...] = acc

    blk = pl.BlockSpec((tr, c), lambda i: (i, 0))
    return pl.pallas_call(
        body, name=name, grid=(r // tr,), in_specs=[blk] * len(parts), out_specs=blk,
        out_shape=jax.ShapeDtypeStruct((r, c), F32), compiler_params=_params("parallel"),
    )(*parts)


def _place():
    return lax.axis_index("x"), lax.axis_index("y"), lax.axis_index("c")


def _window(ref, block, size, axis):
    start = pl.multiple_of(block * size, LANE if size % LANE == 0 else 8)
    return ref.at[(slice(None),) * axis + (pl.ds(start, size),)]


def _all_gather(pieces, name, in_vmem=False):
    n_p = len(pieces)

    def body(*refs):
        x_refs, out_refs = refs[:n_p], refs[n_p:2 * n_p]
        send_sems, recv_sems, local_sems = refs[2 * n_p:]
        px, py, pc = _place()
        me, sibling = (px, py, pc), (px, py, 1 - pc)
        chips = [(1 - px, py), (px, 1 - py), (1 - px, 1 - py)]

        def win(p, block):
            bx, by, bc = block
            x, axis = pieces[p]
            return _window(out_refs[p], 4 * bx + 2 * by + bc, x.shape[axis], axis)

        def copy(k, p, block, to, local=False):
            return pltpu.make_async_remote_copy(
                src_ref=x_refs[p] if local else win(p, block), dst_ref=win(p, block),
                send_sem=send_sems.at[k, p], recv_sem=recv_sems.at[k, p], device_id=to, device_id_type=MESH_ID)

        every = range(n_p)
        mine = [pltpu.make_async_copy(x_refs[p], win(p, me), local_sems.at[p]) for p in every]
        first = [copy(0, p, me, sibling, local=True) for p in every]
        first += [copy(1 + j, p, me, (*chip, pc), local=True) for j, chip in enumerate(chips) for p in every]
        for cp in mine + first:
            cp.start()
        passed = []
        for j, chip in enumerate(chips):
            for p in every:
                copy(1 + j, p, (*chip, pc), me).wait_recv()
                passed.append(copy(4 + j, p, (*chip, pc), sibling))
                passed[-1].start()
        for p in every:
            copy(0, p, sibling, me).wait_recv()
        for j, chip in enumerate(chips):
            for p in every:
                copy(4 + j, p, (*chip, 1 - pc), me).wait_recv()
        for cp in first + passed:
            cp.wait_send()
        for cp in mine:
            cp.wait()

    def gathered(x, axis):
        return jax.ShapeDtypeStruct(x.shape[:axis] + (N_DEV * x.shape[axis],) + x.shape[axis + 1:], x.dtype)

    spec = VMEM_SPEC if in_vmem else ANY
    return pl.pallas_call(
        body, name=name, in_specs=[spec] * n_p, out_specs=[spec] * n_p, out_shape=[gathered(*pc_) for pc_ in pieces],
        scratch_shapes=[pltpu.SemaphoreType.DMA((7, n_p)), pltpu.SemaphoreType.DMA((7, n_p)),
                        pltpu.SemaphoreType.DMA((n_p,))],
        compiler_params=pltpu.CompilerParams(vmem_limit_bytes=VMEM_LIMIT),
    )(*[x for x, _ in pieces])


def _start_copies(bufs, n_copies, plan, name, after=None):
    nb = len(bufs)
    n_in = nb + (after is not None)

    def body(*refs):
        send_sems, recv_sems, token = refs[n_in], refs[n_in + 1], refs[-1]
        for i, (src, dst, dev) in enumerate(plan(refs[:nb])):
            pltpu.make_async_remote_copy(src_ref=src, dst_ref=dst, send_sem=send_sems.at[i], recv_sem=recv_sems.at[i],
                                         device_id=dev, device_id_type=MESH_ID).start()
        token[...] = jnp.zeros_like(token)

    out = pl.pallas_call(
        body, name=name, in_specs=[HBM_SPEC] * nb + [ANY] * (after is not None),
        out_shape=(pltpu.SemaphoreType.DMA((n_copies,)), pltpu.SemaphoreType.DMA((n_copies,)),
                   *[pltpu.HBM(b.shape, b.dtype) for b in bufs], jax.ShapeDtypeStruct((8, LANE), F32)),
        out_specs=(SEM_SPEC, SEM_SPEC, *[HBM_SPEC] * nb, VMEM_SPEC),
        input_output_aliases={i: 2 + i for i in range(nb)},
        compiler_params=pltpu.CompilerParams(has_side_effects=DATAFLOW),
    )(*[pltpu.with_memory_space_constraint(b, pltpu.HBM) for b in bufs], *([after] if after is not None else []))
    return out[0], out[1], list(out[2:2 + nb]), out[-1]


def _wait_copies(started, after, n_copies, plan, name):
    send_sems, recv_sems, bufs, _ = started
    nb = len(bufs)

    def body(*refs):
        send_ref, recv_ref = refs[nb], refs[nb + 1]
        copies = [pltpu.make_async_remote_copy(src_ref=src, dst_ref=dst, send_sem=send_ref.at[i], recv_sem=recv_ref.at[i],
                                               device_id=dev, device_id_type=MESH_ID)
                  for i, (src, dst, dev) in enumerate(plan(refs[:nb]))]
        for cp in copies:
            cp.wait_send()
        for cp in copies:
            cp.wait_recv()

    out = pl.pallas_call(
        body, name=name, in_specs=[HBM_SPEC] * nb + [SEM_SPEC, SEM_SPEC, ANY],
        out_shape=tuple(pltpu.HBM(b.shape, b.dtype) for b in bufs), out_specs=tuple([HBM_SPEC] * nb),
        input_output_aliases={i: i for i in range(nb)},
        compiler_params=pltpu.CompilerParams(has_side_effects=DATAFLOW),
    )(*bufs, send_sems, recv_sems, after)
    return list(out)


def _after(x, token):
    return x + token[0, 0].astype(x.dtype)


def _other_chips():
    px, py, _ = _place()
    return [(1 - px, py), (px, 1 - py), (1 - px, 1 - py)]


def _exchange(srcs, slots, src_block, target, name):
    n_p = len(srcs)

    def body(*refs):
        src_refs, out_refs, send_sems, recv_sems = refs[:n_p], refs[n_p:2 * n_p], refs[-2], refs[-1]
        copies = [pltpu.make_async_remote_copy(
            src_ref=src_refs[p].at[src_block(s)], dst_ref=out_refs[p].at[s], send_sem=send_sems.at[s, p],
            recv_sem=recv_sems.at[s, p], device_id=target(s), device_id_type=MESH_ID)
            for s in range(slots) for p in range(n_p)]
        for cp in copies:
            cp.start()
        for cp in copies:
            cp.wait_recv()
        for cp in copies:
            cp.wait_send()

    return pl.pallas_call(
        body, name=name, in_specs=[ANY] * n_p, out_specs=[ANY] * n_p,
        out_shape=[jax.ShapeDtypeStruct((slots,) + a.shape[1:], a.dtype) for a in srcs],
        scratch_shapes=[pltpu.SemaphoreType.DMA((slots, n_p)), pltpu.SemaphoreType.DMA((slots, n_p))],
        compiler_params=pltpu.CompilerParams(vmem_limit_bytes=VMEM_LIMIT),
    )(*srcs)


def _blocks_to_sibling(sends, name):
    def src_block(j):
        return 2 * j + 1 - lax.axis_index("c")

    def target(j):
        px, py, pc = _place()
        return (px, py, 1 - pc)

    return _exchange(sends, 4, src_block, target, name)


def _pair_sums_for_chips(own, got, name):
    _, r, c = own.shape
    tr = _row_tile(r, 256, 16)

    def body(idx_ref, own_ref, got_ref, o_ref):
        o_ref[...] = (own_ref[...] + got_ref[...].astype(F32)).astype(o_ref.dtype)

    grid_spec = pltpu.PrefetchScalarGridSpec(
        num_scalar_prefetch=1, grid=(3, r // tr),
        in_specs=[pl.BlockSpec((None, tr, c), lambda k, i, idx: (idx[k], i, 0)),
                  pl.BlockSpec((None, tr, c), lambda k, i, idx: (idx[3 + k], i, 0))],
        out_specs=pl.BlockSpec((None, tr, c), lambda k, i, idx: (k, i, 0)))
    chips = [2 * cx + cy for cx, cy in _other_chips()]
    idx = jnp.stack([2 * j + lax.axis_index("c") for j in chips] + chips).astype(jnp.int32)
    return pl.pallas_call(
        body, name=name, grid_spec=grid_spec, out_shape=jax.ShapeDtypeStruct((3, r, c), BF16),
        compiler_params=_params("parallel", "parallel"),
    )(idx, own, got)


def _sum_for_me(own, got_sibling, got_chips, name):
    _, r, c = own.shape
    tr = _row_tile(r, 256, 16)

    def body(idx_ref, own_ref, sib_ref, g0_ref, g1_ref, g2_ref, o_ref):
        acc = own_ref[...] + sib_ref[...].astype(F32)
        for ref in (g0_ref, g1_ref, g2_ref):
            acc = acc + ref[...].astype(F32)
        o_ref[...] = acc

    def part(k):
        return pl.BlockSpec((None, tr, c), lambda i, idx: (k, i, 0))

    grid_spec = pltpu.PrefetchScalarGridSpec(
        num_scalar_prefetch=1, grid=(r // tr,),
        in_specs=[pl.BlockSpec((None, tr, c), lambda i, idx: (idx[0], i, 0)),
                  pl.BlockSpec((None, tr, c), lambda i, idx: (idx[1], i, 0)), part(0), part(1), part(2)],
        out_specs=pl.BlockSpec((tr, c), lambda i, idx: (i, 0)))
    px, py, pc = _place()
    idx = jnp.stack([4 * px + 2 * py + pc, 2 * px + py]).astype(jnp.int32)
    return pl.pallas_call(
        body, name=name, grid_spec=grid_spec, out_shape=jax.ShapeDtypeStruct((r, c), F32),
        compiler_params=_params("parallel"),
    )(idx, own, got_sibling, got_chips, got_chips, got_chips)


def _chip_plan(n_p):
    def plan(refs):
        pc = lax.axis_index("c")
        return [(refs[p].at[k], refs[n_p + p].at[k], (cx, cy, pc))
                for p in range(n_p) for k, (cx, cy) in enumerate(_other_chips())]
    return plan


def _reduce_start(own, sends, tag):
    from_sibling = _blocks_to_sibling(sends, "grads_to_sibling_" + tag)
    pair_sums = [_pair_sums_for_chips(a, b, "grads_pair_sums") for a, b in zip(own, from_sibling)]
    lands = [lax.empty(a.shape, a.dtype) for a in pair_sums]
    started = _start_copies(pair_sums + lands, 3 * len(own), _chip_plan(len(own)), "grads_to_chips_start_" + tag)
    return from_sibling, started


def _reduce_finish(own, from_sibling, started, after, tag):
    n_p = len(own)
    bufs = _wait_copies(started, after, 3 * n_p, _chip_plan(n_p), "grads_to_chips_wait_" + tag)
    return [_sum_for_me(a, b, c, "grads_sum") for a, b, c in zip(own, from_sibling, bufs[n_p:])]


def _gather_plans(pieces):
    n_p = len(pieces)
    dims = [(x.shape[axis], axis) for x, axis in pieces]

    def first(refs):
        px, py, pc = _place()
        targets = [(px, py, 1 - pc)] + [(cx, cy, pc) for cx, cy in _other_chips()]
        return [(refs[p], _window(refs[n_p + p], 4 * px + 2 * py + pc, *dims[p]), to)
                for p in range(n_p) for to in targets]

    def second(refs):
        px, py, pc = _place()
        out = []
        for p in range(n_p):
            for cx, cy in _other_chips():
                win = _window(refs[p], 4 * cx + 2 * cy + pc, *dims[p])
                out.append((win, win, (px, py, 1 - pc)))
        return out

    return first, second


def _flat_rows(a, width):
    return a.reshape(-1, width)


def _full_from_blocks(blocks, name):
    if name in COL_SHARDED:
        _, l, k, nb = blocks.shape
        return jnp.transpose(blocks, (1, 2, 0, 3)).reshape(l, k, N_DEV * nb)
    _, l, rb, n = blocks.shape
    return jnp.transpose(blocks, (1, 0, 2, 3)).reshape(l, N_DEV * rb, n)


def _blocks_from_full(full, name):
    if name in COL_SHARDED:
        l, k, n = full.shape
        return jnp.transpose(full.reshape(l, k, N_DEV, n // N_DEV), (2, 0, 1, 3))
    l, rows, n = full.shape
    return jnp.transpose(full.reshape(l, N_DEV, rows // N_DEV, n), (1, 0, 2, 3))


def _pad_heads(w, width):
    k = w.shape[0]
    return jnp.pad(w.reshape(k, N_HEADS, width), ((0, 0), (0, 0), (0, HEAD_PAD - width))).reshape(k, N_HEADS * HEAD_PAD)


def _unpad_heads(w, width):
    k = w.shape[0]
    return w.reshape(k, N_HEADS, HEAD_PAD)[:, :, :width].reshape(k, N_HEADS * width)


def _layer_operands(full, vec, conv_w_full, l):
    w_in = full['w_in'][l]
    kpe = jnp.pad(w_in[:, LAT:LAT + QK_ROPE], ((0, 0), (QK_NOPE, HEAD_PAD - QK_DIM)))
    w_ukv = full['w_ukv'][l].reshape(KV_LORA, N_HEADS, QK_NOPE + V_DIM)
    w_out = full['w_out'][l]
    d_model = w_out.shape[1]
    wo_attn = jnp.pad(w_out[:N_HEADS * V_DIM].reshape(N_HEADS, V_DIM, d_model),
                      ((0, 0), (0, HEAD_PAD - V_DIM), (0, 0))).reshape(N_HEADS * HEAD_PAD, d_model)
    ops = {
        'w_in': jnp.concatenate([w_in[:, :LAT], kpe, w_in[:, LAT + QK_ROPE:]], axis=1),
        'w_q': _pad_heads(full['w_uq'][l], QK_DIM),
        'w_k': _pad_heads(w_ukv[:, :, :QK_NOPE].reshape(KV_LORA, N_HEADS * QK_NOPE), QK_NOPE),
        'w_v': _pad_heads(w_ukv[:, :, QK_NOPE:].reshape(KV_LORA, N_HEADS * V_DIM), V_DIM),
        'wo_attn': wo_attn,
        'wo_conv': w_out[N_HEADS * V_DIM:],
        'conv_w': jnp.pad(conv_w_full[l], ((0, CONV_HALO - CONV_K), (0, 0))),
        'gq': jnp.pad(vec['q_norm'][l], (0, HEAD_PAD - QK_DIM)),
        'gk': jnp.pad(vec['k_norm'][l], (0, HEAD_PAD - QK_DIM)),
    }
    for n in ('ffn1_norm', 'mix_norm', 'q_latent_norm', 'kv_latent_norm', 'conv_b', 'conv_ln_g', 'conv_ln_b',
              'ffn2_norm', 'post_norm'):
        ops[n] = vec[n][l]
    return ops


def _ffn_fwd(x, g, wgu, wd, fp):
    h = _rms_fwd(x, g, BF16, "rms_fwd_ffn")
    ab, z = _ffn_up(h, wgu, fp, "ffn_up")
    y = _mm(z, wd, res=x, scale=0.5, name="ffn_down")
    return y, (x, h, ab, z)


def _ffn_bwd(dy, dyb, saved, g, wgu, wd, fp, after_dw=None, after=None):
    x, h, ab, z = saved
    d_wd = _mm(z, dyb, ta=True, scale=0.5, blocks=('row', N_DEV), after=after, name="ffn_dwd")
    dab = _ffn_dab(dyb, wd, ab, fp, "ffn_dab")
    d_wgu = _mm(h, dab, ta=True, blocks=('col', N_DEV), name="ffn_dwgu")
    token = after_dw(d_wgu, d_wd) if after_dw is not None else None
    dh = _mm(dab, wgu, tb=True, after=token, name="ffn_dh")
    dx, dxb, dg = _rms_bwd(x, g, dh, dy, "rms_bwd_ffn")
    return dx, dxb, dg, d_wgu, d_wd


def _mixer_fwd(x, ops, tabs, after_attention=None):
    h = _rms_fwd(x, ops['mix_norm'], BF16, "rms_fwd_mix")
    p = _mm(h, ops['w_in'], name="mix_in")
    qln, kvln = _lat_norm_fwd(p, ops['q_latent_norm'], ops['kv_latent_norm'], "lat_norm_fwd")
    q_raw = _mm(qln, ops['w_q'], name="mix_q")
    k_raw = _mm(kvln, ops['w_k'], name="mix_k")
    v = _mm(kvln, ops['w_v'], out_dtype=BF16, name="mix_v")
    q, k = _qk_prep_fwd(q_raw, k_raw, p, ops['gq'], ops['gk'], tabs, "qk_prep_fwd")
    o, lse = _attn_fwd(q, k, v, "attn_fwd")
    token = after_attention(o) if after_attention is not None else None
    conv_b = ops['conv_b'] if token is None else _after(ops['conv_b'], token)
    y_conv, cv = _conv_fwd(p, ops['conv_w'], conv_b, ops['conv_ln_g'], ops['conv_ln_b'], "conv_fwd")
    x_attn = _mm(o, ops['wo_attn'], res=x, name="mix_out_attn")
    x_out = _mm(cv, ops['wo_conv'], res=x_attn, name="mix_out_conv")
    return x_out, (x, h, p, qln, kvln, q_raw, k_raw, v, q, k, o, lse, y_conv, cv)


def _mixer_bwd(dx_out, dxb_out, saved, ops, tabs, token=None):
    x, h, p, qln, kvln, q_raw, k_raw, v, q, k, o, lse, y_conv, cv = saved
    g = {}
    do = _mm(dxb_out, ops['wo_attn'], tb=True, after=token, name="mix_do")
    dcv = _mm(dxb_out, ops['wo_conv'], tb=True, name="mix_dcv")
    g['wo_attn'] = _mm(o, dxb_out, ta=True, name="mix_dwo_attn")
    g['wo_conv'] = _mm(cv, dxb_out, ta=True, name="mix_dwo_conv")
    dq, dk, dv = _attn_bwd(q, k, v, o, lse, do, "attn_bwd")
    dq_raw, dk_raw, dkpe, g['gq'], g['gk'] = _qk_prep_bwd(q_raw, k_raw, p, dq, dk, ops['gq'], ops['gk'], tabs,
                                                          "qk_prep_bwd")
    g['w_q'] = _mm(qln, dq_raw, ta=True, name="mix_dwq")
    g['w_k'] = _mm(kvln, dk_raw, ta=True, name="mix_dwk")
    g['w_v'] = _mm(kvln, dv, ta=True, name="mix_dwv")
    dqln = _mm(dq_raw, ops['w_q'], tb=True, name="mix_dqln")
    dkvln = _mm(dk_raw, ops['w_k'], tb=True, name="mix_dkvln_k")
    dkvln = _mm(dv, ops['w_v'], tb=True, res=dkvln, name="mix_dkvln_v")
    dp_lat, g['q_latent_norm'], g['kv_latent_norm'] = _lat_norm_bwd(
        p, ops['q_latent_norm'], ops['kv_latent_norm'], dqln, dkvln, "lat_norm_bwd")
    dy_conv, g['conv_ln_g'], g['conv_ln_b'], g['conv_b'] = _conv_bwd_ln(
        y_conv, dcv, ops['conv_ln_g'], ops['conv_ln_b'], "conv_bwd_ln")
    dag, g['conv_w'] = _conv_bwd_taps(p, dy_conv, ops['conv_w'], "conv_bwd_taps")
    dp = jnp.concatenate([dp_lat, dkpe, dag], axis=1)
    g['w_in'] = _mm(h, dp, ta=True, name="mix_dw_in")
    dh = _mm(dp, ops['w_in'], tb=True, name="mix_dh")
    dx, dxb, g['mix_norm'] = _rms_bwd(x, ops['mix_norm'], dh, dx_out, "rms_bwd_mix")
    return dx, dxb, g


def _mixer_grads_to_params(g):
    d_w_in = g['w_in']
    d_wk = _unpad_heads(g['w_k'], QK_NOPE).reshape(KV_LORA, N_HEADS, QK_NOPE)
    d_wv = _unpad_heads(g['w_v'], V_DIM).reshape(KV_LORA, N_HEADS, V_DIM)
    d_model = g['wo_attn'].shape[1]
    d_wo_attn = g['wo_attn'].reshape(N_HEADS, HEAD_PAD, d_model)[:, :V_DIM].reshape(N_HEADS * V_DIM, d_model)
    return {
        'mix_norm': g['mix_norm'],
        'w_in': jnp.concatenate([d_w_in[:, :LAT], d_w_in[:, LAT + QK_NOPE:LAT + QK_DIM], d_w_in[:, P_A:]], axis=1),
        'q_latent_norm': g['q_latent_norm'], 'w_uq': _unpad_heads(g['w_q'], QK_DIM),
        'kv_latent_norm': g['kv_latent_norm'],
        'w_ukv': jnp.concatenate([d_wk, d_wv], axis=2).reshape(KV_LORA, N_HEADS * (QK_NOPE + V_DIM)),
        'q_norm': g['gq'][:QK_DIM], 'k_norm': g['gk'][:QK_DIM], 'conv_w': g['conv_w'][:CONV_K],
        'conv_b': g['conv_b'], 'conv_ln_g': g['conv_ln_g'], 'conv_ln_b': g['conv_ln_b'],
        'w_out': jnp.concatenate([d_wo_attn, g['wo_conv']], axis=0),
    }


def kernel(x, ffn1_norm, ffn1_w_gate, ffn1_w_up, ffn1_w_down, mix_norm, w_in, q_latent_norm, w_uq, kv_latent_norm, w_ukv, q_norm, k_norm, conv_w, conv_b, conv_ln_g, conv_ln_b, w_out, ffn2_norm, ffn2_w_gate, ffn2_w_up, ffn2_w_down, post_norm, loss_target, m_ffn1_norm, m_ffn1_w_gate, m_ffn1_w_up, m_ffn1_w_down, m_mix_norm, m_w_in, m_q_latent_norm, m_w_uq, m_kv_latent_norm, m_w_ukv, m_q_norm, m_k_norm, m_conv_w, m_conv_b, m_conv_ln_g, m_conv_ln_b, m_w_out, m_ffn2_norm, m_ffn2_w_gate, m_ffn2_w_up, m_ffn2_w_down, m_post_norm, v_ffn1_norm, v_ffn1_w_gate, v_ffn1_w_up, v_ffn1_w_down, v_mix_norm, v_w_in, v_q_latent_norm, v_w_uq, v_kv_latent_norm, v_w_ukv, v_q_norm, v_k_norm, v_conv_w, v_conv_b, v_conv_ln_g, v_conv_ln_b, v_w_out, v_ffn2_norm, v_ffn2_w_gate, v_ffn2_w_up, v_ffn2_w_down, v_post_norm):
    args = locals()
    w = {n: args[n] for n in WEIGHTS}
    mom = {n: args["m_" + n] for n in WEIGHTS}
    var = {n: args["v_" + n] for n in WEIGHTS}
    depth = ffn1_norm.shape[0]
    x0 = x.reshape(x.shape[-2:])
    target = loss_target.reshape(loss_target.shape[-2:])
    t, d_model = x0.shape
    my_block = 4 * lax.axis_index("x") + 2 * lax.axis_index("y") + lax.axis_index("c")

    fb = ffn1_w_gate.shape[-1]
    fp = -(-fb // LANE) * LANE
    ffns = [(l, f) for l in range(depth) for f in (1, 2)]
    pad_cols = lambda a: jnp.pad(a, ((0, 0), (0, fp - fb)))
    gu_local = {(l, f): jnp.concatenate([pad_cols(w[f'ffn{f}_w_gate'][l]), pad_cols(w[f'ffn{f}_w_up'][l])],
                                        axis=1).astype(BF16) for l, f in ffns}
    dn_local = {(l, f): jnp.pad(w[f'ffn{f}_w_down'][l], ((0, fp - fb), (0, 0))).astype(BF16) for l, f in ffns}
    rows_of = {n: w[n].size // d_model for n in REST}
    rest_local = jnp.concatenate([_flat_rows(w[n].astype(BF16), d_model) for n in REST], axis=0)
    n_rest = rest_local.shape[0]
    first_ffn, later = ffns[0], ffns[1:]
    cw = conv_w.reshape(-1)
    cw_rows = -(-cw.size // (8 * LANE)) * 8
    cw_flat = jnp.pad(cw, (0, cw_rows * LANE - cw.size)).reshape(cw_rows, LANE)
    got = _all_gather([(gu_local[first_ffn], 1), (dn_local[first_ffn], 0), (rest_local, 0), (cw_flat, 0)],
                      "gather_first")
    wgu, wd = {first_ffn: got[0]}, {first_ffn: got[1]}
    gathered = got[2].reshape(N_DEV, n_rest, d_model)
    cw_all = got[3].reshape(N_DEV, cw_rows * LANE)[:, :cw.size]
    later_pieces = [(gu_local[q], 1) for q in later] + [(dn_local[q], 0) for q in later]
    n_later = len(later_pieces)
    gather_plan, forward_plan = _gather_plans(later_pieces)

    def landing(a, axis):
        shape = a.shape[:axis] + (N_DEV * a.shape[axis],) + a.shape[axis + 1:]
        return lax.dynamic_update_slice_in_dim(lax.empty(shape, a.dtype), a, my_block * a.shape[axis], axis)

    gather_later = _start_copies([a for a, _ in later_pieces] + [landing(a, ax) for a, ax in later_pieces],
                                 4 * n_later, gather_plan, "gather_later_start", after=got[2])
    full, start = {}, 0
    for n in REST:
        blocks = gathered[:, start:start + rows_of[n]].reshape((N_DEV,) + w[n].shape)
        full[n] = _full_from_blocks(blocks, n)
        start += rows_of[n]
    conv_w_full =jnp.transpose(cw_all.reshape((N_DEV,) + conv_w.shape), (1, 2, 0, 3)).reshape(depth, CONV_K, CONV_W)
    vec = {n: w[n] for n in VECTORS}
    ops = [_layer_operands(full, vec, conv_w_full, l) for l in range(depth)]
    tabs = _rope_tables(t)

    saved, xl = [], x0
    forward_later = []

    def pass_on_later(o_attn):
        lands = _wait_copies(gather_later, o_attn, 4 * n_later, gather_plan, "gather_later_wait")[n_later:]
        forward_later.append(_start_copies(lands, 3 * n_later, forward_plan, "gather_later_forward_start"))
        return forward_later[0][3]

    for l in range(depth):
        o = ops[l]
        if l == 0:
            x1, s1 = _ffn_fwd(xl, _after(o['ffn1_norm'], gather_later[3]), wgu[l, 1], wd[l, 1], fp)
            x2, sm = _mixer_fwd(x1, o, tabs, after_attention=pass_on_later)
            lands = _wait_copies(forward_later[0], x2, 3 * n_later, forward_plan, "gather_later_forward_wait")
            wgu.update(zip(later, lands[:len(later)]))
            wd.update(zip(later, lands[len(later):]))
        else:
            x1, s1 = _ffn_fwd(xl, o['ffn1_norm'], wgu[l, 1], wd[l, 1], fp)
            x2, sm = _mixer_fwd(x1, o, tabs)
        x3, s2 = _ffn_fwd(x2, o['ffn2_norm'], wgu[l, 2], wd[l, 2], fp)
        xl = _rms_fwd(x3, o['post_norm'], F32, "rms_fwd_post")
        saved.append((s1, sm, s2, x3))
    loss_part, dx = _loss_head(xl, target, "loss_head")
    loss = lax.psum(loss_part[0, 0], ("x", "y", "c"))

    grads, mine_gu, mine_dn, in_flight = [None] * depth, {}, {}, {}

    def exchange(tag):
        def after_dw(d_wgu, d_wd):
            own = [d_wgu[0], d_wd[0]]
            from_sibling, started = _reduce_start(own, [d_wgu[1], d_wd[1]], tag)
            in_flight[tag] = (own, from_sibling, started)
            return started[3]
        return after_dw

    def finish(tag, after):
        own, from_sibling, started = in_flight.pop(tag)
        return _reduce_finish(own, from_sibling, started, after, tag)

    for l in reversed(range(depth)):
        o = ops[l]
        s1, sm, s2, x3 = saved[l]
        dx, dxb, d_post = _rms_bwd(x3, o['post_norm'], dx, None, "rms_bwd_post")
        dx, dxb, d_ffn2, _, _ = _ffn_bwd(dx, dxb, s2, o['ffn2_norm'], wgu[l, 2], wd[l, 2], fp, exchange(f"{l}2"))
        if l + 1 < depth:
            mine_gu[l + 1, 1], mine_dn[l + 1, 1] = finish(f"{l + 1}1", dx)
        dx, dxb, gm = _mixer_bwd(dx, dxb, sm, o, tabs)
        mine_gu[l, 2], mine_dn[l, 2] = finish(f"{l}2", dx)
        grads[l] = _mixer_grads_to_params(gm)
        if l == 0:
            rest_own = jnp.concatenate(
                [_blocks_from_full(jnp.stack([grads[k][n] for k in range(depth)]), n).reshape(N_DEV, rows_of[n], d_model)
                 for n in REST], axis=1)
            sibling_rest, started_rest = _reduce_start([rest_own], [rest_own.astype(BF16)], "rest")
        dx, dxb, d_ffn1, _, _ = _ffn_bwd(dx, dxb, s1, o['ffn1_norm'], wgu[l, 1], wd[l, 1], fp, exchange(f"{l}1"),
                                         after=started_rest[3] if l == 0 else None)
        grads[l].update(post_norm=d_post, ffn2_norm=d_ffn2, ffn1_norm=d_ffn1)
    grad_x = dx.reshape(x.shape)
    part = {n: jnp.stack([grads[l][n] for l in range(depth)]) for n in grads[0]}

    small = jnp.concatenate([part[n].reshape(-1) for n in VECTORS] + [part['conv_w'].reshape(-1)])
    s_rows = -(-small.size // (8 * LANE)) * 8
    small = jnp.pad(small, (0, s_rows * LANE - small.size)).reshape(s_rows, LANE)
    small_all = _all_gather([(small, 0)], "gather_small_grads", in_vmem=True)[0]
    small_sum = _sum_parts([small_all[k * s_rows:(k + 1) * s_rows] for k in range(N_DEV)], "sum_small_grads")

    mine_gu[first_ffn], mine_dn[first_ffn] = finish(f"{first_ffn[0]}{first_ffn[1]}", small_sum)
    mine_rest = _reduce_finish([rest_own], sibling_rest, started_rest, small_sum, "rest")[0]
    grad = {}
    for f in (1, 2):
        grad[f'ffn{f}_w_gate'] = jnp.stack([mine_gu[l, f][:, :fb] for l in range(depth)])
        grad[f'ffn{f}_w_up'] = jnp.stack([mine_gu[l, f][:, fp:fp + fb] for l in range(depth)])
        grad[f'ffn{f}_w_down'] = jnp.stack([mine_dn[l, f][:fb] for l in range(depth)])
    start = 0
    for n in REST:
        grad[n] = mine_rest[start:start + rows_of[n]].reshape(w[n].shape)
        start += rows_of[n]
    small_sum = small_sum.reshape(-1)
    start = 0
    for n in VECTORS:
        grad[n] = small_sum[start:start + w[n].size].reshape(w[n].shape)
        start += w[n].size
    cw_grad = small_sum[start:start + depth * CONV_K * CONV_W].reshape(depth, CONV_K, CONV_W)
    nb = conv_w.shape[-1]
    grad['conv_w'] = lax.dynamic_slice_in_dim(cw_grad, my_block * nb, nb, axis=2)

    delta, new_m, new_v = {}, {}, {}
    for n in BIG + ['conv_w']:
        shp = w[n].shape
        two_d = lambda a: a.reshape(-1, shp[-1])
        dl, mn, vn = _adamw(two_d(w[n]), two_d(grad[n]), two_d(mom[n]), two_d(var[n]), "adamw_" + n)
        delta[n], new_m[n], new_v[n] = dl.reshape(shp), mn.reshape(shp), vn.reshape(shp)
    vcat = lambda src: jnp.concatenate([src[n].reshape(-1) for n in VECTORS]).reshape(-1, LANE)
    dl, mn, vn = _adamw(vcat(w), vcat(grad), vcat(mom), vcat(var), "adamw_vectors")
    start = 0
    for n in VECTORS:
        sl = lambda a: a.reshape(-1)[start:start + w[n].size].reshape(w[n].shape)
        delta[n], new_m[n], new_v[n] = sl(dl), sl(mn), sl(vn)
        start += w[n].size

    return (loss, grad_x, *[grad[n] for n in WEIGHTS], *[delta[n] for n in WEIGHTS],
            *[new_m[n] for n in WEIGHTS], *[new_v[n] for n in WEIGHTS])
```

```python
import functools

import jax
import jax.numpy as jnp
from jax import lax
from jax.experimental import pallas as pl
from jax.experimental.pallas import tpu as pltpu

F32, BF16 = jnp.float32, jnp.bfloat16

N_DEV = 8
N_HEADS = 8
QK_NOPE, QK_ROPE, V_DIM = 64, 32, 64
QK_DIM = QK_NOPE + QK_ROPE
HEAD_PAD = 128
Q_LORA, KV_LORA = 384, 256
LAT = Q_LORA + KV_LORA
CONV_W, CONV_K = 512, 31
CONV_HALO = 32
CHUNK = 64
ROPE_THETA = 10000.0
EPS = 1e-6
ATTN_SCALE = QK_DIM ** -0.5
ATTN_SCALE_LOG2 = ATTN_SCALE * 1.4426950408889634
P_KPE = LAT
P_A = LAT + HEAD_PAD
P_G = P_A + CONV_W
P_COLS = P_G + CONV_W

ADAM_LR, ADAM_B1, ADAM_B2, ADAM_EPS, ADAM_WD, ADAM_STEP = 0.001, 0.9, 0.999, 1e-08, 0.01, 10

V7X_VMEM_BYTES = 64 << 20
VMEM_LIMIT = V7X_VMEM_BYTES - (8 << 20)
MM_VMEM_BUDGET = 36 << 20
LANE = 128
ROW_TILE = 512
ATTN_BLOCK = 512
CONV_TILE = 256

MESH_ID = pl.DeviceIdType.MESH
ANY = pl.BlockSpec(memory_space=pl.ANY)
VMEM_SPEC = pl.BlockSpec(memory_space=pltpu.VMEM)
HBM_SPEC = pl.BlockSpec(memory_space=pltpu.HBM)
SEM_SPEC = pl.BlockSpec(memory_space=pltpu.SEMAPHORE)
DATAFLOW = pltpu.SideEffectType.DATAFLOW_SIDE_EFFECTING

WEIGHTS = ['ffn1_norm', 'ffn1_w_gate', 'ffn1_w_up', 'ffn1_w_down', 'mix_norm', 'w_in', 'q_latent_norm', 'w_uq',
           'kv_latent_norm', 'w_ukv', 'q_norm', 'k_norm', 'conv_w', 'conv_b', 'conv_ln_g', 'conv_ln_b', 'w_out',
           'ffn2_norm', 'ffn2_w_gate', 'ffn2_w_up', 'ffn2_w_down', 'post_norm']
COL_SHARDED = ['ffn1_w_gate', 'ffn1_w_up', 'w_in', 'w_uq', 'w_ukv', 'ffn2_w_gate', 'ffn2_w_up']
ROW_SHARDED = ['ffn1_w_down', 'w_out', 'ffn2_w_down']
REST = ['w_in', 'w_uq', 'w_ukv', 'w_out']
BIG = ['ffn1_w_gate', 'ffn1_w_up', 'ffn1_w_down', 'w_in', 'w_uq', 'w_ukv', 'w_out', 'ffn2_w_gate', 'ffn2_w_up',
       'ffn2_w_down']
VECTORS = ['ffn1_norm', 'mix_norm', 'q_latent_norm', 'kv_latent_norm', 'q_norm', 'k_norm', 'conv_b', 'conv_ln_g',
           'conv_ln_b', 'ffn2_norm', 'post_norm']


def _params(*sem):
    return pltpu.CompilerParams(dimension_semantics=sem if sem else None, vmem_limit_bytes=VMEM_LIMIT)


def _tile(n, cap):
    if n <= cap:
        return n
    best = 0
    for d in range(LANE, cap + 1, LANE):
        if n % d == 0:
            best = d
    assert best, (n, cap)
    return best


def _row_tile(n, cap=ROW_TILE, mult=8):
    if n <= cap:
        return n
    best = 0
    for d in range(mult, cap + 1, mult):
        if n % d == 0:
            best = d
    assert best, (n, cap)
    return best


def _mm(a, b, *, name, ta=False, tb=False, res=None, scale=1.0, out_dtype=F32, tm=None, tn=None, blocks=None,
        after=None):
    (kdim, m) = a.shape if ta else a.shape[::-1]
    (n, kb) = b.shape if tb else b.shape[::-1]
    assert kdim == kb, (a.shape, b.shape, ta, tb)
    tm, tn = tm or _tile(m, 512), tn or _tile(n, 1024)
    if blocks is not None:
        tm, tn = (tm, n // blocks[1]) if blocks[0] == 'col' else (m // blocks[1], tn)
    size = lambda arr: jnp.dtype(arr.dtype).itemsize
    out_bytes = tm * tn * ((6 if blocks is not None else jnp.dtype(out_dtype).itemsize) + (4 if res is not None else 0))

    def vmem_need(tk):
        return 2 * (tm * tk * size(a) + tk * tn * size(b) + out_bytes) + (tm * tn * 4 if tk < kdim else 0)

    tk = kdim
    for cand in [d for d in range(kdim - LANE, 0, -LANE) if kdim % d == 0]:
        if vmem_need(tk) <= MM_VMEM_BUDGET:
            break
        tk = cand
    nk = kdim // tk
    n_in = 2 + (res is not None) + (after is not None)
    n_out = 2 if blocks is not None else 1
    dims = (((0 if ta else 1,), (1 if tb else 0,)), ((), ()))

    def body(*refs):
        a_ref, b_ref = refs[0], refs[1]
        r_ref = refs[2] if res is not None else None
        o_refs = refs[n_in:n_in + n_out]
        acc_ref = refs[-1] if nk > 1 else None
        part = lax.dot_general(a_ref[...].astype(BF16), b_ref[...].astype(BF16), dims, preferred_element_type=F32)

        def finish(acc):
            if scale != 1.0:
                acc = acc * scale
            if r_ref is not None:
                acc = r_ref[...] + acc
            for o_ref in o_refs:
                o_ref[...] = acc.astype(o_ref.dtype)

        if nk == 1:
            finish(part)
        else:
            k = pl.program_id(2)

            @pl.when(k == 0)
            def _():
                acc_ref[...] = part

            @pl.when(k > 0)
            def _():
                acc_ref[...] += part

            @pl.when(k == nk - 1)
            def _():
                finish(acc_ref[...])

    a_spec = pl.BlockSpec((tk, tm), lambda i, j, k: (k, i)) if ta else pl.BlockSpec((tm, tk), lambda i, j, k: (i, k))
    b_spec = pl.BlockSpec((tn, tk), lambda i, j, k: (j, k)) if tb else pl.BlockSpec((tk, tn), lambda i, j, k: (k, j))
    plain = pl.BlockSpec((tm, tn), lambda i, j, k: (i, j))
    if blocks is None:
        out_specs, out_shape = plain, jax.ShapeDtypeStruct((m, n), out_dtype)
    else:
        if blocks[0] == 'col':
            o_spec, shp = pl.BlockSpec((None, tm, tn), lambda i, j, k: (j, i, 0)), (blocks[1], m, tn)
        else:
            o_spec, shp = pl.BlockSpec((None, tm, tn), lambda i, j, k: (i, 0, j)), (blocks[1], tm, n)
        out_specs, out_shape = (o_spec, o_spec), (jax.ShapeDtypeStruct(shp, F32), jax.ShapeDtypeStruct(shp, BF16))
    in_specs = [a_spec, b_spec] + ([plain] if res is not None else [])
    args = (a, b) + ((res,) if res is not None else ())
    if after is not None:
        in_specs.append(pl.BlockSpec(after.shape, lambda i, j, k: (0, 0)))
        args += (after,)
    return pl.pallas_call(
        body, name=name, grid=(m // tm, n // tn, nk), in_specs=in_specs, out_specs=out_specs, out_shape=out_shape,
        scratch_shapes=[pltpu.VMEM((tm, tn), F32)] if nk > 1 else [],
        compiler_params=_params("parallel", "parallel", "arbitrary"),
    )(*args)


def _rms_fwd(x, g, out_dtype, name):
    t, d = x.shape
    tm = _row_tile(t)

    def body(x_ref, g_ref, o_ref):
        xv = x_ref[...]
        r = lax.rsqrt(jnp.mean(xv * xv, axis=-1, keepdims=True) + EPS)
        o_ref[...] = (xv * r * g_ref[...]).astype(o_ref.dtype)

    return pl.pallas_call(
        body, name=name, grid=(t // tm,),
        in_specs=[pl.BlockSpec((tm, d), lambda i: (i, 0)), pl.BlockSpec((1, d), lambda i: (0, 0))],
        out_specs=pl.BlockSpec((tm, d), lambda i: (i, 0)),
        out_shape=jax.ShapeDtypeStruct((t, d), out_dtype), compiler_params=_params("parallel"),
    )(x, g.reshape(1, d))


def _rms_bwd(x, g, dh, res, name):
    t, d = x.shape
    tm = _row_tile(t)

    def body(*refs):
        x_ref, g_ref, dh_ref = refs[:3]
        r_ref = refs[3] if res is not None else None
        dx_ref, dxb_ref, dg_ref = refs[-3:]
        xv, dhv = x_ref[...], dh_ref[...]
        r = lax.rsqrt(jnp.mean(xv * xv, axis=-1, keepdims=True) + EPS)
        y = xv * r
        dy = dhv * g_ref[...]
        dx = r * (dy - y * jnp.mean(dy * y, axis=-1, keepdims=True))
        if r_ref is not None:
            dx = r_ref[...] + dx
        dx_ref[...] = dx
        dxb_ref[...] = dx.astype(BF16)

        @pl.when(pl.program_id(0) == 0)
        def _():
            dg_ref[...] = jnp.zeros_like(dg_ref)

        dg_ref[...] += jnp.sum(dhv * y, axis=0, keepdims=True)

    row = pl.BlockSpec((tm, d), lambda i: (i, 0))
    vec = pl.BlockSpec((1, d), lambda i: (0, 0))
    args = (x, g.reshape(1, d), dh) + ((res,) if res is not None else ())
    dx, dxb, dg = pl.pallas_call(
        body, name=name, grid=(t // tm,), in_specs=[row, vec, row] + ([row] if res is not None else []),
        out_specs=(row, row, vec),
        out_shape=(jax.ShapeDtypeStruct((t, d), F32), jax.ShapeDtypeStruct((t, d), BF16),
                   jax.ShapeDtypeStruct((1, d), F32)),
        compiler_params=_params("arbitrary"),
    )(*args)
    return dx, dxb, dg.reshape(d)


FFN_PAIR = 2


def _ffn_up(h, wgu, fp, name):
    t, d = h.shape
    tm, tn = _tile(t, 512), FFN_PAIR * 2 * fp
    nj = wgu.shape[1] // tn

    def body(h_ref, w_ref, ab_ref, z_ref):
        ab = jnp.dot(h_ref[...], w_ref[...], preferred_element_type=F32)
        ab_ref[...] = ab.astype(ab_ref.dtype)
        for e in range(FFN_PAIR):
            av, bv = ab[:, 2 * fp * e:2 * fp * e + fp], ab[:, 2 * fp * e + fp:2 * fp * (e + 1)]
            z_ref[:, fp * e:fp * (e + 1)] = (av * jax.nn.sigmoid(av) * bv).astype(z_ref.dtype)

    return pl.pallas_call(
        body, name=name, grid=(nj, t // tm),
        in_specs=[pl.BlockSpec((tm, d), lambda j, i: (i, 0)), pl.BlockSpec((d, tn), lambda j, i: (0, j))],
        out_specs=(pl.BlockSpec((tm, tn), lambda j, i: (i, j)), pl.BlockSpec((tm, tn // 2), lambda j, i: (i, j))),
        out_shape=(jax.ShapeDtypeStruct((t, wgu.shape[1]), BF16), jax.ShapeDtypeStruct((t, wgu.shape[1] // 2), BF16)),
        compiler_params=_params("parallel", "parallel"),
    )(h, wgu)


def _ffn_dab(dyb, wd, ab, fp, name):
    t, d = dyb.shape
    tm, tn = _tile(t, 512), FFN_PAIR * 2 * fp
    nj = ab.shape[1] // tn

    def body(dy_ref, wd_ref, ab_ref, dab_ref):
        dz = _dot_nt(dy_ref[...], wd_ref[...]) * 0.5
        for e in range(FFN_PAIR):
            av = ab_ref[:, 2 * fp * e:2 * fp * e + fp].astype(F32)
            bv = ab_ref[:, 2 * fp * e + fp:2 * fp * (e + 1)].astype(F32)
            dze = dz[:, fp * e:fp * (e + 1)]
            s = jax.nn.sigmoid(av)
            dab_ref[:, 2 * fp * e:2 * fp * e + fp] = (dze * bv * (s * (1.0 + av * (1.0 - s)))).astype(dab_ref.dtype)
            dab_ref[:, 2 * fp * e + fp:2 * fp * (e + 1)] = (dze * (av * s)).astype(dab_ref.dtype)

    return pl.pallas_call(
        body, name=name, grid=(nj, t // tm),
        in_specs=[pl.BlockSpec((tm, d), lambda j, i: (i, 0)), pl.BlockSpec((tn // 2, d), lambda j, i: (j, 0)),
                  pl.BlockSpec((tm, tn), lambda j, i: (i, j))],
        out_specs=pl.BlockSpec((tm, tn), lambda j, i: (i, j)),
        out_shape=jax.ShapeDtypeStruct(ab.shape, BF16), compiler_params=_params("parallel", "parallel"),
    )(dyb, wd, ab)


def _lat_norm_fwd(p, g_q, g_kv, name):
    t = p.shape[0]
    tm = _row_tile(t)

    def body(p_ref, gq_ref, gkv_ref, q_ref, kv_ref):
        for lo, hi, g_ref, o_ref in ((0, Q_LORA, gq_ref, q_ref), (Q_LORA, LAT, gkv_ref, kv_ref)):
            xv = p_ref[:, lo:hi]
            r = lax.rsqrt(jnp.mean(xv * xv, axis=-1, keepdims=True) + EPS)
            o_ref[...] = (xv * r * g_ref[...]).astype(o_ref.dtype)

    return pl.pallas_call(
        body, name=name, grid=(t // tm,),
        in_specs=[pl.BlockSpec((tm, P_COLS), lambda i: (i, 0)), pl.BlockSpec((1, Q_LORA), lambda i: (0, 0)),
                  pl.BlockSpec((1, KV_LORA), lambda i: (0, 0))],
        out_specs=(pl.BlockSpec((tm, Q_LORA), lambda i: (i, 0)), pl.BlockSpec((tm, KV_LORA), lambda i: (i, 0))),
        out_shape=(jax.ShapeDtypeStruct((t, Q_LORA), BF16), jax.ShapeDtypeStruct((t, KV_LORA), BF16)),
        compiler_params=_params("parallel"),
    )(p, g_q.reshape(1, Q_LORA), g_kv.reshape(1, KV_LORA))


def _lat_norm_bwd(p, g_q, g_kv, dq, dkv, name):
    t = p.shape[0]
    tm = _row_tile(t)

    def body(p_ref, gq_ref, gkv_ref, dq_ref, dkv_ref, dp_ref, dgq_ref, dgkv_ref):
        first = pl.program_id(0) == 0
        for lo, hi, g_ref, d_ref, dg_ref in ((0, Q_LORA, gq_ref, dq_ref, dgq_ref),
                                             (Q_LORA, LAT, gkv_ref, dkv_ref, dgkv_ref)):
            xv, dhv = p_ref[:, lo:hi], d_ref[...]
            r = lax.rsqrt(jnp.mean(xv * xv, axis=-1, keepdims=True) + EPS)
            y = xv * r
            dy = dhv * g_ref[...]
            dp_ref[:, lo:hi] = r * (dy - y * jnp.mean(dy * y, axis=-1, keepdims=True))

            @pl.when(first)
            def _():
                dg_ref[...] = jnp.zeros_like(dg_ref)

            dg_ref[...] += jnp.sum(dhv * y, axis=0, keepdims=True)

    vq = pl.BlockSpec((1, Q_LORA), lambda i: (0, 0))
    vkv = pl.BlockSpec((1, KV_LORA), lambda i: (0, 0))
    dp, dgq, dgkv = pl.pallas_call(
        body, name=name, grid=(t // tm,),
        in_specs=[pl.BlockSpec((tm, P_COLS), lambda i: (i, 0)), vq, vkv,
                  pl.BlockSpec((tm, Q_LORA), lambda i: (i, 0)), pl.BlockSpec((tm, KV_LORA), lambda i: (i, 0))],
        out_specs=(pl.BlockSpec((tm, LAT), lambda i: (i, 0)), vq, vkv),
        out_shape=(jax.ShapeDtypeStruct((t, LAT), F32), jax.ShapeDtypeStruct((1, Q_LORA), F32),
                   jax.ShapeDtypeStruct((1, KV_LORA), F32)),
        compiler_params=_params("arbitrary"),
    )(p, g_q.reshape(1, Q_LORA), g_kv.reshape(1, KV_LORA), dq, dkv)
    return dp, dgq.reshape(Q_LORA), dgkv.reshape(KV_LORA)


def _rope_tables(t):
    half = QK_ROPE // 2
    pos = jnp.arange(t, dtype=F32)
    inv_freq = 1.0 / (ROPE_THETA ** (jnp.arange(0, QK_ROPE, 2, dtype=F32) / QK_ROPE))
    ang = pos[:, None] * inv_freq[None, :]
    cos, sin = jnp.cos(ang), jnp.sin(ang)
    z = lambda n: jnp.zeros((t, n), F32)
    c_tab = jnp.concatenate([jnp.ones((t, QK_NOPE), F32), cos, cos, z(HEAD_PAD - QK_DIM)], axis=1)
    sa_tab = jnp.concatenate([z(QK_NOPE), -sin, z(half), z(HEAD_PAD - QK_DIM)], axis=1)
    sb_tab = jnp.concatenate([z(QK_NOPE), z(half), sin, z(HEAD_PAD - QK_DIM)], axis=1)
    return c_tab, sa_tab, sb_tab


def _rope(x, c, sa, sb):
    half = QK_ROPE // 2
    return x * c + pltpu.roll(x, HEAD_PAD - half, 1) * sa + pltpu.roll(x, half, 1) * sb


def _rope_t(d, c, sa, sb):
    half = QK_ROPE // 2
    return d * c + pltpu.roll(d * sa, half, 1) + pltpu.roll(d * sb, HEAD_PAD - half, 1)


def _head_rms(x):
    r = lax.rsqrt(jnp.sum(x * x, axis=-1, keepdims=True) * (1.0 / QK_DIM) + EPS)
    return x * r, r


def _qk_prep_fwd(q_raw, k_raw, p, gq, gk, tabs, name):
    t, width = q_raw.shape
    tm = _row_tile(t)

    def body(q_ref, k_ref, p_ref, gq_ref, gk_ref, c_ref, sa_ref, sb_ref, qo_ref, ko_ref):
        c, sa, sb, kpe = c_ref[...], sa_ref[...], sb_ref[...], p_ref[...]
        for h in range(N_HEADS):
            cols = slice(h * HEAD_PAD, (h + 1) * HEAD_PAD)
            qn, _ = _head_rms(q_ref[:, cols])
            qo_ref[:, cols] = _rope(qn * gq_ref[...], c, sa, sb).astype(qo_ref.dtype)
            kn, _ = _head_rms(k_ref[:, cols] + kpe)
            ko_ref[:, cols] = _rope(kn * gk_ref[...], c, sa, sb).astype(ko_ref.dtype)

    rows = pl.BlockSpec((tm, width), lambda i: (i, 0))
    tab = pl.BlockSpec((tm, HEAD_PAD), lambda i: (i, 0))
    vec = pl.BlockSpec((1, HEAD_PAD), lambda i: (0, 0))
    kpe_spec = pl.BlockSpec((tm, HEAD_PAD), lambda i: (i, P_KPE // HEAD_PAD))
    return pl.pallas_call(
        body, name=name, grid=(t // tm,), in_specs=[rows, rows, kpe_spec, vec, vec, tab, tab, tab],
        out_specs=(rows, rows),
        out_shape=(jax.ShapeDtypeStruct(q_raw.shape, BF16), jax.ShapeDtypeStruct(k_raw.shape, BF16)),
        compiler_params=_params("parallel"),
    )(q_raw, k_raw, p, gq.reshape(1, HEAD_PAD), gk.reshape(1, HEAD_PAD), *tabs)


def _qk_prep_bwd(q_raw, k_raw, p, dq, dk, gq, gk, tabs, name):
    t, width = q_raw.shape
    tm = _row_tile(t, 256)

    def body(q_ref, k_ref, p_ref, dq_ref, dk_ref, gq_ref, gk_ref, c_ref, sa_ref, sb_ref,
             dqr_ref, dkr_ref, dkpe_ref, dgq_ref, dgk_ref):
        c, sa, sb, kpe = c_ref[...], sa_ref[...], sb_ref[...], p_ref[...]

        def one(x, d, g_ref):
            n, r = _head_rms(x)
            dng = _rope_t(d, c, sa, sb)
            dn = dng * g_ref[...]
            dx = r * (dn - n * (jnp.sum(dn * n, axis=-1, keepdims=True) * (1.0 / QK_DIM)))
            return dx, jnp.sum(dng * n, axis=0, keepdims=True)

        dgq = dgk = dkpe = None
        for h in range(N_HEADS):
            cols = slice(h * HEAD_PAD, (h + 1) * HEAD_PAD)
            dqr, gq_part = one(q_ref[:, cols], dq_ref[:, cols], gq_ref)
            dkr, gk_part = one(k_ref[:, cols] + kpe, dk_ref[:, cols], gk_ref)
            dqr_ref[:, cols] = dqr
            dkr_ref[:, cols] = dkr
            dgq = gq_part if dgq is None else dgq + gq_part
            dgk = gk_part if dgk is None else dgk + gk_part
            dkpe = dkr if dkpe is None else dkpe + dkr
        dkpe_ref[...] = dkpe

        @pl.when(pl.program_id(0) == 0)
        def _():
            dgq_ref[...] = jnp.zeros_like(dgq_ref)
            dgk_ref[...] = jnp.zeros_like(dgk_ref)

        dgq_ref[...] += dgq
        dgk_ref[...] += dgk

    head = pl.BlockSpec((tm, width), lambda i: (i, 0))
    tab = pl.BlockSpec((tm, HEAD_PAD), lambda i: (i, 0))
    vec = pl.BlockSpec((1, HEAD_PAD), lambda i: (0, 0))
    kpe = pl.BlockSpec((tm, HEAD_PAD), lambda i: (i, P_KPE // HEAD_PAD))
    dqr, dkr, dkpe, dgq, dgk = pl.pallas_call(
        body, name=name, grid=(t // tm,), in_specs=[head, head, kpe, head, head, vec, vec, tab, tab, tab],
        out_specs=(head, head, tab, vec, vec),
        out_shape=(jax.ShapeDtypeStruct(q_raw.shape, F32), jax.ShapeDtypeStruct(k_raw.shape, F32),
                   jax.ShapeDtypeStruct((t, HEAD_PAD), F32), jax.ShapeDtypeStruct((1, HEAD_PAD), F32),
                   jax.ShapeDtypeStruct((1, HEAD_PAD), F32)),
        compiler_params=_params("arbitrary"),
    )(q_raw, k_raw, p, dq, dk, gq.reshape(1, HEAD_PAD), gk.reshape(1, HEAD_PAD), *tabs)
    return dqr, dkr, dkpe, dgq.reshape(HEAD_PAD), dgk.reshape(HEAD_PAD)


def _dot_nt(a, b):
    return lax.dot_general(a, b, (((1,), (1,)), ((), ())), preferred_element_type=F32)


def _dot_tn(a, b):
    return lax.dot_general(a, b, (((0,), (0,)), ((), ())), preferred_element_type=F32)


def _diag_mask():
    rows = lax.broadcasted_iota(jnp.int32, (ATTN_BLOCK, ATTN_BLOCK), 0) // CHUNK
    cols = lax.broadcasted_iota(jnp.int32, (ATTN_BLOCK, ATTN_BLOCK), 1) // CHUNK
    return cols <= rows


def _attn_fwd(q, k, v, name):
    t = q.shape[0]
    bq = ATTN_BLOCK
    nq = t // bq

    def body(q_ref, k_ref, v_ref, o_ref, lse_ref):
        i = pl.program_id(1)
        qv = q_ref[...]

        def block(j, carry, masked):
            m, l, acc = carry
            rows = pl.ds(pl.multiple_of(j * bq, bq), bq)
            s = _dot_nt(qv, k_ref[rows, :]) * ATTN_SCALE_LOG2
            if masked:
                s = jnp.where(_diag_mask(), s, -1e30)
            m_new = jnp.maximum(m, jnp.max(s, axis=-1, keepdims=True))
            alpha = jnp.exp2(m - m_new)
            pe = jnp.exp2(s - m_new)
            l = alpha * l + jnp.sum(pe, axis=-1, keepdims=True)
            acc = alpha * acc + jnp.dot(pe.astype(BF16), v_ref[rows, :], preferred_element_type=F32)
            return m_new, l, acc

        init = (jnp.full((bq, 1), -1e30, F32), jnp.zeros((bq, 1), F32), jnp.zeros((bq, HEAD_PAD), F32))
        carry = lax.fori_loop(0, i, lambda j, cr: block(j, cr, False), init)
        m, l, acc = block(i, carry, True)
        o_ref[...] = acc / l
        lse_ref[...] = jnp.broadcast_to(m + jnp.log2(l), (bq, HEAD_PAD))

    blk = pl.BlockSpec((bq, HEAD_PAD), lambda h, i: (i, h))
    full = pl.BlockSpec((t, HEAD_PAD), lambda h, i: (0, h))
    return pl.pallas_call(
        body, name=name, grid=(N_HEADS, nq), in_specs=[blk, full, full], out_specs=(blk, blk),
        out_shape=(jax.ShapeDtypeStruct(q.shape, F32), jax.ShapeDtypeStruct(q.shape, F32)),
        compiler_params=_params("parallel", "parallel"),
    )(q, k, v)


def _attn_bwd(q, k, v, o, lse, do, name):
    t = q.shape[0]
    bq = ATTN_BLOCK
    nq = t // bq

    def body(q_ref, k_ref, v_ref, o_ref, lse_ref, do_ref, dq_ref, dk_ref, dv_ref, delta_ref):
        def rows_of(i):
            return pl.ds(pl.multiple_of(i * bq, bq), bq)

        def prep(i, _):
            r = rows_of(i)
            delta_ref[r, :] = jnp.broadcast_to(jnp.sum(do_ref[r, :] * o_ref[r, :], axis=-1, keepdims=True),
                                               (bq, HEAD_PAD))
            dq_ref[r, :] = jnp.zeros((bq, HEAD_PAD), F32)
            return 0

        lax.fori_loop(0, nq, prep, 0)

        def key_block(j, _):
            rj = rows_of(j)
            kb, vb = k_ref[rj, :], v_ref[rj, :]

            def query_block(i, carry, masked):
                dk, dv = carry
                ri = rows_of(i)
                qb, dob = q_ref[ri, :], do_ref[ri, :].astype(BF16)
                s = _dot_nt(qb, kb) * ATTN_SCALE_LOG2
                if masked:
                    s = jnp.where(_diag_mask(), s, -1e30)
                pe = jnp.exp2(s - lse_ref[ri, :][:, :1])
                dp = _dot_nt(dob, vb)
                ds = (pe * (dp - delta_ref[ri, :][:, :1]) * ATTN_SCALE).astype(BF16)
                dq_ref[ri, :] += jnp.dot(ds, kb, preferred_element_type=F32)
                return dk + _dot_tn(ds, qb), dv + _dot_tn(pe.astype(BF16), dob)

            zero = jnp.zeros((bq, HEAD_PAD), F32)
            carry = query_block(j, (zero, zero), True)
            dk, dv = lax.fori_loop(j + 1, nq, lambda i, cr: query_block(i, cr, False), carry)
            dk_ref[rj, :] = dk
            dv_ref[rj, :] = dv
            return 0

        lax.fori_loop(0, nq, key_block, 0)

    full = pl.BlockSpec((t, HEAD_PAD), lambda h: (0, h))
    shp = jax.ShapeDtypeStruct(q.shape, F32)
    return pl.pallas_call(
        body, name=name, grid=(N_HEADS,), in_specs=[full] * 6, out_specs=(full, full, full),
        out_shape=(shp, shp, shp), scratch_shapes=[pltpu.VMEM((t, HEAD_PAD), F32)],
        compiler_params=_params("parallel"),
    )(q, k, v, o, lse, do)


def _glu_ext(pc_ref, pp_ref, u_ref, tm, first):
    u_ref[CONV_HALO:CONV_HALO + tm, :] = pc_ref[:, P_A:P_G] * jax.nn.sigmoid(pc_ref[:, P_G:P_COLS])
    up = pp_ref[tm - CONV_HALO:tm, P_A:P_G] * jax.nn.sigmoid(pp_ref[tm - CONV_HALO:tm, P_G:P_COLS])
    u_ref[0:CONV_HALO, :] = jnp.where(first, 0.0, up)


SUBLANES = 8


def _shift_copies(src_ref, sh_ref):
    rows = sh_ref.shape[1]
    for b in range(1, SUBLANES):
        sh_ref[b - 1, :, :] = src_ref[b:b + rows, :]


def _rows_at(src_ref, sh_ref, start, n):
    a, b = divmod(start, SUBLANES)
    if b == 0:
        return src_ref[SUBLANES * a:SUBLANES * a + n, :]
    return sh_ref[b - 1, SUBLANES * a:SUBLANES * a + n, :]


def _conv_fwd(p, w, b, ln_g, ln_b, name):
    t = p.shape[0]
    tm = _row_tile(t, CONV_TILE)
    off = CONV_HALO - (CONV_K - 1)

    def body(pc_ref, pp_ref, w_ref, b_ref, g_ref, bb_ref, y_ref, o_ref, u_ref, ush_ref):
        _glu_ext(pc_ref, pp_ref, u_ref, tm, pl.program_id(0) == 0)
        _shift_copies(u_ref, ush_ref)
        acc = jnp.zeros((tm, CONV_W), F32)
        for kk in range(CONV_K):
            acc = acc + w_ref[kk:kk + 1, :] * _rows_at(u_ref, ush_ref, off + kk, tm)
        y = acc + b_ref[...]
        y_ref[...] = y
        xc = y - jnp.mean(y, axis=-1, keepdims=True)
        lo = xc * lax.rsqrt(jnp.mean(xc * xc, axis=-1, keepdims=True) + EPS) * g_ref[...] + bb_ref[...]
        o_ref[...] = (lo * jax.nn.sigmoid(lo)).astype(o_ref.dtype)

    prow = pl.BlockSpec((tm, P_COLS), lambda i: (i, 0))
    pprev = pl.BlockSpec((tm, P_COLS), lambda i: (jnp.maximum(i - 1, 0), 0))
    vec = pl.BlockSpec((1, CONV_W), lambda i: (0, 0))
    row = pl.BlockSpec((tm, CONV_W), lambda i: (i, 0))
    return pl.pallas_call(
        body, name=name, grid=(t // tm,),
        in_specs=[prow, pprev, pl.BlockSpec((CONV_HALO, CONV_W), lambda i: (0, 0)), vec, vec, vec],
        out_specs=(row, row),
        out_shape=(jax.ShapeDtypeStruct((t, CONV_W), F32), jax.ShapeDtypeStruct((t, CONV_W), BF16)),
        scratch_shapes=[pltpu.VMEM((tm + CONV_HALO, CONV_W), F32),
                        pltpu.VMEM((SUBLANES - 1, tm + CONV_HALO - SUBLANES, CONV_W), F32)],
        compiler_params=_params("parallel"),
    )(p, p, w, b.reshape(1, CONV_W), ln_g.reshape(1, CONV_W), ln_b.reshape(1, CONV_W))


def _conv_bwd_ln(y, dout, ln_g, ln_b, name):
    t = y.shape[0]
    tm = _row_tile(t)

    def body(y_ref, d_ref, g_ref, bb_ref, dy_ref, dg_ref, db_ref, dcb_ref):
        yv = y_ref[...]
        xc = yv - jnp.mean(yv, axis=-1, keepdims=True)
        r = lax.rsqrt(jnp.mean(xc * xc, axis=-1, keepdims=True) + EPS)
        n = xc * r
        lo = n * g_ref[...] + bb_ref[...]
        s = jax.nn.sigmoid(lo)
        dlo = d_ref[...] * (s * (1.0 + lo * (1.0 - s)))
        dn = dlo * g_ref[...]
        dy = r * (dn - jnp.mean(dn, axis=-1, keepdims=True) - n * jnp.mean(dn * n, axis=-1, keepdims=True))
        dy_ref[...] = dy

        @pl.when(pl.program_id(0) == 0)
        def _():
            dg_ref[...] = jnp.zeros_like(dg_ref)
            db_ref[...] = jnp.zeros_like(db_ref)
            dcb_ref[...] = jnp.zeros_like(dcb_ref)

        dg_ref[...] += jnp.sum(dlo * n, axis=0, keepdims=True)
        db_ref[...] += jnp.sum(dlo, axis=0, keepdims=True)
        dcb_ref[...] += jnp.sum(dy, axis=0, keepdims=True)

    row = pl.BlockSpec((tm, CONV_W), lambda i: (i, 0))
    vec = pl.BlockSpec((1, CONV_W), lambda i: (0, 0))
    vshape = jax.ShapeDtypeStruct((1, CONV_W), F32)
    dy, dg, db, dcb = pl.pallas_call(
        body, name=name, grid=(t // tm,), in_specs=[row, row, vec, vec], out_specs=(row, vec, vec, vec),
        out_shape=(jax.ShapeDtypeStruct((t, CONV_W), F32), vshape, vshape, vshape),
        compiler_params=_params("arbitrary"),
    )(y, dout, ln_g.reshape(1, CONV_W), ln_b.reshape(1, CONV_W))
    return dy, dg.reshape(CONV_W), db.reshape(CONV_W), dcb.reshape(CONV_W)


def _conv_bwd_taps(p, dy, w, name):
    t = p.shape[0]
    tm = _row_tile(t, CONV_TILE)
    nt = t // tm
    off = CONV_HALO - (CONV_K - 1)

    def body(pc_ref, pp_ref, dyc_ref, dyn_ref, w_ref, dag_ref, dw_ref, u_ref, dye_ref, ush_ref, dysh_ref):
        i = pl.program_id(0)
        _glu_ext(pc_ref, pp_ref, u_ref, tm, i == 0)
        dyc = dyc_ref[...]
        dye_ref[0:tm, :] = dyc
        dye_ref[tm:tm + CONV_HALO, :] = jnp.where(i == nt - 1, 0.0, dyn_ref[0:CONV_HALO, :])

        _shift_copies(u_ref, ush_ref)
        _shift_copies(dye_ref, dysh_ref)

        @pl.when(i == 0)
        def _():
            dw_ref[...] = jnp.zeros_like(dw_ref)

        du = jnp.zeros((tm, CONV_W), F32)
        for kk in range(CONV_K):
            dw_ref[kk:kk + 1, :] += jnp.sum(dyc * _rows_at(u_ref, ush_ref, off + kk, tm), axis=0, keepdims=True)
            du = du + w_ref[kk:kk + 1, :] * _rows_at(dye_ref, dysh_ref, CONV_K - 1 - kk, tm)
        av, gv = pc_ref[:, P_A:P_G], pc_ref[:, P_G:P_COLS]
        s = jax.nn.sigmoid(gv)
        dag_ref[:, 0:CONV_W] = du * s
        dag_ref[:, CONV_W:2 * CONV_W] = du * av * (s * (1.0 - s))

    prow = pl.BlockSpec((tm, P_COLS), lambda i: (i, 0))
    pprev = pl.BlockSpec((tm, P_COLS), lambda i: (jnp.maximum(i - 1, 0), 0))
    row = pl.BlockSpec((tm, CONV_W), lambda i: (i, 0))
    nxt = pl.BlockSpec((tm, CONV_W), lambda i: (jnp.minimum(i + 1, nt - 1), 0))
    wspec = pl.BlockSpec((CONV_HALO, CONV_W), lambda i: (0, 0))
    return pl.pallas_call(
        body, name=name, grid=(nt,), in_specs=[prow, pprev, row, nxt, wspec],
        out_specs=(pl.BlockSpec((tm, 2 * CONV_W), lambda i: (i, 0)), wspec),
        out_shape=(jax.ShapeDtypeStruct((t, 2 * CONV_W), F32), jax.ShapeDtypeStruct((CONV_HALO, CONV_W), F32)),
        scratch_shapes=[pltpu.VMEM((tm + CONV_HALO, CONV_W), F32), pltpu.VMEM((tm + CONV_HALO, CONV_W), F32),
                        pltpu.VMEM((SUBLANES - 1, tm + CONV_HALO - SUBLANES, CONV_W), F32),
                        pltpu.VMEM((SUBLANES - 1, tm + CONV_HALO - SUBLANES, CONV_W), F32)],
        compiler_params=_params("arbitrary"),
    )(p, p, dy, dy, w)


def _loss_head(y, target, name):
    t, d = y.shape
    tm = _row_tile(t)

    def body(y_ref, t_ref, l_ref, dy_ref):
        err = y_ref[...] - t_ref[...]
        dy_ref[...] = err * (1.0 / d)

        @pl.when(pl.program_id(0) == 0)
        def _():
            l_ref[...] = jnp.zeros_like(l_ref)

        row = jnp.sum(err * err, axis=-1, keepdims=True) * (0.5 / d)
        l_ref[...] += jnp.broadcast_to(jnp.sum(row, axis=0, keepdims=True), (1, LANE))

    row = pl.BlockSpec((tm, d), lambda i: (i, 0))
    return pl.pallas_call(
        body, name=name, grid=(t // tm,), in_specs=[row, row],
        out_specs=(pl.BlockSpec((1, LANE), lambda i: (0, 0)), row),
        out_shape=(jax.ShapeDtypeStruct((1, LANE), F32), jax.ShapeDtypeStruct((t, d), F32)),
        compiler_params=_params("arbitrary"),
    )(y, target)


def _adamw(w, g, m, v, name):
    r, c = w.shape
    tr = _row_tile(r, 256)
    c1, c2 = 1.0 - ADAM_B1 ** ADAM_STEP, 1.0 - ADAM_B2 ** ADAM_STEP

    def body(w_ref, g_ref, m_ref, v_ref, d_ref, mo_ref, vo_ref):
        gv = g_ref[...]
        mn = ADAM_B1 * m_ref[...] + (1.0 - ADAM_B1) * gv
        vn = ADAM_B2 * v_ref[...] + (1.0 - ADAM_B2) * (gv * gv)
        mo_ref[...] = mn
        vo_ref[...] = vn
        d_ref[...] = -ADAM_LR * ((mn / c1) / (jnp.sqrt(vn / c2) + ADAM_EPS) + ADAM_WD * w_ref[...])

    blk = pl.BlockSpec((tr, c), lambda i: (i, 0))
    shp = jax.ShapeDtypeStruct((r, c), F32)
    return pl.pallas_call(
        body, name=name, grid=(r // tr,), in_specs=[blk] * 4, out_specs=(blk, blk, blk), out_shape=(shp, shp, shp),
        compiler_params=_params("parallel"),
    )(w, g, m, v)


def _sum_parts(parts, name):
    r, c = parts[0].shape
    tr = _row_tile(r, 256)

    def body(*refs):
        acc = refs[0][...]
        for ref in refs[1:-1]:
            acc = acc + ref[...]
        refs[-1][...] = acc

    blk = pl.BlockSpec((tr, c), lambda i: (i, 0))
    return pl.pallas_call(
        body, name=name, grid=(r // tr,), in_specs=[blk] * len(parts), out_specs=blk,
        out_shape=jax.ShapeDtypeStruct((r, c), F32), compiler_params=_params("parallel"),
    )(*parts)


def _place():
    return lax.axis_index("x"), lax.axis_index("y"), lax.axis_index("c")


def _window(ref, block, size, axis):
    start = pl.multiple_of(block * size, LANE if size % LANE == 0 else 8)
    return ref.at[(slice(None),) * axis + (pl.ds(start, size),)]


def _all_gather(pieces, name, in_vmem=False, seeds=()):
    n_p, pieces = len(pieces), list(pieces) + list(seeds)
    n_all = len(pieces)

    def body(*refs):
        x_refs, out_refs = refs[:n_all], refs[n_all:2 * n_all]
        send_sems, recv_sems, local_sems = refs[2 * n_all:]
        px, py, pc = _place()
        me, sibling = (px, py, pc), (px, py, 1 - pc)
        chips = [(1 - px, py), (px, 1 - py), (1 - px, 1 - py)]

        def win(p, block):
            bx, by, bc = block
            x, axis = pieces[p]
            return _window(out_refs[p], 4 * bx + 2 * by + bc, x.shape[axis], axis)

        def copy(k, p, block, to, local=False):
            return pltpu.make_async_remote_copy(
                src_ref=x_refs[p] if local else win(p, block), dst_ref=win(p, block),
                send_sem=send_sems.at[k, p], recv_sem=recv_sems.at[k, p], device_id=to, device_id_type=MESH_ID)

        every = range(n_p)
        mine = [pltpu.make_async_copy(x_refs[p], win(p, me), local_sems.at[p]) for p in range(n_all)]
        first = [copy(0, p, me, sibling, local=True) for p in every]
        first += [copy(1 + j, p, me, (*chip, pc), local=True) for j, chip in enumerate(chips) for p in every]
        for cp in mine + first:
            cp.start()
        passed = []
        for j, chip in enumerate(chips):
            for p in every:
                copy(1 + j, p, (*chip, pc), me).wait_recv()
                passed.append(copy(4 + j, p, (*chip, pc), sibling))
                passed[-1].start()
        for p in every:
            copy(0, p, sibling, me).wait_recv()
        for j, chip in enumerate(chips):
            for p in every:
                copy(4 + j, p, (*chip, 1 - pc), me).wait_recv()
        for cp in first + passed:
            cp.wait_send()
        for cp in mine:
            cp.wait()

    def gathered(x, axis):
        return jax.ShapeDtypeStruct(x.shape[:axis] + (N_DEV * x.shape[axis],) + x.shape[axis + 1:], x.dtype)

    spec = VMEM_SPEC if in_vmem else ANY
    return pl.pallas_call(
        body, name=name, in_specs=[spec] * n_all, out_specs=[spec] * n_all,
        out_shape=[gathered(*pc_) for pc_ in pieces],
        scratch_shapes=[pltpu.SemaphoreType.DMA((7, n_p)), pltpu.SemaphoreType.DMA((7, n_p)),
                        pltpu.SemaphoreType.DMA((n_all,))],
        compiler_params=pltpu.CompilerParams(vmem_limit_bytes=VMEM_LIMIT),
    )(*[x for x, _ in pieces])


def _start_copies(bufs, n_copies, plan, name, after=None):
    nb = len(bufs)
    n_in = nb + (after is not None)

    def body(*refs):
        send_sems, recv_sems, token = refs[n_in], refs[n_in + 1], refs[-1]
        for i, (src, dst, dev) in enumerate(plan(refs[:nb])):
            pltpu.make_async_remote_copy(src_ref=src, dst_ref=dst, send_sem=send_sems.at[i], recv_sem=recv_sems.at[i],
                                         device_id=dev, device_id_type=MESH_ID).start()
        token[...] = jnp.zeros_like(token)

    out = pl.pallas_call(
        body, name=name, in_specs=[HBM_SPEC] * nb + [ANY] * (after is not None),
        out_shape=(pltpu.SemaphoreType.DMA((n_copies,)), pltpu.SemaphoreType.DMA((n_copies,)),
                   *[pltpu.HBM(b.shape, b.dtype) for b in bufs], jax.ShapeDtypeStruct((8, LANE), F32)),
        out_specs=(SEM_SPEC, SEM_SPEC, *[HBM_SPEC] * nb, VMEM_SPEC),
        input_output_aliases={i: 2 + i for i in range(nb)},
        compiler_params=pltpu.CompilerParams(has_side_effects=DATAFLOW),
    )(*[pltpu.with_memory_space_constraint(b, pltpu.HBM) for b in bufs], *([after] if after is not None else []))
    return out[0], out[1], list(out[2:2 + nb]), out[-1]


def _wait_copies(started, after, n_copies, plan, name):
    send_sems, recv_sems, bufs, _ = started
    nb = len(bufs)

    def body(*refs):
        send_ref, recv_ref = refs[nb], refs[nb + 1]
        copies = [pltpu.make_async_remote_copy(src_ref=src, dst_ref=dst, send_sem=send_ref.at[i], recv_sem=recv_ref.at[i],
                                               device_id=dev, device_id_type=MESH_ID)
                  for i, (src, dst, dev) in enumerate(plan(refs[:nb]))]
        for cp in copies:
            cp.wait_send()
        for cp in copies:
            cp.wait_recv()

    out = pl.pallas_call(
        body, name=name, in_specs=[HBM_SPEC] * nb + [SEM_SPEC, SEM_SPEC, ANY],
        out_shape=tuple(pltpu.HBM(b.shape, b.dtype) for b in bufs), out_specs=tuple([HBM_SPEC] * nb),
        input_output_aliases={i: i for i in range(nb)},
        compiler_params=pltpu.CompilerParams(has_side_effects=DATAFLOW),
    )(*bufs, send_sems, recv_sems, after)
    return list(out)


def _after(x, token):
    return x + token[0, 0].astype(x.dtype)


def _other_chips():
    px, py, _ = _place()
    return [(1 - px, py), (px, 1 - py), (1 - px, 1 - py)]


def _exchange(srcs, slots, src_block, target, name):
    n_p = len(srcs)

    def body(*refs):
        src_refs, out_refs, send_sems, recv_sems = refs[:n_p], refs[n_p:2 * n_p], refs[-2], refs[-1]
        copies = [pltpu.make_async_remote_copy(
            src_ref=src_refs[p].at[src_block(s)], dst_ref=out_refs[p].at[s], send_sem=send_sems.at[s, p],
            recv_sem=recv_sems.at[s, p], device_id=target(s), device_id_type=MESH_ID)
            for s in range(slots) for p in range(n_p)]
        for cp in copies:
            cp.start()
        for cp in copies:
            cp.wait_recv()
        for cp in copies:
            cp.wait_send()

    return pl.pallas_call(
        body, name=name, in_specs=[ANY] * n_p, out_specs=[ANY] * n_p,
        out_shape=[jax.ShapeDtypeStruct((slots,) + a.shape[1:], a.dtype) for a in srcs],
        scratch_shapes=[pltpu.SemaphoreType.DMA((slots, n_p)), pltpu.SemaphoreType.DMA((slots, n_p))],
        compiler_params=pltpu.CompilerParams(vmem_limit_bytes=VMEM_LIMIT),
    )(*srcs)


def _blocks_to_sibling(sends, name):
    def src_block(j):
        return 2 * j + 1 - lax.axis_index("c")

    def target(j):
        px, py, pc = _place()
        return (px, py, 1 - pc)

    return _exchange(sends, 4, src_block, target, name)


def _pair_sums_for_chips(own, got, name):
    _, r, c = own.shape
    tr = _row_tile(r, 256, 16)

    def body(idx_ref, own_ref, got_ref, o_ref):
        o_ref[...] = (own_ref[...] + got_ref[...].astype(F32)).astype(o_ref.dtype)

    grid_spec = pltpu.PrefetchScalarGridSpec(
        num_scalar_prefetch=1, grid=(3, r // tr),
        in_specs=[pl.BlockSpec((None, tr, c), lambda k, i, idx: (idx[k], i, 0)),
                  pl.BlockSpec((None, tr, c), lambda k, i, idx: (idx[3 + k], i, 0))],
        out_specs=pl.BlockSpec((None, tr, c), lambda k, i, idx: (k, i, 0)))
    chips = [2 * cx + cy for cx, cy in _other_chips()]
    idx = jnp.stack([2 * j + lax.axis_index("c") for j in chips] + chips).astype(jnp.int32)
    return pl.pallas_call(
        body, name=name, grid_spec=grid_spec, out_shape=jax.ShapeDtypeStruct((3, r, c), BF16),
        compiler_params=_params("parallel", "parallel"),
    )(idx, own, got)


def _sum_for_me(own, got_sibling, got_chips, name):
    _, r, c = own.shape
    tr = _row_tile(r, 256, 16)

    def body(idx_ref, own_ref, sib_ref, g0_ref, g1_ref, g2_ref, o_ref):
        acc = own_ref[...] + sib_ref[...].astype(F32)
        for ref in (g0_ref, g1_ref, g2_ref):
            acc = acc + ref[...].astype(F32)
        o_ref[...] = acc

    def part(k):
        return pl.BlockSpec((None, tr, c), lambda i, idx: (k, i, 0))

    grid_spec = pltpu.PrefetchScalarGridSpec(
        num_scalar_prefetch=1, grid=(r // tr,),
        in_specs=[pl.BlockSpec((None, tr, c), lambda i, idx: (idx[0], i, 0)),
                  pl.BlockSpec((None, tr, c), lambda i, idx: (idx[1], i, 0)), part(0), part(1), part(2)],
        out_specs=pl.BlockSpec((tr, c), lambda i, idx: (i, 0)))
    px, py, pc = _place()
    idx = jnp.stack([4 * px + 2 * py + pc, 2 * px + py]).astype(jnp.int32)
    return pl.pallas_call(
        body, name=name, grid_spec=grid_spec, out_shape=jax.ShapeDtypeStruct((r, c), F32),
        compiler_params=_params("parallel"),
    )(idx, own, got_sibling, got_chips, got_chips, got_chips)


def _chip_plan(n_p):
    def plan(refs):
        pc = lax.axis_index("c")
        return [(refs[p].at[k], refs[n_p + p].at[k], (cx, cy, pc))
                for p in range(n_p) for k, (cx, cy) in enumerate(_other_chips())]
    return plan


def _reduce_start(own, sends, tag):
    from_sibling = _blocks_to_sibling(sends, "grads_to_sibling_" + tag)
    pair_sums = [_pair_sums_for_chips(a, b, "grads_pair_sums") for a, b in zip(own, from_sibling)]
    lands = [lax.empty(a.shape, a.dtype) for a in pair_sums]
    started = _start_copies(pair_sums + lands, 3 * len(own), _chip_plan(len(own)), "grads_to_chips_start_" + tag)
    return from_sibling, started


def _reduce_finish(own, from_sibling, started, after, tag):
    n_p = len(own)
    bufs = _wait_copies(started, after, 3 * n_p, _chip_plan(n_p), "grads_to_chips_wait_" + tag)
    return [_sum_for_me(a, b, c, "grads_sum") for a, b, c in zip(own, from_sibling, bufs[n_p:])]


def _gather_plans(pieces):
    n_p = len(pieces)
    dims = [(x.shape[axis], axis) for x, axis in pieces]

    def first(refs):
        px, py, pc = _place()
        targets = [(px, py, 1 - pc)] + [(cx, cy, pc) for cx, cy in _other_chips()]
        return [(refs[p], _window(refs[n_p + p], 4 * px + 2 * py + pc, *dims[p]), to)
                for p in range(n_p) for to in targets]

    def second(refs):
        px, py, pc = _place()
        out = []
        for p in range(n_p):
            for cx, cy in _other_chips():
                win = _window(refs[p], 4 * cx + 2 * cy + pc, *dims[p])
                out.append((win, win, (px, py, 1 - pc)))
        return out

    return first, second


def _flat_rows(a, width):
    return a.reshape(-1, width)


def _full_from_blocks(blocks, name):
    if name in COL_SHARDED:
        _, l, k, nb = blocks.shape
        return jnp.transpose(blocks, (1, 2, 0, 3)).reshape(l, k, N_DEV * nb)
    _, l, rb, n = blocks.shape
    return jnp.transpose(blocks, (1, 0, 2, 3)).reshape(l, N_DEV * rb, n)


def _blocks_from_full(full, name):
    if name in COL_SHARDED:
        l, k, n = full.shape
        return jnp.transpose(full.reshape(l, k, N_DEV, n // N_DEV), (2, 0, 1, 3))
    l, rows, n = full.shape
    return jnp.transpose(full.reshape(l, N_DEV, rows // N_DEV, n), (1, 0, 2, 3))


def _pad_heads(w, width):
    k = w.shape[0]
    return jnp.pad(w.reshape(k, N_HEADS, width), ((0, 0), (0, 0), (0, HEAD_PAD - width))).reshape(k, N_HEADS * HEAD_PAD)


def _unpad_heads(w, width):
    k = w.shape[0]
    return w.reshape(k, N_HEADS, HEAD_PAD)[:, :, :width].reshape(k, N_HEADS * width)


def _layer_operands(full, vec, conv_w_full, l):
    w_in = full['w_in'][l]
    kpe = jnp.pad(w_in[:, LAT:LAT + QK_ROPE], ((0, 0), (QK_NOPE, HEAD_PAD - QK_DIM)))
    w_ukv = full['w_ukv'][l].reshape(KV_LORA, N_HEADS, QK_NOPE + V_DIM)
    w_out = full['w_out'][l]
    d_model = w_out.shape[1]
    wo_attn = jnp.pad(w_out[:N_HEADS * V_DIM].reshape(N_HEADS, V_DIM, d_model),
                      ((0, 0), (0, HEAD_PAD - V_DIM), (0, 0))).reshape(N_HEADS * HEAD_PAD, d_model)
    ops = {
        'w_in': jnp.concatenate([w_in[:, :LAT], kpe, w_in[:, LAT + QK_ROPE:]], axis=1),
        'w_q': _pad_heads(full['w_uq'][l], QK_DIM),
        'w_k': _pad_heads(w_ukv[:, :, :QK_NOPE].reshape(KV_LORA, N_HEADS * QK_NOPE), QK_NOPE),
        'w_v': _pad_heads(w_ukv[:, :, QK_NOPE:].reshape(KV_LORA, N_HEADS * V_DIM), V_DIM),
        'wo_attn': wo_attn,
        'wo_conv': w_out[N_HEADS * V_DIM:],
        'conv_w': jnp.pad(conv_w_full[l], ((0, CONV_HALO - CONV_K), (0, 0))),
        'gq': jnp.pad(vec['q_norm'][l], (0, HEAD_PAD - QK_DIM)),
        'gk': jnp.pad(vec['k_norm'][l], (0, HEAD_PAD - QK_DIM)),
    }
    for n in ('ffn1_norm', 'mix_norm', 'q_latent_norm', 'kv_latent_norm', 'conv_b', 'conv_ln_g', 'conv_ln_b',
              'ffn2_norm', 'post_norm'):
        ops[n] = vec[n][l]
    return ops


def _ffn_fwd(x, g, wgu, wd, fp):
    h = _rms_fwd(x, g, BF16, "rms_fwd_ffn")
    ab, z = _ffn_up(h, wgu, fp, "ffn_up")
    y = _mm(z, wd, res=x, scale=0.5, name="ffn_down")
    return y, (x, h, ab, z)


def _ffn_bwd(dy, dyb, saved, g, wgu, wd, fp, after_dw=None, after=None):
    x, h, ab, z = saved
    d_wd = _mm(z, dyb, ta=True, scale=0.5, blocks=('row', N_DEV), after=after, name="ffn_dwd")
    dab = _ffn_dab(dyb, wd, ab, fp, "ffn_dab")
    d_wgu = _mm(h, dab, ta=True, blocks=('col', N_DEV), name="ffn_dwgu")
    token = after_dw(d_wgu, d_wd) if after_dw is not None else None
    dh = _mm(dab, wgu, tb=True, after=token, name="ffn_dh")
    dx, dxb, dg = _rms_bwd(x, g, dh, dy, "rms_bwd_ffn")
    return dx, dxb, dg, d_wgu, d_wd


def _mixer_fwd(x, ops, tabs, after_attention=None):
    h = _rms_fwd(x, ops['mix_norm'], BF16, "rms_fwd_mix")
    p = _mm(h, ops['w_in'], name="mix_in")
    qln, kvln = _lat_norm_fwd(p, ops['q_latent_norm'], ops['kv_latent_norm'], "lat_norm_fwd")
    q_raw = _mm(qln, ops['w_q'], name="mix_q")
    k_raw = _mm(kvln, ops['w_k'], name="mix_k")
    v = _mm(kvln, ops['w_v'], out_dtype=BF16, name="mix_v")
    q, k = _qk_prep_fwd(q_raw, k_raw, p, ops['gq'], ops['gk'], tabs, "qk_prep_fwd")
    o, lse = _attn_fwd(q, k, v, "attn_fwd")
    token = after_attention(o) if after_attention is not None else None
    conv_b = ops['conv_b'] if token is None else _after(ops['conv_b'], token)
    y_conv, cv = _conv_fwd(p, ops['conv_w'], conv_b, ops['conv_ln_g'], ops['conv_ln_b'], "conv_fwd")
    x_attn = _mm(o, ops['wo_attn'], res=x, name="mix_out_attn")
    x_out = _mm(cv, ops['wo_conv'], res=x_attn, name="mix_out_conv")
    return x_out, (x, h, p, qln, kvln, q_raw, k_raw, v, q, k, o, lse, y_conv, cv)


def _mixer_bwd(dx_out, dxb_out, saved, ops, tabs, token=None):
    x, h, p, qln, kvln, q_raw, k_raw, v, q, k, o, lse, y_conv, cv = saved
    g = {}
    do = _mm(dxb_out, ops['wo_attn'], tb=True, after=token, name="mix_do")
    dcv = _mm(dxb_out, ops['wo_conv'], tb=True, name="mix_dcv")
    g['wo_attn'] = _mm(o, dxb_out, ta=True, name="mix_dwo_attn")
    g['wo_conv'] = _mm(cv, dxb_out, ta=True, name="mix_dwo_conv")
    dq, dk, dv = _attn_bwd(q, k, v, o, lse, do, "attn_bwd")
    dq_raw, dk_raw, dkpe, g['gq'], g['gk'] = _qk_prep_bwd(q_raw, k_raw, p, dq, dk, ops['gq'], ops['gk'], tabs,
                                                          "qk_prep_bwd")
    g['w_q'] = _mm(qln, dq_raw, ta=True, name="mix_dwq")
    g['w_k'] = _mm(kvln, dk_raw, ta=True, name="mix_dwk")
    g['w_v'] = _mm(kvln, dv, ta=True, name="mix_dwv")
    dqln = _mm(dq_raw, ops['w_q'], tb=True, name="mix_dqln")
    dkvln = _mm(dk_raw, ops['w_k'], tb=True, name="mix_dkvln_k")
    dkvln = _mm(dv, ops['w_v'], tb=True, res=dkvln, name="mix_dkvln_v")
    dp_lat, g['q_latent_norm'], g['kv_latent_norm'] = _lat_norm_bwd(
        p, ops['q_latent_norm'], ops['kv_latent_norm'], dqln, dkvln, "lat_norm_bwd")
    dy_conv, g['conv_ln_g'], g['conv_ln_b'], g['conv_b'] = _conv_bwd_ln(
        y_conv, dcv, ops['conv_ln_g'], ops['conv_ln_b'], "conv_bwd_ln")
    dag, g['conv_w'] = _conv_bwd_taps(p, dy_conv, ops['conv_w'], "conv_bwd_taps")
    dp = jnp.concatenate([dp_lat, dkpe, dag], axis=1)
    g['w_in'] = _mm(h, dp, ta=True, name="mix_dw_in")
    dh = _mm(dp, ops['w_in'], tb=True, name="mix_dh")
    dx, dxb, g['mix_norm'] = _rms_bwd(x, ops['mix_norm'], dh, dx_out, "rms_bwd_mix")
    return dx, dxb, g


def _mixer_grads_to_params(g):
    d_w_in = g['w_in']
    d_wk = _unpad_heads(g['w_k'], QK_NOPE).reshape(KV_LORA, N_HEADS, QK_NOPE)
    d_wv = _unpad_heads(g['w_v'], V_DIM).reshape(KV_LORA, N_HEADS, V_DIM)
    d_model = g['wo_attn'].shape[1]
    d_wo_attn = g['wo_attn'].reshape(N_HEADS, HEAD_PAD, d_model)[:, :V_DIM].reshape(N_HEADS * V_DIM, d_model)
    return {
        'mix_norm': g['mix_norm'],
        'w_in': jnp.concatenate([d_w_in[:, :LAT], d_w_in[:, LAT + QK_NOPE:LAT + QK_DIM], d_w_in[:, P_A:]], axis=1),
        'q_latent_norm': g['q_latent_norm'], 'w_uq': _unpad_heads(g['w_q'], QK_DIM),
        'kv_latent_norm': g['kv_latent_norm'],
        'w_ukv': jnp.concatenate([d_wk, d_wv], axis=2).reshape(KV_LORA, N_HEADS * (QK_NOPE + V_DIM)),
        'q_norm': g['gq'][:QK_DIM], 'k_norm': g['gk'][:QK_DIM], 'conv_w': g['conv_w'][:CONV_K],
        'conv_b': g['conv_b'], 'conv_ln_g': g['conv_ln_g'], 'conv_ln_b': g['conv_ln_b'],
        'w_out': jnp.concatenate([d_wo_attn, g['wo_conv']], axis=0),
    }


def kernel(x, ffn1_norm, ffn1_w_gate, ffn1_w_up, ffn1_w_down, mix_norm, w_in, q_latent_norm, w_uq, kv_latent_norm, w_ukv, q_norm, k_norm, conv_w, conv_b, conv_ln_g, conv_ln_b, w_out, ffn2_norm, ffn2_w_gate, ffn2_w_up, ffn2_w_down, post_norm, loss_target, m_ffn1_norm, m_ffn1_w_gate, m_ffn1_w_up, m_ffn1_w_down, m_mix_norm, m_w_in, m_q_latent_norm, m_w_uq, m_kv_latent_norm, m_w_ukv, m_q_norm, m_k_norm, m_conv_w, m_conv_b, m_conv_ln_g, m_conv_ln_b, m_w_out, m_ffn2_norm, m_ffn2_w_gate, m_ffn2_w_up, m_ffn2_w_down, m_post_norm, v_ffn1_norm, v_ffn1_w_gate, v_ffn1_w_up, v_ffn1_w_down, v_mix_norm, v_w_in, v_q_latent_norm, v_w_uq, v_kv_latent_norm, v_w_ukv, v_q_norm, v_k_norm, v_conv_w, v_conv_b, v_conv_ln_g, v_conv_ln_b, v_w_out, v_ffn2_norm, v_ffn2_w_gate, v_ffn2_w_up, v_ffn2_w_down, v_post_norm):
    args = locals()
    w = {n: args[n] for n in WEIGHTS}
    mom = {n: args["m_" + n] for n in WEIGHTS}
    var = {n: args["v_" + n] for n in WEIGHTS}
    depth = ffn1_norm.shape[0]
    x0 = x.reshape(x.shape[-2:])
    target = loss_target.reshape(loss_target.shape[-2:])
    t, d_model = x0.shape
    my_block = 4 * lax.axis_index("x") + 2 * lax.axis_index("y") + lax.axis_index("c")

    fb = ffn1_w_gate.shape[-1]
    fp = -(-fb // LANE) * LANE
    ffns = [(l, f) for l in range(depth) for f in (1, 2)]
    pad_cols = lambda a: jnp.pad(a, ((0, 0), (0, fp - fb)))
    gu_local = {(l, f): jnp.concatenate([pad_cols(w[f'ffn{f}_w_gate'][l]), pad_cols(w[f'ffn{f}_w_up'][l])],
                                        axis=1).astype(BF16) for l, f in ffns}
    dn_local = {(l, f): jnp.pad(w[f'ffn{f}_w_down'][l], ((0, fp - fb), (0, 0))).astype(BF16) for l, f in ffns}
    rows_of = {n: w[n].size // d_model for n in REST}
    rest_local = jnp.concatenate([_flat_rows(w[n].astype(BF16), d_model) for n in REST], axis=0)
    n_rest = rest_local.shape[0]
    first_ffn, later = ffns[0], ffns[1:]
    cw = conv_w.reshape(-1)
    cw_rows = -(-cw.size // (8 * LANE)) * 8
    cw_flat = jnp.pad(cw, (0, cw_rows * LANE - cw.size)).reshape(cw_rows, LANE)
    later_pieces = [(gu_local[q], 1) for q in later] + [(dn_local[q], 0) for q in later]
    n_later = len(later_pieces)
    got = _all_gather([(gu_local[first_ffn], 1), (dn_local[first_ffn], 0), (rest_local, 0), (cw_flat, 0)],
                      "gather_first", seeds=later_pieces)
    wgu, wd = {first_ffn: got[0]}, {first_ffn: got[1]}
    gathered = got[2].reshape(N_DEV, n_rest, d_model)
    cw_all = got[3].reshape(N_DEV, cw_rows * LANE)[:, :cw.size]
    gather_plan, forward_plan = _gather_plans(later_pieces)
    gather_later = _start_copies([a for a, _ in later_pieces] + list(got[4:]), 4 * n_later, gather_plan,
                                 "gather_later_start")
    full, start = {}, 0
    for n in REST:
        blocks = gathered[:, start:start + rows_of[n]].reshape((N_DEV,) + w[n].shape)
        full[n] = _full_from_blocks(blocks, n)
        start += rows_of[n]
    conv_w_full =jnp.transpose(cw_all.reshape((N_DEV,) + conv_w.shape), (1, 2, 0, 3)).reshape(depth, CONV_K, CONV_W)
    vec = {n: w[n] for n in VECTORS}
    ops = [_layer_operands(full, vec, conv_w_full, l) for l in range(depth)]
    tabs = _rope_tables(t)

    saved, xl = [], x0
    forward_later = []

    def pass_on_later(o_attn):
        lands = _wait_copies(gather_later, o_attn, 4 * n_later, gather_plan, "gather_later_wait")[n_later:]
        forward_later.append(_start_copies(lands, 3 * n_later, forward_plan, "gather_later_forward_start"))
        return forward_later[0][3]

    for l in range(depth):
        o = ops[l]
        if l == 0:
            x1, s1 = _ffn_fwd(xl, _after(o['ffn1_norm'], gather_later[3]), wgu[l, 1], wd[l, 1], fp)
            x2, sm = _mixer_fwd(x1, o, tabs, after_attention=pass_on_later)
            lands = _wait_copies(forward_later[0], x2, 3 * n_later, forward_plan, "gather_later_forward_wait")
            wgu.update(zip(later, lands[:len(later)]))
            wd.update(zip(later, lands[len(later):]))
        else:
            x1, s1 = _ffn_fwd(xl, o['ffn1_norm'], wgu[l, 1], wd[l, 1], fp)
            x2, sm = _mixer_fwd(x1, o, tabs)
        x3, s2 = _ffn_fwd(x2, o['ffn2_norm'], wgu[l, 2], wd[l, 2], fp)
        xl = _rms_fwd(x3, o['post_norm'], F32, "rms_fwd_post")
        saved.append((s1, sm, s2, x3))
    loss_part, dx = _loss_head(xl, target, "loss_head")
    loss = lax.psum(loss_part[0, 0], ("x", "y", "c"))

    grads, mine_gu, mine_dn, in_flight = [None] * depth, {}, {}, {}

    def exchange(tag):
        def after_dw(d_wgu, d_wd):
            own = [d_wgu[0], d_wd[0]]
            from_sibling, started = _reduce_start(own, [d_wgu[1], d_wd[1]], tag)
            in_flight[tag] = (own, from_sibling, started)
            return started[3]
        return after_dw

    def finish(tag, after):
        own, from_sibling, started = in_flight.pop(tag)
        return _reduce_finish(own, from_sibling, started, after, tag)

    for l in reversed(range(depth)):
        o = ops[l]
        s1, sm, s2, x3 = saved[l]
        dx, dxb, d_post = _rms_bwd(x3, o['post_norm'], dx, None, "rms_bwd_post")
        dx, dxb, d_ffn2, _, _ = _ffn_bwd(dx, dxb, s2, o['ffn2_norm'], wgu[l, 2], wd[l, 2], fp, exchange(f"{l}2"))
        if l + 1 < depth:
            mine_gu[l + 1, 1], mine_dn[l + 1, 1] = finish(f"{l + 1}1", dx)
        dx, dxb, gm = _mixer_bwd(dx, dxb, sm, o, tabs)
        mine_gu[l, 2], mine_dn[l, 2] = finish(f"{l}2", dx)
        grads[l] = _mixer_grads_to_params(gm)
        if l == 0:
            rest_own = jnp.concatenate(
                [_blocks_from_full(jnp.stack([grads[k][n] for k in range(depth)]), n).reshape(N_DEV, rows_of[n], d_model)
                 for n in REST], axis=1)
            sibling_rest, started_rest = _reduce_start([rest_own], [rest_own.astype(BF16)], "rest")
        dx, dxb, d_ffn1, _, _ = _ffn_bwd(dx, dxb, s1, o['ffn1_norm'], wgu[l, 1], wd[l, 1], fp, exchange(f"{l}1"),
                                         after=started_rest[3] if l == 0 else None)
        grads[l].update(post_norm=d_post, ffn2_norm=d_ffn2, ffn1_norm=d_ffn1)
    grad_x = dx.reshape(x.shape)
    part = {n: jnp.stack([grads[l][n] for l in range(depth)]) for n in grads[0]}

    small = jnp.concatenate([part[n].reshape(-1) for n in VECTORS] + [part['conv_w'].reshape(-1)])
    s_rows = -(-small.size // (8 * LANE)) * 8
    small = jnp.pad(small, (0, s_rows * LANE - small.size)).reshape(s_rows, LANE)
    small_all = _all_gather([(small, 0)], "gather_small_grads", in_vmem=True)[0]
    small_sum = _sum_parts([small_all[k * s_rows:(k + 1) * s_rows] for k in range(N_DEV)], "sum_small_grads")

    mine_gu[first_ffn], mine_dn[first_ffn] = finish(f"{first_ffn[0]}{first_ffn[1]}", small_sum)
    mine_rest = _reduce_finish([rest_own], sibling_rest, started_rest, small_sum, "rest")[0]
    grad = {}
    for f in (1, 2):
        grad[f'ffn{f}_w_gate'] = jnp.stack([mine_gu[l, f][:, :fb] for l in range(depth)])
        grad[f'ffn{f}_w_up'] = jnp.stack([mine_gu[l, f][:, fp:fp + fb] for l in range(depth)])
        grad[f'ffn{f}_w_down'] = jnp.stack([mine_dn[l, f][:fb] for l in range(depth)])
    start = 0
    for n in REST:
        grad[n] = mine_rest[start:start + rows_of[n]].reshape(w[n].shape)
        start += rows_of[n]
    small_sum = small_sum.reshape(-1)
    start = 0
    for n in VECTORS:
        grad[n] = small_sum[start:start + w[n].size].reshape(w[n].shape)
        start += w[n].size
    cw_grad = small_sum[start:start + depth * CONV_K * CONV_W].reshape(depth, CONV_K, CONV_W)
    nb = conv_w.shape[-1]
    grad['conv_w'] = lax.dynamic_slice_in_dim(cw_grad, my_block * nb, nb, axis=2)

    delta, new_m, new_v = {}, {}, {}
    for n in BIG + ['conv_w']:
        shp = w[n].shape
        two_d = lambda a: a.reshape(-1, shp[-1])
        dl, mn, vn = _adamw(two_d(w[n]), two_d(grad[n]), two_d(mom[n]), two_d(var[n]), "adamw_" + n)
        delta[n], new_m[n], new_v[n] = dl.reshape(shp), mn.reshape(shp), vn.reshape(shp)
    vcat = lambda src: jnp.concatenate([src[n].reshape(-1) for n in VECTORS]).reshape(-1, LANE)
    dl, mn, vn = _adamw(vcat(w), vcat(grad), vcat(mom), vcat(var), "adamw_vectors")
    start = 0
    for n in VECTORS:
        sl = lambda a: a.reshape(-1)[start:start + w[n].size].reshape(w[n].shape)
        delta[n], new_m[n], new_v[n] = sl(dl), sl(mn), sl(vn)
        start += w[n].size

    return (loss, grad_x, *[grad[n] for n in WEIGHTS], *[delta[n] for n in WEIGHTS],
            *[new_m[n] for n in WEIGHTS], *[new_v[n] for n in WEIGHTS])
```

```python
import functools

import jax
import jax.numpy as jnp
from jax import lax
from jax.experimental import pallas as pl
from jax.experimental.pallas import tpu as pltpu

F32, BF16 = jnp.float32, jnp.bfloat16

N_DEV = 8
N_HEADS = 8
QK_NOPE, QK_ROPE, V_DIM = 64, 32, 64
QK_DIM = QK_NOPE + QK_ROPE
HEAD_PAD = 128
Q_LORA, KV_LORA = 384, 256
LAT = Q_LORA + KV_LORA
CONV_W, CONV_K = 512, 31
CONV_HALO = 32
CHUNK = 64
ROPE_THETA = 10000.0
EPS = 1e-6
ATTN_SCALE = QK_DIM ** -0.5
ATTN_SCALE_LOG2 = ATTN_SCALE * 1.4426950408889634
P_KPE = LAT
P_A = LAT + HEAD_PAD
P_G = P_A + CONV_W
P_COLS = P_G + CONV_W

ADAM_LR, ADAM_B1, ADAM_B2, ADAM_EPS, ADAM_WD, ADAM_STEP = 0.001, 0.9, 0.999, 1e-08, 0.01, 10

V7X_VMEM_BYTES = 64 << 20
VMEM_LIMIT = V7X_VMEM_BYTES - (8 << 20)
MM_VMEM_BUDGET = 36 << 20
LANE = 128
ROW_TILE = 512
ATTN_BLOCK = 512
CONV_TILE = 256

MESH_ID = pl.DeviceIdType.MESH
ANY = pl.BlockSpec(memory_space=pl.ANY)
VMEM_SPEC = pl.BlockSpec(memory_space=pltpu.VMEM)
HBM_SPEC = pl.BlockSpec(memory_space=pltpu.HBM)
SEM_SPEC = pl.BlockSpec(memory_space=pltpu.SEMAPHORE)
DATAFLOW = pltpu.SideEffectType.DATAFLOW_SIDE_EFFECTING

WEIGHTS = ['ffn1_norm', 'ffn1_w_gate', 'ffn1_w_up', 'ffn1_w_down', 'mix_norm', 'w_in', 'q_latent_norm', 'w_uq',
           'kv_latent_norm', 'w_ukv', 'q_norm', 'k_norm', 'conv_w', 'conv_b', 'conv_ln_g', 'conv_ln_b', 'w_out',
           'ffn2_norm', 'ffn2_w_gate', 'ffn2_w_up', 'ffn2_w_down', 'post_norm']
COL_SHARDED = ['ffn1_w_gate', 'ffn1_w_up', 'w_in', 'w_uq', 'w_ukv', 'ffn2_w_gate', 'ffn2_w_up']
ROW_SHARDED = ['ffn1_w_down', 'w_out', 'ffn2_w_down']
REST = ['w_in', 'w_uq', 'w_ukv', 'w_out']
BIG = ['ffn1_w_gate', 'ffn1_w_up', 'ffn1_w_down', 'w_in', 'w_uq', 'w_ukv', 'w_out', 'ffn2_w_gate', 'ffn2_w_up',
       'ffn2_w_down']
VECTORS = ['ffn1_norm', 'mix_norm', 'q_latent_norm', 'kv_latent_norm', 'q_norm', 'k_norm', 'conv_b', 'conv_ln_g',
           'conv_ln_b', 'ffn2_norm', 'post_norm']


def _params(*sem):
    return pltpu.CompilerParams(dimension_semantics=sem if sem else None, vmem_limit_bytes=VMEM_LIMIT)


def _tile(n, cap):
    if n <= cap:
        return n
    best = 0
    for d in range(LANE, cap + 1, LANE):
        if n % d == 0:
            best = d
    assert best, (n, cap)
    return best


def _row_tile(n, cap=ROW_TILE, mult=8):
    if n <= cap:
        return n
    best = 0
    for d in range(mult, cap + 1, mult):
        if n % d == 0:
            best = d
    assert best, (n, cap)
    return best


def _mm(a, b, *, name, ta=False, tb=False, res=None, scale=1.0, out_dtype=F32, tm=None, tn=None, blocks=None,
        after=None):
    (kdim, m) = a.shape if ta else a.shape[::-1]
    (n, kb) = b.shape if tb else b.shape[::-1]
    assert kdim == kb, (a.shape, b.shape, ta, tb)
    tm, tn = tm or _tile(m, 512), tn or _tile(n, 1024)
    if blocks is not None:
        tm, tn = (tm, n // blocks[1]) if blocks[0] == 'col' else (m // blocks[1], tn)
    size = lambda arr: jnp.dtype(arr.dtype).itemsize
    out_bytes = tm * tn * ((6 if blocks is not None else jnp.dtype(out_dtype).itemsize) + (4 if res is not None else 0))

    def vmem_need(tk):
        return 2 * (tm * tk * size(a) + tk * tn * size(b) + out_bytes) + (tm * tn * 4 if tk < kdim else 0)

    tk = kdim
    for cand in [d for d in range(kdim - LANE, 0, -LANE) if kdim % d == 0]:
        if vmem_need(tk) <= MM_VMEM_BUDGET:
            break
        tk = cand
    nk = kdim // tk
    n_in = 2 + (res is not None) + (after is not None)
    n_out = 2 if blocks is not None else 1
    dims = (((0 if ta else 1,), (1 if tb else 0,)), ((), ()))

    def body(*refs):
        a_ref, b_ref = refs[0], refs[1]
        r_ref = refs[2] if res is not None else None
        o_refs = refs[n_in:n_in + n_out]
        acc_ref = refs[-1] if nk > 1 else None
        part = lax.dot_general(a_ref[...].astype(BF16), b_ref[...].astype(BF16), dims, preferred_element_type=F32)

        def finish(acc):
            if scale != 1.0:
                acc = acc * scale
            if r_ref is not None:
                acc = r_ref[...] + acc
            for o_ref in o_refs:
                o_ref[...] = acc.astype(o_ref.dtype)

        if nk == 1:
            finish(part)
        else:
            k = pl.program_id(2)

            @pl.when(k == 0)
            def _():
                acc_ref[...] = part

            @pl.when(k > 0)
            def _():
                acc_ref[...] += part

            @pl.when(k == nk - 1)
            def _():
                finish(acc_ref[...])

    a_spec = pl.BlockSpec((tk, tm), lambda i, j, k: (k, i)) if ta else pl.BlockSpec((tm, tk), lambda i, j, k: (i, k))
    b_spec = pl.BlockSpec((tn, tk), lambda i, j, k: (j, k)) if tb else pl.BlockSpec((tk, tn), lambda i, j, k: (k, j))
    plain = pl.BlockSpec((tm, tn), lambda i, j, k: (i, j))
    if blocks is None:
        out_specs, out_shape = plain, jax.ShapeDtypeStruct((m, n), out_dtype)
    else:
        if blocks[0] == 'col':
            o_spec, shp = pl.BlockSpec((None, tm, tn), lambda i, j, k: (j, i, 0)), (blocks[1], m, tn)
        else:
            o_spec, shp = pl.BlockSpec((None, tm, tn), lambda i, j, k: (i, 0, j)), (blocks[1], tm, n)
        out_specs, out_shape = (o_spec, o_spec), (jax.ShapeDtypeStruct(shp, F32), jax.ShapeDtypeStruct(shp, BF16))
    in_specs = [a_spec, b_spec] + ([plain] if res is not None else [])
    args = (a, b) + ((res,) if res is not None else ())
    if after is not None:
        in_specs.append(pl.BlockSpec(after.shape, lambda i, j, k: (0, 0)))
        args += (after,)
    return pl.pallas_call(
        body, name=name, grid=(m // tm, n // tn, nk), in_specs=in_specs, out_specs=out_specs, out_shape=out_shape,
        scratch_shapes=[pltpu.VMEM((tm, tn), F32)] if nk > 1 else [],
        compiler_params=_params("parallel", "parallel", "arbitrary"),
    )(*args)


def _rms_fwd(x, g, out_dtype, name):
    t, d = x.shape
    tm = _row_tile(t)

    def body(x_ref, g_ref, o_ref):
        xv = x_ref[...]
        r = lax.rsqrt(jnp.mean(xv * xv, axis=-1, keepdims=True) + EPS)
        o_ref[...] = (xv * r * g_ref[...]).astype(o_ref.dtype)

    return pl.pallas_call(
        body, name=name, grid=(t // tm,),
        in_specs=[pl.BlockSpec((tm, d), lambda i: (i, 0)), pl.BlockSpec((1, d), lambda i: (0, 0))],
        out_specs=pl.BlockSpec((tm, d), lambda i: (i, 0)),
        out_shape=jax.ShapeDtypeStruct((t, d), out_dtype), compiler_params=_params("parallel"),
    )(x, g.reshape(1, d))


def _rms_bwd(x, g, dh, res, name):
    t, d = x.shape
    tm = _row_tile(t)

    def body(*refs):
        x_ref, g_ref, dh_ref = refs[:3]
        r_ref = refs[3] if res is not None else None
        dx_ref, dxb_ref, dg_ref = refs[-3:]
        xv, dhv = x_ref[...], dh_ref[...]
        r = lax.rsqrt(jnp.mean(xv * xv, axis=-1, keepdims=True) + EPS)
        y = xv * r
        dy = dhv * g_ref[...]
        dx = r * (dy - y * jnp.mean(dy * y, axis=-1, keepdims=True))
        if r_ref is not None:
            dx = r_ref[...] + dx
        dx_ref[...] = dx
        dxb_ref[...] = dx.astype(BF16)

        @pl.when(pl.program_id(0) == 0)
        def _():
            dg_ref[...] = jnp.zeros_like(dg_ref)

        dg_ref[...] += jnp.sum(dhv * y, axis=0, keepdims=True)

    row = pl.BlockSpec((tm, d), lambda i: (i, 0))
    vec = pl.BlockSpec((1, d), lambda i: (0, 0))
    args = (x, g.reshape(1, d), dh) + ((res,) if res is not None else ())
    dx, dxb, dg = pl.pallas_call(
        body, name=name, grid=(t // tm,), in_specs=[row, vec, row] + ([row] if res is not None else []),
        out_specs=(row, row, vec),
        out_shape=(jax.ShapeDtypeStruct((t, d), F32), jax.ShapeDtypeStruct((t, d), BF16),
                   jax.ShapeDtypeStruct((1, d), F32)),
        compiler_params=_params("arbitrary"),
    )(*args)
    return dx, dxb, dg.reshape(d)


FFN_PAIR = 2


def _ffn_up(h, wgu, fp, name):
    t, d = h.shape
    tm, tn = _tile(t, 512), FFN_PAIR * 2 * fp
    nj = wgu.shape[1] // tn

    def body(h_ref, w_ref, fac_ref, z_ref):
        ab = jnp.dot(h_ref[...], w_ref[...], preferred_element_type=F32)
        for e in range(FFN_PAIR):
            av, bv = ab[:, 2 * fp * e:2 * fp * e + fp], ab[:, 2 * fp * e + fp:2 * fp * (e + 1)]
            s = jax.nn.sigmoid(av)
            silu = av * s
            z_ref[:, fp * e:fp * (e + 1)] = (silu * bv).astype(z_ref.dtype)
            fac_ref[:, 2 * fp * e:2 * fp * e + fp] = silu.astype(fac_ref.dtype)
            fac_ref[:, 2 * fp * e + fp:2 * fp * (e + 1)] = (bv * (s + silu * (1.0 - s))).astype(fac_ref.dtype)

    return pl.pallas_call(
        body, name=name, grid=(nj, t // tm),
        in_specs=[pl.BlockSpec((tm, d), lambda j, i: (i, 0)), pl.BlockSpec((d, tn), lambda j, i: (0, j))],
        out_specs=(pl.BlockSpec((tm, tn), lambda j, i: (i, j)), pl.BlockSpec((tm, tn // 2), lambda j, i: (i, j))),
        out_shape=(jax.ShapeDtypeStruct((t, wgu.shape[1]), BF16), jax.ShapeDtypeStruct((t, wgu.shape[1] // 2), BF16)),
        compiler_params=_params("parallel", "parallel"),
    )(h, wgu)


def _ffn_dab(dyb, wd, ab, fp, name):
    t, d = dyb.shape
    tm, tn = _tile(t, 512), FFN_PAIR * 2 * fp
    nj = ab.shape[1] // tn

    def body(dy_ref, wd_ref, fac_ref, dab_ref):
        dz = _dot_nt(dy_ref[...], wd_ref[...]) * 0.5
        for e in range(FFN_PAIR):
            dze = dz[:, fp * e:fp * (e + 1)]
            d_up = fac_ref[:, 2 * fp * e:2 * fp * e + fp].astype(F32)
            d_gate = fac_ref[:, 2 * fp * e + fp:2 * fp * (e + 1)].astype(F32)
            dab_ref[:, 2 * fp * e:2 * fp * e + fp] = (dze * d_gate).astype(dab_ref.dtype)
            dab_ref[:, 2 * fp * e + fp:2 * fp * (e + 1)] = (dze * d_up).astype(dab_ref.dtype)

    return pl.pallas_call(
        body, name=name, grid=(nj, t // tm),
        in_specs=[pl.BlockSpec((tm, d), lambda j, i: (i, 0)), pl.BlockSpec((tn // 2, d), lambda j, i: (j, 0)),
                  pl.BlockSpec((tm, tn), lambda j, i: (i, j))],
        out_specs=pl.BlockSpec((tm, tn), lambda j, i: (i, j)),
        out_shape=jax.ShapeDtypeStruct(ab.shape, BF16), compiler_params=_params("parallel", "parallel"),
    )(dyb, wd, ab)


def _lat_norm_fwd(p, g_q, g_kv, name):
    t = p.shape[0]
    tm = _row_tile(t)

    def body(p_ref, gq_ref, gkv_ref, q_ref, kv_ref):
        for lo, hi, g_ref, o_ref in ((0, Q_LORA, gq_ref, q_ref), (Q_LORA, LAT, gkv_ref, kv_ref)):
            xv = p_ref[:, lo:hi]
            r = lax.rsqrt(jnp.mean(xv * xv, axis=-1, keepdims=True) + EPS)
            o_ref[...] = (xv * r * g_ref[...]).astype(o_ref.dtype)

    return pl.pallas_call(
        body, name=name, grid=(t // tm,),
        in_specs=[pl.BlockSpec((tm, P_COLS), lambda i: (i, 0)), pl.BlockSpec((1, Q_LORA), lambda i: (0, 0)),
                  pl.BlockSpec((1, KV_LORA), lambda i: (0, 0))],
        out_specs=(pl.BlockSpec((tm, Q_LORA), lambda i: (i, 0)), pl.BlockSpec((tm, KV_LORA), lambda i: (i, 0))),
        out_shape=(jax.ShapeDtypeStruct((t, Q_LORA), BF16), jax.ShapeDtypeStruct((t, KV_LORA), BF16)),
        compiler_params=_params("parallel"),
    )(p, g_q.reshape(1, Q_LORA), g_kv.reshape(1, KV_LORA))


def _lat_norm_bwd(p, g_q, g_kv, dq, dkv, name):
    t = p.shape[0]
    tm = _row_tile(t)

    def body(p_ref, gq_ref, gkv_ref, dq_ref, dkv_ref, dp_ref, dgq_ref, dgkv_ref):
        first = pl.program_id(0) == 0
        for lo, hi, g_ref, d_ref, dg_ref in ((0, Q_LORA, gq_ref, dq_ref, dgq_ref),
                                             (Q_LORA, LAT, gkv_ref, dkv_ref, dgkv_ref)):
            xv, dhv = p_ref[:, lo:hi], d_ref[...]
            r = lax.rsqrt(jnp.mean(xv * xv, axis=-1, keepdims=True) + EPS)
            y = xv * r
            dy = dhv * g_ref[...]
            dp_ref[:, lo:hi] = r * (dy - y * jnp.mean(dy * y, axis=-1, keepdims=True))

            @pl.when(first)
            def _():
                dg_ref[...] = jnp.zeros_like(dg_ref)

            dg_ref[...] += jnp.sum(dhv * y, axis=0, keepdims=True)

    vq = pl.BlockSpec((1, Q_LORA), lambda i: (0, 0))
    vkv = pl.BlockSpec((1, KV_LORA), lambda i: (0, 0))
    dp, dgq, dgkv = pl.pallas_call(
        body, name=name, grid=(t // tm,),
        in_specs=[pl.BlockSpec((tm, P_COLS), lambda i: (i, 0)), vq, vkv,
                  pl.BlockSpec((tm, Q_LORA), lambda i: (i, 0)), pl.BlockSpec((tm, KV_LORA), lambda i: (i, 0))],
        out_specs=(pl.BlockSpec((tm, LAT), lambda i: (i, 0)), vq, vkv),
        out_shape=(jax.ShapeDtypeStruct((t, LAT), F32), jax.ShapeDtypeStruct((1, Q_LORA), F32),
                   jax.ShapeDtypeStruct((1, KV_LORA), F32)),
        compiler_params=_params("arbitrary"),
    )(p, g_q.reshape(1, Q_LORA), g_kv.reshape(1, KV_LORA), dq, dkv)
    return dp, dgq.reshape(Q_LORA), dgkv.reshape(KV_LORA)


def _rope_tables(t):
    half = QK_ROPE // 2
    pos = jnp.arange(t, dtype=F32)
    inv_freq = 1.0 / (ROPE_THETA ** (jnp.arange(0, QK_ROPE, 2, dtype=F32) / QK_ROPE))
    ang = pos[:, None] * inv_freq[None, :]
    cos, sin = jnp.cos(ang), jnp.sin(ang)
    z = lambda n: jnp.zeros((t, n), F32)
    c_tab = jnp.concatenate([jnp.ones((t, QK_NOPE), F32), cos, cos, z(HEAD_PAD - QK_DIM)], axis=1)
    sa_tab = jnp.concatenate([z(QK_NOPE), -sin, z(half), z(HEAD_PAD - QK_DIM)], axis=1)
    sb_tab = jnp.concatenate([z(QK_NOPE), z(half), sin, z(HEAD_PAD - QK_DIM)], axis=1)
    return c_tab, sa_tab, sb_tab


def _rope(x, c, sa, sb):
    half = QK_ROPE // 2
    return x * c + pltpu.roll(x, HEAD_PAD - half, 1) * sa + pltpu.roll(x, half, 1) * sb


def _rope_t(d, c, sa, sb):
    half = QK_ROPE // 2
    return d * c + pltpu.roll(d * sa, half, 1) + pltpu.roll(d * sb, HEAD_PAD - half, 1)


def _head_rms(x):
    r = lax.rsqrt(jnp.sum(x * x, axis=-1, keepdims=True) * (1.0 / QK_DIM) + EPS)
    return x * r, r


def _qk_prep_fwd(q_raw, k_raw, p, gq, gk, tabs, name):
    t, width = q_raw.shape
    tm = _row_tile(t)

    def body(q_ref, k_ref, p_ref, gq_ref, gk_ref, c_ref, sa_ref, sb_ref, qo_ref, ko_ref):
        c, sa, sb, kpe = c_ref[...], sa_ref[...], sb_ref[...], p_ref[...]
        for h in range(N_HEADS):
            cols = slice(h * HEAD_PAD, (h + 1) * HEAD_PAD)
            qn, _ = _head_rms(q_ref[:, cols])
            qo_ref[:, cols] = _rope(qn * gq_ref[...], c, sa, sb).astype(qo_ref.dtype)
            kn, _ = _head_rms(k_ref[:, cols] + kpe)
            ko_ref[:, cols] = _rope(kn * gk_ref[...], c, sa, sb).astype(ko_ref.dtype)

    rows = pl.BlockSpec((tm, width), lambda i: (i, 0))
    tab = pl.BlockSpec((tm, HEAD_PAD), lambda i: (i, 0))
    vec = pl.BlockSpec((1, HEAD_PAD), lambda i: (0, 0))
    kpe_spec = pl.BlockSpec((tm, HEAD_PAD), lambda i: (i, P_KPE // HEAD_PAD))
    return pl.pallas_call(
        body, name=name, grid=(t // tm,), in_specs=[rows, rows, kpe_spec, vec, vec, tab, tab, tab],
        out_specs=(rows, rows),
        out_shape=(jax.ShapeDtypeStruct(q_raw.shape, BF16), jax.ShapeDtypeStruct(k_raw.shape, BF16)),
        compiler_params=_params("parallel"),
    )(q_raw, k_raw, p, gq.reshape(1, HEAD_PAD), gk.reshape(1, HEAD_PAD), *tabs)


def _qk_prep_bwd(q_raw, k_raw, p, dq, dk, gq, gk, tabs, name):
    t, width = q_raw.shape
    tm = _row_tile(t, 256)

    def body(q_ref, k_ref, p_ref, dq_ref, dk_ref, gq_ref, gk_ref, c_ref, sa_ref, sb_ref,
             dqr_ref, dkr_ref, dkpe_ref, dgq_ref, dgk_ref):
        c, sa, sb, kpe = c_ref[...], sa_ref[...], sb_ref[...], p_ref[...]

        def one(x, d, g_ref):
            n, r = _head_rms(x)
            dng = _rope_t(d, c, sa, sb)
            dn = dng * g_ref[...]
            dx = r * (dn - n * (jnp.sum(dn * n, axis=-1, keepdims=True) * (1.0 / QK_DIM)))
            return dx, jnp.sum(dng * n, axis=0, keepdims=True)

        dgq = dgk = dkpe = None
        for h in range(N_HEADS):
            cols = slice(h * HEAD_PAD, (h + 1) * HEAD_PAD)
            dqr, gq_part = one(q_ref[:, cols], dq_ref[:, cols], gq_ref)
            dkr, gk_part = one(k_ref[:, cols] + kpe, dk_ref[:, cols], gk_ref)
            dqr_ref[:, cols] = dqr
            dkr_ref[:, cols] = dkr
            dgq = gq_part if dgq is None else dgq + gq_part
            dgk = gk_part if dgk is None else dgk + gk_part
            dkpe = dkr if dkpe is None else dkpe + dkr
        dkpe_ref[...] = dkpe

        @pl.when(pl.program_id(0) == 0)
        def _():
            dgq_ref[...] = jnp.zeros_like(dgq_ref)
            dgk_ref[...] = jnp.zeros_like(dgk_ref)

        dgq_ref[...] += dgq
        dgk_ref[...] += dgk

    head = pl.BlockSpec((tm, width), lambda i: (i, 0))
    tab = pl.BlockSpec((tm, HEAD_PAD), lambda i: (i, 0))
    vec = pl.BlockSpec((1, HEAD_PAD), lambda i: (0, 0))
    kpe = pl.BlockSpec((tm, HEAD_PAD), lambda i: (i, P_KPE // HEAD_PAD))
    dqr, dkr, dkpe, dgq, dgk = pl.pallas_call(
        body, name=name, grid=(t // tm,), in_specs=[head, head, kpe, head, head, vec, vec, tab, tab, tab],
        out_specs=(head, head, tab, vec, vec),
        out_shape=(jax.ShapeDtypeStruct(q_raw.shape, F32), jax.ShapeDtypeStruct(k_raw.shape, F32),
                   jax.ShapeDtypeStruct((t, HEAD_PAD), F32), jax.ShapeDtypeStruct((1, HEAD_PAD), F32),
                   jax.ShapeDtypeStruct((1, HEAD_PAD), F32)),
        compiler_params=_params("arbitrary"),
    )(q_raw, k_raw, p, dq, dk, gq.reshape(1, HEAD_PAD), gk.reshape(1, HEAD_PAD), *tabs)
    return dqr, dkr, dkpe, dgq.reshape(HEAD_PAD), dgk.reshape(HEAD_PAD)


def _dot_nt(a, b):
    return lax.dot_general(a, b, (((1,), (1,)), ((), ())), preferred_element_type=F32)


def _dot_tn(a, b):
    return lax.dot_general(a, b, (((0,), (0,)), ((), ())), preferred_element_type=F32)


def _diag_mask():
    rows = lax.broadcasted_iota(jnp.int32, (ATTN_BLOCK, ATTN_BLOCK), 0) // CHUNK
    cols = lax.broadcasted_iota(jnp.int32, (ATTN_BLOCK, ATTN_BLOCK), 1) // CHUNK
    return cols <= rows


def _attn_fwd(q, k, v, name):
    t = q.shape[0]
    bq = ATTN_BLOCK
    nq = t // bq

    def body(q_ref, k_ref, v_ref, o_ref, lse_ref):
        i = pl.program_id(1)
        qv = q_ref[...]

        def block(j, carry, masked):
            m, l, acc = carry
            rows = pl.ds(pl.multiple_of(j * bq, bq), bq)
            s = _dot_nt(qv, k_ref[rows, :]) * ATTN_SCALE_LOG2
            if masked:
                s = jnp.where(_diag_mask(), s, -1e30)
            m_new = jnp.maximum(m, jnp.max(s, axis=-1, keepdims=True))
            alpha = jnp.exp2(m - m_new)
            pe = jnp.exp2(s - m_new)
            l = alpha * l + jnp.sum(pe, axis=-1, keepdims=True)
            acc = alpha * acc + jnp.dot(pe.astype(BF16), v_ref[rows, :], preferred_element_type=F32)
            return m_new, l, acc

        init = (jnp.full((bq, 1), -1e30, F32), jnp.zeros((bq, 1), F32), jnp.zeros((bq, HEAD_PAD), F32))
        carry = lax.fori_loop(0, i, lambda j, cr: block(j, cr, False), init)
        m, l, acc = block(i, carry, True)
        o_ref[...] = acc / l
        lse_ref[...] = jnp.broadcast_to(m + jnp.log2(l), (bq, HEAD_PAD))

    blk = pl.BlockSpec((bq, HEAD_PAD), lambda h, i: (i, h))
    full = pl.BlockSpec((t, HEAD_PAD), lambda h, i: (0, h))
    return pl.pallas_call(
        body, name=name, grid=(N_HEADS, nq), in_specs=[blk, full, full], out_specs=(blk, blk),
        out_shape=(jax.ShapeDtypeStruct(q.shape, F32), jax.ShapeDtypeStruct(q.shape, F32)),
        compiler_params=_params("parallel", "parallel"),
    )(q, k, v)


def _attn_bwd(q, k, v, o, lse, do, name):
    t = q.shape[0]
    bq = ATTN_BLOCK
    nq = t // bq

    def body(q_ref, k_ref, v_ref, o_ref, lse_ref, do_ref, dq_ref, dk_ref, dv_ref, delta_ref):
        def rows_of(i):
            return pl.ds(pl.multiple_of(i * bq, bq), bq)

        def prep(i, _):
            r = rows_of(i)
            delta_ref[r, :] = jnp.broadcast_to(jnp.sum(do_ref[r, :] * o_ref[r, :], axis=-1, keepdims=True),
                                               (bq, HEAD_PAD))
            dq_ref[r, :] = jnp.zeros((bq, HEAD_PAD), F32)
            return 0

        lax.fori_loop(0, nq, prep, 0)

        def key_block(j, _):
            rj = rows_of(j)
            kb, vb = k_ref[rj, :], v_ref[rj, :]

            def query_block(i, carry, masked):
                dk, dv = carry
                ri = rows_of(i)
                qb, dob = q_ref[ri, :], do_ref[ri, :].astype(BF16)
                s = _dot_nt(qb, kb) * ATTN_SCALE_LOG2
                if masked:
                    s = jnp.where(_diag_mask(), s, -1e30)
                pe = jnp.exp2(s - lse_ref[ri, :][:, :1])
                dp = _dot_nt(dob, vb)
                ds = (pe * (dp - delta_ref[ri, :][:, :1]) * ATTN_SCALE).astype(BF16)
                dq_ref[ri, :] += jnp.dot(ds, kb, preferred_element_type=F32)
                return dk + _dot_tn(ds, qb), dv + _dot_tn(pe.astype(BF16), dob)

            zero = jnp.zeros((bq, HEAD_PAD), F32)
            carry = query_block(j, (zero, zero), True)
            dk, dv = lax.fori_loop(j + 1, nq, lambda i, cr: query_block(i, cr, False), carry)
            dk_ref[rj, :] = dk
            dv_ref[rj, :] = dv
            return 0

        lax.fori_loop(0, nq, key_block, 0)

    full = pl.BlockSpec((t, HEAD_PAD), lambda h: (0, h))
    shp = jax.ShapeDtypeStruct(q.shape, F32)
    return pl.pallas_call(
        body, name=name, grid=(N_HEADS,), in_specs=[full] * 6, out_specs=(full, full, full),
        out_shape=(shp, shp, shp), scratch_shapes=[pltpu.VMEM((t, HEAD_PAD), F32)],
        compiler_params=_params("parallel"),
    )(q, k, v, o, lse, do)


def _glu_ext(pc_ref, pp_ref, u_ref, tm, first):
    u_ref[CONV_HALO:CONV_HALO + tm, :] = pc_ref[:, P_A:P_G] * jax.nn.sigmoid(pc_ref[:, P_G:P_COLS])
    up = pp_ref[tm - CONV_HALO:tm, P_A:P_G] * jax.nn.sigmoid(pp_ref[tm - CONV_HALO:tm, P_G:P_COLS])
    u_ref[0:CONV_HALO, :] = jnp.where(first, 0.0, up)


SUBLANES = 8


def _shift_copies(src_ref, sh_ref):
    rows = sh_ref.shape[1]
    for b in range(1, SUBLANES):
        sh_ref[b - 1, :, :] = src_ref[b:b + rows, :]


def _rows_at(src_ref, sh_ref, start, n):
    a, b = divmod(start, SUBLANES)
    if b == 0:
        return src_ref[SUBLANES * a:SUBLANES * a + n, :]
    return sh_ref[b - 1, SUBLANES * a:SUBLANES * a + n, :]


def _conv_fwd(p, w, b, ln_g, ln_b, name):
    t = p.shape[0]
    tm = _row_tile(t, CONV_TILE)
    off = CONV_HALO - (CONV_K - 1)

    def body(pc_ref, pp_ref, w_ref, b_ref, g_ref, bb_ref, y_ref, o_ref, u_ref, ush_ref):
        _glu_ext(pc_ref, pp_ref, u_ref, tm, pl.program_id(0) == 0)
        _shift_copies(u_ref, ush_ref)
        acc = jnp.zeros((tm, CONV_W), F32)
        for kk in range(CONV_K):
            acc = acc + w_ref[kk:kk + 1, :] * _rows_at(u_ref, ush_ref, off + kk, tm)
        y = acc + b_ref[...]
        y_ref[...] = y
        xc = y - jnp.mean(y, axis=-1, keepdims=True)
        lo = xc * lax.rsqrt(jnp.mean(xc * xc, axis=-1, keepdims=True) + EPS) * g_ref[...] + bb_ref[...]
        o_ref[...] = (lo * jax.nn.sigmoid(lo)).astype(o_ref.dtype)

    prow = pl.BlockSpec((tm, P_COLS), lambda i: (i, 0))
    pprev = pl.BlockSpec((tm, P_COLS), lambda i: (jnp.maximum(i - 1, 0), 0))
    vec = pl.BlockSpec((1, CONV_W), lambda i: (0, 0))
    row = pl.BlockSpec((tm, CONV_W), lambda i: (i, 0))
    return pl.pallas_call(
        body, name=name, grid=(t // tm,),
        in_specs=[prow, pprev, pl.BlockSpec((CONV_HALO, CONV_W), lambda i: (0, 0)), vec, vec, vec],
        out_specs=(row, row),
        out_shape=(jax.ShapeDtypeStruct((t, CONV_W), F32), jax.ShapeDtypeStruct((t, CONV_W), BF16)),
        scratch_shapes=[pltpu.VMEM((tm + CONV_HALO, CONV_W), F32),
                        pltpu.VMEM((SUBLANES - 1, tm + CONV_HALO - SUBLANES, CONV_W), F32)],
        compiler_params=_params("parallel"),
    )(p, p, w, b.reshape(1, CONV_W), ln_g.reshape(1, CONV_W), ln_b.reshape(1, CONV_W))


def _conv_bwd_ln(y, dout, ln_g, ln_b, name):
    t = y.shape[0]
    tm = _row_tile(t)

    def body(y_ref, d_ref, g_ref, bb_ref, dy_ref, dg_ref, db_ref, dcb_ref):
        yv = y_ref[...]
        xc = yv - jnp.mean(yv, axis=-1, keepdims=True)
        r = lax.rsqrt(jnp.mean(xc * xc, axis=-1, keepdims=True) + EPS)
        n = xc * r
        lo = n * g_ref[...] + bb_ref[...]
        s = jax.nn.sigmoid(lo)
        dlo = d_ref[...] * (s * (1.0 + lo * (1.0 - s)))
        dn = dlo * g_ref[...]
        dy = r * (dn - jnp.mean(dn, axis=-1, keepdims=True) - n * jnp.mean(dn * n, axis=-1, keepdims=True))
        dy_ref[...] = dy

        @pl.when(pl.program_id(0) == 0)
        def _():
            dg_ref[...] = jnp.zeros_like(dg_ref)
            db_ref[...] = jnp.zeros_like(db_ref)
            dcb_ref[...] = jnp.zeros_like(dcb_ref)

        dg_ref[...] += jnp.sum(dlo * n, axis=0, keepdims=True)
        db_ref[...] += jnp.sum(dlo, axis=0, keepdims=True)
        dcb_ref[...] += jnp.sum(dy, axis=0, keepdims=True)

    row = pl.BlockSpec((tm, CONV_W), lambda i: (i, 0))
    vec = pl.BlockSpec((1, CONV_W), lambda i: (0, 0))
    vshape = jax.ShapeDtypeStruct((1, CONV_W), F32)
    dy, dg, db, dcb = pl.pallas_call(
        body, name=name, grid=(t // tm,), in_specs=[row, row, vec, vec], out_specs=(row, vec, vec, vec),
        out_shape=(jax.ShapeDtypeStruct((t, CONV_W), F32), vshape, vshape, vshape),
        compiler_params=_params("arbitrary"),
    )(y, dout, ln_g.reshape(1, CONV_W), ln_b.reshape(1, CONV_W))
    return dy, dg.reshape(CONV_W), db.reshape(CONV_W), dcb.reshape(CONV_W)


def _conv_bwd_taps(p, dy, w, name):
    t = p.shape[0]
    tm = _row_tile(t, CONV_TILE)
    nt = t // tm
    off = CONV_HALO - (CONV_K - 1)

    def body(pc_ref, pp_ref, dyc_ref, dyn_ref, w_ref, dag_ref, dw_ref, u_ref, dye_ref, ush_ref, dysh_ref):
        i = pl.program_id(0)
        _glu_ext(pc_ref, pp_ref, u_ref, tm, i == 0)
        dyc = dyc_ref[...]
        dye_ref[0:tm, :] = dyc
        dye_ref[tm:tm + CONV_HALO, :] = jnp.where(i == nt - 1, 0.0, dyn_ref[0:CONV_HALO, :])

        _shift_copies(u_ref, ush_ref)
        _shift_copies(dye_ref, dysh_ref)

        @pl.when(i == 0)
        def _():
            dw_ref[...] = jnp.zeros_like(dw_ref)

        du = jnp.zeros((tm, CONV_W), F32)
        for kk in range(CONV_K):
            dw_ref[kk:kk + 1, :] += jnp.sum(dyc * _rows_at(u_ref, ush_ref, off + kk, tm), axis=0, keepdims=True)
            du = du + w_ref[kk:kk + 1, :] * _rows_at(dye_ref, dysh_ref, CONV_K - 1 - kk, tm)
        av, gv = pc_ref[:, P_A:P_G], pc_ref[:, P_G:P_COLS]
        s = jax.nn.sigmoid(gv)
        dag_ref[:, 0:CONV_W] = du * s
        dag_ref[:, CONV_W:2 * CONV_W] = du * av * (s * (1.0 - s))

    prow = pl.BlockSpec((tm, P_COLS), lambda i: (i, 0))
    pprev = pl.BlockSpec((tm, P_COLS), lambda i: (jnp.maximum(i - 1, 0), 0))
    row = pl.BlockSpec((tm, CONV_W), lambda i: (i, 0))
    nxt = pl.BlockSpec((tm, CONV_W), lambda i: (jnp.minimum(i + 1, nt - 1), 0))
    wspec = pl.BlockSpec((CONV_HALO, CONV_W), lambda i: (0, 0))
    return pl.pallas_call(
        body, name=name, grid=(nt,), in_specs=[prow, pprev, row, nxt, wspec],
        out_specs=(pl.BlockSpec((tm, 2 * CONV_W), lambda i: (i, 0)), wspec),
        out_shape=(jax.ShapeDtypeStruct((t, 2 * CONV_W), F32), jax.ShapeDtypeStruct((CONV_HALO, CONV_W), F32)),
        scratch_shapes=[pltpu.VMEM((tm + CONV_HALO, CONV_W), F32), pltpu.VMEM((tm + CONV_HALO, CONV_W), F32),
                        pltpu.VMEM((SUBLANES - 1, tm + CONV_HALO - SUBLANES, CONV_W), F32),
                        pltpu.VMEM((SUBLANES - 1, tm + CONV_HALO - SUBLANES, CONV_W), F32)],
        compiler_params=_params("arbitrary"),
    )(p, p, dy, dy, w)


def _loss_head(y, target, name):
    t, d = y.shape
    tm = _row_tile(t)

    def body(y_ref, t_ref, l_ref, dy_ref):
        err = y_ref[...] - t_ref[...]
        dy_ref[...] = err * (1.0 / d)

        @pl.when(pl.program_id(0) == 0)
        def _():
            l_ref[...] = jnp.zeros_like(l_ref)

        row = jnp.sum(err * err, axis=-1, keepdims=True) * (0.5 / d)
        l_ref[...] += jnp.broadcast_to(jnp.sum(row, axis=0, keepdims=True), (1, LANE))

    row = pl.BlockSpec((tm, d), lambda i: (i, 0))
    return pl.pallas_call(
        body, name=name, grid=(t // tm,), in_specs=[row, row],
        out_specs=(pl.BlockSpec((1, LANE), lambda i: (0, 0)), row),
        out_shape=(jax.ShapeDtypeStruct((1, LANE), F32), jax.ShapeDtypeStruct((t, d), F32)),
        compiler_params=_params("arbitrary"),
    )(y, target)


def _adamw(w, g, m, v, name):
    r, c = w.shape
    tr = _row_tile(r, 256)
    c1, c2 = 1.0 - ADAM_B1 ** ADAM_STEP, 1.0 - ADAM_B2 ** ADAM_STEP

    def body(w_ref, g_ref, m_ref, v_ref, d_ref, mo_ref, vo_ref):
        gv = g_ref[...]
        mn = ADAM_B1 * m_ref[...] + (1.0 - ADAM_B1) * gv
        vn = ADAM_B2 * v_ref[...] + (1.0 - ADAM_B2) * (gv * gv)
        mo_ref[...] = mn
        vo_ref[...] = vn
        d_ref[...] = -ADAM_LR * ((mn / c1) / (jnp.sqrt(vn / c2) + ADAM_EPS) + ADAM_WD * w_ref[...])

    blk = pl.BlockSpec((tr, c), lambda i: (i, 0))
    shp = jax.ShapeDtypeStruct((r, c), F32)
    return pl.pallas_call(
        body, name=name, grid=(r // tr,), in_specs=[blk] * 4, out_specs=(blk, blk, blk), out_shape=(shp, shp, shp),
        compiler_params=_params("parallel"),
    )(w, g, m, v)


def _sum_parts(parts, name):
    r, c = parts[0].shape
    tr = _row_tile(r, 256)

    def body(*refs):
        acc = refs[0][...]
        for ref in refs[1:-1]:
            acc = acc + ref[...]
        refs[-1][...] = acc

    blk = pl.BlockSpec((tr, c), lambda i: (i, 0))
    return pl.pallas_call(
        body, name=name, grid=(r // tr,), in_specs=[blk] * len(parts), out_specs=blk,
        out_shape=jax.ShapeDtypeStruct((r, c), F32), compiler_params=_params("parallel"),
    )(*parts)


def _place():
    return lax.axis_index("x"), lax.axis_index("y"), lax.axis_index("c")


def _window(ref, block, size, axis):
    start = pl.multiple_of(block * size, LANE if size % LANE == 0 else 8)
    return ref.at[(slice(None),) * axis + (pl.ds(start, size),)]


def _all_gather(pieces, name, in_vmem=False, seeds=()):
    n_p, pieces = len(pieces), list(pieces) + list(seeds)
    n_all = len(pieces)

    def body(*refs):
        x_refs, out_refs = refs[:n_all], refs[n_all:2 * n_all]
        send_sems, recv_sems, local_sems = refs[2 * n_all:]
        px, py, pc = _place()
        me, sibling = (px, py, pc), (px, py, 1 - pc)
        chips = [(1 - px, py), (px, 1 - py), (1 - px, 1 - py)]

        def win(p, block):
            bx, by, bc = block
            x, axis = pieces[p]
            return _window(out_refs[p], 4 * bx + 2 * by + bc, x.shape[axis], axis)

        def copy(k, p, block, to, local=False):
            return pltpu.make_async_remote_copy(
                src_ref=x_refs[p] if local else win(p, block), dst_ref=win(p, block),
                send_sem=send_sems.at[k, p], recv_sem=recv_sems.at[k, p], device_id=to, device_id_type=MESH_ID)

        every = range(n_p)
        mine = [pltpu.make_async_copy(x_refs[p], win(p, me), local_sems.at[p]) for p in range(n_all)]
        first = [copy(0, p, me, sibling, local=True) for p in every]
        first += [copy(1 + j, p, me, (*chip, pc), local=True) for j, chip in enumerate(chips) for p in every]
        for cp in mine + first:
            cp.start()
        passed = []
        for j, chip in enumerate(chips):
            for p in every:
                copy(1 + j, p, (*chip, pc), me).wait_recv()
                passed.append(copy(4 + j, p, (*chip, pc), sibling))
                passed[-1].start()
        for p in every:
            copy(0, p, sibling, me).wait_recv()
        for j, chip in enumerate(chips):
            for p in every:
                copy(4 + j, p, (*chip, 1 - pc), me).wait_recv()
        for cp in first + passed:
            cp.wait_send()
        for cp in mine:
            cp.wait()

    def gathered(x, axis):
        return jax.ShapeDtypeStruct(x.shape[:axis] + (N_DEV * x.shape[axis],) + x.shape[axis + 1:], x.dtype)

    spec = VMEM_SPEC if in_vmem else ANY
    return pl.pallas_call(
        body, name=name, in_specs=[spec] * n_all, out_specs=[spec] * n_all,
        out_shape=[gathered(*pc_) for pc_ in pieces],
        scratch_shapes=[pltpu.SemaphoreType.DMA((7, n_p)), pltpu.SemaphoreType.DMA((7, n_p)),
                        pltpu.SemaphoreType.DMA((n_all,))],
        compiler_params=pltpu.CompilerParams(vmem_limit_bytes=VMEM_LIMIT),
    )(*[x for x, _ in pieces])


def _start_copies(bufs, n_copies, plan, name, after=None):
    nb = len(bufs)
    n_in = nb + (after is not None)

    def body(*refs):
        send_sems, recv_sems, token = refs[n_in], refs[n_in + 1], refs[-1]
        for i, (src, dst, dev) in enumerate(plan(refs[:nb])):
            pltpu.make_async_remote_copy(src_ref=src, dst_ref=dst, send_sem=send_sems.at[i], recv_sem=recv_sems.at[i],
                                         device_id=dev, device_id_type=MESH_ID).start()
        token[...] = jnp.zeros_like(token)

    out = pl.pallas_call(
        body, name=name, in_specs=[HBM_SPEC] * nb + [ANY] * (after is not None),
        out_shape=(pltpu.SemaphoreType.DMA((n_copies,)), pltpu.SemaphoreType.DMA((n_copies,)),
                   *[pltpu.HBM(b.shape, b.dtype) for b in bufs], jax.ShapeDtypeStruct((8, LANE), F32)),
        out_specs=(SEM_SPEC, SEM_SPEC, *[HBM_SPEC] * nb, VMEM_SPEC),
        input_output_aliases={i: 2 + i for i in range(nb)},
        compiler_params=pltpu.CompilerParams(has_side_effects=DATAFLOW),
    )(*[pltpu.with_memory_space_constraint(b, pltpu.HBM) for b in bufs], *([after] if after is not None else []))
    return out[0], out[1], list(out[2:2 + nb]), out[-1]


def _wait_copies(started, after, n_copies, plan, name):
    send_sems, recv_sems, bufs, _ = started
    nb = len(bufs)

    def body(*refs):
        send_ref, recv_ref = refs[nb], refs[nb + 1]
        copies = [pltpu.make_async_remote_copy(src_ref=src, dst_ref=dst, send_sem=send_ref.at[i], recv_sem=recv_ref.at[i],
                                               device_id=dev, device_id_type=MESH_ID)
                  for i, (src, dst, dev) in enumerate(plan(refs[:nb]))]
        for cp in copies:
            cp.wait_send()
        for cp in copies:
            cp.wait_recv()

    out = pl.pallas_call(
        body, name=name, in_specs=[HBM_SPEC] * nb + [SEM_SPEC, SEM_SPEC, ANY],
        out_shape=tuple(pltpu.HBM(b.shape, b.dtype) for b in bufs), out_specs=tuple([HBM_SPEC] * nb),
        input_output_aliases={i: i for i in range(nb)},
        compiler_params=pltpu.CompilerParams(has_side_effects=DATAFLOW),
    )(*bufs, send_sems, recv_sems, after)
    return list(out)


def _after(x, token):
    return x + token[0, 0].astype(x.dtype)


def _other_chips():
    px, py, _ = _place()
    return [(1 - px, py), (px, 1 - py), (1 - px, 1 - py)]


def _exchange(srcs, slots, src_block, target, name):
    n_p = len(srcs)

    def body(*refs):
        src_refs, out_refs, send_sems, recv_sems = refs[:n_p], refs[n_p:2 * n_p], refs[-2], refs[-1]
        copies = [pltpu.make_async_remote_copy(
            src_ref=src_refs[p].at[src_block(s)], dst_ref=out_refs[p].at[s], send_sem=send_sems.at[s, p],
            recv_sem=recv_sems.at[s, p], device_id=target(s), device_id_type=MESH_ID)
            for s in range(slots) for p in range(n_p)]
        for cp in copies:
            cp.start()
        for cp in copies:
            cp.wait_recv()
        for cp in copies:
            cp.wait_send()

    return pl.pallas_call(
        body, name=name, in_specs=[ANY] * n_p, out_specs=[ANY] * n_p,
        out_shape=[jax.ShapeDtypeStruct((slots,) + a.shape[1:], a.dtype) for a in srcs],
        scratch_shapes=[pltpu.SemaphoreType.DMA((slots, n_p)), pltpu.SemaphoreType.DMA((slots, n_p))],
        compiler_params=pltpu.CompilerParams(vmem_limit_bytes=VMEM_LIMIT),
    )(*srcs)


def _blocks_to_sibling(sends, name):
    def src_block(j):
        return 2 * j + 1 - lax.axis_index("c")

    def target(j):
        px, py, pc = _place()
        return (px, py, 1 - pc)

    return _exchange(sends, 4, src_block, target, name)


def _pair_sums_for_chips(own, got, name):
    _, r, c = own.shape
    tr = _row_tile(r, 256, 16)

    def body(idx_ref, own_ref, got_ref, o_ref):
        o_ref[...] = (own_ref[...] + got_ref[...].astype(F32)).astype(o_ref.dtype)

    grid_spec = pltpu.PrefetchScalarGridSpec(
        num_scalar_prefetch=1, grid=(3, r // tr),
        in_specs=[pl.BlockSpec((None, tr, c), lambda k, i, idx: (idx[k], i, 0)),
                  pl.BlockSpec((None, tr, c), lambda k, i, idx: (idx[3 + k], i, 0))],
        out_specs=pl.BlockSpec((None, tr, c), lambda k, i, idx: (k, i, 0)))
    chips = [2 * cx + cy for cx, cy in _other_chips()]
    idx = jnp.stack([2 * j + lax.axis_index("c") for j in chips] + chips).astype(jnp.int32)
    return pl.pallas_call(
        body, name=name, grid_spec=grid_spec, out_shape=jax.ShapeDtypeStruct((3, r, c), BF16),
        compiler_params=_params("parallel", "parallel"),
    )(idx, own, got)


def _sum_for_me(own, got_sibling, got_chips, name):
    _, r, c = own.shape
    tr = _row_tile(r, 256, 16)

    def body(idx_ref, own_ref, sib_ref, g0_ref, g1_ref, g2_ref, o_ref):
        acc = own_ref[...] + sib_ref[...].astype(F32)
        for ref in (g0_ref, g1_ref, g2_ref):
            acc = acc + ref[...].astype(F32)
        o_ref[...] = acc

    def part(k):
        return pl.BlockSpec((None, tr, c), lambda i, idx: (k, i, 0))

    grid_spec = pltpu.PrefetchScalarGridSpec(
        num_scalar_prefetch=1, grid=(r // tr,),
        in_specs=[pl.BlockSpec((None, tr, c), lambda i, idx: (idx[0], i, 0)),
                  pl.BlockSpec((None, tr, c), lambda i, idx: (idx[1], i, 0)), part(0), part(1), part(2)],
        out_specs=pl.BlockSpec((tr, c), lambda i, idx: (i, 0)))
    px, py, pc = _place()
    idx = jnp.stack([4 * px + 2 * py + pc, 2 * px + py]).astype(jnp.int32)
    return pl.pallas_call(
        body, name=name, grid_spec=grid_spec, out_shape=jax.ShapeDtypeStruct((r, c), F32),
        compiler_params=_params("parallel"),
    )(idx, own, got_sibling, got_chips, got_chips, got_chips)


def _chip_plan(n_p):
    def plan(refs):
        pc = lax.axis_index("c")
        return [(refs[p].at[k], refs[n_p + p].at[k], (cx, cy, pc))
                for p in range(n_p) for k, (cx, cy) in enumerate(_other_chips())]
    return plan


def _reduce_start(own, sends, tag):
    from_sibling = _blocks_to_sibling(sends, "grads_to_sibling_" + tag)
    pair_sums = [_pair_sums_for_chips(a, b, "grads_pair_sums") for a, b in zip(own, from_sibling)]
    lands = [lax.empty(a.shape, a.dtype) for a in pair_sums]
    started = _start_copies(pair_sums + lands, 3 * len(own), _chip_plan(len(own)), "grads_to_chips_start_" + tag)
    return from_sibling, started


def _reduce_finish(own, from_sibling, started, after, tag):
    n_p = len(own)
    bufs = _wait_copies(started, after, 3 * n_p, _chip_plan(n_p), "grads_to_chips_wait_" + tag)
    return [_sum_for_me(a, b, c, "grads_sum") for a, b, c in zip(own, from_sibling, bufs[n_p:])]


def _gather_plans(pieces):
    n_p = len(pieces)
    dims = [(x.shape[axis], axis) for x, axis in pieces]

    def first(refs):
        px, py, pc = _place()
        targets = [(px, py, 1 - pc)] + [(cx, cy, pc) for cx, cy in _other_chips()]
        return [(refs[p], _window(refs[n_p + p], 4 * px + 2 * py + pc, *dims[p]), to)
                for p in range(n_p) for to in targets]

    def second(refs):
        px, py, pc = _place()
        out = []
        for p in range(n_p):
            for cx, cy in _other_chips():
                win = _window(refs[p], 4 * cx + 2 * cy + pc, *dims[p])
                out.append((win, win, (px, py, 1 - pc)))
        return out

    return first, second


def _flat_rows(a, width):
    return a.reshape(-1, width)


def _full_from_blocks(blocks, name):
    if name in COL_SHARDED:
        _, l, k, nb = blocks.shape
        return jnp.transpose(blocks, (1, 2, 0, 3)).reshape(l, k, N_DEV * nb)
    _, l, rb, n = blocks.shape
    return jnp.transpose(blocks, (1, 0, 2, 3)).reshape(l, N_DEV * rb, n)


def _blocks_from_full(full, name):
    if name in COL_SHARDED:
        l, k, n = full.shape
        return jnp.transpose(full.reshape(l, k, N_DEV, n // N_DEV), (2, 0, 1, 3))
    l, rows, n = full.shape
    return jnp.transpose(full.reshape(l, N_DEV, rows // N_DEV, n), (1, 0, 2, 3))


def _pad_heads(w, width):
    k = w.shape[0]
    return jnp.pad(w.reshape(k, N_HEADS, width), ((0, 0), (0, 0), (0, HEAD_PAD - width))).reshape(k, N_HEADS * HEAD_PAD)


def _unpad_heads(w, width):
    k = w.shape[0]
    return w.reshape(k, N_HEADS, HEAD_PAD)[:, :, :width].reshape(k, N_HEADS * width)


def _layer_operands(full, vec, conv_w_full, l):
    w_in = full['w_in'][l]
    kpe = jnp.pad(w_in[:, LAT:LAT + QK_ROPE], ((0, 0), (QK_NOPE, HEAD_PAD - QK_DIM)))
    w_ukv = full['w_ukv'][l].reshape(KV_LORA, N_HEADS, QK_NOPE + V_DIM)
    w_out = full['w_out'][l]
    d_model = w_out.shape[1]
    wo_attn = jnp.pad(w_out[:N_HEADS * V_DIM].reshape(N_HEADS, V_DIM, d_model),
                      ((0, 0), (0, HEAD_PAD - V_DIM), (0, 0))).reshape(N_HEADS * HEAD_PAD, d_model)
    ops = {
        'w_in': jnp.concatenate([w_in[:, :LAT], kpe, w_in[:, LAT + QK_ROPE:]], axis=1),
        'w_q': _pad_heads(full['w_uq'][l], QK_DIM),
        'w_k': _pad_heads(w_ukv[:, :, :QK_NOPE].reshape(KV_LORA, N_HEADS * QK_NOPE), QK_NOPE),
        'w_v': _pad_heads(w_ukv[:, :, QK_NOPE:].reshape(KV_LORA, N_HEADS * V_DIM), V_DIM),
        'wo_attn': wo_attn,
        'wo_conv': w_out[N_HEADS * V_DIM:],
        'conv_w': jnp.pad(conv_w_full[l], ((0, CONV_HALO - CONV_K), (0, 0))),
        'gq': jnp.pad(vec['q_norm'][l], (0, HEAD_PAD - QK_DIM)),
        'gk': jnp.pad(vec['k_norm'][l], (0, HEAD_PAD - QK_DIM)),
    }
    for n in ('ffn1_norm', 'mix_norm', 'q_latent_norm', 'kv_latent_norm', 'conv_b', 'conv_ln_g', 'conv_ln_b',
              'ffn2_norm', 'post_norm'):
        ops[n] = vec[n][l]
    return ops


def _ffn_fwd(x, g, wgu, wd, fp):
    h = _rms_fwd(x, g, BF16, "rms_fwd_ffn")
    ab, z = _ffn_up(h, wgu, fp, "ffn_up")
    y = _mm(z, wd, res=x, scale=0.5, name="ffn_down")
    return y, (x, h, ab, z)


def _ffn_bwd(dy, dyb, saved, g, wgu, wd, fp, after_dw=None, after=None):
    x, h, ab, z = saved
    d_wd = _mm(z, dyb, ta=True, scale=0.5, blocks=('row', N_DEV), after=after, name="ffn_dwd")
    dab = _ffn_dab(dyb, wd, ab, fp, "ffn_dab")
    d_wgu = _mm(h, dab, ta=True, blocks=('col', N_DEV), name="ffn_dwgu")
    token = after_dw(d_wgu, d_wd) if after_dw is not None else None
    dh = _mm(dab, wgu, tb=True, after=token, name="ffn_dh")
    dx, dxb, dg = _rms_bwd(x, g, dh, dy, "rms_bwd_ffn")
    return dx, dxb, dg, d_wgu, d_wd


def _mixer_fwd(x, ops, tabs, after_attention=None):
    h = _rms_fwd(x, ops['mix_norm'], BF16, "rms_fwd_mix")
    p = _mm(h, ops['w_in'], name="mix_in")
    qln, kvln = _lat_norm_fwd(p, ops['q_latent_norm'], ops['kv_latent_norm'], "lat_norm_fwd")
    q_raw = _mm(qln, ops['w_q'], name="mix_q")
    k_raw = _mm(kvln, ops['w_k'], name="mix_k")
    v = _mm(kvln, ops['w_v'], out_dtype=BF16, name="mix_v")
    q, k = _qk_prep_fwd(q_raw, k_raw, p, ops['gq'], ops['gk'], tabs, "qk_prep_fwd")
    o, lse = _attn_fwd(q, k, v, "attn_fwd")
    token = after_attention(o) if after_attention is not None else None
    conv_b = ops['conv_b'] if token is None else _after(ops['conv_b'], token)
    y_conv, cv = _conv_fwd(p, ops['conv_w'], conv_b, ops['conv_ln_g'], ops['conv_ln_b'], "conv_fwd")
    x_attn = _mm(o, ops['wo_attn'], res=x, name="mix_out_attn")
    x_out = _mm(cv, ops['wo_conv'], res=x_attn, name="mix_out_conv")
    return x_out, (x, h, p, qln, kvln, q_raw, k_raw, v, q, k, o, lse, y_conv, cv)


def _mixer_bwd(dx_out, dxb_out, saved, ops, tabs, token=None):
    x, h, p, qln, kvln, q_raw, k_raw, v, q, k, o, lse, y_conv, cv = saved
    g = {}
    do = _mm(dxb_out, ops['wo_attn'], tb=True, after=token, name="mix_do")
    dcv = _mm(dxb_out, ops['wo_conv'], tb=True, name="mix_dcv")
    g['wo_attn'] = _mm(o, dxb_out, ta=True, name="mix_dwo_attn")
    g['wo_conv'] = _mm(cv, dxb_out, ta=True, name="mix_dwo_conv")
    dq, dk, dv = _attn_bwd(q, k, v, o, lse, do, "attn_bwd")
    dq_raw, dk_raw, dkpe, g['gq'], g['gk'] = _qk_prep_bwd(q_raw, k_raw, p, dq, dk, ops['gq'], ops['gk'], tabs,
                                                          "qk_prep_bwd")
    g['w_q'] = _mm(qln, dq_raw, ta=True, name="mix_dwq")
    g['w_k'] = _mm(kvln, dk_raw, ta=True, name="mix_dwk")
    g['w_v'] = _mm(kvln, dv, ta=True, name="mix_dwv")
    dqln = _mm(dq_raw, ops['w_q'], tb=True, name="mix_dqln")
    dkvln = _mm(dk_raw, ops['w_k'], tb=True, name="mix_dkvln_k")
    dkvln = _mm(dv, ops['w_v'], tb=True, res=dkvln, name="mix_dkvln_v")
    dp_lat, g['q_latent_norm'], g['kv_latent_norm'] = _lat_norm_bwd(
        p, ops['q_latent_norm'], ops['kv_latent_norm'], dqln, dkvln, "lat_norm_bwd")
    dy_conv, g['conv_ln_g'], g['conv_ln_b'], g['conv_b'] = _conv_bwd_ln(
        y_conv, dcv, ops['conv_ln_g'], ops['conv_ln_b'], "conv_bwd_ln")
    dag, g['conv_w'] = _conv_bwd_taps(p, dy_conv, ops['conv_w'], "conv_bwd_taps")
    dp = jnp.concatenate([dp_lat, dkpe, dag], axis=1)
    g['w_in'] = _mm(h, dp, ta=True, name="mix_dw_in")
    dh = _mm(dp, ops['w_in'], tb=True, name="mix_dh")
    dx, dxb, g['mix_norm'] = _rms_bwd(x, ops['mix_norm'], dh, dx_out, "rms_bwd_mix")
    return dx, dxb, g


def _mixer_grads_to_params(g):
    d_w_in = g['w_in']
    d_wk = _unpad_heads(g['w_k'], QK_NOPE).reshape(KV_LORA, N_HEADS, QK_NOPE)
    d_wv = _unpad_heads(g['w_v'], V_DIM).reshape(KV_LORA, N_HEADS, V_DIM)
    d_model = g['wo_attn'].shape[1]
    d_wo_attn = g['wo_attn'].reshape(N_HEADS, HEAD_PAD, d_model)[:, :V_DIM].reshape(N_HEADS * V_DIM, d_model)
    return {
        'mix_norm': g['mix_norm'],
        'w_in': jnp.concatenate([d_w_in[:, :LAT], d_w_in[:, LAT + QK_NOPE:LAT + QK_DIM], d_w_in[:, P_A:]], axis=1),
        'q_latent_norm': g['q_latent_norm'], 'w_uq': _unpad_heads(g['w_q'], QK_DIM),
        'kv_latent_norm': g['kv_latent_norm'],
        'w_ukv': jnp.concatenate([d_wk, d_wv], axis=2).reshape(KV_LORA, N_HEADS * (QK_NOPE + V_DIM)),
        'q_norm': g['gq'][:QK_DIM], 'k_norm': g['gk'][:QK_DIM], 'conv_w': g['conv_w'][:CONV_K],
        'conv_b': g['conv_b'], 'conv_ln_g': g['conv_ln_g'], 'conv_ln_b': g['conv_ln_b'],
        'w_out': jnp.concatenate([d_wo_attn, g['wo_conv']], axis=0),
    }


def kernel(x, ffn1_norm, ffn1_w_gate, ffn1_w_up, ffn1_w_down, mix_norm, w_in, q_latent_norm, w_uq, kv_latent_norm, w_ukv, q_norm, k_norm, conv_w, conv_b, conv_ln_g, conv_ln_b, w_out, ffn2_norm, ffn2_w_gate, ffn2_w_up, ffn2_w_down, post_norm, loss_target, m_ffn1_norm, m_ffn1_w_gate, m_ffn1_w_up, m_ffn1_w_down, m_mix_norm, m_w_in, m_q_latent_norm, m_w_uq, m_kv_latent_norm, m_w_ukv, m_q_norm, m_k_norm, m_conv_w, m_conv_b, m_conv_ln_g, m_conv_ln_b, m_w_out, m_ffn2_norm, m_ffn2_w_gate, m_ffn2_w_up, m_ffn2_w_down, m_post_norm, v_ffn1_norm, v_ffn1_w_gate, v_ffn1_w_up, v_ffn1_w_down, v_mix_norm, v_w_in, v_q_latent_norm, v_w_uq, v_kv_latent_norm, v_w_ukv, v_q_norm, v_k_norm, v_conv_w, v_conv_b, v_conv_ln_g, v_conv_ln_b, v_w_out, v_ffn2_norm, v_ffn2_w_gate, v_ffn2_w_up, v_ffn2_w_down, v_post_norm):
    args = locals()
    w = {n: args[n] for n in WEIGHTS}
    mom = {n: args["m_" + n] for n in WEIGHTS}
    var = {n: args["v_" + n] for n in WEIGHTS}
    depth = ffn1_norm.shape[0]
    x0 = x.reshape(x.shape[-2:])
    target = loss_target.reshape(loss_target.shape[-2:])
    t, d_model = x0.shape
    my_block = 4 * lax.axis_index("x") + 2 * lax.axis_index("y") + lax.axis_index("c")

    fb = ffn1_w_gate.shape[-1]
    fp = -(-fb // LANE) * LANE
    ffns = [(l, f) for l in range(depth) for f in (1, 2)]
    pad_cols = lambda a: jnp.pad(a, ((0, 0), (0, fp - fb)))
    gu_local = {(l, f): jnp.concatenate([pad_cols(w[f'ffn{f}_w_gate'][l]), pad_cols(w[f'ffn{f}_w_up'][l])],
                                        axis=1).astype(BF16) for l, f in ffns}
    dn_local = {(l, f): jnp.pad(w[f'ffn{f}_w_down'][l], ((0, fp - fb), (0, 0))).astype(BF16) for l, f in ffns}
    rows_of = {n: w[n].size // d_model for n in REST}
    rest_local = jnp.concatenate([_flat_rows(w[n].astype(BF16), d_model) for n in REST], axis=0)
    n_rest = rest_local.shape[0]
    first_ffn, later = ffns[0], ffns[1:]
    cw = conv_w.reshape(-1)
    cw_rows = -(-cw.size // (8 * LANE)) * 8
    cw_flat = jnp.pad(cw, (0, cw_rows * LANE - cw.size)).reshape(cw_rows, LANE)
    later_pieces = [(gu_local[q], 1) for q in later] + [(dn_local[q], 0) for q in later]
    n_later = len(later_pieces)
    got = _all_gather([(gu_local[first_ffn], 1), (dn_local[first_ffn], 0)], "gather_first")
    wgu, wd = {first_ffn: got[0]}, {first_ffn: got[1]}
    mixer_pieces = [(rest_local, 0), (cw_flat, 0)]
    mixer_plan, mixer_forward_plan = _gather_plans(mixer_pieces)
    gather_plan, forward_plan = _gather_plans(later_pieces)

    def landing(a, axis):
        shape = a.shape[:axis] + (N_DEV * a.shape[axis],) + a.shape[axis + 1:]
        return lax.dynamic_update_slice_in_dim(lax.empty(shape, a.dtype), a, my_block * a.shape[axis], axis)

    gather_mixer = _start_copies([a for a, _ in mixer_pieces] + [landing(a, ax) for a, ax in mixer_pieces],
                                 4 * len(mixer_pieces), mixer_plan, "gather_mixer_start", after=got[0])
    gather_later = _start_copies([a for a, _ in later_pieces] + [landing(a, ax) for a, ax in later_pieces],
                                 4 * n_later, gather_plan, "gather_later_start", after=gather_mixer[3])

    x1_first, s1_first = _ffn_fwd(x0, _after(w['ffn1_norm'][0], gather_later[3]), wgu[first_ffn], wd[first_ffn], fp)
    lands = _wait_copies(gather_mixer, x1_first, 4 * len(mixer_pieces), mixer_plan, "gather_mixer_wait")[2:]
    pass_on = _start_copies(lands, 3 * len(mixer_pieces), mixer_forward_plan, "gather_mixer_forward_start")
    lands = _wait_copies(pass_on, pass_on[3], 3 * len(mixer_pieces), mixer_forward_plan, "gather_mixer_forward_wait")
    gathered = lands[0].reshape(N_DEV, n_rest, d_model)
    cw_all = lands[1].reshape(N_DEV, cw_rows * LANE)[:, :cw.size]
    full, start = {}, 0
    for n in REST:
        blocks = gathered[:, start:start + rows_of[n]].reshape((N_DEV,) + w[n].shape)
        full[n] = _full_from_blocks(blocks, n)
        start += rows_of[n]
    conv_w_full =jnp.transpose(cw_all.reshape((N_DEV,) + conv_w.shape), (1, 2, 0, 3)).reshape(depth, CONV_K, CONV_W)
    vec = {n: w[n] for n in VECTORS}
    ops = [_layer_operands(full, vec, conv_w_full, l) for l in range(depth)]
    tabs = _rope_tables(t)

    saved, xl = [], x0
    forward_later = []

    def pass_on_later(o_attn):
        lands = _wait_copies(gather_later, o_attn, 4 * n_later, gather_plan, "gather_later_wait")[n_later:]
        forward_later.append(_start_copies(lands, 3 * n_later, forward_plan, "gather_later_forward_start"))
        return forward_later[0][3]

    for l in range(depth):
        o = ops[l]
        if l == 0:
            x1, s1 = x1_first, s1_first
            x2, sm = _mixer_fwd(x1, o, tabs, after_attention=pass_on_later)
            lands = _wait_copies(forward_later[0], x2, 3 * n_later, forward_plan, "gather_later_forward_wait")
            wgu.update(zip(later, lands[:len(later)]))
            wd.update(zip(later, lands[len(later):]))
        else:
            x1, s1 = _ffn_fwd(xl, o['ffn1_norm'], wgu[l, 1], wd[l, 1], fp)
            x2, sm = _mixer_fwd(x1, o, tabs)
        x3, s2 = _ffn_fwd(x2, o['ffn2_norm'], wgu[l, 2], wd[l, 2], fp)
        xl = _rms_fwd(x3, o['post_norm'], F32, "rms_fwd_post")
        saved.append((s1, sm, s2, x3))
    loss_part, dx = _loss_head(xl, target, "loss_head")
    loss = lax.psum(loss_part[0, 0], ("x", "y", "c"))

    grads, mine_gu, mine_dn, in_flight = [None] * depth, {}, {}, {}

    def exchange(tag):
        def after_dw(d_wgu, d_wd):
            own = [d_wgu[0], d_wd[0]]
            from_sibling, started = _reduce_start(own, [d_wgu[1], d_wd[1]], tag)
            in_flight[tag] = (own, from_sibling, started)
            return started[3]
        return after_dw

    def finish(tag, after):
        own, from_sibling, started = in_flight.pop(tag)
        return _reduce_finish(own, from_sibling, started, after, tag)

    for l in reversed(range(depth)):
        o = ops[l]
        s1, sm, s2, x3 = saved[l]
        dx, dxb, d_post = _rms_bwd(x3, o['post_norm'], dx, None, "rms_bwd_post")
        dx, dxb, d_ffn2, _, _ = _ffn_bwd(dx, dxb, s2, o['ffn2_norm'], wgu[l, 2], wd[l, 2], fp, exchange(f"{l}2"))
        if l + 1 < depth:
            mine_gu[l + 1, 1], mine_dn[l + 1, 1] = finish(f"{l + 1}1", dx)
        dx, dxb, gm = _mixer_bwd(dx, dxb, sm, o, tabs)
        mine_gu[l, 2], mine_dn[l, 2] = finish(f"{l}2", dx)
        grads[l] = _mixer_grads_to_params(gm)
        if l == 0:
            rest_own = jnp.concatenate(
                [_blocks_from_full(jnp.stack([grads[k][n] for k in range(depth)]), n).reshape(N_DEV, rows_of[n], d_model)
                 for n in REST], axis=1)
            sibling_rest, started_rest = _reduce_start([rest_own], [rest_own.astype(BF16)], "rest")
        dx, dxb, d_ffn1, _, _ = _ffn_bwd(dx, dxb, s1, o['ffn1_norm'], wgu[l, 1], wd[l, 1], fp, exchange(f"{l}1"),
                                         after=started_rest[3] if l == 0 else None)
        grads[l].update(post_norm=d_post, ffn2_norm=d_ffn2, ffn1_norm=d_ffn1)
    grad_x = dx.reshape(x.shape)
    part = {n: jnp.stack([grads[l][n] for l in range(depth)]) for n in grads[0]}

    small = jnp.concatenate([part[n].reshape(-1) for n in VECTORS] + [part['conv_w'].reshape(-1)])
    s_rows = -(-small.size // (8 * LANE)) * 8
    small = jnp.pad(small, (0, s_rows * LANE - small.size)).reshape(s_rows, LANE)
    small_all = _all_gather([(small, 0)], "gather_small_grads", in_vmem=True)[0]
    small_sum = _sum_parts([small_all[k * s_rows:(k + 1) * s_rows] for k in range(N_DEV)], "sum_small_grads")

    mine_gu[first_ffn], mine_dn[first_ffn] = finish(f"{first_ffn[0]}{first_ffn[1]}", small_sum)
    mine_rest = _reduce_finish([rest_own], sibling_rest, started_rest, small_sum, "rest")[0]
    grad = {}
    for f in (1, 2):
        grad[f'ffn{f}_w_gate'] = jnp.stack([mine_gu[l, f][:, :fb] for l in range(depth)])
        grad[f'ffn{f}_w_up'] = jnp.stack([mine_gu[l, f][:, fp:fp + fb] for l in range(depth)])
        grad[f'ffn{f}_w_down'] = jnp.stack([mine_dn[l, f][:fb] for l in range(depth)])
    start = 0
    for n in REST:
        grad[n] = mine_rest[start:start + rows_of[n]].reshape(w[n].shape)
        start += rows_of[n]
    small_sum = small_sum.reshape(-1)
    start = 0
    for n in VECTORS:
        grad[n] = small_sum[start:start + w[n].size].reshape(w[n].shape)
        start += w[n].size
    cw_grad = small_sum[start:start + depth * CONV_K * CONV_W].reshape(depth, CONV_K, CONV_W)
    nb = conv_w.shape[-1]
    grad['conv_w'] = lax.dynamic_slice_in_dim(cw_grad, my_block * nb, nb, axis=2)

    delta, new_m, new_v = {}, {}, {}
    for n in BIG + ['conv_w']:
        shp = w[n].shape
        two_d = lambda a: a.reshape(-1, shp[-1])
        dl, mn, vn = _adamw(two_d(w[n]), two_d(grad[n]), two_d(mom[n]), two_d(var[n]), "adamw_" + n)
        delta[n], new_m[n], new_v[n] = dl.reshape(shp), mn.reshape(shp), vn.reshape(shp)
    vcat = lambda src: jnp.concatenate([src[n].reshape(-1) for n in VECTORS]).reshape(-1, LANE)
    dl, mn, vn = _adamw(vcat(w), vcat(grad), vcat(mom), vcat(var), "adamw_vectors")
    start = 0
    for n in VECTORS:
        sl = lambda a: a.reshape(-1)[start:start + w[n].size].reshape(w[n].shape)
        delta[n], new_m[n], new_v[n] = sl(dl), sl(mn), sl(vn)
        start += w[n].size

    return (loss, grad_x, *[grad[n] for n in WEIGHTS], *[delta[n] for n in WEIGHTS],
            *[new_m[n] for n in WEIGHTS], *[new_v[n] for n in WEIGHTS])
```

```python
import functools

import jax
import jax.numpy as jnp
from jax import lax
from jax.experimental import pallas as pl
from jax.experimental.pallas import tpu as pltpu

F32, BF16 = jnp.float32, jnp.bfloat16

N_DEV = 8
N_HEADS = 8
QK_NOPE, QK_ROPE, V_DIM = 64, 32, 64
QK_DIM = QK_NOPE + QK_ROPE
HEAD_PAD = 128
Q_LORA, KV_LORA = 384, 256
LAT = Q_LORA + KV_LORA
CONV_W, CONV_K = 512, 31
CONV_HALO = 32
CHUNK = 64
ROPE_THETA = 10000.0
EPS = 1e-6
ATTN_SCALE = QK_DIM ** -0.5
ATTN_SCALE_LOG2 = ATTN_SCALE * 1.4426950408889634
P_KPE = LAT
P_A = LAT + HEAD_PAD
P_G = P_A + CONV_W
P_COLS = P_G + CONV_W

ADAM_LR, ADAM_B1, ADAM_B2, ADAM_EPS, ADAM_WD, ADAM_STEP = 0.001, 0.9, 0.999, 1e-08, 0.01, 10

V7X_VMEM_BYTES = 64 << 20
VMEM_LIMIT = V7X_VMEM_BYTES - (8 << 20)
MM_VMEM_BUDGET = 36 << 20
LANE = 128
ROW_TILE = 512
ATTN_BLOCK = 512
CONV_TILE = 256

MESH_ID = pl.DeviceIdType.MESH
ANY = pl.BlockSpec(memory_space=pl.ANY)
VMEM_SPEC = pl.BlockSpec(memory_space=pltpu.VMEM)
HBM_SPEC = pl.BlockSpec(memory_space=pltpu.HBM)
SEM_SPEC = pl.BlockSpec(memory_space=pltpu.SEMAPHORE)
DATAFLOW = pltpu.SideEffectType.DATAFLOW_SIDE_EFFECTING

WEIGHTS = ['ffn1_norm', 'ffn1_w_gate', 'ffn1_w_up', 'ffn1_w_down', 'mix_norm', 'w_in', 'q_latent_norm', 'w_uq',
           'kv_latent_norm', 'w_ukv', 'q_norm', 'k_norm', 'conv_w', 'conv_b', 'conv_ln_g', 'conv_ln_b', 'w_out',
           'ffn2_norm', 'ffn2_w_gate', 'ffn2_w_up', 'ffn2_w_down', 'post_norm']
COL_SHARDED = ['ffn1_w_gate', 'ffn1_w_up', 'w_in', 'w_uq', 'w_ukv', 'ffn2_w_gate', 'ffn2_w_up']
ROW_SHARDED = ['ffn1_w_down', 'w_out', 'ffn2_w_down']
REST = ['w_in', 'w_uq', 'w_ukv', 'w_out']
BIG = ['ffn1_w_gate', 'ffn1_w_up', 'ffn1_w_down', 'w_in', 'w_uq', 'w_ukv', 'w_out', 'ffn2_w_gate', 'ffn2_w_up',
       'ffn2_w_down']
VECTORS = ['ffn1_norm', 'mix_norm', 'q_latent_norm', 'kv_latent_norm', 'q_norm', 'k_norm', 'conv_b', 'conv_ln_g',
           'conv_ln_b', 'ffn2_norm', 'post_norm']


def _params(*sem):
    return pltpu.CompilerParams(dimension_semantics=sem if sem else None, vmem_limit_bytes=VMEM_LIMIT)


def _tile(n, cap):
    if n <= cap:
        return n
    best = 0
    for d in range(LANE, cap + 1, LANE):
        if n % d == 0:
            best = d
    assert best, (n, cap)
    return best


def _row_tile(n, cap=ROW_TILE, mult=8):
    if n <= cap:
        return n
    best = 0
    for d in range(mult, cap + 1, mult):
        if n % d == 0:
            best = d
    assert best, (n, cap)
    return best


def _mm(a, b, *, name, ta=False, tb=False, res=None, scale=1.0, out_dtype=F32, tm=None, tn=None, blocks=None,
        after=None):
    (kdim, m) = a.shape if ta else a.shape[::-1]
    (n, kb) = b.shape if tb else b.shape[::-1]
    assert kdim == kb, (a.shape, b.shape, ta, tb)
    tm, tn = tm or _tile(m, 512), tn or _tile(n, 1024)
    if blocks is not None:
        tm, tn = (tm, n // blocks[1]) if blocks[0] == 'col' else (m // blocks[1], tn)
    size = lambda arr: jnp.dtype(arr.dtype).itemsize
    out_bytes = tm * tn * ((6 if blocks is not None else jnp.dtype(out_dtype).itemsize) + (4 if res is not None else 0))

    def vmem_need(tk):
        return 2 * (tm * tk * size(a) + tk * tn * size(b) + out_bytes) + (tm * tn * 4 if tk < kdim else 0)

    tk = kdim
    for cand in [d for d in range(kdim - LANE, 0, -LANE) if kdim % d == 0]:
        if vmem_need(tk) <= MM_VMEM_BUDGET:
            break
        tk = cand
    nk = kdim // tk
    n_in = 2 + (res is not None) + (after is not None)
    n_out = 2 if blocks is not None else 1
    dims = (((0 if ta else 1,), (1 if tb else 0,)), ((), ()))

    def body(*refs):
        a_ref, b_ref = refs[0], refs[1]
        r_ref = refs[2] if res is not None else None
        o_refs = refs[n_in:n_in + n_out]
        acc_ref = refs[-1] if nk > 1 else None
        part = lax.dot_general(a_ref[...].astype(BF16), b_ref[...].astype(BF16), dims, preferred_element_type=F32)

        def finish(acc):
            if scale != 1.0:
                acc = acc * scale
            if r_ref is not None:
                acc = r_ref[...] + acc
            for o_ref in o_refs:
                o_ref[...] = acc.astype(o_ref.dtype)

        if nk == 1:
            finish(part)
        else:
            k = pl.program_id(2)

            @pl.when(k == 0)
            def _():
                acc_ref[...] = part

            @pl.when(k > 0)
            def _():
                acc_ref[...] += part

            @pl.when(k == nk - 1)
            def _():
                finish(acc_ref[...])

    a_spec = pl.BlockSpec((tk, tm), lambda i, j, k: (k, i)) if ta else pl.BlockSpec((tm, tk), lambda i, j, k: (i, k))
    b_spec = pl.BlockSpec((tn, tk), lambda i, j, k: (j, k)) if tb else pl.BlockSpec((tk, tn), lambda i, j, k: (k, j))
    plain = pl.BlockSpec((tm, tn), lambda i, j, k: (i, j))
    if blocks is None:
        out_specs, out_shape = plain, jax.ShapeDtypeStruct((m, n), out_dtype)
    else:
        if blocks[0] == 'col':
            o_spec, shp = pl.BlockSpec((None, tm, tn), lambda i, j, k: (j, i, 0)), (blocks[1], m, tn)
        else:
            o_spec, shp = pl.BlockSpec((None, tm, tn), lambda i, j, k: (i, 0, j)), (blocks[1], tm, n)
        out_specs, out_shape = (o_spec, o_spec), (jax.ShapeDtypeStruct(shp, F32), jax.ShapeDtypeStruct(shp, BF16))
    in_specs = [a_spec, b_spec] + ([plain] if res is not None else [])
    args = (a, b) + ((res,) if res is not None else ())
    if after is not None:
        in_specs.append(pl.BlockSpec(after.shape, lambda i, j, k: (0, 0)))
        args += (after,)
    return pl.pallas_call(
        body, name=name, grid=(m // tm, n // tn, nk), in_specs=in_specs, out_specs=out_specs, out_shape=out_shape,
        scratch_shapes=[pltpu.VMEM((tm, tn), F32)] if nk > 1 else [],
        compiler_params=_params("parallel", "parallel", "arbitrary"),
    )(*args)


def _rms_fwd(x, g, out_dtype, name):
    t, d = x.shape
    tm = _row_tile(t)

    def body(x_ref, g_ref, o_ref):
        xv = x_ref[...]
        r = lax.rsqrt(jnp.mean(xv * xv, axis=-1, keepdims=True) + EPS)
        o_ref[...] = (xv * r * g_ref[...]).astype(o_ref.dtype)

    return pl.pallas_call(
        body, name=name, grid=(t // tm,),
        in_specs=[pl.BlockSpec((tm, d), lambda i: (i, 0)), pl.BlockSpec((1, d), lambda i: (0, 0))],
        out_specs=pl.BlockSpec((tm, d), lambda i: (i, 0)),
        out_shape=jax.ShapeDtypeStruct((t, d), out_dtype), compiler_params=_params("parallel"),
    )(x, g.reshape(1, d))


def _rms_bwd(x, g, dh, res, name):
    t, d = x.shape
    tm = _row_tile(t)

    def body(*refs):
        x_ref, g_ref, dh_ref = refs[:3]
        r_ref = refs[3] if res is not None else None
        dx_ref, dxb_ref, dg_ref = refs[-3:]
        xv, dhv = x_ref[...], dh_ref[...]
        r = lax.rsqrt(jnp.mean(xv * xv, axis=-1, keepdims=True) + EPS)
        y = xv * r
        dy = dhv * g_ref[...]
        dx = r * (dy - y * jnp.mean(dy * y, axis=-1, keepdims=True))
        if r_ref is not None:
            dx = r_ref[...] + dx
        dx_ref[...] = dx
        dxb_ref[...] = dx.astype(BF16)

        @pl.when(pl.program_id(0) == 0)
        def _():
            dg_ref[...] = jnp.zeros_like(dg_ref)

        dg_ref[...] += jnp.sum(dhv * y, axis=0, keepdims=True)

    row = pl.BlockSpec((tm, d), lambda i: (i, 0))
    vec = pl.BlockSpec((1, d), lambda i: (0, 0))
    args = (x, g.reshape(1, d), dh) + ((res,) if res is not None else ())
    dx, dxb, dg = pl.pallas_call(
        body, name=name, grid=(t // tm,), in_specs=[row, vec, row] + ([row] if res is not None else []),
        out_specs=(row, row, vec),
        out_shape=(jax.ShapeDtypeStruct((t, d), F32), jax.ShapeDtypeStruct((t, d), BF16),
                   jax.ShapeDtypeStruct((1, d), F32)),
        compiler_params=_params("arbitrary"),
    )(*args)
    return dx, dxb, dg.reshape(d)


FFN_PAIR = 2


def _ffn_up(h, wgu, fp, name):
    t, d = h.shape
    tm, tn = _tile(t, 512), FFN_PAIR * 2 * fp
    nj = wgu.shape[1] // tn

    def body(h_ref, w_ref, fac_ref, z_ref):
        ab = jnp.dot(h_ref[...], w_ref[...], preferred_element_type=F32)
        for e in range(FFN_PAIR):
            av, bv = ab[:, 2 * fp * e:2 * fp * e + fp], ab[:, 2 * fp * e + fp:2 * fp * (e + 1)]
            s = jax.nn.sigmoid(av)
            silu = av * s
            z_ref[:, fp * e:fp * (e + 1)] = (silu * bv).astype(z_ref.dtype)
            fac_ref[:, 2 * fp * e:2 * fp * e + fp] = silu.astype(fac_ref.dtype)
            fac_ref[:, 2 * fp * e + fp:2 * fp * (e + 1)] = (bv * (s + silu * (1.0 - s))).astype(fac_ref.dtype)

    return pl.pallas_call(
        body, name=name, grid=(nj, t // tm),
        in_specs=[pl.BlockSpec((tm, d), lambda j, i: (i, 0)), pl.BlockSpec((d, tn), lambda j, i: (0, j))],
        out_specs=(pl.BlockSpec((tm, tn), lambda j, i: (i, j)), pl.BlockSpec((tm, tn // 2), lambda j, i: (i, j))),
        out_shape=(jax.ShapeDtypeStruct((t, wgu.shape[1]), BF16), jax.ShapeDtypeStruct((t, wgu.shape[1] // 2), BF16)),
        compiler_params=_params("parallel", "parallel"),
    )(h, wgu)


def _ffn_dab(dyb, wd, ab, fp, name):
    t, d = dyb.shape
    tm, tn = _tile(t, 512), FFN_PAIR * 2 * fp
    nj = ab.shape[1] // tn

    def body(dy_ref, wd_ref, fac_ref, dab_ref):
        dz = _dot_nt(dy_ref[...], wd_ref[...]) * 0.5
        for e in range(FFN_PAIR):
            dze = dz[:, fp * e:fp * (e + 1)]
            d_up = fac_ref[:, 2 * fp * e:2 * fp * e + fp].astype(F32)
            d_gate = fac_ref[:, 2 * fp * e + fp:2 * fp * (e + 1)].astype(F32)
            dab_ref[:, 2 * fp * e:2 * fp * e + fp] = (dze * d_gate).astype(dab_ref.dtype)
            dab_ref[:, 2 * fp * e + fp:2 * fp * (e + 1)] = (dze * d_up).astype(dab_ref.dtype)

    return pl.pallas_call(
        body, name=name, grid=(nj, t // tm),
        in_specs=[pl.BlockSpec((tm, d), lambda j, i: (i, 0)), pl.BlockSpec((tn // 2, d), lambda j, i: (j, 0)),
                  pl.BlockSpec((tm, tn), lambda j, i: (i, j))],
        out_specs=pl.BlockSpec((tm, tn), lambda j, i: (i, j)),
        out_shape=jax.ShapeDtypeStruct(ab.shape, BF16), compiler_params=_params("parallel", "parallel"),
    )(dyb, wd, ab)


def _lat_norm_fwd(p, g_q, g_kv, name):
    t = p.shape[0]
    tm = _row_tile(t)

    def body(p_ref, gq_ref, gkv_ref, q_ref, kv_ref):
        for lo, hi, g_ref, o_ref in ((0, Q_LORA, gq_ref, q_ref), (Q_LORA, LAT, gkv_ref, kv_ref)):
            xv = p_ref[:, lo:hi]
            r = lax.rsqrt(jnp.mean(xv * xv, axis=-1, keepdims=True) + EPS)
            o_ref[...] = (xv * r * g_ref[...]).astype(o_ref.dtype)

    return pl.pallas_call(
        body, name=name, grid=(t // tm,),
        in_specs=[pl.BlockSpec((tm, P_COLS), lambda i: (i, 0)), pl.BlockSpec((1, Q_LORA), lambda i: (0, 0)),
                  pl.BlockSpec((1, KV_LORA), lambda i: (0, 0))],
        out_specs=(pl.BlockSpec((tm, Q_LORA), lambda i: (i, 0)), pl.BlockSpec((tm, KV_LORA), lambda i: (i, 0))),
        out_shape=(jax.ShapeDtypeStruct((t, Q_LORA), BF16), jax.ShapeDtypeStruct((t, KV_LORA), BF16)),
        compiler_params=_params("parallel"),
    )(p, g_q.reshape(1, Q_LORA), g_kv.reshape(1, KV_LORA))


def _lat_norm_bwd(p, g_q, g_kv, dq, dkv, name):
    t = p.shape[0]
    tm = _row_tile(t)

    def body(p_ref, gq_ref, gkv_ref, dq_ref, dkv_ref, dp_ref, dgq_ref, dgkv_ref):
        first = pl.program_id(0) == 0
        for lo, hi, g_ref, d_ref, dg_ref in ((0, Q_LORA, gq_ref, dq_ref, dgq_ref),
                                             (Q_LORA, LAT, gkv_ref, dkv_ref, dgkv_ref)):
            xv, dhv = p_ref[:, lo:hi], d_ref[...]
            r = lax.rsqrt(jnp.mean(xv * xv, axis=-1, keepdims=True) + EPS)
            y = xv * r
            dy = dhv * g_ref[...]
            dp_ref[:, lo:hi] = r * (dy - y * jnp.mean(dy * y, axis=-1, keepdims=True))

            @pl.when(first)
            def _():
                dg_ref[...] = jnp.zeros_like(dg_ref)

            dg_ref[...] += jnp.sum(dhv * y, axis=0, keepdims=True)

    vq = pl.BlockSpec((1, Q_LORA), lambda i: (0, 0))
    vkv = pl.BlockSpec((1, KV_LORA), lambda i: (0, 0))
    dp, dgq, dgkv = pl.pallas_call(
        body, name=name, grid=(t // tm,),
        in_specs=[pl.BlockSpec((tm, P_COLS), lambda i: (i, 0)), vq, vkv,
                  pl.BlockSpec((tm, Q_LORA), lambda i: (i, 0)), pl.BlockSpec((tm, KV_LORA), lambda i: (i, 0))],
        out_specs=(pl.BlockSpec((tm, LAT), lambda i: (i, 0)), vq, vkv),
        out_shape=(jax.ShapeDtypeStruct((t, LAT), F32), jax.ShapeDtypeStruct((1, Q_LORA), F32),
                   jax.ShapeDtypeStruct((1, KV_LORA), F32)),
        compiler_params=_params("arbitrary"),
    )(p, g_q.reshape(1, Q_LORA), g_kv.reshape(1, KV_LORA), dq, dkv)
    return dp, dgq.reshape(Q_LORA), dgkv.reshape(KV_LORA)


def _rope_tables(t):
    half = QK_ROPE // 2
    pos = jnp.arange(t, dtype=F32)
    inv_freq = 1.0 / (ROPE_THETA ** (jnp.arange(0, QK_ROPE, 2, dtype=F32) / QK_ROPE))
    ang = pos[:, None] * inv_freq[None, :]
    cos, sin = jnp.cos(ang), jnp.sin(ang)
    z = lambda n: jnp.zeros((t, n), F32)
    c_tab = jnp.concatenate([jnp.ones((t, QK_NOPE), F32), cos, cos, z(HEAD_PAD - QK_DIM)], axis=1)
    sa_tab = jnp.concatenate([z(QK_NOPE), -sin, z(half), z(HEAD_PAD - QK_DIM)], axis=1)
    sb_tab = jnp.concatenate([z(QK_NOPE), z(half), sin, z(HEAD_PAD - QK_DIM)], axis=1)
    return c_tab, sa_tab, sb_tab


def _rope(x, c, sa, sb):
    half = QK_ROPE // 2
    return x * c + pltpu.roll(x, HEAD_PAD - half, 1) * sa + pltpu.roll(x, half, 1) * sb


def _rope_t(d, c, sa, sb):
    half = QK_ROPE // 2
    return d * c + pltpu.roll(d * sa, half, 1) + pltpu.roll(d * sb, HEAD_PAD - half, 1)


def _head_rms(x):
    r = lax.rsqrt(jnp.sum(x * x, axis=-1, keepdims=True) * (1.0 / QK_DIM) + EPS)
    return x * r, r


def _qk_prep_fwd(q_raw, k_raw, p, gq, gk, tabs, name):
    t, width = q_raw.shape
    tm = _row_tile(t)

    def body(q_ref, k_ref, p_ref, gq_ref, gk_ref, c_ref, sa_ref, sb_ref, qo_ref, ko_ref):
        c, sa, sb, kpe = c_ref[...], sa_ref[...], sb_ref[...], p_ref[...]
        for h in range(N_HEADS):
            cols = slice(h * HEAD_PAD, (h + 1) * HEAD_PAD)
            qn, _ = _head_rms(q_ref[:, cols])
            qo_ref[:, cols] = _rope(qn * gq_ref[...], c, sa, sb).astype(qo_ref.dtype)
            kn, _ = _head_rms(k_ref[:, cols] + kpe)
            ko_ref[:, cols] = _rope(kn * gk_ref[...], c, sa, sb).astype(ko_ref.dtype)

    rows = pl.BlockSpec((tm, width), lambda i: (i, 0))
    tab = pl.BlockSpec((tm, HEAD_PAD), lambda i: (i, 0))
    vec = pl.BlockSpec((1, HEAD_PAD), lambda i: (0, 0))
    kpe_spec = pl.BlockSpec((tm, HEAD_PAD), lambda i: (i, P_KPE // HEAD_PAD))
    return pl.pallas_call(
        body, name=name, grid=(t // tm,), in_specs=[rows, rows, kpe_spec, vec, vec, tab, tab, tab],
        out_specs=(rows, rows),
        out_shape=(jax.ShapeDtypeStruct(q_raw.shape, BF16), jax.ShapeDtypeStruct(k_raw.shape, BF16)),
        compiler_params=_params("parallel"),
    )(q_raw, k_raw, p, gq.reshape(1, HEAD_PAD), gk.reshape(1, HEAD_PAD), *tabs)


def _qk_prep_bwd(q_raw, k_raw, p, dq, dk, gq, gk, tabs, name):
    t, width = q_raw.shape
    tm = _row_tile(t, 256)

    def body(q_ref, k_ref, p_ref, dq_ref, dk_ref, gq_ref, gk_ref, c_ref, sa_ref, sb_ref,
             dqr_ref, dkr_ref, dkpe_ref, dgq_ref, dgk_ref):
        c, sa, sb, kpe = c_ref[...], sa_ref[...], sb_ref[...], p_ref[...]

        def one(x, d, g_ref):
            n, r = _head_rms(x)
            dng = _rope_t(d, c, sa, sb)
            dn = dng * g_ref[...]
            dx = r * (dn - n * (jnp.sum(dn * n, axis=-1, keepdims=True) * (1.0 / QK_DIM)))
            return dx, jnp.sum(dng * n, axis=0, keepdims=True)

        dgq = dgk = dkpe = None
        for h in range(N_HEADS):
            cols = slice(h * HEAD_PAD, (h + 1) * HEAD_PAD)
            dqr, gq_part = one(q_ref[:, cols], dq_ref[:, cols], gq_ref)
            dkr, gk_part = one(k_ref[:, cols] + kpe, dk_ref[:, cols], gk_ref)
            dqr_ref[:, cols] = dqr
            dkr_ref[:, cols] = dkr
            dgq = gq_part if dgq is None else dgq + gq_part
            dgk = gk_part if dgk is None else dgk + gk_part
            dkpe = dkr if dkpe is None else dkpe + dkr
        dkpe_ref[...] = dkpe

        @pl.when(pl.program_id(0) == 0)
        def _():
            dgq_ref[...] = jnp.zeros_like(dgq_ref)
            dgk_ref[...] = jnp.zeros_like(dgk_ref)

        dgq_ref[...] += dgq
        dgk_ref[...] += dgk

    head = pl.BlockSpec((tm, width), lambda i: (i, 0))
    tab = pl.BlockSpec((tm, HEAD_PAD), lambda i: (i, 0))
    vec = pl.BlockSpec((1, HEAD_PAD), lambda i: (0, 0))
    kpe = pl.BlockSpec((tm, HEAD_PAD), lambda i: (i, P_KPE // HEAD_PAD))
    dqr, dkr, dkpe, dgq, dgk = pl.pallas_call(
        body, name=name, grid=(t // tm,), in_specs=[head, head, kpe, head, head, vec, vec, tab, tab, tab],
        out_specs=(head, head, tab, vec, vec),
        out_shape=(jax.ShapeDtypeStruct(q_raw.shape, F32), jax.ShapeDtypeStruct(k_raw.shape, F32),
                   jax.ShapeDtypeStruct((t, HEAD_PAD), F32), jax.ShapeDtypeStruct((1, HEAD_PAD), F32),
                   jax.ShapeDtypeStruct((1, HEAD_PAD), F32)),
        compiler_params=_params("arbitrary"),
    )(q_raw, k_raw, p, dq, dk, gq.reshape(1, HEAD_PAD), gk.reshape(1, HEAD_PAD), *tabs)
    return dqr, dkr, dkpe, dgq.reshape(HEAD_PAD), dgk.reshape(HEAD_PAD)


def _dot_nt(a, b):
    return lax.dot_general(a, b, (((1,), (1,)), ((), ())), preferred_element_type=F32)


def _dot_tn(a, b):
    return lax.dot_general(a, b, (((0,), (0,)), ((), ())), preferred_element_type=F32)


def _diag_mask():
    rows = lax.broadcasted_iota(jnp.int32, (ATTN_BLOCK, ATTN_BLOCK), 0) // CHUNK
    cols = lax.broadcasted_iota(jnp.int32, (ATTN_BLOCK, ATTN_BLOCK), 1) // CHUNK
    return cols <= rows


def _attn_fwd(q, k, v, name):
    t = q.shape[0]
    bq = ATTN_BLOCK
    nq = t // bq

    def body(q_ref, k_ref, v_ref, o_ref, lse_ref):
        i = pl.program_id(1)
        qv = q_ref[...]

        def block(j, carry, masked):
            m, l, acc = carry
            rows = pl.ds(pl.multiple_of(j * bq, bq), bq)
            s = _dot_nt(qv, k_ref[rows, :]) * ATTN_SCALE_LOG2
            if masked:
                s = jnp.where(_diag_mask(), s, -1e30)
            m_new = jnp.maximum(m, jnp.max(s, axis=-1, keepdims=True))
            alpha = jnp.exp2(m - m_new)
            pe = jnp.exp2(s - m_new)
            l = alpha * l + jnp.sum(pe, axis=-1, keepdims=True)
            acc = alpha * acc + jnp.dot(pe.astype(BF16), v_ref[rows, :], preferred_element_type=F32)
            return m_new, l, acc

        init = (jnp.full((bq, 1), -1e30, F32), jnp.zeros((bq, 1), F32), jnp.zeros((bq, HEAD_PAD), F32))
        carry = lax.fori_loop(0, i, lambda j, cr: block(j, cr, False), init)
        m, l, acc = block(i, carry, True)
        o_ref[...] = acc / l
        lse_ref[...] = jnp.broadcast_to(m + jnp.log2(l), (bq, HEAD_PAD))

    blk = pl.BlockSpec((bq, HEAD_PAD), lambda h, i: (i, h))
    full = pl.BlockSpec((t, HEAD_PAD), lambda h, i: (0, h))
    return pl.pallas_call(
        body, name=name, grid=(N_HEADS, nq), in_specs=[blk, full, full], out_specs=(blk, blk),
        out_shape=(jax.ShapeDtypeStruct(q.shape, F32), jax.ShapeDtypeStruct(q.shape, F32)),
        compiler_params=_params("parallel", "parallel"),
    )(q, k, v)


def _attn_bwd(q, k, v, o, lse, do, name):
    t = q.shape[0]
    bq = ATTN_BLOCK
    nq = t // bq

    def body(q_ref, k_ref, v_ref, o_ref, lse_ref, do_ref, dq_ref, dk_ref, dv_ref, delta_ref):
        def rows_of(i):
            return pl.ds(pl.multiple_of(i * bq, bq), bq)

        def prep(i, _):
            r = rows_of(i)
            delta_ref[r, :] = jnp.broadcast_to(jnp.sum(do_ref[r, :] * o_ref[r, :], axis=-1, keepdims=True),
                                               (bq, HEAD_PAD))
            dq_ref[r, :] = jnp.zeros((bq, HEAD_PAD), F32)
            return 0

        lax.fori_loop(0, nq, prep, 0)

        def key_block(j, _):
            rj = rows_of(j)
            kb, vb = k_ref[rj, :], v_ref[rj, :]

            def query_block(i, carry, masked):
                dk, dv = carry
                ri = rows_of(i)
                qb, dob = q_ref[ri, :], do_ref[ri, :].astype(BF16)
                s = _dot_nt(qb, kb) * ATTN_SCALE_LOG2
                if masked:
                    s = jnp.where(_diag_mask(), s, -1e30)
                pe = jnp.exp2(s - lse_ref[ri, :][:, :1])
                dp = _dot_nt(dob, vb)
                ds = (pe * (dp - delta_ref[ri, :][:, :1]) * ATTN_SCALE).astype(BF16)
                dq_ref[ri, :] += jnp.dot(ds, kb, preferred_element_type=F32)
                return dk + _dot_tn(ds, qb), dv + _dot_tn(pe.astype(BF16), dob)

            zero = jnp.zeros((bq, HEAD_PAD), F32)
            carry = query_block(j, (zero, zero), True)
            dk, dv = lax.fori_loop(j + 1, nq, lambda i, cr: query_block(i, cr, False), carry)
            dk_ref[rj, :] = dk
            dv_ref[rj, :] = dv
            return 0

        lax.fori_loop(0, nq, key_block, 0)

    full = pl.BlockSpec((t, HEAD_PAD), lambda h: (0, h))
    shp = jax.ShapeDtypeStruct(q.shape, F32)
    return pl.pallas_call(
        body, name=name, grid=(N_HEADS,), in_specs=[full] * 6, out_specs=(full, full, full),
        out_shape=(shp, shp, shp), scratch_shapes=[pltpu.VMEM((t, HEAD_PAD), F32)],
        compiler_params=_params("parallel"),
    )(q, k, v, o, lse, do)


def _glu_ext(pc_ref, pp_ref, u_ref, tm, first):
    u_ref[CONV_HALO:CONV_HALO + tm, :] = pc_ref[:, P_A:P_G] * jax.nn.sigmoid(pc_ref[:, P_G:P_COLS])
    up = pp_ref[tm - CONV_HALO:tm, P_A:P_G] * jax.nn.sigmoid(pp_ref[tm - CONV_HALO:tm, P_G:P_COLS])
    u_ref[0:CONV_HALO, :] = jnp.where(first, 0.0, up)


SUBLANES = 8


def _shift_copies(src_ref, sh_ref):
    rows = sh_ref.shape[1]
    for b in range(1, SUBLANES):
        sh_ref[b - 1, :, :] = src_ref[b:b + rows, :]


def _rows_at(src_ref, sh_ref, start, n):
    a, b = divmod(start, SUBLANES)
    if b == 0:
        return src_ref[SUBLANES * a:SUBLANES * a + n, :]
    return sh_ref[b - 1, SUBLANES * a:SUBLANES * a + n, :]


def _conv_fwd(p, w, b, ln_g, ln_b, name):
    t = p.shape[0]
    tm = _row_tile(t, CONV_TILE)
    off = CONV_HALO - (CONV_K - 1)

    def body(pc_ref, pp_ref, w_ref, b_ref, g_ref, bb_ref, y_ref, o_ref, u_ref, ush_ref):
        _glu_ext(pc_ref, pp_ref, u_ref, tm, pl.program_id(0) == 0)
        _shift_copies(u_ref, ush_ref)
        acc = jnp.zeros((tm, CONV_W), F32)
        for kk in range(CONV_K):
            acc = acc + w_ref[kk:kk + 1, :] * _rows_at(u_ref, ush_ref, off + kk, tm)
        y = acc + b_ref[...]
        y_ref[...] = y
        xc = y - jnp.mean(y, axis=-1, keepdims=True)
        lo = xc * lax.rsqrt(jnp.mean(xc * xc, axis=-1, keepdims=True) + EPS) * g_ref[...] + bb_ref[...]
        o_ref[...] = (lo * jax.nn.sigmoid(lo)).astype(o_ref.dtype)

    prow = pl.BlockSpec((tm, P_COLS), lambda i: (i, 0))
    pprev = pl.BlockSpec((tm, P_COLS), lambda i: (jnp.maximum(i - 1, 0), 0))
    vec = pl.BlockSpec((1, CONV_W), lambda i: (0, 0))
    row = pl.BlockSpec((tm, CONV_W), lambda i: (i, 0))
    return pl.pallas_call(
        body, name=name, grid=(t // tm,),
        in_specs=[prow, pprev, pl.BlockSpec((CONV_HALO, CONV_W), lambda i: (0, 0)), vec, vec, vec],
        out_specs=(row, row),
        out_shape=(jax.ShapeDtypeStruct((t, CONV_W), F32), jax.ShapeDtypeStruct((t, CONV_W), BF16)),
        scratch_shapes=[pltpu.VMEM((tm + CONV_HALO, CONV_W), F32),
                        pltpu.VMEM((SUBLANES - 1, tm + CONV_HALO - SUBLANES, CONV_W), F32)],
        compiler_params=_params("parallel"),
    )(p, p, w, b.reshape(1, CONV_W), ln_g.reshape(1, CONV_W), ln_b.reshape(1, CONV_W))


def _conv_bwd_ln(y, dout, ln_g, ln_b, name):
    t = y.shape[0]
    tm = _row_tile(t)

    def body(y_ref, d_ref, g_ref, bb_ref, dy_ref, dg_ref, db_ref, dcb_ref):
        yv = y_ref[...]
        xc = yv - jnp.mean(yv, axis=-1, keepdims=True)
        r = lax.rsqrt(jnp.mean(xc * xc, axis=-1, keepdims=True) + EPS)
        n = xc * r
        lo = n * g_ref[...] + bb_ref[...]
        s = jax.nn.sigmoid(lo)
        dlo = d_ref[...] * (s * (1.0 + lo * (1.0 - s)))
        dn = dlo * g_ref[...]
        dy = r * (dn - jnp.mean(dn, axis=-1, keepdims=True) - n * jnp.mean(dn * n, axis=-1, keepdims=True))
        dy_ref[...] = dy

        @pl.when(pl.program_id(0) == 0)
        def _():
            dg_ref[...] = jnp.zeros_like(dg_ref)
            db_ref[...] = jnp.zeros_like(db_ref)
            dcb_ref[...] = jnp.zeros_like(dcb_ref)

        dg_ref[...] += jnp.sum(dlo * n, axis=0, keepdims=True)
        db_ref[...] += jnp.sum(dlo, axis=0, keepdims=True)
        dcb_ref[...] += jnp.sum(dy, axis=0, keepdims=True)

    row = pl.BlockSpec((tm, CONV_W), lambda i: (i, 0))
    vec = pl.BlockSpec((1, CONV_W), lambda i: (0, 0))
    vshape = jax.ShapeDtypeStruct((1, CONV_W), F32)
    dy, dg, db, dcb = pl.pallas_call(
        body, name=name, grid=(t // tm,), in_specs=[row, row, vec, vec], out_specs=(row, vec, vec, vec),
        out_shape=(jax.ShapeDtypeStruct((t, CONV_W), F32), vshape, vshape, vshape),
        compiler_params=_params("arbitrary"),
    )(y, dout, ln_g.reshape(1, CONV_W), ln_b.reshape(1, CONV_W))
    return dy, dg.reshape(CONV_W), db.reshape(CONV_W), dcb.reshape(CONV_W)


def _conv_bwd_taps(p, dy, w, name):
    t = p.shape[0]
    tm = _row_tile(t, CONV_TILE)
    nt = t // tm
    off = CONV_HALO - (CONV_K - 1)

    def body(pc_ref, pp_ref, dyc_ref, dyn_ref, w_ref, dag_ref, dw_ref, u_ref, dye_ref, ush_ref, dysh_ref):
        i = pl.program_id(0)
        _glu_ext(pc_ref, pp_ref, u_ref, tm, i == 0)
        dyc = dyc_ref[...]
        dye_ref[0:tm, :] = dyc
        dye_ref[tm:tm + CONV_HALO, :] = jnp.where(i == nt - 1, 0.0, dyn_ref[0:CONV_HALO, :])

        _shift_copies(u_ref, ush_ref)
        _shift_copies(dye_ref, dysh_ref)

        @pl.when(i == 0)
        def _():
            dw_ref[...] = jnp.zeros_like(dw_ref)

        du = jnp.zeros((tm, CONV_W), F32)
        for kk in range(CONV_K):
            dw_ref[kk:kk + 1, :] += jnp.sum(dyc * _rows_at(u_ref, ush_ref, off + kk, tm), axis=0, keepdims=True)
            du = du + w_ref[kk:kk + 1, :] * _rows_at(dye_ref, dysh_ref, CONV_K - 1 - kk, tm)
        av, gv = pc_ref[:, P_A:P_G], pc_ref[:, P_G:P_COLS]
        s = jax.nn.sigmoid(gv)
        dag_ref[:, 0:CONV_W] = du * s
        dag_ref[:, CONV_W:2 * CONV_W] = du * av * (s * (1.0 - s))

    prow = pl.BlockSpec((tm, P_COLS), lambda i: (i, 0))
    pprev = pl.BlockSpec((tm, P_COLS), lambda i: (jnp.maximum(i - 1, 0), 0))
    row = pl.BlockSpec((tm, CONV_W), lambda i: (i, 0))
    nxt = pl.BlockSpec((tm, CONV_W), lambda i: (jnp.minimum(i + 1, nt - 1), 0))
    wspec = pl.BlockSpec((CONV_HALO, CONV_W), lambda i: (0, 0))
    return pl.pallas_call(
        body, name=name, grid=(nt,), in_specs=[prow, pprev, row, nxt, wspec],
        out_specs=(pl.BlockSpec((tm, 2 * CONV_W), lambda i: (i, 0)), wspec),
        out_shape=(jax.ShapeDtypeStruct((t, 2 * CONV_W), F32), jax.ShapeDtypeStruct((CONV_HALO, CONV_W), F32)),
        scratch_shapes=[pltpu.VMEM((tm + CONV_HALO, CONV_W), F32), pltpu.VMEM((tm + CONV_HALO, CONV_W), F32),
                        pltpu.VMEM((SUBLANES - 1, tm + CONV_HALO - SUBLANES, CONV_W), F32),
                        pltpu.VMEM((SUBLANES - 1, tm + CONV_HALO - SUBLANES, CONV_W), F32)],
        compiler_params=_params("arbitrary"),
    )(p, p, dy, dy, w)


def _loss_head(y, target, name):
    t, d = y.shape
    tm = _row_tile(t)

    def body(y_ref, t_ref, l_ref, dy_ref):
        err = y_ref[...] - t_ref[...]
        dy_ref[...] = err * (1.0 / d)

        @pl.when(pl.program_id(0) == 0)
        def _():
            l_ref[...] = jnp.zeros_like(l_ref)

        row = jnp.sum(err * err, axis=-1, keepdims=True) * (0.5 / d)
        l_ref[...] += jnp.broadcast_to(jnp.sum(row, axis=0, keepdims=True), (1, LANE))

    row = pl.BlockSpec((tm, d), lambda i: (i, 0))
    return pl.pallas_call(
        body, name=name, grid=(t // tm,), in_specs=[row, row],
        out_specs=(pl.BlockSpec((1, LANE), lambda i: (0, 0)), row),
        out_shape=(jax.ShapeDtypeStruct((1, LANE), F32), jax.ShapeDtypeStruct((t, d), F32)),
        compiler_params=_params("arbitrary"),
    )(y, target)


def _adamw(w, g, m, v, name):
    r, c = w.shape
    tr = _row_tile(r, 256)
    c1, c2 = 1.0 - ADAM_B1 ** ADAM_STEP, 1.0 - ADAM_B2 ** ADAM_STEP

    def body(w_ref, g_ref, m_ref, v_ref, d_ref, mo_ref, vo_ref):
        gv = g_ref[...]
        mn = ADAM_B1 * m_ref[...] + (1.0 - ADAM_B1) * gv
        vn = ADAM_B2 * v_ref[...] + (1.0 - ADAM_B2) * (gv * gv)
        mo_ref[...] = mn
        vo_ref[...] = vn
        d_ref[...] = -ADAM_LR * ((mn / c1) / (jnp.sqrt(vn / c2) + ADAM_EPS) + ADAM_WD * w_ref[...])

    blk = pl.BlockSpec((tr, c), lambda i: (i, 0))
    shp = jax.ShapeDtypeStruct((r, c), F32)
    return pl.pallas_call(
        body, name=name, grid=(r // tr,), in_specs=[blk] * 4, out_specs=(blk, blk, blk), out_shape=(shp, shp, shp),
        compiler_params=_params("parallel"),
    )(w, g, m, v)


def _sum_parts(parts, name):
    r, c = parts[0].shape
    tr = _row_tile(r, 256)

    def body(*refs):
        acc = refs[0][...]
        for ref in refs[1:-1]:
            acc = acc + ref[...]
        refs[-1][...] = acc

    blk = pl.BlockSpec((tr, c), lambda i: (i, 0))
    return pl.pallas_call(
        body, name=name, grid=(r // tr,), in_specs=[blk] * len(parts), out_specs=blk,
        out_shape=jax.ShapeDtypeStruct((r, c), F32), compiler_params=_params("parallel"),
    )(*parts)


def _place():
    return lax.axis_index("x"), lax.axis_index("y"), lax.axis_index("c")


def _window(ref, block, size, axis):
    start = pl.multiple_of(block * size, LANE if size % LANE == 0 else 8)
    return ref.at[(slice(None),) * axis + (pl.ds(start, size),)]


def _all_gather(pieces, name, in_vmem=False, seeds=()):
    n_p, pieces = len(pieces), list(pieces) + list(seeds)
    n_all = len(pieces)

    def body(*refs):
        x_refs, out_refs = refs[:n_all], refs[n_all:2 * n_all]
        send_sems, recv_sems, local_sems = refs[2 * n_all:]
        px, py, pc = _place()
        me, sibling = (px, py, pc), (px, py, 1 - pc)
        chips = [(1 - px, py), (px, 1 - py), (1 - px, 1 - py)]

        def win(p, block):
            bx, by, bc = block
            x, axis = pieces[p]
            return _window(out_refs[p], 4 * bx + 2 * by + bc, x.shape[axis], axis)

        def copy(k, p, block, to, local=False):
            return pltpu.make_async_remote_copy(
                src_ref=x_refs[p] if local else win(p, block), dst_ref=win(p, block),
                send_sem=send_sems.at[k, p], recv_sem=recv_sems.at[k, p], device_id=to, device_id_type=MESH_ID)

        every = range(n_p)
        mine = [pltpu.make_async_copy(x_refs[p], win(p, me), local_sems.at[p]) for p in range(n_all)]
        first = [copy(0, p, me, sibling, local=True) for p in every]
        first += [copy(1 + j, p, me, (*chip, pc), local=True) for j, chip in enumerate(chips) for p in every]
        for cp in mine + first:
            cp.start()
        passed = []
        for j, chip in enumerate(chips):
            for p in every:
                copy(1 + j, p, (*chip, pc), me).wait_recv()
                passed.append(copy(4 + j, p, (*chip, pc), sibling))
                passed[-1].start()
        for p in every:
            copy(0, p, sibling, me).wait_recv()
        for j, chip in enumerate(chips):
            for p in every:
                copy(4 + j, p, (*chip, 1 - pc), me).wait_recv()
        for cp in first + passed:
            cp.wait_send()
        for cp in mine:
            cp.wait()

    def gathered(x, axis):
        return jax.ShapeDtypeStruct(x.shape[:axis] + (N_DEV * x.shape[axis],) + x.shape[axis + 1:], x.dtype)

    spec = VMEM_SPEC if in_vmem else ANY
    return pl.pallas_call(
        body, name=name, in_specs=[spec] * n_all, out_specs=[spec] * n_all,
        out_shape=[gathered(*pc_) for pc_ in pieces],
        scratch_shapes=[pltpu.SemaphoreType.DMA((7, n_p)), pltpu.SemaphoreType.DMA((7, n_p)),
                        pltpu.SemaphoreType.DMA((n_all,))],
        compiler_params=pltpu.CompilerParams(vmem_limit_bytes=VMEM_LIMIT),
    )(*[x for x, _ in pieces])


def _start_copies(bufs, n_copies, plan, name, after=None):
    nb = len(bufs)
    n_in = nb + (after is not None)

    def body(*refs):
        send_sems, recv_sems, token = refs[n_in], refs[n_in + 1], refs[-1]
        for i, (src, dst, dev) in enumerate(plan(refs[:nb])):
            pltpu.make_async_remote_copy(src_ref=src, dst_ref=dst, send_sem=send_sems.at[i], recv_sem=recv_sems.at[i],
                                         device_id=dev, device_id_type=MESH_ID).start()
        token[...] = jnp.zeros_like(token)

    out = pl.pallas_call(
        body, name=name, in_specs=[HBM_SPEC] * nb + [ANY] * (after is not None),
        out_shape=(pltpu.SemaphoreType.DMA((n_copies,)), pltpu.SemaphoreType.DMA((n_copies,)),
                   *[pltpu.HBM(b.shape, b.dtype) for b in bufs], jax.ShapeDtypeStruct((8, LANE), F32)),
        out_specs=(SEM_SPEC, SEM_SPEC, *[HBM_SPEC] * nb, VMEM_SPEC),
        input_output_aliases={i: 2 + i for i in range(nb)},
        compiler_params=pltpu.CompilerParams(has_side_effects=DATAFLOW),
    )(*[pltpu.with_memory_space_constraint(b, pltpu.HBM) for b in bufs], *([after] if after is not None else []))
    return out[0], out[1], list(out[2:2 + nb]), out[-1]


def _wait_copies(started, after, n_copies, plan, name):
    send_sems, recv_sems, bufs, _ = started
    nb = len(bufs)

    def body(*refs):
        send_ref, recv_ref = refs[nb], refs[nb + 1]
        copies = [pltpu.make_async_remote_copy(src_ref=src, dst_ref=dst, send_sem=send_ref.at[i], recv_sem=recv_ref.at[i],
                                               device_id=dev, device_id_type=MESH_ID)
                  for i, (src, dst, dev) in enumerate(plan(refs[:nb]))]
        for cp in copies:
            cp.wait_send()
        for cp in copies:
            cp.wait_recv()

    out = pl.pallas_call(
        body, name=name, in_specs=[HBM_SPEC] * nb + [SEM_SPEC, SEM_SPEC, ANY],
        out_shape=tuple(pltpu.HBM(b.shape, b.dtype) for b in bufs), out_specs=tuple([HBM_SPEC] * nb),
        input_output_aliases={i: i for i in range(nb)},
        compiler_params=pltpu.CompilerParams(has_side_effects=DATAFLOW),
    )(*bufs, send_sems, recv_sems, after)
    return list(out)


def _after(x, token):
    return x + token[0, 0].astype(x.dtype)


def _other_chips():
    px, py, _ = _place()
    return [(1 - px, py), (px, 1 - py), (1 - px, 1 - py)]


def _exchange(srcs, slots, src_block, target, name):
    n_p = len(srcs)

    def body(*refs):
        src_refs, out_refs, send_sems, recv_sems = refs[:n_p], refs[n_p:2 * n_p], refs[-2], refs[-1]
        copies = [pltpu.make_async_remote_copy(
            src_ref=src_refs[p].at[src_block(s)], dst_ref=out_refs[p].at[s], send_sem=send_sems.at[s, p],
            recv_sem=recv_sems.at[s, p], device_id=target(s), device_id_type=MESH_ID)
            for s in range(slots) for p in range(n_p)]
        for cp in copies:
            cp.start()
        for cp in copies:
            cp.wait_recv()
        for cp in copies:
            cp.wait_send()

    return pl.pallas_call(
        body, name=name, in_specs=[ANY] * n_p, out_specs=[ANY] * n_p,
        out_shape=[jax.ShapeDtypeStruct((slots,) + a.shape[1:], a.dtype) for a in srcs],
        scratch_shapes=[pltpu.SemaphoreType.DMA((slots, n_p)), pltpu.SemaphoreType.DMA((slots, n_p))],
        compiler_params=pltpu.CompilerParams(vmem_limit_bytes=VMEM_LIMIT),
    )(*srcs)


def _blocks_to_sibling(sends, name):
    def src_block(j):
        return 2 * j + 1 - lax.axis_index("c")

    def target(j):
        px, py, pc = _place()
        return (px, py, 1 - pc)

    return _exchange(sends, 4, src_block, target, name)


def _pair_sums_for_chips(own, got, name):
    _, r, c = own.shape
    tr = _row_tile(r, 256, 16)

    def body(idx_ref, own_ref, got_ref, o_ref):
        o_ref[...] = (own_ref[...] + got_ref[...].astype(F32)).astype(o_ref.dtype)

    grid_spec = pltpu.PrefetchScalarGridSpec(
        num_scalar_prefetch=1, grid=(3, r // tr),
        in_specs=[pl.BlockSpec((None, tr, c), lambda k, i, idx: (idx[k], i, 0)),
                  pl.BlockSpec((None, tr, c), lambda k, i, idx: (idx[3 + k], i, 0))],
        out_specs=pl.BlockSpec((None, tr, c), lambda k, i, idx: (k, i, 0)))
    chips = [2 * cx + cy for cx, cy in _other_chips()]
    idx = jnp.stack([2 * j + lax.axis_index("c") for j in chips] + chips).astype(jnp.int32)
    return pl.pallas_call(
        body, name=name, grid_spec=grid_spec, out_shape=jax.ShapeDtypeStruct((3, r, c), BF16),
        compiler_params=_params("parallel", "parallel"),
    )(idx, own, got)


def _sum_for_me(own, got_sibling, got_chips, name):
    _, r, c = own.shape
    tr = _row_tile(r, 256, 16)

    def body(idx_ref, own_ref, sib_ref, g0_ref, g1_ref, g2_ref, o_ref):
        acc = own_ref[...] + sib_ref[...].astype(F32)
        for ref in (g0_ref, g1_ref, g2_ref):
            acc = acc + ref[...].astype(F32)
        o_ref[...] = acc

    def part(k):
        return pl.BlockSpec((None, tr, c), lambda i, idx: (k, i, 0))

    grid_spec = pltpu.PrefetchScalarGridSpec(
        num_scalar_prefetch=1, grid=(r // tr,),
        in_specs=[pl.BlockSpec((None, tr, c), lambda i, idx: (idx[0], i, 0)),
                  pl.BlockSpec((None, tr, c), lambda i, idx: (idx[1], i, 0)), part(0), part(1), part(2)],
        out_specs=pl.BlockSpec((tr, c), lambda i, idx: (i, 0)))
    px, py, pc = _place()
    idx = jnp.stack([4 * px + 2 * py + pc, 2 * px + py]).astype(jnp.int32)
    return pl.pallas_call(
        body, name=name, grid_spec=grid_spec, out_shape=jax.ShapeDtypeStruct((r, c), F32),
        compiler_params=_params("parallel"),
    )(idx, own, got_sibling, got_chips, got_chips, got_chips)


def _chip_plan(n_p):
    def plan(refs):
        pc = lax.axis_index("c")
        return [(refs[p].at[k], refs[n_p + p].at[k], (cx, cy, pc))
                for p in range(n_p) for k, (cx, cy) in enumerate(_other_chips())]
    return plan


def _reduce_start(own, sends, tag):
    from_sibling = _blocks_to_sibling(sends, "grads_to_sibling_" + tag)
    pair_sums = [_pair_sums_for_chips(a, b, "grads_pair_sums") for a, b in zip(own, from_sibling)]
    lands = [lax.empty(a.shape, a.dtype) for a in pair_sums]
    started = _start_copies(pair_sums + lands, 3 * len(own), _chip_plan(len(own)), "grads_to_chips_start_" + tag)
    return from_sibling, started


def _reduce_finish(own, from_sibling, started, after, tag):
    n_p = len(own)
    bufs = _wait_copies(started, after, 3 * n_p, _chip_plan(n_p), "grads_to_chips_wait_" + tag)
    return [_sum_for_me(a, b, c, "grads_sum") for a, b, c in zip(own, from_sibling, bufs[n_p:])]


def _direct_plan(n_p):
    def plan(refs):
        px, py, pc = _place()
        out = []
        for p in range(n_p):
            for m in range(1, N_DEV):
                tx = 1 - px if m & 4 else px
                ty = 1 - py if m & 2 else py
                tc = 1 - pc if m & 1 else pc
                out.append((refs[p].at[4 * tx + 2 * ty + tc], refs[n_p + p].at[m - 1], (tx, ty, tc)))
        return out
    return plan


def _sum_direct(own, got, name):
    _, r, c = own.shape
    tr = _row_tile(r, 256, 16)

    def body(idx_ref, own_ref, *refs):
        acc = own_ref[...]
        for ref in refs[:-1]:
            acc = acc + ref[...].astype(F32)
        refs[-1][...] = acc

    def part(k):
        return pl.BlockSpec((None, tr, c), lambda i, idx: (k, i, 0))

    grid_spec = pltpu.PrefetchScalarGridSpec(
        num_scalar_prefetch=1, grid=(r // tr,),
        in_specs=[pl.BlockSpec((None, tr, c), lambda i, idx: (idx[0], i, 0))] + [part(k) for k in range(N_DEV - 1)],
        out_specs=pl.BlockSpec((tr, c), lambda i, idx: (i, 0)))
    px, py, pc = _place()
    idx = (4 * px + 2 * py + pc).astype(jnp.int32).reshape(1)
    return pl.pallas_call(
        body, name=name, grid_spec=grid_spec, out_shape=jax.ShapeDtypeStruct((r, c), F32),
        compiler_params=_params("parallel"),
    )(idx, own, *([got] * (N_DEV - 1)))


def _reduce_direct_start(sends, tag):
    lands = [lax.empty((N_DEV - 1,) + a.shape[1:], a.dtype) for a in sends]
    return _start_copies(list(sends) + lands, (N_DEV - 1) * len(sends), _direct_plan(len(sends)),
                         "grads_direct_start_" + tag)


def _reduce_direct_finish(own, started, after, tag):
    n_p = len(own)
    bufs = _wait_copies(started, after, (N_DEV - 1) * n_p, _direct_plan(n_p), "grads_direct_wait_" + tag)
    return [_sum_direct(a, b, "grads_sum_direct") for a, b in zip(own, bufs[n_p:])]


def _gather_plans(pieces):
    n_p = len(pieces)
    dims = [(x.shape[axis], axis) for x, axis in pieces]

    def first(refs):
        px, py, pc = _place()
        targets = [(px, py, 1 - pc)] + [(cx, cy, pc) for cx, cy in _other_chips()]
        return [(refs[p], _window(refs[n_p + p], 4 * px + 2 * py + pc, *dims[p]), to)
                for p in range(n_p) for to in targets]

    def second(refs):
        px, py, pc = _place()
        out = []
        for p in range(n_p):
            for cx, cy, cc in [(cx, cy, pc) for cx, cy in _other_chips()] + [(px, py, 1 - pc)]:
                win = _window(refs[p], 4 * cx + 2 * cy + cc, *dims[p])
                out.append((win, win, (px, py, 1 - pc)))
        return out

    return first, second


FORWARD_COPIES = 4


def _flat_rows(a, width):
    return a.reshape(-1, width)


def _full_from_blocks(blocks, name):
    if name in COL_SHARDED:
        _, l, k, nb = blocks.shape
        return jnp.transpose(blocks, (1, 2, 0, 3)).reshape(l, k, N_DEV * nb)
    _, l, rb, n = blocks.shape
    return jnp.transpose(blocks, (1, 0, 2, 3)).reshape(l, N_DEV * rb, n)


def _blocks_from_full(full, name):
    if name in COL_SHARDED:
        l, k, n = full.shape
        return jnp.transpose(full.reshape(l, k, N_DEV, n // N_DEV), (2, 0, 1, 3))
    l, rows, n = full.shape
    return jnp.transpose(full.reshape(l, N_DEV, rows // N_DEV, n), (1, 0, 2, 3))


def _pad_heads(w, width):
    k = w.shape[0]
    return jnp.pad(w.reshape(k, N_HEADS, width), ((0, 0), (0, 0), (0, HEAD_PAD - width))).reshape(k, N_HEADS * HEAD_PAD)


def _unpad_heads(w, width):
    k = w.shape[0]
    return w.reshape(k, N_HEADS, HEAD_PAD)[:, :, :width].reshape(k, N_HEADS * width)


def _layer_operands(full, vec, conv_w_full, l):
    w_in = full['w_in'][l]
    kpe = jnp.pad(w_in[:, LAT:LAT + QK_ROPE], ((0, 0), (QK_NOPE, HEAD_PAD - QK_DIM)))
    w_ukv = full['w_ukv'][l].reshape(KV_LORA, N_HEADS, QK_NOPE + V_DIM)
    w_out = full['w_out'][l]
    d_model = w_out.shape[1]
    wo_attn = jnp.pad(w_out[:N_HEADS * V_DIM].reshape(N_HEADS, V_DIM, d_model),
                      ((0, 0), (0, HEAD_PAD - V_DIM), (0, 0))).reshape(N_HEADS * HEAD_PAD, d_model)
    ops = {
        'w_in': jnp.concatenate([w_in[:, :LAT], kpe, w_in[:, LAT + QK_ROPE:]], axis=1),
        'w_q': _pad_heads(full['w_uq'][l], QK_DIM),
        'w_k': _pad_heads(w_ukv[:, :, :QK_NOPE].reshape(KV_LORA, N_HEADS * QK_NOPE), QK_NOPE),
        'w_v': _pad_heads(w_ukv[:, :, QK_NOPE:].reshape(KV_LORA, N_HEADS * V_DIM), V_DIM),
        'wo_attn': wo_attn,
        'wo_conv': w_out[N_HEADS * V_DIM:],
        'conv_w': jnp.pad(conv_w_full[l], ((0, CONV_HALO - CONV_K), (0, 0))),
        'gq': jnp.pad(vec['q_norm'][l], (0, HEAD_PAD - QK_DIM)),
        'gk': jnp.pad(vec['k_norm'][l], (0, HEAD_PAD - QK_DIM)),
    }
    for n in ('ffn1_norm', 'mix_norm', 'q_latent_norm', 'kv_latent_norm', 'conv_b', 'conv_ln_g', 'conv_ln_b',
              'ffn2_norm', 'post_norm'):
        ops[n] = vec[n][l]
    return ops


def _ffn_fwd(x, g, wgu, wd, fp):
    h = _rms_fwd(x, g, BF16, "rms_fwd_ffn")
    ab, z = _ffn_up(h, wgu, fp, "ffn_up")
    y = _mm(z, wd, res=x, scale=0.5, name="ffn_down")
    return y, (x, h, ab, z)


def _ffn_bwd(dy, dyb, saved, g, wgu, wd, fp, after_dw=None, after=None):
    x, h, ab, z = saved
    d_wd = _mm(z, dyb, ta=True, scale=0.5, blocks=('row', N_DEV), after=after, name="ffn_dwd")
    dab = _ffn_dab(dyb, wd, ab, fp, "ffn_dab")
    d_wgu = _mm(h, dab, ta=True, blocks=('col', N_DEV), name="ffn_dwgu")
    token = after_dw(d_wgu, d_wd) if after_dw is not None else None
    dh = _mm(dab, wgu, tb=True, after=token, name="ffn_dh")
    dx, dxb, dg = _rms_bwd(x, g, dh, dy, "rms_bwd_ffn")
    return dx, dxb, dg, d_wgu, d_wd


def _mixer_fwd(x, ops, tabs, after_attention=None):
    h = _rms_fwd(x, ops['mix_norm'], BF16, "rms_fwd_mix")
    p = _mm(h, ops['w_in'], name="mix_in")
    qln, kvln = _lat_norm_fwd(p, ops['q_latent_norm'], ops['kv_latent_norm'], "lat_norm_fwd")
    q_raw = _mm(qln, ops['w_q'], name="mix_q")
    k_raw = _mm(kvln, ops['w_k'], name="mix_k")
    v = _mm(kvln, ops['w_v'], out_dtype=BF16, name="mix_v")
    q, k = _qk_prep_fwd(q_raw, k_raw, p, ops['gq'], ops['gk'], tabs, "qk_prep_fwd")
    o, lse = _attn_fwd(q, k, v, "attn_fwd")
    token = after_attention(o) if after_attention is not None else None
    conv_b = ops['conv_b'] if token is None else _after(ops['conv_b'], token)
    y_conv, cv = _conv_fwd(p, ops['conv_w'], conv_b, ops['conv_ln_g'], ops['conv_ln_b'], "conv_fwd")
    x_attn = _mm(o, ops['wo_attn'], res=x, name="mix_out_attn")
    x_out = _mm(cv, ops['wo_conv'], res=x_attn, name="mix_out_conv")
    return x_out, (x, h, p, qln, kvln, q_raw, k_raw, v, q, k, o, lse, y_conv, cv)


def _mixer_bwd(dx_out, dxb_out, saved, ops, tabs, token=None):
    x, h, p, qln, kvln, q_raw, k_raw, v, q, k, o, lse, y_conv, cv = saved
    g = {}
    do = _mm(dxb_out, ops['wo_attn'], tb=True, after=token, name="mix_do")
    dcv = _mm(dxb_out, ops['wo_conv'], tb=True, name="mix_dcv")
    g['wo_attn'] = _mm(o, dxb_out, ta=True, name="mix_dwo_attn")
    g['wo_conv'] = _mm(cv, dxb_out, ta=True, name="mix_dwo_conv")
    dq, dk, dv = _attn_bwd(q, k, v, o, lse, do, "attn_bwd")
    dq_raw, dk_raw, dkpe, g['gq'], g['gk'] = _qk_prep_bwd(q_raw, k_raw, p, dq, dk, ops['gq'], ops['gk'], tabs,
                                                          "qk_prep_bwd")
    g['w_q'] = _mm(qln, dq_raw, ta=True, name="mix_dwq")
    g['w_k'] = _mm(kvln, dk_raw, ta=True, name="mix_dwk")
    g['w_v'] = _mm(kvln, dv, ta=True, name="mix_dwv")
    dqln = _mm(dq_raw, ops['w_q'], tb=True, name="mix_dqln")
    dkvln = _mm(dk_raw, ops['w_k'], tb=True, name="mix_dkvln_k")
    dkvln = _mm(dv, ops['w_v'], tb=True, res=dkvln, name="mix_dkvln_v")
    dp_lat, g['q_latent_norm'], g['kv_latent_norm'] = _lat_norm_bwd(
        p, ops['q_latent_norm'], ops['kv_latent_norm'], dqln, dkvln, "lat_norm_bwd")
    dy_conv, g['conv_ln_g'], g['conv_ln_b'], g['conv_b'] = _conv_bwd_ln(
        y_conv, dcv, ops['conv_ln_g'], ops['conv_ln_b'], "conv_bwd_ln")
    dag, g['conv_w'] = _conv_bwd_taps(p, dy_conv, ops['conv_w'], "conv_bwd_taps")
    dp = jnp.concatenate([dp_lat, dkpe, dag], axis=1)
    g['w_in'] = _mm(h, dp, ta=True, name="mix_dw_in")
    dh = _mm(dp, ops['w_in'], tb=True, name="mix_dh")
    dx, dxb, g['mix_norm'] = _rms_bwd(x, ops['mix_norm'], dh, dx_out, "rms_bwd_mix")
    return dx, dxb, g


def _mixer_grads_to_params(g):
    d_w_in = g['w_in']
    d_wk = _unpad_heads(g['w_k'], QK_NOPE).reshape(KV_LORA, N_HEADS, QK_NOPE)
    d_wv = _unpad_heads(g['w_v'], V_DIM).reshape(KV_LORA, N_HEADS, V_DIM)
    d_model = g['wo_attn'].shape[1]
    d_wo_attn = g['wo_attn'].reshape(N_HEADS, HEAD_PAD, d_model)[:, :V_DIM].reshape(N_HEADS * V_DIM, d_model)
    return {
        'mix_norm': g['mix_norm'],
        'w_in': jnp.concatenate([d_w_in[:, :LAT], d_w_in[:, LAT + QK_NOPE:LAT + QK_DIM], d_w_in[:, P_A:]], axis=1),
        'q_latent_norm': g['q_latent_norm'], 'w_uq': _unpad_heads(g['w_q'], QK_DIM),
        'kv_latent_norm': g['kv_latent_norm'],
        'w_ukv': jnp.concatenate([d_wk, d_wv], axis=2).reshape(KV_LORA, N_HEADS * (QK_NOPE + V_DIM)),
        'q_norm': g['gq'][:QK_DIM], 'k_norm': g['gk'][:QK_DIM], 'conv_w': g['conv_w'][:CONV_K],
        'conv_b': g['conv_b'], 'conv_ln_g': g['conv_ln_g'], 'conv_ln_b': g['conv_ln_b'],
        'w_out': jnp.concatenate([d_wo_attn, g['wo_conv']], axis=0),
    }


def kernel(x, ffn1_norm, ffn1_w_gate, ffn1_w_up, ffn1_w_down, mix_norm, w_in, q_latent_norm, w_uq, kv_latent_norm, w_ukv, q_norm, k_norm, conv_w, conv_b, conv_ln_g, conv_ln_b, w_out, ffn2_norm, ffn2_w_gate, ffn2_w_up, ffn2_w_down, post_norm, loss_target, m_ffn1_norm, m_ffn1_w_gate, m_ffn1_w_up, m_ffn1_w_down, m_mix_norm, m_w_in, m_q_latent_norm, m_w_uq, m_kv_latent_norm, m_w_ukv, m_q_norm, m_k_norm, m_conv_w, m_conv_b, m_conv_ln_g, m_conv_ln_b, m_w_out, m_ffn2_norm, m_ffn2_w_gate, m_ffn2_w_up, m_ffn2_w_down, m_post_norm, v_ffn1_norm, v_ffn1_w_gate, v_ffn1_w_up, v_ffn1_w_down, v_mix_norm, v_w_in, v_q_latent_norm, v_w_uq, v_kv_latent_norm, v_w_ukv, v_q_norm, v_k_norm, v_conv_w, v_conv_b, v_conv_ln_g, v_conv_ln_b, v_w_out, v_ffn2_norm, v_ffn2_w_gate, v_ffn2_w_up, v_ffn2_w_down, v_post_norm):
    args = locals()
    w = {n: args[n] for n in WEIGHTS}
    mom = {n: args["m_" + n] for n in WEIGHTS}
    var = {n: args["v_" + n] for n in WEIGHTS}
    depth = ffn1_norm.shape[0]
    x0 = x.reshape(x.shape[-2:])
    target = loss_target.reshape(loss_target.shape[-2:])
    t, d_model = x0.shape
    my_block = 4 * lax.axis_index("x") + 2 * lax.axis_index("y") + lax.axis_index("c")

    fb = ffn1_w_gate.shape[-1]
    fp = -(-fb // LANE) * LANE
    ffns = [(l, f) for l in range(depth) for f in (1, 2)]
    pad_cols = lambda a: jnp.pad(a, ((0, 0), (0, fp - fb)))
    gu_local = {(l, f): jnp.concatenate([pad_cols(w[f'ffn{f}_w_gate'][l]), pad_cols(w[f'ffn{f}_w_up'][l])],
                                        axis=1).astype(BF16) for l, f in ffns}
    dn_local = {(l, f): jnp.pad(w[f'ffn{f}_w_down'][l], ((0, fp - fb), (0, 0))).astype(BF16) for l, f in ffns}
    rows_of = {n: w[n].size // d_model for n in REST}
    rest_local = jnp.concatenate([_flat_rows(w[n].astype(BF16), d_model) for n in REST], axis=0)
    n_rest = rest_local.shape[0]
    first_ffn, later = ffns[0], ffns[1:]
    cw = conv_w.reshape(-1)
    cw_rows = -(-cw.size // (8 * LANE)) * 8
    cw_flat = jnp.pad(cw, (0, cw_rows * LANE - cw.size)).reshape(cw_rows, LANE)
    later_pieces = [(gu_local[q], 1) for q in later] + [(dn_local[q], 0) for q in later]
    n_later = len(later_pieces)
    got = _all_gather([(gu_local[first_ffn], 1), (dn_local[first_ffn], 0)], "gather_first")
    wgu, wd = {first_ffn: got[0]}, {first_ffn: got[1]}
    mixer_pieces = [(rest_local, 0), (cw_flat, 0)]
    mixer_plan, mixer_forward_plan = _gather_plans(mixer_pieces)
    gather_plan, forward_plan = _gather_plans(later_pieces)

    def landing(a, axis):
        return lax.empty(a.shape[:axis] + (N_DEV * a.shape[axis],) + a.shape[axis + 1:], a.dtype)

    gather_mixer = _start_copies([a for a, _ in mixer_pieces] + [landing(a, ax) for a, ax in mixer_pieces],
                                 4 * len(mixer_pieces), mixer_plan, "gather_mixer_start", after=got[0])
    gather_later = _start_copies([a for a, _ in later_pieces] + [landing(a, ax) for a, ax in later_pieces],
                                 4 * n_later, gather_plan, "gather_later_start", after=gather_mixer[3])

    x1_first, s1_first = _ffn_fwd(x0, _after(w['ffn1_norm'][0], gather_later[3]), wgu[first_ffn], wd[first_ffn], fp)
    lands = _wait_copies(gather_mixer, x1_first, 4 * len(mixer_pieces), mixer_plan, "gather_mixer_wait")[2:]
    pass_on = _start_copies(lands, FORWARD_COPIES * len(mixer_pieces), mixer_forward_plan, "gather_mixer_forward_start")
    lands = _wait_copies(pass_on, pass_on[3], FORWARD_COPIES * len(mixer_pieces), mixer_forward_plan, "gather_mixer_forward_wait")
    gathered = lands[0].reshape(N_DEV, n_rest, d_model)
    cw_all = lands[1].reshape(N_DEV, cw_rows * LANE)[:, :cw.size]
    full, start = {}, 0
    for n in REST:
        blocks = gathered[:, start:start + rows_of[n]].reshape((N_DEV,) + w[n].shape)
        full[n] = _full_from_blocks(blocks, n)
        start += rows_of[n]
    conv_w_full =jnp.transpose(cw_all.reshape((N_DEV,) + conv_w.shape), (1, 2, 0, 3)).reshape(depth, CONV_K, CONV_W)
    vec = {n: w[n] for n in VECTORS}
    ops = [_layer_operands(full, vec, conv_w_full, l) for l in range(depth)]
    tabs = _rope_tables(t)

    saved, xl = [], x0
    forward_later = []

    def pass_on_later(o_attn):
        lands = _wait_copies(gather_later, o_attn, 4 * n_later, gather_plan, "gather_later_wait")[n_later:]
        forward_later.append(_start_copies(lands, FORWARD_COPIES * n_later, forward_plan, "gather_later_forward_start"))
        return forward_later[0][3]

    for l in range(depth):
        o = ops[l]
        if l == 0:
            x1, s1 = x1_first, s1_first
            x2, sm = _mixer_fwd(x1, o, tabs, after_attention=pass_on_later)
            lands = _wait_copies(forward_later[0], x2, FORWARD_COPIES * n_later, forward_plan, "gather_later_forward_wait")
            wgu.update(zip(later, lands[:len(later)]))
            wd.update(zip(later, lands[len(later):]))
        else:
            x1, s1 = _ffn_fwd(xl, o['ffn1_norm'], wgu[l, 1], wd[l, 1], fp)
            x2, sm = _mixer_fwd(x1, o, tabs)
        x3, s2 = _ffn_fwd(x2, o['ffn2_norm'], wgu[l, 2], wd[l, 2], fp)
        xl = _rms_fwd(x3, o['post_norm'], F32, "rms_fwd_post")
        saved.append((s1, sm, s2, x3))
    loss_part, dx = _loss_head(xl, target, "loss_head")
    loss = lax.psum(loss_part[0, 0], ("x", "y", "c"))

    grads, mine_gu, mine_dn, in_flight = [None] * depth, {}, {}, {}

    def exchange(tag):
        def after_dw(d_wgu, d_wd):
            own = [d_wgu[0], d_wd[0]]
            if tag == last_tag:
                from_sibling, started = _reduce_start(own, [d_wgu[1], d_wd[1]], tag)
            else:
                from_sibling, started = None, _reduce_direct_start([d_wgu[1], d_wd[1]], tag)
            in_flight[tag] = (own, from_sibling, started)
            return started[3]
        return after_dw

    def finish(tag, after):
        own, from_sibling, started = in_flight.pop(tag)
        if from_sibling is None:
            return _reduce_direct_finish(own, started, after, tag)
        return _reduce_finish(own, from_sibling, started, after, tag)

    last_tag = f"{first_ffn[0]}{first_ffn[1]}"

    for l in reversed(range(depth)):
        o = ops[l]
        s1, sm, s2, x3 = saved[l]
        dx, dxb, d_post = _rms_bwd(x3, o['post_norm'], dx, None, "rms_bwd_post")
        dx, dxb, d_ffn2, _, _ = _ffn_bwd(dx, dxb, s2, o['ffn2_norm'], wgu[l, 2], wd[l, 2], fp, exchange(f"{l}2"))
        if l + 1 < depth:
            mine_gu[l + 1, 1], mine_dn[l + 1, 1] = finish(f"{l + 1}1", dx)
        dx, dxb, gm = _mixer_bwd(dx, dxb, sm, o, tabs)
        mine_gu[l, 2], mine_dn[l, 2] = finish(f"{l}2", dx)
        grads[l] = _mixer_grads_to_params(gm)
        if l == 0:
            rest_own = jnp.concatenate(
                [_blocks_from_full(jnp.stack([grads[k][n] for k in range(depth)]), n).reshape(N_DEV, rows_of[n], d_model)
                 for n in REST], axis=1)
            sibling_rest, started_rest = _reduce_start([rest_own], [rest_own.astype(BF16)], "rest")
        dx, dxb, d_ffn1, _, _ = _ffn_bwd(dx, dxb, s1, o['ffn1_norm'], wgu[l, 1], wd[l, 1], fp, exchange(f"{l}1"),
                                         after=started_rest[3] if l == 0 else None)
        grads[l].update(post_norm=d_post, ffn2_norm=d_ffn2, ffn1_norm=d_ffn1)
    grad_x = dx.reshape(x.shape)
    part = {n: jnp.stack([grads[l][n] for l in range(depth)]) for n in grads[0]}

    small = jnp.concatenate([part[n].reshape(-1) for n in VECTORS] + [part['conv_w'].reshape(-1)])
    s_rows = -(-small.size // (8 * LANE)) * 8
    small = jnp.pad(small, (0, s_rows * LANE - small.size)).reshape(s_rows, LANE)
    small_all = _all_gather([(small, 0)], "gather_small_grads", in_vmem=True)[0]
    small_sum = _sum_parts([small_all[k * s_rows:(k + 1) * s_rows] for k in range(N_DEV)], "sum_small_grads")

    mine_gu[first_ffn], mine_dn[first_ffn] = finish(f"{first_ffn[0]}{first_ffn[1]}", small_sum)
    mine_rest = _reduce_finish([rest_own], sibling_rest, started_rest, small_sum, "rest")[0]
    grad = {}
    for f in (1, 2):
        grad[f'ffn{f}_w_gate'] = jnp.stack([mine_gu[l, f][:, :fb] for l in range(depth)])
        grad[f'ffn{f}_w_up'] = jnp.stack([mine_gu[l, f][:, fp:fp + fb] for l in range(depth)])
        grad[f'ffn{f}_w_down'] = jnp.stack([mine_dn[l, f][:fb] for l in range(depth)])
    start = 0
    for n in REST:
        grad[n] = mine_rest[start:start + rows_of[n]].reshape(w[n].shape)
        start += rows_of[n]
    small_sum = small_sum.reshape(-1)
    start = 0
    for n in VECTORS:
        grad[n] = small_sum[start:start + w[n].size].reshape(w[n].shape)
        start += w[n].size
    cw_grad = small_sum[start:start + depth * CONV_K * CONV_W].reshape(depth, CONV_K, CONV_W)
    nb = conv_w.shape[-1]
    grad['conv_w'] = lax.dynamic_slice_in_dim(cw_grad, my_block * nb, nb, axis=2)

    delta, new_m, new_v = {}, {}, {}
    for n in BIG + ['conv_w']:
        shp = w[n].shape
        two_d = lambda a: a.reshape(-1, shp[-1])
        dl, mn, vn = _adamw(two_d(w[n]), two_d(grad[n]), two_d(mom[n]), two_d(var[n]), "adamw_" + n)
        delta[n], new_m[n], new_v[n] = dl.reshape(shp), mn.reshape(shp), vn.reshape(shp)
    vcat = lambda src: jnp.concatenate([src[n].reshape(-1) for n in VECTORS]).reshape(-1, LANE)
    dl, mn, vn = _adamw(vcat(w), vcat(grad), vcat(mom), vcat(var), "adamw_vectors")
    start = 0
    for n in VECTORS:
        sl = lambda a: a.reshape(-1)[start:start + w[n].size].reshape(w[n].shape)
        delta[n], new_m[n], new_v[n] = sl(dl), sl(mn), sl(vn)
        start += w[n].size

    return (loss, grad_x, *[grad[n] for n in WEIGHTS], *[delta[n] for n in WEIGHTS],
            *[new_m[n] for n in WEIGHTS], *[new_v[n] for n in WEIGHTS])
```

```python
import functools

import jax
import jax.numpy as jnp
from jax import lax
from jax.experimental import pallas as pl
from jax.experimental.pallas import tpu as pltpu

F32, BF16 = jnp.float32, jnp.bfloat16

N_DEV = 8
N_HEADS = 8
QK_NOPE, QK_ROPE, V_DIM = 64, 32, 64
QK_DIM = QK_NOPE + QK_ROPE
HEAD_PAD = 128
Q_LORA, KV_LORA = 384, 256
LAT = Q_LORA + KV_LORA
CONV_W, CONV_K = 512, 31
CONV_HALO = 32
CHUNK = 64
ROPE_THETA = 10000.0
EPS = 1e-6
ATTN_SCALE = QK_DIM ** -0.5
ATTN_SCALE_LOG2 = ATTN_SCALE * 1.4426950408889634
P_KPE = LAT
P_A = LAT + HEAD_PAD
P_G = P_A + CONV_W
P_COLS = P_G + CONV_W

ADAM_LR, ADAM_B1, ADAM_B2, ADAM_EPS, ADAM_WD, ADAM_STEP = 0.001, 0.9, 0.999, 1e-08, 0.01, 10

V7X_VMEM_BYTES = 64 << 20
VMEM_LIMIT = V7X_VMEM_BYTES - (8 << 20)
MM_VMEM_BUDGET = 36 << 20
LANE = 128
ROW_TILE = 512
ATTN_BLOCK = 512
ATTN_HEADS = 1
CONV_TILE = 256

MESH_ID = pl.DeviceIdType.MESH
ANY = pl.BlockSpec(memory_space=pl.ANY)
VMEM_SPEC = pl.BlockSpec(memory_space=pltpu.VMEM)
HBM_SPEC = pl.BlockSpec(memory_space=pltpu.HBM)
SEM_SPEC = pl.BlockSpec(memory_space=pltpu.SEMAPHORE)
DATAFLOW = pltpu.SideEffectType.DATAFLOW_SIDE_EFFECTING

WEIGHTS = ['ffn1_norm', 'ffn1_w_gate', 'ffn1_w_up', 'ffn1_w_down', 'mix_norm', 'w_in', 'q_latent_norm', 'w_uq',
           'kv_latent_norm', 'w_ukv', 'q_norm', 'k_norm', 'conv_w', 'conv_b', 'conv_ln_g', 'conv_ln_b', 'w_out',
           'ffn2_norm', 'ffn2_w_gate', 'ffn2_w_up', 'ffn2_w_down', 'post_norm']
COL_SHARDED = ['ffn1_w_gate', 'ffn1_w_up', 'w_in', 'w_uq', 'w_ukv', 'ffn2_w_gate', 'ffn2_w_up']
ROW_SHARDED = ['ffn1_w_down', 'w_out', 'ffn2_w_down']
REST = ['w_in', 'w_uq', 'w_ukv', 'w_out']
BIG = ['ffn1_w_gate', 'ffn1_w_up', 'ffn1_w_down', 'w_in', 'w_uq', 'w_ukv', 'w_out', 'ffn2_w_gate', 'ffn2_w_up',
       'ffn2_w_down']
VECTORS = ['ffn1_norm', 'mix_norm', 'q_latent_norm', 'kv_latent_norm', 'q_norm', 'k_norm', 'conv_b', 'conv_ln_g',
           'conv_ln_b', 'ffn2_norm', 'post_norm']


def _params(*sem):
    return pltpu.CompilerParams(dimension_semantics=sem if sem else None, vmem_limit_bytes=VMEM_LIMIT)


def _tile(n, cap):
    if n <= cap:
        return n
    best = 0
    for d in range(LANE, cap + 1, LANE):
        if n % d == 0:
            best = d
    assert best, (n, cap)
    return best


def _row_tile(n, cap=ROW_TILE, mult=8):
    if n <= cap:
        return n
    best = 0
    for d in range(mult, cap + 1, mult):
        if n % d == 0:
            best = d
    assert best, (n, cap)
    return best


def _mm(a, b, *, name, ta=False, tb=False, res=None, scale=1.0, out_dtype=F32, tm=None, tn=None, blocks=None,
        after=None):
    (kdim, m) = a.shape if ta else a.shape[::-1]
    (n, kb) = b.shape if tb else b.shape[::-1]
    assert kdim == kb, (a.shape, b.shape, ta, tb)
    tm, tn = tm or _tile(m, 512), tn or _tile(n, 1024)
    if blocks is not None:
        tm, tn = (tm, n // blocks[1]) if blocks[0] == 'col' else (m // blocks[1], tn)
    size = lambda arr: jnp.dtype(arr.dtype).itemsize
    out_bytes = tm * tn * ((6 if blocks is not None else jnp.dtype(out_dtype).itemsize) + (4 if res is not None else 0))

    def vmem_need(tk):
        return 2 * (tm * tk * size(a) + tk * tn * size(b) + out_bytes) + (tm * tn * 4 if tk < kdim else 0)

    tk = kdim
    for cand in [d for d in range(kdim - LANE, 0, -LANE) if kdim % d == 0]:
        if vmem_need(tk) <= MM_VMEM_BUDGET:
            break
        tk = cand
    nk = kdim // tk
    n_in = 2 + (res is not None) + (after is not None)
    n_out = 2 if blocks is not None else 1
    dims = (((0 if ta else 1,), (1 if tb else 0,)), ((), ()))

    def body(*refs):
        a_ref, b_ref = refs[0], refs[1]
        r_ref = refs[2] if res is not None else None
        o_refs = refs[n_in:n_in + n_out]
        acc_ref = refs[-1] if nk > 1 else None
        part = lax.dot_general(a_ref[...].astype(BF16), b_ref[...].astype(BF16), dims, preferred_element_type=F32)

        def finish(acc):
            if scale != 1.0:
                acc = acc * scale
            if r_ref is not None:
                acc = r_ref[...] + acc
            for o_ref in o_refs:
                o_ref[...] = acc.astype(o_ref.dtype)

        if nk == 1:
            finish(part)
        else:
            k = pl.program_id(2)

            @pl.when(k == 0)
            def _():
                acc_ref[...] = part

            @pl.when(k > 0)
            def _():
                acc_ref[...] += part

            @pl.when(k == nk - 1)
            def _():
                finish(acc_ref[...])

    a_spec = pl.BlockSpec((tk, tm), lambda i, j, k: (k, i)) if ta else pl.BlockSpec((tm, tk), lambda i, j, k: (i, k))
    b_spec = pl.BlockSpec((tn, tk), lambda i, j, k: (j, k)) if tb else pl.BlockSpec((tk, tn), lambda i, j, k: (k, j))
    plain = pl.BlockSpec((tm, tn), lambda i, j, k: (i, j))
    if blocks is None:
        out_specs, out_shape = plain, jax.ShapeDtypeStruct((m, n), out_dtype)
    else:
        if blocks[0] == 'col':
            o_spec, shp = pl.BlockSpec((None, tm, tn), lambda i, j, k: (j, i, 0)), (blocks[1], m, tn)
        else:
            o_spec, shp = pl.BlockSpec((None, tm, tn), lambda i, j, k: (i, 0, j)), (blocks[1], tm, n)
        out_specs, out_shape = (o_spec, o_spec), (jax.ShapeDtypeStruct(shp, F32), jax.ShapeDtypeStruct(shp, BF16))
    in_specs = [a_spec, b_spec] + ([plain] if res is not None else [])
    args = (a, b) + ((res,) if res is not None else ())
    if after is not None:
        in_specs.append(pl.BlockSpec(after.shape, lambda i, j, k: (0, 0)))
        args += (after,)
    return pl.pallas_call(
        body, name=name, grid=(m // tm, n // tn, nk), in_specs=in_specs, out_specs=out_specs, out_shape=out_shape,
        scratch_shapes=[pltpu.VMEM((tm, tn), F32)] if nk > 1 else [],
        compiler_params=_params("parallel", "parallel", "arbitrary"),
    )(*args)


def _rms_fwd(x, g, out_dtype, name):
    t, d = x.shape
    tm = _row_tile(t)

    def body(x_ref, g_ref, o_ref):
        xv = x_ref[...]
        r = lax.rsqrt(jnp.mean(xv * xv, axis=-1, keepdims=True) + EPS)
        o_ref[...] = (xv * r * g_ref[...]).astype(o_ref.dtype)

    return pl.pallas_call(
        body, name=name, grid=(t // tm,),
        in_specs=[pl.BlockSpec((tm, d), lambda i: (i, 0)), pl.BlockSpec((1, d), lambda i: (0, 0))],
        out_specs=pl.BlockSpec((tm, d), lambda i: (i, 0)),
        out_shape=jax.ShapeDtypeStruct((t, d), out_dtype), compiler_params=_params("parallel"),
    )(x, g.reshape(1, d))


def _rms_bwd(x, g, dh, res, name):
    t, d = x.shape
    tm = _row_tile(t)

    def body(*refs):
        x_ref, g_ref, dh_ref = refs[:3]
        r_ref = refs[3] if res is not None else None
        dx_ref, dxb_ref, dg_ref = refs[-3:]
        xv, dhv = x_ref[...], dh_ref[...]
        r = lax.rsqrt(jnp.mean(xv * xv, axis=-1, keepdims=True) + EPS)
        y = xv * r
        dy = dhv * g_ref[...]
        dx = r * (dy - y * jnp.mean(dy * y, axis=-1, keepdims=True))
        if r_ref is not None:
            dx = r_ref[...] + dx
        dx_ref[...] = dx
        dxb_ref[...] = dx.astype(BF16)

        @pl.when(pl.program_id(0) == 0)
        def _():
            dg_ref[...] = jnp.zeros_like(dg_ref)

        dg_ref[...] += jnp.sum(dhv * y, axis=0, keepdims=True)

    row = pl.BlockSpec((tm, d), lambda i: (i, 0))
    vec = pl.BlockSpec((1, d), lambda i: (0, 0))
    args = (x, g.reshape(1, d), dh) + ((res,) if res is not None else ())
    dx, dxb, dg = pl.pallas_call(
        body, name=name, grid=(t // tm,), in_specs=[row, vec, row] + ([row] if res is not None else []),
        out_specs=(row, row, vec),
        out_shape=(jax.ShapeDtypeStruct((t, d), F32), jax.ShapeDtypeStruct((t, d), BF16),
                   jax.ShapeDtypeStruct((1, d), F32)),
        compiler_params=_params("arbitrary"),
    )(*args)
    return dx, dxb, dg.reshape(d)


FFN_PAIR = 2


def _ffn_up(h, wgu, fp, name):
    t, d = h.shape
    tm, tn = _tile(t, 512), FFN_PAIR * 2 * fp
    nj = wgu.shape[1] // tn

    def body(h_ref, w_ref, fac_ref, z_ref):
        ab = jnp.dot(h_ref[...], w_ref[...], preferred_element_type=F32)
        for e in range(FFN_PAIR):
            av, bv = ab[:, 2 * fp * e:2 * fp * e + fp], ab[:, 2 * fp * e + fp:2 * fp * (e + 1)]
            s = jax.nn.sigmoid(av)
            silu = av * s
            z_ref[:, fp * e:fp * (e + 1)] = (silu * bv).astype(z_ref.dtype)
            fac_ref[:, 2 * fp * e:2 * fp * e + fp] = silu.astype(fac_ref.dtype)
            fac_ref[:, 2 * fp * e + fp:2 * fp * (e + 1)] = (bv * (s + silu * (1.0 - s))).astype(fac_ref.dtype)

    return pl.pallas_call(
        body, name=name, grid=(nj, t // tm),
        in_specs=[pl.BlockSpec((tm, d), lambda j, i: (i, 0)), pl.BlockSpec((d, tn), lambda j, i: (0, j))],
        out_specs=(pl.BlockSpec((tm, tn), lambda j, i: (i, j)), pl.BlockSpec((tm, tn // 2), lambda j, i: (i, j))),
        out_shape=(jax.ShapeDtypeStruct((t, wgu.shape[1]), BF16), jax.ShapeDtypeStruct((t, wgu.shape[1] // 2), BF16)),
        compiler_params=_params("parallel", "parallel"),
    )(h, wgu)


def _ffn_dab(dyb, wd, ab, fp, name):
    t, d = dyb.shape
    tm, tn = _tile(t, 512), FFN_PAIR * 2 * fp
    nj = ab.shape[1] // tn

    def body(dy_ref, wd_ref, fac_ref, dab_ref):
        dz = _dot_nt(dy_ref[...], wd_ref[...]) * 0.5
        for e in range(FFN_PAIR):
            dze = dz[:, fp * e:fp * (e + 1)]
            d_up = fac_ref[:, 2 * fp * e:2 * fp * e + fp].astype(F32)
            d_gate = fac_ref[:, 2 * fp * e + fp:2 * fp * (e + 1)].astype(F32)
            dab_ref[:, 2 * fp * e:2 * fp * e + fp] = (dze * d_gate).astype(dab_ref.dtype)
            dab_ref[:, 2 * fp * e + fp:2 * fp * (e + 1)] = (dze * d_up).astype(dab_ref.dtype)

    return pl.pallas_call(
        body, name=name, grid=(nj, t // tm),
        in_specs=[pl.BlockSpec((tm, d), lambda j, i: (i, 0)), pl.BlockSpec((tn // 2, d), lambda j, i: (j, 0)),
                  pl.BlockSpec((tm, tn), lambda j, i: (i, j))],
        out_specs=pl.BlockSpec((tm, tn), lambda j, i: (i, j)),
        out_shape=jax.ShapeDtypeStruct(ab.shape, BF16), compiler_params=_params("parallel", "parallel"),
    )(dyb, wd, ab)


def _matmul_rms_bwd(dab, wgu, x, g, dy, name, after=None):
    t, kdim = dab.shape
    d = wgu.shape[0]
    tm = _tile(t, 256)

    def body(a_ref, b_ref, x_ref, g_ref, dy_ref, *refs):
        dx_ref, dxb_ref, dg_ref = refs[-3:]
        dh = _dot_nt(a_ref[...].astype(BF16), b_ref[...])
        xv = x_ref[...]
        r = lax.rsqrt(jnp.mean(xv * xv, axis=-1, keepdims=True) + EPS)
        y = xv * r
        dyn = dh * g_ref[...]
        dx = dy_ref[...] + r * (dyn - y * jnp.mean(dyn * y, axis=-1, keepdims=True))
        dx_ref[...] = dx
        dxb_ref[...] = dx.astype(BF16)

        @pl.when(pl.program_id(0) == 0)
        def _():
            dg_ref[...] = jnp.zeros_like(dg_ref)

        dg_ref[...] += jnp.sum(dh * y, axis=0, keepdims=True)

    row = pl.BlockSpec((tm, d), lambda i: (i, 0))
    vec = pl.BlockSpec((1, d), lambda i: (0, 0))
    in_specs = [pl.BlockSpec((tm, kdim), lambda i: (i, 0)), pl.BlockSpec((d, kdim), lambda i: (0, 0)), row, vec, row]
    args = [dab, wgu, x, g.reshape(1, d), dy]
    if after is not None:
        in_specs.append(pl.BlockSpec(after.shape, lambda i: (0, 0)))
        args.append(after)
    dx, dxb, dg = pl.pallas_call(
        body, name=name, grid=(t // tm,), in_specs=in_specs, out_specs=(row, row, vec),
        out_shape=(jax.ShapeDtypeStruct((t, d), F32), jax.ShapeDtypeStruct((t, d), BF16),
                   jax.ShapeDtypeStruct((1, d), F32)),
        compiler_params=_params("arbitrary"),
    )(*args)
    return dx, dxb, dg.reshape(d)


def _lat_norm_fwd(p, g_q, g_kv, name):
    t = p.shape[0]
    tm = _row_tile(t)

    def body(p_ref, gq_ref, gkv_ref, q_ref, kv_ref):
        for lo, hi, g_ref, o_ref in ((0, Q_LORA, gq_ref, q_ref), (Q_LORA, LAT, gkv_ref, kv_ref)):
            xv = p_ref[:, lo:hi]
            r = lax.rsqrt(jnp.mean(xv * xv, axis=-1, keepdims=True) + EPS)
            o_ref[...] = (xv * r * g_ref[...]).astype(o_ref.dtype)

    return pl.pallas_call(
        body, name=name, grid=(t // tm,),
        in_specs=[pl.BlockSpec((tm, P_COLS), lambda i: (i, 0)), pl.BlockSpec((1, Q_LORA), lambda i: (0, 0)),
                  pl.BlockSpec((1, KV_LORA), lambda i: (0, 0))],
        out_specs=(pl.BlockSpec((tm, Q_LORA), lambda i: (i, 0)), pl.BlockSpec((tm, KV_LORA), lambda i: (i, 0))),
        out_shape=(jax.ShapeDtypeStruct((t, Q_LORA), BF16), jax.ShapeDtypeStruct((t, KV_LORA), BF16)),
        compiler_params=_params("parallel"),
    )(p, g_q.reshape(1, Q_LORA), g_kv.reshape(1, KV_LORA))


def _lat_norm_bwd(p, g_q, g_kv, dq, dkv, name):
    t = p.shape[0]
    tm = _row_tile(t)

    def body(p_ref, gq_ref, gkv_ref, dq_ref, dkv_ref, dp_ref, dgq_ref, dgkv_ref):
        first = pl.program_id(0) == 0
        for lo, hi, g_ref, d_ref, dg_ref in ((0, Q_LORA, gq_ref, dq_ref, dgq_ref),
                                             (Q_LORA, LAT, gkv_ref, dkv_ref, dgkv_ref)):
            xv, dhv = p_ref[:, lo:hi], d_ref[...]
            r = lax.rsqrt(jnp.mean(xv * xv, axis=-1, keepdims=True) + EPS)
            y = xv * r
            dy = dhv * g_ref[...]
            dp_ref[:, lo:hi] = r * (dy - y * jnp.mean(dy * y, axis=-1, keepdims=True))

            @pl.when(first)
            def _():
                dg_ref[...] = jnp.zeros_like(dg_ref)

            dg_ref[...] += jnp.sum(dhv * y, axis=0, keepdims=True)

    vq = pl.BlockSpec((1, Q_LORA), lambda i: (0, 0))
    vkv = pl.BlockSpec((1, KV_LORA), lambda i: (0, 0))
    dp, dgq, dgkv = pl.pallas_call(
        body, name=name, grid=(t // tm,),
        in_specs=[pl.BlockSpec((tm, P_COLS), lambda i: (i, 0)), vq, vkv,
                  pl.BlockSpec((tm, Q_LORA), lambda i: (i, 0)), pl.BlockSpec((tm, KV_LORA), lambda i: (i, 0))],
        out_specs=(pl.BlockSpec((tm, LAT), lambda i: (i, 0)), vq, vkv),
        out_shape=(jax.ShapeDtypeStruct((t, LAT), F32), jax.ShapeDtypeStruct((1, Q_LORA), F32),
                   jax.ShapeDtypeStruct((1, KV_LORA), F32)),
        compiler_params=_params("arbitrary"),
    )(p, g_q.reshape(1, Q_LORA), g_kv.reshape(1, KV_LORA), dq, dkv)
    return dp, dgq.reshape(Q_LORA), dgkv.reshape(KV_LORA)


def _rope_tables(t):
    half = QK_ROPE // 2
    pos = jnp.arange(t, dtype=F32)
    inv_freq = 1.0 / (ROPE_THETA ** (jnp.arange(0, QK_ROPE, 2, dtype=F32) / QK_ROPE))
    ang = pos[:, None] * inv_freq[None, :]
    cos, sin = jnp.cos(ang), jnp.sin(ang)
    z = lambda n: jnp.zeros((t, n), F32)
    c_tab = jnp.concatenate([jnp.ones((t, QK_NOPE), F32), cos, cos, z(HEAD_PAD - QK_DIM)], axis=1)
    sa_tab = jnp.concatenate([z(QK_NOPE), -sin, z(half), z(HEAD_PAD - QK_DIM)], axis=1)
    sb_tab = jnp.concatenate([z(QK_NOPE), z(half), sin, z(HEAD_PAD - QK_DIM)], axis=1)
    return c_tab, sa_tab, sb_tab


def _rope(x, c, sa, sb):
    half = QK_ROPE // 2
    return x * c + pltpu.roll(x, HEAD_PAD - half, 1) * sa + pltpu.roll(x, half, 1) * sb


def _rope_t(d, c, sa, sb):
    half = QK_ROPE // 2
    return d * c + pltpu.roll(d * sa, half, 1) + pltpu.roll(d * sb, HEAD_PAD - half, 1)


def _head_rms(x):
    r = lax.rsqrt(jnp.sum(x * x, axis=-1, keepdims=True) * (1.0 / QK_DIM) + EPS)
    return x * r, r


def _qk_prep_fwd(q_raw, k_raw, p, gq, gk, tabs, name):
    t, width = q_raw.shape
    tm = _row_tile(t)

    def body(q_ref, k_ref, p_ref, gq_ref, gk_ref, c_ref, sa_ref, sb_ref, qo_ref, ko_ref):
        c, sa, sb, kpe = c_ref[...], sa_ref[...], sb_ref[...], p_ref[...]
        for h in range(N_HEADS):
            cols = slice(h * HEAD_PAD, (h + 1) * HEAD_PAD)
            qn, _ = _head_rms(q_ref[:, cols])
            qo_ref[:, cols] = _rope(qn * gq_ref[...], c, sa, sb).astype(qo_ref.dtype)
            kn, _ = _head_rms(k_ref[:, cols] + kpe)
            ko_ref[:, cols] = _rope(kn * gk_ref[...], c, sa, sb).astype(ko_ref.dtype)

    rows = pl.BlockSpec((tm, width), lambda i: (i, 0))
    tab = pl.BlockSpec((tm, HEAD_PAD), lambda i: (i, 0))
    vec = pl.BlockSpec((1, HEAD_PAD), lambda i: (0, 0))
    kpe_spec = pl.BlockSpec((tm, HEAD_PAD), lambda i: (i, P_KPE // HEAD_PAD))
    return pl.pallas_call(
        body, name=name, grid=(t // tm,), in_specs=[rows, rows, kpe_spec, vec, vec, tab, tab, tab],
        out_specs=(rows, rows),
        out_shape=(jax.ShapeDtypeStruct(q_raw.shape, BF16), jax.ShapeDtypeStruct(k_raw.shape, BF16)),
        compiler_params=_params("parallel"),
    )(q_raw, k_raw, p, gq.reshape(1, HEAD_PAD), gk.reshape(1, HEAD_PAD), *tabs)


def _qk_prep_bwd(q_raw, k_raw, p, dq, dk, gq, gk, tabs, name):
    t, width = q_raw.shape
    tm = _row_tile(t, 256)

    def body(q_ref, k_ref, p_ref, dq_ref, dk_ref, gq_ref, gk_ref, c_ref, sa_ref, sb_ref,
             dqr_ref, dkr_ref, dkpe_ref, dgq_ref, dgk_ref):
        c, sa, sb, kpe = c_ref[...], sa_ref[...], sb_ref[...], p_ref[...]

        def one(x, d, g_ref):
            n, r = _head_rms(x)
            dng = _rope_t(d, c, sa, sb)
            dn = dng * g_ref[...]
            dx = r * (dn - n * (jnp.sum(dn * n, axis=-1, keepdims=True) * (1.0 / QK_DIM)))
            return dx, jnp.sum(dng * n, axis=0, keepdims=True)

        dgq = dgk = dkpe = None
        for h in range(N_HEADS):
            cols = slice(h * HEAD_PAD, (h + 1) * HEAD_PAD)
            dqr, gq_part = one(q_ref[:, cols], dq_ref[:, cols], gq_ref)
            dkr, gk_part = one(k_ref[:, cols] + kpe, dk_ref[:, cols], gk_ref)
            dqr_ref[:, cols] = dqr
            dkr_ref[:, cols] = dkr
            dgq = gq_part if dgq is None else dgq + gq_part
            dgk = gk_part if dgk is None else dgk + gk_part
            dkpe = dkr if dkpe is None else dkpe + dkr
        dkpe_ref[...] = dkpe

        @pl.when(pl.program_id(0) == 0)
        def _():
            dgq_ref[...] = jnp.zeros_like(dgq_ref)
            dgk_ref[...] = jnp.zeros_like(dgk_ref)

        dgq_ref[...] += dgq
        dgk_ref[...] += dgk

    head = pl.BlockSpec((tm, width), lambda i: (i, 0))
    tab = pl.BlockSpec((tm, HEAD_PAD), lambda i: (i, 0))
    vec = pl.BlockSpec((1, HEAD_PAD), lambda i: (0, 0))
    kpe = pl.BlockSpec((tm, HEAD_PAD), lambda i: (i, P_KPE // HEAD_PAD))
    dqr, dkr, dkpe, dgq, dgk = pl.pallas_call(
        body, name=name, grid=(t // tm,), in_specs=[head, head, kpe, head, head, vec, vec, tab, tab, tab],
        out_specs=(head, head, tab, vec, vec),
        out_shape=(jax.ShapeDtypeStruct(q_raw.shape, F32), jax.ShapeDtypeStruct(k_raw.shape, F32),
                   jax.ShapeDtypeStruct((t, HEAD_PAD), F32), jax.ShapeDtypeStruct((1, HEAD_PAD), F32),
                   jax.ShapeDtypeStruct((1, HEAD_PAD), F32)),
        compiler_params=_params("arbitrary"),
    )(q_raw, k_raw, p, dq, dk, gq.reshape(1, HEAD_PAD), gk.reshape(1, HEAD_PAD), *tabs)
    return dqr, dkr, dkpe, dgq.reshape(HEAD_PAD), dgk.reshape(HEAD_PAD)


def _dot_nt(a, b):
    return lax.dot_general(a, b, (((1,), (1,)), ((), ())), preferred_element_type=F32)


def _dot_tn(a, b):
    return lax.dot_general(a, b, (((0,), (0,)), ((), ())), preferred_element_type=F32)


def _diag_mask():
    rows = lax.broadcasted_iota(jnp.int32, (ATTN_BLOCK, ATTN_BLOCK), 0) // CHUNK
    cols = lax.broadcasted_iota(jnp.int32, (ATTN_BLOCK, ATTN_BLOCK), 1) // CHUNK
    return cols <= rows


def _attn_fwd(q, k, v, name):
    t = q.shape[0]
    bq = ATTN_BLOCK
    nq = t // bq

    width = ATTN_HEADS * HEAD_PAD

    def body(q_ref, k_ref, v_ref, o_ref, lse_ref):
        i = pl.program_id(1)
        heads = [slice(e * HEAD_PAD, (e + 1) * HEAD_PAD) for e in range(ATTN_HEADS)]
        qv = [q_ref[:, cols] for cols in heads]

        def block(j, carries, masked):
            rows = pl.ds(pl.multiple_of(j * bq, bq), bq)
            out = []
            for e, (m, l, acc) in enumerate(carries):
                s = _dot_nt(qv[e], k_ref[rows, heads[e]]) * ATTN_SCALE_LOG2
                if masked:
                    s = jnp.where(_diag_mask(), s, -1e30)
                m_new = jnp.maximum(m, jnp.max(s, axis=-1, keepdims=True))
                alpha = jnp.exp2(m - m_new)
                pe = jnp.exp2(s - m_new)
                l = alpha * l + jnp.sum(pe, axis=-1, keepdims=True)
                acc = alpha * acc + jnp.dot(pe.astype(BF16), v_ref[rows, heads[e]], preferred_element_type=F32)
                out.append((m_new, l, acc))
            return tuple(out)

        init = tuple((jnp.full((bq, 1), -1e30, F32), jnp.zeros((bq, 1), F32), jnp.zeros((bq, HEAD_PAD), F32))
                     for _ in heads)
        carries = lax.fori_loop(0, i, lambda j, cr: block(j, cr, False), init)
        for cols, (m, l, acc) in zip(heads, block(i, carries, True)):
            o_ref[:, cols] = acc / l
            lse_ref[:, cols] = jnp.broadcast_to(m + jnp.log2(l), (bq, HEAD_PAD))

    blk = pl.BlockSpec((bq, width), lambda h, i: (i, h))
    full = pl.BlockSpec((t, width), lambda h, i: (0, h))
    return pl.pallas_call(
        body, name=name, grid=(N_HEADS // ATTN_HEADS, nq), in_specs=[blk, full, full], out_specs=(blk, blk),
        out_shape=(jax.ShapeDtypeStruct(q.shape, F32), jax.ShapeDtypeStruct(q.shape, F32)),
        compiler_params=_params("parallel", "parallel"),
    )(q, k, v)


def _attn_bwd(q, k, v, o, lse, do, name):
    t = q.shape[0]
    bq = ATTN_BLOCK
    nq = t // bq

    def body(q_ref, k_ref, v_ref, o_ref, lse_ref, do_ref, dq_ref, dk_ref, dv_ref, delta_ref):
        def rows_of(i):
            return pl.ds(pl.multiple_of(i * bq, bq), bq)

        def prep(i, _):
            r = rows_of(i)
            delta_ref[r, :] = jnp.broadcast_to(jnp.sum(do_ref[r, :] * o_ref[r, :], axis=-1, keepdims=True),
                                               (bq, HEAD_PAD))
            dq_ref[r, :] = jnp.zeros((bq, HEAD_PAD), F32)
            return 0

        lax.fori_loop(0, nq, prep, 0)

        def key_block(j, _):
            rj = rows_of(j)
            kb, vb = k_ref[rj, :], v_ref[rj, :]

            def query_block(i, carry, masked):
                dk, dv = carry
                ri = rows_of(i)
                qb, dob = q_ref[ri, :], do_ref[ri, :].astype(BF16)
                s = _dot_nt(qb, kb) * ATTN_SCALE_LOG2
                if masked:
                    s = jnp.where(_diag_mask(), s, -1e30)
                pe = jnp.exp2(s - lse_ref[ri, :][:, :1])
                dp = _dot_nt(dob, vb)
                ds = (pe * (dp - delta_ref[ri, :][:, :1]) * ATTN_SCALE).astype(BF16)
                dq_ref[ri, :] += jnp.dot(ds, kb, preferred_element_type=F32)
                return dk + _dot_tn(ds, qb), dv + _dot_tn(pe.astype(BF16), dob)

            zero = jnp.zeros((bq, HEAD_PAD), F32)
            carry = query_block(j, (zero, zero), True)
            dk, dv = lax.fori_loop(j + 1, nq, lambda i, cr: query_block(i, cr, False), carry)
            dk_ref[rj, :] = dk
            dv_ref[rj, :] = dv
            return 0

        lax.fori_loop(0, nq, key_block, 0)

    full = pl.BlockSpec((t, HEAD_PAD), lambda h: (0, h))
    shp = jax.ShapeDtypeStruct(q.shape, F32)
    return pl.pallas_call(
        body, name=name, grid=(N_HEADS,), in_specs=[full] * 6, out_specs=(full, full, full),
        out_shape=(shp, shp, shp), scratch_shapes=[pltpu.VMEM((t, HEAD_PAD), F32)],
        compiler_params=_params("parallel"),
    )(q, k, v, o, lse, do)


def _glu_ext(pc_ref, pp_ref, u_ref, tm, first):
    u_ref[CONV_HALO:CONV_HALO + tm, :] = pc_ref[:, P_A:P_G] * jax.nn.sigmoid(pc_ref[:, P_G:P_COLS])
    up = pp_ref[tm - CONV_HALO:tm, P_A:P_G] * jax.nn.sigmoid(pp_ref[tm - CONV_HALO:tm, P_G:P_COLS])
    u_ref[0:CONV_HALO, :] = jnp.where(first, 0.0, up)


SUBLANES = 8


def _shift_copies(src_ref, sh_ref):
    rows = sh_ref.shape[1]
    for b in range(1, SUBLANES):
        sh_ref[b - 1, :, :] = src_ref[b:b + rows, :]


def _rows_at(src_ref, sh_ref, start, n):
    a, b = divmod(start, SUBLANES)
    if b == 0:
        return src_ref[SUBLANES * a:SUBLANES * a + n, :]
    return sh_ref[b - 1, SUBLANES * a:SUBLANES * a + n, :]


def _conv_fwd(p, w, b, ln_g, ln_b, name):
    t = p.shape[0]
    tm = _row_tile(t, CONV_TILE)
    off = CONV_HALO - (CONV_K - 1)

    def body(pc_ref, pp_ref, w_ref, b_ref, g_ref, bb_ref, y_ref, o_ref, u_ref, ush_ref):
        _glu_ext(pc_ref, pp_ref, u_ref, tm, pl.program_id(0) == 0)
        _shift_copies(u_ref, ush_ref)
        acc = jnp.zeros((tm, CONV_W), F32)
        for kk in range(CONV_K):
            acc = acc + w_ref[kk:kk + 1, :] * _rows_at(u_ref, ush_ref, off + kk, tm)
        y = acc + b_ref[...]
        y_ref[...] = y
        xc = y - jnp.mean(y, axis=-1, keepdims=True)
        lo = xc * lax.rsqrt(jnp.mean(xc * xc, axis=-1, keepdims=True) + EPS) * g_ref[...] + bb_ref[...]
        o_ref[...] = (lo * jax.nn.sigmoid(lo)).astype(o_ref.dtype)

    prow = pl.BlockSpec((tm, P_COLS), lambda i: (i, 0))
    pprev = pl.BlockSpec((tm, P_COLS), lambda i: (jnp.maximum(i - 1, 0), 0))
    vec = pl.BlockSpec((1, CONV_W), lambda i: (0, 0))
    row = pl.BlockSpec((tm, CONV_W), lambda i: (i, 0))
    return pl.pallas_call(
        body, name=name, grid=(t // tm,),
        in_specs=[prow, pprev, pl.BlockSpec((CONV_HALO, CONV_W), lambda i: (0, 0)), vec, vec, vec],
        out_specs=(row, row),
        out_shape=(jax.ShapeDtypeStruct((t, CONV_W), F32), jax.ShapeDtypeStruct((t, CONV_W), BF16)),
        scratch_shapes=[pltpu.VMEM((tm + CONV_HALO, CONV_W), F32),
                        pltpu.VMEM((SUBLANES - 1, tm + CONV_HALO - SUBLANES, CONV_W), F32)],
        compiler_params=_params("parallel"),
    )(p, p, w, b.reshape(1, CONV_W), ln_g.reshape(1, CONV_W), ln_b.reshape(1, CONV_W))


def _conv_bwd_ln(y, dout, ln_g, ln_b, name):
    t = y.shape[0]
    tm = _row_tile(t)

    def body(y_ref, d_ref, g_ref, bb_ref, dy_ref, dg_ref, db_ref, dcb_ref):
        yv = y_ref[...]
        xc = yv - jnp.mean(yv, axis=-1, keepdims=True)
        r = lax.rsqrt(jnp.mean(xc * xc, axis=-1, keepdims=True) + EPS)
        n = xc * r
        lo = n * g_ref[...] + bb_ref[...]
        s = jax.nn.sigmoid(lo)
        dlo = d_ref[...] * (s * (1.0 + lo * (1.0 - s)))
        dn = dlo * g_ref[...]
        dy = r * (dn - jnp.mean(dn, axis=-1, keepdims=True) - n * jnp.mean(dn * n, axis=-1, keepdims=True))
        dy_ref[...] = dy

        @pl.when(pl.program_id(0) == 0)
        def _():
            dg_ref[...] = jnp.zeros_like(dg_ref)
            db_ref[...] = jnp.zeros_like(db_ref)
            dcb_ref[...] = jnp.zeros_like(dcb_ref)

        dg_ref[...] += jnp.sum(dlo * n, axis=0, keepdims=True)
        db_ref[...] += jnp.sum(dlo, axis=0, keepdims=True)
        dcb_ref[...] += jnp.sum(dy, axis=0, keepdims=True)

    row = pl.BlockSpec((tm, CONV_W), lambda i: (i, 0))
    vec = pl.BlockSpec((1, CONV_W), lambda i: (0, 0))
    vshape = jax.ShapeDtypeStruct((1, CONV_W), F32)
    dy, dg, db, dcb = pl.pallas_call(
        body, name=name, grid=(t // tm,), in_specs=[row, row, vec, vec], out_specs=(row, vec, vec, vec),
        out_shape=(jax.ShapeDtypeStruct((t, CONV_W), F32), vshape, vshape, vshape),
        compiler_params=_params("arbitrary"),
    )(y, dout, ln_g.reshape(1, CONV_W), ln_b.reshape(1, CONV_W))
    return dy, dg.reshape(CONV_W), db.reshape(CONV_W), dcb.reshape(CONV_W)


def _conv_bwd_taps(p, dy, w, name):
    t = p.shape[0]
    tm = _row_tile(t, CONV_TILE)
    nt = t // tm
    off = CONV_HALO - (CONV_K - 1)

    def body(pc_ref, pp_ref, dyc_ref, dyn_ref, w_ref, dag_ref, dw_ref, u_ref, dye_ref, ush_ref, dysh_ref):
        i = pl.program_id(0)
        _glu_ext(pc_ref, pp_ref, u_ref, tm, i == 0)
        dyc = dyc_ref[...]
        dye_ref[0:tm, :] = dyc
        dye_ref[tm:tm + CONV_HALO, :] = jnp.where(i == nt - 1, 0.0, dyn_ref[0:CONV_HALO, :])

        _shift_copies(u_ref, ush_ref)
        _shift_copies(dye_ref, dysh_ref)

        @pl.when(i == 0)
        def _():
            dw_ref[...] = jnp.zeros_like(dw_ref)

        du = jnp.zeros((tm, CONV_W), F32)
        for kk in range(CONV_K):
            dw_ref[kk:kk + 1, :] += jnp.sum(dyc * _rows_at(u_ref, ush_ref, off + kk, tm), axis=0, keepdims=True)
            du = du + w_ref[kk:kk + 1, :] * _rows_at(dye_ref, dysh_ref, CONV_K - 1 - kk, tm)
        av, gv = pc_ref[:, P_A:P_G], pc_ref[:, P_G:P_COLS]
        s = jax.nn.sigmoid(gv)
        dag_ref[:, 0:CONV_W] = du * s
        dag_ref[:, CONV_W:2 * CONV_W] = du * av * (s * (1.0 - s))

    prow = pl.BlockSpec((tm, P_COLS), lambda i: (i, 0))
    pprev = pl.BlockSpec((tm, P_COLS), lambda i: (jnp.maximum(i - 1, 0), 0))
    row = pl.BlockSpec((tm, CONV_W), lambda i: (i, 0))
    nxt = pl.BlockSpec((tm, CONV_W), lambda i: (jnp.minimum(i + 1, nt - 1), 0))
    wspec = pl.BlockSpec((CONV_HALO, CONV_W), lambda i: (0, 0))
    return pl.pallas_call(
        body, name=name, grid=(nt,), in_specs=[prow, pprev, row, nxt, wspec],
        out_specs=(pl.BlockSpec((tm, 2 * CONV_W), lambda i: (i, 0)), wspec),
        out_shape=(jax.ShapeDtypeStruct((t, 2 * CONV_W), F32), jax.ShapeDtypeStruct((CONV_HALO, CONV_W), F32)),
        scratch_shapes=[pltpu.VMEM((tm + CONV_HALO, CONV_W), F32), pltpu.VMEM((tm + CONV_HALO, CONV_W), F32),
                        pltpu.VMEM((SUBLANES - 1, tm + CONV_HALO - SUBLANES, CONV_W), F32),
                        pltpu.VMEM((SUBLANES - 1, tm + CONV_HALO - SUBLANES, CONV_W), F32)],
        compiler_params=_params("arbitrary"),
    )(p, p, dy, dy, w)


def _loss_head(y, target, name):
    t, d = y.shape
    tm = _row_tile(t)

    def body(y_ref, t_ref, l_ref, dy_ref):
        err = y_ref[...] - t_ref[...]
        dy_ref[...] = err * (1.0 / d)

        @pl.when(pl.program_id(0) == 0)
        def _():
            l_ref[...] = jnp.zeros_like(l_ref)

        row = jnp.sum(err * err, axis=-1, keepdims=True) * (0.5 / d)
        l_ref[...] += jnp.broadcast_to(jnp.sum(row, axis=0, keepdims=True), (1, LANE))

    row = pl.BlockSpec((tm, d), lambda i: (i, 0))
    return pl.pallas_call(
        body, name=name, grid=(t // tm,), in_specs=[row, row],
        out_specs=(pl.BlockSpec((1, LANE), lambda i: (0, 0)), row),
        out_shape=(jax.ShapeDtypeStruct((1, LANE), F32), jax.ShapeDtypeStruct((t, d), F32)),
        compiler_params=_params("arbitrary"),
    )(y, target)


def _adamw(w, g, m, v, name):
    r, c = w.shape
    tr = _row_tile(r, 256)
    c1, c2 = 1.0 - ADAM_B1 ** ADAM_STEP, 1.0 - ADAM_B2 ** ADAM_STEP

    def body(w_ref, g_ref, m_ref, v_ref, d_ref, mo_ref, vo_ref):
        gv = g_ref[...]
        mn = ADAM_B1 * m_ref[...] + (1.0 - ADAM_B1) * gv
        vn = ADAM_B2 * v_ref[...] + (1.0 - ADAM_B2) * (gv * gv)
        mo_ref[...] = mn
        vo_ref[...] = vn
        d_ref[...] = -ADAM_LR * ((mn / c1) / (jnp.sqrt(vn / c2) + ADAM_EPS) + ADAM_WD * w_ref[...])

    blk = pl.BlockSpec((tr, c), lambda i: (i, 0))
    shp = jax.ShapeDtypeStruct((r, c), F32)
    return pl.pallas_call(
        body, name=name, grid=(r // tr,), in_specs=[blk] * 4, out_specs=(blk, blk, blk), out_shape=(shp, shp, shp),
        compiler_params=_params("parallel"),
    )(w, g, m, v)


def _sum_parts(parts, name):
    r, c = parts[0].shape
    tr = _row_tile(r, 256)

    def body(*refs):
        acc = refs[0][...]
        for ref in refs[1:-1]:
            acc = acc + ref[...]
        refs[-1][...] = acc

    blk = pl.BlockSpec((tr, c), lambda i: (i, 0))
    return pl.pallas_call(
        body, name=name, grid=(r // tr,), in_specs=[blk] * len(parts), out_specs=blk,
        out_shape=jax.ShapeDtypeStruct((r, c), F32), compiler_params=_params("parallel"),
    )(*parts)


def _place():
    return lax.axis_index("x"), lax.axis_index("y"), lax.axis_index("c")


def _window(ref, block, size, axis):
    start = pl.multiple_of(block * size, LANE if size % LANE == 0 else 8)
    return ref.at[(slice(None),) * axis + (pl.ds(start, size),)]


def _all_gather(pieces, name, in_vmem=False, seeds=()):
    n_p, pieces = len(pieces), list(pieces) + list(seeds)
    n_all = len(pieces)

    def body(*refs):
        x_refs, out_refs = refs[:n_all], refs[n_all:2 * n_all]
        send_sems, recv_sems, local_sems = refs[2 * n_all:]
        px, py, pc = _place()
        me, sibling = (px, py, pc), (px, py, 1 - pc)
        chips = [(1 - px, py), (px, 1 - py), (1 - px, 1 - py)]

        def win(p, block):
            bx, by, bc = block
            x, axis = pieces[p]
            return _window(out_refs[p], 4 * bx + 2 * by + bc, x.shape[axis], axis)

        def copy(k, p, block, to, local=False):
            return pltpu.make_async_remote_copy(
                src_ref=x_refs[p] if local else win(p, block), dst_ref=win(p, block),
                send_sem=send_sems.at[k, p], recv_sem=recv_sems.at[k, p], device_id=to, device_id_type=MESH_ID)

        every = range(n_p)
        mine = [pltpu.make_async_copy(x_refs[p], win(p, me), local_sems.at[p]) for p in range(n_all)]
        first = [copy(0, p, me, sibling, local=True) for p in every]
        first += [copy(1 + j, p, me, (*chip, pc), local=True) for j, chip in enumerate(chips) for p in every]
        for cp in mine + first:
            cp.start()
        passed = []
        for j, chip in enumerate(chips):
            for p in every:
                copy(1 + j, p, (*chip, pc), me).wait_recv()
                passed.append(copy(4 + j, p, (*chip, pc), sibling))
                passed[-1].start()
        for p in every:
            copy(0, p, sibling, me).wait_recv()
        for j, chip in enumerate(chips):
            for p in every:
                copy(4 + j, p, (*chip, 1 - pc), me).wait_recv()
        for cp in first + passed:
            cp.wait_send()
        for cp in mine:
            cp.wait()

    def gathered(x, axis):
        return jax.ShapeDtypeStruct(x.shape[:axis] + (N_DEV * x.shape[axis],) + x.shape[axis + 1:], x.dtype)

    spec = VMEM_SPEC if in_vmem else ANY
    return pl.pallas_call(
        body, name=name, in_specs=[spec] * n_all, out_specs=[spec] * n_all,
        out_shape=[gathered(*pc_) for pc_ in pieces],
        scratch_shapes=[pltpu.SemaphoreType.DMA((7, n_p)), pltpu.SemaphoreType.DMA((7, n_p)),
                        pltpu.SemaphoreType.DMA((n_all,))],
        compiler_params=pltpu.CompilerParams(vmem_limit_bytes=VMEM_LIMIT),
    )(*[x for x, _ in pieces])


def _start_copies(bufs, n_copies, plan, name, after=None):
    nb = len(bufs)
    n_in = nb + (after is not None)

    def body(*refs):
        send_sems, recv_sems, token = refs[n_in], refs[n_in + 1], refs[-1]
        for i, (src, dst, dev) in enumerate(plan(refs[:nb])):
            pltpu.make_async_remote_copy(src_ref=src, dst_ref=dst, send_sem=send_sems.at[i], recv_sem=recv_sems.at[i],
                                         device_id=dev, device_id_type=MESH_ID).start()
        token[...] = jnp.zeros_like(token)

    out = pl.pallas_call(
        body, name=name, in_specs=[HBM_SPEC] * nb + [ANY] * (after is not None),
        out_shape=(pltpu.SemaphoreType.DMA((n_copies,)), pltpu.SemaphoreType.DMA((n_copies,)),
                   *[pltpu.HBM(b.shape, b.dtype) for b in bufs], jax.ShapeDtypeStruct((8, LANE), F32)),
        out_specs=(SEM_SPEC, SEM_SPEC, *[HBM_SPEC] * nb, VMEM_SPEC),
        input_output_aliases={i: 2 + i for i in range(nb)},
        compiler_params=pltpu.CompilerParams(has_side_effects=DATAFLOW),
    )(*[pltpu.with_memory_space_constraint(b, pltpu.HBM) for b in bufs], *([after] if after is not None else []))
    return out[0], out[1], list(out[2:2 + nb]), out[-1]


def _wait_copies(started, after, n_copies, plan, name):
    send_sems, recv_sems, bufs, _ = started
    nb = len(bufs)

    def body(*refs):
        send_ref, recv_ref = refs[nb], refs[nb + 1]
        copies = [pltpu.make_async_remote_copy(src_ref=src, dst_ref=dst, send_sem=send_ref.at[i], recv_sem=recv_ref.at[i],
                                               device_id=dev, device_id_type=MESH_ID)
                  for i, (src, dst, dev) in enumerate(plan(refs[:nb]))]
        for cp in copies:
            cp.wait_send()
        for cp in copies:
            cp.wait_recv()

    out = pl.pallas_call(
        body, name=name, in_specs=[HBM_SPEC] * nb + [SEM_SPEC, SEM_SPEC, ANY],
        out_shape=tuple(pltpu.HBM(b.shape, b.dtype) for b in bufs), out_specs=tuple([HBM_SPEC] * nb),
        input_output_aliases={i: i for i in range(nb)},
        compiler_params=pltpu.CompilerParams(has_side_effects=DATAFLOW),
    )(*bufs, send_sems, recv_sems, after)
    return list(out)


def _after(x, token):
    return x + token[0, 0].astype(x.dtype)


def _other_chips():
    px, py, _ = _place()
    return [(1 - px, py), (px, 1 - py), (1 - px, 1 - py)]


def _exchange(srcs, slots, src_block, target, name):
    n_p = len(srcs)

    def body(*refs):
        src_refs, out_refs, send_sems, recv_sems = refs[:n_p], refs[n_p:2 * n_p], refs[-2], refs[-1]
        copies = [pltpu.make_async_remote_copy(
            src_ref=src_refs[p].at[src_block(s)], dst_ref=out_refs[p].at[s], send_sem=send_sems.at[s, p],
            recv_sem=recv_sems.at[s, p], device_id=target(s), device_id_type=MESH_ID)
            for s in range(slots) for p in range(n_p)]
        for cp in copies:
            cp.start()
        for cp in copies:
            cp.wait_recv()
        for cp in copies:
            cp.wait_send()

    return pl.pallas_call(
        body, name=name, in_specs=[ANY] * n_p, out_specs=[ANY] * n_p,
        out_shape=[jax.ShapeDtypeStruct((slots,) + a.shape[1:], a.dtype) for a in srcs],
        scratch_shapes=[pltpu.SemaphoreType.DMA((slots, n_p)), pltpu.SemaphoreType.DMA((slots, n_p))],
        compiler_params=pltpu.CompilerParams(vmem_limit_bytes=VMEM_LIMIT),
    )(*srcs)


def _blocks_to_sibling(sends, name):
    def src_block(j):
        return 2 * j + 1 - lax.axis_index("c")

    def target(j):
        px, py, pc = _place()
        return (px, py, 1 - pc)

    return _exchange(sends, 4, src_block, target, name)


def _pair_sums_for_chips(own, got, name):
    _, r, c = own.shape
    tr = _row_tile(r, 256, 16)

    def body(idx_ref, own_ref, got_ref, o_ref):
        o_ref[...] = (own_ref[...] + got_ref[...].astype(F32)).astype(o_ref.dtype)

    grid_spec = pltpu.PrefetchScalarGridSpec(
        num_scalar_prefetch=1, grid=(3, r // tr),
        in_specs=[pl.BlockSpec((None, tr, c), lambda k, i, idx: (idx[k], i, 0)),
                  pl.BlockSpec((None, tr, c), lambda k, i, idx: (idx[3 + k], i, 0))],
        out_specs=pl.BlockSpec((None, tr, c), lambda k, i, idx: (k, i, 0)))
    chips = [2 * cx + cy for cx, cy in _other_chips()]
    idx = jnp.stack([2 * j + lax.axis_index("c") for j in chips] + chips).astype(jnp.int32)
    return pl.pallas_call(
        body, name=name, grid_spec=grid_spec, out_shape=jax.ShapeDtypeStruct((3, r, c), BF16),
        compiler_params=_params("parallel", "parallel"),
    )(idx, own, got)


def _sum_for_me(own, got_sibling, got_chips, name):
    _, r, c = own.shape
    tr = _row_tile(r, 256, 16)

    def body(idx_ref, own_ref, sib_ref, g0_ref, g1_ref, g2_ref, o_ref):
        acc = own_ref[...] + sib_ref[...].astype(F32)
        for ref in (g0_ref, g1_ref, g2_ref):
            acc = acc + ref[...].astype(F32)
        o_ref[...] = acc

    def part(k):
        return pl.BlockSpec((None, tr, c), lambda i, idx: (k, i, 0))

    grid_spec = pltpu.PrefetchScalarGridSpec(
        num_scalar_prefetch=1, grid=(r // tr,),
        in_specs=[pl.BlockSpec((None, tr, c), lambda i, idx: (idx[0], i, 0)),
                  pl.BlockSpec((None, tr, c), lambda i, idx: (idx[1], i, 0)), part(0), part(1), part(2)],
        out_specs=pl.BlockSpec((tr, c), lambda i, idx: (i, 0)))
    px, py, pc = _place()
    idx = jnp.stack([4 * px + 2 * py + pc, 2 * px + py]).astype(jnp.int32)
    return pl.pallas_call(
        body, name=name, grid_spec=grid_spec, out_shape=jax.ShapeDtypeStruct((r, c), F32),
        compiler_params=_params("parallel"),
    )(idx, own, got_sibling, got_chips, got_chips, got_chips)


def _chip_plan(n_p):
    def plan(refs):
        pc = lax.axis_index("c")
        return [(refs[p].at[k], refs[n_p + p].at[k], (cx, cy, pc))
                for p in range(n_p) for k, (cx, cy) in enumerate(_other_chips())]
    return plan


def _reduce_start(own, sends, tag):
    from_sibling = _blocks_to_sibling(sends, "grads_to_sibling_" + tag)
    pair_sums = [_pair_sums_for_chips(a, b, "grads_pair_sums") for a, b in zip(own, from_sibling)]
    lands = [lax.empty(a.shape, a.dtype) for a in pair_sums]
    started = _start_copies(pair_sums + lands, 3 * len(own), _chip_plan(len(own)), "grads_to_chips_start_" + tag)
    return from_sibling, started


def _reduce_finish(own, from_sibling, started, after, tag):
    n_p = len(own)
    bufs = _wait_copies(started, after, 3 * n_p, _chip_plan(n_p), "grads_to_chips_wait_" + tag)
    return [_sum_for_me(a, b, c, "grads_sum") for a, b, c in zip(own, from_sibling, bufs[n_p:])]


def _direct_plan(n_p):
    def plan(refs):
        px, py, pc = _place()
        out = []
        for p in range(n_p):
            for m in range(1, N_DEV):
                tx = 1 - px if m & 4 else px
                ty = 1 - py if m & 2 else py
                tc = 1 - pc if m & 1 else pc
                out.append((refs[p].at[4 * tx + 2 * ty + tc], refs[n_p + p].at[m - 1], (tx, ty, tc)))
        return out
    return plan


def _sum_direct(own, got, name):
    _, r, c = own.shape
    tr = _row_tile(r, 256, 16)

    def body(idx_ref, own_ref, *refs):
        acc = own_ref[...]
        for ref in refs[:-1]:
            acc = acc + ref[...].astype(F32)
        refs[-1][...] = acc

    def part(k):
        return pl.BlockSpec((None, tr, c), lambda i, idx: (k, i, 0))

    grid_spec = pltpu.PrefetchScalarGridSpec(
        num_scalar_prefetch=1, grid=(r // tr,),
        in_specs=[pl.BlockSpec((None, tr, c), lambda i, idx: (idx[0], i, 0))] + [part(k) for k in range(N_DEV - 1)],
        out_specs=pl.BlockSpec((tr, c), lambda i, idx: (i, 0)))
    px, py, pc = _place()
    idx = (4 * px + 2 * py + pc).astype(jnp.int32).reshape(1)
    return pl.pallas_call(
        body, name=name, grid_spec=grid_spec, out_shape=jax.ShapeDtypeStruct((r, c), F32),
        compiler_params=_params("parallel"),
    )(idx, own, *([got] * (N_DEV - 1)))


def _reduce_direct_start(sends, tag):
    lands = [lax.empty((N_DEV - 1,) + a.shape[1:], a.dtype) for a in sends]
    return _start_copies(list(sends) + lands, (N_DEV - 1) * len(sends), _direct_plan(len(sends)),
                         "grads_direct_start_" + tag)


def _reduce_direct_finish(own, started, after, tag):
    n_p = len(own)
    bufs = _wait_copies(started, after, (N_DEV - 1) * n_p, _direct_plan(n_p), "grads_direct_wait_" + tag)
    return [_sum_direct(a, b, "grads_sum_direct") for a, b in zip(own, bufs[n_p:])]


def _gather_plans(pieces):
    n_p = len(pieces)
    dims = [(x.shape[axis], axis) for x, axis in pieces]

    def first(refs):
        px, py, pc = _place()
        targets = [(px, py, 1 - pc)] + [(cx, cy, pc) for cx, cy in _other_chips()]
        return [(refs[p], _window(refs[n_p + p], 4 * px + 2 * py + pc, *dims[p]), to)
                for p in range(n_p) for to in targets]

    def second(refs):
        px, py, pc = _place()
        out = []
        for p in range(n_p):
            for cx, cy, cc in [(cx, cy, pc) for cx, cy in _other_chips()] + [(px, py, 1 - pc)]:
                win = _window(refs[p], 4 * cx + 2 * cy + cc, *dims[p])
                out.append((win, win, (px, py, 1 - pc)))
        return out

    return first, second


FORWARD_COPIES = 4


def _flat_rows(a, width):
    return a.reshape(-1, width)


def _full_from_blocks(blocks, name):
    if name in COL_SHARDED:
        _, l, k, nb = blocks.shape
        return jnp.transpose(blocks, (1, 2, 0, 3)).reshape(l, k, N_DEV * nb)
    _, l, rb, n = blocks.shape
    return jnp.transpose(blocks, (1, 0, 2, 3)).reshape(l, N_DEV * rb, n)


def _blocks_from_full(full, name):
    if name in COL_SHARDED:
        l, k, n = full.shape
        return jnp.transpose(full.reshape(l, k, N_DEV, n // N_DEV), (2, 0, 1, 3))
    l, rows, n = full.shape
    return jnp.transpose(full.reshape(l, N_DEV, rows // N_DEV, n), (1, 0, 2, 3))


def _pad_heads(w, width):
    k = w.shape[0]
    return jnp.pad(w.reshape(k, N_HEADS, width), ((0, 0), (0, 0), (0, HEAD_PAD - width))).reshape(k, N_HEADS * HEAD_PAD)


def _unpad_heads(w, width):
    k = w.shape[0]
    return w.reshape(k, N_HEADS, HEAD_PAD)[:, :, :width].reshape(k, N_HEADS * width)


def _layer_operands(full, vec, conv_w_full, l):
    w_in = full['w_in'][l]
    kpe = jnp.pad(w_in[:, LAT:LAT + QK_ROPE], ((0, 0), (QK_NOPE, HEAD_PAD - QK_DIM)))
    w_ukv = full['w_ukv'][l].reshape(KV_LORA, N_HEADS, QK_NOPE + V_DIM)
    w_out = full['w_out'][l]
    d_model = w_out.shape[1]
    wo_attn = jnp.pad(w_out[:N_HEADS * V_DIM].reshape(N_HEADS, V_DIM, d_model),
                      ((0, 0), (0, HEAD_PAD - V_DIM), (0, 0))).reshape(N_HEADS * HEAD_PAD, d_model)
    ops = {
        'w_in': jnp.concatenate([w_in[:, :LAT], kpe, w_in[:, LAT + QK_ROPE:]], axis=1),
        'w_q': _pad_heads(full['w_uq'][l], QK_DIM),
        'w_k': _pad_heads(w_ukv[:, :, :QK_NOPE].reshape(KV_LORA, N_HEADS * QK_NOPE), QK_NOPE),
        'w_v': _pad_heads(w_ukv[:, :, QK_NOPE:].reshape(KV_LORA, N_HEADS * V_DIM), V_DIM),
        'wo_attn': wo_attn,
        'wo_conv': w_out[N_HEADS * V_DIM:],
        'conv_w': jnp.pad(conv_w_full[l], ((0, CONV_HALO - CONV_K), (0, 0))),
        'gq': jnp.pad(vec['q_norm'][l], (0, HEAD_PAD - QK_DIM)),
        'gk': jnp.pad(vec['k_norm'][l], (0, HEAD_PAD - QK_DIM)),
    }
    for n in ('ffn1_norm', 'mix_norm', 'q_latent_norm', 'kv_latent_norm', 'conv_b', 'conv_ln_g', 'conv_ln_b',
              'ffn2_norm', 'post_norm'):
        ops[n] = vec[n][l]
    return ops


def _ffn_fwd(x, g, wgu, wd, fp):
    h = _rms_fwd(x, g, BF16, "rms_fwd_ffn")
    ab, z = _ffn_up(h, wgu, fp, "ffn_up")
    y = _mm(z, wd, res=x, scale=0.5, name="ffn_down")
    return y, (x, h, ab, z)


def _ffn_bwd(dy, dyb, saved, g, wgu, wd, fp, after_dw=None, after=None):
    x, h, ab, z = saved
    d_wd = _mm(z, dyb, ta=True, scale=0.5, blocks=('row', N_DEV), after=after, name="ffn_dwd")
    dab = _ffn_dab(dyb, wd, ab, fp, "ffn_dab")
    d_wgu = _mm(h, dab, ta=True, blocks=('col', N_DEV), name="ffn_dwgu")
    token = after_dw(d_wgu, d_wd) if after_dw is not None else None
    dx, dxb, dg = _matmul_rms_bwd(dab, wgu, x, g, dy, "ffn_dh_rms", after=token)
    return dx, dxb, dg, d_wgu, d_wd


def _mixer_fwd(x, ops, tabs, after_attention=None):
    h = _rms_fwd(x, ops['mix_norm'], BF16, "rms_fwd_mix")
    p = _mm(h, ops['w_in'], name="mix_in")
    qln, kvln = _lat_norm_fwd(p, ops['q_latent_norm'], ops['kv_latent_norm'], "lat_norm_fwd")
    q_raw = _mm(qln, ops['w_q'], name="mix_q")
    k_raw = _mm(kvln, ops['w_k'], name="mix_k")
    v = _mm(kvln, ops['w_v'], out_dtype=BF16, name="mix_v")
    q, k = _qk_prep_fwd(q_raw, k_raw, p, ops['gq'], ops['gk'], tabs, "qk_prep_fwd")
    o, lse = _attn_fwd(q, k, v, "attn_fwd")
    token = after_attention(o) if after_attention is not None else None
    conv_b = ops['conv_b'] if token is None else _after(ops['conv_b'], token)
    y_conv, cv = _conv_fwd(p, ops['conv_w'], conv_b, ops['conv_ln_g'], ops['conv_ln_b'], "conv_fwd")
    x_attn = _mm(o, ops['wo_attn'], res=x, name="mix_out_attn")
    x_out = _mm(cv, ops['wo_conv'], res=x_attn, name="mix_out_conv")
    return x_out, (x, h, p, qln, kvln, q_raw, k_raw, v, q, k, o, lse, y_conv, cv)


def _mixer_bwd(dx_out, dxb_out, saved, ops, tabs, token=None):
    x, h, p, qln, kvln, q_raw, k_raw, v, q, k, o, lse, y_conv, cv = saved
    g = {}
    do = _mm(dxb_out, ops['wo_attn'], tb=True, after=token, name="mix_do")
    dcv = _mm(dxb_out, ops['wo_conv'], tb=True, name="mix_dcv")
    g['wo_attn'] = _mm(o, dxb_out, ta=True, name="mix_dwo_attn")
    g['wo_conv'] = _mm(cv, dxb_out, ta=True, name="mix_dwo_conv")
    dq, dk, dv = _attn_bwd(q, k, v, o, lse, do, "attn_bwd")
    dq_raw, dk_raw, dkpe, g['gq'], g['gk'] = _qk_prep_bwd(q_raw, k_raw, p, dq, dk, ops['gq'], ops['gk'], tabs,
                                                          "qk_prep_bwd")
    g['w_q'] = _mm(qln, dq_raw, ta=True, name="mix_dwq")
    g['w_k'] = _mm(kvln, dk_raw, ta=True, name="mix_dwk")
    g['w_v'] = _mm(kvln, dv, ta=True, name="mix_dwv")
    dqln = _mm(dq_raw, ops['w_q'], tb=True, name="mix_dqln")
    dkvln = _mm(dk_raw, ops['w_k'], tb=True, name="mix_dkvln_k")
    dkvln = _mm(dv, ops['w_v'], tb=True, res=dkvln, name="mix_dkvln_v")
    dp_lat, g['q_latent_norm'], g['kv_latent_norm'] = _lat_norm_bwd(
        p, ops['q_latent_norm'], ops['kv_latent_norm'], dqln, dkvln, "lat_norm_bwd")
    dy_conv, g['conv_ln_g'], g['conv_ln_b'], g['conv_b'] = _conv_bwd_ln(
        y_conv, dcv, ops['conv_ln_g'], ops['conv_ln_b'], "conv_bwd_ln")
    dag, g['conv_w'] = _conv_bwd_taps(p, dy_conv, ops['conv_w'], "conv_bwd_taps")
    dp = jnp.concatenate([dp_lat, dkpe, dag], axis=1)
    g['w_in'] = _mm(h, dp, ta=True, name="mix_dw_in")
    dx, dxb, g['mix_norm'] = _matmul_rms_bwd(dp, ops['w_in'], x, ops['mix_norm'], dx_out, "mix_dh_rms")
    return dx, dxb, g


def _mixer_grads_to_params(g):
    d_w_in = g['w_in']
    d_wk = _unpad_heads(g['w_k'], QK_NOPE).reshape(KV_LORA, N_HEADS, QK_NOPE)
    d_wv = _unpad_heads(g['w_v'], V_DIM).reshape(KV_LORA, N_HEADS, V_DIM)
    d_model = g['wo_attn'].shape[1]
    d_wo_attn = g['wo_attn'].reshape(N_HEADS, HEAD_PAD, d_model)[:, :V_DIM].reshape(N_HEADS * V_DIM, d_model)
    return {
        'mix_norm': g['mix_norm'],
        'w_in': jnp.concatenate([d_w_in[:, :LAT], d_w_in[:, LAT + QK_NOPE:LAT + QK_DIM], d_w_in[:, P_A:]], axis=1),
        'q_latent_norm': g['q_latent_norm'], 'w_uq': _unpad_heads(g['w_q'], QK_DIM),
        'kv_latent_norm': g['kv_latent_norm'],
        'w_ukv': jnp.concatenate([d_wk, d_wv], axis=2).reshape(KV_LORA, N_HEADS * (QK_NOPE + V_DIM)),
        'q_norm': g['gq'][:QK_DIM], 'k_norm': g['gk'][:QK_DIM], 'conv_w': g['conv_w'][:CONV_K],
        'conv_b': g['conv_b'], 'conv_ln_g': g['conv_ln_g'], 'conv_ln_b': g['conv_ln_b'],
        'w_out': jnp.concatenate([d_wo_attn, g['wo_conv']], axis=0),
    }


def kernel(x, ffn1_norm, ffn1_w_gate, ffn1_w_up, ffn1_w_down, mix_norm, w_in, q_latent_norm, w_uq, kv_latent_norm, w_ukv, q_norm, k_norm, conv_w, conv_b, conv_ln_g, conv_ln_b, w_out, ffn2_norm, ffn2_w_gate, ffn2_w_up, ffn2_w_down, post_norm, loss_target, m_ffn1_norm, m_ffn1_w_gate, m_ffn1_w_up, m_ffn1_w_down, m_mix_norm, m_w_in, m_q_latent_norm, m_w_uq, m_kv_latent_norm, m_w_ukv, m_q_norm, m_k_norm, m_conv_w, m_conv_b, m_conv_ln_g, m_conv_ln_b, m_w_out, m_ffn2_norm, m_ffn2_w_gate, m_ffn2_w_up, m_ffn2_w_down, m_post_norm, v_ffn1_norm, v_ffn1_w_gate, v_ffn1_w_up, v_ffn1_w_down, v_mix_norm, v_w_in, v_q_latent_norm, v_w_uq, v_kv_latent_norm, v_w_ukv, v_q_norm, v_k_norm, v_conv_w, v_conv_b, v_conv_ln_g, v_conv_ln_b, v_w_out, v_ffn2_norm, v_ffn2_w_gate, v_ffn2_w_up, v_ffn2_w_down, v_post_norm):
    args = locals()
    w = {n: args[n] for n in WEIGHTS}
    mom = {n: args["m_" + n] for n in WEIGHTS}
    var = {n: args["v_" + n] for n in WEIGHTS}
    depth = ffn1_norm.shape[0]
    x0 = x.reshape(x.shape[-2:])
    target = loss_target.reshape(loss_target.shape[-2:])
    t, d_model = x0.shape
    my_block = 4 * lax.axis_index("x") + 2 * lax.axis_index("y") + lax.axis_index("c")

    fb = ffn1_w_gate.shape[-1]
    fp = -(-fb // LANE) * LANE
    ffns = [(l, f) for l in range(depth) for f in (1, 2)]
    pad_cols = lambda a: jnp.pad(a, ((0, 0), (0, fp - fb)))
    gu_local = {(l, f): jnp.concatenate([pad_cols(w[f'ffn{f}_w_gate'][l]), pad_cols(w[f'ffn{f}_w_up'][l])],
                                        axis=1).astype(BF16) for l, f in ffns}
    dn_local = {(l, f): jnp.pad(w[f'ffn{f}_w_down'][l], ((0, fp - fb), (0, 0))).astype(BF16) for l, f in ffns}
    rows_of = {n: w[n].size // d_model for n in REST}
    rest_local = jnp.concatenate([_flat_rows(w[n].astype(BF16), d_model) for n in REST], axis=0)
    n_rest = rest_local.shape[0]
    first_ffn, later = ffns[0], ffns[1:]
    cw = conv_w.reshape(-1)
    cw_rows = -(-cw.size // (8 * LANE)) * 8
    cw_flat = jnp.pad(cw, (0, cw_rows * LANE - cw.size)).reshape(cw_rows, LANE)
    later_pieces = [(gu_local[q], 1) for q in later] + [(dn_local[q], 0) for q in later]
    n_later = len(later_pieces)
    got = _all_gather([(gu_local[first_ffn], 1), (dn_local[first_ffn], 0)], "gather_first")
    wgu, wd = {first_ffn: got[0]}, {first_ffn: got[1]}
    mixer_pieces = [(rest_local, 0), (cw_flat, 0)]
    mixer_plan, mixer_forward_plan = _gather_plans(mixer_pieces)
    gather_plan, forward_plan = _gather_plans(later_pieces)

    def landing(a, axis):
        return lax.empty(a.shape[:axis] + (N_DEV * a.shape[axis],) + a.shape[axis + 1:], a.dtype)

    gather_mixer = _start_copies([a for a, _ in mixer_pieces] + [landing(a, ax) for a, ax in mixer_pieces],
                                 4 * len(mixer_pieces), mixer_plan, "gather_mixer_start", after=got[0])
    gather_later = _start_copies([a for a, _ in later_pieces] + [landing(a, ax) for a, ax in later_pieces],
                                 4 * n_later, gather_plan, "gather_later_start", after=gather_mixer[3])

    x1_first, s1_first = _ffn_fwd(x0, _after(w['ffn1_norm'][0], gather_later[3]), wgu[first_ffn], wd[first_ffn], fp)
    lands = _wait_copies(gather_mixer, x1_first, 4 * len(mixer_pieces), mixer_plan, "gather_mixer_wait")[2:]
    pass_on = _start_copies(lands, FORWARD_COPIES * len(mixer_pieces), mixer_forward_plan, "gather_mixer_forward_start")
    lands = _wait_copies(pass_on, pass_on[3], FORWARD_COPIES * len(mixer_pieces), mixer_forward_plan, "gather_mixer_forward_wait")
    gathered = lands[0].reshape(N_DEV, n_rest, d_model)
    cw_all = lands[1].reshape(N_DEV, cw_rows * LANE)[:, :cw.size]
    full, start = {}, 0
    for n in REST:
        blocks = gathered[:, start:start + rows_of[n]].reshape((N_DEV,) + w[n].shape)
        full[n] = _full_from_blocks(blocks, n)
        start += rows_of[n]
    conv_w_full =jnp.transpose(cw_all.reshape((N_DEV,) + conv_w.shape), (1, 2, 0, 3)).reshape(depth, CONV_K, CONV_W)
    vec = {n: w[n] for n in VECTORS}
    ops = [_layer_operands(full, vec, conv_w_full, l) for l in range(depth)]
    tabs = _rope_tables(t)

    saved, xl = [], x0
    forward_later = []

    def pass_on_later(o_attn):
        lands = _wait_copies(gather_later, o_attn, 4 * n_later, gather_plan, "gather_later_wait")[n_later:]
        forward_later.append(_start_copies(lands, FORWARD_COPIES * n_later, forward_plan, "gather_later_forward_start"))
        return forward_later[0][3]

    for l in range(depth):
        o = ops[l]
        if l == 0:
            x1, s1 = x1_first, s1_first
            x2, sm = _mixer_fwd(x1, o, tabs, after_attention=pass_on_later)
            lands = _wait_copies(forward_later[0], x2, FORWARD_COPIES * n_later, forward_plan, "gather_later_forward_wait")
            wgu.update(zip(later, lands[:len(later)]))
            wd.update(zip(later, lands[len(later):]))
        else:
            x1, s1 = _ffn_fwd(xl, o['ffn1_norm'], wgu[l, 1], wd[l, 1], fp)
            x2, sm = _mixer_fwd(x1, o, tabs)
        x3, s2 = _ffn_fwd(x2, o['ffn2_norm'], wgu[l, 2], wd[l, 2], fp)
        xl = _rms_fwd(x3, o['post_norm'], F32, "rms_fwd_post")
        saved.append((s1, sm, s2, x3))
    loss_part, dx = _loss_head(xl, target, "loss_head")
    loss = lax.psum(loss_part[0, 0], ("x", "y", "c"))

    grads, mine_gu, mine_dn, in_flight = [None] * depth, {}, {}, {}

    def exchange(tag):
        def after_dw(d_wgu, d_wd):
            own = [d_wgu[0], d_wd[0]]
            if tag == last_tag:
                from_sibling, started = _reduce_start(own, [d_wgu[1], d_wd[1]], tag)
            else:
                from_sibling, started = None, _reduce_direct_start([d_wgu[1], d_wd[1]], tag)
            in_flight[tag] = (own, from_sibling, started)
            return started[3]
        return after_dw

    def finish(tag, after):
        own, from_sibling, started = in_flight.pop(tag)
        if from_sibling is None:
            return _reduce_direct_finish(own, started, after, tag)
        return _reduce_finish(own, from_sibling, started, after, tag)

    last_tag = f"{first_ffn[0]}{first_ffn[1]}"

    for l in reversed(range(depth)):
        o = ops[l]
        s1, sm, s2, x3 = saved[l]
        dx, dxb, d_post = _rms_bwd(x3, o['post_norm'], dx, None, "rms_bwd_post")
        dx, dxb, d_ffn2, _, _ = _ffn_bwd(dx, dxb, s2, o['ffn2_norm'], wgu[l, 2], wd[l, 2], fp, exchange(f"{l}2"))
        if l + 1 < depth:
            mine_gu[l + 1, 1], mine_dn[l + 1, 1] = finish(f"{l + 1}1", dx)
        dx, dxb, gm = _mixer_bwd(dx, dxb, sm, o, tabs)
        mine_gu[l, 2], mine_dn[l, 2] = finish(f"{l}2", dx)
        grads[l] = _mixer_grads_to_params(gm)
        if l == 0:
            rest_own = jnp.concatenate(
                [_blocks_from_full(jnp.stack([grads[k][n] for k in range(depth)]), n).reshape(N_DEV, rows_of[n], d_model)
                 for n in REST], axis=1)
            sibling_rest, started_rest = _reduce_start([rest_own], [rest_own.astype(BF16)], "rest")
        dx, dxb, d_ffn1, _, _ = _ffn_bwd(dx, dxb, s1, o['ffn1_norm'], wgu[l, 1], wd[l, 1], fp, exchange(f"{l}1"),
                                         after=started_rest[3] if l == 0 else None)
        grads[l].update(post_norm=d_post, ffn2_norm=d_ffn2, ffn1_norm=d_ffn1)
    grad_x = dx.reshape(x.shape)
    part = {n: jnp.stack([grads[l][n] for l in range(depth)]) for n in grads[0]}

    small = jnp.concatenate([part[n].reshape(-1) for n in VECTORS] + [part['conv_w'].reshape(-1)])
    s_rows = -(-small.size // (8 * LANE)) * 8
    small = jnp.pad(small, (0, s_rows * LANE - small.size)).reshape(s_rows, LANE)
    small_all = _all_gather([(small, 0)], "gather_small_grads", in_vmem=True)[0]
    small_sum = _sum_parts([small_all[k * s_rows:(k + 1) * s_rows] for k in range(N_DEV)], "sum_small_grads")

    mine_gu[first_ffn], mine_dn[first_ffn] = finish(f"{first_ffn[0]}{first_ffn[1]}", small_sum)
    mine_rest = _reduce_finish([rest_own], sibling_rest, started_rest, small_sum, "rest")[0]
    grad = {}
    for f in (1, 2):
        grad[f'ffn{f}_w_gate'] = jnp.stack([mine_gu[l, f][:, :fb] for l in range(depth)])
        grad[f'ffn{f}_w_up'] = jnp.stack([mine_gu[l, f][:, fp:fp + fb] for l in range(depth)])
        grad[f'ffn{f}_w_down'] = jnp.stack([mine_dn[l, f][:fb] for l in range(depth)])
    start = 0
    for n in REST:
        grad[n] = mine_rest[start:start + rows_of[n]].reshape(w[n].shape)
        start += rows_of[n]
    small_sum = small_sum.reshape(-1)
    start = 0
    for n in VECTORS:
        grad[n] = small_sum[start:start + w[n].size].reshape(w[n].shape)
        start += w[n].size
    cw_grad = small_sum[start:start + depth * CONV_K * CONV_W].reshape(depth, CONV_K, CONV_W)
    nb = conv_w.shape[-1]
    grad['conv_w'] = lax.dynamic_slice_in_dim(cw_grad, my_block * nb, nb, axis=2)

    delta, new_m, new_v = {}, {}, {}
    for n in BIG + ['conv_w']:
        shp = w[n].shape
        two_d = lambda a: a.reshape(-1, shp[-1])
        dl, mn, vn = _adamw(two_d(w[n]), two_d(grad[n]), two_d(mom[n]), two_d(var[n]), "adamw_" + n)
        delta[n], new_m[n], new_v[n] = dl.reshape(shp), mn.reshape(shp), vn.reshape(shp)
    vcat = lambda src: jnp.concatenate([src[n].reshape(-1) for n in VECTORS]).reshape(-1, LANE)
    dl, mn, vn = _adamw(vcat(w), vcat(grad), vcat(mom), vcat(var), "adamw_vectors")
    start = 0
    for n in VECTORS:
        sl = lambda a: a.reshape(-1)[start:start + w[n].size].reshape(w[n].shape)
        delta[n], new_m[n], new_v[n] = sl(dl), sl(mn), sl(vn)
        start += w[n].size

    return (loss, grad_x, *[grad[n] for n in WEIGHTS], *[delta[n] for n in WEIGHTS],
            *[new_m[n] for n in WEIGHTS], *[new_v[n] for n in WEIGHTS])
```

```python
import functools

import jax
import jax.numpy as jnp
from jax import lax
from jax.experimental import pallas as pl
from jax.experimental.pallas import tpu as pltpu

F32, BF16 = jnp.float32, jnp.bfloat16

N_DEV = 8
N_HEADS = 8
QK_NOPE, QK_ROPE, V_DIM = 64, 32, 64
QK_DIM = QK_NOPE + QK_ROPE
HEAD_PAD = 128
Q_LORA, KV_LORA = 384, 256
LAT = Q_LORA + KV_LORA
CONV_W, CONV_K = 512, 31
CONV_HALO = 32
CHUNK = 64
ROPE_THETA = 10000.0
EPS = 1e-6
ATTN_SCALE = QK_DIM ** -0.5
ATTN_SCALE_LOG2 = ATTN_SCALE * 1.4426950408889634
P_KPE = LAT
P_A = LAT + HEAD_PAD
P_G = P_A + CONV_W
P_COLS = P_G + CONV_W

ADAM_LR, ADAM_B1, ADAM_B2, ADAM_EPS, ADAM_WD, ADAM_STEP = 0.001, 0.9, 0.999, 1e-08, 0.01, 10

V7X_VMEM_BYTES = 64 << 20
VMEM_LIMIT = V7X_VMEM_BYTES - (8 << 20)
MM_VMEM_BUDGET = 40 << 20
LANE = 128
ROW_TILE = 512
ATTN_BLOCK = 512
ATTN_HEADS = 1
CONV_TILE = 256

MESH_ID = pl.DeviceIdType.MESH
ANY = pl.BlockSpec(memory_space=pl.ANY)
VMEM_SPEC = pl.BlockSpec(memory_space=pltpu.VMEM)
HBM_SPEC = pl.BlockSpec(memory_space=pltpu.HBM)
SEM_SPEC = pl.BlockSpec(memory_space=pltpu.SEMAPHORE)
DATAFLOW = pltpu.SideEffectType.DATAFLOW_SIDE_EFFECTING

WEIGHTS = ['ffn1_norm', 'ffn1_w_gate', 'ffn1_w_up', 'ffn1_w_down', 'mix_norm', 'w_in', 'q_latent_norm', 'w_uq',
           'kv_latent_norm', 'w_ukv', 'q_norm', 'k_norm', 'conv_w', 'conv_b', 'conv_ln_g', 'conv_ln_b', 'w_out',
           'ffn2_norm', 'ffn2_w_gate', 'ffn2_w_up', 'ffn2_w_down', 'post_norm']
COL_SHARDED = ['ffn1_w_gate', 'ffn1_w_up', 'w_in', 'w_uq', 'w_ukv', 'ffn2_w_gate', 'ffn2_w_up']
ROW_SHARDED = ['ffn1_w_down', 'w_out', 'ffn2_w_down']
REST = ['w_in', 'w_uq', 'w_ukv', 'w_out']
BIG = ['ffn1_w_gate', 'ffn1_w_up', 'ffn1_w_down', 'w_in', 'w_uq', 'w_ukv', 'w_out', 'ffn2_w_gate', 'ffn2_w_up',
       'ffn2_w_down']
VECTORS = ['ffn1_norm', 'mix_norm', 'q_latent_norm', 'kv_latent_norm', 'q_norm', 'k_norm', 'conv_b', 'conv_ln_g',
           'conv_ln_b', 'ffn2_norm', 'post_norm']


def _params(*sem):
    return pltpu.CompilerParams(dimension_semantics=sem if sem else None, vmem_limit_bytes=VMEM_LIMIT)


def _tile(n, cap):
    if n <= cap:
        return n
    best = 0
    for d in range(LANE, cap + 1, LANE):
        if n % d == 0:
            best = d
    assert best, (n, cap)
    return best


def _row_tile(n, cap=ROW_TILE, mult=8):
    if n <= cap:
        return n
    best = 0
    for d in range(mult, cap + 1, mult):
        if n % d == 0:
            best = d
    assert best, (n, cap)
    return best


def _mm(a, b, *, name, ta=False, tb=False, res=None, scale=1.0, out_dtype=F32, tm=None, tn=None, blocks=None,
        after=None):
    (kdim, m) = a.shape if ta else a.shape[::-1]
    (n, kb) = b.shape if tb else b.shape[::-1]
    assert kdim == kb, (a.shape, b.shape, ta, tb)
    tm, tn = tm or _tile(m, 512), tn or _tile(n, 1024)
    if blocks is not None:
        tm, tn = (tm, n // blocks[1]) if blocks[0] == 'col' else (m // blocks[1], tn)
    size = lambda arr: jnp.dtype(arr.dtype).itemsize
    out_bytes = tm * tn * ((6 if blocks is not None else jnp.dtype(out_dtype).itemsize) + (4 if res is not None else 0))

    def vmem_need(tk):
        return 2 * (tm * tk * size(a) + tk * tn * size(b) + out_bytes) + (tm * tn * 4 if tk < kdim else 0)

    tk = kdim
    for cand in [d for d in range(kdim - LANE, 0, -LANE) if kdim % d == 0]:
        if vmem_need(tk) <= MM_VMEM_BUDGET:
            break
        tk = cand
    nk = kdim // tk
    n_in = 2 + (res is not None) + (after is not None)
    n_out = 2 if blocks is not None else 1
    dims = (((0 if ta else 1,), (1 if tb else 0,)), ((), ()))

    def body(*refs):
        a_ref, b_ref = refs[0], refs[1]
        r_ref = refs[2] if res is not None else None
        o_refs = refs[n_in:n_in + n_out]
        acc_ref = refs[-1] if nk > 1 else None
        part = lax.dot_general(a_ref[...].astype(BF16), b_ref[...].astype(BF16), dims, preferred_element_type=F32)

        def finish(acc):
            if scale != 1.0:
                acc = acc * scale
            if r_ref is not None:
                acc = r_ref[...] + acc
            for o_ref in o_refs:
                o_ref[...] = acc.astype(o_ref.dtype)

        if nk == 1:
            finish(part)
        else:
            k = pl.program_id(2)

            @pl.when(k == 0)
            def _():
                acc_ref[...] = part

            @pl.when(k > 0)
            def _():
                acc_ref[...] += part

            @pl.when(k == nk - 1)
            def _():
                finish(acc_ref[...])

    a_spec = pl.BlockSpec((tk, tm), lambda i, j, k: (k, i)) if ta else pl.BlockSpec((tm, tk), lambda i, j, k: (i, k))
    b_spec = pl.BlockSpec((tn, tk), lambda i, j, k: (j, k)) if tb else pl.BlockSpec((tk, tn), lambda i, j, k: (k, j))
    plain = pl.BlockSpec((tm, tn), lambda i, j, k: (i, j))
    if blocks is None:
        out_specs, out_shape = plain, jax.ShapeDtypeStruct((m, n), out_dtype)
    else:
        if blocks[0] == 'col':
            o_spec, shp = pl.BlockSpec((None, tm, tn), lambda i, j, k: (j, i, 0)), (blocks[1], m, tn)
        else:
            o_spec, shp = pl.BlockSpec((None, tm, tn), lambda i, j, k: (i, 0, j)), (blocks[1], tm, n)
        out_specs, out_shape = (o_spec, o_spec), (jax.ShapeDtypeStruct(shp, F32), jax.ShapeDtypeStruct(shp, BF16))
    in_specs = [a_spec, b_spec] + ([plain] if res is not None else [])
    args = (a, b) + ((res,) if res is not None else ())
    if after is not None:
        in_specs.append(pl.BlockSpec(after.shape, lambda i, j, k: (0, 0)))
        args += (after,)
    return pl.pallas_call(
        body, name=name, grid=(m // tm, n // tn, nk), in_specs=in_specs, out_specs=out_specs, out_shape=out_shape,
        scratch_shapes=[pltpu.VMEM((tm, tn), F32)] if nk > 1 else [],
        compiler_params=_params("parallel", "parallel", "arbitrary"),
    )(*args)


def _rms_fwd(x, g, out_dtype, name):
    t, d = x.shape
    tm = _row_tile(t)

    def body(x_ref, g_ref, o_ref):
        xv = x_ref[...]
        r = lax.rsqrt(jnp.mean(xv * xv, axis=-1, keepdims=True) + EPS)
        o_ref[...] = (xv * r * g_ref[...]).astype(o_ref.dtype)

    return pl.pallas_call(
        body, name=name, grid=(t // tm,),
        in_specs=[pl.BlockSpec((tm, d), lambda i: (i, 0)), pl.BlockSpec((1, d), lambda i: (0, 0))],
        out_specs=pl.BlockSpec((tm, d), lambda i: (i, 0)),
        out_shape=jax.ShapeDtypeStruct((t, d), out_dtype), compiler_params=_params("parallel"),
    )(x, g.reshape(1, d))


def _rms_bwd(x, g, dh, res, name):
    t, d = x.shape
    tm = _row_tile(t)

    def body(*refs):
        x_ref, g_ref, dh_ref = refs[:3]
        r_ref = refs[3] if res is not None else None
        dx_ref, dxb_ref, dg_ref = refs[-3:]
        xv, dhv = x_ref[...], dh_ref[...]
        r = lax.rsqrt(jnp.mean(xv * xv, axis=-1, keepdims=True) + EPS)
        y = xv * r
        dy = dhv * g_ref[...]
        dx = r * (dy - y * jnp.mean(dy * y, axis=-1, keepdims=True))
        if r_ref is not None:
            dx = r_ref[...] + dx
        dx_ref[...] = dx
        dxb_ref[...] = dx.astype(BF16)

        @pl.when(pl.program_id(0) == 0)
        def _():
            dg_ref[...] = jnp.zeros_like(dg_ref)

        dg_ref[...] += jnp.sum(dhv * y, axis=0, keepdims=True)

    row = pl.BlockSpec((tm, d), lambda i: (i, 0))
    vec = pl.BlockSpec((1, d), lambda i: (0, 0))
    args = (x, g.reshape(1, d), dh) + ((res,) if res is not None else ())
    dx, dxb, dg = pl.pallas_call(
        body, name=name, grid=(t // tm,), in_specs=[row, vec, row] + ([row] if res is not None else []),
        out_specs=(row, row, vec),
        out_shape=(jax.ShapeDtypeStruct((t, d), F32), jax.ShapeDtypeStruct((t, d), BF16),
                   jax.ShapeDtypeStruct((1, d), F32)),
        compiler_params=_params("arbitrary"),
    )(*args)
    return dx, dxb, dg.reshape(d)


FFN_PAIR = 2


def _ffn_up(h, wgu, fp, name):
    t, d = h.shape
    tm, tn = _tile(t, 512), FFN_PAIR * 2 * fp
    nj = wgu.shape[1] // tn

    def body(h_ref, w_ref, fac_ref, z_ref):
        ab = jnp.dot(h_ref[...], w_ref[...], preferred_element_type=F32)
        for e in range(FFN_PAIR):
            av, bv = ab[:, 2 * fp * e:2 * fp * e + fp], ab[:, 2 * fp * e + fp:2 * fp * (e + 1)]
            s = jax.nn.sigmoid(av)
            silu = av * s
            z_ref[:, fp * e:fp * (e + 1)] = (silu * bv).astype(z_ref.dtype)
            fac_ref[:, 2 * fp * e:2 * fp * e + fp] = silu.astype(fac_ref.dtype)
            fac_ref[:, 2 * fp * e + fp:2 * fp * (e + 1)] = (bv * (s + silu * (1.0 - s))).astype(fac_ref.dtype)

    return pl.pallas_call(
        body, name=name, grid=(nj, t // tm),
        in_specs=[pl.BlockSpec((tm, d), lambda j, i: (i, 0)), pl.BlockSpec((d, tn), lambda j, i: (0, j))],
        out_specs=(pl.BlockSpec((tm, tn), lambda j, i: (i, j)), pl.BlockSpec((tm, tn // 2), lambda j, i: (i, j))),
        out_shape=(jax.ShapeDtypeStruct((t, wgu.shape[1]), BF16), jax.ShapeDtypeStruct((t, wgu.shape[1] // 2), BF16)),
        compiler_params=_params("parallel", "parallel"),
    )(h, wgu)


def _ffn_dab(dyb, wd, ab, fp, name):
    t, d = dyb.shape
    tm, tn = _tile(t, 512), FFN_PAIR * 2 * fp
    nj = ab.shape[1] // tn

    def body(dy_ref, wd_ref, fac_ref, dab_ref):
        dz = _dot_nt(dy_ref[...], wd_ref[...]) * 0.5
        for e in range(FFN_PAIR):
            dze = dz[:, fp * e:fp * (e + 1)]
            d_up = fac_ref[:, 2 * fp * e:2 * fp * e + fp].astype(F32)
            d_gate = fac_ref[:, 2 * fp * e + fp:2 * fp * (e + 1)].astype(F32)
            dab_ref[:, 2 * fp * e:2 * fp * e + fp] = (dze * d_gate).astype(dab_ref.dtype)
            dab_ref[:, 2 * fp * e + fp:2 * fp * (e + 1)] = (dze * d_up).astype(dab_ref.dtype)

    return pl.pallas_call(
        body, name=name, grid=(nj, t // tm),
        in_specs=[pl.BlockSpec((tm, d), lambda j, i: (i, 0)), pl.BlockSpec((tn // 2, d), lambda j, i: (j, 0)),
                  pl.BlockSpec((tm, tn), lambda j, i: (i, j))],
        out_specs=pl.BlockSpec((tm, tn), lambda j, i: (i, j)),
        out_shape=jax.ShapeDtypeStruct(ab.shape, BF16), compiler_params=_params("parallel", "parallel"),
    )(dyb, wd, ab)


def _resident(shape):
    return pl.BlockSpec(shape, lambda i: (0,) * len(shape), pipeline_mode=pl.Buffered(1))


def _ffn_bwd_rows(dyb, wd, fac, wgu, x, g, dy, fp, name, after=None):
    t, d = dyb.shape
    width = fac.shape[1]
    tm = _tile(t, 256)

    def body(dyb_ref, wd_ref, fac_ref, wgu_ref, x_ref, g_ref, dy_ref, *refs):
        dab_ref, dx_ref, dxb_ref, dg_ref = refs[-4:]
        dz = _dot_nt(dyb_ref[...], wd_ref[...]) * 0.5
        for e in range(width // (2 * fp)):
            dze = dz[:, fp * e:fp * (e + 1)]
            d_up = fac_ref[:, 2 * fp * e:2 * fp * e + fp].astype(F32)
            d_gate = fac_ref[:, 2 * fp * e + fp:2 * fp * (e + 1)].astype(F32)
            dab_ref[:, 2 * fp * e:2 * fp * e + fp] = (dze * d_gate).astype(dab_ref.dtype)
            dab_ref[:, 2 * fp * e + fp:2 * fp * (e + 1)] = (dze * d_up).astype(dab_ref.dtype)
        dh = _dot_nt(dab_ref[...], wgu_ref[...])
        xv = x_ref[...]
        r = lax.rsqrt(jnp.mean(xv * xv, axis=-1, keepdims=True) + EPS)
        y = xv * r
        dyn = dh * g_ref[...]
        dx = dy_ref[...] + r * (dyn - y * jnp.mean(dyn * y, axis=-1, keepdims=True))
        dx_ref[...] = dx
        dxb_ref[...] = dx.astype(BF16)

        @pl.when(pl.program_id(0) == 0)
        def _():
            dg_ref[...] = jnp.zeros_like(dg_ref)

        dg_ref[...] += jnp.sum(dh * y, axis=0, keepdims=True)

    row = pl.BlockSpec((tm, d), lambda i: (i, 0))
    wide = pl.BlockSpec((tm, width), lambda i: (i, 0))
    vec = pl.BlockSpec((1, d), lambda i: (0, 0))
    in_specs = [row, _resident(wd.shape), wide, _resident(wgu.shape), row, vec, row]
    args = [dyb, wd, fac, wgu, x, g.reshape(1, d), dy]
    if after is not None:
        in_specs.append(pl.BlockSpec(after.shape, lambda i: (0, 0)))
        args.append(after)
    dab, dx, dxb, dg = pl.pallas_call(
        body, name=name, grid=(t // tm,), in_specs=in_specs, out_specs=(wide, row, row, vec),
        out_shape=(jax.ShapeDtypeStruct((t, width), BF16), jax.ShapeDtypeStruct((t, d), F32),
                   jax.ShapeDtypeStruct((t, d), BF16), jax.ShapeDtypeStruct((1, d), F32)),
        compiler_params=_params("arbitrary"),
    )(*args)
    return dab, dx, dxb, dg.reshape(d)


def _matmul_rms_bwd(dab, wgu, x, g, dy, name, after=None):
    t, kdim = dab.shape
    d = wgu.shape[0]
    tm = _tile(t, 256)

    def body(a_ref, b_ref, x_ref, g_ref, dy_ref, *refs):
        dx_ref, dxb_ref, dg_ref = refs[-3:]
        dh = _dot_nt(a_ref[...].astype(BF16), b_ref[...])
        xv = x_ref[...]
        r = lax.rsqrt(jnp.mean(xv * xv, axis=-1, keepdims=True) + EPS)
        y = xv * r
        dyn = dh * g_ref[...]
        dx = dy_ref[...] + r * (dyn - y * jnp.mean(dyn * y, axis=-1, keepdims=True))
        dx_ref[...] = dx
        dxb_ref[...] = dx.astype(BF16)

        @pl.when(pl.program_id(0) == 0)
        def _():
            dg_ref[...] = jnp.zeros_like(dg_ref)

        dg_ref[...] += jnp.sum(dh * y, axis=0, keepdims=True)

    row = pl.BlockSpec((tm, d), lambda i: (i, 0))
    vec = pl.BlockSpec((1, d), lambda i: (0, 0))
    in_specs = [pl.BlockSpec((tm, kdim), lambda i: (i, 0)), pl.BlockSpec((d, kdim), lambda i: (0, 0)), row, vec, row]
    args = [dab, wgu, x, g.reshape(1, d), dy]
    if after is not None:
        in_specs.append(pl.BlockSpec(after.shape, lambda i: (0, 0)))
        args.append(after)
    dx, dxb, dg = pl.pallas_call(
        body, name=name, grid=(t // tm,), in_specs=in_specs, out_specs=(row, row, vec),
        out_shape=(jax.ShapeDtypeStruct((t, d), F32), jax.ShapeDtypeStruct((t, d), BF16),
                   jax.ShapeDtypeStruct((1, d), F32)),
        compiler_params=_params("arbitrary"),
    )(*args)
    return dx, dxb, dg.reshape(d)


def _lat_norm_fwd(p, g_q, g_kv, name):
    t = p.shape[0]
    tm = _row_tile(t)

    def body(p_ref, gq_ref, gkv_ref, q_ref, kv_ref):
        for lo, hi, g_ref, o_ref in ((0, Q_LORA, gq_ref, q_ref), (Q_LORA, LAT, gkv_ref, kv_ref)):
            xv = p_ref[:, lo:hi]
            r = lax.rsqrt(jnp.mean(xv * xv, axis=-1, keepdims=True) + EPS)
            o_ref[...] = (xv * r * g_ref[...]).astype(o_ref.dtype)

    return pl.pallas_call(
        body, name=name, grid=(t // tm,),
        in_specs=[pl.BlockSpec((tm, P_COLS), lambda i: (i, 0)), pl.BlockSpec((1, Q_LORA), lambda i: (0, 0)),
                  pl.BlockSpec((1, KV_LORA), lambda i: (0, 0))],
        out_specs=(pl.BlockSpec((tm, Q_LORA), lambda i: (i, 0)), pl.BlockSpec((tm, KV_LORA), lambda i: (i, 0))),
        out_shape=(jax.ShapeDtypeStruct((t, Q_LORA), BF16), jax.ShapeDtypeStruct((t, KV_LORA), BF16)),
        compiler_params=_params("parallel"),
    )(p, g_q.reshape(1, Q_LORA), g_kv.reshape(1, KV_LORA))


def _lat_norm_bwd(p, g_q, g_kv, dq, dkv, name):
    t = p.shape[0]
    tm = _row_tile(t)

    def body(p_ref, gq_ref, gkv_ref, dq_ref, dkv_ref, dp_ref, dgq_ref, dgkv_ref):
        first = pl.program_id(0) == 0
        for lo, hi, g_ref, d_ref, dg_ref in ((0, Q_LORA, gq_ref, dq_ref, dgq_ref),
                                             (Q_LORA, LAT, gkv_ref, dkv_ref, dgkv_ref)):
            xv, dhv = p_ref[:, lo:hi], d_ref[...]
            r = lax.rsqrt(jnp.mean(xv * xv, axis=-1, keepdims=True) + EPS)
            y = xv * r
            dy = dhv * g_ref[...]
            dp_ref[:, lo:hi] = r * (dy - y * jnp.mean(dy * y, axis=-1, keepdims=True))

            @pl.when(first)
            def _():
                dg_ref[...] = jnp.zeros_like(dg_ref)

            dg_ref[...] += jnp.sum(dhv * y, axis=0, keepdims=True)

    vq = pl.BlockSpec((1, Q_LORA), lambda i: (0, 0))
    vkv = pl.BlockSpec((1, KV_LORA), lambda i: (0, 0))
    dp, dgq, dgkv = pl.pallas_call(
        body, name=name, grid=(t // tm,),
        in_specs=[pl.BlockSpec((tm, P_COLS), lambda i: (i, 0)), vq, vkv,
                  pl.BlockSpec((tm, Q_LORA), lambda i: (i, 0)), pl.BlockSpec((tm, KV_LORA), lambda i: (i, 0))],
        out_specs=(pl.BlockSpec((tm, LAT), lambda i: (i, 0)), vq, vkv),
        out_shape=(jax.ShapeDtypeStruct((t, LAT), F32), jax.ShapeDtypeStruct((1, Q_LORA), F32),
                   jax.ShapeDtypeStruct((1, KV_LORA), F32)),
        compiler_params=_params("arbitrary"),
    )(p, g_q.reshape(1, Q_LORA), g_kv.reshape(1, KV_LORA), dq, dkv)
    return dp, dgq.reshape(Q_LORA), dgkv.reshape(KV_LORA)


def _rope_tables(t):
    half = QK_ROPE // 2
    pos = jnp.arange(t, dtype=F32)
    inv_freq = 1.0 / (ROPE_THETA ** (jnp.arange(0, QK_ROPE, 2, dtype=F32) / QK_ROPE))
    ang = pos[:, None] * inv_freq[None, :]
    cos, sin = jnp.cos(ang), jnp.sin(ang)
    z = lambda n: jnp.zeros((t, n), F32)
    c_tab = jnp.concatenate([jnp.ones((t, QK_NOPE), F32), cos, cos, z(HEAD_PAD - QK_DIM)], axis=1)
    sa_tab = jnp.concatenate([z(QK_NOPE), -sin, z(half), z(HEAD_PAD - QK_DIM)], axis=1)
    sb_tab = jnp.concatenate([z(QK_NOPE), z(half), sin, z(HEAD_PAD - QK_DIM)], axis=1)
    return c_tab, sa_tab, sb_tab


def _rope(x, c, sa, sb):
    half = QK_ROPE // 2
    return x * c + pltpu.roll(x, HEAD_PAD - half, 1) * sa + pltpu.roll(x, half, 1) * sb


def _rope_t(d, c, sa, sb):
    half = QK_ROPE // 2
    return d * c + pltpu.roll(d * sa, half, 1) + pltpu.roll(d * sb, HEAD_PAD - half, 1)


def _head_rms(x):
    r = lax.rsqrt(jnp.sum(x * x, axis=-1, keepdims=True) * (1.0 / QK_DIM) + EPS)
    return x * r, r


def _qk_prep_fwd(q_raw, k_raw, p, gq, gk, tabs, name):
    t, width = q_raw.shape
    tm = _row_tile(t)

    def body(q_ref, k_ref, p_ref, gq_ref, gk_ref, c_ref, sa_ref, sb_ref, qo_ref, ko_ref):
        c, sa, sb, kpe = c_ref[...], sa_ref[...], sb_ref[...], p_ref[...]
        for h in range(N_HEADS):
            cols = slice(h * HEAD_PAD, (h + 1) * HEAD_PAD)
            qn, _ = _head_rms(q_ref[:, cols])
            qo_ref[:, cols] = _rope(qn * gq_ref[...], c, sa, sb).astype(qo_ref.dtype)
            kn, _ = _head_rms(k_ref[:, cols] + kpe)
            ko_ref[:, cols] = _rope(kn * gk_ref[...], c, sa, sb).astype(ko_ref.dtype)

    rows = pl.BlockSpec((tm, width), lambda i: (i, 0))
    tab = pl.BlockSpec((tm, HEAD_PAD), lambda i: (i, 0))
    vec = pl.BlockSpec((1, HEAD_PAD), lambda i: (0, 0))
    kpe_spec = pl.BlockSpec((tm, HEAD_PAD), lambda i: (i, P_KPE // HEAD_PAD))
    return pl.pallas_call(
        body, name=name, grid=(t // tm,), in_specs=[rows, rows, kpe_spec, vec, vec, tab, tab, tab],
        out_specs=(rows, rows),
        out_shape=(jax.ShapeDtypeStruct(q_raw.shape, BF16), jax.ShapeDtypeStruct(k_raw.shape, BF16)),
        compiler_params=_params("parallel"),
    )(q_raw, k_raw, p, gq.reshape(1, HEAD_PAD), gk.reshape(1, HEAD_PAD), *tabs)


def _qk_prep_bwd(q_raw, k_raw, p, dq, dk, gq, gk, tabs, name):
    t, width = q_raw.shape
    tm = _row_tile(t, 256)

    def body(q_ref, k_ref, p_ref, dq_ref, dk_ref, gq_ref, gk_ref, c_ref, sa_ref, sb_ref,
             dqr_ref, dkr_ref, dkpe_ref, dgq_ref, dgk_ref):
        c, sa, sb, kpe = c_ref[...], sa_ref[...], sb_ref[...], p_ref[...]

        def one(x, d, g_ref):
            n, r = _head_rms(x)
            dng = _rope_t(d, c, sa, sb)
            dn = dng * g_ref[...]
            dx = r * (dn - n * (jnp.sum(dn * n, axis=-1, keepdims=True) * (1.0 / QK_DIM)))
            return dx, jnp.sum(dng * n, axis=0, keepdims=True)

        dgq = dgk = dkpe = None
        for h in range(N_HEADS):
            cols = slice(h * HEAD_PAD, (h + 1) * HEAD_PAD)
            dqr, gq_part = one(q_ref[:, cols], dq_ref[:, cols], gq_ref)
            dkr, gk_part = one(k_ref[:, cols] + kpe, dk_ref[:, cols], gk_ref)
            dqr_ref[:, cols] = dqr
            dkr_ref[:, cols] = dkr
            dgq = gq_part if dgq is None else dgq + gq_part
            dgk = gk_part if dgk is None else dgk + gk_part
            dkpe = dkr if dkpe is None else dkpe + dkr
        dkpe_ref[...] = dkpe

        @pl.when(pl.program_id(0) == 0)
        def _():
            dgq_ref[...] = jnp.zeros_like(dgq_ref)
            dgk_ref[...] = jnp.zeros_like(dgk_ref)

        dgq_ref[...] += dgq
        dgk_ref[...] += dgk

    head = pl.BlockSpec((tm, width), lambda i: (i, 0))
    tab = pl.BlockSpec((tm, HEAD_PAD), lambda i: (i, 0))
    vec = pl.BlockSpec((1, HEAD_PAD), lambda i: (0, 0))
    kpe = pl.BlockSpec((tm, HEAD_PAD), lambda i: (i, P_KPE // HEAD_PAD))
    dqr, dkr, dkpe, dgq, dgk = pl.pallas_call(
        body, name=name, grid=(t // tm,), in_specs=[head, head, kpe, head, head, vec, vec, tab, tab, tab],
        out_specs=(head, head, tab, vec, vec),
        out_shape=(jax.ShapeDtypeStruct(q_raw.shape, F32), jax.ShapeDtypeStruct(k_raw.shape, F32),
                   jax.ShapeDtypeStruct((t, HEAD_PAD), F32), jax.ShapeDtypeStruct((1, HEAD_PAD), F32),
                   jax.ShapeDtypeStruct((1, HEAD_PAD), F32)),
        compiler_params=_params("arbitrary"),
    )(q_raw, k_raw, p, dq, dk, gq.reshape(1, HEAD_PAD), gk.reshape(1, HEAD_PAD), *tabs)
    return dqr, dkr, dkpe, dgq.reshape(HEAD_PAD), dgk.reshape(HEAD_PAD)


def _dot_nt(a, b):
    return lax.dot_general(a, b, (((1,), (1,)), ((), ())), preferred_element_type=F32)


def _dot_tn(a, b):
    return lax.dot_general(a, b, (((0,), (0,)), ((), ())), preferred_element_type=F32)


def _diag_mask():
    rows = lax.broadcasted_iota(jnp.int32, (ATTN_BLOCK, ATTN_BLOCK), 0) // CHUNK
    cols = lax.broadcasted_iota(jnp.int32, (ATTN_BLOCK, ATTN_BLOCK), 1) // CHUNK
    return cols <= rows


def _attn_fwd(q, k, v, name):
    t = q.shape[0]
    bq = ATTN_BLOCK
    nq = t // bq

    width = ATTN_HEADS * HEAD_PAD

    def body(q_ref, k_ref, v_ref, o_ref, lse_ref):
        i = pl.program_id(1)
        heads = [slice(e * HEAD_PAD, (e + 1) * HEAD_PAD) for e in range(ATTN_HEADS)]
        qv = [q_ref[:, cols] for cols in heads]

        def block(j, carries, masked):
            rows = pl.ds(pl.multiple_of(j * bq, bq), bq)
            out = []
            for e, (m, l, acc) in enumerate(carries):
                s = _dot_nt(qv[e], k_ref[rows, heads[e]]) * ATTN_SCALE_LOG2
                if masked:
                    s = jnp.where(_diag_mask(), s, -1e30)
                m_new = jnp.maximum(m, jnp.max(s, axis=-1, keepdims=True))
                alpha = jnp.exp2(m - m_new)
                pe = jnp.exp2(s - m_new)
                l = alpha * l + jnp.sum(pe, axis=-1, keepdims=True)
                acc = alpha * acc + jnp.dot(pe.astype(BF16), v_ref[rows, heads[e]], preferred_element_type=F32)
                out.append((m_new, l, acc))
            return tuple(out)

        init = tuple((jnp.full((bq, 1), -1e30, F32), jnp.zeros((bq, 1), F32), jnp.zeros((bq, HEAD_PAD), F32))
                     for _ in heads)
        carries = lax.fori_loop(0, i, lambda j, cr: block(j, cr, False), init)
        for cols, (m, l, acc) in zip(heads, block(i, carries, True)):
            o_ref[:, cols] = acc / l
            lse_ref[:, cols] = jnp.broadcast_to(m + jnp.log2(l), (bq, HEAD_PAD))

    blk = pl.BlockSpec((bq, width), lambda h, i: (i, h))
    full = pl.BlockSpec((t, width), lambda h, i: (0, h))
    return pl.pallas_call(
        body, name=name, grid=(N_HEADS // ATTN_HEADS, nq), in_specs=[blk, full, full], out_specs=(blk, blk),
        out_shape=(jax.ShapeDtypeStruct(q.shape, F32), jax.ShapeDtypeStruct(q.shape, F32)),
        compiler_params=_params("parallel", "parallel"),
    )(q, k, v)


def _attn_bwd(q, k, v, o, lse, do, name):
    t = q.shape[0]
    bq = ATTN_BLOCK
    nq = t // bq

    def body(q_ref, k_ref, v_ref, o_ref, lse_ref, do_ref, dq_ref, dk_ref, dv_ref, delta_ref):
        def rows_of(i):
            return pl.ds(pl.multiple_of(i * bq, bq), bq)

        def prep(i, _):
            r = rows_of(i)
            delta_ref[r, :] = jnp.broadcast_to(jnp.sum(do_ref[r, :] * o_ref[r, :], axis=-1, keepdims=True),
                                               (bq, HEAD_PAD))
            dq_ref[r, :] = jnp.zeros((bq, HEAD_PAD), F32)
            return 0

        lax.fori_loop(0, nq, prep, 0)

        def key_block(j, _):
            rj = rows_of(j)
            kb, vb = k_ref[rj, :], v_ref[rj, :]

            def query_block(i, carry, masked):
                dk, dv = carry
                ri = rows_of(i)
                qb, dob = q_ref[ri, :], do_ref[ri, :].astype(BF16)
                s = _dot_nt(qb, kb) * ATTN_SCALE_LOG2
                if masked:
                    s = jnp.where(_diag_mask(), s, -1e30)
                pe = jnp.exp2(s - lse_ref[ri, :][:, :1])
                dp = _dot_nt(dob, vb)
                ds = (pe * (dp - delta_ref[ri, :][:, :1]) * ATTN_SCALE).astype(BF16)
                dq_ref[ri, :] += jnp.dot(ds, kb, preferred_element_type=F32)
                return dk + _dot_tn(ds, qb), dv + _dot_tn(pe.astype(BF16), dob)

            zero = jnp.zeros((bq, HEAD_PAD), F32)
            carry = query_block(j, (zero, zero), True)
            dk, dv = lax.fori_loop(j + 1, nq, lambda i, cr: query_block(i, cr, False), carry)
            dk_ref[rj, :] = dk
            dv_ref[rj, :] = dv
            return 0

        lax.fori_loop(0, nq, key_block, 0)

    full = pl.BlockSpec((t, HEAD_PAD), lambda h: (0, h))
    shp = jax.ShapeDtypeStruct(q.shape, F32)
    return pl.pallas_call(
        body, name=name, grid=(N_HEADS,), in_specs=[full] * 6, out_specs=(full, full, full),
        out_shape=(shp, shp, shp), scratch_shapes=[pltpu.VMEM((t, HEAD_PAD), F32)],
        compiler_params=_params("parallel"),
    )(q, k, v, o, lse, do)


def _glu_ext(pc_ref, pp_ref, u_ref, tm, first):
    u_ref[CONV_HALO:CONV_HALO + tm, :] = pc_ref[:, P_A:P_G] * jax.nn.sigmoid(pc_ref[:, P_G:P_COLS])
    up = pp_ref[tm - CONV_HALO:tm, P_A:P_G] * jax.nn.sigmoid(pp_ref[tm - CONV_HALO:tm, P_G:P_COLS])
    u_ref[0:CONV_HALO, :] = jnp.where(first, 0.0, up)


SUBLANES = 8


def _shift_copies(src_ref, sh_ref):
    rows = sh_ref.shape[1]
    for b in range(1, SUBLANES):
        sh_ref[b - 1, :, :] = src_ref[b:b + rows, :]


def _rows_at(src_ref, sh_ref, start, n):
    a, b = divmod(start, SUBLANES)
    if b == 0:
        return src_ref[SUBLANES * a:SUBLANES * a + n, :]
    return sh_ref[b - 1, SUBLANES * a:SUBLANES * a + n, :]


def _conv_fwd(p, w, b, ln_g, ln_b, name):
    t = p.shape[0]
    tm = _row_tile(t, CONV_TILE)
    off = CONV_HALO - (CONV_K - 1)

    def body(pc_ref, pp_ref, w_ref, b_ref, g_ref, bb_ref, y_ref, o_ref, u_ref, ush_ref):
        _glu_ext(pc_ref, pp_ref, u_ref, tm, pl.program_id(0) == 0)
        _shift_copies(u_ref, ush_ref)
        acc = jnp.zeros((tm, CONV_W), F32)
        for kk in range(CONV_K):
            acc = acc + w_ref[kk:kk + 1, :] * _rows_at(u_ref, ush_ref, off + kk, tm)
        y = acc + b_ref[...]
        y_ref[...] = y
        xc = y - jnp.mean(y, axis=-1, keepdims=True)
        lo = xc * lax.rsqrt(jnp.mean(xc * xc, axis=-1, keepdims=True) + EPS) * g_ref[...] + bb_ref[...]
        o_ref[...] = (lo * jax.nn.sigmoid(lo)).astype(o_ref.dtype)

    prow = pl.BlockSpec((tm, P_COLS), lambda i: (i, 0))
    pprev = pl.BlockSpec((tm, P_COLS), lambda i: (jnp.maximum(i - 1, 0), 0))
    vec = pl.BlockSpec((1, CONV_W), lambda i: (0, 0))
    row = pl.BlockSpec((tm, CONV_W), lambda i: (i, 0))
    return pl.pallas_call(
        body, name=name, grid=(t // tm,),
        in_specs=[prow, pprev, pl.BlockSpec((CONV_HALO, CONV_W), lambda i: (0, 0)), vec, vec, vec],
        out_specs=(row, row),
        out_shape=(jax.ShapeDtypeStruct((t, CONV_W), F32), jax.ShapeDtypeStruct((t, CONV_W), BF16)),
        scratch_shapes=[pltpu.VMEM((tm + CONV_HALO, CONV_W), F32),
                        pltpu.VMEM((SUBLANES - 1, tm + CONV_HALO - SUBLANES, CONV_W), F32)],
        compiler_params=_params("parallel"),
    )(p, p, w, b.reshape(1, CONV_W), ln_g.reshape(1, CONV_W), ln_b.reshape(1, CONV_W))


def _conv_bwd_ln(y, dout, ln_g, ln_b, name):
    t = y.shape[0]
    tm = _row_tile(t)

    def body(y_ref, d_ref, g_ref, bb_ref, dy_ref, dg_ref, db_ref, dcb_ref):
        yv = y_ref[...]
        xc = yv - jnp.mean(yv, axis=-1, keepdims=True)
        r = lax.rsqrt(jnp.mean(xc * xc, axis=-1, keepdims=True) + EPS)
        n = xc * r
        lo = n * g_ref[...] + bb_ref[...]
        s = jax.nn.sigmoid(lo)
        dlo = d_ref[...] * (s * (1.0 + lo * (1.0 - s)))
        dn = dlo * g_ref[...]
        dy = r * (dn - jnp.mean(dn, axis=-1, keepdims=True) - n * jnp.mean(dn * n, axis=-1, keepdims=True))
        dy_ref[...] = dy

        @pl.when(pl.program_id(0) == 0)
        def _():
            dg_ref[...] = jnp.zeros_like(dg_ref)
            db_ref[...] = jnp.zeros_like(db_ref)
            dcb_ref[...] = jnp.zeros_like(dcb_ref)

        dg_ref[...] += jnp.sum(dlo * n, axis=0, keepdims=True)
        db_ref[...] += jnp.sum(dlo, axis=0, keepdims=True)
        dcb_ref[...] += jnp.sum(dy, axis=0, keepdims=True)

    row = pl.BlockSpec((tm, CONV_W), lambda i: (i, 0))
    vec = pl.BlockSpec((1, CONV_W), lambda i: (0, 0))
    vshape = jax.ShapeDtypeStruct((1, CONV_W), F32)
    dy, dg, db, dcb = pl.pallas_call(
        body, name=name, grid=(t // tm,), in_specs=[row, row, vec, vec], out_specs=(row, vec, vec, vec),
        out_shape=(jax.ShapeDtypeStruct((t, CONV_W), F32), vshape, vshape, vshape),
        compiler_params=_params("arbitrary"),
    )(y, dout, ln_g.reshape(1, CONV_W), ln_b.reshape(1, CONV_W))
    return dy, dg.reshape(CONV_W), db.reshape(CONV_W), dcb.reshape(CONV_W)


def _conv_bwd_taps(p, dy, w, name):
    t = p.shape[0]
    tm = _row_tile(t, CONV_TILE)
    nt = t // tm
    off = CONV_HALO - (CONV_K - 1)

    def body(pc_ref, pp_ref, dyc_ref, dyn_ref, w_ref, dag_ref, dw_ref, u_ref, dye_ref, ush_ref, dysh_ref):
        i = pl.program_id(0)
        _glu_ext(pc_ref, pp_ref, u_ref, tm, i == 0)
        dyc = dyc_ref[...]
        dye_ref[0:tm, :] = dyc
        dye_ref[tm:tm + CONV_HALO, :] = jnp.where(i == nt - 1, 0.0, dyn_ref[0:CONV_HALO, :])

        _shift_copies(u_ref, ush_ref)
        _shift_copies(dye_ref, dysh_ref)

        @pl.when(i == 0)
        def _():
            dw_ref[...] = jnp.zeros_like(dw_ref)

        du = jnp.zeros((tm, CONV_W), F32)
        for kk in range(CONV_K):
            dw_ref[kk:kk + 1, :] += jnp.sum(dyc * _rows_at(u_ref, ush_ref, off + kk, tm), axis=0, keepdims=True)
            du = du + w_ref[kk:kk + 1, :] * _rows_at(dye_ref, dysh_ref, CONV_K - 1 - kk, tm)
        av, gv = pc_ref[:, P_A:P_G], pc_ref[:, P_G:P_COLS]
        s = jax.nn.sigmoid(gv)
        dag_ref[:, 0:CONV_W] = du * s
        dag_ref[:, CONV_W:2 * CONV_W] = du * av * (s * (1.0 - s))

    prow = pl.BlockSpec((tm, P_COLS), lambda i: (i, 0))
    pprev = pl.BlockSpec((tm, P_COLS), lambda i: (jnp.maximum(i - 1, 0), 0))
    row = pl.BlockSpec((tm, CONV_W), lambda i: (i, 0))
    nxt = pl.BlockSpec((tm, CONV_W), lambda i: (jnp.minimum(i + 1, nt - 1), 0))
    wspec = pl.BlockSpec((CONV_HALO, CONV_W), lambda i: (0, 0))
    return pl.pallas_call(
        body, name=name, grid=(nt,), in_specs=[prow, pprev, row, nxt, wspec],
        out_specs=(pl.BlockSpec((tm, 2 * CONV_W), lambda i: (i, 0)), wspec),
        out_shape=(jax.ShapeDtypeStruct((t, 2 * CONV_W), F32), jax.ShapeDtypeStruct((CONV_HALO, CONV_W), F32)),
        scratch_shapes=[pltpu.VMEM((tm + CONV_HALO, CONV_W), F32), pltpu.VMEM((tm + CONV_HALO, CONV_W), F32),
                        pltpu.VMEM((SUBLANES - 1, tm + CONV_HALO - SUBLANES, CONV_W), F32),
                        pltpu.VMEM((SUBLANES - 1, tm + CONV_HALO - SUBLANES, CONV_W), F32)],
        compiler_params=_params("arbitrary"),
    )(p, p, dy, dy, w)


def _loss_head(y, target, name):
    t, d = y.shape
    tm = _row_tile(t)

    def body(y_ref, t_ref, l_ref, dy_ref):
        err = y_ref[...] - t_ref[...]
        dy_ref[...] = err * (1.0 / d)

        @pl.when(pl.program_id(0) == 0)
        def _():
            l_ref[...] = jnp.zeros_like(l_ref)

        row = jnp.sum(err * err, axis=-1, keepdims=True) * (0.5 / d)
        l_ref[...] += jnp.broadcast_to(jnp.sum(row, axis=0, keepdims=True), (1, LANE))

    row = pl.BlockSpec((tm, d), lambda i: (i, 0))
    return pl.pallas_call(
        body, name=name, grid=(t // tm,), in_specs=[row, row],
        out_specs=(pl.BlockSpec((1, LANE), lambda i: (0, 0)), row),
        out_shape=(jax.ShapeDtypeStruct((1, LANE), F32), jax.ShapeDtypeStruct((t, d), F32)),
        compiler_params=_params("arbitrary"),
    )(y, target)


def _adamw(w, g, m, v, name, after=None):
    r, c = w.shape
    tr = _row_tile(r, 256)
    c1, c2 = 1.0 - ADAM_B1 ** ADAM_STEP, 1.0 - ADAM_B2 ** ADAM_STEP

    def body(w_ref, g_ref, m_ref, v_ref, *refs):
        d_ref, mo_ref, vo_ref = refs[-3:]
        gv = g_ref[...]
        mn = ADAM_B1 * m_ref[...] + (1.0 - ADAM_B1) * gv
        vn = ADAM_B2 * v_ref[...] + (1.0 - ADAM_B2) * (gv * gv)
        mo_ref[...] = mn
        vo_ref[...] = vn
        d_ref[...] = -ADAM_LR * ((mn / c1) / (jnp.sqrt(vn / c2) + ADAM_EPS) + ADAM_WD * w_ref[...])

    blk = pl.BlockSpec((tr, c), lambda i: (i, 0))
    shp = jax.ShapeDtypeStruct((r, c), F32)
    extra = [] if after is None else [pl.BlockSpec(after.shape, lambda i: (0, 0))]
    return pl.pallas_call(
        body, name=name, grid=(r // tr,), in_specs=[blk] * 4 + extra, out_specs=(blk, blk, blk),
        out_shape=(shp, shp, shp), compiler_params=_params("parallel"),
    )(w, g, m, v, *([] if after is None else [after]))


def _sum_parts(parts, name):
    r, c = parts[0].shape
    tr = _row_tile(r, 256)

    def body(*refs):
        acc = refs[0][...]
        for ref in refs[1:-1]:
            acc = acc + ref[...]
        refs[-1][...] = acc

    blk = pl.BlockSpec((tr, c), lambda i: (i, 0))
    return pl.pallas_call(
        body, name=name, grid=(r // tr,), in_specs=[blk] * len(parts), out_specs=blk,
        out_shape=jax.ShapeDtypeStruct((r, c), F32), compiler_params=_params("parallel"),
    )(*parts)


def _place():
    return lax.axis_index("x"), lax.axis_index("y"), lax.axis_index("c")


def _window(ref, block, size, axis):
    start = pl.multiple_of(block * size, LANE if size % LANE == 0 else 8)
    return ref.at[(slice(None),) * axis + (pl.ds(start, size),)]


def _all_gather(pieces, name, in_vmem=False, seeds=()):
    n_p, pieces = len(pieces), list(pieces) + list(seeds)
    n_all = len(pieces)

    def body(*refs):
        x_refs, out_refs = refs[:n_all], refs[n_all:2 * n_all]
        send_sems, recv_sems, local_sems = refs[2 * n_all:]
        px, py, pc = _place()
        me, sibling = (px, py, pc), (px, py, 1 - pc)
        chips = [(1 - px, py), (px, 1 - py), (1 - px, 1 - py)]

        def win(p, block):
            bx, by, bc = block
            x, axis = pieces[p]
            return _window(out_refs[p], 4 * bx + 2 * by + bc, x.shape[axis], axis)

        def copy(k, p, block, to, local=False):
            return pltpu.make_async_remote_copy(
                src_ref=x_refs[p] if local else win(p, block), dst_ref=win(p, block),
                send_sem=send_sems.at[k, p], recv_sem=recv_sems.at[k, p], device_id=to, device_id_type=MESH_ID)

        every = range(n_p)
        mine = [pltpu.make_async_copy(x_refs[p], win(p, me), local_sems.at[p]) for p in range(n_all)]
        first = [copy(0, p, me, sibling, local=True) for p in every]
        first += [copy(1 + j, p, me, (*chip, pc), local=True) for j, chip in enumerate(chips) for p in every]
        for cp in mine + first:
            cp.start()
        passed = []
        for j, chip in enumerate(chips):
            for p in every:
                copy(1 + j, p, (*chip, pc), me).wait_recv()
                passed.append(copy(4 + j, p, (*chip, pc), sibling))
                passed[-1].start()
        for p in every:
            copy(0, p, sibling, me).wait_recv()
        for j, chip in enumerate(chips):
            for p in every:
                copy(4 + j, p, (*chip, 1 - pc), me).wait_recv()
        for cp in first + passed:
            cp.wait_send()
        for cp in mine:
            cp.wait()

    def gathered(x, axis):
        return jax.ShapeDtypeStruct(x.shape[:axis] + (N_DEV * x.shape[axis],) + x.shape[axis + 1:], x.dtype)

    spec = VMEM_SPEC if in_vmem else ANY
    return pl.pallas_call(
        body, name=name, in_specs=[spec] * n_all, out_specs=[spec] * n_all,
        out_shape=[gathered(*pc_) for pc_ in pieces],
        scratch_shapes=[pltpu.SemaphoreType.DMA((7, n_p)), pltpu.SemaphoreType.DMA((7, n_p)),
                        pltpu.SemaphoreType.DMA((n_all,))],
        compiler_params=pltpu.CompilerParams(vmem_limit_bytes=VMEM_LIMIT),
    )(*[x for x, _ in pieces])


def _start_copies(bufs, n_copies, plan, name, after=None):
    nb = len(bufs)
    n_in = nb + (after is not None)

    def body(*refs):
        send_sems, recv_sems, token = refs[n_in], refs[n_in + 1], refs[-1]
        for i, (src, dst, dev) in enumerate(plan(refs[:nb])):
            pltpu.make_async_remote_copy(src_ref=src, dst_ref=dst, send_sem=send_sems.at[i], recv_sem=recv_sems.at[i],
                                         device_id=dev, device_id_type=MESH_ID).start()
        token[...] = jnp.zeros_like(token)

    out = pl.pallas_call(
        body, name=name, in_specs=[HBM_SPEC] * nb + [ANY] * (after is not None),
        out_shape=(pltpu.SemaphoreType.DMA((n_copies,)), pltpu.SemaphoreType.DMA((n_copies,)),
                   *[pltpu.HBM(b.shape, b.dtype) for b in bufs], jax.ShapeDtypeStruct((8, LANE), F32)),
        out_specs=(SEM_SPEC, SEM_SPEC, *[HBM_SPEC] * nb, VMEM_SPEC),
        input_output_aliases={i: 2 + i for i in range(nb)},
        compiler_params=pltpu.CompilerParams(has_side_effects=DATAFLOW),
    )(*[pltpu.with_memory_space_constraint(b, pltpu.HBM) for b in bufs], *([after] if after is not None else []))
    return out[0], out[1], list(out[2:2 + nb]), out[-1]


def _wait_copies(started, after, n_copies, plan, name):
    send_sems, recv_sems, bufs, _ = started
    nb = len(bufs)

    def body(*refs):
        send_ref, recv_ref = refs[nb], refs[nb + 1]
        copies = [pltpu.make_async_remote_copy(src_ref=src, dst_ref=dst, send_sem=send_ref.at[i], recv_sem=recv_ref.at[i],
                                               device_id=dev, device_id_type=MESH_ID)
                  for i, (src, dst, dev) in enumerate(plan(refs[:nb]))]
        for cp in copies:
            cp.wait_send()
        for cp in copies:
            cp.wait_recv()

    out = pl.pallas_call(
        body, name=name, in_specs=[HBM_SPEC] * nb + [SEM_SPEC, SEM_SPEC, ANY],
        out_shape=tuple(pltpu.HBM(b.shape, b.dtype) for b in bufs), out_specs=tuple([HBM_SPEC] * nb),
        input_output_aliases={i: i for i in range(nb)},
        compiler_params=pltpu.CompilerParams(has_side_effects=DATAFLOW),
    )(*bufs, send_sems, recv_sems, after)
    return list(out)


def _after(x, token):
    return x + token[0, 0].astype(x.dtype)


def _other_chips():
    px, py, _ = _place()
    return [(1 - px, py), (px, 1 - py), (1 - px, 1 - py)]


def _exchange(srcs, slots, src_block, target, name):
    n_p = len(srcs)

    def body(*refs):
        src_refs, out_refs, send_sems, recv_sems = refs[:n_p], refs[n_p:2 * n_p], refs[-2], refs[-1]
        copies = [pltpu.make_async_remote_copy(
            src_ref=src_refs[p].at[src_block(s)], dst_ref=out_refs[p].at[s], send_sem=send_sems.at[s, p],
            recv_sem=recv_sems.at[s, p], device_id=target(s), device_id_type=MESH_ID)
            for s in range(slots) for p in range(n_p)]
        for cp in copies:
            cp.start()
        for cp in copies:
            cp.wait_recv()
        for cp in copies:
            cp.wait_send()

    return pl.pallas_call(
        body, name=name, in_specs=[ANY] * n_p, out_specs=[ANY] * n_p,
        out_shape=[jax.ShapeDtypeStruct((slots,) + a.shape[1:], a.dtype) for a in srcs],
        scratch_shapes=[pltpu.SemaphoreType.DMA((slots, n_p)), pltpu.SemaphoreType.DMA((slots, n_p))],
        compiler_params=pltpu.CompilerParams(vmem_limit_bytes=VMEM_LIMIT),
    )(*srcs)


def _blocks_to_sibling(sends, name):
    def src_block(j):
        return 2 * j + 1 - lax.axis_index("c")

    def target(j):
        px, py, pc = _place()
        return (px, py, 1 - pc)

    return _exchange(sends, 4, src_block, target, name)


def _pair_sums_for_chips(own, got, name):
    _, r, c = own.shape
    tr = _row_tile(r, 256, 16)

    def body(idx_ref, own_ref, got_ref, o_ref):
        o_ref[...] = (own_ref[...] + got_ref[...].astype(F32)).astype(o_ref.dtype)

    grid_spec = pltpu.PrefetchScalarGridSpec(
        num_scalar_prefetch=1, grid=(3, r // tr),
        in_specs=[pl.BlockSpec((None, tr, c), lambda k, i, idx: (idx[k], i, 0)),
                  pl.BlockSpec((None, tr, c), lambda k, i, idx: (idx[3 + k], i, 0))],
        out_specs=pl.BlockSpec((None, tr, c), lambda k, i, idx: (k, i, 0)))
    chips = [2 * cx + cy for cx, cy in _other_chips()]
    idx = jnp.stack([2 * j + lax.axis_index("c") for j in chips] + chips).astype(jnp.int32)
    return pl.pallas_call(
        body, name=name, grid_spec=grid_spec, out_shape=jax.ShapeDtypeStruct((3, r, c), BF16),
        compiler_params=_params("parallel", "parallel"),
    )(idx, own, got)


def _sum_for_me(own, got_sibling, got_chips, name):
    _, r, c = own.shape
    tr = _row_tile(r, 256, 16)

    def body(idx_ref, own_ref, sib_ref, g0_ref, g1_ref, g2_ref, o_ref):
        acc = own_ref[...] + sib_ref[...].astype(F32)
        for ref in (g0_ref, g1_ref, g2_ref):
            acc = acc + ref[...].astype(F32)
        o_ref[...] = acc

    def part(k):
        return pl.BlockSpec((None, tr, c), lambda i, idx: (k, i, 0))

    grid_spec = pltpu.PrefetchScalarGridSpec(
        num_scalar_prefetch=1, grid=(r // tr,),
        in_specs=[pl.BlockSpec((None, tr, c), lambda i, idx: (idx[0], i, 0)),
                  pl.BlockSpec((None, tr, c), lambda i, idx: (idx[1], i, 0)), part(0), part(1), part(2)],
        out_specs=pl.BlockSpec((tr, c), lambda i, idx: (i, 0)))
    px, py, pc = _place()
    idx = jnp.stack([4 * px + 2 * py + pc, 2 * px + py]).astype(jnp.int32)
    return pl.pallas_call(
        body, name=name, grid_spec=grid_spec, out_shape=jax.ShapeDtypeStruct((r, c), F32),
        compiler_params=_params("parallel"),
    )(idx, own, got_sibling, got_chips, got_chips, got_chips)


def _chip_plan(n_p):
    def plan(refs):
        pc = lax.axis_index("c")
        return [(refs[p].at[k], refs[n_p + p].at[k], (cx, cy, pc))
                for p in range(n_p) for k, (cx, cy) in enumerate(_other_chips())]
    return plan


def _reduce_start(own, sends, tag):
    from_sibling = _blocks_to_sibling(sends, "grads_to_sibling_" + tag)
    pair_sums = [_pair_sums_for_chips(a, b, "grads_pair_sums") for a, b in zip(own, from_sibling)]
    lands = [lax.empty(a.shape, a.dtype) for a in pair_sums]
    started = _start_copies(pair_sums + lands, 3 * len(own), _chip_plan(len(own)), "grads_to_chips_start_" + tag)
    return from_sibling, started


def _reduce_finish(own, from_sibling, started, after, tag):
    n_p = len(own)
    bufs = _wait_copies(started, after, 3 * n_p, _chip_plan(n_p), "grads_to_chips_wait_" + tag)
    return [_sum_for_me(a, b, c, "grads_sum") for a, b, c in zip(own, from_sibling, bufs[n_p:])]


def _direct_plan(n_p):
    def plan(refs):
        px, py, pc = _place()
        out = []
        for p in range(n_p):
            for m in range(1, N_DEV):
                tx = 1 - px if m & 4 else px
                ty = 1 - py if m & 2 else py
                tc = 1 - pc if m & 1 else pc
                out.append((refs[p].at[4 * tx + 2 * ty + tc], refs[n_p + p].at[m - 1], (tx, ty, tc)))
        return out
    return plan


def _sum_direct(own, got, name):
    _, r, c = own.shape
    tr = _row_tile(r, 256, 16)

    def body(idx_ref, own_ref, *refs):
        acc = own_ref[...]
        for ref in refs[:-1]:
            acc = acc + ref[...].astype(F32)
        refs[-1][...] = acc

    def part(k):
        return pl.BlockSpec((None, tr, c), lambda i, idx: (k, i, 0))

    grid_spec = pltpu.PrefetchScalarGridSpec(
        num_scalar_prefetch=1, grid=(r // tr,),
        in_specs=[pl.BlockSpec((None, tr, c), lambda i, idx: (idx[0], i, 0))] + [part(k) for k in range(N_DEV - 1)],
        out_specs=pl.BlockSpec((tr, c), lambda i, idx: (i, 0)))
    px, py, pc = _place()
    idx = (4 * px + 2 * py + pc).astype(jnp.int32).reshape(1)
    return pl.pallas_call(
        body, name=name, grid_spec=grid_spec, out_shape=jax.ShapeDtypeStruct((r, c), F32),
        compiler_params=_params("parallel"),
    )(idx, own, *([got] * (N_DEV - 1)))


def _reduce_direct_start(sends, tag):
    lands = [lax.empty((N_DEV - 1,) + a.shape[1:], a.dtype) for a in sends]
    return _start_copies(list(sends) + lands, (N_DEV - 1) * len(sends), _direct_plan(len(sends)),
                         "grads_direct_start_" + tag)


def _reduce_direct_finish(own, started, after, tag):
    n_p = len(own)
    bufs = _wait_copies(started, after, (N_DEV - 1) * n_p, _direct_plan(n_p), "grads_direct_wait_" + tag)
    return [_sum_direct(a, b, "grads_sum_direct") for a, b in zip(own, bufs[n_p:])]


def _gather_plans(pieces):
    n_p = len(pieces)
    dims = [(x.shape[axis], axis) for x, axis in pieces]

    def first(refs):
        px, py, pc = _place()
        targets = [(px, py, 1 - pc)] + [(cx, cy, pc) for cx, cy in _other_chips()]
        return [(refs[p], _window(refs[n_p + p], 4 * px + 2 * py + pc, *dims[p]), to)
                for p in range(n_p) for to in targets]

    def second(refs):
        px, py, pc = _place()
        out = []
        for p in range(n_p):
            for cx, cy, cc in [(cx, cy, pc) for cx, cy in _other_chips()] + [(px, py, 1 - pc)]:
                win = _window(refs[p], 4 * cx + 2 * cy + cc, *dims[p])
                out.append((win, win, (px, py, 1 - pc)))
        return out

    return first, second


FORWARD_COPIES = 4


def _flat_rows(a, width):
    return a.reshape(-1, width)


def _full_from_blocks(blocks, name):
    if name in COL_SHARDED:
        _, l, k, nb = blocks.shape
        return jnp.transpose(blocks, (1, 2, 0, 3)).reshape(l, k, N_DEV * nb)
    _, l, rb, n = blocks.shape
    return jnp.transpose(blocks, (1, 0, 2, 3)).reshape(l, N_DEV * rb, n)


def _blocks_from_full(full, name):
    if name in COL_SHARDED:
        l, k, n = full.shape
        return jnp.transpose(full.reshape(l, k, N_DEV, n // N_DEV), (2, 0, 1, 3))
    l, rows, n = full.shape
    return jnp.transpose(full.reshape(l, N_DEV, rows // N_DEV, n), (1, 0, 2, 3))


def _pad_heads(w, width):
    k = w.shape[0]
    return jnp.pad(w.reshape(k, N_HEADS, width), ((0, 0), (0, 0), (0, HEAD_PAD - width))).reshape(k, N_HEADS * HEAD_PAD)


def _unpad_heads(w, width):
    k = w.shape[0]
    return w.reshape(k, N_HEADS, HEAD_PAD)[:, :, :width].reshape(k, N_HEADS * width)


def _layer_operands(full, vec, conv_w_full, l):
    w_in = full['w_in'][l]
    kpe = jnp.pad(w_in[:, LAT:LAT + QK_ROPE], ((0, 0), (QK_NOPE, HEAD_PAD - QK_DIM)))
    w_ukv = full['w_ukv'][l].reshape(KV_LORA, N_HEADS, QK_NOPE + V_DIM)
    w_out = full['w_out'][l]
    d_model = w_out.shape[1]
    wo_attn = jnp.pad(w_out[:N_HEADS * V_DIM].reshape(N_HEADS, V_DIM, d_model),
                      ((0, 0), (0, HEAD_PAD - V_DIM), (0, 0))).reshape(N_HEADS * HEAD_PAD, d_model)
    ops = {
        'w_in': jnp.concatenate([w_in[:, :LAT], kpe, w_in[:, LAT + QK_ROPE:]], axis=1),
        'w_q': _pad_heads(full['w_uq'][l], QK_DIM),
        'w_k': _pad_heads(w_ukv[:, :, :QK_NOPE].reshape(KV_LORA, N_HEADS * QK_NOPE), QK_NOPE),
        'w_v': _pad_heads(w_ukv[:, :, QK_NOPE:].reshape(KV_LORA, N_HEADS * V_DIM), V_DIM),
        'wo_attn': wo_attn,
        'wo_conv': w_out[N_HEADS * V_DIM:],
        'conv_w': jnp.pad(conv_w_full[l], ((0, CONV_HALO - CONV_K), (0, 0))),
        'gq': jnp.pad(vec['q_norm'][l], (0, HEAD_PAD - QK_DIM)),
        'gk': jnp.pad(vec['k_norm'][l], (0, HEAD_PAD - QK_DIM)),
    }
    for n in ('ffn1_norm', 'mix_norm', 'q_latent_norm', 'kv_latent_norm', 'conv_b', 'conv_ln_g', 'conv_ln_b',
              'ffn2_norm', 'post_norm'):
        ops[n] = vec[n][l]
    return ops


def _ffn_fwd(x, g, wgu, wd, fp):
    h = _rms_fwd(x, g, BF16, "rms_fwd_ffn")
    ab, z = _ffn_up(h, wgu, fp, "ffn_up")
    y = _mm(z, wd, res=x, scale=0.5, name="ffn_down")
    return y, (x, h, ab, z)


def _ffn_bwd(dy, dyb, saved, g, wgu, wd, fp, after_dw=None, after=None, before_dw=None):
    x, h, ab, z = saved
    dab, dx, dxb, dg = _ffn_bwd_rows(dyb, wd, ab, wgu, x, g, dy, fp, "ffn_bwd_rows", after=after)
    first = before_dw(dg) if before_dw is not None else None
    d_wgu = _mm(h, dab, ta=True, blocks=('col', N_DEV), tm=h.shape[1], after=first, name="ffn_dwgu")
    d_wd = _mm(z, dyb, ta=True, scale=0.5, blocks=('row', N_DEV), name="ffn_dwd")
    token = after_dw(d_wgu, d_wd) if after_dw is not None else None
    return dx, dxb, dg, token


def _mixer_fwd(x, ops, tabs, after_attention=None):
    h = _rms_fwd(x, ops['mix_norm'], BF16, "rms_fwd_mix")
    p = _mm(h, ops['w_in'], name="mix_in")
    qln, kvln = _lat_norm_fwd(p, ops['q_latent_norm'], ops['kv_latent_norm'], "lat_norm_fwd")
    q_raw = _mm(qln, ops['w_q'], name="mix_q")
    k_raw = _mm(kvln, ops['w_k'], name="mix_k")
    v = _mm(kvln, ops['w_v'], out_dtype=BF16, name="mix_v")
    q, k = _qk_prep_fwd(q_raw, k_raw, p, ops['gq'], ops['gk'], tabs, "qk_prep_fwd")
    o, lse = _attn_fwd(q, k, v, "attn_fwd")
    token = after_attention(o) if after_attention is not None else None
    conv_b = ops['conv_b'] if token is None else _after(ops['conv_b'], token)
    y_conv, cv = _conv_fwd(p, ops['conv_w'], conv_b, ops['conv_ln_g'], ops['conv_ln_b'], "conv_fwd")
    x_attn = _mm(o, ops['wo_attn'], res=x, name="mix_out_attn")
    x_out = _mm(cv, ops['wo_conv'], res=x_attn, name="mix_out_conv")
    return x_out, (x, h, p, qln, kvln, q_raw, k_raw, v, q, k, o, lse, y_conv, cv)


def _mixer_bwd(dx_out, dxb_out, saved, ops, tabs, token=None):
    x, h, p, qln, kvln, q_raw, k_raw, v, q, k, o, lse, y_conv, cv = saved
    g = {}
    do = _mm(dxb_out, ops['wo_attn'], tb=True, after=token, name="mix_do")
    dcv = _mm(dxb_out, ops['wo_conv'], tb=True, name="mix_dcv")
    g['wo_attn'] = _mm(o, dxb_out, ta=True, name="mix_dwo_attn")
    g['wo_conv'] = _mm(cv, dxb_out, ta=True, name="mix_dwo_conv")
    dq, dk, dv = _attn_bwd(q, k, v, o, lse, do, "attn_bwd")
    dq_raw, dk_raw, dkpe, g['gq'], g['gk'] = _qk_prep_bwd(q_raw, k_raw, p, dq, dk, ops['gq'], ops['gk'], tabs,
                                                          "qk_prep_bwd")
    g['w_q'] = _mm(qln, dq_raw, ta=True, name="mix_dwq")
    g['w_k'] = _mm(kvln, dk_raw, ta=True, name="mix_dwk")
    g['w_v'] = _mm(kvln, dv, ta=True, name="mix_dwv")
    dqln = _mm(dq_raw, ops['w_q'], tb=True, name="mix_dqln")
    dkvln = _mm(dk_raw, ops['w_k'], tb=True, name="mix_dkvln_k")
    dkvln = _mm(dv, ops['w_v'], tb=True, res=dkvln, name="mix_dkvln_v")
    dp_lat, g['q_latent_norm'], g['kv_latent_norm'] = _lat_norm_bwd(
        p, ops['q_latent_norm'], ops['kv_latent_norm'], dqln, dkvln, "lat_norm_bwd")
    dy_conv, g['conv_ln_g'], g['conv_ln_b'], g['conv_b'] = _conv_bwd_ln(
        y_conv, dcv, ops['conv_ln_g'], ops['conv_ln_b'], "conv_bwd_ln")
    dag, g['conv_w'] = _conv_bwd_taps(p, dy_conv, ops['conv_w'], "conv_bwd_taps")
    dp = jnp.concatenate([dp_lat, dkpe, dag], axis=1)
    g['w_in'] = _mm(h, dp, ta=True, name="mix_dw_in")
    dx, dxb, g['mix_norm'] = _matmul_rms_bwd(dp, ops['w_in'], x, ops['mix_norm'], dx_out, "mix_dh_rms")
    return dx, dxb, g


def _mixer_grads_to_params(g):
    d_w_in = g['w_in']
    d_wk = _unpad_heads(g['w_k'], QK_NOPE).reshape(KV_LORA, N_HEADS, QK_NOPE)
    d_wv = _unpad_heads(g['w_v'], V_DIM).reshape(KV_LORA, N_HEADS, V_DIM)
    d_model = g['wo_attn'].shape[1]
    d_wo_attn = g['wo_attn'].reshape(N_HEADS, HEAD_PAD, d_model)[:, :V_DIM].reshape(N_HEADS * V_DIM, d_model)
    return {
        'mix_norm': g['mix_norm'],
        'w_in': jnp.concatenate([d_w_in[:, :LAT], d_w_in[:, LAT + QK_NOPE:LAT + QK_DIM], d_w_in[:, P_A:]], axis=1),
        'q_latent_norm': g['q_latent_norm'], 'w_uq': _unpad_heads(g['w_q'], QK_DIM),
        'kv_latent_norm': g['kv_latent_norm'],
        'w_ukv': jnp.concatenate([d_wk, d_wv], axis=2).reshape(KV_LORA, N_HEADS * (QK_NOPE + V_DIM)),
        'q_norm': g['gq'][:QK_DIM], 'k_norm': g['gk'][:QK_DIM], 'conv_w': g['conv_w'][:CONV_K],
        'conv_b': g['conv_b'], 'conv_ln_g': g['conv_ln_g'], 'conv_ln_b': g['conv_ln_b'],
        'w_out': jnp.concatenate([d_wo_attn, g['wo_conv']], axis=0),
    }


def kernel(x, ffn1_norm, ffn1_w_gate, ffn1_w_up, ffn1_w_down, mix_norm, w_in, q_latent_norm, w_uq, kv_latent_norm, w_ukv, q_norm, k_norm, conv_w, conv_b, conv_ln_g, conv_ln_b, w_out, ffn2_norm, ffn2_w_gate, ffn2_w_up, ffn2_w_down, post_norm, loss_target, m_ffn1_norm, m_ffn1_w_gate, m_ffn1_w_up, m_ffn1_w_down, m_mix_norm, m_w_in, m_q_latent_norm, m_w_uq, m_kv_latent_norm, m_w_ukv, m_q_norm, m_k_norm, m_conv_w, m_conv_b, m_conv_ln_g, m_conv_ln_b, m_w_out, m_ffn2_norm, m_ffn2_w_gate, m_ffn2_w_up, m_ffn2_w_down, m_post_norm, v_ffn1_norm, v_ffn1_w_gate, v_ffn1_w_up, v_ffn1_w_down, v_mix_norm, v_w_in, v_q_latent_norm, v_w_uq, v_kv_latent_norm, v_w_ukv, v_q_norm, v_k_norm, v_conv_w, v_conv_b, v_conv_ln_g, v_conv_ln_b, v_w_out, v_ffn2_norm, v_ffn2_w_gate, v_ffn2_w_up, v_ffn2_w_down, v_post_norm):
    args = locals()
    w = {n: args[n] for n in WEIGHTS}
    mom = {n: args["m_" + n] for n in WEIGHTS}
    var = {n: args["v_" + n] for n in WEIGHTS}
    depth = ffn1_norm.shape[0]
    x0 = x.reshape(x.shape[-2:])
    target = loss_target.reshape(loss_target.shape[-2:])
    t, d_model = x0.shape
    my_block = 4 * lax.axis_index("x") + 2 * lax.axis_index("y") + lax.axis_index("c")

    fb = ffn1_w_gate.shape[-1]
    fp = -(-fb // LANE) * LANE
    ffns = [(l, f) for l in range(depth) for f in (1, 2)]
    pad_cols = lambda a: jnp.pad(a, ((0, 0), (0, fp - fb)))
    gu_local = {(l, f): jnp.concatenate([pad_cols(w[f'ffn{f}_w_gate'][l]), pad_cols(w[f'ffn{f}_w_up'][l])],
                                        axis=1).astype(BF16) for l, f in ffns}
    dn_local = {(l, f): jnp.pad(w[f'ffn{f}_w_down'][l], ((0, fp - fb), (0, 0))).astype(BF16) for l, f in ffns}
    rows_of = {n: w[n].size // d_model for n in REST}
    rest_local = jnp.concatenate([_flat_rows(w[n].astype(BF16), d_model) for n in REST], axis=0)
    n_rest = rest_local.shape[0]
    first_ffn, later = ffns[0], ffns[1:]
    cw = conv_w.reshape(-1)
    cw_rows = -(-cw.size // (8 * LANE)) * 8
    cw_flat = jnp.pad(cw, (0, cw_rows * LANE - cw.size)).reshape(cw_rows, LANE)
    later_pieces = [(gu_local[q], 1) for q in later] + [(dn_local[q], 0) for q in later]
    n_later = len(later_pieces)
    got = _all_gather([(gu_local[first_ffn], 1), (dn_local[first_ffn], 0)], "gather_first")
    wgu, wd = {first_ffn: got[0]}, {first_ffn: got[1]}
    mixer_pieces = [(rest_local, 0), (cw_flat, 0)]
    mixer_plan, mixer_forward_plan = _gather_plans(mixer_pieces)
    gather_plan, forward_plan = _gather_plans(later_pieces)

    def landing(a, axis):
        return lax.empty(a.shape[:axis] + (N_DEV * a.shape[axis],) + a.shape[axis + 1:], a.dtype)

    gather_mixer = _start_copies([a for a, _ in mixer_pieces] + [landing(a, ax) for a, ax in mixer_pieces],
                                 4 * len(mixer_pieces), mixer_plan, "gather_mixer_start", after=got[0])
    gather_later = _start_copies([a for a, _ in later_pieces] + [landing(a, ax) for a, ax in later_pieces],
                                 4 * n_later, gather_plan, "gather_later_start", after=gather_mixer[3])

    x1_first, s1_first = _ffn_fwd(x0, _after(w['ffn1_norm'][0], gather_later[3]), wgu[first_ffn], wd[first_ffn], fp)
    lands = _wait_copies(gather_mixer, x1_first, 4 * len(mixer_pieces), mixer_plan, "gather_mixer_wait")[2:]
    pass_on = _start_copies(lands, FORWARD_COPIES * len(mixer_pieces), mixer_forward_plan, "gather_mixer_forward_start")
    lands = _wait_copies(pass_on, pass_on[3], FORWARD_COPIES * len(mixer_pieces), mixer_forward_plan, "gather_mixer_forward_wait")
    gathered = lands[0].reshape(N_DEV, n_rest, d_model)
    cw_all = lands[1].reshape(N_DEV, cw_rows * LANE)[:, :cw.size]
    full, start = {}, 0
    for n in REST:
        blocks = gathered[:, start:start + rows_of[n]].reshape((N_DEV,) + w[n].shape)
        full[n] = _full_from_blocks(blocks, n)
        start += rows_of[n]
    conv_w_full =jnp.transpose(cw_all.reshape((N_DEV,) + conv_w.shape), (1, 2, 0, 3)).reshape(depth, CONV_K, CONV_W)
    vec = {n: w[n] for n in VECTORS}
    ops = [_layer_operands(full, vec, conv_w_full, l) for l in range(depth)]
    tabs = _rope_tables(t)

    saved, xl = [], x0
    forward_later = []

    def pass_on_later(o_attn):
        lands = _wait_copies(gather_later, o_attn, 4 * n_later, gather_plan, "gather_later_wait")[n_later:]
        forward_later.append(_start_copies(lands, FORWARD_COPIES * n_later, forward_plan, "gather_later_forward_start"))
        return forward_later[0][3]

    for l in range(depth):
        o = ops[l]
        if l == 0:
            x1, s1 = x1_first, s1_first
            x2, sm = _mixer_fwd(x1, o, tabs, after_attention=pass_on_later)
            lands = _wait_copies(forward_later[0], x2, FORWARD_COPIES * n_later, forward_plan, "gather_later_forward_wait")
            wgu.update(zip(later, lands[:len(later)]))
            wd.update(zip(later, lands[len(later):]))
        else:
            x1, s1 = _ffn_fwd(xl, o['ffn1_norm'], wgu[l, 1], wd[l, 1], fp)
            x2, sm = _mixer_fwd(x1, o, tabs)
        x3, s2 = _ffn_fwd(x2, o['ffn2_norm'], wgu[l, 2], wd[l, 2], fp)
        xl = _rms_fwd(x3, o['post_norm'], F32, "rms_fwd_post")
        saved.append((s1, sm, s2, x3))
    loss_part, dx = _loss_head(xl, target, "loss_head")
    loss = lax.psum(loss_part[0, 0], ("x", "y", "c"))

    grads, mine_gu, mine_dn, in_flight = [None] * depth, {}, {}, {}

    def exchange(tag):
        def after_dw(d_wgu, d_wd):
            own = [d_wgu[0], d_wd[0]]
            if tag == last_tag:
                from_sibling, started = _reduce_start(own, [d_wgu[1], d_wd[1]], tag)
            else:
                from_sibling, started = None, _reduce_direct_start([d_wgu[1], d_wd[1]], tag)
            in_flight[tag] = (own, from_sibling, started)
            return started[3]
        return after_dw

    def finish(tag, after):
        own, from_sibling, started = in_flight.pop(tag)
        if from_sibling is None:
            return _reduce_direct_finish(own, started, after, tag)
        return _reduce_finish(own, from_sibling, started, after, tag)

    last_tag = f"{first_ffn[0]}{first_ffn[1]}"
    vector_sums = []

    def reduce_vectors(norms0):
        layers = [dict(grads[k]) for k in range(depth)]
        layers[0].update(norms0)
        parts = [jnp.stack([layers[k][n] for k in range(depth)]).reshape(-1) for n in VECTORS + ['conv_w']]
        small = jnp.concatenate(parts)
        s_rows = -(-small.size // (8 * LANE)) * 8
        small = jnp.pad(small, (0, s_rows * LANE - small.size)).reshape(s_rows, LANE)
        small_all = _all_gather([(small, 0)], "gather_small_grads", in_vmem=True)[0]
        vector_sums.append(_sum_parts([small_all[k * s_rows:(k + 1) * s_rows] for k in range(N_DEV)],
                                      "sum_small_grads"))
        return vector_sums[0]

    token = None
    for l in reversed(range(depth)):
        o = ops[l]
        s1, sm, s2, x3 = saved[l]
        post_gain = o['post_norm'] if token is None else _after(o['post_norm'], token)
        dx, dxb, d_post = _rms_bwd(x3, post_gain, dx, None, "rms_bwd_post")
        dx, dxb, d_ffn2, token = _ffn_bwd(dx, dxb, s2, o['ffn2_norm'], wgu[l, 2], wd[l, 2], fp, exchange(f"{l}2"))
        if l + 1 < depth:
            mine_gu[l + 1, 1], mine_dn[l + 1, 1] = finish(f"{l + 1}1", dx)
        dx, dxb, gm = _mixer_bwd(dx, dxb, sm, o, tabs, token)
        mine_gu[l, 2], mine_dn[l, 2] = finish(f"{l}2", dx)
        grads[l] = _mixer_grads_to_params(gm)
        if l == 0:
            rest_own = jnp.concatenate(
                [_blocks_from_full(jnp.stack([grads[k][n] for k in range(depth)]), n).reshape(N_DEV, rows_of[n], d_model)
                 for n in REST], axis=1)
            sibling_rest, started_rest = _reduce_start([rest_own], [rest_own.astype(BF16)], "rest")
        norms = dict(post_norm=d_post, ffn2_norm=d_ffn2)
        dx, dxb, d_ffn1, token = _ffn_bwd(
            dx, dxb, s1, o['ffn1_norm'], wgu[l, 1], wd[l, 1], fp, exchange(f"{l}1"),
            after=started_rest[3] if l == 0 else None,
            before_dw=(lambda dg: reduce_vectors(dict(norms, ffn1_norm=dg))) if l == 0 else None)
        grads[l].update(norms, ffn1_norm=d_ffn1)
    grad_x = dx.reshape(x.shape)

    grad, delta, new_m, new_v = {}, {}, {}, {}

    def adamw(names, after=None):
        for n in names:
            shp = w[n].shape
            two_d = lambda a: a.reshape(-1, shp[-1])
            dl, mn, vn = _adamw(two_d(w[n]), two_d(grad[n]), two_d(mom[n]), two_d(var[n]), "adamw_" + n, after)
            delta[n], new_m[n], new_v[n] = dl.reshape(shp), mn.reshape(shp), vn.reshape(shp)
            after = dl[:8]
        return after

    def ffn_grads(f):
        grad[f'ffn{f}_w_gate'] = jnp.stack([mine_gu[l, f][:, :fb] for l in range(depth)])
        grad[f'ffn{f}_w_up'] = jnp.stack([mine_gu[l, f][:, fp:fp + fb] for l in range(depth)])
        grad[f'ffn{f}_w_down'] = jnp.stack([mine_dn[l, f][:fb] for l in range(depth)])
        return [f'ffn{f}_w_gate', f'ffn{f}_w_up', f'ffn{f}_w_down']

    small_sum = vector_sums[0].reshape(-1)
    start = 0
    for n in VECTORS:
        grad[n] = small_sum[start:start + w[n].size].reshape(w[n].shape)
        start += w[n].size
    cw_grad = small_sum[start:start + depth * CONV_K * CONV_W].reshape(depth, CONV_K, CONV_W)
    nb = conv_w.shape[-1]
    grad['conv_w'] = lax.dynamic_slice_in_dim(cw_grad, my_block * nb, nb, axis=2)
    done = adamw(ffn_grads(2) + ['conv_w'], after=token)
    vcat = lambda src: jnp.concatenate([src[n].reshape(-1) for n in VECTORS]).reshape(-1, LANE)
    dl, mn, vn = _adamw(vcat(w), vcat(grad), vcat(mom), vcat(var), "adamw_vectors", done)
    start = 0
    for n in VECTORS:
        sl = lambda a: a.reshape(-1)[start:start + w[n].size].reshape(w[n].shape)
        delta[n], new_m[n], new_v[n] = sl(dl), sl(mn), sl(vn)
        start += w[n].size
    mine_rest = _reduce_finish([rest_own], sibling_rest, started_rest, dl, "rest")[0]
    start = 0
    for n in REST:
        grad[n] = mine_rest[start:start + rows_of[n]].reshape(w[n].shape)
        start += rows_of[n]
    done = adamw(REST)
    mine_gu[first_ffn], mine_dn[first_ffn] = finish(f"{first_ffn[0]}{first_ffn[1]}", done)
    adamw(ffn_grads(1))

    return (loss, grad_x, *[grad[n] for n in WEIGHTS], *[delta[n] for n in WEIGHTS],
            *[new_m[n] for n in WEIGHTS], *[new_v[n] for n in WEIGHTS])
```

```python
import jax
import jax.numpy as jnp
from jax import lax
from jax.experimental import pallas as pl
from jax.experimental.pallas import tpu as pltpu

F32, BF16 = jnp.float32, jnp.bfloat16

N_DEV = 8
N_HEADS = 8
QK_NOPE, QK_ROPE, V_DIM = 64, 32, 64
QK_DIM = QK_NOPE + QK_ROPE
HEAD_PAD = 128
Q_LORA, KV_LORA = 384, 256
LAT = Q_LORA + KV_LORA
CONV_W, CONV_K = 512, 31
CONV_HALO = 32
CHUNK = 64
ROPE_THETA = 10000.0
EPS = 1e-6
ATTN_SCALE = QK_DIM ** -0.5
ATTN_SCALE_LOG2 = ATTN_SCALE * 1.4426950408889634
P_KPE = LAT
P_A = LAT + HEAD_PAD
P_G = P_A + CONV_W
P_COLS = P_G + CONV_W

ADAM_LR, ADAM_B1, ADAM_B2, ADAM_EPS, ADAM_WD, ADAM_STEP = 0.001, 0.9, 0.999, 1e-08, 0.01, 10

V7X_VMEM_BYTES = 64 << 20
VMEM_LIMIT = V7X_VMEM_BYTES - (8 << 20)
MM_VMEM_BUDGET = 40 << 20
LANE = 128
ROW_TILE = 512
ATTN_BLOCK = 512
ATTN_HEADS = 1
CONV_TILE = 256

MESH_ID = pl.DeviceIdType.MESH
ANY = pl.BlockSpec(memory_space=pl.ANY)
VMEM_SPEC = pl.BlockSpec(memory_space=pltpu.VMEM)
HBM_SPEC = pl.BlockSpec(memory_space=pltpu.HBM)
SEM_SPEC = pl.BlockSpec(memory_space=pltpu.SEMAPHORE)
DATAFLOW = pltpu.SideEffectType.DATAFLOW_SIDE_EFFECTING

WEIGHTS = ['ffn1_norm', 'ffn1_w_gate', 'ffn1_w_up', 'ffn1_w_down', 'mix_norm', 'w_in', 'q_latent_norm', 'w_uq',
           'kv_latent_norm', 'w_ukv', 'q_norm', 'k_norm', 'conv_w', 'conv_b', 'conv_ln_g', 'conv_ln_b', 'w_out',
           'ffn2_norm', 'ffn2_w_gate', 'ffn2_w_up', 'ffn2_w_down', 'post_norm']
COL_SHARDED =['ffn1_w_gate', 'ffn1_w_up', 'w_in', 'w_uq', 'w_ukv', 'ffn2_w_gate', 'ffn2_w_up']
REST = ['w_in', 'w_uq', 'w_ukv', 'w_out']
VECTORS = ['ffn1_norm', 'mix_norm', 'q_latent_norm', 'kv_latent_norm', 'q_norm', 'k_norm', 'conv_b', 'conv_ln_g',
           'conv_ln_b', 'ffn2_norm', 'post_norm']


def _params(*sem):
    return pltpu.CompilerParams(dimension_semantics=sem if sem else None, vmem_limit_bytes=VMEM_LIMIT)


def _tile(n, cap):
    if n <= cap:
        return n
    best = 0
    for d in range(LANE, cap + 1, LANE):
        if n % d == 0:
            best = d
    assert best, (n, cap)
    return best


def _row_tile(n, cap=ROW_TILE, mult=8):
    if n <= cap:
        return n
    best = 0
    for d in range(mult, cap + 1, mult):
        if n % d == 0:
            best = d
    assert best, (n, cap)
    return best


def _mm(a, b, *, name, ta=False, tb=False, res=None, scale=1.0, out_dtype=F32, tm=None, tn=None, blocks=None,
        after=None):
    (kdim, m) = a.shape if ta else a.shape[::-1]
    (n, kb) = b.shape if tb else b.shape[::-1]
    assert kdim == kb, (a.shape, b.shape, ta, tb)
    tm, tn = tm or _tile(m, 512), tn or _tile(n, 1024)
    if blocks is not None:
        tm, tn = (tm, n // blocks[1]) if blocks[0] == 'col' else (m // blocks[1], tn)
    size = lambda arr: jnp.dtype(arr.dtype).itemsize
    out_bytes = tm * tn * ((6 if blocks is not None else jnp.dtype(out_dtype).itemsize) + (4 if res is not None else 0))

    def vmem_need(tk):
        return 2 * (tm * tk * size(a) + tk * tn * size(b) + out_bytes) + (tm * tn * 4 if tk < kdim else 0)

    tk = kdim
    for cand in [d for d in range(kdim - LANE, 0, -LANE) if kdim % d == 0]:
        if vmem_need(tk) <= MM_VMEM_BUDGET:
            break
        tk = cand
    nk = kdim // tk
    n_in = 2 + (res is not None) + (after is not None)
    n_out = 2 if blocks is not None else 1
    dims = (((0 if ta else 1,), (1 if tb else 0,)), ((), ()))

    def body(*refs):
        a_ref, b_ref = refs[0], refs[1]
        r_ref = refs[2] if res is not None else None
        o_refs = refs[n_in:n_in + n_out]
        acc_ref = refs[-1] if nk > 1 else None
        part = lax.dot_general(a_ref[...].astype(BF16), b_ref[...].astype(BF16), dims, preferred_element_type=F32)

        def finish(acc):
            if scale != 1.0:
                acc = acc * scale
            if r_ref is not None:
                acc = r_ref[...] + acc
            for o_ref in o_refs:
                o_ref[...] = acc.astype(o_ref.dtype)

        if nk == 1:
            finish(part)
        else:
            k = pl.program_id(2)

            @pl.when(k == 0)
            def _():
                acc_ref[...] = part

            @pl.when(k > 0)
            def _():
                acc_ref[...] += part

            @pl.when(k == nk - 1)
            def _():
                finish(acc_ref[...])

    a_spec = pl.BlockSpec((tk, tm), lambda i, j, k: (k, i)) if ta else pl.BlockSpec((tm, tk), lambda i, j, k: (i, k))
    b_spec = pl.BlockSpec((tn, tk), lambda i, j, k: (j, k)) if tb else pl.BlockSpec((tk, tn), lambda i, j, k: (k, j))
    plain = pl.BlockSpec((tm, tn), lambda i, j, k: (i, j))
    if blocks is None:
        out_specs, out_shape = plain, jax.ShapeDtypeStruct((m, n), out_dtype)
    else:
        if blocks[0] == 'col':
            o_spec, shp = pl.BlockSpec((None, tm, tn), lambda i, j, k: (j, i, 0)), (blocks[1], m, tn)
        else:
            o_spec, shp = pl.BlockSpec((None, tm, tn), lambda i, j, k: (i, 0, j)), (blocks[1], tm, n)
        out_specs, out_shape = (o_spec, o_spec), (jax.ShapeDtypeStruct(shp, F32), jax.ShapeDtypeStruct(shp, BF16))
    in_specs = [a_spec, b_spec] + ([plain] if res is not None else [])
    args = (a, b) + ((res,) if res is not None else ())
    if after is not None:
        in_specs.append(pl.BlockSpec(after.shape, lambda i, j, k: (0, 0)))
        args += (after,)
    return pl.pallas_call(
        body, name=name, grid=(m // tm, n // tn, nk), in_specs=in_specs, out_specs=out_specs, out_shape=out_shape,
        scratch_shapes=[pltpu.VMEM((tm, tn), F32)] if nk > 1 else [],
        compiler_params=_params("parallel", "parallel", "arbitrary"),
    )(*args)


def _rms_fwd(x, g, out_dtype, name):
    t, d = x.shape
    tm = _row_tile(t)

    def body(x_ref, g_ref, o_ref):
        xv = x_ref[...]
        r = lax.rsqrt(jnp.mean(xv * xv, axis=-1, keepdims=True) + EPS)
        o_ref[...] = (xv * r * g_ref[...]).astype(o_ref.dtype)

    return pl.pallas_call(
        body, name=name, grid=(t // tm,),
        in_specs=[pl.BlockSpec((tm, d), lambda i: (i, 0)), pl.BlockSpec((1, d), lambda i: (0, 0))],
        out_specs=pl.BlockSpec((tm, d), lambda i: (i, 0)),
        out_shape=jax.ShapeDtypeStruct((t, d), out_dtype), compiler_params=_params("parallel"),
    )(x, g.reshape(1, d))


def _rms_bwd(x, g, dh, res, name):
    t, d = x.shape
    tm = _row_tile(t)

    def body(*refs):
        x_ref, g_ref, dh_ref = refs[:3]
        r_ref = refs[3] if res is not None else None
        dx_ref, dxb_ref, dg_ref = refs[-3:]
        xv, dhv = x_ref[...], dh_ref[...]
        r = lax.rsqrt(jnp.mean(xv * xv, axis=-1, keepdims=True) + EPS)
        y = xv * r
        dy = dhv * g_ref[...]
        dx = r * (dy - y * jnp.mean(dy * y, axis=-1, keepdims=True))
        if r_ref is not None:
            dx = r_ref[...] + dx
        dx_ref[...] = dx
        dxb_ref[...] = dx.astype(BF16)

        @pl.when(pl.program_id(0) == 0)
        def _():
            dg_ref[...] = jnp.zeros_like(dg_ref)

        dg_ref[...] += jnp.sum(dhv * y, axis=0, keepdims=True)

    row = pl.BlockSpec((tm, d), lambda i: (i, 0))
    vec = pl.BlockSpec((1, d), lambda i: (0, 0))
    args = (x, g.reshape(1, d), dh) + ((res,) if res is not None else ())
    dx, dxb, dg = pl.pallas_call(
        body, name=name, grid=(t // tm,), in_specs=[row, vec, row] + ([row] if res is not None else []),
        out_specs=(row, row, vec),
        out_shape=(jax.ShapeDtypeStruct((t, d), F32), jax.ShapeDtypeStruct((t, d), BF16),
                   jax.ShapeDtypeStruct((1, d), F32)),
        compiler_params=_params("arbitrary"),
    )(*args)
    return dx, dxb, dg.reshape(d)


FFN_PAIR = 4


def _ffn_up(h, wgu, fp, name):
    t, d = h.shape
    tm, tn = _tile(t, 512), FFN_PAIR * 2 * fp
    nj = wgu.shape[1] // tn

    def body(h_ref, w_ref, fac_ref, z_ref):
        ab = jnp.dot(h_ref[...], w_ref[...], preferred_element_type=F32)
        for e in range(FFN_PAIR):
            av, bv = ab[:, 2 * fp * e:2 * fp * e + fp], ab[:, 2 * fp * e + fp:2 * fp * (e + 1)]
            s = jax.nn.sigmoid(av)
            silu = av * s
            z_ref[:, fp * e:fp * (e + 1)] = (silu * bv).astype(z_ref.dtype)
            fac_ref[:, 2 * fp * e:2 * fp * e + fp] = silu.astype(fac_ref.dtype)
            fac_ref[:, 2 * fp * e + fp:2 * fp * (e + 1)] = (bv * (s + silu * (1.0 - s))).astype(fac_ref.dtype)

    return pl.pallas_call(
        body, name=name, grid=(nj, t // tm),
        in_specs=[pl.BlockSpec((tm, d), lambda j, i: (i, 0)), pl.BlockSpec((d, tn), lambda j, i: (0, j))],
        out_specs=(pl.BlockSpec((tm, tn), lambda j, i: (i, j)), pl.BlockSpec((tm, tn // 2), lambda j, i: (i, j))),
        out_shape=(jax.ShapeDtypeStruct((t, wgu.shape[1]), BF16), jax.ShapeDtypeStruct((t, wgu.shape[1] // 2), BF16)),
        compiler_params=_params("parallel", "parallel"),
    )(h, wgu)


def _resident(shape):
    return pl.BlockSpec(shape, lambda i: (0,) * len(shape), pipeline_mode=pl.Buffered(1))


def _ffn_bwd_rows(dyb, wd, fac, wgu, x, g, dy, fp, name, after=None):
    t, d = dyb.shape
    width = fac.shape[1]
    tm = _tile(t, 256)

    def body(dyb_ref, wd_ref, fac_ref, wgu_ref, x_ref, g_ref, dy_ref, *refs):
        dab_ref, dx_ref, dxb_ref, dg_ref = refs[-4:]
        dz = _dot_nt(dyb_ref[...], wd_ref[...]) * 0.5
        for e in range(width // (2 * fp)):
            dze = dz[:, fp * e:fp * (e + 1)]
            d_up = fac_ref[:, 2 * fp * e:2 * fp * e + fp].astype(F32)
            d_gate = fac_ref[:, 2 * fp * e + fp:2 * fp * (e + 1)].astype(F32)
            dab_ref[:, 2 * fp * e:2 * fp * e + fp] = (dze * d_gate).astype(dab_ref.dtype)
            dab_ref[:, 2 * fp * e + fp:2 * fp * (e + 1)] = (dze * d_up).astype(dab_ref.dtype)
        dh = _dot_nt(dab_ref[...], wgu_ref[...])
        xv = x_ref[...]
        r = lax.rsqrt(jnp.mean(xv * xv, axis=-1, keepdims=True) + EPS)
        y = xv * r
        dyn = dh * g_ref[...]
        dx = dy_ref[...] + r * (dyn - y * jnp.mean(dyn * y, axis=-1, keepdims=True))
        dx_ref[...] = dx
        dxb_ref[...] = dx.astype(BF16)

        @pl.when(pl.program_id(0) == 0)
        def _():
            dg_ref[...] = jnp.zeros_like(dg_ref)

        dg_ref[...] += jnp.sum(dh * y, axis=0, keepdims=True)

    row = pl.BlockSpec((tm, d), lambda i: (i, 0))
    wide = pl.BlockSpec((tm, width), lambda i: (i, 0))
    vec = pl.BlockSpec((1, d), lambda i: (0, 0))
    in_specs = [row, _resident(wd.shape), wide, _resident(wgu.shape), row, vec, row]
    args = [dyb, wd, fac, wgu, x, g.reshape(1, d), dy]
    if after is not None:
        in_specs.append(pl.BlockSpec(after.shape, lambda i: (0, 0)))
        args.append(after)
    dab, dx, dxb, dg = pl.pallas_call(
        body, name=name, grid=(t // tm,), in_specs=in_specs, out_specs=(wide, row, row, vec),
        out_shape=(jax.ShapeDtypeStruct((t, width), BF16), jax.ShapeDtypeStruct((t, d), F32),
                   jax.ShapeDtypeStruct((t, d), BF16), jax.ShapeDtypeStruct((1, d), F32)),
        compiler_params=_params("arbitrary"),
    )(*args)
    return dab, dx, dxb, dg.reshape(d)


def _matmul_rms_bwd(dab, wgu, x, g, dy, name, after=None):
    t, kdim = dab.shape
    d = wgu.shape[0]
    tm = _tile(t, 256)

    def body(a_ref, b_ref, x_ref, g_ref, dy_ref, *refs):
        dx_ref, dxb_ref, dg_ref = refs[-3:]
        dh = _dot_nt(a_ref[...].astype(BF16), b_ref[...])
        xv = x_ref[...]
        r = lax.rsqrt(jnp.mean(xv * xv, axis=-1, keepdims=True) + EPS)
        y = xv * r
        dyn = dh * g_ref[...]
        dx = dy_ref[...] + r * (dyn - y * jnp.mean(dyn * y, axis=-1, keepdims=True))
        dx_ref[...] = dx
        dxb_ref[...] = dx.astype(BF16)

        @pl.when(pl.program_id(0) == 0)
        def _():
            dg_ref[...] = jnp.zeros_like(dg_ref)

        dg_ref[...] += jnp.sum(dh * y, axis=0, keepdims=True)

    row = pl.BlockSpec((tm, d), lambda i: (i, 0))
    vec = pl.BlockSpec((1, d), lambda i: (0, 0))
    in_specs = [pl.BlockSpec((tm, kdim), lambda i: (i, 0)), pl.BlockSpec((d, kdim), lambda i: (0, 0)), row, vec, row]
    args = [dab, wgu, x, g.reshape(1, d), dy]
    if after is not None:
        in_specs.append(pl.BlockSpec(after.shape, lambda i: (0, 0)))
        args.append(after)
    dx, dxb, dg = pl.pallas_call(
        body, name=name, grid=(t // tm,), in_specs=in_specs, out_specs=(row, row, vec),
        out_shape=(jax.ShapeDtypeStruct((t, d), F32), jax.ShapeDtypeStruct((t, d), BF16),
                   jax.ShapeDtypeStruct((1, d), F32)),
        compiler_params=_params("arbitrary"),
    )(*args)
    return dx, dxb, dg.reshape(d)


def _lat_norm_fwd(p, g_q, g_kv, name):
    t = p.shape[0]
    tm = _row_tile(t)

    def body(p_ref, gq_ref, gkv_ref, q_ref, kv_ref):
        for lo, hi, g_ref, o_ref in ((0, Q_LORA, gq_ref, q_ref), (Q_LORA, LAT, gkv_ref, kv_ref)):
            xv = p_ref[:, lo:hi]
            r = lax.rsqrt(jnp.mean(xv * xv, axis=-1, keepdims=True) + EPS)
            o_ref[...] = (xv * r * g_ref[...]).astype(o_ref.dtype)

    return pl.pallas_call(
        body, name=name, grid=(t // tm,),
        in_specs=[pl.BlockSpec((tm, P_COLS), lambda i: (i, 0)), pl.BlockSpec((1, Q_LORA), lambda i: (0, 0)),
                  pl.BlockSpec((1, KV_LORA), lambda i: (0, 0))],
        out_specs=(pl.BlockSpec((tm, Q_LORA), lambda i: (i, 0)), pl.BlockSpec((tm, KV_LORA), lambda i: (i, 0))),
        out_shape=(jax.ShapeDtypeStruct((t, Q_LORA), BF16), jax.ShapeDtypeStruct((t, KV_LORA), BF16)),
        compiler_params=_params("parallel"),
    )(p, g_q.reshape(1, Q_LORA), g_kv.reshape(1, KV_LORA))


def _lat_norm_bwd(p, g_q, g_kv, dq, dkv, name):
    t = p.shape[0]
    tm = _row_tile(t)

    def body(p_ref, gq_ref, gkv_ref, dq_ref, dkv_ref, dp_ref, dgq_ref, dgkv_ref):
        first = pl.program_id(0) == 0
        for lo, hi, g_ref, d_ref, dg_ref in ((0, Q_LORA, gq_ref, dq_ref, dgq_ref),
                                             (Q_LORA, LAT, gkv_ref, dkv_ref, dgkv_ref)):
            xv, dhv = p_ref[:, lo:hi], d_ref[...]
            r = lax.rsqrt(jnp.mean(xv * xv, axis=-1, keepdims=True) + EPS)
            y = xv * r
            dy = dhv * g_ref[...]
            dp_ref[:, lo:hi] = r * (dy - y * jnp.mean(dy * y, axis=-1, keepdims=True))

            @pl.when(first)
            def _():
                dg_ref[...] = jnp.zeros_like(dg_ref)

            dg_ref[...] += jnp.sum(dhv * y, axis=0, keepdims=True)

    vq = pl.BlockSpec((1, Q_LORA), lambda i: (0, 0))
    vkv = pl.BlockSpec((1, KV_LORA), lambda i: (0, 0))
    dp, dgq, dgkv = pl.pallas_call(
        body, name=name, grid=(t // tm,),
        in_specs=[pl.BlockSpec((tm, P_COLS), lambda i: (i, 0)), vq, vkv,
                  pl.BlockSpec((tm, Q_LORA), lambda i: (i, 0)), pl.BlockSpec((tm, KV_LORA), lambda i: (i, 0))],
        out_specs=(pl.BlockSpec((tm, LAT), lambda i: (i, 0)), vq, vkv),
        out_shape=(jax.ShapeDtypeStruct((t, LAT), F32), jax.ShapeDtypeStruct((1, Q_LORA), F32),
                   jax.ShapeDtypeStruct((1, KV_LORA), F32)),
        compiler_params=_params("arbitrary"),
    )(p, g_q.reshape(1, Q_LORA), g_kv.reshape(1, KV_LORA), dq, dkv)
    return dp, dgq.reshape(Q_LORA), dgkv.reshape(KV_LORA)


def _rope_tables(t):
    half = QK_ROPE // 2
    pos = jnp.arange(t, dtype=F32)
    inv_freq = 1.0 / (ROPE_THETA ** (jnp.arange(0, QK_ROPE, 2, dtype=F32) / QK_ROPE))
    ang = pos[:, None] * inv_freq[None, :]
    cos, sin = jnp.cos(ang), jnp.sin(ang)
    z = lambda n: jnp.zeros((t, n), F32)
    c_tab = jnp.concatenate([jnp.ones((t, QK_NOPE), F32), cos, cos, z(HEAD_PAD - QK_DIM)], axis=1)
    sa_tab = jnp.concatenate([z(QK_NOPE), -sin, z(half), z(HEAD_PAD - QK_DIM)], axis=1)
    sb_tab = jnp.concatenate([z(QK_NOPE), z(half), sin, z(HEAD_PAD - QK_DIM)], axis=1)
    return c_tab, sa_tab, sb_tab


def _rope(x, c, sa, sb):
    half = QK_ROPE // 2
    return x * c + pltpu.roll(x, HEAD_PAD - half, 1) * sa + pltpu.roll(x, half, 1) * sb


def _rope_t(d, c, sa, sb):
    half = QK_ROPE // 2
    return d * c + pltpu.roll(d * sa, half, 1) + pltpu.roll(d * sb, HEAD_PAD - half, 1)


def _head_rms(x):
    r = lax.rsqrt(jnp.sum(x * x, axis=-1, keepdims=True) * (1.0 / QK_DIM) + EPS)
    return x * r, r


def _qk_prep_fwd(q_raw, k_raw, p, gq, gk, tabs, name):
    t, width = q_raw.shape
    tm = _row_tile(t)

    def body(q_ref, k_ref, p_ref, gq_ref, gk_ref, c_ref, sa_ref, sb_ref, qo_ref, ko_ref):
        c, sa, sb, kpe = c_ref[...], sa_ref[...], sb_ref[...], p_ref[...]
        for h in range(N_HEADS):
            cols = slice(h * HEAD_PAD, (h + 1) * HEAD_PAD)
            qn, _ = _head_rms(q_ref[:, cols])
            qo_ref[:, cols] = _rope(qn * gq_ref[...], c, sa, sb).astype(qo_ref.dtype)
            kn, _ = _head_rms(k_ref[:, cols] + kpe)
            ko_ref[:, cols] = _rope(kn * gk_ref[...], c, sa, sb).astype(ko_ref.dtype)

    rows = pl.BlockSpec((tm, width), lambda i: (i, 0))
    tab = pl.BlockSpec((tm, HEAD_PAD), lambda i: (i, 0))
    vec = pl.BlockSpec((1, HEAD_PAD), lambda i: (0, 0))
    kpe_spec = pl.BlockSpec((tm, HEAD_PAD), lambda i: (i, P_KPE // HEAD_PAD))
    return pl.pallas_call(
        body, name=name, grid=(t // tm,), in_specs=[rows, rows, kpe_spec, vec, vec, tab, tab, tab],
        out_specs=(rows, rows),
        out_shape=(jax.ShapeDtypeStruct(q_raw.shape, BF16), jax.ShapeDtypeStruct(k_raw.shape, BF16)),
        compiler_params=_params("parallel"),
    )(q_raw, k_raw, p, gq.reshape(1, HEAD_PAD), gk.reshape(1, HEAD_PAD), *tabs)


def _qk_prep_bwd(q_raw, k_raw, p, dq, dk, gq, gk, tabs, name):
    t, width = q_raw.shape
    tm = _row_tile(t, 256)

    def body(q_ref, k_ref, p_ref, dq_ref, dk_ref, gq_ref, gk_ref, c_ref, sa_ref, sb_ref,
             dqr_ref, dkr_ref, dkpe_ref, dgq_ref, dgk_ref):
        c, sa, sb, kpe = c_ref[...], sa_ref[...], sb_ref[...], p_ref[...]

        def one(x, d, g_ref):
            n, r = _head_rms(x)
            dng = _rope_t(d, c, sa, sb)
            dn = dng * g_ref[...]
            dx = r * (dn - n * (jnp.sum(dn * n, axis=-1, keepdims=True) * (1.0 / QK_DIM)))
            return dx, jnp.sum(dng * n, axis=0, keepdims=True)

        dgq = dgk = dkpe = None
        for h in range(N_HEADS):
            cols = slice(h * HEAD_PAD, (h + 1) * HEAD_PAD)
            dqr, gq_part = one(q_ref[:, cols], dq_ref[:, cols], gq_ref)
            dkr, gk_part = one(k_ref[:, cols] + kpe, dk_ref[:, cols], gk_ref)
            dqr_ref[:, cols] = dqr
            dkr_ref[:, cols] = dkr
            dgq = gq_part if dgq is None else dgq + gq_part
            dgk = gk_part if dgk is None else dgk + gk_part
            dkpe = dkr if dkpe is None else dkpe + dkr
        dkpe_ref[...] = dkpe

        @pl.when(pl.program_id(0) == 0)
        def _():
            dgq_ref[...] = jnp.zeros_like(dgq_ref)
            dgk_ref[...] = jnp.zeros_like(dgk_ref)

        dgq_ref[...] += dgq
        dgk_ref[...] += dgk

    head = pl.BlockSpec((tm, width), lambda i: (i, 0))
    tab = pl.BlockSpec((tm, HEAD_PAD), lambda i: (i, 0))
    vec = pl.BlockSpec((1, HEAD_PAD), lambda i: (0, 0))
    kpe = pl.BlockSpec((tm, HEAD_PAD), lambda i: (i, P_KPE // HEAD_PAD))
    dqr, dkr, dkpe, dgq, dgk = pl.pallas_call(
        body, name=name, grid=(t // tm,), in_specs=[head, head, kpe, head, head, vec, vec, tab, tab, tab],
        out_specs=(head, head, tab, vec, vec),
        out_shape=(jax.ShapeDtypeStruct(q_raw.shape, F32), jax.ShapeDtypeStruct(k_raw.shape, F32),
                   jax.ShapeDtypeStruct((t, HEAD_PAD), F32), jax.ShapeDtypeStruct((1, HEAD_PAD), F32),
                   jax.ShapeDtypeStruct((1, HEAD_PAD), F32)),
        compiler_params=_params("arbitrary"),
    )(q_raw, k_raw, p, dq, dk, gq.reshape(1, HEAD_PAD), gk.reshape(1, HEAD_PAD), *tabs)
    return dqr, dkr, dkpe, dgq.reshape(HEAD_PAD), dgk.reshape(HEAD_PAD)


def _dot_nt(a, b):
    return lax.dot_general(a, b, (((1,), (1,)), ((), ())), preferred_element_type=F32)


def _dot_tn(a, b):
    return lax.dot_general(a, b, (((0,), (0,)), ((), ())), preferred_element_type=F32)


def _diag_mask():
    rows = lax.broadcasted_iota(jnp.int32, (ATTN_BLOCK, ATTN_BLOCK), 0) // CHUNK
    cols = lax.broadcasted_iota(jnp.int32, (ATTN_BLOCK, ATTN_BLOCK), 1) // CHUNK
    return cols <= rows


def _attn_fwd(q, k, v, name):
    t = q.shape[0]
    bq = ATTN_BLOCK
    nq = t // bq

    width = ATTN_HEADS * HEAD_PAD

    def body(q_ref, k_ref, v_ref, o_ref, lse_ref):
        i = pl.program_id(1)
        heads = [slice(e * HEAD_PAD, (e + 1) * HEAD_PAD) for e in range(ATTN_HEADS)]
        qv = [q_ref[:, cols] for cols in heads]

        def block(j, carries, masked):
            rows = pl.ds(pl.multiple_of(j * bq, bq), bq)
            out = []
            for e, (m, l, acc) in enumerate(carries):
                s = _dot_nt(qv[e], k_ref[rows, heads[e]]) * ATTN_SCALE_LOG2
                if masked:
                    s = jnp.where(_diag_mask(), s, -1e30)
                m_new = jnp.maximum(m, jnp.max(s, axis=-1, keepdims=True))
                alpha = jnp.exp2(m - m_new)
                pe = jnp.exp2(s - m_new)
                l = alpha * l + jnp.sum(pe, axis=-1, keepdims=True)
                acc = alpha * acc + jnp.dot(pe.astype(BF16), v_ref[rows, heads[e]], preferred_element_type=F32)
                out.append((m_new, l, acc))
            return tuple(out)

        init = tuple((jnp.full((bq, 1), -1e30, F32), jnp.zeros((bq, 1), F32), jnp.zeros((bq, HEAD_PAD), F32))
                     for _ in heads)
        carries = lax.fori_loop(0, i, lambda j, cr: block(j, cr, False), init)
        for cols, (m, l, acc) in zip(heads, block(i, carries, True)):
            o_ref[:, cols] = acc / l
            lse_ref[:, cols] = jnp.broadcast_to(m + jnp.log2(l), (bq, HEAD_PAD))

    blk = pl.BlockSpec((bq, width), lambda h, i: (i, h))
    full = pl.BlockSpec((t, width), lambda h, i: (0, h))
    return pl.pallas_call(
        body, name=name, grid=(N_HEADS // ATTN_HEADS, nq), in_specs=[blk, full, full], out_specs=(blk, blk),
        out_shape=(jax.ShapeDtypeStruct(q.shape, F32), jax.ShapeDtypeStruct(q.shape, F32)),
        compiler_params=_params("parallel", "parallel"),
    )(q, k, v)


def _attn_bwd(q, k, v, o, lse, do, name):
    t = q.shape[0]
    bq = ATTN_BLOCK
    nq = t // bq

    def body(q_ref, k_ref, v_ref, o_ref, lse_ref, do_ref, dq_ref, dk_ref, dv_ref, delta_ref):
        def rows_of(i):
            return pl.ds(pl.multiple_of(i * bq, bq), bq)

        def prep(i, _):
            r = rows_of(i)
            delta_ref[r, :] = jnp.broadcast_to(jnp.sum(do_ref[r, :] * o_ref[r, :], axis=-1, keepdims=True),
                                               (bq, HEAD_PAD))
            dq_ref[r, :] = jnp.zeros((bq, HEAD_PAD), F32)
            return 0

        lax.fori_loop(0, nq, prep, 0)

        def key_block(j, _):
            rj = rows_of(j)
            kb, vb = k_ref[rj, :], v_ref[rj, :]

            def query_block(i, carry, masked):
                dk, dv = carry
                ri = rows_of(i)
                qb, dob = q_ref[ri, :], do_ref[ri, :].astype(BF16)
                s = _dot_nt(qb, kb) * ATTN_SCALE_LOG2
                if masked:
                    s = jnp.where(_diag_mask(), s, -1e30)
                pe = jnp.exp2(s - lse_ref[ri, :][:, :1])
                dp = _dot_nt(dob, vb)
                ds = (pe * (dp - delta_ref[ri, :][:, :1]) * ATTN_SCALE).astype(BF16)
                dq_ref[ri, :] += jnp.dot(ds, kb, preferred_element_type=F32)
                return dk + _dot_tn(ds, qb), dv + _dot_tn(pe.astype(BF16), dob)

            zero = jnp.zeros((bq, HEAD_PAD), F32)
            carry = query_block(j, (zero, zero), True)
            dk, dv = lax.fori_loop(j + 1, nq, lambda i, cr: query_block(i, cr, False), carry)
            dk_ref[rj, :] = dk
            dv_ref[rj, :] = dv
            return 0

        lax.fori_loop(0, nq, key_block, 0)

    full = pl.BlockSpec((t, HEAD_PAD), lambda h: (0, h))
    shp = jax.ShapeDtypeStruct(q.shape, F32)
    return pl.pallas_call(
        body, name=name, grid=(N_HEADS,), in_specs=[full] * 6, out_specs=(full, full, full),
        out_shape=(shp, shp, shp), scratch_shapes=[pltpu.VMEM((t, HEAD_PAD), F32)],
        compiler_params=_params("parallel"),
    )(q, k, v, o, lse, do)


def _glu_ext(pc_ref, pp_ref, u_ref, tm, first):
    u_ref[CONV_HALO:CONV_HALO + tm, :] = pc_ref[:, P_A:P_G] * jax.nn.sigmoid(pc_ref[:, P_G:P_COLS])
    up = pp_ref[tm - CONV_HALO:tm, P_A:P_G] * jax.nn.sigmoid(pp_ref[tm - CONV_HALO:tm, P_G:P_COLS])
    u_ref[0:CONV_HALO, :] = jnp.where(first, 0.0, up)


SUBLANES = 8


def _shift_copies(src_ref, sh_ref):
    rows = sh_ref.shape[1]
    for b in range(1, SUBLANES):
        sh_ref[b - 1, :, :] = src_ref[b:b + rows, :]


def _rows_at(src_ref, sh_ref, start, n):
    a, b = divmod(start, SUBLANES)
    if b == 0:
        return src_ref[SUBLANES * a:SUBLANES * a + n, :]
    return sh_ref[b - 1, SUBLANES * a:SUBLANES * a + n, :]


def _conv_fwd(p, w, b, ln_g, ln_b, name):
    t = p.shape[0]
    tm = _row_tile(t, CONV_TILE)
    off = CONV_HALO - (CONV_K - 1)

    def body(pc_ref, pp_ref, w_ref, b_ref, g_ref, bb_ref, y_ref, o_ref, u_ref, ush_ref):
        _glu_ext(pc_ref, pp_ref, u_ref, tm, pl.program_id(0) == 0)
        _shift_copies(u_ref, ush_ref)
        acc = jnp.zeros((tm, CONV_W), F32)
        for kk in range(CONV_K):
            acc = acc + w_ref[kk:kk + 1, :] * _rows_at(u_ref, ush_ref, off + kk, tm)
        y = acc + b_ref[...]
        y_ref[...] = y
        xc = y - jnp.mean(y, axis=-1, keepdims=True)
        lo = xc * lax.rsqrt(jnp.mean(xc * xc, axis=-1, keepdims=True) + EPS) * g_ref[...] + bb_ref[...]
        o_ref[...] = (lo * jax.nn.sigmoid(lo)).astype(o_ref.dtype)

    prow = pl.BlockSpec((tm, P_COLS), lambda i: (i, 0))
    pprev = pl.BlockSpec((tm, P_COLS), lambda i: (jnp.maximum(i - 1, 0), 0))
    vec = pl.BlockSpec((1, CONV_W), lambda i: (0, 0))
    row = pl.BlockSpec((tm, CONV_W), lambda i: (i, 0))
    return pl.pallas_call(
        body, name=name, grid=(t // tm,),
        in_specs=[prow, pprev, pl.BlockSpec((CONV_HALO, CONV_W), lambda i: (0, 0)), vec, vec, vec],
        out_specs=(row, row),
        out_shape=(jax.ShapeDtypeStruct((t, CONV_W), F32), jax.ShapeDtypeStruct((t, CONV_W), BF16)),
        scratch_shapes=[pltpu.VMEM((tm + CONV_HALO, CONV_W), F32),
                        pltpu.VMEM((SUBLANES - 1, tm + CONV_HALO - SUBLANES, CONV_W), F32)],
        compiler_params=_params("parallel"),
    )(p, p, w, b.reshape(1, CONV_W), ln_g.reshape(1, CONV_W), ln_b.reshape(1, CONV_W))


def _conv_bwd_ln(y, dout, ln_g, ln_b, name):
    t = y.shape[0]
    tm = _row_tile(t)

    def body(y_ref, d_ref, g_ref, bb_ref, dy_ref, dg_ref, db_ref, dcb_ref):
        yv = y_ref[...]
        xc = yv - jnp.mean(yv, axis=-1, keepdims=True)
        r = lax.rsqrt(jnp.mean(xc * xc, axis=-1, keepdims=True) + EPS)
        n = xc * r
        lo = n * g_ref[...] + bb_ref[...]
        s = jax.nn.sigmoid(lo)
        dlo = d_ref[...] * (s * (1.0 + lo * (1.0 - s)))
        dn = dlo * g_ref[...]
        dy = r * (dn - jnp.mean(dn, axis=-1, keepdims=True) - n * jnp.mean(dn * n, axis=-1, keepdims=True))
        dy_ref[...] = dy

        @pl.when(pl.program_id(0) == 0)
        def _():
            dg_ref[...] = jnp.zeros_like(dg_ref)
            db_ref[...] = jnp.zeros_like(db_ref)
            dcb_ref[...] = jnp.zeros_like(dcb_ref)

        dg_ref[...] += jnp.sum(dlo * n, axis=0, keepdims=True)
        db_ref[...] += jnp.sum(dlo, axis=0, keepdims=True)
        dcb_ref[...] += jnp.sum(dy, axis=0, keepdims=True)

    row = pl.BlockSpec((tm, CONV_W), lambda i: (i, 0))
    vec = pl.BlockSpec((1, CONV_W), lambda i: (0, 0))
    vshape = jax.ShapeDtypeStruct((1, CONV_W), F32)
    dy, dg, db, dcb = pl.pallas_call(
        body, name=name, grid=(t // tm,), in_specs=[row, row, vec, vec], out_specs=(row, vec, vec, vec),
        out_shape=(jax.ShapeDtypeStruct((t, CONV_W), F32), vshape, vshape, vshape),
        compiler_params=_params("arbitrary"),
    )(y, dout, ln_g.reshape(1, CONV_W), ln_b.reshape(1, CONV_W))
    return dy, dg.reshape(CONV_W), db.reshape(CONV_W), dcb.reshape(CONV_W)


def _conv_bwd_taps(p, dy, w, name):
    t = p.shape[0]
    tm = _row_tile(t, CONV_TILE)
    nt = t // tm
    off = CONV_HALO - (CONV_K - 1)

    def body(pc_ref, pp_ref, dyc_ref, dyn_ref, w_ref, dag_ref, dw_ref, u_ref, dye_ref, ush_ref, dysh_ref):
        i = pl.program_id(0)
        _glu_ext(pc_ref, pp_ref, u_ref, tm, i == 0)
        dyc = dyc_ref[...]
        dye_ref[0:tm, :] = dyc
        dye_ref[tm:tm + CONV_HALO, :] = jnp.where(i == nt - 1, 0.0, dyn_ref[0:CONV_HALO, :])

        _shift_copies(u_ref, ush_ref)
        _shift_copies(dye_ref, dysh_ref)

        @pl.when(i == 0)
        def _():
            dw_ref[...] = jnp.zeros_like(dw_ref)

        du = jnp.zeros((tm, CONV_W), F32)
        for kk in range(CONV_K):
            dw_ref[kk:kk + 1, :] += jnp.sum(dyc * _rows_at(u_ref, ush_ref, off + kk, tm), axis=0, keepdims=True)
            du = du + w_ref[kk:kk + 1, :] * _rows_at(dye_ref, dysh_ref, CONV_K - 1 - kk, tm)
        av, gv = pc_ref[:, P_A:P_G], pc_ref[:, P_G:P_COLS]
        s = jax.nn.sigmoid(gv)
        dag_ref[:, 0:CONV_W] = du * s
        dag_ref[:, CONV_W:2 * CONV_W] = du * av * (s * (1.0 - s))

    prow = pl.BlockSpec((tm, P_COLS), lambda i: (i, 0))
    pprev = pl.BlockSpec((tm, P_COLS), lambda i: (jnp.maximum(i - 1, 0), 0))
    row = pl.BlockSpec((tm, CONV_W), lambda i: (i, 0))
    nxt = pl.BlockSpec((tm, CONV_W), lambda i: (jnp.minimum(i + 1, nt - 1), 0))
    wspec = pl.BlockSpec((CONV_HALO, CONV_W), lambda i: (0, 0))
    return pl.pallas_call(
        body, name=name, grid=(nt,), in_specs=[prow, pprev, row, nxt, wspec],
        out_specs=(pl.BlockSpec((tm, 2 * CONV_W), lambda i: (i, 0)), wspec),
        out_shape=(jax.ShapeDtypeStruct((t, 2 * CONV_W), F32), jax.ShapeDtypeStruct((CONV_HALO, CONV_W), F32)),
        scratch_shapes=[pltpu.VMEM((tm + CONV_HALO, CONV_W), F32), pltpu.VMEM((tm + CONV_HALO, CONV_W), F32),
                        pltpu.VMEM((SUBLANES - 1, tm + CONV_HALO - SUBLANES, CONV_W), F32),
                        pltpu.VMEM((SUBLANES - 1, tm + CONV_HALO - SUBLANES, CONV_W), F32)],
        compiler_params=_params("arbitrary"),
    )(p, p, dy, dy, w)


def _post_norm_loss(x, g, target, name):
    t, d = x.shape
    tm = _row_tile(t)

    def body(x_ref, g_ref, t_ref, l_ref, dx_ref, dxb_ref, dg_ref):
        xv = x_ref[...]
        r = lax.rsqrt(jnp.mean(xv * xv, axis=-1, keepdims=True) + EPS)
        n = xv * r
        err = n * g_ref[...] - t_ref[...]
        dy = err * (1.0 / d)
        dn = dy * g_ref[...]
        dx = r * (dn - n * jnp.mean(dn * n, axis=-1, keepdims=True))
        dx_ref[...] = dx
        dxb_ref[...] = dx.astype(BF16)

        @pl.when(pl.program_id(0) == 0)
        def _():
            l_ref[...] = jnp.zeros_like(l_ref)
            dg_ref[...] = jnp.zeros_like(dg_ref)

        row = jnp.sum(err * err, axis=-1, keepdims=True) * (0.5 / d)
        l_ref[...] += jnp.broadcast_to(jnp.sum(row, axis=0, keepdims=True), (1, LANE))
        dg_ref[...] += jnp.sum(dy * n, axis=0, keepdims=True)

    row = pl.BlockSpec((tm, d), lambda i: (i, 0))
    vec = pl.BlockSpec((1, d), lambda i: (0, 0))
    loss, dx, dxb, dg = pl.pallas_call(
        body, name=name, grid=(t // tm,), in_specs=[row, vec, row],
        out_specs=(pl.BlockSpec((1, LANE), lambda i: (0, 0)), row, row, vec),
        out_shape=(jax.ShapeDtypeStruct((1, LANE), F32), jax.ShapeDtypeStruct((t, d), F32),
                   jax.ShapeDtypeStruct((t, d), BF16), jax.ShapeDtypeStruct((1, d), F32)),
        compiler_params=_params("arbitrary"),
    )(x, g.reshape(1, d), target)
    return loss, dx, dxb, dg.reshape(d)


def _adamw(w, g, m, v, name, after=None):
    r, c = w.shape
    tr = _row_tile(r, 256)
    c1, c2 = 1.0 - ADAM_B1 ** ADAM_STEP, 1.0 - ADAM_B2 ** ADAM_STEP

    def body(w_ref, g_ref, m_ref, v_ref, *refs):
        d_ref, mo_ref, vo_ref = refs[-3:]
        gv = g_ref[...]
        mn = ADAM_B1 * m_ref[...] + (1.0 - ADAM_B1) * gv
        vn = ADAM_B2 * v_ref[...] + (1.0 - ADAM_B2) * (gv * gv)
        mo_ref[...] = mn
        vo_ref[...] = vn
        d_ref[...] = -ADAM_LR * ((mn / c1) / (jnp.sqrt(vn / c2) + ADAM_EPS) + ADAM_WD * w_ref[...])

    blk = pl.BlockSpec((tr, c), lambda i: (i, 0))
    shp = jax.ShapeDtypeStruct((r, c), F32)
    extra = [] if after is None else [pl.BlockSpec(after.shape, lambda i: (0, 0))]
    return pl.pallas_call(
        body, name=name, grid=(r // tr,), in_specs=[blk] * 4 + extra, out_specs=(blk, blk, blk),
        out_shape=(shp, shp, shp), compiler_params=_params("parallel"),
    )(w, g, m, v, *([] if after is None else [after]))


def _sum_parts(parts, name):
    r, c = parts[0].shape
    tr = _row_tile(r, 256)

    def body(*refs):
        acc = refs[0][...]
        for ref in refs[1:-1]:
            acc = acc + ref[...]
        refs[-1][...] = acc

    blk = pl.BlockSpec((tr, c), lambda i: (i, 0))
    return pl.pallas_call(
        body, name=name, grid=(r // tr,), in_specs=[blk] * len(parts), out_specs=blk,
        out_shape=jax.ShapeDtypeStruct((r, c), F32), compiler_params=_params("parallel"),
    )(*parts)


def _place():
    return lax.axis_index("x"), lax.axis_index("y"), lax.axis_index("c")


def _window(ref, block, size, axis):
    start = pl.multiple_of(block * size, LANE if size % LANE == 0 else 8)
    return ref.at[(slice(None),) * axis + (pl.ds(start, size),)]


def _all_gather(pieces, name, in_vmem=False):
    n_p = n_all = len(pieces)

    def body(*refs):
        x_refs, out_refs = refs[:n_all], refs[n_all:2 * n_all]
        send_sems, recv_sems, local_sems = refs[2 * n_all:]
        px, py, pc = _place()
        me, sibling = (px, py, pc), (px, py, 1 - pc)
        chips = [(1 - px, py), (px, 1 - py), (1 - px, 1 - py)]

        def win(p, block):
            bx, by, bc = block
            x, axis = pieces[p]
            return _window(out_refs[p], 4 * bx + 2 * by + bc, x.shape[axis], axis)

        def copy(k, p, block, to, local=False):
            return pltpu.make_async_remote_copy(
                src_ref=x_refs[p] if local else win(p, block), dst_ref=win(p, block),
                send_sem=send_sems.at[k, p], recv_sem=recv_sems.at[k, p], device_id=to, device_id_type=MESH_ID)

        every = range(n_p)
        mine = [pltpu.make_async_copy(x_refs[p], win(p, me), local_sems.at[p]) for p in range(n_all)]
        first = [copy(0, p, me, sibling, local=True) for p in every]
        first += [copy(1 + j, p, me, (*chip, pc), local=True) for j, chip in enumerate(chips) for p in every]
        for cp in mine + first:
            cp.start()
        passed = []
        for j, chip in enumerate(chips):
            for p in every:
                copy(1 + j, p, (*chip, pc), me).wait_recv()
                passed.append(copy(4 + j, p, (*chip, pc), sibling))
                passed[-1].start()
        for p in every:
            copy(0, p, sibling, me).wait_recv()
        for j, chip in enumerate(chips):
            for p in every:
                copy(4 + j, p, (*chip, 1 - pc), me).wait_recv()
        for cp in first + passed:
            cp.wait_send()
        for cp in mine:
            cp.wait()

    def gathered(x, axis):
        return jax.ShapeDtypeStruct(x.shape[:axis] + (N_DEV * x.shape[axis],) + x.shape[axis + 1:], x.dtype)

    spec = VMEM_SPEC if in_vmem else ANY
    return pl.pallas_call(
        body, name=name, in_specs=[spec] * n_all, out_specs=[spec] * n_all,
        out_shape=[gathered(*pc_) for pc_ in pieces],
        scratch_shapes=[pltpu.SemaphoreType.DMA((7, n_p)), pltpu.SemaphoreType.DMA((7, n_p)),
                        pltpu.SemaphoreType.DMA((n_all,))],
        compiler_params=pltpu.CompilerParams(vmem_limit_bytes=VMEM_LIMIT),
    )(*[x for x, _ in pieces])


def _start_copies(bufs, n_copies, plan, name, after=None):
    nb = len(bufs)
    n_in = nb + (after is not None)

    def body(*refs):
        send_sems, recv_sems, token = refs[n_in], refs[n_in + 1], refs[-1]
        for i, (src, dst, dev) in enumerate(plan(refs[:nb])):
            pltpu.make_async_remote_copy(src_ref=src, dst_ref=dst, send_sem=send_sems.at[i], recv_sem=recv_sems.at[i],
                                         device_id=dev, device_id_type=MESH_ID).start()
        token[...] = jnp.zeros_like(token)

    out = pl.pallas_call(
        body, name=name, in_specs=[HBM_SPEC] * nb + [ANY] * (after is not None),
        out_shape=(pltpu.SemaphoreType.DMA((n_copies,)), pltpu.SemaphoreType.DMA((n_copies,)),
                   *[pltpu.HBM(b.shape, b.dtype) for b in bufs], jax.ShapeDtypeStruct((8, LANE), F32)),
        out_specs=(SEM_SPEC, SEM_SPEC, *[HBM_SPEC] * nb, VMEM_SPEC),
        input_output_aliases={i: 2 + i for i in range(nb)},
        compiler_params=pltpu.CompilerParams(has_side_effects=DATAFLOW),
    )(*[pltpu.with_memory_space_constraint(b, pltpu.HBM) for b in bufs], *([after] if after is not None else []))
    return out[0], out[1], list(out[2:2 + nb]), out[-1]


def _wait_copies(started, after, n_copies, plan, name):
    send_sems, recv_sems, bufs, _ = started
    nb = len(bufs)

    def body(*refs):
        send_ref, recv_ref = refs[nb], refs[nb + 1]
        copies = [pltpu.make_async_remote_copy(src_ref=src, dst_ref=dst, send_sem=send_ref.at[i], recv_sem=recv_ref.at[i],
                                               device_id=dev, device_id_type=MESH_ID)
                  for i, (src, dst, dev) in enumerate(plan(refs[:nb]))]
        for cp in copies:
            cp.wait_send()
        for cp in copies:
            cp.wait_recv()

    out = pl.pallas_call(
        body, name=name, in_specs=[HBM_SPEC] * nb + [SEM_SPEC, SEM_SPEC, ANY],
        out_shape=tuple(pltpu.HBM(b.shape, b.dtype) for b in bufs), out_specs=tuple([HBM_SPEC] * nb),
        input_output_aliases={i: i for i in range(nb)},
        compiler_params=pltpu.CompilerParams(has_side_effects=DATAFLOW),
    )(*bufs, send_sems, recv_sems, after)
    return list(out)


def _after(x, token):
    return x + token[0, 0].astype(x.dtype)


def _other_chips():
    px, py, _ = _place()
    return [(1 - px, py), (px, 1 - py), (1 - px, 1 - py)]


def _exchange(srcs, slots, src_block, target, name):
    n_p = len(srcs)

    def body(*refs):
        src_refs, out_refs, send_sems, recv_sems = refs[:n_p], refs[n_p:2 * n_p], refs[-2], refs[-1]
        copies = [pltpu.make_async_remote_copy(
            src_ref=src_refs[p].at[src_block(s)], dst_ref=out_refs[p].at[s], send_sem=send_sems.at[s, p],
            recv_sem=recv_sems.at[s, p], device_id=target(s), device_id_type=MESH_ID)
            for s in range(slots) for p in range(n_p)]
        for cp in copies:
            cp.start()
        for cp in copies:
            cp.wait_recv()
        for cp in copies:
            cp.wait_send()

    return pl.pallas_call(
        body, name=name, in_specs=[ANY] * n_p, out_specs=[ANY] * n_p,
        out_shape=[jax.ShapeDtypeStruct((slots,) + a.shape[1:], a.dtype) for a in srcs],
        scratch_shapes=[pltpu.SemaphoreType.DMA((slots, n_p)), pltpu.SemaphoreType.DMA((slots, n_p))],
        compiler_params=pltpu.CompilerParams(vmem_limit_bytes=VMEM_LIMIT),
    )(*srcs)


def _blocks_to_sibling(sends, name):
    def src_block(j):
        return 2 * j + 1 - lax.axis_index("c")

    def target(j):
        px, py, pc = _place()
        return (px, py, 1 - pc)

    return _exchange(sends, 4, src_block, target, name)


def _pair_sums_for_chips(own, got, name):
    _, r, c = own.shape
    tr = _row_tile(r, 256, 16)

    def body(idx_ref, own_ref, got_ref, o_ref):
        o_ref[...] = (own_ref[...] + got_ref[...].astype(F32)).astype(o_ref.dtype)

    grid_spec = pltpu.PrefetchScalarGridSpec(
        num_scalar_prefetch=1, grid=(3, r // tr),
        in_specs=[pl.BlockSpec((None, tr, c), lambda k, i, idx: (idx[k], i, 0)),
                  pl.BlockSpec((None, tr, c), lambda k, i, idx: (idx[3 + k], i, 0))],
        out_specs=pl.BlockSpec((None, tr, c), lambda k, i, idx: (k, i, 0)))
    chips = [2 * cx + cy for cx, cy in _other_chips()]
    idx = jnp.stack([2 * j + lax.axis_index("c") for j in chips] + chips).astype(jnp.int32)
    return pl.pallas_call(
        body, name=name, grid_spec=grid_spec, out_shape=jax.ShapeDtypeStruct((3, r, c), BF16),
        compiler_params=_params("parallel", "parallel"),
    )(idx, own, got)


def _sum_for_me(own, got_sibling, got_chips, name):
    _, r, c = own.shape
    tr = _row_tile(r, 256, 16)

    def body(idx_ref, own_ref, sib_ref, g0_ref, g1_ref, g2_ref, o_ref):
        acc = own_ref[...] + sib_ref[...].astype(F32)
        for ref in (g0_ref, g1_ref, g2_ref):
            acc = acc + ref[...].astype(F32)
        o_ref[...] = acc

    def part(k):
        return pl.BlockSpec((None, tr, c), lambda i, idx: (k, i, 0))

    grid_spec = pltpu.PrefetchScalarGridSpec(
        num_scalar_prefetch=1, grid=(r // tr,),
        in_specs=[pl.BlockSpec((None, tr, c), lambda i, idx: (idx[0], i, 0)),
                  pl.BlockSpec((None, tr, c), lambda i, idx: (idx[1], i, 0)), part(0), part(1), part(2)],
        out_specs=pl.BlockSpec((tr, c), lambda i, idx: (i, 0)))
    px, py, pc = _place()
    idx = jnp.stack([4 * px + 2 * py + pc, 2 * px + py]).astype(jnp.int32)
    return pl.pallas_call(
        body, name=name, grid_spec=grid_spec, out_shape=jax.ShapeDtypeStruct((r, c), F32),
        compiler_params=_params("parallel"),
    )(idx, own, got_sibling, got_chips, got_chips, got_chips)


def _chip_plan(n_p):
    def plan(refs):
        pc = lax.axis_index("c")
        return [(refs[p].at[k], refs[n_p + p].at[k], (cx, cy, pc))
                for p in range(n_p) for k, (cx, cy) in enumerate(_other_chips())]
    return plan


def _reduce_start(own, sends, tag):
    from_sibling = _blocks_to_sibling(sends, "grads_to_sibling_" + tag)
    pair_sums = [_pair_sums_for_chips(a, b, "grads_pair_sums") for a, b in zip(own, from_sibling)]
    lands = [lax.empty(a.shape, a.dtype) for a in pair_sums]
    started = _start_copies(pair_sums + lands, 3 * len(own), _chip_plan(len(own)), "grads_to_chips_start_" + tag)
    return from_sibling, started


def _reduce_finish(own, from_sibling, started, after, tag):
    n_p = len(own)
    bufs = _wait_copies(started, after, 3 * n_p, _chip_plan(n_p), "grads_to_chips_wait_" + tag)
    return [_sum_for_me(a, b, c, "grads_sum") for a, b, c in zip(own, from_sibling, bufs[n_p:])]


def _direct_plan(n_p):
    def plan(refs):
        px, py, pc = _place()
        out = []
        for p in range(n_p):
            for m in range(1, N_DEV):
                tx = 1 - px if m & 4 else px
                ty = 1 - py if m & 2 else py
                tc = 1 - pc if m & 1 else pc
                out.append((refs[p].at[4 * tx + 2 * ty + tc], refs[n_p + p].at[m - 1], (tx, ty, tc)))
        return out
    return plan


def _sum_direct(own, got, name):
    _, r, c = own.shape
    tr = _row_tile(r, 256, 16)

    def body(idx_ref, own_ref, *refs):
        acc = own_ref[...]
        for ref in refs[:-1]:
            acc = acc + ref[...].astype(F32)
        refs[-1][...] = acc

    def part(k):
        return pl.BlockSpec((None, tr, c), lambda i, idx: (k, i, 0))

    grid_spec = pltpu.PrefetchScalarGridSpec(
        num_scalar_prefetch=1, grid=(r // tr,),
        in_specs=[pl.BlockSpec((None, tr, c), lambda i, idx: (idx[0], i, 0))] + [part(k) for k in range(N_DEV - 1)],
        out_specs=pl.BlockSpec((tr, c), lambda i, idx: (i, 0)))
    px, py, pc = _place()
    idx = (4 * px + 2 * py + pc).astype(jnp.int32).reshape(1)
    return pl.pallas_call(
        body, name=name, grid_spec=grid_spec, out_shape=jax.ShapeDtypeStruct((r, c), F32),
        compiler_params=_params("parallel"),
    )(idx, own, *([got] * (N_DEV - 1)))


def _reduce_direct_start(sends, tag):
    lands = [lax.empty((N_DEV - 1,) + a.shape[1:], a.dtype) for a in sends]
    return _start_copies(list(sends) + lands, (N_DEV - 1) * len(sends), _direct_plan(len(sends)),
                         "grads_direct_start_" + tag)


def _reduce_direct_finish(own, started, after, tag):
    n_p = len(own)
    bufs = _wait_copies(started, after, (N_DEV - 1) * n_p, _direct_plan(n_p), "grads_direct_wait_" + tag)
    return [_sum_direct(a, b, "grads_sum_direct") for a, b in zip(own, bufs[n_p:])]


def _gather_plans(pieces):
    n_p = len(pieces)
    dims = [(x.shape[axis], axis) for x, axis in pieces]

    def first(refs):
        px, py, pc = _place()
        targets = [(px, py, 1 - pc)] + [(cx, cy, pc) for cx, cy in _other_chips()]
        return [(refs[p], _window(refs[n_p + p], 4 * px + 2 * py + pc, *dims[p]), to)
                for p in range(n_p) for to in targets]

    def second(refs):
        px, py, pc = _place()
        out = []
        for p in range(n_p):
            for cx, cy, cc in [(cx, cy, pc) for cx, cy in _other_chips()] + [(px, py, 1 - pc)]:
                win = _window(refs[p], 4 * cx + 2 * cy + cc, *dims[p])
                out.append((win, win, (px, py, 1 - pc)))
        return out

    return first, second


FORWARD_COPIES = 4


def _flat_rows(a, width):
    return a.reshape(-1, width)


def _full_from_blocks(blocks, name):
    if name in COL_SHARDED:
        _, l, k, nb = blocks.shape
        return jnp.transpose(blocks, (1, 2, 0, 3)).reshape(l, k, N_DEV * nb)
    _, l, rb, n = blocks.shape
    return jnp.transpose(blocks, (1, 0, 2, 3)).reshape(l, N_DEV * rb, n)


def _blocks_from_full(full, name):
    if name in COL_SHARDED:
        l, k, n = full.shape
        return jnp.transpose(full.reshape(l, k, N_DEV, n // N_DEV), (2, 0, 1, 3))
    l, rows, n = full.shape
    return jnp.transpose(full.reshape(l, N_DEV, rows // N_DEV, n), (1, 0, 2, 3))


def _pad_heads(w, width):
    k = w.shape[0]
    return jnp.pad(w.reshape(k, N_HEADS, width), ((0, 0), (0, 0), (0, HEAD_PAD - width))).reshape(k, N_HEADS * HEAD_PAD)


def _unpad_heads(w, width):
    k = w.shape[0]
    return w.reshape(k, N_HEADS, HEAD_PAD)[:, :, :width].reshape(k, N_HEADS * width)


def _layer_operands(full, vec, conv_w_full, l):
    w_in = full['w_in'][l]
    kpe = jnp.pad(w_in[:, LAT:LAT + QK_ROPE], ((0, 0), (QK_NOPE, HEAD_PAD - QK_DIM)))
    w_ukv = full['w_ukv'][l].reshape(KV_LORA, N_HEADS, QK_NOPE + V_DIM)
    w_out = full['w_out'][l]
    d_model = w_out.shape[1]
    wo_attn = jnp.pad(w_out[:N_HEADS * V_DIM].reshape(N_HEADS, V_DIM, d_model),
                      ((0, 0), (0, HEAD_PAD - V_DIM), (0, 0))).reshape(N_HEADS * HEAD_PAD, d_model)
    ops = {
        'w_in': jnp.concatenate([w_in[:, :LAT], kpe, w_in[:, LAT + QK_ROPE:]], axis=1),
        'w_q': _pad_heads(full['w_uq'][l], QK_DIM),
        'w_k': _pad_heads(w_ukv[:, :, :QK_NOPE].reshape(KV_LORA, N_HEADS * QK_NOPE), QK_NOPE),
        'w_v': _pad_heads(w_ukv[:, :, QK_NOPE:].reshape(KV_LORA, N_HEADS * V_DIM), V_DIM),
        'wo_attn': wo_attn,
        'wo_conv': w_out[N_HEADS * V_DIM:],
        'conv_w': jnp.pad(conv_w_full[l], ((0, CONV_HALO - CONV_K), (0, 0))),
        'gq': jnp.pad(vec['q_norm'][l], (0, HEAD_PAD - QK_DIM)),
        'gk': jnp.pad(vec['k_norm'][l], (0, HEAD_PAD - QK_DIM)),
    }
    for n in ('ffn1_norm', 'mix_norm', 'q_latent_norm', 'kv_latent_norm', 'conv_b', 'conv_ln_g', 'conv_ln_b',
              'ffn2_norm', 'post_norm'):
        ops[n] = vec[n][l]
    return ops


def _ffn_fwd(x, g, wgu, wd, fp):
    h = _rms_fwd(x, g, BF16, "rms_fwd_ffn")
    ab, z = _ffn_up(h, wgu, fp, "ffn_up")
    y = _mm(z, wd, res=x, scale=0.5, name="ffn_down")
    return y, (x, h, ab, z)


def _ffn_bwd(dy, dyb, saved, g, wgu, wd, fp, after_dw=None, after=None, before_dw=None):
    x, h, ab, z = saved
    dab, dx, dxb, dg = _ffn_bwd_rows(dyb, wd, ab, wgu, x, g, dy, fp, "ffn_bwd_rows", after=after)
    first = before_dw(dg) if before_dw is not None else None
    d_wgu = _mm(h, dab, ta=True, blocks=('col', N_DEV), tm=h.shape[1], after=first, name="ffn_dwgu")
    d_wd = _mm(z, dyb, ta=True, scale=0.5, blocks=('row', N_DEV), name="ffn_dwd")
    token = after_dw(d_wgu, d_wd) if after_dw is not None else None
    return dx, dxb, dg, token


def _mixer_fwd(x, ops, tabs, after_attention=None):
    h = _rms_fwd(x, ops['mix_norm'], BF16, "rms_fwd_mix")
    p = _mm(h, ops['w_in'], name="mix_in")
    qln, kvln = _lat_norm_fwd(p, ops['q_latent_norm'], ops['kv_latent_norm'], "lat_norm_fwd")
    q_raw = _mm(qln, ops['w_q'], name="mix_q")
    k_raw = _mm(kvln, ops['w_k'], name="mix_k")
    v = _mm(kvln, ops['w_v'], out_dtype=BF16, name="mix_v")
    q, k = _qk_prep_fwd(q_raw, k_raw, p, ops['gq'], ops['gk'], tabs, "qk_prep_fwd")
    o, lse = _attn_fwd(q, k, v, "attn_fwd")
    token = after_attention(o) if after_attention is not None else None
    conv_b = ops['conv_b'] if token is None else _after(ops['conv_b'], token)
    y_conv, cv = _conv_fwd(p, ops['conv_w'], conv_b, ops['conv_ln_g'], ops['conv_ln_b'], "conv_fwd")
    x_attn = _mm(o, ops['wo_attn'], res=x, name="mix_out_attn")
    x_out = _mm(cv, ops['wo_conv'], res=x_attn, name="mix_out_conv")
    return x_out, (x, h, p, qln, kvln, q_raw, k_raw, v, q, k, o, lse, y_conv, cv)


def _mixer_bwd(dx_out, dxb_out, saved, ops, tabs, token=None):
    x, h, p, qln, kvln, q_raw, k_raw, v, q, k, o, lse, y_conv, cv = saved
    g = {}
    do = _mm(dxb_out, ops['wo_attn'], tb=True, after=token, name="mix_do")
    dcv = _mm(dxb_out, ops['wo_conv'], tb=True, name="mix_dcv")
    g['wo_attn'] = _mm(o, dxb_out, ta=True, name="mix_dwo_attn")
    g['wo_conv'] = _mm(cv, dxb_out, ta=True, name="mix_dwo_conv")
    dq, dk, dv = _attn_bwd(q, k, v, o, lse, do, "attn_bwd")
    dq_raw, dk_raw, dkpe, g['gq'], g['gk'] = _qk_prep_bwd(q_raw, k_raw, p, dq, dk, ops['gq'], ops['gk'], tabs,
                                                          "qk_prep_bwd")
    g['w_q'] = _mm(qln, dq_raw, ta=True, name="mix_dwq")
    g['w_k'] = _mm(kvln, dk_raw, ta=True, name="mix_dwk")
    g['w_v'] = _mm(kvln, dv, ta=True, name="mix_dwv")
    dqln = _mm(dq_raw, ops['w_q'], tb=True, name="mix_dqln")
    dkvln = _mm(dk_raw, ops['w_k'], tb=True, name="mix_dkvln_k")
    dkvln = _mm(dv, ops['w_v'], tb=True, res=dkvln, name="mix_dkvln_v")
    dp_lat, g['q_latent_norm'], g['kv_latent_norm'] = _lat_norm_bwd(
        p, ops['q_latent_norm'], ops['kv_latent_norm'], dqln, dkvln, "lat_norm_bwd")
    dy_conv, g['conv_ln_g'], g['conv_ln_b'], g['conv_b'] = _conv_bwd_ln(
        y_conv, dcv, ops['conv_ln_g'], ops['conv_ln_b'], "conv_bwd_ln")
    dag, g['conv_w'] = _conv_bwd_taps(p, dy_conv, ops['conv_w'], "conv_bwd_taps")
    dp = jnp.concatenate([dp_lat, dkpe, dag], axis=1)
    g['w_in'] = _mm(h, dp, ta=True, name="mix_dw_in")
    dx, dxb, g['mix_norm'] = _matmul_rms_bwd(dp, ops['w_in'], x, ops['mix_norm'], dx_out, "mix_dh_rms")
    return dx, dxb, g


def _mixer_grads_to_params(g):
    d_w_in = g['w_in']
    d_wk = _unpad_heads(g['w_k'], QK_NOPE).reshape(KV_LORA, N_HEADS, QK_NOPE)
    d_wv = _unpad_heads(g['w_v'], V_DIM).reshape(KV_LORA, N_HEADS, V_DIM)
    d_model = g['wo_attn'].shape[1]
    d_wo_attn = g['wo_attn'].reshape(N_HEADS, HEAD_PAD, d_model)[:, :V_DIM].reshape(N_HEADS * V_DIM, d_model)
    return {
        'mix_norm': g['mix_norm'],
        'w_in': jnp.concatenate([d_w_in[:, :LAT], d_w_in[:, LAT + QK_NOPE:LAT + QK_DIM], d_w_in[:, P_A:]], axis=1),
        'q_latent_norm': g['q_latent_norm'], 'w_uq': _unpad_heads(g['w_q'], QK_DIM),
        'kv_latent_norm': g['kv_latent_norm'],
        'w_ukv': jnp.concatenate([d_wk, d_wv], axis=2).reshape(KV_LORA, N_HEADS * (QK_NOPE + V_DIM)),
        'q_norm': g['gq'][:QK_DIM], 'k_norm': g['gk'][:QK_DIM], 'conv_w': g['conv_w'][:CONV_K],
        'conv_b': g['conv_b'], 'conv_ln_g': g['conv_ln_g'], 'conv_ln_b': g['conv_ln_b'],
        'w_out': jnp.concatenate([d_wo_attn, g['wo_conv']], axis=0),
    }


def kernel(x, ffn1_norm, ffn1_w_gate, ffn1_w_up, ffn1_w_down, mix_norm, w_in, q_latent_norm, w_uq, kv_latent_norm, w_ukv, q_norm, k_norm, conv_w, conv_b, conv_ln_g, conv_ln_b, w_out, ffn2_norm, ffn2_w_gate, ffn2_w_up, ffn2_w_down, post_norm, loss_target, m_ffn1_norm, m_ffn1_w_gate, m_ffn1_w_up, m_ffn1_w_down, m_mix_norm, m_w_in, m_q_latent_norm, m_w_uq, m_kv_latent_norm, m_w_ukv, m_q_norm, m_k_norm, m_conv_w, m_conv_b, m_conv_ln_g, m_conv_ln_b, m_w_out, m_ffn2_norm, m_ffn2_w_gate, m_ffn2_w_up, m_ffn2_w_down, m_post_norm, v_ffn1_norm, v_ffn1_w_gate, v_ffn1_w_up, v_ffn1_w_down, v_mix_norm, v_w_in, v_q_latent_norm, v_w_uq, v_kv_latent_norm, v_w_ukv, v_q_norm, v_k_norm, v_conv_w, v_conv_b, v_conv_ln_g, v_conv_ln_b, v_w_out, v_ffn2_norm, v_ffn2_w_gate, v_ffn2_w_up, v_ffn2_w_down, v_post_norm):
    args = locals()
    w = {n: args[n] for n in WEIGHTS}
    mom = {n: args["m_" + n] for n in WEIGHTS}
    var = {n: args["v_" + n] for n in WEIGHTS}
    depth = ffn1_norm.shape[0]
    x0 = x.reshape(x.shape[-2:])
    target = loss_target.reshape(loss_target.shape[-2:])
    t, d_model = x0.shape
    my_block = 4 * lax.axis_index("x") + 2 * lax.axis_index("y") + lax.axis_index("c")

    fb = ffn1_w_gate.shape[-1]
    fp = -(-fb // LANE) * LANE
    ffns = [(l, f) for l in range(depth) for f in (1, 2)]
    pad_cols = lambda a: jnp.pad(a, ((0, 0), (0, fp - fb)))
    gu_local = {(l, f): jnp.concatenate([pad_cols(w[f'ffn{f}_w_gate'][l]), pad_cols(w[f'ffn{f}_w_up'][l])],
                                        axis=1).astype(BF16) for l, f in ffns}
    dn_local = {(l, f): jnp.pad(w[f'ffn{f}_w_down'][l], ((0, fp - fb), (0, 0))).astype(BF16) for l, f in ffns}
    rows_of = {n: w[n].size // d_model for n in REST}
    rest_local = jnp.concatenate([_flat_rows(w[n].astype(BF16), d_model) for n in REST], axis=0)
    n_rest = rest_local.shape[0]
    first_ffn, later = ffns[0], ffns[1:]
    cw = conv_w.reshape(-1)
    cw_rows = -(-cw.size // (8 * LANE)) * 8
    cw_flat = jnp.pad(cw, (0, cw_rows * LANE - cw.size)).reshape(cw_rows, LANE)
    later_pieces = [(gu_local[q], 1) for q in later] + [(dn_local[q], 0) for q in later]
    n_later = len(later_pieces)
    got = _all_gather([(gu_local[first_ffn], 1), (dn_local[first_ffn], 0)], "gather_first")
    wgu, wd = {first_ffn: got[0]}, {first_ffn: got[1]}
    mixer_pieces = [(rest_local, 0), (cw_flat, 0)]
    mixer_plan, mixer_forward_plan = _gather_plans(mixer_pieces)
    gather_plan, forward_plan = _gather_plans(later_pieces)

    def landing(a, axis):
        return lax.empty(a.shape[:axis] + (N_DEV * a.shape[axis],) + a.shape[axis + 1:], a.dtype)

    gather_mixer = _start_copies([a for a, _ in mixer_pieces] + [landing(a, ax) for a, ax in mixer_pieces],
                                 4 * len(mixer_pieces), mixer_plan, "gather_mixer_start", after=got[0])
    gather_later = _start_copies([a for a, _ in later_pieces] + [landing(a, ax) for a, ax in later_pieces],
                                 4 * n_later, gather_plan, "gather_later_start", after=gather_mixer[3])

    x1_first, s1_first = _ffn_fwd(x0, _after(w['ffn1_norm'][0], gather_later[3]), wgu[first_ffn], wd[first_ffn], fp)
    lands = _wait_copies(gather_mixer, x1_first, 4 * len(mixer_pieces), mixer_plan, "gather_mixer_wait")[2:]
    pass_on = _start_copies(lands, FORWARD_COPIES * len(mixer_pieces), mixer_forward_plan, "gather_mixer_forward_start")
    lands = _wait_copies(pass_on, pass_on[3], FORWARD_COPIES * len(mixer_pieces), mixer_forward_plan, "gather_mixer_forward_wait")
    gathered = lands[0].reshape(N_DEV, n_rest, d_model)
    cw_all = lands[1].reshape(N_DEV, cw_rows * LANE)[:, :cw.size]
    full, start = {}, 0
    for n in REST:
        blocks = gathered[:, start:start + rows_of[n]].reshape((N_DEV,) + w[n].shape)
        full[n] = _full_from_blocks(blocks, n)
        start += rows_of[n]
    conv_w_full =jnp.transpose(cw_all.reshape((N_DEV,) + conv_w.shape), (1, 2, 0, 3)).reshape(depth, CONV_K, CONV_W)
    vec = {n: w[n] for n in VECTORS}
    ops = [_layer_operands(full, vec, conv_w_full, l) for l in range(depth)]
    tabs = _rope_tables(t)

    saved, xl = [], x0
    forward_later = []

    def pass_on_later(o_attn):
        lands = _wait_copies(gather_later, o_attn, 4 * n_later, gather_plan, "gather_later_wait")[n_later:]
        forward_later.append(_start_copies(lands, FORWARD_COPIES * n_later, forward_plan, "gather_later_forward_start"))
        return forward_later[0][3]

    for l in range(depth):
        o = ops[l]
        if l == 0:
            x1, s1 = x1_first, s1_first
            x2, sm = _mixer_fwd(x1, o, tabs, after_attention=pass_on_later)
            lands = _wait_copies(forward_later[0], x2, FORWARD_COPIES * n_later, forward_plan, "gather_later_forward_wait")
            wgu.update(zip(later, lands[:len(later)]))
            wd.update(zip(later, lands[len(later):]))
        else:
            x1, s1 = _ffn_fwd(xl, o['ffn1_norm'], wgu[l, 1], wd[l, 1], fp)
            x2, sm = _mixer_fwd(x1, o, tabs)
        x3, s2 = _ffn_fwd(x2, o['ffn2_norm'], wgu[l, 2], wd[l, 2], fp)
        if l + 1 < depth:
            xl = _rms_fwd(x3, o['post_norm'], F32, "rms_fwd_post")
        saved.append((s1, sm, s2, x3))
    loss_part, dx, dxb, d_post_last = _post_norm_loss(x3, ops[depth - 1]['post_norm'], target, "post_norm_loss")
    loss = lax.psum(loss_part[0, 0], ("x", "y", "c"))

    grads, mine_gu, mine_dn, in_flight = [None] * depth, {}, {}, {}

    def exchange(tag):
        def after_dw(d_wgu, d_wd):
            own = [d_wgu[0], d_wd[0]]
            if tag == last_tag:
                from_sibling, started = _reduce_start(own, [d_wgu[1], d_wd[1]], tag)
            else:
                from_sibling, started = None, _reduce_direct_start([d_wgu[1], d_wd[1]], tag)
            in_flight[tag] = (own, from_sibling, started)
            return started[3]
        return after_dw

    def finish(tag, after):
        own, from_sibling, started = in_flight.pop(tag)
        if from_sibling is None:
            return _reduce_direct_finish(own, started, after, tag)
        return _reduce_finish(own, from_sibling, started, after, tag)

    last_tag = f"{first_ffn[0]}{first_ffn[1]}"
    vector_sums = []

    def reduce_vectors(norms0):
        layers = [dict(grads[k]) for k in range(depth)]
        layers[0].update(norms0)
        parts = [jnp.stack([layers[k][n] for k in range(depth)]).reshape(-1) for n in VECTORS + ['conv_w']]
        small = jnp.concatenate(parts)
        s_rows = -(-small.size // (8 * LANE)) * 8
        small = jnp.pad(small, (0, s_rows * LANE - small.size)).reshape(s_rows, LANE)
        small_all = _all_gather([(small, 0)], "gather_small_grads", in_vmem=True)[0]
        vector_sums.append(_sum_parts([small_all[k * s_rows:(k + 1) * s_rows] for k in range(N_DEV)],
                                      "sum_small_grads"))
        return vector_sums[0]

    token = None
    for l in reversed(range(depth)):
        o = ops[l]
        s1, sm, s2, x3 = saved[l]
        if l + 1 < depth:
            dx, dxb, d_post = _rms_bwd(x3, _after(o['post_norm'], token), dx, None, "rms_bwd_post")
        else:
            d_post = d_post_last
        dx, dxb, d_ffn2, token = _ffn_bwd(dx, dxb, s2, o['ffn2_norm'], wgu[l, 2], wd[l, 2], fp, exchange(f"{l}2"))
        if l + 1 < depth:
            mine_gu[l + 1, 1], mine_dn[l + 1, 1] = finish(f"{l + 1}1", dx)
        dx, dxb, gm = _mixer_bwd(dx, dxb, sm, o, tabs, token)
        mine_gu[l, 2], mine_dn[l, 2] = finish(f"{l}2", dx)
        grads[l] = _mixer_grads_to_params(gm)
        if l == 0:
            rest_own = jnp.concatenate(
                [_blocks_from_full(jnp.stack([grads[k][n] for k in range(depth)]), n).reshape(N_DEV, rows_of[n], d_model)
                 for n in REST], axis=1)
            sibling_rest, started_rest = _reduce_start([rest_own], [rest_own.astype(BF16)], "rest")
        norms = dict(post_norm=d_post, ffn2_norm=d_ffn2)
        dx, dxb, d_ffn1, token = _ffn_bwd(
            dx, dxb, s1, o['ffn1_norm'], wgu[l, 1], wd[l, 1], fp, exchange(f"{l}1"),
            after=started_rest[3] if l == 0 else None,
            before_dw=(lambda dg: reduce_vectors(dict(norms, ffn1_norm=dg))) if l == 0 else None)
        grads[l].update(norms, ffn1_norm=d_ffn1)
    grad_x = dx.reshape(x.shape)

    grad, delta, new_m, new_v = {}, {}, {}, {}

    def adamw(names, after=None):
        for n in names:
            shp = w[n].shape
            two_d = lambda a: a.reshape(-1, shp[-1])
            dl, mn, vn = _adamw(two_d(w[n]), two_d(grad[n]), two_d(mom[n]), two_d(var[n]), "adamw_" + n, after)
            delta[n], new_m[n], new_v[n] = dl.reshape(shp), mn.reshape(shp), vn.reshape(shp)
            after = dl[:8]
        return after

    def ffn_grads(f):
        grad[f'ffn{f}_w_gate'] = jnp.stack([mine_gu[l, f][:, :fb] for l in range(depth)])
        grad[f'ffn{f}_w_up'] = jnp.stack([mine_gu[l, f][:, fp:fp + fb] for l in range(depth)])
        grad[f'ffn{f}_w_down'] = jnp.stack([mine_dn[l, f][:fb] for l in range(depth)])
        return [f'ffn{f}_w_gate', f'ffn{f}_w_up', f'ffn{f}_w_down']

    small_sum = vector_sums[0].reshape(-1)
    start = 0
    for n in VECTORS:
        grad[n] = small_sum[start:start + w[n].size].reshape(w[n].shape)
        start += w[n].size
    cw_grad = small_sum[start:start + depth * CONV_K * CONV_W].reshape(depth, CONV_K, CONV_W)
    nb = conv_w.shape[-1]
    grad['conv_w'] = lax.dynamic_slice_in_dim(cw_grad, my_block * nb, nb, axis=2)
    done = adamw(ffn_grads(2) + ['conv_w'], after=token)
    vcat = lambda src: jnp.concatenate([src[n].reshape(-1) for n in VECTORS]).reshape(-1, LANE)
    dl, mn, vn = _adamw(vcat(w), vcat(grad), vcat(mom), vcat(var), "adamw_vectors", done)
    start = 0
    for n in VECTORS:
        sl = lambda a: a.reshape(-1)[start:start + w[n].size].reshape(w[n].shape)
        delta[n], new_m[n], new_v[n] = sl(dl), sl(mn), sl(vn)
        start += w[n].size
    mine_rest = _reduce_finish([rest_own], sibling_rest, started_rest, dl, "rest")[0]
    start = 0
    for n in REST:
        grad[n] = mine_rest[start:start + rows_of[n]].reshape(w[n].shape)
        start += rows_of[n]
    done = adamw(REST)
    mine_gu[first_ffn], mine_dn[first_ffn] = finish(f"{first_ffn[0]}{first_ffn[1]}", done)
    adamw(ffn_grads(1))

    return (loss, grad_x, *[grad[n] for n in WEIGHTS], *[delta[n] for n in WEIGHTS],
            *[new_m[n] for n in WEIGHTS], *[new_v[n] for n in WEIGHTS])
```

```python
import jax
import jax.numpy as jnp
from jax import lax
from jax.experimental import pallas as pl
from jax.experimental.pallas import tpu as pltpu

F32, BF16 = jnp.float32, jnp.bfloat16

N_DEV = 8
N_HEADS = 8
QK_NOPE, QK_ROPE, V_DIM = 64, 32, 64
QK_DIM = QK_NOPE + QK_ROPE
HEAD_PAD = 128
Q_LORA, KV_LORA = 384, 256
LAT = Q_LORA + KV_LORA
CONV_W, CONV_K = 512, 31
CONV_HALO = 32
CHUNK = 64
ROPE_THETA = 10000.0
EPS = 1e-6
ATTN_SCALE = QK_DIM ** -0.5
ATTN_SCALE_LOG2 = ATTN_SCALE * 1.4426950408889634
P_KPE = LAT
P_A = LAT + HEAD_PAD
P_G = P_A + CONV_W
P_COLS = P_G + CONV_W

ADAM_LR, ADAM_B1, ADAM_B2, ADAM_EPS, ADAM_WD, ADAM_STEP = 0.001, 0.9, 0.999, 1e-08, 0.01, 10

V7X_VMEM_BYTES = 64 << 20
VMEM_LIMIT = V7X_VMEM_BYTES - (8 << 20)
MM_VMEM_BUDGET = 40 << 20
LANE = 128
ROW_TILE = 512
ATTN_BLOCK = 512
ATTN_HEADS = 1
CONV_TILE = 256

MESH_ID = pl.DeviceIdType.MESH
ANY = pl.BlockSpec(memory_space=pl.ANY)
VMEM_SPEC = pl.BlockSpec(memory_space=pltpu.VMEM)
HBM_SPEC = pl.BlockSpec(memory_space=pltpu.HBM)
SEM_SPEC = pl.BlockSpec(memory_space=pltpu.SEMAPHORE)
DATAFLOW = pltpu.SideEffectType.DATAFLOW_SIDE_EFFECTING

WEIGHTS = ['ffn1_norm', 'ffn1_w_gate', 'ffn1_w_up', 'ffn1_w_down', 'mix_norm', 'w_in', 'q_latent_norm', 'w_uq',
           'kv_latent_norm', 'w_ukv', 'q_norm', 'k_norm', 'conv_w', 'conv_b', 'conv_ln_g', 'conv_ln_b', 'w_out',
           'ffn2_norm', 'ffn2_w_gate', 'ffn2_w_up', 'ffn2_w_down', 'post_norm']
COL_SHARDED =['ffn1_w_gate', 'ffn1_w_up', 'w_in', 'w_uq', 'w_ukv', 'ffn2_w_gate', 'ffn2_w_up']
REST = ['w_in', 'w_uq', 'w_ukv', 'w_out']
VECTORS = ['ffn1_norm', 'mix_norm', 'q_latent_norm', 'kv_latent_norm', 'q_norm', 'k_norm', 'conv_b', 'conv_ln_g',
           'conv_ln_b', 'ffn2_norm', 'post_norm']


def _params(*sem):
    return pltpu.CompilerParams(dimension_semantics=sem if sem else None, vmem_limit_bytes=VMEM_LIMIT)


def _tile(n, cap):
    if n <= cap:
        return n
    best = 0
    for d in range(LANE, cap + 1, LANE):
        if n % d == 0:
            best = d
    assert best, (n, cap)
    return best


def _row_tile(n, cap=ROW_TILE, mult=8):
    if n <= cap:
        return n
    best = 0
    for d in range(mult, cap + 1, mult):
        if n % d == 0:
            best = d
    assert best, (n, cap)
    return best


def _mm(a, b, *, name, ta=False, tb=False, res=None, scale=1.0, out_dtype=F32, tm=None, tn=None, blocks=None,
        after=None, norm_gain=None):
    (kdim, m) = a.shape if ta else a.shape[::-1]
    (n, kb) = b.shape if tb else b.shape[::-1]
    assert kdim == kb, (a.shape, b.shape, ta, tb)
    tm, tn = tm or _tile(m, 512), tn or _tile(n, 1024)
    if blocks is not None:
        tm, tn = (tm, n // blocks[1]) if blocks[0] == 'col' else (m // blocks[1], tn)
    assert norm_gain is None or (blocks is None and tn == n), (name, tn, n)
    size = lambda arr: jnp.dtype(arr.dtype).itemsize
    out_bytes = tm * tn * ((6 if blocks is not None else jnp.dtype(out_dtype).itemsize) + (4 if res is not None else 0))

    def vmem_need(tk):
        return 2 * (tm * tk * size(a) + tk * tn * size(b) + out_bytes) + (tm * tn * 4 if tk < kdim else 0)

    tk = kdim
    for cand in [d for d in range(kdim - LANE, 0, -LANE) if kdim % d == 0]:
        if vmem_need(tk) <= MM_VMEM_BUDGET:
            break
        tk = cand
    nk = kdim // tk
    n_in = 2 + (res is not None) + (after is not None) + (norm_gain is not None)
    n_out = 2 if blocks is not None or norm_gain is not None else 1
    dims = (((0 if ta else 1,), (1 if tb else 0,)), ((), ()))

    def body(*refs):
        a_ref, b_ref = refs[0], refs[1]
        r_ref = refs[2] if res is not None else None
        o_refs = refs[n_in:n_in + n_out]
        acc_ref = refs[-1] if nk > 1 else None
        part = lax.dot_general(a_ref[...].astype(BF16), b_ref[...].astype(BF16), dims, preferred_element_type=F32)

        def finish(acc):
            if scale != 1.0:
                acc = acc * scale
            if r_ref is not None:
                acc = r_ref[...] + acc
            if norm_gain is not None:
                gain_ref, (o_ref, h_ref) = refs[n_in - 1], o_refs
                o_ref[...] = acc.astype(o_ref.dtype)
                r = lax.rsqrt(jnp.mean(acc * acc, axis=-1, keepdims=True) + EPS)
                h_ref[...] = (acc * r * gain_ref[...]).astype(h_ref.dtype)
                return
            for o_ref in o_refs:
                o_ref[...] = acc.astype(o_ref.dtype)

        if nk == 1:
            finish(part)
        else:
            k = pl.program_id(2)

            @pl.when(k == 0)
            def _():
                acc_ref[...] = part

            @pl.when(k > 0)
            def _():
                acc_ref[...] += part

            @pl.when(k == nk - 1)
            def _():
                finish(acc_ref[...])

    a_spec = pl.BlockSpec((tk, tm), lambda i, j, k: (k, i)) if ta else pl.BlockSpec((tm, tk), lambda i, j, k: (i, k))
    b_spec = pl.BlockSpec((tn, tk), lambda i, j, k: (j, k)) if tb else pl.BlockSpec((tk, tn), lambda i, j, k: (k, j))
    plain = pl.BlockSpec((tm, tn), lambda i, j, k: (i, j))
    if blocks is None:
        out_specs, out_shape = plain, jax.ShapeDtypeStruct((m, n), out_dtype)
    else:
        if blocks[0] == 'col':
            o_spec, shp = pl.BlockSpec((None, tm, tn), lambda i, j, k: (j, i, 0)), (blocks[1], m, tn)
        else:
            o_spec, shp = pl.BlockSpec((None, tm, tn), lambda i, j, k: (i, 0, j)), (blocks[1], tm, n)
        out_specs, out_shape = (o_spec, o_spec), (jax.ShapeDtypeStruct(shp, F32), jax.ShapeDtypeStruct(shp, BF16))
    in_specs = [a_spec, b_spec] + ([plain] if res is not None else [])
    args = (a, b) + ((res,) if res is not None else ())
    if after is not None:
        in_specs.append(pl.BlockSpec(after.shape, lambda i, j, k: (0, 0)))
        args += (after,)
    if norm_gain is not None:
        in_specs.append(pl.BlockSpec((1, n), lambda i, j, k: (0, 0)))
        args += (norm_gain.reshape(1, n),)
        out_specs, out_shape = (plain, plain), (out_shape, jax.ShapeDtypeStruct((m, n), BF16))
    return pl.pallas_call(
        body, name=name, grid=(m // tm, n // tn, nk), in_specs=in_specs, out_specs=out_specs, out_shape=out_shape,
        scratch_shapes=[pltpu.VMEM((tm, tn), F32)] if nk > 1 else [],
        compiler_params=_params("parallel", "parallel", "arbitrary"),
    )(*args)


def _rms_fwd(x, g, out_dtype, name):
    t, d = x.shape
    tm = _row_tile(t)

    def body(x_ref, g_ref, o_ref):
        xv = x_ref[...]
        r = lax.rsqrt(jnp.mean(xv * xv, axis=-1, keepdims=True) + EPS)
        o_ref[...] = (xv * r * g_ref[...]).astype(o_ref.dtype)

    return pl.pallas_call(
        body, name=name, grid=(t // tm,),
        in_specs=[pl.BlockSpec((tm, d), lambda i: (i, 0)), pl.BlockSpec((1, d), lambda i: (0, 0))],
        out_specs=pl.BlockSpec((tm, d), lambda i: (i, 0)),
        out_shape=jax.ShapeDtypeStruct((t, d), out_dtype), compiler_params=_params("parallel"),
    )(x, g.reshape(1, d))


def _rms_bwd(x, g, dh, res, name):
    t, d = x.shape
    tm = _row_tile(t)

    def body(*refs):
        x_ref, g_ref, dh_ref = refs[:3]
        r_ref = refs[3] if res is not None else None
        dx_ref, dxb_ref, dg_ref = refs[-3:]
        xv, dhv = x_ref[...], dh_ref[...]
        r = lax.rsqrt(jnp.mean(xv * xv, axis=-1, keepdims=True) + EPS)
        y = xv * r
        dy = dhv * g_ref[...]
        dx = r * (dy - y * jnp.mean(dy * y, axis=-1, keepdims=True))
        if r_ref is not None:
            dx = r_ref[...] + dx
        dx_ref[...] = dx
        dxb_ref[...] = dx.astype(BF16)

        @pl.when(pl.program_id(0) == 0)
        def _():
            dg_ref[...] = jnp.zeros_like(dg_ref)

        dg_ref[...] += jnp.sum(dhv * y, axis=0, keepdims=True)

    row = pl.BlockSpec((tm, d), lambda i: (i, 0))
    vec = pl.BlockSpec((1, d), lambda i: (0, 0))
    args = (x, g.reshape(1, d), dh) + ((res,) if res is not None else ())
    dx, dxb, dg = pl.pallas_call(
        body, name=name, grid=(t // tm,), in_specs=[row, vec, row] + ([row] if res is not None else []),
        out_specs=(row, row, vec),
        out_shape=(jax.ShapeDtypeStruct((t, d), F32), jax.ShapeDtypeStruct((t, d), BF16),
                   jax.ShapeDtypeStruct((1, d), F32)),
        compiler_params=_params("arbitrary"),
    )(*args)
    return dx, dxb, dg.reshape(d)


FFN_PAIR = 4


def _ffn_up(h, wgu, fp, name):
    t, d = h.shape
    tm, tn = _tile(t, 512), FFN_PAIR * 2 * fp
    nj = wgu.shape[1] // tn

    def body(h_ref, w_ref, fac_ref, z_ref):
        ab = jnp.dot(h_ref[...], w_ref[...], preferred_element_type=F32)
        for e in range(FFN_PAIR):
            av, bv = ab[:, 2 * fp * e:2 * fp * e + fp], ab[:, 2 * fp * e + fp:2 * fp * (e + 1)]
            s = jax.nn.sigmoid(av)
            silu = av * s
            z_ref[:, fp * e:fp * (e + 1)] = (silu * bv).astype(z_ref.dtype)
            fac_ref[:, 2 * fp * e:2 * fp * e + fp] = silu.astype(fac_ref.dtype)
            fac_ref[:, 2 * fp * e + fp:2 * fp * (e + 1)] = (bv * (s + silu * (1.0 - s))).astype(fac_ref.dtype)

    return pl.pallas_call(
        body, name=name, grid=(nj, t // tm),
        in_specs=[pl.BlockSpec((tm, d), lambda j, i: (i, 0)), pl.BlockSpec((d, tn), lambda j, i: (0, j))],
        out_specs=(pl.BlockSpec((tm, tn), lambda j, i: (i, j)), pl.BlockSpec((tm, tn // 2), lambda j, i: (i, j))),
        out_shape=(jax.ShapeDtypeStruct((t, wgu.shape[1]), BF16), jax.ShapeDtypeStruct((t, wgu.shape[1] // 2), BF16)),
        compiler_params=_params("parallel", "parallel"),
    )(h, wgu)


def _resident(shape):
    return pl.BlockSpec(shape, lambda i: (0,) * len(shape), pipeline_mode=pl.Buffered(1))


def _ffn_bwd_rows(dyb, wd, fac, wgu, x, g, dy, fp, name, after=None):
    t, d = dyb.shape
    width = fac.shape[1]
    tm = _tile(t, 256)

    def body(dyb_ref, wd_ref, fac_ref, wgu_ref, x_ref, g_ref, dy_ref, *refs):
        dab_ref, dx_ref, dxb_ref, dg_ref = refs[-4:]
        dz = _dot_nt(dyb_ref[...], wd_ref[...]) * 0.5
        for e in range(width // (2 * fp)):
            dze = dz[:, fp * e:fp * (e + 1)]
            d_up = fac_ref[:, 2 * fp * e:2 * fp * e + fp].astype(F32)
            d_gate = fac_ref[:, 2 * fp * e + fp:2 * fp * (e + 1)].astype(F32)
            dab_ref[:, 2 * fp * e:2 * fp * e + fp] = (dze * d_gate).astype(dab_ref.dtype)
            dab_ref[:, 2 * fp * e + fp:2 * fp * (e + 1)] = (dze * d_up).astype(dab_ref.dtype)
        dh = _dot_nt(dab_ref[...], wgu_ref[...])
        xv = x_ref[...]
        r = lax.rsqrt(jnp.mean(xv * xv, axis=-1, keepdims=True) + EPS)
        y = xv * r
        dyn = dh * g_ref[...]
        dx = dy_ref[...] + r * (dyn - y * jnp.mean(dyn * y, axis=-1, keepdims=True))
        dx_ref[...] = dx
        dxb_ref[...] = dx.astype(BF16)

        @pl.when(pl.program_id(0) == 0)
        def _():
            dg_ref[...] = jnp.zeros_like(dg_ref)

        dg_ref[...] += jnp.sum(dh * y, axis=0, keepdims=True)

    row = pl.BlockSpec((tm, d), lambda i: (i, 0))
    wide = pl.BlockSpec((tm, width), lambda i: (i, 0))
    vec = pl.BlockSpec((1, d), lambda i: (0, 0))
    in_specs = [row, _resident(wd.shape), wide, _resident(wgu.shape), row, vec, row]
    args = [dyb, wd, fac, wgu, x, g.reshape(1, d), dy]
    if after is not None:
        in_specs.append(pl.BlockSpec(after.shape, lambda i: (0, 0)))
        args.append(after)
    dab, dx, dxb, dg = pl.pallas_call(
        body, name=name, grid=(t // tm,), in_specs=in_specs, out_specs=(wide, row, row, vec),
        out_shape=(jax.ShapeDtypeStruct((t, width), BF16), jax.ShapeDtypeStruct((t, d), F32),
                   jax.ShapeDtypeStruct((t, d), BF16), jax.ShapeDtypeStruct((1, d), F32)),
        compiler_params=_params("arbitrary"),
    )(*args)
    return dab, dx, dxb, dg.reshape(d)


def _matmul_rms_bwd(dab, wgu, x, g, dy, name, after=None):
    t, kdim = dab.shape
    d = wgu.shape[0]
    tm = _tile(t, 256)

    def body(a_ref, b_ref, x_ref, g_ref, dy_ref, *refs):
        dx_ref, dxb_ref, dg_ref = refs[-3:]
        dh = _dot_nt(a_ref[...].astype(BF16), b_ref[...])
        xv = x_ref[...]
        r = lax.rsqrt(jnp.mean(xv * xv, axis=-1, keepdims=True) + EPS)
        y = xv * r
        dyn = dh * g_ref[...]
        dx = dy_ref[...] + r * (dyn - y * jnp.mean(dyn * y, axis=-1, keepdims=True))
        dx_ref[...] = dx
        dxb_ref[...] = dx.astype(BF16)

        @pl.when(pl.program_id(0) == 0)
        def _():
            dg_ref[...] = jnp.zeros_like(dg_ref)

        dg_ref[...] += jnp.sum(dh * y, axis=0, keepdims=True)

    row = pl.BlockSpec((tm, d), lambda i: (i, 0))
    vec = pl.BlockSpec((1, d), lambda i: (0, 0))
    in_specs = [pl.BlockSpec((tm, kdim), lambda i: (i, 0)), pl.BlockSpec((d, kdim), lambda i: (0, 0)), row, vec, row]
    args = [dab, wgu, x, g.reshape(1, d), dy]
    if after is not None:
        in_specs.append(pl.BlockSpec(after.shape, lambda i: (0, 0)))
        args.append(after)
    dx, dxb, dg = pl.pallas_call(
        body, name=name, grid=(t // tm,), in_specs=in_specs, out_specs=(row, row, vec),
        out_shape=(jax.ShapeDtypeStruct((t, d), F32), jax.ShapeDtypeStruct((t, d), BF16),
                   jax.ShapeDtypeStruct((1, d), F32)),
        compiler_params=_params("arbitrary"),
    )(*args)
    return dx, dxb, dg.reshape(d)


def _lat_norm_fwd(p, g_q, g_kv, name):
    t = p.shape[0]
    tm = _row_tile(t)

    def body(p_ref, gq_ref, gkv_ref, q_ref, kv_ref):
        for lo, hi, g_ref, o_ref in ((0, Q_LORA, gq_ref, q_ref), (Q_LORA, LAT, gkv_ref, kv_ref)):
            xv = p_ref[:, lo:hi]
            r = lax.rsqrt(jnp.mean(xv * xv, axis=-1, keepdims=True) + EPS)
            o_ref[...] = (xv * r * g_ref[...]).astype(o_ref.dtype)

    return pl.pallas_call(
        body, name=name, grid=(t // tm,),
        in_specs=[pl.BlockSpec((tm, P_COLS), lambda i: (i, 0)), pl.BlockSpec((1, Q_LORA), lambda i: (0, 0)),
                  pl.BlockSpec((1, KV_LORA), lambda i: (0, 0))],
        out_specs=(pl.BlockSpec((tm, Q_LORA), lambda i: (i, 0)), pl.BlockSpec((tm, KV_LORA), lambda i: (i, 0))),
        out_shape=(jax.ShapeDtypeStruct((t, Q_LORA), BF16), jax.ShapeDtypeStruct((t, KV_LORA), BF16)),
        compiler_params=_params("parallel"),
    )(p, g_q.reshape(1, Q_LORA), g_kv.reshape(1, KV_LORA))


def _lat_norm_bwd(p, g_q, g_kv, dq, dkv, name):
    t = p.shape[0]
    tm = _row_tile(t)

    def body(p_ref, gq_ref, gkv_ref, dq_ref, dkv_ref, dp_ref, dgq_ref, dgkv_ref):
        first = pl.program_id(0) == 0
        for lo, hi, g_ref, d_ref, dg_ref in ((0, Q_LORA, gq_ref, dq_ref, dgq_ref),
                                             (Q_LORA, LAT, gkv_ref, dkv_ref, dgkv_ref)):
            xv, dhv = p_ref[:, lo:hi], d_ref[...]
            r = lax.rsqrt(jnp.mean(xv * xv, axis=-1, keepdims=True) + EPS)
            y = xv * r
            dy = dhv * g_ref[...]
            dp_ref[:, lo:hi] = r * (dy - y * jnp.mean(dy * y, axis=-1, keepdims=True))

            @pl.when(first)
            def _():
                dg_ref[...] = jnp.zeros_like(dg_ref)

            dg_ref[...] += jnp.sum(dhv * y, axis=0, keepdims=True)

    vq = pl.BlockSpec((1, Q_LORA), lambda i: (0, 0))
    vkv = pl.BlockSpec((1, KV_LORA), lambda i: (0, 0))
    dp, dgq, dgkv = pl.pallas_call(
        body, name=name, grid=(t // tm,),
        in_specs=[pl.BlockSpec((tm, P_COLS), lambda i: (i, 0)), vq, vkv,
                  pl.BlockSpec((tm, Q_LORA), lambda i: (i, 0)), pl.BlockSpec((tm, KV_LORA), lambda i: (i, 0))],
        out_specs=(pl.BlockSpec((tm, LAT), lambda i: (i, 0)), vq, vkv),
        out_shape=(jax.ShapeDtypeStruct((t, LAT), F32), jax.ShapeDtypeStruct((1, Q_LORA), F32),
                   jax.ShapeDtypeStruct((1, KV_LORA), F32)),
        compiler_params=_params("arbitrary"),
    )(p, g_q.reshape(1, Q_LORA), g_kv.reshape(1, KV_LORA), dq, dkv)
    return dp, dgq.reshape(Q_LORA), dgkv.reshape(KV_LORA)


def _rope_tables(t):
    half = QK_ROPE // 2
    pos = jnp.arange(t, dtype=F32)
    inv_freq = 1.0 / (ROPE_THETA ** (jnp.arange(0, QK_ROPE, 2, dtype=F32) / QK_ROPE))
    ang = pos[:, None] * inv_freq[None, :]
    cos, sin = jnp.cos(ang), jnp.sin(ang)
    z = lambda n: jnp.zeros((t, n), F32)
    c_tab = jnp.concatenate([jnp.ones((t, QK_NOPE), F32), cos, cos, z(HEAD_PAD - QK_DIM)], axis=1)
    sa_tab = jnp.concatenate([z(QK_NOPE), -sin, z(half), z(HEAD_PAD - QK_DIM)], axis=1)
    sb_tab = jnp.concatenate([z(QK_NOPE), z(half), sin, z(HEAD_PAD - QK_DIM)], axis=1)
    return c_tab, sa_tab, sb_tab


def _rope(x, c, sa, sb):
    half = QK_ROPE // 2
    return x * c + pltpu.roll(x, HEAD_PAD - half, 1) * sa + pltpu.roll(x, half, 1) * sb


def _rope_t(d, c, sa, sb):
    half = QK_ROPE // 2
    return d * c + pltpu.roll(d * sa, half, 1) + pltpu.roll(d * sb, HEAD_PAD - half, 1)


def _head_rms(x):
    r = lax.rsqrt(jnp.sum(x * x, axis=-1, keepdims=True) * (1.0 / QK_DIM) + EPS)
    return x * r, r


def _qk_prep_fwd(q_raw, k_raw, p, gq, gk, tabs, name):
    t, width = q_raw.shape
    tm = _row_tile(t)

    def body(q_ref, k_ref, p_ref, gq_ref, gk_ref, c_ref, sa_ref, sb_ref, qo_ref, ko_ref):
        c, sa, sb, kpe = c_ref[...], sa_ref[...], sb_ref[...], p_ref[...]
        for h in range(N_HEADS):
            cols = slice(h * HEAD_PAD, (h + 1) * HEAD_PAD)
            qn, _ = _head_rms(q_ref[:, cols])
            qo_ref[:, cols] = _rope(qn * gq_ref[...], c, sa, sb).astype(qo_ref.dtype)
            kn, _ = _head_rms(k_ref[:, cols] + kpe)
            ko_ref[:, cols] = _rope(kn * gk_ref[...], c, sa, sb).astype(ko_ref.dtype)

    rows = pl.BlockSpec((tm, width), lambda i: (i, 0))
    tab = pl.BlockSpec((tm, HEAD_PAD), lambda i: (i, 0))
    vec = pl.BlockSpec((1, HEAD_PAD), lambda i: (0, 0))
    kpe_spec = pl.BlockSpec((tm, HEAD_PAD), lambda i: (i, P_KPE // HEAD_PAD))
    return pl.pallas_call(
        body, name=name, grid=(t // tm,), in_specs=[rows, rows, kpe_spec, vec, vec, tab, tab, tab],
        out_specs=(rows, rows),
        out_shape=(jax.ShapeDtypeStruct(q_raw.shape, BF16), jax.ShapeDtypeStruct(k_raw.shape, BF16)),
        compiler_params=_params("parallel"),
    )(q_raw, k_raw, p, gq.reshape(1, HEAD_PAD), gk.reshape(1, HEAD_PAD), *tabs)


def _qk_prep_bwd(q_raw, k_raw, p, dq, dk, gq, gk, tabs, name):
    t, width = q_raw.shape
    tm = _row_tile(t, 256)

    def body(q_ref, k_ref, p_ref, dq_ref, dk_ref, gq_ref, gk_ref, c_ref, sa_ref, sb_ref,
             dqr_ref, dkr_ref, dkpe_ref, dgq_ref, dgk_ref):
        c, sa, sb, kpe = c_ref[...], sa_ref[...], sb_ref[...], p_ref[...]

        def one(x, d, g_ref):
            n, r = _head_rms(x)
            dng = _rope_t(d, c, sa, sb)
            dn = dng * g_ref[...]
            dx = r * (dn - n * (jnp.sum(dn * n, axis=-1, keepdims=True) * (1.0 / QK_DIM)))
            return dx, jnp.sum(dng * n, axis=0, keepdims=True)

        dgq = dgk = dkpe = None
        for h in range(N_HEADS):
            cols = slice(h * HEAD_PAD, (h + 1) * HEAD_PAD)
            dqr, gq_part = one(q_ref[:, cols], dq_ref[:, cols], gq_ref)
            dkr, gk_part = one(k_ref[:, cols] + kpe, dk_ref[:, cols], gk_ref)
            dqr_ref[:, cols] = dqr
            dkr_ref[:, cols] = dkr
            dgq = gq_part if dgq is None else dgq + gq_part
            dgk = gk_part if dgk is None else dgk + gk_part
            dkpe = dkr if dkpe is None else dkpe + dkr
        dkpe_ref[...] = dkpe

        @pl.when(pl.program_id(0) == 0)
        def _():
            dgq_ref[...] = jnp.zeros_like(dgq_ref)
            dgk_ref[...] = jnp.zeros_like(dgk_ref)

        dgq_ref[...] += dgq
        dgk_ref[...] += dgk

    head = pl.BlockSpec((tm, width), lambda i: (i, 0))
    tab = pl.BlockSpec((tm, HEAD_PAD), lambda i: (i, 0))
    vec = pl.BlockSpec((1, HEAD_PAD), lambda i: (0, 0))
    kpe = pl.BlockSpec((tm, HEAD_PAD), lambda i: (i, P_KPE // HEAD_PAD))
    dqr, dkr, dkpe, dgq, dgk = pl.pallas_call(
        body, name=name, grid=(t // tm,), in_specs=[head, head, kpe, head, head, vec, vec, tab, tab, tab],
        out_specs=(head, head, tab, vec, vec),
        out_shape=(jax.ShapeDtypeStruct(q_raw.shape, F32), jax.ShapeDtypeStruct(k_raw.shape, F32),
                   jax.ShapeDtypeStruct((t, HEAD_PAD), F32), jax.ShapeDtypeStruct((1, HEAD_PAD), F32),
                   jax.ShapeDtypeStruct((1, HEAD_PAD), F32)),
        compiler_params=_params("arbitrary"),
    )(q_raw, k_raw, p, dq, dk, gq.reshape(1, HEAD_PAD), gk.reshape(1, HEAD_PAD), *tabs)
    return dqr, dkr, dkpe, dgq.reshape(HEAD_PAD), dgk.reshape(HEAD_PAD)


def _dot_nt(a, b):
    return lax.dot_general(a, b, (((1,), (1,)), ((), ())), preferred_element_type=F32)


def _dot_tn(a, b):
    return lax.dot_general(a, b, (((0,), (0,)), ((), ())), preferred_element_type=F32)


def _diag_mask():
    rows = lax.broadcasted_iota(jnp.int32, (ATTN_BLOCK, ATTN_BLOCK), 0) // CHUNK
    cols = lax.broadcasted_iota(jnp.int32, (ATTN_BLOCK, ATTN_BLOCK), 1) // CHUNK
    return cols <= rows


def _attn_fwd(q, k, v, name):
    t = q.shape[0]
    bq = ATTN_BLOCK
    nq = t // bq

    width = ATTN_HEADS * HEAD_PAD

    def body(q_ref, k_ref, v_ref, o_ref, lse_ref):
        i = pl.program_id(1)
        heads = [slice(e * HEAD_PAD, (e + 1) * HEAD_PAD) for e in range(ATTN_HEADS)]
        qv = [q_ref[:, cols] for cols in heads]

        def block(j, carries, masked):
            rows = pl.ds(pl.multiple_of(j * bq, bq), bq)
            out = []
            for e, (m, l, acc) in enumerate(carries):
                s = _dot_nt(qv[e], k_ref[rows, heads[e]]) * ATTN_SCALE_LOG2
                if masked:
                    s = jnp.where(_diag_mask(), s, -1e30)
                m_new = jnp.maximum(m, jnp.max(s, axis=-1, keepdims=True))
                alpha = jnp.exp2(m - m_new)
                pe = jnp.exp2(s - m_new)
                l = alpha * l + jnp.sum(pe, axis=-1, keepdims=True)
                acc = alpha * acc + jnp.dot(pe.astype(BF16), v_ref[rows, heads[e]], preferred_element_type=F32)
                out.append((m_new, l, acc))
            return tuple(out)

        init = tuple((jnp.full((bq, 1), -1e30, F32), jnp.zeros((bq, 1), F32), jnp.zeros((bq, HEAD_PAD), F32))
                     for _ in heads)
        carries = lax.fori_loop(0, i, lambda j, cr: block(j, cr, False), init)
        for cols, (m, l, acc) in zip(heads, block(i, carries, True)):
            o_ref[:, cols] = acc / l
            lse_ref[:, cols] = jnp.broadcast_to(m + jnp.log2(l), (bq, HEAD_PAD))

    blk = pl.BlockSpec((bq, width), lambda h, i: (i, h))
    full = pl.BlockSpec((t, width), lambda h, i: (0, h))
    return pl.pallas_call(
        body, name=name, grid=(N_HEADS // ATTN_HEADS, nq), in_specs=[blk, full, full], out_specs=(blk, blk),
        out_shape=(jax.ShapeDtypeStruct(q.shape, F32), jax.ShapeDtypeStruct(q.shape, F32)),
        compiler_params=_params("parallel", "parallel"),
    )(q, k, v)


def _attn_bwd(q, k, v, o, lse, do, name):
    t = q.shape[0]
    bq = ATTN_BLOCK
    nq = t // bq

    def body(q_ref, k_ref, v_ref, o_ref, lse_ref, do_ref, dq_ref, dk_ref, dv_ref, delta_ref):
        def rows_of(i):
            return pl.ds(pl.multiple_of(i * bq, bq), bq)

        def prep(i, _):
            r = rows_of(i)
            delta_ref[r, :] = jnp.broadcast_to(jnp.sum(do_ref[r, :] * o_ref[r, :], axis=-1, keepdims=True),
                                               (bq, HEAD_PAD))
            dq_ref[r, :] = jnp.zeros((bq, HEAD_PAD), F32)
            return 0

        lax.fori_loop(0, nq, prep, 0)

        def key_block(j, _):
            rj = rows_of(j)
            kb, vb = k_ref[rj, :], v_ref[rj, :]

            def query_block(i, carry, masked):
                dk, dv = carry
                ri = rows_of(i)
                qb, dob = q_ref[ri, :], do_ref[ri, :].astype(BF16)
                s = _dot_nt(qb, kb) * ATTN_SCALE_LOG2
                if masked:
                    s = jnp.where(_diag_mask(), s, -1e30)
                pe = jnp.exp2(s - lse_ref[ri, :][:, :1])
                dp = _dot_nt(dob, vb)
                ds = (pe * (dp - delta_ref[ri, :][:, :1]) * ATTN_SCALE).astype(BF16)
                dq_ref[ri, :] += jnp.dot(ds, kb, preferred_element_type=F32)
                return dk + _dot_tn(ds, qb), dv + _dot_tn(pe.astype(BF16), dob)

            zero = jnp.zeros((bq, HEAD_PAD), F32)
            carry = query_block(j, (zero, zero), True)
            dk, dv = lax.fori_loop(j + 1, nq, lambda i, cr: query_block(i, cr, False), carry)
            dk_ref[rj, :] = dk
            dv_ref[rj, :] = dv
            return 0

        lax.fori_loop(0, nq, key_block, 0)

    full = pl.BlockSpec((t, HEAD_PAD), lambda h: (0, h))
    shp = jax.ShapeDtypeStruct(q.shape, F32)
    return pl.pallas_call(
        body, name=name, grid=(N_HEADS,), in_specs=[full] * 6, out_specs=(full, full, full),
        out_shape=(shp, shp, shp), scratch_shapes=[pltpu.VMEM((t, HEAD_PAD), F32)],
        compiler_params=_params("parallel"),
    )(q, k, v, o, lse, do)


def _glu_ext(pc_ref, pp_ref, u_ref, tm, first):
    u_ref[CONV_HALO:CONV_HALO + tm, :] = pc_ref[:, P_A:P_G] * jax.nn.sigmoid(pc_ref[:, P_G:P_COLS])
    up = pp_ref[tm - CONV_HALO:tm, P_A:P_G] * jax.nn.sigmoid(pp_ref[tm - CONV_HALO:tm, P_G:P_COLS])
    u_ref[0:CONV_HALO, :] = jnp.where(first, 0.0, up)


SUBLANES = 8


def _shift_copies(src_ref, sh_ref):
    rows = sh_ref.shape[1]
    for b in range(1, SUBLANES):
        sh_ref[b - 1, :, :] = src_ref[b:b + rows, :]


def _rows_at(src_ref, sh_ref, start, n):
    a, b = divmod(start, SUBLANES)
    if b == 0:
        return src_ref[SUBLANES * a:SUBLANES * a + n, :]
    return sh_ref[b - 1, SUBLANES * a:SUBLANES * a + n, :]


def _conv_fwd(p, w, b, ln_g, ln_b, name):
    t = p.shape[0]
    tm = _row_tile(t, CONV_TILE)
    off = CONV_HALO - (CONV_K - 1)

    def body(pc_ref, pp_ref, w_ref, b_ref, g_ref, bb_ref, y_ref, o_ref, u_ref, ush_ref):
        _glu_ext(pc_ref, pp_ref, u_ref, tm, pl.program_id(0) == 0)
        _shift_copies(u_ref, ush_ref)
        acc = jnp.zeros((tm, CONV_W), F32)
        for kk in range(CONV_K):
            acc = acc + w_ref[kk:kk + 1, :] * _rows_at(u_ref, ush_ref, off + kk, tm)
        y = acc + b_ref[...]
        y_ref[...] = y
        xc = y - jnp.mean(y, axis=-1, keepdims=True)
        lo = xc * lax.rsqrt(jnp.mean(xc * xc, axis=-1, keepdims=True) + EPS) * g_ref[...] + bb_ref[...]
        o_ref[...] = (lo * jax.nn.sigmoid(lo)).astype(o_ref.dtype)

    prow = pl.BlockSpec((tm, P_COLS), lambda i: (i, 0))
    pprev = pl.BlockSpec((tm, P_COLS), lambda i: (jnp.maximum(i - 1, 0), 0))
    vec = pl.BlockSpec((1, CONV_W), lambda i: (0, 0))
    row = pl.BlockSpec((tm, CONV_W), lambda i: (i, 0))
    return pl.pallas_call(
        body, name=name, grid=(t // tm,),
        in_specs=[prow, pprev, pl.BlockSpec((CONV_HALO, CONV_W), lambda i: (0, 0)), vec, vec, vec],
        out_specs=(row, row),
        out_shape=(jax.ShapeDtypeStruct((t, CONV_W), F32), jax.ShapeDtypeStruct((t, CONV_W), BF16)),
        scratch_shapes=[pltpu.VMEM((tm + CONV_HALO, CONV_W), F32),
                        pltpu.VMEM((SUBLANES - 1, tm + CONV_HALO - SUBLANES, CONV_W), F32)],
        compiler_params=_params("parallel"),
    )(p, p, w, b.reshape(1, CONV_W), ln_g.reshape(1, CONV_W), ln_b.reshape(1, CONV_W))


def _conv_bwd_ln(y, dout, ln_g, ln_b, name):
    t = y.shape[0]
    tm = _row_tile(t)

    def body(y_ref, d_ref, g_ref, bb_ref, dy_ref, dg_ref, db_ref, dcb_ref):
        yv = y_ref[...]
        xc = yv - jnp.mean(yv, axis=-1, keepdims=True)
        r = lax.rsqrt(jnp.mean(xc * xc, axis=-1, keepdims=True) + EPS)
        n = xc * r
        lo = n * g_ref[...] + bb_ref[...]
        s = jax.nn.sigmoid(lo)
        dlo = d_ref[...] * (s * (1.0 + lo * (1.0 - s)))
        dn = dlo * g_ref[...]
        dy = r * (dn - jnp.mean(dn, axis=-1, keepdims=True) - n * jnp.mean(dn * n, axis=-1, keepdims=True))
        dy_ref[...] = dy

        @pl.when(pl.program_id(0) == 0)
        def _():
            dg_ref[...] = jnp.zeros_like(dg_ref)
            db_ref[...] = jnp.zeros_like(db_ref)
            dcb_ref[...] = jnp.zeros_like(dcb_ref)

        dg_ref[...] += jnp.sum(dlo * n, axis=0, keepdims=True)
        db_ref[...] += jnp.sum(dlo, axis=0, keepdims=True)
        dcb_ref[...] += jnp.sum(dy, axis=0, keepdims=True)

    row = pl.BlockSpec((tm, CONV_W), lambda i: (i, 0))
    vec = pl.BlockSpec((1, CONV_W), lambda i: (0, 0))
    vshape = jax.ShapeDtypeStruct((1, CONV_W), F32)
    dy, dg, db, dcb = pl.pallas_call(
        body, name=name, grid=(t // tm,), in_specs=[row, row, vec, vec], out_specs=(row, vec, vec, vec),
        out_shape=(jax.ShapeDtypeStruct((t, CONV_W), F32), vshape, vshape, vshape),
        compiler_params=_params("arbitrary"),
    )(y, dout, ln_g.reshape(1, CONV_W), ln_b.reshape(1, CONV_W))
    return dy, dg.reshape(CONV_W), db.reshape(CONV_W), dcb.reshape(CONV_W)


def _conv_bwd_taps(p, dy, w, name):
    t = p.shape[0]
    tm = _row_tile(t, CONV_TILE)
    nt = t // tm
    off = CONV_HALO - (CONV_K - 1)

    def body(pc_ref, pp_ref, dyc_ref, dyn_ref, w_ref, dag_ref, dw_ref, u_ref, dye_ref, ush_ref, dysh_ref):
        i = pl.program_id(0)
        _glu_ext(pc_ref, pp_ref, u_ref, tm, i == 0)
        dyc = dyc_ref[...]
        dye_ref[0:tm, :] = dyc
        dye_ref[tm:tm + CONV_HALO, :] = jnp.where(i == nt - 1, 0.0, dyn_ref[0:CONV_HALO, :])

        _shift_copies(u_ref, ush_ref)
        _shift_copies(dye_ref, dysh_ref)

        @pl.when(i == 0)
        def _():
            dw_ref[...] = jnp.zeros_like(dw_ref)

        du = jnp.zeros((tm, CONV_W), F32)
        for kk in range(CONV_K):
            dw_ref[kk:kk + 1, :] += jnp.sum(dyc * _rows_at(u_ref, ush_ref, off + kk, tm), axis=0, keepdims=True)
            du = du + w_ref[kk:kk + 1, :] * _rows_at(dye_ref, dysh_ref, CONV_K - 1 - kk, tm)
        av, gv = pc_ref[:, P_A:P_G], pc_ref[:, P_G:P_COLS]
        s = jax.nn.sigmoid(gv)
        dag_ref[:, 0:CONV_W] = du * s
        dag_ref[:, CONV_W:2 * CONV_W] = du * av * (s * (1.0 - s))

    prow = pl.BlockSpec((tm, P_COLS), lambda i: (i, 0))
    pprev = pl.BlockSpec((tm, P_COLS), lambda i: (jnp.maximum(i - 1, 0), 0))
    row = pl.BlockSpec((tm, CONV_W), lambda i: (i, 0))
    nxt = pl.BlockSpec((tm, CONV_W), lambda i: (jnp.minimum(i + 1, nt - 1), 0))
    wspec = pl.BlockSpec((CONV_HALO, CONV_W), lambda i: (0, 0))
    return pl.pallas_call(
        body, name=name, grid=(nt,), in_specs=[prow, pprev, row, nxt, wspec],
        out_specs=(pl.BlockSpec((tm, 2 * CONV_W), lambda i: (i, 0)), wspec),
        out_shape=(jax.ShapeDtypeStruct((t, 2 * CONV_W), F32), jax.ShapeDtypeStruct((CONV_HALO, CONV_W), F32)),
        scratch_shapes=[pltpu.VMEM((tm + CONV_HALO, CONV_W), F32), pltpu.VMEM((tm + CONV_HALO, CONV_W), F32),
                        pltpu.VMEM((SUBLANES - 1, tm + CONV_HALO - SUBLANES, CONV_W), F32),
                        pltpu.VMEM((SUBLANES - 1, tm + CONV_HALO - SUBLANES, CONV_W), F32)],
        compiler_params=_params("arbitrary"),
    )(p, p, dy, dy, w)


def _post_norm_loss(x, g, target, name):
    t, d = x.shape
    tm = _row_tile(t)

    def body(x_ref, g_ref, t_ref, l_ref, dx_ref, dxb_ref, dg_ref):
        xv = x_ref[...]
        r = lax.rsqrt(jnp.mean(xv * xv, axis=-1, keepdims=True) + EPS)
        n = xv * r
        err = n * g_ref[...] - t_ref[...]
        dy = err * (1.0 / d)
        dn = dy * g_ref[...]
        dx = r * (dn - n * jnp.mean(dn * n, axis=-1, keepdims=True))
        dx_ref[...] = dx
        dxb_ref[...] = dx.astype(BF16)

        @pl.when(pl.program_id(0) == 0)
        def _():
            l_ref[...] = jnp.zeros_like(l_ref)
            dg_ref[...] = jnp.zeros_like(dg_ref)

        row = jnp.sum(err * err, axis=-1, keepdims=True) * (0.5 / d)
        l_ref[...] += jnp.broadcast_to(jnp.sum(row, axis=0, keepdims=True), (1, LANE))
        dg_ref[...] += jnp.sum(dy * n, axis=0, keepdims=True)

    row = pl.BlockSpec((tm, d), lambda i: (i, 0))
    vec = pl.BlockSpec((1, d), lambda i: (0, 0))
    loss, dx, dxb, dg = pl.pallas_call(
        body, name=name, grid=(t // tm,), in_specs=[row, vec, row],
        out_specs=(pl.BlockSpec((1, LANE), lambda i: (0, 0)), row, row, vec),
        out_shape=(jax.ShapeDtypeStruct((1, LANE), F32), jax.ShapeDtypeStruct((t, d), F32),
                   jax.ShapeDtypeStruct((t, d), BF16), jax.ShapeDtypeStruct((1, d), F32)),
        compiler_params=_params("arbitrary"),
    )(x, g.reshape(1, d), target)
    return loss, dx, dxb, dg.reshape(d)


def _adamw(w, g, m, v, name, after=None):
    r, c = w.shape
    tr = _row_tile(r, 256)
    c1, c2 = 1.0 - ADAM_B1 ** ADAM_STEP, 1.0 - ADAM_B2 ** ADAM_STEP

    def body(w_ref, g_ref, m_ref, v_ref, *refs):
        d_ref, mo_ref, vo_ref = refs[-3:]
        gv = g_ref[...]
        mn = ADAM_B1 * m_ref[...] + (1.0 - ADAM_B1) * gv
        vn = ADAM_B2 * v_ref[...] + (1.0 - ADAM_B2) * (gv * gv)
        mo_ref[...] = mn
        vo_ref[...] = vn
        d_ref[...] = -ADAM_LR * ((mn / c1) / (jnp.sqrt(vn / c2) + ADAM_EPS) + ADAM_WD * w_ref[...])

    blk = pl.BlockSpec((tr, c), lambda i: (i, 0))
    shp = jax.ShapeDtypeStruct((r, c), F32)
    extra = [] if after is None else [pl.BlockSpec(after.shape, lambda i: (0, 0))]
    return pl.pallas_call(
        body, name=name, grid=(r // tr,), in_specs=[blk] * 4 + extra, out_specs=(blk, blk, blk),
        out_shape=(shp, shp, shp), compiler_params=_params("parallel"),
    )(w, g, m, v, *([] if after is None else [after]))


def _sum_parts(parts, name):
    r, c = parts[0].shape
    tr = _row_tile(r, 256)

    def body(*refs):
        acc = refs[0][...]
        for ref in refs[1:-1]:
            acc = acc + ref[...]
        refs[-1][...] = acc

    blk = pl.BlockSpec((tr, c), lambda i: (i, 0))
    return pl.pallas_call(
        body, name=name, grid=(r // tr,), in_specs=[blk] * len(parts), out_specs=blk,
        out_shape=jax.ShapeDtypeStruct((r, c), F32), compiler_params=_params("parallel"),
    )(*parts)


def _place():
    return lax.axis_index("x"), lax.axis_index("y"), lax.axis_index("c")


def _window(ref, block, size, axis):
    start = pl.multiple_of(block * size, LANE if size % LANE == 0 else 8)
    return ref.at[(slice(None),) * axis + (pl.ds(start, size),)]


def _all_gather(pieces, name, in_vmem=False):
    n_p = n_all = len(pieces)

    def body(*refs):
        x_refs, out_refs = refs[:n_all], refs[n_all:2 * n_all]
        send_sems, recv_sems, local_sems = refs[2 * n_all:]
        px, py, pc = _place()
        me, sibling = (px, py, pc), (px, py, 1 - pc)
        chips = [(1 - px, py), (px, 1 - py), (1 - px, 1 - py)]

        def win(p, block):
            bx, by, bc = block
            x, axis = pieces[p]
            return _window(out_refs[p], 4 * bx + 2 * by + bc, x.shape[axis], axis)

        def copy(k, p, block, to, local=False):
            return pltpu.make_async_remote_copy(
                src_ref=x_refs[p] if local else win(p, block), dst_ref=win(p, block),
                send_sem=send_sems.at[k, p], recv_sem=recv_sems.at[k, p], device_id=to, device_id_type=MESH_ID)

        every = range(n_p)
        mine = [pltpu.make_async_copy(x_refs[p], win(p, me), local_sems.at[p]) for p in range(n_all)]
        first = [copy(0, p, me, sibling, local=True) for p in every]
        first += [copy(1 + j, p, me, (*chip, pc), local=True) for j, chip in enumerate(chips) for p in every]
        for cp in mine + first:
            cp.start()
        passed = []
        for j, chip in enumerate(chips):
            for p in every:
                copy(1 + j, p, (*chip, pc), me).wait_recv()
                passed.append(copy(4 + j, p, (*chip, pc), sibling))
                passed[-1].start()
        for p in every:
            copy(0, p, sibling, me).wait_recv()
        for j, chip in enumerate(chips):
            for p in every:
                copy(4 + j, p, (*chip, 1 - pc), me).wait_recv()
        for cp in first + passed:
            cp.wait_send()
        for cp in mine:
            cp.wait()

    def gathered(x, axis):
        return jax.ShapeDtypeStruct(x.shape[:axis] + (N_DEV * x.shape[axis],) + x.shape[axis + 1:], x.dtype)

    spec = VMEM_SPEC if in_vmem else ANY
    return pl.pallas_call(
        body, name=name, in_specs=[spec] * n_all, out_specs=[spec] * n_all,
        out_shape=[gathered(*pc_) for pc_ in pieces],
        scratch_shapes=[pltpu.SemaphoreType.DMA((7, n_p)), pltpu.SemaphoreType.DMA((7, n_p)),
                        pltpu.SemaphoreType.DMA((n_all,))],
        compiler_params=pltpu.CompilerParams(vmem_limit_bytes=VMEM_LIMIT),
    )(*[x for x, _ in pieces])


def _start_copies(bufs, n_copies, plan, name, after=None):
    nb = len(bufs)
    n_in = nb + (after is not None)

    def body(*refs):
        send_sems, recv_sems, token = refs[n_in], refs[n_in + 1], refs[-1]
        for i, (src, dst, dev) in enumerate(plan(refs[:nb])):
            pltpu.make_async_remote_copy(src_ref=src, dst_ref=dst, send_sem=send_sems.at[i], recv_sem=recv_sems.at[i],
                                         device_id=dev, device_id_type=MESH_ID).start()
        token[...] = jnp.zeros_like(token)

    out = pl.pallas_call(
        body, name=name, in_specs=[HBM_SPEC] * nb + [ANY] * (after is not None),
        out_shape=(pltpu.SemaphoreType.DMA((n_copies,)), pltpu.SemaphoreType.DMA((n_copies,)),
                   *[pltpu.HBM(b.shape, b.dtype) for b in bufs], jax.ShapeDtypeStruct((8, LANE), F32)),
        out_specs=(SEM_SPEC, SEM_SPEC, *[HBM_SPEC] * nb, VMEM_SPEC),
        input_output_aliases={i: 2 + i for i in range(nb)},
        compiler_params=pltpu.CompilerParams(has_side_effects=DATAFLOW),
    )(*[pltpu.with_memory_space_constraint(b, pltpu.HBM) for b in bufs], *([after] if after is not None else []))
    return out[0], out[1], list(out[2:2 + nb]), out[-1]


def _wait_copies(started, after, n_copies, plan, name):
    send_sems, recv_sems, bufs, _ = started
    nb = len(bufs)

    def body(*refs):
        send_ref, recv_ref = refs[nb], refs[nb + 1]
        copies = [pltpu.make_async_remote_copy(src_ref=src, dst_ref=dst, send_sem=send_ref.at[i], recv_sem=recv_ref.at[i],
                                               device_id=dev, device_id_type=MESH_ID)
                  for i, (src, dst, dev) in enumerate(plan(refs[:nb]))]
        for cp in copies:
            cp.wait_send()
        for cp in copies:
            cp.wait_recv()

    out = pl.pallas_call(
        body, name=name, in_specs=[HBM_SPEC] * nb + [SEM_SPEC, SEM_SPEC, ANY],
        out_shape=tuple(pltpu.HBM(b.shape, b.dtype) for b in bufs), out_specs=tuple([HBM_SPEC] * nb),
        input_output_aliases={i: i for i in range(nb)},
        compiler_params=pltpu.CompilerParams(has_side_effects=DATAFLOW),
    )(*bufs, send_sems, recv_sems, after)
    return list(out)


def _after(x, token):
    return x + token[0, 0].astype(x.dtype)


def _other_chips():
    px, py, _ = _place()
    return [(1 - px, py), (px, 1 - py), (1 - px, 1 - py)]


def _exchange(srcs, slots, src_block, target, name):
    n_p = len(srcs)

    def body(*refs):
        src_refs, out_refs, send_sems, recv_sems = refs[:n_p], refs[n_p:2 * n_p], refs[-2], refs[-1]
        copies = [pltpu.make_async_remote_copy(
            src_ref=src_refs[p].at[src_block(s)], dst_ref=out_refs[p].at[s], send_sem=send_sems.at[s, p],
            recv_sem=recv_sems.at[s, p], device_id=target(s), device_id_type=MESH_ID)
            for s in range(slots) for p in range(n_p)]
        for cp in copies:
            cp.start()
        for cp in copies:
            cp.wait_recv()
        for cp in copies:
            cp.wait_send()

    return pl.pallas_call(
        body, name=name, in_specs=[ANY] * n_p, out_specs=[ANY] * n_p,
        out_shape=[jax.ShapeDtypeStruct((slots,) + a.shape[1:], a.dtype) for a in srcs],
        scratch_shapes=[pltpu.SemaphoreType.DMA((slots, n_p)), pltpu.SemaphoreType.DMA((slots, n_p))],
        compiler_params=pltpu.CompilerParams(vmem_limit_bytes=VMEM_LIMIT),
    )(*srcs)


def _blocks_to_sibling(sends, name):
    def src_block(j):
        return 2 * j + 1 - lax.axis_index("c")

    def target(j):
        px, py, pc = _place()
        return (px, py, 1 - pc)

    return _exchange(sends, 4, src_block, target, name)


def _pair_sums_for_chips(own, got, name):
    _, r, c = own.shape
    tr = _row_tile(r, 256, 16)

    def body(idx_ref, own_ref, got_ref, o_ref):
        o_ref[...] = (own_ref[...] + got_ref[...].astype(F32)).astype(o_ref.dtype)

    grid_spec = pltpu.PrefetchScalarGridSpec(
        num_scalar_prefetch=1, grid=(3, r // tr),
        in_specs=[pl.BlockSpec((None, tr, c), lambda k, i, idx: (idx[k], i, 0)),
                  pl.BlockSpec((None, tr, c), lambda k, i, idx: (idx[3 + k], i, 0))],
        out_specs=pl.BlockSpec((None, tr, c), lambda k, i, idx: (k, i, 0)))
    chips = [2 * cx + cy for cx, cy in _other_chips()]
    idx = jnp.stack([2 * j + lax.axis_index("c") for j in chips] + chips).astype(jnp.int32)
    return pl.pallas_call(
        body, name=name, grid_spec=grid_spec, out_shape=jax.ShapeDtypeStruct((3, r, c), BF16),
        compiler_params=_params("parallel", "parallel"),
    )(idx, own, got)


def _sum_for_me(own, got_sibling, got_chips, name):
    _, r, c = own.shape
    tr = _row_tile(r, 256, 16)

    def body(idx_ref, own_ref, sib_ref, g0_ref, g1_ref, g2_ref, o_ref):
        acc = own_ref[...] + sib_ref[...].astype(F32)
        for ref in (g0_ref, g1_ref, g2_ref):
            acc = acc + ref[...].astype(F32)
        o_ref[...] = acc

    def part(k):
        return pl.BlockSpec((None, tr, c), lambda i, idx: (k, i, 0))

    grid_spec = pltpu.PrefetchScalarGridSpec(
        num_scalar_prefetch=1, grid=(r // tr,),
        in_specs=[pl.BlockSpec((None, tr, c), lambda i, idx: (idx[0], i, 0)),
                  pl.BlockSpec((None, tr, c), lambda i, idx: (idx[1], i, 0)), part(0), part(1), part(2)],
        out_specs=pl.BlockSpec((tr, c), lambda i, idx: (i, 0)))
    px, py, pc = _place()
    idx = jnp.stack([4 * px + 2 * py + pc, 2 * px + py]).astype(jnp.int32)
    return pl.pallas_call(
        body, name=name, grid_spec=grid_spec, out_shape=jax.ShapeDtypeStruct((r, c), F32),
        compiler_params=_params("parallel"),
    )(idx, own, got_sibling, got_chips, got_chips, got_chips)


def _chip_plan(n_p):
    def plan(refs):
        pc = lax.axis_index("c")
        return [(refs[p].at[k], refs[n_p + p].at[k], (cx, cy, pc))
                for p in range(n_p) for k, (cx, cy) in enumerate(_other_chips())]
    return plan


def _reduce_start(own, sends, tag):
    from_sibling = _blocks_to_sibling(sends, "grads_to_sibling_" + tag)
    pair_sums = [_pair_sums_for_chips(a, b, "grads_pair_sums") for a, b in zip(own, from_sibling)]
    lands = [lax.empty(a.shape, a.dtype) for a in pair_sums]
    started = _start_copies(pair_sums + lands, 3 * len(own), _chip_plan(len(own)), "grads_to_chips_start_" + tag)
    return from_sibling, started


def _reduce_finish(own, from_sibling, started, after, tag):
    n_p = len(own)
    bufs = _wait_copies(started, after, 3 * n_p, _chip_plan(n_p), "grads_to_chips_wait_" + tag)
    return [_sum_for_me(a, b, c, "grads_sum") for a, b, c in zip(own, from_sibling, bufs[n_p:])]


def _direct_plan(n_p):
    def plan(refs):
        px, py, pc = _place()
        out = []
        for p in range(n_p):
            for m in range(1, N_DEV):
                tx = 1 - px if m & 4 else px
                ty = 1 - py if m & 2 else py
                tc = 1 - pc if m & 1 else pc
                out.append((refs[p].at[4 * tx + 2 * ty + tc], refs[n_p + p].at[m - 1], (tx, ty, tc)))
        return out
    return plan


def _sum_direct(own, got, name):
    _, r, c = own.shape
    tr = _row_tile(r, 256, 16)

    def body(idx_ref, own_ref, *refs):
        acc = own_ref[...]
        for ref in refs[:-1]:
            acc = acc + ref[...].astype(F32)
        refs[-1][...] = acc

    def part(k):
        return pl.BlockSpec((None, tr, c), lambda i, idx: (k, i, 0))

    grid_spec = pltpu.PrefetchScalarGridSpec(
        num_scalar_prefetch=1, grid=(r // tr,),
        in_specs=[pl.BlockSpec((None, tr, c), lambda i, idx: (idx[0], i, 0))] + [part(k) for k in range(N_DEV - 1)],
        out_specs=pl.BlockSpec((tr, c), lambda i, idx: (i, 0)))
    px, py, pc = _place()
    idx = (4 * px + 2 * py + pc).astype(jnp.int32).reshape(1)
    return pl.pallas_call(
        body, name=name, grid_spec=grid_spec, out_shape=jax.ShapeDtypeStruct((r, c), F32),
        compiler_params=_params("parallel"),
    )(idx, own, *([got] * (N_DEV - 1)))


def _reduce_direct_start(sends, tag):
    lands = [lax.empty((N_DEV - 1,) + a.shape[1:], a.dtype) for a in sends]
    return _start_copies(list(sends) + lands, (N_DEV - 1) * len(sends), _direct_plan(len(sends)),
                         "grads_direct_start_" + tag)


def _reduce_direct_finish(own, started, after, tag):
    n_p = len(own)
    bufs = _wait_copies(started, after, (N_DEV - 1) * n_p, _direct_plan(n_p), "grads_direct_wait_" + tag)
    return [_sum_direct(a, b, "grads_sum_direct") for a, b in zip(own, bufs[n_p:])]


def _gather_plans(pieces):
    n_p = len(pieces)
    dims = [(x.shape[axis], axis) for x, axis in pieces]

    def first(refs):
        px, py, pc = _place()
        targets = [(px, py, 1 - pc)] + [(cx, cy, pc) for cx, cy in _other_chips()]
        return [(refs[p], _window(refs[n_p + p], 4 * px + 2 * py + pc, *dims[p]), to)
                for p in range(n_p) for to in targets]

    def second(refs):
        px, py, pc = _place()
        out = []
        for p in range(n_p):
            for cx, cy, cc in [(cx, cy, pc) for cx, cy in _other_chips()] + [(px, py, 1 - pc)]:
                win = _window(refs[p], 4 * cx + 2 * cy + cc, *dims[p])
                out.append((win, win, (px, py, 1 - pc)))
        return out

    return first, second


FORWARD_COPIES = 4


def _flat_rows(a, width):
    return a.reshape(-1, width)


def _full_from_blocks(blocks, name):
    if name in COL_SHARDED:
        _, l, k, nb = blocks.shape
        return jnp.transpose(blocks, (1, 2, 0, 3)).reshape(l, k, N_DEV * nb)
    _, l, rb, n = blocks.shape
    return jnp.transpose(blocks, (1, 0, 2, 3)).reshape(l, N_DEV * rb, n)


def _blocks_from_full(full, name):
    if name in COL_SHARDED:
        l, k, n = full.shape
        return jnp.transpose(full.reshape(l, k, N_DEV, n // N_DEV), (2, 0, 1, 3))
    l, rows, n = full.shape
    return jnp.transpose(full.reshape(l, N_DEV, rows // N_DEV, n), (1, 0, 2, 3))


def _pad_heads(w, width):
    k = w.shape[0]
    return jnp.pad(w.reshape(k, N_HEADS, width), ((0, 0), (0, 0), (0, HEAD_PAD - width))).reshape(k, N_HEADS * HEAD_PAD)


def _unpad_heads(w, width):
    k = w.shape[0]
    return w.reshape(k, N_HEADS, HEAD_PAD)[:, :, :width].reshape(k, N_HEADS * width)


def _layer_operands(full, vec, conv_w_full, l):
    w_in = full['w_in'][l]
    kpe = jnp.pad(w_in[:, LAT:LAT + QK_ROPE], ((0, 0), (QK_NOPE, HEAD_PAD - QK_DIM)))
    w_ukv = full['w_ukv'][l].reshape(KV_LORA, N_HEADS, QK_NOPE + V_DIM)
    w_out = full['w_out'][l]
    d_model = w_out.shape[1]
    wo_attn = jnp.pad(w_out[:N_HEADS * V_DIM].reshape(N_HEADS, V_DIM, d_model),
                      ((0, 0), (0, HEAD_PAD - V_DIM), (0, 0))).reshape(N_HEADS * HEAD_PAD, d_model)
    ops = {
        'w_in': jnp.concatenate([w_in[:, :LAT], kpe, w_in[:, LAT + QK_ROPE:]], axis=1),
        'w_q': _pad_heads(full['w_uq'][l], QK_DIM),
        'w_k': _pad_heads(w_ukv[:, :, :QK_NOPE].reshape(KV_LORA, N_HEADS * QK_NOPE), QK_NOPE),
        'w_v': _pad_heads(w_ukv[:, :, QK_NOPE:].reshape(KV_LORA, N_HEADS * V_DIM), V_DIM),
        'wo_attn': wo_attn,
        'wo_conv': w_out[N_HEADS * V_DIM:],
        'conv_w': jnp.pad(conv_w_full[l], ((0, CONV_HALO - CONV_K), (0, 0))),
        'gq': jnp.pad(vec['q_norm'][l], (0, HEAD_PAD - QK_DIM)),
        'gk': jnp.pad(vec['k_norm'][l], (0, HEAD_PAD - QK_DIM)),
    }
    for n in ('ffn1_norm', 'mix_norm', 'q_latent_norm', 'kv_latent_norm', 'conv_b', 'conv_ln_g', 'conv_ln_b',
              'ffn2_norm', 'post_norm'):
        ops[n] = vec[n][l]
    return ops


def _ffn_fwd(x, g, wgu, wd, fp, h=None, next_gain=None):
    if h is None:
        h = _rms_fwd(x, g, BF16, "rms_fwd_ffn")
    ab, z = _ffn_up(h, wgu, fp, "ffn_up")
    out = _mm(z, wd, res=x, scale=0.5, norm_gain=next_gain, name="ffn_down")
    y, h_next = out if next_gain is not None else (out, None)
    return y, (x, h, ab, z), h_next


def _ffn_bwd(dy, dyb, saved, g, wgu, wd, fp, after_dw=None, after=None, before_dw=None):
    x, h, ab, z = saved
    dab, dx, dxb, dg = _ffn_bwd_rows(dyb, wd, ab, wgu, x, g, dy, fp, "ffn_bwd_rows", after=after)
    first = before_dw(dg) if before_dw is not None else None
    d_wgu = _mm(h, dab, ta=True, blocks=('col', N_DEV), tm=h.shape[1], after=first, name="ffn_dwgu")
    d_wd = _mm(z, dyb, ta=True, scale=0.5, blocks=('row', N_DEV), name="ffn_dwd")
    token = after_dw(d_wgu, d_wd) if after_dw is not None else None
    return dx, dxb, dg, token


def _mixer_fwd(x, h, ops, tabs, after_attention=None):
    p = _mm(h, ops['w_in'], name="mix_in")
    qln, kvln = _lat_norm_fwd(p, ops['q_latent_norm'], ops['kv_latent_norm'], "lat_norm_fwd")
    q_raw = _mm(qln, ops['w_q'], name="mix_q")
    k_raw = _mm(kvln, ops['w_k'], name="mix_k")
    v = _mm(kvln, ops['w_v'], out_dtype=BF16, name="mix_v")
    q, k = _qk_prep_fwd(q_raw, k_raw, p, ops['gq'], ops['gk'], tabs, "qk_prep_fwd")
    o, lse = _attn_fwd(q, k, v, "attn_fwd")
    token = after_attention(o) if after_attention is not None else None
    conv_b = ops['conv_b'] if token is None else _after(ops['conv_b'], token)
    y_conv, cv = _conv_fwd(p, ops['conv_w'], conv_b, ops['conv_ln_g'], ops['conv_ln_b'], "conv_fwd")
    x_attn = _mm(o, ops['wo_attn'], res=x, name="mix_out_attn")
    x_out, h_next = _mm(cv, ops['wo_conv'], res=x_attn, norm_gain=ops['ffn2_norm'], name="mix_out_conv")
    return x_out, (x, h, p, qln, kvln, q_raw, k_raw, v, q, k, o, lse, y_conv, cv), h_next


def _mixer_bwd(dx_out, dxb_out, saved, ops, tabs, token=None):
    x, h, p, qln, kvln, q_raw, k_raw, v, q, k, o, lse, y_conv, cv = saved
    g = {}
    do = _mm(dxb_out, ops['wo_attn'], tb=True, after=token, name="mix_do")
    dcv = _mm(dxb_out, ops['wo_conv'], tb=True, name="mix_dcv")
    g['wo_attn'] = _mm(o, dxb_out, ta=True, name="mix_dwo_attn")
    g['wo_conv'] = _mm(cv, dxb_out, ta=True, name="mix_dwo_conv")
    dq, dk, dv = _attn_bwd(q, k, v, o, lse, do, "attn_bwd")
    dq_raw, dk_raw, dkpe, g['gq'], g['gk'] = _qk_prep_bwd(q_raw, k_raw, p, dq, dk, ops['gq'], ops['gk'], tabs,
                                                          "qk_prep_bwd")
    g['w_q'] = _mm(qln, dq_raw, ta=True, name="mix_dwq")
    g['w_k'] = _mm(kvln, dk_raw, ta=True, name="mix_dwk")
    g['w_v'] = _mm(kvln, dv, ta=True, name="mix_dwv")
    dqln = _mm(dq_raw, ops['w_q'], tb=True, name="mix_dqln")
    dkvln = _mm(dk_raw, ops['w_k'], tb=True, name="mix_dkvln_k")
    dkvln = _mm(dv, ops['w_v'], tb=True, res=dkvln, name="mix_dkvln_v")
    dp_lat, g['q_latent_norm'], g['kv_latent_norm'] = _lat_norm_bwd(
        p, ops['q_latent_norm'], ops['kv_latent_norm'], dqln, dkvln, "lat_norm_bwd")
    dy_conv, g['conv_ln_g'], g['conv_ln_b'], g['conv_b'] = _conv_bwd_ln(
        y_conv, dcv, ops['conv_ln_g'], ops['conv_ln_b'], "conv_bwd_ln")
    dag, g['conv_w'] = _conv_bwd_taps(p, dy_conv, ops['conv_w'], "conv_bwd_taps")
    dp = jnp.concatenate([dp_lat, dkpe, dag], axis=1)
    g['w_in'] = _mm(h, dp, ta=True, name="mix_dw_in")
    dx, dxb, g['mix_norm'] = _matmul_rms_bwd(dp, ops['w_in'], x, ops['mix_norm'], dx_out, "mix_dh_rms")
    return dx, dxb, g


def _mixer_grads_to_params(g):
    d_w_in = g['w_in']
    d_wk = _unpad_heads(g['w_k'], QK_NOPE).reshape(KV_LORA, N_HEADS, QK_NOPE)
    d_wv = _unpad_heads(g['w_v'], V_DIM).reshape(KV_LORA, N_HEADS, V_DIM)
    d_model = g['wo_attn'].shape[1]
    d_wo_attn = g['wo_attn'].reshape(N_HEADS, HEAD_PAD, d_model)[:, :V_DIM].reshape(N_HEADS * V_DIM, d_model)
    return {
        'mix_norm': g['mix_norm'],
        'w_in': jnp.concatenate([d_w_in[:, :LAT], d_w_in[:, LAT + QK_NOPE:LAT + QK_DIM], d_w_in[:, P_A:]], axis=1),
        'q_latent_norm': g['q_latent_norm'], 'w_uq': _unpad_heads(g['w_q'], QK_DIM),
        'kv_latent_norm': g['kv_latent_norm'],
        'w_ukv': jnp.concatenate([d_wk, d_wv], axis=2).reshape(KV_LORA, N_HEADS * (QK_NOPE + V_DIM)),
        'q_norm': g['gq'][:QK_DIM], 'k_norm': g['gk'][:QK_DIM], 'conv_w': g['conv_w'][:CONV_K],
        'conv_b': g['conv_b'], 'conv_ln_g': g['conv_ln_g'], 'conv_ln_b': g['conv_ln_b'],
        'w_out': jnp.concatenate([d_wo_attn, g['wo_conv']], axis=0),
    }


def kernel(x, ffn1_norm, ffn1_w_gate, ffn1_w_up, ffn1_w_down, mix_norm, w_in, q_latent_norm, w_uq, kv_latent_norm, w_ukv, q_norm, k_norm, conv_w, conv_b, conv_ln_g, conv_ln_b, w_out, ffn2_norm, ffn2_w_gate, ffn2_w_up, ffn2_w_down, post_norm, loss_target, m_ffn1_norm, m_ffn1_w_gate, m_ffn1_w_up, m_ffn1_w_down, m_mix_norm, m_w_in, m_q_latent_norm, m_w_uq, m_kv_latent_norm, m_w_ukv, m_q_norm, m_k_norm, m_conv_w, m_conv_b, m_conv_ln_g, m_conv_ln_b, m_w_out, m_ffn2_norm, m_ffn2_w_gate, m_ffn2_w_up, m_ffn2_w_down, m_post_norm, v_ffn1_norm, v_ffn1_w_gate, v_ffn1_w_up, v_ffn1_w_down, v_mix_norm, v_w_in, v_q_latent_norm, v_w_uq, v_kv_latent_norm, v_w_ukv, v_q_norm, v_k_norm, v_conv_w, v_conv_b, v_conv_ln_g, v_conv_ln_b, v_w_out, v_ffn2_norm, v_ffn2_w_gate, v_ffn2_w_up, v_ffn2_w_down, v_post_norm):
    args = locals()
    w = {n: args[n] for n in WEIGHTS}
    mom = {n: args["m_" + n] for n in WEIGHTS}
    var = {n: args["v_" + n] for n in WEIGHTS}
    depth = ffn1_norm.shape[0]
    x0 = x.reshape(x.shape[-2:])
    target = loss_target.reshape(loss_target.shape[-2:])
    t, d_model = x0.shape
    my_block = 4 * lax.axis_index("x") + 2 * lax.axis_index("y") + lax.axis_index("c")

    fb = ffn1_w_gate.shape[-1]
    fp = -(-fb // LANE) * LANE
    ffns = [(l, f) for l in range(depth) for f in (1, 2)]
    pad_cols = lambda a: jnp.pad(a, ((0, 0), (0, fp - fb)))
    gu_local = {(l, f): jnp.concatenate([pad_cols(w[f'ffn{f}_w_gate'][l]), pad_cols(w[f'ffn{f}_w_up'][l])],
                                        axis=1).astype(BF16) for l, f in ffns}
    dn_local = {(l, f): jnp.pad(w[f'ffn{f}_w_down'][l], ((0, fp - fb), (0, 0))).astype(BF16) for l, f in ffns}
    rows_of = {n: w[n].size // d_model for n in REST}
    rest_local = jnp.concatenate([_flat_rows(w[n].astype(BF16), d_model) for n in REST], axis=0)
    n_rest = rest_local.shape[0]
    first_ffn, later = ffns[0], ffns[1:]
    cw = conv_w.reshape(-1)
    cw_rows = -(-cw.size // (8 * LANE)) * 8
    cw_flat = jnp.pad(cw, (0, cw_rows * LANE - cw.size)).reshape(cw_rows, LANE)
    later_pieces = [(gu_local[q], 1) for q in later] + [(dn_local[q], 0) for q in later]
    n_later = len(later_pieces)
    got = _all_gather([(gu_local[first_ffn], 1), (dn_local[first_ffn], 0)], "gather_first")
    wgu, wd = {first_ffn: got[0]}, {first_ffn: got[1]}
    mixer_pieces = [(rest_local, 0), (cw_flat, 0)]
    mixer_plan, mixer_forward_plan = _gather_plans(mixer_pieces)
    gather_plan, forward_plan = _gather_plans(later_pieces)

    def landing(a, axis):
        return lax.empty(a.shape[:axis] + (N_DEV * a.shape[axis],) + a.shape[axis + 1:], a.dtype)

    gather_mixer = _start_copies([a for a, _ in mixer_pieces] + [landing(a, ax) for a, ax in mixer_pieces],
                                 4 * len(mixer_pieces), mixer_plan, "gather_mixer_start", after=got[0])
    gather_later = _start_copies([a for a, _ in later_pieces] + [landing(a, ax) for a, ax in later_pieces],
                                 4 * n_later, gather_plan, "gather_later_start", after=gather_mixer[3])

    x1_first, s1_first, h_first = _ffn_fwd(x0, _after(w['ffn1_norm'][0], gather_later[3]), wgu[first_ffn],
                                           wd[first_ffn], fp, next_gain=w['mix_norm'][0])
    lands = _wait_copies(gather_mixer, x1_first, 4 * len(mixer_pieces), mixer_plan, "gather_mixer_wait")[2:]
    pass_on = _start_copies(lands, FORWARD_COPIES * len(mixer_pieces), mixer_forward_plan, "gather_mixer_forward_start")
    lands = _wait_copies(pass_on, pass_on[3], FORWARD_COPIES * len(mixer_pieces), mixer_forward_plan, "gather_mixer_forward_wait")
    gathered = lands[0].reshape(N_DEV, n_rest, d_model)
    cw_all = lands[1].reshape(N_DEV, cw_rows * LANE)[:, :cw.size]
    full, start = {}, 0
    for n in REST:
        blocks = gathered[:, start:start + rows_of[n]].reshape((N_DEV,) + w[n].shape)
        full[n] = _full_from_blocks(blocks, n)
        start += rows_of[n]
    conv_w_full =jnp.transpose(cw_all.reshape((N_DEV,) + conv_w.shape), (1, 2, 0, 3)).reshape(depth, CONV_K, CONV_W)
    vec = {n: w[n] for n in VECTORS}
    ops = [_layer_operands(full, vec, conv_w_full, l) for l in range(depth)]
    tabs = _rope_tables(t)

    saved, xl = [], x0
    forward_later = []

    def pass_on_later(o_attn):
        lands = _wait_copies(gather_later, o_attn, 4 * n_later, gather_plan, "gather_later_wait")[n_later:]
        forward_later.append(_start_copies(lands, FORWARD_COPIES * n_later, forward_plan, "gather_later_forward_start"))
        return forward_later[0][3]

    for l in range(depth):
        o = ops[l]
        if l == 0:
            x1, s1, h_mix = x1_first, s1_first, h_first
            x2, sm, h2 = _mixer_fwd(x1, h_mix, o, tabs, after_attention=pass_on_later)
            lands = _wait_copies(forward_later[0], x2, FORWARD_COPIES * n_later, forward_plan, "gather_later_forward_wait")
            wgu.update(zip(later, lands[:len(later)]))
            wd.update(zip(later, lands[len(later):]))
        else:
            x1, s1, h_mix = _ffn_fwd(xl, o['ffn1_norm'], wgu[l, 1], wd[l, 1], fp, next_gain=o['mix_norm'])
            x2, sm, h2 = _mixer_fwd(x1, h_mix, o, tabs)
        x3, s2, _ = _ffn_fwd(x2, o['ffn2_norm'], wgu[l, 2], wd[l, 2], fp, h=h2)
        if l + 1 < depth:
            xl = _rms_fwd(x3, o['post_norm'], F32, "rms_fwd_post")
        saved.append((s1, sm, s2, x3))
    loss_part, dx, dxb, d_post_last = _post_norm_loss(x3, ops[depth - 1]['post_norm'], target, "post_norm_loss")
    loss = lax.psum(loss_part[0, 0], ("x", "y", "c"))

    grads, mine_gu, mine_dn, in_flight = [None] * depth, {}, {}, {}

    def exchange(tag):
        def after_dw(d_wgu, d_wd):
            own = [d_wgu[0], d_wd[0]]
            if tag == last_tag:
                from_sibling, started = _reduce_start(own, [d_wgu[1], d_wd[1]], tag)
            else:
                from_sibling, started = None, _reduce_direct_start([d_wgu[1], d_wd[1]], tag)
            in_flight[tag] = (own, from_sibling, started)
            return started[3]
        return after_dw

    def finish(tag, after):
        own, from_sibling, started = in_flight.pop(tag)
        if from_sibling is None:
            return _reduce_direct_finish(own, started, after, tag)
        return _reduce_finish(own, from_sibling, started, after, tag)

    last_tag = f"{first_ffn[0]}{first_ffn[1]}"
    vector_sums = []

    def reduce_vectors(norms0):
        layers = [dict(grads[k]) for k in range(depth)]
        layers[0].update(norms0)
        parts = [jnp.stack([layers[k][n] for k in range(depth)]).reshape(-1) for n in VECTORS + ['conv_w']]
        small = jnp.concatenate(parts)
        s_rows = -(-small.size // (8 * LANE)) * 8
        small = jnp.pad(small, (0, s_rows * LANE - small.size)).reshape(s_rows, LANE)
        small_all = _all_gather([(small, 0)], "gather_small_grads", in_vmem=True)[0]
        vector_sums.append(_sum_parts([small_all[k * s_rows:(k + 1) * s_rows] for k in range(N_DEV)],
                                      "sum_small_grads"))
        return vector_sums[0]

    token = None
    for l in reversed(range(depth)):
        o = ops[l]
        s1, sm, s2, x3 = saved[l]
        if l + 1 < depth:
            dx, dxb, d_post = _rms_bwd(x3, _after(o['post_norm'], token), dx, None, "rms_bwd_post")
        else:
            d_post = d_post_last
        dx, dxb, d_ffn2, token = _ffn_bwd(dx, dxb, s2, o['ffn2_norm'], wgu[l, 2], wd[l, 2], fp, exchange(f"{l}2"))
        if l + 1 < depth:
            mine_gu[l + 1, 1], mine_dn[l + 1, 1] = finish(f"{l + 1}1", dx)
        dx, dxb, gm = _mixer_bwd(dx, dxb, sm, o, tabs, token)
        mine_gu[l, 2], mine_dn[l, 2] = finish(f"{l}2", dx)
        grads[l] = _mixer_grads_to_params(gm)
        if l == 0:
            rest_own = jnp.concatenate(
                [_blocks_from_full(jnp.stack([grads[k][n] for k in range(depth)]), n).reshape(N_DEV, rows_of[n], d_model)
                 for n in REST], axis=1)
            sibling_rest, started_rest = _reduce_start([rest_own], [rest_own.astype(BF16)], "rest")
        norms = dict(post_norm=d_post, ffn2_norm=d_ffn2)
        dx, dxb, d_ffn1, token = _ffn_bwd(
            dx, dxb, s1, o['ffn1_norm'], wgu[l, 1], wd[l, 1], fp, exchange(f"{l}1"),
            after=started_rest[3] if l == 0 else None,
            before_dw=(lambda dg: reduce_vectors(dict(norms, ffn1_norm=dg))) if l == 0 else None)
        grads[l].update(norms, ffn1_norm=d_ffn1)
    grad_x = dx.reshape(x.shape)

    grad, delta, new_m, new_v = {}, {}, {}, {}

    def adamw(names, after=None):
        for n in names:
            shp = w[n].shape
            two_d = lambda a: a.reshape(-1, shp[-1])
            dl, mn, vn = _adamw(two_d(w[n]), two_d(grad[n]), two_d(mom[n]), two_d(var[n]), "adamw_" + n, after)
            delta[n], new_m[n], new_v[n] = dl.reshape(shp), mn.reshape(shp), vn.reshape(shp)
            after = dl[:8]
        return after

    def ffn_grads(f):
        grad[f'ffn{f}_w_gate'] = jnp.stack([mine_gu[l, f][:, :fb] for l in range(depth)])
        grad[f'ffn{f}_w_up'] = jnp.stack([mine_gu[l, f][:, fp:fp + fb] for l in range(depth)])
        grad[f'ffn{f}_w_down'] = jnp.stack([mine_dn[l, f][:fb] for l in range(depth)])
        return [f'ffn{f}_w_gate', f'ffn{f}_w_up', f'ffn{f}_w_down']

    small_sum = vector_sums[0].reshape(-1)
    start = 0
    for n in VECTORS:
        grad[n] = small_sum[start:start + w[n].size].reshape(w[n].shape)
        start += w[n].size
    cw_grad = small_sum[start:start + depth * CONV_K * CONV_W].reshape(depth, CONV_K, CONV_W)
    nb = conv_w.shape[-1]
    grad['conv_w'] = lax.dynamic_slice_in_dim(cw_grad, my_block * nb, nb, axis=2)
    done = adamw(ffn_grads(2) + ['conv_w'], after=token)
    vcat = lambda src: jnp.concatenate([src[n].reshape(-1) for n in VECTORS]).reshape(-1, LANE)
    dl, mn, vn = _adamw(vcat(w), vcat(grad), vcat(mom), vcat(var), "adamw_vectors", done)
    start = 0
    for n in VECTORS:
        sl = lambda a: a.reshape(-1)[start:start + w[n].size].reshape(w[n].shape)
        delta[n], new_m[n], new_v[n] = sl(dl), sl(mn), sl(vn)
        start += w[n].size
    mine_rest = _reduce_finish([rest_own], sibling_rest, started_rest, dl, "rest")[0]
    start = 0
    for n in REST:
        grad[n] = mine_rest[start:start + rows_of[n]].reshape(w[n].shape)
        start += rows_of[n]
    done = adamw(REST)
    mine_gu[first_ffn], mine_dn[first_ffn] = finish(f"{first_ffn[0]}{first_ffn[1]}", done)
    adamw(ffn_grads(1))

    return (loss, grad_x, *[grad[n] for n in WEIGHTS], *[delta[n] for n in WEIGHTS],
            *[new_m[n] for n in WEIGHTS], *[new_v[n] for n in WEIGHTS])
```

```python
import jax
import jax.numpy as jnp
from jax import lax
from jax.experimental import pallas as pl
from jax.experimental.pallas import tpu as pltpu

F32, BF16 = jnp.float32, jnp.bfloat16

N_DEV = 8
N_HEADS = 8
QK_NOPE, QK_ROPE, V_DIM = 64, 32, 64
QK_DIM = QK_NOPE + QK_ROPE
HEAD_PAD = 128
Q_LORA, KV_LORA = 384, 256
LAT = Q_LORA + KV_LORA
CONV_W, CONV_K = 512, 31
CONV_HALO = 32
CHUNK = 64
ROPE_THETA = 10000.0
EPS = 1e-6
ATTN_SCALE = QK_DIM ** -0.5
ATTN_SCALE_LOG2 = ATTN_SCALE * 1.4426950408889634
P_KPE = LAT
P_A = LAT + HEAD_PAD
P_G = P_A + CONV_W
P_COLS = P_G + CONV_W

ADAM_LR, ADAM_B1, ADAM_B2, ADAM_EPS, ADAM_WD, ADAM_STEP = 0.001, 0.9, 0.999, 1e-08, 0.01, 10

V7X_VMEM_BYTES = 64 << 20
VMEM_LIMIT = V7X_VMEM_BYTES - (8 << 20)
MM_VMEM_BUDGET = 40 << 20
LANE = 128
ROW_TILE = 512
ATTN_BLOCK = 512
ATTN_HEADS = 1
CONV_TILE = 256

MESH_ID = pl.DeviceIdType.MESH
ANY = pl.BlockSpec(memory_space=pl.ANY)
VMEM_SPEC = pl.BlockSpec(memory_space=pltpu.VMEM)
HBM_SPEC = pl.BlockSpec(memory_space=pltpu.HBM)
SEM_SPEC = pl.BlockSpec(memory_space=pltpu.SEMAPHORE)
DATAFLOW = pltpu.SideEffectType.DATAFLOW_SIDE_EFFECTING

WEIGHTS = ['ffn1_norm', 'ffn1_w_gate', 'ffn1_w_up', 'ffn1_w_down', 'mix_norm', 'w_in', 'q_latent_norm', 'w_uq',
           'kv_latent_norm', 'w_ukv', 'q_norm', 'k_norm', 'conv_w', 'conv_b', 'conv_ln_g', 'conv_ln_b', 'w_out',
           'ffn2_norm', 'ffn2_w_gate', 'ffn2_w_up', 'ffn2_w_down', 'post_norm']
COL_SHARDED =['ffn1_w_gate', 'ffn1_w_up', 'w_in', 'w_uq', 'w_ukv', 'ffn2_w_gate', 'ffn2_w_up']
REST = ['w_in', 'w_uq', 'w_ukv', 'w_out']
VECTORS = ['ffn1_norm', 'mix_norm', 'q_latent_norm', 'kv_latent_norm', 'q_norm', 'k_norm', 'conv_b', 'conv_ln_g',
           'conv_ln_b', 'ffn2_norm', 'post_norm']


def _params(*sem):
    return pltpu.CompilerParams(dimension_semantics=sem if sem else None, vmem_limit_bytes=VMEM_LIMIT)


def _tile(n, cap):
    if n <= cap:
        return n
    best = 0
    for d in range(LANE, cap + 1, LANE):
        if n % d == 0:
            best = d
    assert best, (n, cap)
    return best


def _row_tile(n, cap=ROW_TILE, mult=8):
    if n <= cap:
        return n
    best = 0
    for d in range(mult, cap + 1, mult):
        if n % d == 0:
            best = d
    assert best, (n, cap)
    return best


def _mm(a, b, *, name, ta=False, tb=False, res=None, scale=1.0, out_dtype=F32, tm=None, tn=None, blocks=None,
        after=None, norm_gain=None):
    (kdim, m) = a.shape if ta else a.shape[::-1]
    (n, kb) = b.shape if tb else b.shape[::-1]
    assert kdim == kb, (a.shape, b.shape, ta, tb)
    tm, tn = tm or _tile(m, 512), tn or _tile(n, 1024)
    if blocks is not None and blocks[0] == 'col':
        tn = n // blocks[1]
    row_block = m // blocks[1] if blocks is not None and blocks[0] == 'row' else None
    if row_block is not None:
        tm = tm if tm % row_block == 0 else row_block
    assert norm_gain is None or (blocks is None and tn == n), (name, tn, n)
    size = lambda arr: jnp.dtype(arr.dtype).itemsize
    out_bytes = tm * tn * ((6 if blocks is not None else jnp.dtype(out_dtype).itemsize) + (4 if res is not None else 0))

    def vmem_need(tk):
        return 2 * (tm * tk * size(a) + tk * tn * size(b) + out_bytes) + (tm * tn * 4 if tk < kdim else 0)

    tk = kdim
    for cand in [d for d in range(kdim - LANE, 0, -LANE) if kdim % d == 0]:
        if vmem_need(tk) <= MM_VMEM_BUDGET:
            break
        tk = cand
    nk = kdim // tk
    n_in = 2 + (res is not None) + (after is not None) + (norm_gain is not None)
    n_out = 2 if blocks is not None or norm_gain is not None else 1
    dims = (((0 if ta else 1,), (1 if tb else 0,)), ((), ()))

    def body(*refs):
        a_ref, b_ref = refs[0], refs[1]
        r_ref = refs[2] if res is not None else None
        o_refs = refs[n_in:n_in + n_out]
        acc_ref = refs[-1] if nk > 1 else None
        part = lax.dot_general(a_ref[...].astype(BF16), b_ref[...].astype(BF16), dims, preferred_element_type=F32)

        def finish(acc):
            if scale != 1.0:
                acc = acc * scale
            if r_ref is not None:
                acc = r_ref[...] + acc
            if norm_gain is not None:
                gain_ref, (o_ref, h_ref) = refs[n_in - 1], o_refs
                o_ref[...] = acc.astype(o_ref.dtype)
                r = lax.rsqrt(jnp.mean(acc * acc, axis=-1, keepdims=True) + EPS)
                h_ref[...] = (acc * r * gain_ref[...]).astype(h_ref.dtype)
                return
            if row_block is not None:
                acc = acc.reshape(tm // row_block, row_block, tn)
            for o_ref in o_refs:
                o_ref[...] = acc.astype(o_ref.dtype)

        if nk == 1:
            finish(part)
        else:
            k = pl.program_id(2)

            @pl.when(k == 0)
            def _():
                acc_ref[...] = part

            @pl.when(k > 0)
            def _():
                acc_ref[...] += part

            @pl.when(k == nk - 1)
            def _():
                finish(acc_ref[...])

    a_spec = pl.BlockSpec((tk, tm), lambda i, j, k: (k, i)) if ta else pl.BlockSpec((tm, tk), lambda i, j, k: (i, k))
    b_spec = pl.BlockSpec((tn, tk), lambda i, j, k: (j, k)) if tb else pl.BlockSpec((tk, tn), lambda i, j, k: (k, j))
    plain = pl.BlockSpec((tm, tn), lambda i, j, k: (i, j))
    if blocks is None:
        out_specs, out_shape = plain, jax.ShapeDtypeStruct((m, n), out_dtype)
    else:
        if blocks[0] == 'col':
            o_spec, shp = pl.BlockSpec((None, tm, tn), lambda i, j, k: (j, i, 0)), (blocks[1], m, tn)
        else:
            o_spec = pl.BlockSpec((tm // row_block, row_block, tn), lambda i, j, k: (i, 0, j))
            shp = (blocks[1], row_block, n)
        out_specs, out_shape = (o_spec, o_spec), (jax.ShapeDtypeStruct(shp, F32), jax.ShapeDtypeStruct(shp, BF16))
    in_specs = [a_spec, b_spec] + ([plain] if res is not None else [])
    args = (a, b) + ((res,) if res is not None else ())
    if after is not None:
        in_specs.append(pl.BlockSpec(after.shape, lambda i, j, k: (0, 0)))
        args += (after,)
    if norm_gain is not None:
        in_specs.append(pl.BlockSpec((1, n), lambda i, j, k: (0, 0)))
        args += (norm_gain.reshape(1, n),)
        out_specs, out_shape = (plain, plain), (out_shape, jax.ShapeDtypeStruct((m, n), BF16))
    return pl.pallas_call(
        body, name=name, grid=(m // tm, n // tn, nk), in_specs=in_specs, out_specs=out_specs, out_shape=out_shape,
        scratch_shapes=[pltpu.VMEM((tm, tn), F32)] if nk > 1 else [],
        compiler_params=_params("parallel", "parallel", "arbitrary"),
    )(*args)


def _rms_fwd(x, g, out_dtype, name):
    t, d = x.shape
    tm = _row_tile(t)

    def body(x_ref, g_ref, o_ref):
        xv = x_ref[...]
        r = lax.rsqrt(jnp.mean(xv * xv, axis=-1, keepdims=True) + EPS)
        o_ref[...] = (xv * r * g_ref[...]).astype(o_ref.dtype)

    return pl.pallas_call(
        body, name=name, grid=(t // tm,),
        in_specs=[pl.BlockSpec((tm, d), lambda i: (i, 0)), pl.BlockSpec((1, d), lambda i: (0, 0))],
        out_specs=pl.BlockSpec((tm, d), lambda i: (i, 0)),
        out_shape=jax.ShapeDtypeStruct((t, d), out_dtype), compiler_params=_params("parallel"),
    )(x, g.reshape(1, d))


def _rms_bwd(x, g, dh, res, name):
    t, d = x.shape
    tm = _row_tile(t)

    def body(*refs):
        x_ref, g_ref, dh_ref = refs[:3]
        r_ref = refs[3] if res is not None else None
        dx_ref, dxb_ref, dg_ref = refs[-3:]
        xv, dhv = x_ref[...], dh_ref[...]
        r = lax.rsqrt(jnp.mean(xv * xv, axis=-1, keepdims=True) + EPS)
        y = xv * r
        dy = dhv * g_ref[...]
        dx = r * (dy - y * jnp.mean(dy * y, axis=-1, keepdims=True))
        if r_ref is not None:
            dx = r_ref[...] + dx
        dx_ref[...] = dx
        dxb_ref[...] = dx.astype(BF16)

        @pl.when(pl.program_id(0) == 0)
        def _():
            dg_ref[...] = jnp.zeros_like(dg_ref)

        dg_ref[...] += jnp.sum(dhv * y, axis=0, keepdims=True)

    row = pl.BlockSpec((tm, d), lambda i: (i, 0))
    vec = pl.BlockSpec((1, d), lambda i: (0, 0))
    args = (x, g.reshape(1, d), dh) + ((res,) if res is not None else ())
    dx, dxb, dg = pl.pallas_call(
        body, name=name, grid=(t // tm,), in_specs=[row, vec, row] + ([row] if res is not None else []),
        out_specs=(row, row, vec),
        out_shape=(jax.ShapeDtypeStruct((t, d), F32), jax.ShapeDtypeStruct((t, d), BF16),
                   jax.ShapeDtypeStruct((1, d), F32)),
        compiler_params=_params("arbitrary"),
    )(*args)
    return dx, dxb, dg.reshape(d)


FFN_PAIR = 4


def _ffn_up(h, wgu, fp, name):
    t, d = h.shape
    tm, tn = _tile(t, 512), FFN_PAIR * 2 * fp
    nj = wgu.shape[1] // tn

    def body(h_ref, w_ref, fac_ref, z_ref):
        ab = jnp.dot(h_ref[...], w_ref[...], preferred_element_type=F32)
        for e in range(FFN_PAIR):
            av, bv = ab[:, 2 * fp * e:2 * fp * e + fp], ab[:, 2 * fp * e + fp:2 * fp * (e + 1)]
            s = jax.nn.sigmoid(av)
            silu = av * s
            z_ref[:, fp * e:fp * (e + 1)] = (silu * bv).astype(z_ref.dtype)
            fac_ref[:, 2 * fp * e:2 * fp * e + fp] = silu.astype(fac_ref.dtype)
            fac_ref[:, 2 * fp * e + fp:2 * fp * (e + 1)] = (bv * (s + silu * (1.0 - s))).astype(fac_ref.dtype)

    return pl.pallas_call(
        body, name=name, grid=(nj, t // tm),
        in_specs=[pl.BlockSpec((tm, d), lambda j, i: (i, 0)), pl.BlockSpec((d, tn), lambda j, i: (0, j))],
        out_specs=(pl.BlockSpec((tm, tn), lambda j, i: (i, j)), pl.BlockSpec((tm, tn // 2), lambda j, i: (i, j))),
        out_shape=(jax.ShapeDtypeStruct((t, wgu.shape[1]), BF16), jax.ShapeDtypeStruct((t, wgu.shape[1] // 2), BF16)),
        compiler_params=_params("parallel", "parallel"),
    )(h, wgu)


def _resident(shape):
    return pl.BlockSpec(shape, lambda i: (0,) * len(shape), pipeline_mode=pl.Buffered(1))


def _ffn_bwd_rows(dyb, wd, fac, wgu, x, g, dy, fp, name, after=None):
    t, d = dyb.shape
    width = fac.shape[1]
    tm = _tile(t, 256)

    def body(dyb_ref, wd_ref, fac_ref, wgu_ref, x_ref, g_ref, dy_ref, *refs):
        dab_ref, dx_ref, dxb_ref, dg_ref = refs[-4:]
        dz = _dot_nt(dyb_ref[...], wd_ref[...]) * 0.5
        for e in range(width // (2 * fp)):
            dze = dz[:, fp * e:fp * (e + 1)]
            d_up = fac_ref[:, 2 * fp * e:2 * fp * e + fp].astype(F32)
            d_gate = fac_ref[:, 2 * fp * e + fp:2 * fp * (e + 1)].astype(F32)
            dab_ref[:, 2 * fp * e:2 * fp * e + fp] = (dze * d_gate).astype(dab_ref.dtype)
            dab_ref[:, 2 * fp * e + fp:2 * fp * (e + 1)] = (dze * d_up).astype(dab_ref.dtype)
        dh = _dot_nt(dab_ref[...], wgu_ref[...])
        xv = x_ref[...]
        r = lax.rsqrt(jnp.mean(xv * xv, axis=-1, keepdims=True) + EPS)
        y = xv * r
        dyn = dh * g_ref[...]
        dx = dy_ref[...] + r * (dyn - y * jnp.mean(dyn * y, axis=-1, keepdims=True))
        dx_ref[...] = dx
        dxb_ref[...] = dx.astype(BF16)

        @pl.when(pl.program_id(0) == 0)
        def _():
            dg_ref[...] = jnp.zeros_like(dg_ref)

        dg_ref[...] += jnp.sum(dh * y, axis=0, keepdims=True)

    row = pl.BlockSpec((tm, d), lambda i: (i, 0))
    wide = pl.BlockSpec((tm, width), lambda i: (i, 0))
    vec = pl.BlockSpec((1, d), lambda i: (0, 0))
    in_specs = [row, _resident(wd.shape), wide, _resident(wgu.shape), row, vec, row]
    args = [dyb, wd, fac, wgu, x, g.reshape(1, d), dy]
    if after is not None:
        in_specs.append(pl.BlockSpec(after.shape, lambda i: (0, 0)))
        args.append(after)
    dab, dx, dxb, dg = pl.pallas_call(
        body, name=name, grid=(t // tm,), in_specs=in_specs, out_specs=(wide, row, row, vec),
        out_shape=(jax.ShapeDtypeStruct((t, width), BF16), jax.ShapeDtypeStruct((t, d), F32),
                   jax.ShapeDtypeStruct((t, d), BF16), jax.ShapeDtypeStruct((1, d), F32)),
        compiler_params=_params("arbitrary"),
    )(*args)
    return dab, dx, dxb, dg.reshape(d)


def _matmul_rms_bwd(dab, wgu, x, g, dy, name, after=None):
    t, kdim = dab.shape
    d = wgu.shape[0]
    tm = _tile(t, 256)

    def body(a_ref, b_ref, x_ref, g_ref, dy_ref, *refs):
        dx_ref, dxb_ref, dg_ref = refs[-3:]
        dh = _dot_nt(a_ref[...].astype(BF16), b_ref[...])
        xv = x_ref[...]
        r = lax.rsqrt(jnp.mean(xv * xv, axis=-1, keepdims=True) + EPS)
        y = xv * r
        dyn = dh * g_ref[...]
        dx = dy_ref[...] + r * (dyn - y * jnp.mean(dyn * y, axis=-1, keepdims=True))
        dx_ref[...] = dx
        dxb_ref[...] = dx.astype(BF16)

        @pl.when(pl.program_id(0) == 0)
        def _():
            dg_ref[...] = jnp.zeros_like(dg_ref)

        dg_ref[...] += jnp.sum(dh * y, axis=0, keepdims=True)

    row = pl.BlockSpec((tm, d), lambda i: (i, 0))
    vec = pl.BlockSpec((1, d), lambda i: (0, 0))
    in_specs = [pl.BlockSpec((tm, kdim), lambda i: (i, 0)), pl.BlockSpec((d, kdim), lambda i: (0, 0)), row, vec, row]
    args = [dab, wgu, x, g.reshape(1, d), dy]
    if after is not None:
        in_specs.append(pl.BlockSpec(after.shape, lambda i: (0, 0)))
        args.append(after)
    dx, dxb, dg = pl.pallas_call(
        body, name=name, grid=(t // tm,), in_specs=in_specs, out_specs=(row, row, vec),
        out_shape=(jax.ShapeDtypeStruct((t, d), F32), jax.ShapeDtypeStruct((t, d), BF16),
                   jax.ShapeDtypeStruct((1, d), F32)),
        compiler_params=_params("arbitrary"),
    )(*args)
    return dx, dxb, dg.reshape(d)


def _lat_norm_fwd(p, g_q, g_kv, name):
    t = p.shape[0]
    tm = _row_tile(t)

    def body(p_ref, gq_ref, gkv_ref, q_ref, kv_ref):
        for lo, hi, g_ref, o_ref in ((0, Q_LORA, gq_ref, q_ref), (Q_LORA, LAT, gkv_ref, kv_ref)):
            xv = p_ref[:, lo:hi]
            r = lax.rsqrt(jnp.mean(xv * xv, axis=-1, keepdims=True) + EPS)
            o_ref[...] = (xv * r * g_ref[...]).astype(o_ref.dtype)

    return pl.pallas_call(
        body, name=name, grid=(t // tm,),
        in_specs=[pl.BlockSpec((tm, P_COLS), lambda i: (i, 0)), pl.BlockSpec((1, Q_LORA), lambda i: (0, 0)),
                  pl.BlockSpec((1, KV_LORA), lambda i: (0, 0))],
        out_specs=(pl.BlockSpec((tm, Q_LORA), lambda i: (i, 0)), pl.BlockSpec((tm, KV_LORA), lambda i: (i, 0))),
        out_shape=(jax.ShapeDtypeStruct((t, Q_LORA), BF16), jax.ShapeDtypeStruct((t, KV_LORA), BF16)),
        compiler_params=_params("parallel"),
    )(p, g_q.reshape(1, Q_LORA), g_kv.reshape(1, KV_LORA))


def _lat_norm_bwd(p, g_q, g_kv, dq, dkv, name):
    t = p.shape[0]
    tm = _row_tile(t)

    def body(p_ref, gq_ref, gkv_ref, dq_ref, dkv_ref, dp_ref, dgq_ref, dgkv_ref):
        first = pl.program_id(0) == 0
        for lo, hi, g_ref, d_ref, dg_ref in ((0, Q_LORA, gq_ref, dq_ref, dgq_ref),
                                             (Q_LORA, LAT, gkv_ref, dkv_ref, dgkv_ref)):
            xv, dhv = p_ref[:, lo:hi], d_ref[...]
            r = lax.rsqrt(jnp.mean(xv * xv, axis=-1, keepdims=True) + EPS)
            y = xv * r
            dy = dhv * g_ref[...]
            dp_ref[:, lo:hi] = r * (dy - y * jnp.mean(dy * y, axis=-1, keepdims=True))

            @pl.when(first)
            def _():
                dg_ref[...] = jnp.zeros_like(dg_ref)

            dg_ref[...] += jnp.sum(dhv * y, axis=0, keepdims=True)

    vq = pl.BlockSpec((1, Q_LORA), lambda i: (0, 0))
    vkv = pl.BlockSpec((1, KV_LORA), lambda i: (0, 0))
    dp, dgq, dgkv = pl.pallas_call(
        body, name=name, grid=(t // tm,),
        in_specs=[pl.BlockSpec((tm, P_COLS), lambda i: (i, 0)), vq, vkv,
                  pl.BlockSpec((tm, Q_LORA), lambda i: (i, 0)), pl.BlockSpec((tm, KV_LORA), lambda i: (i, 0))],
        out_specs=(pl.BlockSpec((tm, LAT), lambda i: (i, 0)), vq, vkv),
        out_shape=(jax.ShapeDtypeStruct((t, LAT), F32), jax.ShapeDtypeStruct((1, Q_LORA), F32),
                   jax.ShapeDtypeStruct((1, KV_LORA), F32)),
        compiler_params=_params("arbitrary"),
    )(p, g_q.reshape(1, Q_LORA), g_kv.reshape(1, KV_LORA), dq, dkv)
    return dp, dgq.reshape(Q_LORA), dgkv.reshape(KV_LORA)


def _rope_tables(t):
    half = QK_ROPE // 2
    pos = jnp.arange(t, dtype=F32)
    inv_freq = 1.0 / (ROPE_THETA ** (jnp.arange(0, QK_ROPE, 2, dtype=F32) / QK_ROPE))
    ang = pos[:, None] * inv_freq[None, :]
    cos, sin = jnp.cos(ang), jnp.sin(ang)
    z = lambda n: jnp.zeros((t, n), F32)
    c_tab = jnp.concatenate([jnp.ones((t, QK_NOPE), F32), cos, cos, z(HEAD_PAD - QK_DIM)], axis=1)
    sa_tab = jnp.concatenate([z(QK_NOPE), -sin, z(half), z(HEAD_PAD - QK_DIM)], axis=1)
    sb_tab = jnp.concatenate([z(QK_NOPE), z(half), sin, z(HEAD_PAD - QK_DIM)], axis=1)
    return c_tab, sa_tab, sb_tab


def _rope(x, c, sa, sb):
    half = QK_ROPE // 2
    return x * c + pltpu.roll(x, HEAD_PAD - half, 1) * sa + pltpu.roll(x, half, 1) * sb


def _rope_t(d, c, sa, sb):
    half = QK_ROPE // 2
    return d * c + pltpu.roll(d * sa, half, 1) + pltpu.roll(d * sb, HEAD_PAD - half, 1)


def _head_rms(x):
    r = lax.rsqrt(jnp.sum(x * x, axis=-1, keepdims=True) * (1.0 / QK_DIM) + EPS)
    return x * r, r


def _qk_prep_fwd(q_raw, k_raw, p, gq, gk, tabs, name):
    t, width = q_raw.shape
    tm = _row_tile(t)

    def body(q_ref, k_ref, p_ref, gq_ref, gk_ref, c_ref, sa_ref, sb_ref, qo_ref, ko_ref):
        c, sa, sb, kpe = c_ref[...], sa_ref[...], sb_ref[...], p_ref[...]
        for h in range(N_HEADS):
            cols = slice(h * HEAD_PAD, (h + 1) * HEAD_PAD)
            qn, _ = _head_rms(q_ref[:, cols])
            qo_ref[:, cols] = _rope(qn * gq_ref[...], c, sa, sb).astype(qo_ref.dtype)
            kn, _ = _head_rms(k_ref[:, cols] + kpe)
            ko_ref[:, cols] = _rope(kn * gk_ref[...], c, sa, sb).astype(ko_ref.dtype)

    rows = pl.BlockSpec((tm, width), lambda i: (i, 0))
    tab = pl.BlockSpec((tm, HEAD_PAD), lambda i: (i, 0))
    vec = pl.BlockSpec((1, HEAD_PAD), lambda i: (0, 0))
    kpe_spec = pl.BlockSpec((tm, HEAD_PAD), lambda i: (i, P_KPE // HEAD_PAD))
    return pl.pallas_call(
        body, name=name, grid=(t // tm,), in_specs=[rows, rows, kpe_spec, vec, vec, tab, tab, tab],
        out_specs=(rows, rows),
        out_shape=(jax.ShapeDtypeStruct(q_raw.shape, BF16), jax.ShapeDtypeStruct(k_raw.shape, BF16)),
        compiler_params=_params("parallel"),
    )(q_raw, k_raw, p, gq.reshape(1, HEAD_PAD), gk.reshape(1, HEAD_PAD), *tabs)


def _qk_prep_bwd(q_raw, k_raw, p, dq, dk, gq, gk, tabs, name):
    t, width = q_raw.shape
    tm = _row_tile(t, 256)

    def body(q_ref, k_ref, p_ref, dq_ref, dk_ref, gq_ref, gk_ref, c_ref, sa_ref, sb_ref,
             dqr_ref, dkr_ref, dkpe_ref, dgq_ref, dgk_ref):
        c, sa, sb, kpe = c_ref[...], sa_ref[...], sb_ref[...], p_ref[...]

        def one(x, d, g_ref):
            n, r = _head_rms(x)
            dng = _rope_t(d, c, sa, sb)
            dn = dng * g_ref[...]
            dx = r * (dn - n * (jnp.sum(dn * n, axis=-1, keepdims=True) * (1.0 / QK_DIM)))
            return dx, jnp.sum(dng * n, axis=0, keepdims=True)

        dgq = dgk = dkpe = None
        for h in range(N_HEADS):
            cols = slice(h * HEAD_PAD, (h + 1) * HEAD_PAD)
            dqr, gq_part = one(q_ref[:, cols], dq_ref[:, cols], gq_ref)
            dkr, gk_part = one(k_ref[:, cols] + kpe, dk_ref[:, cols], gk_ref)
            dqr_ref[:, cols] = dqr
            dkr_ref[:, cols] = dkr
            dgq = gq_part if dgq is None else dgq + gq_part
            dgk = gk_part if dgk is None else dgk + gk_part
            dkpe = dkr if dkpe is None else dkpe + dkr
        dkpe_ref[...] = dkpe

        @pl.when(pl.program_id(0) == 0)
        def _():
            dgq_ref[...] = jnp.zeros_like(dgq_ref)
            dgk_ref[...] = jnp.zeros_like(dgk_ref)

        dgq_ref[...] += dgq
        dgk_ref[...] += dgk

    head = pl.BlockSpec((tm, width), lambda i: (i, 0))
    tab = pl.BlockSpec((tm, HEAD_PAD), lambda i: (i, 0))
    vec = pl.BlockSpec((1, HEAD_PAD), lambda i: (0, 0))
    kpe = pl.BlockSpec((tm, HEAD_PAD), lambda i: (i, P_KPE // HEAD_PAD))
    dqr, dkr, dkpe, dgq, dgk = pl.pallas_call(
        body, name=name, grid=(t // tm,), in_specs=[head, head, kpe, head, head, vec, vec, tab, tab, tab],
        out_specs=(head, head, tab, vec, vec),
        out_shape=(jax.ShapeDtypeStruct(q_raw.shape, F32), jax.ShapeDtypeStruct(k_raw.shape, F32),
                   jax.ShapeDtypeStruct((t, HEAD_PAD), F32), jax.ShapeDtypeStruct((1, HEAD_PAD), F32),
                   jax.ShapeDtypeStruct((1, HEAD_PAD), F32)),
        compiler_params=_params("arbitrary"),
    )(q_raw, k_raw, p, dq, dk, gq.reshape(1, HEAD_PAD), gk.reshape(1, HEAD_PAD), *tabs)
    return dqr, dkr, dkpe, dgq.reshape(HEAD_PAD), dgk.reshape(HEAD_PAD)


def _dot_nt(a, b):
    return lax.dot_general(a, b, (((1,), (1,)), ((), ())), preferred_element_type=F32)


def _dot_tn(a, b):
    return lax.dot_general(a, b, (((0,), (0,)), ((), ())), preferred_element_type=F32)


def _diag_mask():
    rows = lax.broadcasted_iota(jnp.int32, (ATTN_BLOCK, ATTN_BLOCK), 0) // CHUNK
    cols = lax.broadcasted_iota(jnp.int32, (ATTN_BLOCK, ATTN_BLOCK), 1) // CHUNK
    return cols <= rows


def _attn_fwd(q, k, v, name):
    t = q.shape[0]
    bq = ATTN_BLOCK
    nq = t // bq

    width = ATTN_HEADS * HEAD_PAD

    def body(q_ref, k_ref, v_ref, o_ref, lse_ref):
        i = pl.program_id(1)
        heads = [slice(e * HEAD_PAD, (e + 1) * HEAD_PAD) for e in range(ATTN_HEADS)]
        qv = [q_ref[:, cols] for cols in heads]

        def block(j, carries, masked):
            rows = pl.ds(pl.multiple_of(j * bq, bq), bq)
            out = []
            for e, (m, l, acc) in enumerate(carries):
                s = _dot_nt(qv[e], k_ref[rows, heads[e]]) * ATTN_SCALE_LOG2
                if masked:
                    s = jnp.where(_diag_mask(), s, -1e30)
                m_new = jnp.maximum(m, jnp.max(s, axis=-1, keepdims=True))
                alpha = jnp.exp2(m - m_new)
                pe = jnp.exp2(s - m_new)
                l = alpha * l + jnp.sum(pe, axis=-1, keepdims=True)
                acc = alpha * acc + jnp.dot(pe.astype(BF16), v_ref[rows, heads[e]], preferred_element_type=F32)
                out.append((m_new, l, acc))
            return tuple(out)

        init = tuple((jnp.full((bq, 1), -1e30, F32), jnp.zeros((bq, 1), F32), jnp.zeros((bq, HEAD_PAD), F32))
                     for _ in heads)
        carries = lax.fori_loop(0, i, lambda j, cr: block(j, cr, False), init)
        for cols, (m, l, acc) in zip(heads, block(i, carries, True)):
            o_ref[:, cols] = acc / l
            lse_ref[:, cols] = jnp.broadcast_to(m + jnp.log2(l), (bq, HEAD_PAD))

    blk = pl.BlockSpec((bq, width), lambda h, i: (i, h))
    full = pl.BlockSpec((t, width), lambda h, i: (0, h))
    return pl.pallas_call(
        body, name=name, grid=(N_HEADS // ATTN_HEADS, nq), in_specs=[blk, full, full], out_specs=(blk, blk),
        out_shape=(jax.ShapeDtypeStruct(q.shape, F32), jax.ShapeDtypeStruct(q.shape, F32)),
        compiler_params=_params("parallel", "parallel"),
    )(q, k, v)


def _attn_bwd(q, k, v, o, lse, do, name):
    t = q.shape[0]
    bq = ATTN_BLOCK
    nq = t // bq

    def body(q_ref, k_ref, v_ref, o_ref, lse_ref, do_ref, dq_ref, dk_ref, dv_ref, delta_ref):
        def rows_of(i):
            return pl.ds(pl.multiple_of(i * bq, bq), bq)

        def prep(i, _):
            r = rows_of(i)
            delta_ref[r, :] = jnp.broadcast_to(jnp.sum(do_ref[r, :] * o_ref[r, :], axis=-1, keepdims=True),
                                               (bq, HEAD_PAD))
            dq_ref[r, :] = jnp.zeros((bq, HEAD_PAD), F32)
            return 0

        lax.fori_loop(0, nq, prep, 0)

        def key_block(j, _):
            rj = rows_of(j)
            kb, vb = k_ref[rj, :], v_ref[rj, :]

            def query_block(i, carry, masked):
                dk, dv = carry
                ri = rows_of(i)
                qb, dob = q_ref[ri, :], do_ref[ri, :].astype(BF16)
                s = _dot_nt(qb, kb) * ATTN_SCALE_LOG2
                if masked:
                    s = jnp.where(_diag_mask(), s, -1e30)
                pe = jnp.exp2(s - lse_ref[ri, :][:, :1])
                dp = _dot_nt(dob, vb)
                ds = (pe * (dp - delta_ref[ri, :][:, :1]) * ATTN_SCALE).astype(BF16)
                dq_ref[ri, :] += jnp.dot(ds, kb, preferred_element_type=F32)
                return dk + _dot_tn(ds, qb), dv + _dot_tn(pe.astype(BF16), dob)

            zero = jnp.zeros((bq, HEAD_PAD), F32)
            carry = query_block(j, (zero, zero), True)
            dk, dv = lax.fori_loop(j + 1, nq, lambda i, cr: query_block(i, cr, False), carry)
            dk_ref[rj, :] = dk
            dv_ref[rj, :] = dv
            return 0

        lax.fori_loop(0, nq, key_block, 0)

    full = pl.BlockSpec((t, HEAD_PAD), lambda h: (0, h))
    shp = jax.ShapeDtypeStruct(q.shape, F32)
    return pl.pallas_call(
        body, name=name, grid=(N_HEADS,), in_specs=[full] * 6, out_specs=(full, full, full),
        out_shape=(shp, shp, shp), scratch_shapes=[pltpu.VMEM((t, HEAD_PAD), F32)],
        compiler_params=_params("parallel"),
    )(q, k, v, o, lse, do)


def _glu_ext(pc_ref, pp_ref, u_ref, tm, first):
    u_ref[CONV_HALO:CONV_HALO + tm, :] = pc_ref[:, P_A:P_G] * jax.nn.sigmoid(pc_ref[:, P_G:P_COLS])
    up = pp_ref[tm - CONV_HALO:tm, P_A:P_G] * jax.nn.sigmoid(pp_ref[tm - CONV_HALO:tm, P_G:P_COLS])
    u_ref[0:CONV_HALO, :] = jnp.where(first, 0.0, up)


SUBLANES = 8


def _shift_copies(src_ref, sh_ref):
    rows = sh_ref.shape[1]
    for b in range(1, SUBLANES):
        sh_ref[b - 1, :, :] = src_ref[b:b + rows, :]


def _rows_at(src_ref, sh_ref, start, n):
    a, b = divmod(start, SUBLANES)
    if b == 0:
        return src_ref[SUBLANES * a:SUBLANES * a + n, :]
    return sh_ref[b - 1, SUBLANES * a:SUBLANES * a + n, :]


def _conv_fwd(p, w, b, ln_g, ln_b, name):
    t = p.shape[0]
    tm = _row_tile(t, CONV_TILE)
    off = CONV_HALO - (CONV_K - 1)

    def body(pc_ref, pp_ref, w_ref, b_ref, g_ref, bb_ref, y_ref, o_ref, u_ref, ush_ref):
        _glu_ext(pc_ref, pp_ref, u_ref, tm, pl.program_id(0) == 0)
        _shift_copies(u_ref, ush_ref)
        acc = jnp.zeros((tm, CONV_W), F32)
        for kk in range(CONV_K):
            acc = acc + w_ref[kk:kk + 1, :] * _rows_at(u_ref, ush_ref, off + kk, tm)
        y = acc + b_ref[...]
        y_ref[...] = y
        xc = y - jnp.mean(y, axis=-1, keepdims=True)
        lo = xc * lax.rsqrt(jnp.mean(xc * xc, axis=-1, keepdims=True) + EPS) * g_ref[...] + bb_ref[...]
        o_ref[...] = (lo * jax.nn.sigmoid(lo)).astype(o_ref.dtype)

    prow = pl.BlockSpec((tm, P_COLS), lambda i: (i, 0))
    pprev = pl.BlockSpec((tm, P_COLS), lambda i: (jnp.maximum(i - 1, 0), 0))
    vec = pl.BlockSpec((1, CONV_W), lambda i: (0, 0))
    row = pl.BlockSpec((tm, CONV_W), lambda i: (i, 0))
    return pl.pallas_call(
        body, name=name, grid=(t // tm,),
        in_specs=[prow, pprev, pl.BlockSpec((CONV_HALO, CONV_W), lambda i: (0, 0)), vec, vec, vec],
        out_specs=(row, row),
        out_shape=(jax.ShapeDtypeStruct((t, CONV_W), F32), jax.ShapeDtypeStruct((t, CONV_W), BF16)),
        scratch_shapes=[pltpu.VMEM((tm + CONV_HALO, CONV_W), F32),
                        pltpu.VMEM((SUBLANES - 1, tm + CONV_HALO - SUBLANES, CONV_W), F32)],
        compiler_params=_params("parallel"),
    )(p, p, w, b.reshape(1, CONV_W), ln_g.reshape(1, CONV_W), ln_b.reshape(1, CONV_W))


def _conv_bwd_ln(y, dout, ln_g, ln_b, name):
    t = y.shape[0]
    tm = _row_tile(t)

    def body(y_ref, d_ref, g_ref, bb_ref, dy_ref, dg_ref, db_ref, dcb_ref):
        yv = y_ref[...]
        xc = yv - jnp.mean(yv, axis=-1, keepdims=True)
        r = lax.rsqrt(jnp.mean(xc * xc, axis=-1, keepdims=True) + EPS)
        n = xc * r
        lo = n * g_ref[...] + bb_ref[...]
        s = jax.nn.sigmoid(lo)
        dlo = d_ref[...] * (s * (1.0 + lo * (1.0 - s)))
        dn = dlo * g_ref[...]
        dy = r * (dn - jnp.mean(dn, axis=-1, keepdims=True) - n * jnp.mean(dn * n, axis=-1, keepdims=True))
        dy_ref[...] = dy

        @pl.when(pl.program_id(0) == 0)
        def _():
            dg_ref[...] = jnp.zeros_like(dg_ref)
            db_ref[...] = jnp.zeros_like(db_ref)
            dcb_ref[...] = jnp.zeros_like(dcb_ref)

        dg_ref[...] += jnp.sum(dlo * n, axis=0, keepdims=True)
        db_ref[...] += jnp.sum(dlo, axis=0, keepdims=True)
        dcb_ref[...] += jnp.sum(dy, axis=0, keepdims=True)

    row = pl.BlockSpec((tm, CONV_W), lambda i: (i, 0))
    vec = pl.BlockSpec((1, CONV_W), lambda i: (0, 0))
    vshape = jax.ShapeDtypeStruct((1, CONV_W), F32)
    dy, dg, db, dcb = pl.pallas_call(
        body, name=name, grid=(t // tm,), in_specs=[row, row, vec, vec], out_specs=(row, vec, vec, vec),
        out_shape=(jax.ShapeDtypeStruct((t, CONV_W), F32), vshape, vshape, vshape),
        compiler_params=_params("arbitrary"),
    )(y, dout, ln_g.reshape(1, CONV_W), ln_b.reshape(1, CONV_W))
    return dy, dg.reshape(CONV_W), db.reshape(CONV_W), dcb.reshape(CONV_W)


def _conv_bwd_taps(p, dy, w, name):
    t = p.shape[0]
    tm = _row_tile(t, CONV_TILE)
    nt = t // tm
    off = CONV_HALO - (CONV_K - 1)

    def body(pc_ref, pp_ref, dyc_ref, dyn_ref, w_ref, dag_ref, dw_ref, u_ref, dye_ref, ush_ref, dysh_ref):
        i = pl.program_id(0)
        _glu_ext(pc_ref, pp_ref, u_ref, tm, i == 0)
        dyc = dyc_ref[...]
        dye_ref[0:tm, :] = dyc
        dye_ref[tm:tm + CONV_HALO, :] = jnp.where(i == nt - 1, 0.0, dyn_ref[0:CONV_HALO, :])

        _shift_copies(u_ref, ush_ref)
        _shift_copies(dye_ref, dysh_ref)

        @pl.when(i == 0)
        def _():
            dw_ref[...] = jnp.zeros_like(dw_ref)

        du = jnp.zeros((tm, CONV_W), F32)
        for kk in range(CONV_K):
            dw_ref[kk:kk + 1, :] += jnp.sum(dyc * _rows_at(u_ref, ush_ref, off + kk, tm), axis=0, keepdims=True)
            du = du + w_ref[kk:kk + 1, :] * _rows_at(dye_ref, dysh_ref, CONV_K - 1 - kk, tm)
        av, gv = pc_ref[:, P_A:P_G], pc_ref[:, P_G:P_COLS]
        s = jax.nn.sigmoid(gv)
        dag_ref[:, 0:CONV_W] = du * s
        dag_ref[:, CONV_W:2 * CONV_W] = du * av * (s * (1.0 - s))

    prow = pl.BlockSpec((tm, P_COLS), lambda i: (i, 0))
    pprev = pl.BlockSpec((tm, P_COLS), lambda i: (jnp.maximum(i - 1, 0), 0))
    row = pl.BlockSpec((tm, CONV_W), lambda i: (i, 0))
    nxt = pl.BlockSpec((tm, CONV_W), lambda i: (jnp.minimum(i + 1, nt - 1), 0))
    wspec = pl.BlockSpec((CONV_HALO, CONV_W), lambda i: (0, 0))
    return pl.pallas_call(
        body, name=name, grid=(nt,), in_specs=[prow, pprev, row, nxt, wspec],
        out_specs=(pl.BlockSpec((tm, 2 * CONV_W), lambda i: (i, 0)), wspec),
        out_shape=(jax.ShapeDtypeStruct((t, 2 * CONV_W), F32), jax.ShapeDtypeStruct((CONV_HALO, CONV_W), F32)),
        scratch_shapes=[pltpu.VMEM((tm + CONV_HALO, CONV_W), F32), pltpu.VMEM((tm + CONV_HALO, CONV_W), F32),
                        pltpu.VMEM((SUBLANES - 1, tm + CONV_HALO - SUBLANES, CONV_W), F32),
                        pltpu.VMEM((SUBLANES - 1, tm + CONV_HALO - SUBLANES, CONV_W), F32)],
        compiler_params=_params("arbitrary"),
    )(p, p, dy, dy, w)


def _post_norm_loss(x, g, target, name):
    t, d = x.shape
    tm = _row_tile(t)

    def body(x_ref, g_ref, t_ref, l_ref, dx_ref, dxb_ref, dg_ref):
        xv = x_ref[...]
        r = lax.rsqrt(jnp.mean(xv * xv, axis=-1, keepdims=True) + EPS)
        n = xv * r
        err = n * g_ref[...] - t_ref[...]
        dy = err * (1.0 / d)
        dn = dy * g_ref[...]
        dx = r * (dn - n * jnp.mean(dn * n, axis=-1, keepdims=True))
        dx_ref[...] = dx
        dxb_ref[...] = dx.astype(BF16)

        @pl.when(pl.program_id(0) == 0)
        def _():
            l_ref[...] = jnp.zeros_like(l_ref)
            dg_ref[...] = jnp.zeros_like(dg_ref)

        row = jnp.sum(err * err, axis=-1, keepdims=True) * (0.5 / d)
        l_ref[...] += jnp.broadcast_to(jnp.sum(row, axis=0, keepdims=True), (1, LANE))
        dg_ref[...] += jnp.sum(dy * n, axis=0, keepdims=True)

    row = pl.BlockSpec((tm, d), lambda i: (i, 0))
    vec = pl.BlockSpec((1, d), lambda i: (0, 0))
    loss, dx, dxb, dg = pl.pallas_call(
        body, name=name, grid=(t // tm,), in_specs=[row, vec, row],
        out_specs=(pl.BlockSpec((1, LANE), lambda i: (0, 0)), row, row, vec),
        out_shape=(jax.ShapeDtypeStruct((1, LANE), F32), jax.ShapeDtypeStruct((t, d), F32),
                   jax.ShapeDtypeStruct((t, d), BF16), jax.ShapeDtypeStruct((1, d), F32)),
        compiler_params=_params("arbitrary"),
    )(x, g.reshape(1, d), target)
    return loss, dx, dxb, dg.reshape(d)


def _adamw(w, g, m, v, name, after=None):
    r, c = w.shape
    tr = _row_tile(r, 256)
    c1, c2 = 1.0 - ADAM_B1 ** ADAM_STEP, 1.0 - ADAM_B2 ** ADAM_STEP

    def body(w_ref, g_ref, m_ref, v_ref, *refs):
        d_ref, mo_ref, vo_ref = refs[-3:]
        gv = g_ref[...]
        mn = ADAM_B1 * m_ref[...] + (1.0 - ADAM_B1) * gv
        vn = ADAM_B2 * v_ref[...] + (1.0 - ADAM_B2) * (gv * gv)
        mo_ref[...] = mn
        vo_ref[...] = vn
        d_ref[...] = -ADAM_LR * ((mn / c1) / (jnp.sqrt(vn / c2) + ADAM_EPS) + ADAM_WD * w_ref[...])

    blk = pl.BlockSpec((tr, c), lambda i: (i, 0))
    shp = jax.ShapeDtypeStruct((r, c), F32)
    extra = [] if after is None else [pl.BlockSpec(after.shape, lambda i: (0, 0))]
    return pl.pallas_call(
        body, name=name, grid=(r // tr,), in_specs=[blk] * 4 + extra, out_specs=(blk, blk, blk),
        out_shape=(shp, shp, shp), compiler_params=_params("parallel"),
    )(w, g, m, v, *([] if after is None else [after]))


def _sum_parts(parts, name):
    r, c = parts[0].shape
    tr = _row_tile(r, 256)

    def body(*refs):
        acc = refs[0][...]
        for ref in refs[1:-1]:
            acc = acc + ref[...]
        refs[-1][...] = acc

    blk = pl.BlockSpec((tr, c), lambda i: (i, 0))
    return pl.pallas_call(
        body, name=name, grid=(r // tr,), in_specs=[blk] * len(parts), out_specs=blk,
        out_shape=jax.ShapeDtypeStruct((r, c), F32), compiler_params=_params("parallel"),
    )(*parts)


def _place():
    return lax.axis_index("x"), lax.axis_index("y"), lax.axis_index("c")


def _window(ref, block, size, axis):
    start = pl.multiple_of(block * size, LANE if size % LANE == 0 else 8)
    return ref.at[(slice(None),) * axis + (pl.ds(start, size),)]


def _all_gather(pieces, name, in_vmem=False):
    n_p = n_all = len(pieces)

    def body(*refs):
        x_refs, out_refs = refs[:n_all], refs[n_all:2 * n_all]
        send_sems, recv_sems, local_sems = refs[2 * n_all:]
        px, py, pc = _place()
        me, sibling = (px, py, pc), (px, py, 1 - pc)
        chips = [(1 - px, py), (px, 1 - py), (1 - px, 1 - py)]

        def win(p, block):
            bx, by, bc = block
            x, axis = pieces[p]
            return _window(out_refs[p], 4 * bx + 2 * by + bc, x.shape[axis], axis)

        def copy(k, p, block, to, local=False):
            return pltpu.make_async_remote_copy(
                src_ref=x_refs[p] if local else win(p, block), dst_ref=win(p, block),
                send_sem=send_sems.at[k, p], recv_sem=recv_sems.at[k, p], device_id=to, device_id_type=MESH_ID)

        every = range(n_p)
        mine = [pltpu.make_async_copy(x_refs[p], win(p, me), local_sems.at[p]) for p in range(n_all)]
        first = [copy(0, p, me, sibling, local=True) for p in every]
        first += [copy(1 + j, p, me, (*chip, pc), local=True) for j, chip in enumerate(chips) for p in every]
        for cp in mine + first:
            cp.start()
        passed = []
        for j, chip in enumerate(chips):
            for p in every:
                copy(1 + j, p, (*chip, pc), me).wait_recv()
                passed.append(copy(4 + j, p, (*chip, pc), sibling))
                passed[-1].start()
        for p in every:
            copy(0, p, sibling, me).wait_recv()
        for j, chip in enumerate(chips):
            for p in every:
                copy(4 + j, p, (*chip, 1 - pc), me).wait_recv()
        for cp in first + passed:
            cp.wait_send()
        for cp in mine:
            cp.wait()

    def gathered(x, axis):
        return jax.ShapeDtypeStruct(x.shape[:axis] + (N_DEV * x.shape[axis],) + x.shape[axis + 1:], x.dtype)

    spec = VMEM_SPEC if in_vmem else ANY
    return pl.pallas_call(
        body, name=name, in_specs=[spec] * n_all, out_specs=[spec] * n_all,
        out_shape=[gathered(*pc_) for pc_ in pieces],
        scratch_shapes=[pltpu.SemaphoreType.DMA((7, n_p)), pltpu.SemaphoreType.DMA((7, n_p)),
                        pltpu.SemaphoreType.DMA((n_all,))],
        compiler_params=pltpu.CompilerParams(vmem_limit_bytes=VMEM_LIMIT),
    )(*[x for x, _ in pieces])


def _start_copies(bufs, n_copies, plan, name, after=None):
    nb = len(bufs)
    n_in = nb + (after is not None)

    def body(*refs):
        send_sems, recv_sems, token = refs[n_in], refs[n_in + 1], refs[-1]
        for i, (src, dst, dev) in enumerate(plan(refs[:nb])):
            pltpu.make_async_remote_copy(src_ref=src, dst_ref=dst, send_sem=send_sems.at[i], recv_sem=recv_sems.at[i],
                                         device_id=dev, device_id_type=MESH_ID).start()
        token[...] = jnp.zeros_like(token)

    out = pl.pallas_call(
        body, name=name, in_specs=[HBM_SPEC] * nb + [ANY] * (after is not None),
        out_shape=(pltpu.SemaphoreType.DMA((n_copies,)), pltpu.SemaphoreType.DMA((n_copies,)),
                   *[pltpu.HBM(b.shape, b.dtype) for b in bufs], jax.ShapeDtypeStruct((8, LANE), F32)),
        out_specs=(SEM_SPEC, SEM_SPEC, *[HBM_SPEC] * nb, VMEM_SPEC),
        input_output_aliases={i: 2 + i for i in range(nb)},
        compiler_params=pltpu.CompilerParams(has_side_effects=DATAFLOW),
    )(*[pltpu.with_memory_space_constraint(b, pltpu.HBM) for b in bufs], *([after] if after is not None else []))
    return out[0], out[1], list(out[2:2 + nb]), out[-1]


def _wait_copies(started, after, n_copies, plan, name):
    send_sems, recv_sems, bufs, _ = started
    nb = len(bufs)

    def body(*refs):
        send_ref, recv_ref = refs[nb], refs[nb + 1]
        copies = [pltpu.make_async_remote_copy(src_ref=src, dst_ref=dst, send_sem=send_ref.at[i], recv_sem=recv_ref.at[i],
                                               device_id=dev, device_id_type=MESH_ID)
                  for i, (src, dst, dev) in enumerate(plan(refs[:nb]))]
        for cp in copies:
            cp.wait_send()
        for cp in copies:
            cp.wait_recv()

    out = pl.pallas_call(
        body, name=name, in_specs=[HBM_SPEC] * nb + [SEM_SPEC, SEM_SPEC, ANY],
        out_shape=tuple(pltpu.HBM(b.shape, b.dtype) for b in bufs), out_specs=tuple([HBM_SPEC] * nb),
        input_output_aliases={i: i for i in range(nb)},
        compiler_params=pltpu.CompilerParams(has_side_effects=DATAFLOW),
    )(*bufs, send_sems, recv_sems, after)
    return list(out)


def _after(x, token):
    return x + token[0, 0].astype(x.dtype)


def _other_chips():
    px, py, _ = _place()
    return [(1 - px, py), (px, 1 - py), (1 - px, 1 - py)]


def _exchange(srcs, slots, src_block, target, name):
    n_p = len(srcs)

    def body(*refs):
        src_refs, out_refs, send_sems, recv_sems = refs[:n_p], refs[n_p:2 * n_p], refs[-2], refs[-1]
        copies = [pltpu.make_async_remote_copy(
            src_ref=src_refs[p].at[src_block(s)], dst_ref=out_refs[p].at[s], send_sem=send_sems.at[s, p],
            recv_sem=recv_sems.at[s, p], device_id=target(s), device_id_type=MESH_ID)
            for s in range(slots) for p in range(n_p)]
        for cp in copies:
            cp.start()
        for cp in copies:
            cp.wait_recv()
        for cp in copies:
            cp.wait_send()

    return pl.pallas_call(
        body, name=name, in_specs=[ANY] * n_p, out_specs=[ANY] * n_p,
        out_shape=[jax.ShapeDtypeStruct((slots,) + a.shape[1:], a.dtype) for a in srcs],
        scratch_shapes=[pltpu.SemaphoreType.DMA((slots, n_p)), pltpu.SemaphoreType.DMA((slots, n_p))],
        compiler_params=pltpu.CompilerParams(vmem_limit_bytes=VMEM_LIMIT),
    )(*srcs)


def _blocks_to_sibling(sends, name):
    def src_block(j):
        return 2 * j + 1 - lax.axis_index("c")

    def target(j):
        px, py, pc = _place()
        return (px, py, 1 - pc)

    return _exchange(sends, 4, src_block, target, name)


def _pair_sums_for_chips(own, got, name):
    _, r, c = own.shape
    tr = _row_tile(r, 256, 16)

    def body(idx_ref, own_ref, got_ref, o_ref):
        o_ref[...] = (own_ref[...] + got_ref[...].astype(F32)).astype(o_ref.dtype)

    grid_spec = pltpu.PrefetchScalarGridSpec(
        num_scalar_prefetch=1, grid=(3, r // tr),
        in_specs=[pl.BlockSpec((None, tr, c), lambda k, i, idx: (idx[k], i, 0)),
                  pl.BlockSpec((None, tr, c), lambda k, i, idx: (idx[3 + k], i, 0))],
        out_specs=pl.BlockSpec((None, tr, c), lambda k, i, idx: (k, i, 0)))
    chips = [2 * cx + cy for cx, cy in _other_chips()]
    idx = jnp.stack([2 * j + lax.axis_index("c") for j in chips] + chips).astype(jnp.int32)
    return pl.pallas_call(
        body, name=name, grid_spec=grid_spec, out_shape=jax.ShapeDtypeStruct((3, r, c), BF16),
        compiler_params=_params("parallel", "parallel"),
    )(idx, own, got)


def _sum_for_me(own, got_sibling, got_chips, name):
    _, r, c = own.shape
    tr = _row_tile(r, 256, 16)

    def body(idx_ref, own_ref, sib_ref, g0_ref, g1_ref, g2_ref, o_ref):
        acc = own_ref[...] + sib_ref[...].astype(F32)
        for ref in (g0_ref, g1_ref, g2_ref):
            acc = acc + ref[...].astype(F32)
        o_ref[...] = acc

    def part(k):
        return pl.BlockSpec((None, tr, c), lambda i, idx: (k, i, 0))

    grid_spec = pltpu.PrefetchScalarGridSpec(
        num_scalar_prefetch=1, grid=(r // tr,),
        in_specs=[pl.BlockSpec((None, tr, c), lambda i, idx: (idx[0], i, 0)),
                  pl.BlockSpec((None, tr, c), lambda i, idx: (idx[1], i, 0)), part(0), part(1), part(2)],
        out_specs=pl.BlockSpec((tr, c), lambda i, idx: (i, 0)))
    px, py, pc = _place()
    idx = jnp.stack([4 * px + 2 * py + pc, 2 * px + py]).astype(jnp.int32)
    return pl.pallas_call(
        body, name=name, grid_spec=grid_spec, out_shape=jax.ShapeDtypeStruct((r, c), F32),
        compiler_params=_params("parallel"),
    )(idx, own, got_sibling, got_chips, got_chips, got_chips)


def _chip_plan(n_p):
    def plan(refs):
        pc = lax.axis_index("c")
        return [(refs[p].at[k], refs[n_p + p].at[k], (cx, cy, pc))
                for p in range(n_p) for k, (cx, cy) in enumerate(_other_chips())]
    return plan


def _reduce_start(own, sends, tag):
    from_sibling = _blocks_to_sibling(sends, "grads_to_sibling_" + tag)
    pair_sums = [_pair_sums_for_chips(a, b, "grads_pair_sums") for a, b in zip(own, from_sibling)]
    lands = [lax.empty(a.shape, a.dtype) for a in pair_sums]
    started = _start_copies(pair_sums + lands, 3 * len(own), _chip_plan(len(own)), "grads_to_chips_start_" + tag)
    return from_sibling, started


def _reduce_finish(own, from_sibling, started, after, tag):
    n_p = len(own)
    bufs = _wait_copies(started, after, 3 * n_p, _chip_plan(n_p), "grads_to_chips_wait_" + tag)
    return [_sum_for_me(a, b, c, "grads_sum") for a, b, c in zip(own, from_sibling, bufs[n_p:])]


def _direct_plan(n_p):
    def plan(refs):
        px, py, pc = _place()
        out = []
        for p in range(n_p):
            for m in range(1, N_DEV):
                tx = 1 - px if m & 4 else px
                ty = 1 - py if m & 2 else py
                tc = 1 - pc if m & 1 else pc
                out.append((refs[p].at[4 * tx + 2 * ty + tc], refs[n_p + p].at[m - 1], (tx, ty, tc)))
        return out
    return plan


def _sum_direct(own, got, name):
    _, r, c = own.shape
    tr = _row_tile(r, 256, 16)

    def body(idx_ref, own_ref, *refs):
        acc = own_ref[...]
        for ref in refs[:-1]:
            acc = acc + ref[...].astype(F32)
        refs[-1][...] = acc

    def part(k):
        return pl.BlockSpec((None, tr, c), lambda i, idx: (k, i, 0))

    grid_spec = pltpu.PrefetchScalarGridSpec(
        num_scalar_prefetch=1, grid=(r // tr,),
        in_specs=[pl.BlockSpec((None, tr, c), lambda i, idx: (idx[0], i, 0))] + [part(k) for k in range(N_DEV - 1)],
        out_specs=pl.BlockSpec((tr, c), lambda i, idx: (i, 0)))
    px, py, pc = _place()
    idx = (4 * px + 2 * py + pc).astype(jnp.int32).reshape(1)
    return pl.pallas_call(
        body, name=name, grid_spec=grid_spec, out_shape=jax.ShapeDtypeStruct((r, c), F32),
        compiler_params=_params("parallel"),
    )(idx, own, *([got] * (N_DEV - 1)))


def _reduce_direct_start(sends, tag):
    lands = [lax.empty((N_DEV - 1,) + a.shape[1:], a.dtype) for a in sends]
    return _start_copies(list(sends) + lands, (N_DEV - 1) * len(sends), _direct_plan(len(sends)),
                         "grads_direct_start_" + tag)


def _reduce_direct_finish(own, started, after, tag):
    n_p = len(own)
    bufs = _wait_copies(started, after, (N_DEV - 1) * n_p, _direct_plan(n_p), "grads_direct_wait_" + tag)
    return [_sum_direct(a, b, "grads_sum_direct") for a, b in zip(own, bufs[n_p:])]


def _gather_plans(pieces):
    n_p = len(pieces)
    dims = [(x.shape[axis], axis) for x, axis in pieces]

    def first(refs):
        px, py, pc = _place()
        targets = [(px, py, 1 - pc)] + [(cx, cy, pc) for cx, cy in _other_chips()]
        return [(refs[p], _window(refs[n_p + p], 4 * px + 2 * py + pc, *dims[p]), to)
                for p in range(n_p) for to in targets]

    def second(refs):
        px, py, pc = _place()
        out = []
        for p in range(n_p):
            for cx, cy, cc in [(cx, cy, pc) for cx, cy in _other_chips()] + [(px, py, 1 - pc)]:
                win = _window(refs[p], 4 * cx + 2 * cy + cc, *dims[p])
                out.append((win, win, (px, py, 1 - pc)))
        return out

    return first, second


FORWARD_COPIES = 4


def _flat_rows(a, width):
    return a.reshape(-1, width)


def _full_from_blocks(blocks, name):
    if name in COL_SHARDED:
        _, l, k, nb = blocks.shape
        return jnp.transpose(blocks, (1, 2, 0, 3)).reshape(l, k, N_DEV * nb)
    _, l, rb, n = blocks.shape
    return jnp.transpose(blocks, (1, 0, 2, 3)).reshape(l, N_DEV * rb, n)


def _blocks_from_full(full, name):
    if name in COL_SHARDED:
        l, k, n = full.shape
        return jnp.transpose(full.reshape(l, k, N_DEV, n // N_DEV), (2, 0, 1, 3))
    l, rows, n = full.shape
    return jnp.transpose(full.reshape(l, N_DEV, rows // N_DEV, n), (1, 0, 2, 3))


def _pad_heads(w, width):
    k = w.shape[0]
    return jnp.pad(w.reshape(k, N_HEADS, width), ((0, 0), (0, 0), (0, HEAD_PAD - width))).reshape(k, N_HEADS * HEAD_PAD)


def _unpad_heads(w, width):
    k = w.shape[0]
    return w.reshape(k, N_HEADS, HEAD_PAD)[:, :, :width].reshape(k, N_HEADS * width)


def _layer_operands(full, vec, conv_w_full, l):
    w_in = full['w_in'][l]
    kpe = jnp.pad(w_in[:, LAT:LAT + QK_ROPE], ((0, 0), (QK_NOPE, HEAD_PAD - QK_DIM)))
    w_ukv = full['w_ukv'][l].reshape(KV_LORA, N_HEADS, QK_NOPE + V_DIM)
    w_out = full['w_out'][l]
    d_model = w_out.shape[1]
    wo_attn = jnp.pad(w_out[:N_HEADS * V_DIM].reshape(N_HEADS, V_DIM, d_model),
                      ((0, 0), (0, HEAD_PAD - V_DIM), (0, 0))).reshape(N_HEADS * HEAD_PAD, d_model)
    ops = {
        'w_in': jnp.concatenate([w_in[:, :LAT], kpe, w_in[:, LAT + QK_ROPE:]], axis=1),
        'w_q': _pad_heads(full['w_uq'][l], QK_DIM),
        'w_k': _pad_heads(w_ukv[:, :, :QK_NOPE].reshape(KV_LORA, N_HEADS * QK_NOPE), QK_NOPE),
        'w_v': _pad_heads(w_ukv[:, :, QK_NOPE:].reshape(KV_LORA, N_HEADS * V_DIM), V_DIM),
        'wo_attn': wo_attn,
        'wo_conv': w_out[N_HEADS * V_DIM:],
        'conv_w': jnp.pad(conv_w_full[l], ((0, CONV_HALO - CONV_K), (0, 0))),
        'gq': jnp.pad(vec['q_norm'][l], (0, HEAD_PAD - QK_DIM)),
        'gk': jnp.pad(vec['k_norm'][l], (0, HEAD_PAD - QK_DIM)),
    }
    for n in ('ffn1_norm', 'mix_norm', 'q_latent_norm', 'kv_latent_norm', 'conv_b', 'conv_ln_g', 'conv_ln_b',
              'ffn2_norm', 'post_norm'):
        ops[n] = vec[n][l]
    return ops


def _ffn_fwd(x, g, wgu, wd, fp, h=None, next_gain=None):
    if h is None:
        h = _rms_fwd(x, g, BF16, "rms_fwd_ffn")
    ab, z = _ffn_up(h, wgu, fp, "ffn_up")
    out = _mm(z, wd, res=x, scale=0.5, norm_gain=next_gain, name="ffn_down")
    y, h_next = out if next_gain is not None else (out, None)
    return y, (x, h, ab, z), h_next


def _ffn_bwd(dy, dyb, saved, g, wgu, wd, fp, after_dw=None, after=None, before_dw=None):
    x, h, ab, z = saved
    dab, dx, dxb, dg = _ffn_bwd_rows(dyb, wd, ab, wgu, x, g, dy, fp, "ffn_bwd_rows", after=after)
    first = before_dw(dg) if before_dw is not None else None
    d_wgu = _mm(h, dab, ta=True, blocks=('col', N_DEV), tm=h.shape[1], after=first, name="ffn_dwgu")
    d_wd = _mm(z, dyb, ta=True, scale=0.5, blocks=('row', N_DEV), tm=2 * (z.shape[1] // N_DEV), name="ffn_dwd")
    token = after_dw(d_wgu, d_wd) if after_dw is not None else None
    return dx, dxb, dg, token


def _mixer_fwd(x, h, ops, tabs, after_attention=None):
    p = _mm(h, ops['w_in'], name="mix_in")
    qln, kvln = _lat_norm_fwd(p, ops['q_latent_norm'], ops['kv_latent_norm'], "lat_norm_fwd")
    q_raw = _mm(qln, ops['w_q'], name="mix_q")
    k_raw = _mm(kvln, ops['w_k'], name="mix_k")
    v = _mm(kvln, ops['w_v'], out_dtype=BF16, name="mix_v")
    q, k = _qk_prep_fwd(q_raw, k_raw, p, ops['gq'], ops['gk'], tabs, "qk_prep_fwd")
    o, lse = _attn_fwd(q, k, v, "attn_fwd")
    token = after_attention(o) if after_attention is not None else None
    conv_b = ops['conv_b'] if token is None else _after(ops['conv_b'], token)
    y_conv, cv = _conv_fwd(p, ops['conv_w'], conv_b, ops['conv_ln_g'], ops['conv_ln_b'], "conv_fwd")
    x_attn = _mm(o, ops['wo_attn'], res=x, name="mix_out_attn")
    x_out, h_next = _mm(cv, ops['wo_conv'], res=x_attn, norm_gain=ops['ffn2_norm'], name="mix_out_conv")
    return x_out, (x, h, p, qln, kvln, q_raw, k_raw, v, q, k, o, lse, y_conv, cv), h_next


def _mixer_bwd(dx_out, dxb_out, saved, ops, tabs, token=None):
    x, h, p, qln, kvln, q_raw, k_raw, v, q, k, o, lse, y_conv, cv = saved
    g = {}
    do = _mm(dxb_out, ops['wo_attn'], tb=True, after=token, name="mix_do")
    dcv = _mm(dxb_out, ops['wo_conv'], tb=True, name="mix_dcv")
    g['wo_attn'] = _mm(o, dxb_out, ta=True, name="mix_dwo_attn")
    g['wo_conv'] = _mm(cv, dxb_out, ta=True, name="mix_dwo_conv")
    dq, dk, dv = _attn_bwd(q, k, v, o, lse, do, "attn_bwd")
    dq_raw, dk_raw, dkpe, g['gq'], g['gk'] = _qk_prep_bwd(q_raw, k_raw, p, dq, dk, ops['gq'], ops['gk'], tabs,
                                                          "qk_prep_bwd")
    g['w_q'] = _mm(qln, dq_raw, ta=True, name="mix_dwq")
    g['w_k'] = _mm(kvln, dk_raw, ta=True, name="mix_dwk")
    g['w_v'] = _mm(kvln, dv, ta=True, name="mix_dwv")
    dqln = _mm(dq_raw, ops['w_q'], tb=True, name="mix_dqln")
    dkvln = _mm(dk_raw, ops['w_k'], tb=True, name="mix_dkvln_k")
    dkvln = _mm(dv, ops['w_v'], tb=True, res=dkvln, name="mix_dkvln_v")
    dp_lat, g['q_latent_norm'], g['kv_latent_norm'] = _lat_norm_bwd(
        p, ops['q_latent_norm'], ops['kv_latent_norm'], dqln, dkvln, "lat_norm_bwd")
    dy_conv, g['conv_ln_g'], g['conv_ln_b'], g['conv_b'] = _conv_bwd_ln(
        y_conv, dcv, ops['conv_ln_g'], ops['conv_ln_b'], "conv_bwd_ln")
    dag, g['conv_w'] = _conv_bwd_taps(p, dy_conv, ops['conv_w'], "conv_bwd_taps")
    dp = jnp.concatenate([dp_lat, dkpe, dag], axis=1)
    g['w_in'] = _mm(h, dp, ta=True, name="mix_dw_in")
    dx, dxb, g['mix_norm'] = _matmul_rms_bwd(dp, ops['w_in'], x, ops['mix_norm'], dx_out, "mix_dh_rms")
    return dx, dxb, g


def _mixer_grads_to_params(g):
    d_w_in = g['w_in']
    d_wk = _unpad_heads(g['w_k'], QK_NOPE).reshape(KV_LORA, N_HEADS, QK_NOPE)
    d_wv = _unpad_heads(g['w_v'], V_DIM).reshape(KV_LORA, N_HEADS, V_DIM)
    d_model = g['wo_attn'].shape[1]
    d_wo_attn = g['wo_attn'].reshape(N_HEADS, HEAD_PAD, d_model)[:, :V_DIM].reshape(N_HEADS * V_DIM, d_model)
    return {
        'mix_norm': g['mix_norm'],
        'w_in': jnp.concatenate([d_w_in[:, :LAT], d_w_in[:, LAT + QK_NOPE:LAT + QK_DIM], d_w_in[:, P_A:]], axis=1),
        'q_latent_norm': g['q_latent_norm'], 'w_uq': _unpad_heads(g['w_q'], QK_DIM),
        'kv_latent_norm': g['kv_latent_norm'],
        'w_ukv': jnp.concatenate([d_wk, d_wv], axis=2).reshape(KV_LORA, N_HEADS * (QK_NOPE + V_DIM)),
        'q_norm': g['gq'][:QK_DIM], 'k_norm': g['gk'][:QK_DIM], 'conv_w': g['conv_w'][:CONV_K],
        'conv_b': g['conv_b'], 'conv_ln_g': g['conv_ln_g'], 'conv_ln_b': g['conv_ln_b'],
        'w_out': jnp.concatenate([d_wo_attn, g['wo_conv']], axis=0),
    }


def kernel(x, ffn1_norm, ffn1_w_gate, ffn1_w_up, ffn1_w_down, mix_norm, w_in, q_latent_norm, w_uq, kv_latent_norm, w_ukv, q_norm, k_norm, conv_w, conv_b, conv_ln_g, conv_ln_b, w_out, ffn2_norm, ffn2_w_gate, ffn2_w_up, ffn2_w_down, post_norm, loss_target, m_ffn1_norm, m_ffn1_w_gate, m_ffn1_w_up, m_ffn1_w_down, m_mix_norm, m_w_in, m_q_latent_norm, m_w_uq, m_kv_latent_norm, m_w_ukv, m_q_norm, m_k_norm, m_conv_w, m_conv_b, m_conv_ln_g, m_conv_ln_b, m_w_out, m_ffn2_norm, m_ffn2_w_gate, m_ffn2_w_up, m_ffn2_w_down, m_post_norm, v_ffn1_norm, v_ffn1_w_gate, v_ffn1_w_up, v_ffn1_w_down, v_mix_norm, v_w_in, v_q_latent_norm, v_w_uq, v_kv_latent_norm, v_w_ukv, v_q_norm, v_k_norm, v_conv_w, v_conv_b, v_conv_ln_g, v_conv_ln_b, v_w_out, v_ffn2_norm, v_ffn2_w_gate, v_ffn2_w_up, v_ffn2_w_down, v_post_norm):
    args = locals()
    w = {n: args[n] for n in WEIGHTS}
    mom = {n: args["m_" + n] for n in WEIGHTS}
    var = {n: args["v_" + n] for n in WEIGHTS}
    depth = ffn1_norm.shape[0]
    x0 = x.reshape(x.shape[-2:])
    target = loss_target.reshape(loss_target.shape[-2:])
    t, d_model = x0.shape
    my_block = 4 * lax.axis_index("x") + 2 * lax.axis_index("y") + lax.axis_index("c")

    fb = ffn1_w_gate.shape[-1]
    fp = -(-fb // LANE) * LANE
    ffns = [(l, f) for l in range(depth) for f in (1, 2)]
    pad_cols = lambda a: jnp.pad(a, ((0, 0), (0, fp - fb)))
    gu_local = {(l, f): jnp.concatenate([pad_cols(w[f'ffn{f}_w_gate'][l]), pad_cols(w[f'ffn{f}_w_up'][l])],
                                        axis=1).astype(BF16) for l, f in ffns}
    dn_local = {(l, f): jnp.pad(w[f'ffn{f}_w_down'][l], ((0, fp - fb), (0, 0))).astype(BF16) for l, f in ffns}
    rows_of = {n: w[n].size // d_model for n in REST}
    rest_local = jnp.concatenate([_flat_rows(w[n].astype(BF16), d_model) for n in REST], axis=0)
    n_rest = rest_local.shape[0]
    first_ffn, later = ffns[0], ffns[1:]
    cw = conv_w.reshape(-1)
    cw_rows = -(-cw.size // (8 * LANE)) * 8
    cw_flat = jnp.pad(cw, (0, cw_rows * LANE - cw.size)).reshape(cw_rows, LANE)
    later_pieces = [(gu_local[q], 1) for q in later] + [(dn_local[q], 0) for q in later]
    n_later = len(later_pieces)
    got = _all_gather([(gu_local[first_ffn], 1), (dn_local[first_ffn], 0)], "gather_first")
    wgu, wd = {first_ffn: got[0]}, {first_ffn: got[1]}
    mixer_pieces = [(rest_local, 0), (cw_flat, 0)]
    mixer_plan, mixer_forward_plan = _gather_plans(mixer_pieces)
    gather_plan, forward_plan = _gather_plans(later_pieces)

    def landing(a, axis):
        return lax.empty(a.shape[:axis] + (N_DEV * a.shape[axis],) + a.shape[axis + 1:], a.dtype)

    gather_mixer = _start_copies([a for a, _ in mixer_pieces] + [landing(a, ax) for a, ax in mixer_pieces],
                                 4 * len(mixer_pieces), mixer_plan, "gather_mixer_start", after=got[0])
    gather_later = _start_copies([a for a, _ in later_pieces] + [landing(a, ax) for a, ax in later_pieces],
                                 4 * n_later, gather_plan, "gather_later_start", after=gather_mixer[3])

    x1_first, s1_first, h_first = _ffn_fwd(x0, _after(w['ffn1_norm'][0], gather_later[3]), wgu[first_ffn],
                                           wd[first_ffn], fp, next_gain=w['mix_norm'][0])
    lands = _wait_copies(gather_mixer, x1_first, 4 * len(mixer_pieces), mixer_plan, "gather_mixer_wait")[2:]
    pass_on = _start_copies(lands, FORWARD_COPIES * len(mixer_pieces), mixer_forward_plan, "gather_mixer_forward_start")
    lands = _wait_copies(pass_on, pass_on[3], FORWARD_COPIES * len(mixer_pieces), mixer_forward_plan, "gather_mixer_forward_wait")
    gathered = lands[0].reshape(N_DEV, n_rest, d_model)
    cw_all = lands[1].reshape(N_DEV, cw_rows * LANE)[:, :cw.size]
    full, start = {}, 0
    for n in REST:
        blocks = gathered[:, start:start + rows_of[n]].reshape((N_DEV,) + w[n].shape)
        full[n] = _full_from_blocks(blocks, n)
        start += rows_of[n]
    conv_w_full =jnp.transpose(cw_all.reshape((N_DEV,) + conv_w.shape), (1, 2, 0, 3)).reshape(depth, CONV_K, CONV_W)
    vec = {n: w[n] for n in VECTORS}
    ops = [_layer_operands(full, vec, conv_w_full, l) for l in range(depth)]
    tabs = _rope_tables(t)

    saved, xl = [], x0
    forward_later = []

    def pass_on_later(o_attn):
        lands = _wait_copies(gather_later, o_attn, 4 * n_later, gather_plan, "gather_later_wait")[n_later:]
        forward_later.append(_start_copies(lands, FORWARD_COPIES * n_later, forward_plan, "gather_later_forward_start"))
        return forward_later[0][3]

    for l in range(depth):
        o = ops[l]
        if l == 0:
            x1, s1, h_mix = x1_first, s1_first, h_first
            x2, sm, h2 = _mixer_fwd(x1, h_mix, o, tabs, after_attention=pass_on_later)
            lands = _wait_copies(forward_later[0], x2, FORWARD_COPIES * n_later, forward_plan, "gather_later_forward_wait")
            wgu.update(zip(later, lands[:len(later)]))
            wd.update(zip(later, lands[len(later):]))
        else:
            x1, s1, h_mix = _ffn_fwd(xl, o['ffn1_norm'], wgu[l, 1], wd[l, 1], fp, next_gain=o['mix_norm'])
            x2, sm, h2 = _mixer_fwd(x1, h_mix, o, tabs)
        x3, s2, _ = _ffn_fwd(x2, o['ffn2_norm'], wgu[l, 2], wd[l, 2], fp, h=h2)
        if l + 1 < depth:
            xl = _rms_fwd(x3, o['post_norm'], F32, "rms_fwd_post")
        saved.append((s1, sm, s2, x3))
    loss_part, dx, dxb, d_post_last = _post_norm_loss(x3, ops[depth - 1]['post_norm'], target, "post_norm_loss")
    loss = lax.psum(loss_part[0, 0], ("x", "y", "c"))

    grads, mine_gu, mine_dn, in_flight = [None] * depth, {}, {}, {}

    def exchange(tag):
        def after_dw(d_wgu, d_wd):
            own = [d_wgu[0], d_wd[0]]
            if tag == last_tag:
                from_sibling, started = _reduce_start(own, [d_wgu[1], d_wd[1]], tag)
            else:
                from_sibling, started = None, _reduce_direct_start([d_wgu[1], d_wd[1]], tag)
            in_flight[tag] = (own, from_sibling, started)
            return started[3]
        return after_dw

    def finish(tag, after):
        own, from_sibling, started = in_flight.pop(tag)
        if from_sibling is None:
            return _reduce_direct_finish(own, started, after, tag)
        return _reduce_finish(own, from_sibling, started, after, tag)

    last_tag = f"{first_ffn[0]}{first_ffn[1]}"
    vector_sums = []

    def reduce_vectors(norms0):
        layers = [dict(grads[k]) for k in range(depth)]
        layers[0].update(norms0)
        parts = [jnp.stack([layers[k][n] for k in range(depth)]).reshape(-1) for n in VECTORS + ['conv_w']]
        small = jnp.concatenate(parts)
        s_rows = -(-small.size // (8 * LANE)) * 8
        small = jnp.pad(small, (0, s_rows * LANE - small.size)).reshape(s_rows, LANE)
        small_all = _all_gather([(small, 0)], "gather_small_grads", in_vmem=True)[0]
        vector_sums.append(_sum_parts([small_all[k * s_rows:(k + 1) * s_rows] for k in range(N_DEV)],
                                      "sum_small_grads"))
        return vector_sums[0]

    token = None
    for l in reversed(range(depth)):
        o = ops[l]
        s1, sm, s2, x3 = saved[l]
        if l + 1 < depth:
            dx, dxb, d_post = _rms_bwd(x3, _after(o['post_norm'], token), dx, None, "rms_bwd_post")
        else:
            d_post = d_post_last
        dx, dxb, d_ffn2, token = _ffn_bwd(dx, dxb, s2, o['ffn2_norm'], wgu[l, 2], wd[l, 2], fp, exchange(f"{l}2"))
        if l + 1 < depth:
            mine_gu[l + 1, 1], mine_dn[l + 1, 1] = finish(f"{l + 1}1", dx)
        dx, dxb, gm = _mixer_bwd(dx, dxb, sm, o, tabs, token)
        mine_gu[l, 2], mine_dn[l, 2] = finish(f"{l}2", dx)
        grads[l] = _mixer_grads_to_params(gm)
        if l == 0:
            rest_own = jnp.concatenate(
                [_blocks_from_full(jnp.stack([grads[k][n] for k in range(depth)]), n).reshape(N_DEV, rows_of[n], d_model)
                 for n in REST], axis=1)
            sibling_rest, started_rest = _reduce_start([rest_own], [rest_own.astype(BF16)], "rest")
        norms = dict(post_norm=d_post, ffn2_norm=d_ffn2)
        dx, dxb, d_ffn1, token = _ffn_bwd(
            dx, dxb, s1, o['ffn1_norm'], wgu[l, 1], wd[l, 1], fp, exchange(f"{l}1"),
            after=started_rest[3] if l == 0 else None,
            before_dw=(lambda dg: reduce_vectors(dict(norms, ffn1_norm=dg))) if l == 0 else None)
        grads[l].update(norms, ffn1_norm=d_ffn1)
    grad_x = dx.reshape(x.shape)

    grad, delta, new_m, new_v = {}, {}, {}, {}

    def adamw(names, after=None):
        for n in names:
            shp = w[n].shape
            two_d = lambda a: a.reshape(-1, shp[-1])
            dl, mn, vn = _adamw(two_d(w[n]), two_d(grad[n]), two_d(mom[n]), two_d(var[n]), "adamw_" + n, after)
            delta[n], new_m[n], new_v[n] = dl.reshape(shp), mn.reshape(shp), vn.reshape(shp)
            after = dl[:8]
        return after

    def ffn_grads(f):
        grad[f'ffn{f}_w_gate'] = jnp.stack([mine_gu[l, f][:, :fb] for l in range(depth)])
        grad[f'ffn{f}_w_up'] = jnp.stack([mine_gu[l, f][:, fp:fp + fb] for l in range(depth)])
        grad[f'ffn{f}_w_down'] = jnp.stack([mine_dn[l, f][:fb] for l in range(depth)])
        return [f'ffn{f}_w_gate', f'ffn{f}_w_up', f'ffn{f}_w_down']

    small_sum = vector_sums[0].reshape(-1)
    start = 0
    for n in VECTORS:
        grad[n] = small_sum[start:start + w[n].size].reshape(w[n].shape)
        start += w[n].size
    cw_grad = small_sum[start:start + depth * CONV_K * CONV_W].reshape(depth, CONV_K, CONV_W)
    nb = conv_w.shape[-1]
    grad['conv_w'] = lax.dynamic_slice_in_dim(cw_grad, my_block * nb, nb, axis=2)
    done = adamw(ffn_grads(2) + ['conv_w'], after=token)
    vcat = lambda src: jnp.concatenate([src[n].reshape(-1) for n in VECTORS]).reshape(-1, LANE)
    dl, mn, vn = _adamw(vcat(w), vcat(grad), vcat(mom), vcat(var), "adamw_vectors", done)
    start = 0
    for n in VECTORS:
        sl = lambda a: a.reshape(-1)[start:start + w[n].size].reshape(w[n].shape)
        delta[n], new_m[n], new_v[n] = sl(dl), sl(mn), sl(vn)
        start += w[n].size
    mine_rest = _reduce_finish([rest_own], sibling_rest, started_rest, dl, "rest")[0]
    start = 0
    for n in REST:
        grad[n] = mine_rest[start:start + rows_of[n]].reshape(w[n].shape)
        start += rows_of[n]
    done = adamw(REST)
    mine_gu[first_ffn], mine_dn[first_ffn] = finish(f"{first_ffn[0]}{first_ffn[1]}", done)
    adamw(ffn_grads(1))

    return (loss, grad_x, *[grad[n] for n in WEIGHTS], *[delta[n] for n in WEIGHTS],
            *[new_m[n] for n in WEIGHTS], *[new_v[n] for n in WEIGHTS])
```
